```python
import jax
import jax.numpy as jnp
from jax import lax
import numpy as np

D_MODEL = 2048
BATCH = 8
SEQ = 8192
DEPTH = 4

CHUNK = 64
Q_BLOCK = 128
MIX_WIDTH = D_MODEL
HG_WIDTH = MIX_WIDTH // 2
SB_WIDTH = MIX_WIDTH - HG_WIDTH
HG_HEAD_DIM = 128
HG_HEADS = HG_WIDTH // HG_HEAD_DIM
SB_HEAD_DIM = 128
SB_HEADS = SB_WIDTH // SB_HEAD_DIM
D_FF = -(-(8 * D_MODEL) // (3 * 256)) * 256
IN_COLS = 4 * HG_WIDTH + 3 * SB_WIDTH
IN_SPLITS = (HG_WIDTH, 2 * HG_WIDTH, 3 * HG_WIDTH, 4 * HG_WIDTH,
             4 * HG_WIDTH + SB_WIDTH, 4 * HG_WIDTH + 2 * SB_WIDTH)
N_MOD = 6
EPS = 1e-6
TINY = 1e-30

kernel_name = "hymba_hgrn2_stickbreaking_adaln_trunk"


def rms_norm(x, g):
    xf = x.astype(jnp.float32)
    y = xf * lax.rsqrt(jnp.mean(xf * xf, axis=-1, keepdims=True) + EPS)
    return (y * g.astype(jnp.float32)).astype(x.dtype)


def hgrn2_mixer(q, f_logit, v, g, lb, out_g):
    bsz, seq, _ = q.shape
    n_chunks = seq // CHUNK
    f32 = jnp.float32
    lb = lb.astype(f32)
    fl = f_logit.astype(f32)
    q_act = jax.nn.silu(q.astype(f32))
    forget = lb + (1.0 - lb) * jax.nn.sigmoid(fl)
    log_f = jnp.log(jnp.maximum(forget, TINY))
    key = (1.0 - lb) * jax.nn.sigmoid(-fl)

    def to_chunks(t):
        return t.reshape(bsz, n_chunks, CHUNK, HG_HEADS, HG_HEAD_DIM).transpose(1, 0, 3, 2, 4)

    causal = jnp.tril(jnp.ones((CHUNK, CHUNK), dtype=bool))[:, :, None]

    def chunk_step(state, inp):
        q_c, k_c, v_c, lf_c = inp
        b = jnp.cumsum(lf_c, axis=2)
        diff = b[:, :, :, None, :] - b[:, :, None, :, :]
        decay = jnp.where(causal, jnp.exp(jnp.where(causal, diff, 0.0)), 0.0)
        scores = jnp.einsum('bhtd,bhsd,bhtsd->bhts', q_c, k_c, decay)
        o_intra = jnp.einsum('bhts,bhsv->bhtv', scores, v_c)
        o_inter = jnp.einsum('bhtd,bhdv->bhtv', q_c * jnp.exp(b), state)
        b_last = b[:, :, -1, :]
        k_to_end = k_c * jnp.exp(b_last[:, :, None, :] - b)
        state = jnp.exp(b_last)[..., None] * state + jnp.einsum('bhsd,bhsv->bhdv', k_to_end, v_c)
        return state, o_intra + o_inter

    state0 = jnp.zeros((bsz, HG_HEADS, HG_HEAD_DIM, HG_HEAD_DIM), f32)
    _, o = lax.scan(chunk_step, state0,
                    (to_chunks(q_act), to_chunks(key), to_chunks(v.astype(f32)), to_chunks(log_f)))
    o = o.transpose(1, 0, 3, 2, 4).reshape(bsz, seq, HG_HEADS, HG_HEAD_DIM)
    o = rms_norm(o, out_g.reshape(HG_HEADS, HG_HEAD_DIM))
    return o.reshape(bsz, seq, HG_WIDTH) * jax.nn.silu(g.astype(f32))


def stick_breaking_mixer(q, k, v, q_g, k_g, out_g):
    bsz, seq, _ = q.shape
    f32 = jnp.float32

    def heads(t):
        return t.astype(f32).reshape(bsz, seq, SB_HEADS, SB_HEAD_DIM).transpose(0, 2, 1, 3)

    qh = rms_norm(heads(q), q_g)
    kh = rms_norm(heads(k), k_g)
    vh = heads(v)
    key_pos = jnp.arange(seq)
    scale = SB_HEAD_DIM ** -0.5

    def query_block(blk):
        q0 = blk * Q_BLOCK
        qb = lax.dynamic_slice_in_dim(qh, q0, Q_BLOCK, axis=2)
        z = jnp.einsum('bhqd,bhkd->bhqk', qb, kh) * scale
        q_pos = q0 + jnp.arange(Q_BLOCK)
        earlier = key_pos[None, :] < q_pos[:, None]
        log_keep = jnp.where(earlier, jax.nn.log_sigmoid(-z), 0.0)
        log_keep_between = lax.cumsum(log_keep, axis=3, reverse=True) - log_keep
        a = jnp.where(earlier, jnp.exp(jax.nn.log_sigmoid(z) + log_keep_between), 0.0)
        return jnp.einsum('bhqk,bhkv->bhqv', a, vh)

    o = lax.map(query_block, jnp.arange(seq // Q_BLOCK))
    o = o.transpose(1, 0, 3, 2, 4).reshape(bsz, seq, SB_HEADS, SB_HEAD_DIM)
    o = rms_norm(o, out_g.reshape(SB_HEADS, SB_HEAD_DIM))
    return o.reshape(bsz, seq, SB_WIDTH)


def _fwd_setup_inputs(seed: int = 0) -> dict:
    key = jax.random.key(seed)
    ks = jax.random.split(key, 16)
    f32 = jnp.float32

    def normal(k, shape, scale):
        return jax.random.normal(k, shape, f32) * scale

    def gain(k, shape):
        return 1.0 + 0.02 * jax.random.normal(k, shape, f32)

    return {
        'x': normal(ks[0], (BATCH, SEQ, D_MODEL), 1.0),
        'c': normal(ks[1], (BATCH, D_MODEL), 1.0),
        'norm1_g': gain(ks[2], (DEPTH, D_MODEL)),
        'w_in': normal(ks[3], (DEPTH, D_MODEL, IN_COLS), D_MODEL ** -0.5),
        'hg_lb_logits': normal(ks[4], (DEPTH, HG_WIDTH), 0.5),
        'hg_out_g': gain(ks[5], (DEPTH, HG_WIDTH)),
        'sb_q_g': gain(ks[6], (DEPTH, SB_HEAD_DIM)),
        'sb_k_g': gain(ks[7], (DEPTH, SB_HEAD_DIM)),
        'sb_out_g': gain(ks[8], (DEPTH, SB_WIDTH)),
        'w_out': normal(ks[9], (DEPTH, MIX_WIDTH, D_MODEL), MIX_WIDTH ** -0.5),
        'norm2_g': gain(ks[10], (DEPTH, D_MODEL)),
        'w_ffn_in': normal(ks[11], (DEPTH, D_MODEL, 2 * D_FF), D_MODEL ** -0.5),
        'w_ffn_out': normal(ks[12], (DEPTH, D_FF, D_MODEL), D_FF ** -0.5),
        'w_ada': normal(ks[13], (DEPTH, D_MODEL, N_MOD * D_MODEL), 0.5 * D_MODEL ** -0.5),
        'b_ada': normal(ks[14], (DEPTH, N_MOD * D_MODEL), 0.01),
    }


def _fwd_reference(x, c, norm1_g, w_in, hg_lb_logits, hg_out_g, sb_q_g, sb_k_g, sb_out_g,
              w_out, norm2_g, w_ffn_in, w_ffn_out, w_ada, b_ada):
    lb_soft = jax.nn.softmax(hg_lb_logits.astype(jnp.float32), axis=0)
    lower_bounds = jnp.cumsum(lb_soft, axis=0) - lb_soft[0]
    cond = jax.nn.silu(c)
    for layer in range(DEPTH):
        mod = cond @ w_ada[layer] + b_ada[layer]
        sh1, sc1, g1, sh2, sc2, g2 = [m[:, None, :] for m in jnp.split(mod, N_MOD, axis=-1)]

        h = rms_norm(x, norm1_g[layer]) * (1.0 + sc1) + sh1
        proj = h @ w_in[layer]
        hg_q, hg_f, hg_i, hg_g, sb_q, sb_k, sb_v = jnp.split(proj, IN_SPLITS, axis=-1)
        o_hg = hgrn2_mixer(hg_q, hg_f, hg_i, hg_g, lower_bounds[layer], hg_out_g[layer])
        o_sb = stick_breaking_mixer(sb_q, sb_k, sb_v, sb_q_g[layer], sb_k_g[layer], sb_out_g[layer])
        mixed = jnp.concatenate([o_hg, o_sb], axis=-1).astype(x.dtype) @ w_out[layer]
        x = x + g1 * mixed

        h = rms_norm(x, norm2_g[layer]) * (1.0 + sc2) + sh2
        gate, up = jnp.split(h @ w_ffn_in[layer], 2, axis=-1)
        x = x + g2 * ((jax.nn.silu(gate) * up) @ w_ffn_out[layer])
    return x


import jax as _jax
import jax.numpy as _jnp

TWIN_FORMAT = 'train_step'
FWD_PARAMS = ['x', 'c', 'norm1_g', 'w_in', 'hg_lb_logits', 'hg_out_g', 'sb_q_g', 'sb_k_g', 'sb_out_g', 'w_out', 'norm2_g', 'w_ffn_in', 'w_ffn_out', 'w_ada', 'b_ada']
TWIN_WEIGHTS = ['norm1_g', 'w_in', 'hg_lb_logits', 'hg_out_g', 'sb_q_g', 'sb_k_g', 'sb_out_g', 'w_out', 'norm2_g', 'w_ffn_in', 'w_ffn_out', 'w_ada', 'b_ada']
TWIN_DIFF_INPUT = 'x'
TWIN_INPUTS = ['x', 'c', 'norm1_g', 'w_in', 'hg_lb_logits', 'hg_out_g', 'sb_q_g', 'sb_k_g', 'sb_out_g', 'w_out', 'norm2_g', 'w_ffn_in', 'w_ffn_out', 'w_ada', 'b_ada', 'loss_target', 'm_norm1_g', 'm_w_in', 'm_hg_lb_logits', 'm_hg_out_g', 'm_sb_q_g', 'm_sb_k_g', 'm_sb_out_g', 'm_w_out', 'm_norm2_g', 'm_w_ffn_in', 'm_w_ffn_out', 'm_w_ada', 'm_b_ada', 'v_norm1_g', 'v_w_in', 'v_hg_lb_logits', 'v_hg_out_g', 'v_sb_q_g', 'v_sb_k_g', 'v_sb_out_g', 'v_w_out', 'v_norm2_g', 'v_w_ffn_in', 'v_w_ffn_out', 'v_w_ada', 'v_b_ada']
TWIN_OUTPUTS = ['loss', 'grad_x', 'grad_norm1_g', 'grad_w_in', 'grad_hg_lb_logits', 'grad_hg_out_g', 'grad_sb_q_g', 'grad_sb_k_g', 'grad_sb_out_g', 'grad_w_out', 'grad_norm2_g', 'grad_w_ffn_in', 'grad_w_ffn_out', 'grad_w_ada', 'grad_b_ada', 'delta_norm1_g', 'delta_w_in', 'delta_hg_lb_logits', 'delta_hg_out_g', 'delta_sb_q_g', 'delta_sb_k_g', 'delta_sb_out_g', 'delta_w_out', 'delta_norm2_g', 'delta_w_ffn_in', 'delta_w_ffn_out', 'delta_w_ada', 'delta_b_ada', 'new_m_norm1_g', 'new_m_w_in', 'new_m_hg_lb_logits', 'new_m_hg_out_g', 'new_m_sb_q_g', 'new_m_sb_k_g', 'new_m_sb_out_g', 'new_m_w_out', 'new_m_norm2_g', 'new_m_w_ffn_in', 'new_m_w_ffn_out', 'new_m_w_ada', 'new_m_b_ada', 'new_v_norm1_g', 'new_v_w_in', 'new_v_hg_lb_logits', 'new_v_hg_out_g', 'new_v_sb_q_g', 'new_v_sb_k_g', 'new_v_sb_out_g', 'new_v_w_out', 'new_v_norm2_g', 'new_v_w_ffn_in', 'new_v_w_ffn_out', 'new_v_w_ada', 'new_v_b_ada']
TWIN_LEAF_KINDS = {'loss': 'loss', 'grad_x': 'grad_x', 'grad_norm1_g': 'grad_w', 'grad_w_in': 'grad_w', 'grad_hg_lb_logits': 'grad_w', 'grad_hg_out_g': 'grad_w', 'grad_sb_q_g': 'grad_w', 'grad_sb_k_g': 'grad_w', 'grad_sb_out_g': 'grad_w', 'grad_w_out': 'grad_w', 'grad_norm2_g': 'grad_w', 'grad_w_ffn_in': 'grad_w', 'grad_w_ffn_out': 'grad_w', 'grad_w_ada': 'grad_w', 'grad_b_ada': 'grad_w', 'delta_norm1_g': 'delta_w', 'delta_w_in': 'delta_w', 'delta_hg_lb_logits': 'delta_w', 'delta_hg_out_g': 'delta_w', 'delta_sb_q_g': 'delta_w', 'delta_sb_k_g': 'delta_w', 'delta_sb_out_g': 'delta_w', 'delta_w_out': 'delta_w', 'delta_norm2_g': 'delta_w', 'delta_w_ffn_in': 'delta_w', 'delta_w_ffn_out': 'delta_w', 'delta_w_ada': 'delta_w', 'delta_b_ada': 'delta_w', 'new_m_norm1_g': 'new_m', 'new_m_w_in': 'new_m', 'new_m_hg_lb_logits': 'new_m', 'new_m_hg_out_g': 'new_m', 'new_m_sb_q_g': 'new_m', 'new_m_sb_k_g': 'new_m', 'new_m_sb_out_g': 'new_m', 'new_m_w_out': 'new_m', 'new_m_norm2_g': 'new_m', 'new_m_w_ffn_in': 'new_m', 'new_m_w_ffn_out': 'new_m', 'new_m_w_ada': 'new_m', 'new_m_b_ada': 'new_m', 'new_v_norm1_g': 'new_v', 'new_v_w_in': 'new_v', 'new_v_hg_lb_logits': 'new_v', 'new_v_hg_out_g': 'new_v', 'new_v_sb_q_g': 'new_v', 'new_v_sb_k_g': 'new_v', 'new_v_sb_out_g': 'new_v', 'new_v_w_out': 'new_v', 'new_v_norm2_g': 'new_v', 'new_v_w_ffn_in': 'new_v', 'new_v_w_ffn_out': 'new_v', 'new_v_w_ada': 'new_v', 'new_v_b_ada': 'new_v'}


def _forward(args):
    return _fwd_reference(*[args[k] for k in FWD_PARAMS])


def _output_shape():
    def fwd():
        inp = _fwd_setup_inputs(0)
        return _fwd_reference(*[inp[k] for k in FWD_PARAMS])
    out = _jax.eval_shape(fwd)
    return out.shape, out.dtype

N_MICROBATCH = 1
ADAM_LR = 0.001
ADAM_B1 = 0.9
ADAM_B2 = 0.999
ADAM_EPS = 1e-08
ADAM_WD = 0.01
ADAM_STEP = 10
PER_EXAMPLE_BATCH_AXIS = {'x': 0, 'c': 0, 'loss_target': 0}
SHARED_INPUTS = []
_WEIGHT_DTYPES = {'norm1_g': _jnp.float32, 'w_in': _jnp.float32, 'hg_lb_logits': _jnp.float32, 'hg_out_g': _jnp.float32, 'sb_q_g': _jnp.float32, 'sb_k_g': _jnp.float32, 'sb_out_g': _jnp.float32, 'w_out': _jnp.float32, 'norm2_g': _jnp.float32, 'w_ffn_in': _jnp.float32, 'w_ffn_out': _jnp.float32, 'w_ada': _jnp.float32, 'b_ada': _jnp.float32}
MOMENT_SCALE = {'norm1_g': 6.381463e-01, 'w_in': 1.438155e-01, 'hg_lb_logits': 3.562204e-03, 'hg_out_g': 1.286792e+00, 'sb_q_g': 8.110198e-02, 'sb_k_g': 8.122472e-02, 'sb_out_g': 2.911362e+00, 'w_out': 2.559166e-01, 'norm2_g': 3.019757e+00, 'w_ffn_in': 4.916842e-02, 'w_ffn_out': 7.082631e-02, 'w_ada': 8.579559e-01, 'b_ada': 2.038770e+00}


def _to_microbatches(a, axis):
    t = _jnp.moveaxis(a, axis, 0)
    t = t.reshape((N_MICROBATCH, t.shape[0] // N_MICROBATCH) + t.shape[1:])
    return _jnp.moveaxis(t, 1, axis + 1)


def setup_inputs(seed: int = 0) -> dict:
    inp = _fwd_setup_inputs(seed)
    key = _jax.random.fold_in(_jax.random.key(seed), 7919)
    shape, _ = _output_shape()
    out = dict(inp)
    out["loss_target"] = _jax.random.normal(_jax.random.fold_in(key, 0), shape, _jnp.float32)
    for i, name in enumerate(TWIN_WEIGHTS):
        w = inp[name].astype(_jnp.float32)
        if MOMENT_SCALE is None:
            s = _jnp.sqrt(_jnp.mean(_jnp.square(w)) + 1e-30)
        else:
            s = MOMENT_SCALE[name]
        km, kv = _jax.random.split(_jax.random.fold_in(key, i + 1))
        out[name] = w
        out["m_" + name] = s * _jax.random.normal(km, w.shape, _jnp.float32)
        out["v_" + name] = (s * s) * _jax.random.uniform(kv, w.shape, _jnp.float32, 0.5, 1.5)
    if N_MICROBATCH > 1:
        for name, axis in PER_EXAMPLE_BATCH_AXIS.items():
            out[name] = _to_microbatches(out[name], axis)
    return {'x': out['x'], 'c': out['c'], 'norm1_g': out['norm1_g'], 'w_in': out['w_in'], 'hg_lb_logits': out['hg_lb_logits'], 'hg_out_g': out['hg_out_g'], 'sb_q_g': out['sb_q_g'], 'sb_k_g': out['sb_k_g'], 'sb_out_g': out['sb_out_g'], 'w_out': out['w_out'], 'norm2_g': out['norm2_g'], 'w_ffn_in': out['w_ffn_in'], 'w_ffn_out': out['w_ffn_out'], 'w_ada': out['w_ada'], 'b_ada': out['b_ada'], 'loss_target': out['loss_target'], 'm_norm1_g': out['m_norm1_g'], 'm_w_in': out['m_w_in'], 'm_hg_lb_logits': out['m_hg_lb_logits'], 'm_hg_out_g': out['m_hg_out_g'], 'm_sb_q_g': out['m_sb_q_g'], 'm_sb_k_g': out['m_sb_k_g'], 'm_sb_out_g': out['m_sb_out_g'], 'm_w_out': out['m_w_out'], 'm_norm2_g': out['m_norm2_g'], 'm_w_ffn_in': out['m_w_ffn_in'], 'm_w_ffn_out': out['m_w_ffn_out'], 'm_w_ada': out['m_w_ada'], 'm_b_ada': out['m_b_ada'], 'v_norm1_g': out['v_norm1_g'], 'v_w_in': out['v_w_in'], 'v_hg_lb_logits': out['v_hg_lb_logits'], 'v_hg_out_g': out['v_hg_out_g'], 'v_sb_q_g': out['v_sb_q_g'], 'v_sb_k_g': out['v_sb_k_g'], 'v_sb_out_g': out['v_sb_out_g'], 'v_w_out': out['v_w_out'], 'v_norm2_g': out['v_norm2_g'], 'v_w_ffn_in': out['v_w_ffn_in'], 'v_w_ffn_out': out['v_w_ffn_out'], 'v_w_ada': out['v_w_ada'], 'v_b_ada': out['v_b_ada']}


def _loss(weights, diff, rest, loss_target):
    with _jax.named_scope("forward"):
        args = {**rest, TWIN_DIFF_INPUT: diff, **{k: w.astype(_WEIGHT_DTYPES[k]) for k, w in weights.items()}}
        y = _forward(args)
    with _jax.named_scope("loss_head"):
        err = _jnp.square(y.astype(_jnp.float32) - loss_target)
        return 0.5 * _jnp.sum(_jnp.mean(err, axis=-1)) if err.ndim else 0.5 * err


def _adamw(w, g, m, v):
    m = ADAM_B1 * m + (1.0 - ADAM_B1) * g
    v = ADAM_B2 * v + (1.0 - ADAM_B2) * _jnp.square(g)
    m_hat = m / (1.0 - ADAM_B1 ** ADAM_STEP)
    v_hat = v / (1.0 - ADAM_B2 ** ADAM_STEP)
    delta = -ADAM_LR * (m_hat / (_jnp.sqrt(v_hat) + ADAM_EPS) + ADAM_WD * w)
    return delta, m, v


def reference(x, c, norm1_g, w_in, hg_lb_logits, hg_out_g, sb_q_g, sb_k_g, sb_out_g, w_out, norm2_g, w_ffn_in, w_ffn_out, w_ada, b_ada, loss_target, m_norm1_g, m_w_in, m_hg_lb_logits, m_hg_out_g, m_sb_q_g, m_sb_k_g, m_sb_out_g, m_w_out, m_norm2_g, m_w_ffn_in, m_w_ffn_out, m_w_ada, m_b_ada, v_norm1_g, v_w_in, v_hg_lb_logits, v_hg_out_g, v_sb_q_g, v_sb_k_g, v_sb_out_g, v_w_out, v_norm2_g, v_w_ffn_in, v_w_ffn_out, v_w_ada, v_b_ada):
    given = dict(x=x, c=c, norm1_g=norm1_g, w_in=w_in, hg_lb_logits=hg_lb_logits, hg_out_g=hg_out_g, sb_q_g=sb_q_g, sb_k_g=sb_k_g, sb_out_g=sb_out_g, w_out=w_out, norm2_g=norm2_g, w_ffn_in=w_ffn_in, w_ffn_out=w_ffn_out, w_ada=w_ada, b_ada=b_ada, loss_target=loss_target, m_norm1_g=m_norm1_g, m_w_in=m_w_in, m_hg_lb_logits=m_hg_lb_logits, m_hg_out_g=m_hg_out_g, m_sb_q_g=m_sb_q_g, m_sb_k_g=m_sb_k_g, m_sb_out_g=m_sb_out_g, m_w_out=m_w_out, m_norm2_g=m_norm2_g, m_w_ffn_in=m_w_ffn_in, m_w_ffn_out=m_w_ffn_out, m_w_ada=m_w_ada, m_b_ada=m_b_ada, v_norm1_g=v_norm1_g, v_w_in=v_w_in, v_hg_lb_logits=v_hg_lb_logits, v_hg_out_g=v_hg_out_g, v_sb_q_g=v_sb_q_g, v_sb_k_g=v_sb_k_g, v_sb_out_g=v_sb_out_g, v_w_out=v_w_out, v_norm2_g=v_norm2_g, v_w_ffn_in=v_w_ffn_in, v_w_ffn_out=v_w_ffn_out, v_w_ada=v_w_ada, v_b_ada=v_b_ada)
    weights = {n: given[n] for n in TWIN_WEIGHTS}
    shared = {n: given[n] for n in SHARED_INPUTS}
    per_example = {n: given[n] for n in ['x', 'c']}
    grad_fn = _jax.value_and_grad(_loss, argnums=(0, 1))

    def one_microbatch(ex, loss_target):
        ex = dict(ex)
        diff = ex.pop(TWIN_DIFF_INPUT)
        return grad_fn(weights, diff, {**shared, **ex}, loss_target)

    if N_MICROBATCH == 1:
        loss, (grad_w, grad_x) = one_microbatch(per_example, given["loss_target"])
    else:
        def body(carry, xs):
            loss_sum, grad_sum = carry
            l_k, (gw_k, gx_k) = one_microbatch(xs[0], xs[1])
            with _jax.named_scope("update"):
                return (loss_sum + l_k, _jax.tree.map(_jnp.add, grad_sum, gw_k)), gx_k

        init = (_jnp.zeros((), _jnp.float32), _jax.tree.map(_jnp.zeros_like, weights))
        (loss, grad_w), grad_x = _jax.lax.scan(body, init, (per_example, given["loss_target"]))
    with _jax.named_scope("update"):
        delta_w, new_m, new_v = {}, {}, {}
        for n in TWIN_WEIGHTS:
            delta_w[n], new_m[n], new_v[n] = _adamw(weights[n], grad_w[n], given["m_" + n], given["v_" + n])
    return (loss, grad_x, *[grad_w[n] for n in TWIN_WEIGHTS], *[delta_w[n] for n in TWIN_WEIGHTS],
            *[new_m[n] for n in TWIN_WEIGHTS], *[new_v[n] for n in TWIN_WEIGHTS])
```

```python
import functools

import jax
import jax.numpy as jnp
from jax import lax
from jax.experimental import pallas as pl
from jax.experimental.pallas import tpu as pltpu

F32 = jnp.float32
BF16 = jnp.bfloat16
MESH = pl.DeviceIdType.MESH

N_DEV = 8
HEAD = 128
CHUNK = 64
N_MOD = 6
EPS = 1e-6
TINY = 1e-30
ADAM_LR = 0.001
ADAM_B1 = 0.9
ADAM_B2 = 0.999
ADAM_EPS = 1e-08
ADAM_WD = 0.01
ADAM_STEP = 10
V7X_VMEM_LIMIT = 56 * 1024 * 1024
SKIP_LOG = -104.0


def _params(sem):
    return pltpu.CompilerParams(dimension_semantics=sem, vmem_limit_bytes=V7X_VMEM_LIMIT)


def _bdot(a, b, dims=(((1,), (0,)), ((), ()))):
    return lax.dot_general(a.astype(BF16), b.astype(BF16), dims, preferred_element_type=F32)


_NT = (((1,), (1,)), ((), ()))
_TN = (((0,), (0,)), ((), ()))


def _sigmoid_pair(x):
    e = jnp.exp(-jnp.abs(x))
    r = 1.0 / (1.0 + e)
    er = e * r
    pos = x >= 0
    return jnp.where(pos, r, er), jnp.where(pos, er, r)


def _split_dot(x, u, parts):
    acc = None
    rem = x
    for _ in range(parts):
        p = rem.astype(BF16)
        rem = rem - p.astype(F32)
        t = lax.dot_general(p, u, (((1,), (0,)), ((), ())), preferred_element_type=F32)
        acc = t if acc is None else acc + t
    return acc


def _split_dot_left(u, x, parts):
    acc = None
    rem = x
    for _ in range(parts):
        p = rem.astype(BF16)
        rem = rem - p.astype(F32)
        t = lax.dot_general(u, p, (((1,), (0,)), ((), ())), preferred_element_type=F32)
        acc = t if acc is None else acc + t
    return acc


def _mm_nn(a, b, *, tm, nb=None, out_dtype=F32, resid=None, gate=None, name):
    M, K = a.shape
    if b.ndim == 3:
        NB, _, n = b.shape
        b_spec = pl.BlockSpec((None, K, n), lambda j, i: (j, 0, 0))
    else:
        n = nb
        NB = b.shape[1] // nb
        b_spec = pl.BlockSpec((K, n), lambda j, i: (0, j))
    N = NB * n
    epi = resid is not None

    def body(*refs):
        if epi:
            a_ref, b_ref, r_ref, g_ref, o_ref, acc_ref = refs
        else:
            a_ref, b_ref, o_ref = refs
        acc = jnp.dot(a_ref[...], b_ref[...], preferred_element_type=F32)
        if epi:
            o_ref[...] = r_ref[...] + g_ref[...] * acc
            acc_ref[...] = acc.astype(BF16)
        else:
            o_ref[...] = acc.astype(out_dtype)

    in_specs = [pl.BlockSpec((tm, K), lambda j, i: (i, 0)), b_spec]
    args = [a, b]
    o_spec = pl.BlockSpec((tm, n), lambda j, i: (i, j))
    if epi:
        in_specs += [pl.BlockSpec((tm, n), lambda j, i: (i, j)), pl.BlockSpec((1, n), lambda j, i: (0, j))]
        args += [resid, gate]
        out_shape = (jax.ShapeDtypeStruct((M, N), F32), jax.ShapeDtypeStruct((M, N), BF16))
        out_specs = (o_spec, o_spec)
    else:
        out_shape = jax.ShapeDtypeStruct((M, N), out_dtype)
        out_specs = o_spec
    return pl.pallas_call(body, grid=(NB, M // tm), in_specs=in_specs, out_specs=out_specs,
                          out_shape=out_shape, compiler_params=_params(("parallel", "parallel")),
                          name=name)(*args)


def _mm_nt(dy, w, *, tm, kb, nb=None, out_dtype=F32, name):
    M, N = dy.shape
    if w.ndim == 3:
        NB, Kt, n = w.shape
        w_spec = pl.BlockSpec((None, kb, n), lambda i, k, j: (j, k, 0))
    else:
        Kt = w.shape[0]
        n = nb
        NB = N // nb
        w_spec = pl.BlockSpec((kb, n), lambda i, k, j: (k, j))
    KB = Kt // kb

    def body(dy_ref, w_ref, o_ref, acc_ref):
        j = pl.program_id(2)
        part = lax.dot_general(dy_ref[...], w_ref[...], _NT, preferred_element_type=F32)

        @pl.when(j == 0)
        def _():
            acc_ref[...] = part

        @pl.when(j > 0)
        def _():
            acc_ref[...] += part

        @pl.when(j == NB - 1)
        def _():
            o_ref[...] = acc_ref[...].astype(out_dtype)

    return pl.pallas_call(
        body, grid=(M // tm, KB, NB),
        in_specs=[pl.BlockSpec((tm, n), lambda i, k, j: (i, j)), w_spec],
        out_specs=pl.BlockSpec((tm, kb), lambda i, k, j: (i, k)),
        out_shape=jax.ShapeDtypeStruct((M, Kt), out_dtype),
        scratch_shapes=[pltpu.VMEM((tm, kb), F32)],
        compiler_params=_params(("parallel", "parallel", "arbitrary")), name=name)(dy, w)


def _mm_tn(x, dy, *, tm, kb, nb, blocked, name):
    M, K = x.shape
    N = dy.shape[1]
    KB, NB, MB = K // kb, N // nb, M // tm

    def body(x_ref, dy_ref, o_ref, acc_ref):
        m = pl.program_id(2)
        part = lax.dot_general(x_ref[...], dy_ref[...], _TN, preferred_element_type=F32)

        @pl.when(m == 0)
        def _():
            acc_ref[...] = part

        @pl.when(m > 0)
        def _():
            acc_ref[...] += part

        @pl.when(m == MB - 1)
        def _():
            o_ref[...] = acc_ref[...].astype(BF16)

    if blocked:
        out_shape = jax.ShapeDtypeStruct((NB, K, nb), BF16)
        o_spec = pl.BlockSpec((None, kb, nb), lambda k, n, m: (n, k, 0))
    else:
        out_shape = jax.ShapeDtypeStruct((K, N), BF16)
        o_spec = pl.BlockSpec((kb, nb), lambda k, n, m: (k, n))
    return pl.pallas_call(
        body, grid=(KB, NB, MB),
        in_specs=[pl.BlockSpec((tm, kb), lambda k, n, m: (m, k)),
                  pl.BlockSpec((tm, nb), lambda k, n, m: (m, n))],
        out_specs=o_spec, out_shape=out_shape,
        scratch_shapes=[pltpu.VMEM((kb, nb), F32)],
        compiler_params=_params(("parallel", "parallel", "arbitrary")), name=name)(x, dy)


def _modnorm_fwd(x, gain, sc, sh, *, tm, name):
    S, D = x.shape

    def body(x_ref, g_ref, sc_ref, sh_ref, h_ref):
        xv = x_ref[...]
        rstd = lax.rsqrt(jnp.mean(xv * xv, axis=-1, keepdims=True) + EPS)
        y = (xv * rstd) * g_ref[...]
        h_ref[...] = (y * (1.0 + sc_ref[...]) + sh_ref[...]).astype(BF16)

    row = pl.BlockSpec((1, D), lambda i: (0, 0))
    return pl.pallas_call(
        body, grid=(S // tm,),
        in_specs=[pl.BlockSpec((tm, D), lambda i: (i, 0)), row, row, row],
        out_specs=pl.BlockSpec((tm, D), lambda i: (i, 0)),
        out_shape=jax.ShapeDtypeStruct((S, D), BF16),
        compiler_params=_params(("parallel",)), name=name)(x, gain, sc, sh)


def _modnorm_bwd(x, dh, dres, gain, sc, branch, gate, *, tm, name):
    S, D = x.shape
    has_prev = branch is not None

    def body(*refs):
        if has_prev:
            (x_ref, dh_ref, dr_ref, g_ref, sc_ref, br_ref, gt_ref,
             dx_ref, dbr_ref, dgt_ref, dsh_ref, dsc_ref, dgn_ref) = refs
        else:
            (x_ref, dh_ref, dr_ref, g_ref, sc_ref,
             dx_ref, dsh_ref, dsc_ref, dgn_ref) = refs
        i = pl.program_id(0)
        xv = x_ref[...]
        dh_v = dh_ref[...]
        gv = g_ref[...]
        scale1 = 1.0 + sc_ref[...]
        rstd = lax.rsqrt(jnp.mean(xv * xv, axis=-1, keepdims=True) + EPS)
        n = xv * rstd
        dn = dh_v * (gv * scale1)
        dx = rstd * (dn - n * jnp.mean(dn * n, axis=-1, keepdims=True)) + dr_ref[...]
        dx_ref[...] = dx
        dhn = dh_v * n
        p_sh = jnp.sum(dh_v, axis=0, keepdims=True)
        p_sc = jnp.sum(dhn, axis=0, keepdims=True) * gv
        p_gn = jnp.sum(dhn, axis=0, keepdims=True) * scale1
        if has_prev:
            dbr_ref[...] = (gt_ref[...] * dx).astype(BF16)
            p_gt = jnp.sum(dx * br_ref[...].astype(F32), axis=0, keepdims=True)

        @pl.when(i == 0)
        def _():
            dsh_ref[...] = p_sh
            dsc_ref[...] = p_sc
            dgn_ref[...] = p_gn
            if has_prev:
                dgt_ref[...] = p_gt

        @pl.when(i > 0)
        def _():
            dsh_ref[...] += p_sh
            dsc_ref[...] += p_sc
            dgn_ref[...] += p_gn
            if has_prev:
                dgt_ref[...] += p_gt

    tile = pl.BlockSpec((tm, D), lambda i: (i, 0))
    row = pl.BlockSpec((1, D), lambda i: (0, 0))
    row_shape = jax.ShapeDtypeStruct((1, D), F32)
    if has_prev:
        in_specs = [tile, tile, tile, row, row, tile, row]
        args = (x, dh, dres, gain, sc, branch, gate)
        out_specs = (tile, tile, row, row, row, row)
        out_shape = (jax.ShapeDtypeStruct((S, D), F32), jax.ShapeDtypeStruct((S, D), BF16),
                     row_shape, row_shape, row_shape, row_shape)
    else:
        in_specs = [tile, tile, tile, row, row]
        args = (x, dh, dres, gain, sc)
        out_specs = (tile, row, row, row)
        out_shape = (jax.ShapeDtypeStruct((S, D), F32), row_shape, row_shape, row_shape)
    return pl.pallas_call(body, grid=(S // tm,), in_specs=in_specs, out_specs=out_specs,
                          out_shape=out_shape, compiler_params=_params(("arbitrary",)),
                          name=name)(*args)


def _swiglu_fwd(u, *, tm, name):
    S, F2 = u.shape
    F = F2 // 2

    def body(u_ref, a_ref):
        gate = u_ref[:, :F]
        up = u_ref[:, F:]
        s, _ = _sigmoid_pair(gate)
        a_ref[...] = (gate * s * up).astype(BF16)

    return pl.pallas_call(
        body, grid=(S // tm,),
        in_specs=[pl.BlockSpec((tm, F2), lambda i: (i, 0))],
        out_specs=pl.BlockSpec((tm, F), lambda i: (i, 0)),
        out_shape=jax.ShapeDtypeStruct((S, F), BF16),
        compiler_params=_params(("parallel",)), name=name)(u)


def _swiglu_bwd(u, da, *, tm, name):
    S, F2 = u.shape
    F = F2 // 2

    def body(u_ref, da_ref, du_ref):
        gate = u_ref[:, :F]
        up = u_ref[:, F:]
        dav = da_ref[...]
        s, ns = _sigmoid_pair(gate)
        du_ref[:, :F] = (dav * up * (s * (1.0 + gate * ns))).astype(BF16)
        du_ref[:, F:] = (dav * (gate * s)).astype(BF16)

    return pl.pallas_call(
        body, grid=(S // tm,),
        in_specs=[pl.BlockSpec((tm, F2), lambda i: (i, 0)), pl.BlockSpec((tm, F), lambda i: (i, 0))],
        out_specs=pl.BlockSpec((tm, F2), lambda i: (i, 0)),
        out_shape=jax.ShapeDtypeStruct((S, F2), BF16),
        compiler_params=_params(("parallel",)), name=name)(u, da)


def _loss_bwd(y, target, branch, gate, *, tm, name):
    S, D = y.shape
    nsteps = S // tm

    def body(y_ref, t_ref, br_ref, gt_ref, dy_ref, dbr_ref, dgt_ref, loss_ref, col_ref):
        i = pl.program_id(0)
        diff = y_ref[...] - t_ref[...]
        dy = diff * (1.0 / D)
        dy_ref[...] = dy
        dbr_ref[...] = (gt_ref[...] * dy).astype(BF16)
        p_gt = jnp.sum(dy * br_ref[...].astype(F32), axis=0, keepdims=True)
        p_col = jnp.sum(diff * diff, axis=0, keepdims=True)

        @pl.when(i == 0)
        def _():
            dgt_ref[...] = p_gt
            col_ref[...] = p_col

        @pl.when(i > 0)
        def _():
            dgt_ref[...] += p_gt
            col_ref[...] += p_col

        @pl.when(i == nsteps - 1)
        def _():
            tot = jnp.sum(col_ref[...], axis=-1, keepdims=True) * (0.5 / D)
            loss_ref[...] = jnp.broadcast_to(tot, (1, 128))

    tile = pl.BlockSpec((tm, D), lambda i: (i, 0))
    row = pl.BlockSpec((1, D), lambda i: (0, 0))
    return pl.pallas_call(
        body, grid=(nsteps,), in_specs=[tile, tile, tile, row],
        out_specs=(tile, tile, row, pl.BlockSpec((1, 128), lambda i: (0, 0))),
        out_shape=(jax.ShapeDtypeStruct((S, D), F32), jax.ShapeDtypeStruct((S, D), BF16),
                   jax.ShapeDtypeStruct((1, D), F32), jax.ShapeDtypeStruct((1, 128), F32)),
        scratch_shapes=[pltpu.VMEM((1, D), F32)],
        compiler_params=_params(("arbitrary",)), name=name)(y, target, branch, gate)


def _hg_chunk(q, fl, lbv, tri):
    C = q.shape[0]
    sq, nsq = _sigmoid_pair(q)
    qa = q * sq
    sig, nsig = _sigmoid_pair(fl)
    one_lb = 1.0 - lbv
    f = lbv + one_lb * sig
    fc = jnp.maximum(f, TINY)
    lf = jnp.log(fc)
    k = one_lb * nsig
    b = _split_dot_left(tri, lf, 3)
    row = lax.broadcasted_iota(jnp.int32, b.shape, 0)
    bm = jnp.sum(jnp.where(row == C // 2 - 1, b, 0.0), axis=0, keepdims=True)
    bl = jnp.sum(jnp.where(row == C - 1, b, 0.0), axis=0, keepdims=True)
    eb = jnp.exp(b)
    ebm = jnp.exp(b - bm)
    enbm = jnp.exp(bm - b)
    ebl = jnp.exp(bl - b)
    ebL = jnp.exp(bl)

    def operand(t):
        return t.astype(BF16).astype(F32)

    return dict(sq=sq, nsq=nsq, qa=qa, sig=sig, nsig=nsig, one_lb=one_lb, f=f, fc=fc, k=k,
                eb=eb, ebm=ebm, enbm=enbm, ebl=ebl, ebL=ebL,
                Qm=operand(qa * ebm), Km=operand(k * enbm), Qb=operand(qa * eb), Kh=operand(k * ebl), row=row)


def _causal_incl(C):
    r = lax.broadcasted_iota(jnp.int32, (C, C), 0)
    c = lax.broadcasted_iota(jnp.int32, (C, C), 1)
    return r >= c


def _hg_fwd(proj, lb, out_g, *, n_heads, cg, name):
    S = proj.shape[0]
    H = n_heads
    W = H * HEAD
    T = cg * CHUNK
    NG = S // T
    tri = jnp.tril(jnp.ones((CHUNK, CHUNK), F32)).astype(BF16)

    def body(q_ref, f_ref, v_ref, g_ref, lb_ref, og_ref, tri_ref, o_ref, on_ref, st_ref, s_scr):
        @pl.when(pl.program_id(1) == 0)
        def _():
            s_scr[...] = jnp.zeros_like(s_scr)

        lbv = lb_ref[...]
        ogv = og_ref[...]
        triv = tri_ref[...]
        mask = _causal_incl(CHUNK)
        for c in range(cg):
            rows = pl.ds(c * CHUNK, CHUNK)
            v = v_ref[rows, :]
            gg = g_ref[rows, :]
            cm = _hg_chunk(q_ref[rows, :], f_ref[rows, :], lbv, triv)
            s0 = s_scr[...]
            st_ref[c] = s0
            A = jnp.where(mask, _bdot(cm["Qm"], cm["Km"], _NT), 0.0)
            o = _bdot(A, v) + _bdot(cm["Qb"], s0, _NT)
            s_scr[...] = s0 * cm["ebL"] + _bdot(v, cm["Kh"], _TN)
            o_ref[rows, :] = o
            rstd = lax.rsqrt(jnp.mean(o * o, axis=-1, keepdims=True) + EPS)
            sg, _ = _sigmoid_pair(gg)
            on_ref[rows, :] = (((o * rstd) * ogv) * (gg * sg)).astype(BF16)

    def col(group):
        return pl.BlockSpec((T, HEAD), lambda h, g: (g, group * H + h))

    vec = pl.BlockSpec((1, HEAD), lambda h, g: (0, h))
    return pl.pallas_call(
        body, grid=(H, NG),
        in_specs=[col(0), col(1), col(2), col(3), vec, vec,
                  pl.BlockSpec((CHUNK, CHUNK), lambda h, g: (0, 0))],
        out_specs=(pl.BlockSpec((T, HEAD), lambda h, g: (g, h)),
                   pl.BlockSpec((T, HEAD), lambda h, g: (g, h)),
                   pl.BlockSpec((cg, None, HEAD, HEAD), lambda h, g: (g, h, 0, 0))),
        out_shape=(jax.ShapeDtypeStruct((S, W), F32), jax.ShapeDtypeStruct((S, W), BF16),
                   jax.ShapeDtypeStruct((S // CHUNK, H, HEAD, HEAD), F32)),
        scratch_shapes=[pltpu.VMEM((HEAD, HEAD), F32)],
        compiler_params=_params(("parallel", "arbitrary")), name=name,
    )(proj, proj, proj, proj, lb, out_g, tri)


def _hg_bwd(proj, o_pre, d_on, d_on_col0, states, lb, out_g, *, n_heads, cg, name):
    S = proj.shape[0]
    H = n_heads
    W = H * HEAD
    T = cg * CHUNK
    NG = S // T
    tri = jnp.tril(jnp.ones((CHUNK, CHUNK), F32)).astype(BF16)
    triu = jnp.triu(jnp.ones((CHUNK, CHUNK), F32)).astype(BF16)

    def body(q_ref, f_ref, v_ref, g_ref, o_ref, dy_ref, st_ref, lb_ref, og_ref, tri_ref, triu_ref,
             dq_ref, df_ref, di_ref, dg_ref, dlb_ref, dog_ref, ds_scr):
        gstep = pl.program_id(1)

        @pl.when(gstep == 0)
        def _():
            ds_scr[...] = jnp.zeros_like(ds_scr)
            dlb_ref[...] = jnp.zeros_like(dlb_ref)
            dog_ref[...] = jnp.zeros_like(dog_ref)

        lbv = lb_ref[...]
        ogv = og_ref[...]
        triv = tri_ref[...]
        triuv = triu_ref[...]
        mask = _causal_incl(CHUNK)
        dlb_acc = jnp.zeros((1, HEAD), F32)
        dog_acc = jnp.zeros((1, HEAD), F32)
        for c in reversed(range(cg)):
            rows = pl.ds(c * CHUNK, CHUNK)
            q = q_ref[rows, :]
            v = v_ref[rows, :]
            gg = g_ref[rows, :]
            o = o_ref[rows, :]
            dy = dy_ref[rows, :]
            cm = _hg_chunk(q, f_ref[rows, :], lbv, triv)
            s0 = st_ref[c]
            ds1 = ds_scr[...]
            rstd = lax.rsqrt(jnp.mean(o * o, axis=-1, keepdims=True) + EPS)
            n = o * rstd
            sg, nsg = _sigmoid_pair(gg)
            silu_g = gg * sg
            dyn = dy * n
            dog_acc = dog_acc + jnp.sum(dyn * silu_g, axis=0, keepdims=True)
            dg_ref[rows, :] = (dyn * ogv * (sg * (1.0 + gg * nsg))).astype(BF16)
            dn = dy * (ogv * silu_g)
            d_o = rstd * (dn - n * jnp.mean(dn * n, axis=-1, keepdims=True))
            A = jnp.where(mask, _bdot(cm["Qm"], cm["Km"], _NT), 0.0)
            dA = jnp.where(mask, _bdot(d_o, v, _NT), 0.0)
            dV = _bdot(A, d_o, _TN) + _bdot(cm["Kh"], ds1, _NT)
            dQm = _bdot(dA, cm["Km"])
            dKm = _bdot(dA, cm["Qm"], _TN)
            dQb = _bdot(d_o, s0)
            dKh = _bdot(v, ds1)
            ds_scr[...] = ds1 * cm["ebL"] + _bdot(d_o, cm["Qb"], _TN)
            kh_term = dKh * cm["Kh"]
            db = dQm * cm["Qm"] - dKm * cm["Km"] + dQb * cm["Qb"] - kh_term
            dbl = (jnp.sum(kh_term, axis=0, keepdims=True)
                   + cm["ebL"] * jnp.sum(ds1 * s0, axis=0, keepdims=True))
            db = db + jnp.where(cm["row"] == CHUNK - 1, dbl, 0.0)
            dlf = _split_dot_left(triuv, db, 3)
            dqa = dQm * cm["ebm"] + dQb * cm["eb"]
            dq_ref[rows, :] = (dqa * (cm["sq"] * (1.0 + q * cm["nsq"]))).astype(BF16)
            dk = dKm * cm["enbm"] + dKh * cm["ebl"]
            dfc = jnp.where(cm["f"] > TINY, dlf / cm["fc"], 0.0)
            t = dfc - dk
            df_ref[rows, :] = (t * (cm["one_lb"] * cm["sig"] * cm["nsig"])).astype(BF16)
            dlb_acc = dlb_acc + jnp.sum(t * cm["nsig"], axis=0, keepdims=True)
            di_ref[rows, :] = dV.astype(BF16)
        dlb_ref[...] += dlb_acc
        dog_ref[...] += dog_acc

    def col(group):
        return pl.BlockSpec((T, HEAD), lambda h, g: (NG - 1 - g, group * H + h))

    own = pl.BlockSpec((T, HEAD), lambda h, g: (NG - 1 - g, h))
    vec = pl.BlockSpec((1, HEAD), lambda h, g: (0, h))
    cst = pl.BlockSpec((CHUNK, CHUNK), lambda h, g: (0, 0))
    act = jax.ShapeDtypeStruct((S, W), BF16)
    vec_shape = jax.ShapeDtypeStruct((1, W), F32)
    return pl.pallas_call(
        body, grid=(H, NG),
        in_specs=[col(0), col(1), col(2), col(3), own,
                  pl.BlockSpec((T, HEAD), lambda h, g: (NG - 1 - g, d_on_col0 + h)),
                  pl.BlockSpec((cg, None, HEAD, HEAD), lambda h, g: (NG - 1 - g, h, 0, 0)),
                  vec, vec, cst, cst],
        out_specs=(own, own, own, own, vec, vec),
        out_shape=(act, act, act, act, vec_shape, vec_shape),
        scratch_shapes=[pltpu.VMEM((HEAD, HEAD), F32)],
        compiler_params=_params(("parallel", "arbitrary")), name=name,
    )(proj, proj, proj, proj, o_pre, d_on, states, lb, out_g, tri, triu)


def _sb_pre(proj, q_g, k_g, *, n_heads, col0, tm, name):
    S = proj.shape[0]
    H = n_heads
    W = H * HEAD

    def body(q_ref, k_ref, v_ref, qg_ref, kg_ref, qh_ref, kh_ref, vh_ref):
        for src, g_ref, dst in ((q_ref, qg_ref, qh_ref), (k_ref, kg_ref, kh_ref)):
            xv = src[...]
            rstd = lax.rsqrt(jnp.mean(xv * xv, axis=-1, keepdims=True) + EPS)
            dst[...] = ((xv * rstd) * g_ref[...]).astype(BF16)
        vh_ref[...] = v_ref[...].astype(BF16)

    def col(group):
        return pl.BlockSpec((tm, HEAD), lambda i, h: (i, col0 + group * H + h))

    vec = pl.BlockSpec((1, HEAD), lambda i, h: (0, 0))
    own = pl.BlockSpec((tm, HEAD), lambda i, h: (i, h))
    act = jax.ShapeDtypeStruct((S, W), BF16)
    return pl.pallas_call(
        body, grid=(S // tm, H), in_specs=[col(0), col(1), col(2), vec, vec],
        out_specs=(own, own, own), out_shape=(act, act, act),
        compiler_params=_params(("parallel", "parallel")), name=name)(proj, proj, proj, q_g, k_g)


def _sb_scores(q, k_blk, scale):
    z = lax.dot_general(q, k_blk, _NT, preferred_element_type=F32) * scale
    e = jnp.exp(-jnp.abs(z))
    sp = jnp.maximum(z, 0.0) + jnp.log1p(e)
    return z, e, sp


def _strict_lower_mask(t):
    r = lax.broadcasted_iota(jnp.int32, (t, t), 0)
    c = lax.broadcasted_iota(jnp.int32, (t, t), 1)
    return c < r


def _sb_fwd(qh, kh, vh, out_g, *, n_heads, tq, name):
    S, W = qh.shape
    H = n_heads
    NQ = S // tq
    scale = HEAD ** -0.5
    u_strict = jnp.tril(jnp.ones((tq, tq), F32), -1).astype(BF16)

    def body(q_ref, k_ref, v_ref, og_ref, u_ref, o_ref, on_ref):
        qi = pl.program_id(1)
        q = q_ref[...]
        u = u_ref[...]

        def block(kb, r_carry, diag):
            rows = pl.ds(pl.multiple_of(kb * tq, tq), tq)
            k_blk = k_ref[rows, :]
            v_blk = v_ref[rows, :]
            z, _, sp = _sb_scores(q, k_blk, scale)
            if diag:
                m = _strict_lower_mask(tq)
                L = jnp.where(m, -sp, 0.0)
            else:
                L = -sp
            C = _split_dot(L, u, 2)
            a = jnp.exp(z - sp + C + r_carry)
            if diag:
                a = jnp.where(m, a, 0.0)
            pv = lax.dot_general(a.astype(BF16), v_blk, (((1,), (0,)), ((), ())),
                                 preferred_element_type=F32)
            return pv, r_carry + (C[:, 0:1] + L[:, 0:1])

        acc0, r0 = block(qi, jnp.zeros((tq, 1), F32), True)

        def cond(st):
            kb, _, _, rmax = st
            return jnp.logical_and(kb >= 0, rmax > SKIP_LOG)

        def step(st):
            kb, acc, r, _ = st
            pv, r2 = block(kb, r, False)
            return kb - 1, acc + pv, r2, jnp.max(r2)

        _, acc, _, _ = lax.while_loop(cond, step, (qi - 1, acc0, r0, jnp.max(r0)))
        o_ref[...] = acc
        rstd = lax.rsqrt(jnp.mean(acc * acc, axis=-1, keepdims=True) + EPS)
        on_ref[...] = ((acc * rstd) * og_ref[...]).astype(BF16)

    blk = pl.BlockSpec((tq, HEAD), lambda h, i: (i, h))
    full = pl.BlockSpec((S, HEAD), lambda h, i: (0, h))
    return pl.pallas_call(
        body, grid=(H, NQ),
        in_specs=[blk, full, full, pl.BlockSpec((1, HEAD), lambda h, i: (0, h)),
                  pl.BlockSpec((tq, tq), lambda h, i: (0, 0))],
        out_specs=(blk, blk),
        out_shape=(jax.ShapeDtypeStruct((S, W), F32), jax.ShapeDtypeStruct((S, W), BF16)),
        compiler_params=_params(("parallel", "arbitrary")), name=name)(qh, kh, vh, out_g, u_strict)


def _sb_bwd(qh, kh, vh, o_pre, d_on, d_on_col0, out_g, *, n_heads, tq, name):
    S, W = qh.shape
    H = n_heads
    NQ = S // tq
    scale = HEAD ** -0.5
    u_strict = jnp.tril(jnp.ones((tq, tq), F32), -1).astype(BF16)
    u_incl = jnp.tril(jnp.ones((tq, tq), F32)).astype(BF16)

    def body(q_ref, k_ref, v_ref, o_ref, dy_ref, og_ref, us_ref, ui_ref,
             dq_ref, dk_ref, dv_ref, dog_ref):
        qi = pl.program_id(1)

        @pl.when(qi == 0)
        def _():
            dk_ref[...] = jnp.zeros_like(dk_ref)
            dv_ref[...] = jnp.zeros_like(dv_ref)
            dog_ref[...] = jnp.zeros_like(dog_ref)

        q = q_ref[...]
        us = us_ref[...]
        ui = ui_ref[...]
        o = o_ref[...]
        dy = dy_ref[...]
        ogv = og_ref[...]
        rstd = lax.rsqrt(jnp.mean(o * o, axis=-1, keepdims=True) + EPS)
        n = o * rstd
        dog_ref[...] += jnp.sum(dy * n, axis=0, keepdims=True)
        dn = dy * ogv
        d_o = rstd * (dn - n * jnp.mean(dn * n, axis=-1, keepdims=True))
        d_ob = d_o.astype(BF16)
        delta = jnp.sum(d_ob.astype(F32) * o, axis=-1, keepdims=True)

        def block(kb, r_carry, g_carry, diag):
            rows = pl.ds(pl.multiple_of(kb * tq, tq), tq)
            k_blk = k_ref[rows, :]
            v_blk = v_ref[rows, :]
            z, e, sp = _sb_scores(q, k_blk, scale)
            if diag:
                m = _strict_lower_mask(tq)
                L = jnp.where(m, -sp, 0.0)
            else:
                L = -sp
            C = _split_dot(L, us, 2)
            a = jnp.exp(z - sp + C + r_carry)
            if diag:
                a = jnp.where(m, a, 0.0)
            ab = a.astype(BF16)
            dA = lax.dot_general(d_ob, v_blk, _NT, preferred_element_type=F32)
            G = ab.astype(F32) * dA
            SI = _split_dot(G, ui, 3)
            P = delta - (g_carry + SI)
            r = 1.0 / (1.0 + e)
            er = e * r
            pos = z >= 0
            dz = G * jnp.where(pos, er, r) - P * jnp.where(pos, r, er)
            if diag:
                dz = jnp.where(m, dz, 0.0)
            dzb = (dz * scale).astype(BF16)
            dq_part = lax.dot_general(dzb, k_blk, (((1,), (0,)), ((), ())), preferred_element_type=F32)
            dk_ref[rows, :] += lax.dot_general(dzb, q, _TN, preferred_element_type=F32)
            dv_ref[rows, :] += lax.dot_general(ab, d_ob, _TN, preferred_element_type=F32)
            return dq_part, r_carry + (C[:, 0:1] + L[:, 0:1]), g_carry + SI[:, 0:1]

        zero = jnp.zeros((tq, 1), F32)
        dq0, r0, g0 = block(qi, zero, zero, True)

        def cond(st):
            kb, _, _, _, rmax = st
            return jnp.logical_and(kb >= 0, rmax > SKIP_LOG)

        def step(st):
            kb, dq, r, g, _ = st
            dq_part, r2, g2 = block(kb, r, g, False)
            return kb - 1, dq + dq_part, r2, g2, jnp.max(r2)

        _, dq, _, _, _ = lax.while_loop(cond, step, (qi - 1, dq0, r0, g0, jnp.max(r0)))
        dq_ref[...] = dq

    blk = pl.BlockSpec((tq, HEAD), lambda h, i: (i, h))
    full = pl.BlockSpec((S, HEAD), lambda h, i: (0, h))
    vec = pl.BlockSpec((1, HEAD), lambda h, i: (0, h))
    cst = pl.BlockSpec((tq, tq), lambda h, i: (0, 0))
    act = jax.ShapeDtypeStruct((S, W), F32)
    return pl.pallas_call(
        body, grid=(H, NQ),
        in_specs=[blk, full, full, blk,
                  pl.BlockSpec((tq, HEAD), lambda h, i: (i, d_on_col0 + h)), vec, cst, cst],
        out_specs=(blk, full, full, vec),
        out_shape=(act, act, act, jax.ShapeDtypeStruct((1, W), F32)),
        compiler_params=_params(("parallel", "arbitrary")), name=name,
    )(qh, kh, vh, o_pre, d_on, out_g, u_strict, u_incl)


def _sb_pre_bwd(proj, dqh, dkh, dvh, q_g, k_g, *, n_heads, col0, tm, name):
    S = proj.shape[0]
    H = n_heads
    W = H * HEAD

    def body(q_ref, k_ref, dqh_ref, dkh_ref, dvh_ref, qg_ref, kg_ref,
             dq_ref, dk_ref, dv_ref, dqg_ref, dkg_ref):
        first = jnp.logical_and(pl.program_id(0) == 0, pl.program_id(1) == 0)

        @pl.when(first)
        def _():
            dqg_ref[...] = jnp.zeros_like(dqg_ref)
            dkg_ref[...] = jnp.zeros_like(dkg_ref)

        for src, dh_ref, g_ref, dst, dg_ref in ((q_ref, dqh_ref, qg_ref, dq_ref, dqg_ref),
                                                (k_ref, dkh_ref, kg_ref, dk_ref, dkg_ref)):
            xv = src[...]
            dh = dh_ref[...]
            rstd = lax.rsqrt(jnp.mean(xv * xv, axis=-1, keepdims=True) + EPS)
            n = xv * rstd
            dg_ref[...] += jnp.sum(dh * n, axis=0, keepdims=True)
            dn = dh * g_ref[...]
            dst[...] = (rstd * (dn - n * jnp.mean(dn * n, axis=-1, keepdims=True))).astype(BF16)
        dv_ref[...] = dvh_ref[...].astype(BF16)

    def col(group):
        return pl.BlockSpec((tm, HEAD), lambda i, h: (i, col0 + group * H + h))

    vec = pl.BlockSpec((1, HEAD), lambda i, h: (0, 0))
    own = pl.BlockSpec((tm, HEAD), lambda i, h: (i, h))
    act = jax.ShapeDtypeStruct((S, W), BF16)
    vec_shape = jax.ShapeDtypeStruct((1, HEAD), F32)
    return pl.pallas_call(
        body, grid=(S // tm, H), in_specs=[col(0), col(1), own, own, own, vec, vec],
        out_specs=(own, own, own, vec, vec), out_shape=(act, act, act, vec_shape, vec_shape),
        compiler_params=_params(("arbitrary", "arbitrary")), name=name,
    )(proj, proj, dqh, dkh, dvh, q_g, k_g)


def _softmax_rows(x_ref, L):
    rows = [x_ref[l:l + 1, :] for l in range(L)]
    mx = rows[0]
    for r in rows[1:]:
        mx = jnp.maximum(mx, r)
    ex = [jnp.exp(r - mx) for r in rows]
    tot = ex[0]
    for e in ex[1:]:
        tot = tot + e
    return [e / tot for e in ex]


def _lb_fwd(logits, *, name):
    L, W = logits.shape

    def body(x_ref, o_ref):
        s = _softmax_rows(x_ref, L)
        run = jnp.zeros((1, W), F32)
        for l in range(L):
            run = run + s[l]
            o_ref[l:l + 1, :] = run - s[0]

    return pl.pallas_call(body, out_shape=jax.ShapeDtypeStruct((L, W), F32), name=name)(logits)


def _lb_bwd(logits, dlb_parts, *, name):
    L, W = logits.shape
    P = dlb_parts.shape[0]

    def body(x_ref, d_ref, o_ref):
        s = _softmax_rows(x_ref, L)
        dlb = []
        for l in range(L):
            t = d_ref[0, l:l + 1, :]
            for q in range(1, P):
                t = t + d_ref[q, l:l + 1, :]
            dlb.append(t)
        ds = [None] * L
        run = jnp.zeros((1, W), F32)
        for j in reversed(range(L)):
            run = run + dlb[j]
            ds[j] = run
        ds[0] = jnp.zeros((1, W), F32)
        inner = jnp.zeros((1, W), F32)
        for j in range(L):
            inner = inner + s[j] * ds[j]
        for j in range(L):
            o_ref[j:j + 1, :] = s[j] * (ds[j] - inner)

    return pl.pallas_call(body, out_shape=jax.ShapeDtypeStruct((L, W), F32), name=name)(logits, dlb_parts)


def _ada_mod(c_all, w_ada, *, nb, name):
    L, D, n = w_ada.shape
    B = c_all.shape[0]

    def body(c_ref, w_ref, o_ref, cond_ref):
        cv = c_ref[...]
        s, _ = _sigmoid_pair(cv)
        cond = cv * s
        cond_ref[...] = cond
        o_ref[...] = _bdot(cond, w_ref[...])

    return pl.pallas_call(
        body, grid=(L, n // nb),
        in_specs=[pl.BlockSpec((B, D), lambda l, j: (0, 0)),
                  pl.BlockSpec((None, D, nb), lambda l, j: (l, 0, j))],
        out_specs=(pl.BlockSpec((None, B, nb), lambda l, j: (l, 0, j)),
                   pl.BlockSpec((B, D), lambda l, j: (0, 0))),
        out_shape=(jax.ShapeDtypeStruct((L, B, n), F32), jax.ShapeDtypeStruct((B, D), F32)),
        compiler_params=_params(("arbitrary", "arbitrary")), name=name)(c_all, w_ada)


def _adam_math(w, g, m, v):
    m2 = ADAM_B1 * m + (1.0 - ADAM_B1) * g
    v2 = ADAM_B2 * v + (1.0 - ADAM_B2) * (g * g)
    m_hat = m2 / (1.0 - ADAM_B1 ** ADAM_STEP)
    v_hat = v2 / (1.0 - ADAM_B2 ** ADAM_STEP)
    delta = -ADAM_LR * (m_hat / (jnp.sqrt(v_hat) + ADAM_EPS) + ADAM_WD * w)
    return delta, m2, v2


def _adamw(w, m, v, gparts, *, tr, name):
    R, C = w.shape
    P = gparts.shape[0]

    def body(w_ref, m_ref, v_ref, gp_ref, g_ref, d_ref, m2_ref, v2_ref):
        g = gp_ref[0].astype(F32)
        for p in range(1, P):
            g = g + gp_ref[p].astype(F32)
        delta, m2, v2 = _adam_math(w_ref[...], g, m_ref[...], v_ref[...])
        g_ref[...] = g
        d_ref[...] = delta
        m2_ref[...] = m2
        v2_ref[...] = v2

    tile = pl.BlockSpec((tr, C), lambda i: (i, 0))
    shp = jax.ShapeDtypeStruct((R, C), F32)
    return pl.pallas_call(
        body, grid=(R // tr,),
        in_specs=[tile, tile, tile, pl.BlockSpec((P, tr, C), lambda i: (0, i, 0))],
        out_specs=(tile, tile, tile, tile), out_shape=(shp, shp, shp, shp),
        compiler_params=_params(("parallel",)), name=name)(w, m, v, gparts)


def _adamw_ada(w, m, v, cond_t, dmod, *, tr, name):
    L, D, n = w.shape
    Bp = cond_t.shape[1]

    def body(w_ref, m_ref, v_ref, c_ref, dm_ref, g_ref, d_ref, m2_ref, v2_ref):
        g = _bdot(c_ref[...], dm_ref[...])
        delta, m2, v2 = _adam_math(w_ref[...], g, m_ref[...], v_ref[...])
        g_ref[...] = g
        d_ref[...] = delta
        m2_ref[...] = m2
        v2_ref[...] = v2

    tile = pl.BlockSpec((None, tr, n), lambda l, i: (l, i, 0))
    shp = jax.ShapeDtypeStruct((L, D, n), F32)
    return pl.pallas_call(
        body, grid=(L, D // tr),
        in_specs=[tile, tile, tile, pl.BlockSpec((tr, Bp), lambda l, i: (i, 0)),
                  pl.BlockSpec((None, Bp, n), lambda l, i: (l, 0, 0))],
        out_specs=(tile, tile, tile, tile), out_shape=(shp, shp, shp, shp),
        compiler_params=_params(("parallel", "parallel")), name=name)(w, m, v, cond_t, dmod)


def _peer(x, y, c, rel):
    return (x ^ ((rel >> 2) & 1), y ^ ((rel >> 1) & 1), c ^ (rel & 1))


def _allgather_small(block, *, name):
    R, C = block.shape

    def body(x_ref, out_ref, send_sems, recv_sems, local_sem):
        x, y, c = lax.axis_index("x"), lax.axis_index("y"), lax.axis_index("c")

        def rows(px, py, pc):
            return out_ref.at[pl.ds((4 * px + 2 * py + pc) * R, R), :]

        mine = pltpu.make_async_copy(x_ref, rows(x, y, c), local_sem)
        mine.start()
        sends = []
        for rel in range(1, N_DEV):
            to = _peer(x, y, c, rel)
            cp = pltpu.make_async_remote_copy(src_ref=x_ref, dst_ref=rows(x, y, c),
                                              send_sem=send_sems.at[rel - 1], recv_sem=recv_sems.at[rel - 1],
                                              device_id=to, device_id_type=MESH)
            cp.start()
            sends.append(cp)
        for rel in range(1, N_DEV):
            frm = _peer(x, y, c, rel)
            pltpu.make_async_remote_copy(src_ref=x_ref, dst_ref=rows(*frm),
                                         send_sem=send_sems.at[rel - 1], recv_sem=recv_sems.at[rel - 1],
                                         device_id=frm, device_id_type=MESH).wait_recv()
        for cp in sends:
            cp.wait_send()
        mine.wait()

    return pl.pallas_call(
        body, out_shape=jax.ShapeDtypeStruct((N_DEV * R, C), block.dtype),
        in_specs=[pl.BlockSpec(memory_space=pltpu.VMEM)],
        out_specs=pl.BlockSpec(memory_space=pltpu.VMEM),
        scratch_shapes=[pltpu.SemaphoreType.DMA((N_DEV - 1,)), pltpu.SemaphoreType.DMA((N_DEV - 1,)),
                        pltpu.SemaphoreType.DMA],
        compiler_params=pltpu.CompilerParams(vmem_limit_bytes=V7X_VMEM_LIMIT), name=name)(block)


def _allgather_hbm(shards, *, name):
    n = len(shards)

    def body(*refs):
        ins = refs[:n]
        outs = refs[n:2 * n]
        send_sems, recv_sems, local_sems = refs[2 * n:]
        x, y, c = lax.axis_index("x"), lax.axis_index("y"), lax.axis_index("c")
        sibling = (x, y, 1 - c)
        chips = [(1 - x, y), (x, 1 - y), (1 - x, 1 - y)]

        def slot(t, px, py, pc):
            return outs[t].at[4 * px + 2 * py + pc]

        def copy(t, k, block, to, src=None):
            return pltpu.make_async_remote_copy(
                src_ref=slot(t, *block) if src is None else src, dst_ref=slot(t, *block),
                send_sem=send_sems.at[t * 7 + k], recv_sem=recv_sems.at[t * 7 + k],
                device_id=to, device_id_type=MESH)

        me = (x, y, c)
        started = []
        mine = []
        for t in range(n):
            cp = pltpu.make_async_copy(ins[t], slot(t, *me), local_sems.at[t])
            cp.start()
            mine.append(cp)
            first = [copy(t, 0, me, sibling, src=ins[t])]
            first += [copy(t, 1 + j, me, (*chip, c), src=ins[t]) for j, chip in enumerate(chips)]
            for cp in first:
                cp.start()
            started += first
        for t in range(n):
            for j, chip in enumerate(chips):
                copy(t, 1 + j, (*chip, c), me).wait_recv()
                fwd = copy(t, 4 + j, (*chip, c), sibling)
                fwd.start()
                started.append(fwd)
        for t in range(n):
            copy(t, 0, sibling, me).wait_recv()
            for j, chip in enumerate(chips):
                copy(t, 4 + j, (*chip, 1 - c), me).wait_recv()
        for cp in started:
            cp.wait_send()
        for cp in mine:
            cp.wait()

    any_spec = pl.BlockSpec(memory_space=pl.ANY)
    return pl.pallas_call(
        body, out_shape=[jax.ShapeDtypeStruct((N_DEV,) + s.shape, s.dtype) for s in shards],
        in_specs=[any_spec] * n, out_specs=[any_spec] * n,
        scratch_shapes=[pltpu.SemaphoreType.DMA((7 * n,)), pltpu.SemaphoreType.DMA((7 * n,)),
                        pltpu.SemaphoreType.DMA((n,))],
        name=name)(*shards)


def _alltoall_hbm(blocks, *, name):
    n = len(blocks)

    def body(*refs):
        ins = refs[:n]
        outs = refs[n:2 * n]
        send_sems, recv_sems, local_sems = refs[2 * n:]
        x, y, c = lax.axis_index("x"), lax.axis_index("y"), lax.axis_index("c")
        me = 4 * x + 2 * y + c
        started = []
        for t in range(n):
            cp = pltpu.make_async_copy(ins[t].at[me], outs[t].at[me], local_sems.at[t])
            cp.start()
            started.append(cp)
        sends = []
        for t in range(n):
            for rel in range(1, N_DEV):
                px, py, pc = _peer(x, y, c, rel)
                cp = pltpu.make_async_remote_copy(
                    src_ref=ins[t].at[4 * px + 2 * py + pc], dst_ref=outs[t].at[me],
                    send_sem=send_sems.at[t * 7 + rel - 1], recv_sem=recv_sems.at[t * 7 + rel - 1],
                    device_id=(px, py, pc), device_id_type=MESH)
                cp.start()
                sends.append(cp)
        for t in range(n):
            for rel in range(1, N_DEV):
                px, py, pc = _peer(x, y, c, rel)
                pltpu.make_async_remote_copy(
                    src_ref=ins[t].at[me], dst_ref=outs[t].at[4 * px + 2 * py + pc],
                    send_sem=send_sems.at[t * 7 + rel - 1], recv_sem=recv_sems.at[t * 7 + rel - 1],
                    device_id=(px, py, pc), device_id_type=MESH).wait_recv()
        for cp in sends:
            cp.wait_send()
        for cp in started:
            cp.wait()

    any_spec = pl.BlockSpec(memory_space=pl.ANY)
    return pl.pallas_call(
        body, out_shape=[jax.ShapeDtypeStruct(b.shape, b.dtype) for b in blocks],
        in_specs=[any_spec] * n, out_specs=[any_spec] * n,
        scratch_shapes=[pltpu.SemaphoreType.DMA((7 * n,)), pltpu.SemaphoreType.DMA((7 * n,)),
                        pltpu.SemaphoreType.DMA((n,))],
        name=name)(*blocks)


def _tile(total, want):
    step = 128 if total % 128 == 0 else 8
    best = step
    t = step
    while t <= min(total, want):
        if total % t == 0:
            best = t
        t += step
    return best


def _local_step(x, target, mods, lbs, p, wg):
    S, D = x.shape
    L = mods.shape[0]
    W = D // 2
    H = W // HEAD
    F = wg["w_ffn_out"][0].shape[0]
    tm = _tile(S, 512)
    tm_sw = _tile(S, 128)
    tq = _tile(S, 256)
    cg = max(1, min(8, S // CHUNK))
    nb_out = _tile(D, 1024)
    kb_f = _tile(F, 1408)

    def row(a, l):
        return a[l][None, :]

    saved = []
    xcur = x
    for l in range(L):
        mod = mods[l]
        sh1, sc1, g1, sh2, sc2, g2 = [mod[:, i * D:(i + 1) * D] for i in range(N_MOD)]
        h1 = _modnorm_fwd(xcur, row(p["norm1_g"], l), sc1, sh1, tm=tm, name="norm1_fwd")
        proj = _mm_nn(h1, wg["w_in"][l], tm=tm, name="proj_fwd")
        lb = lbs[l][None, :]
        o_hg, on_hg, states = _hg_fwd(proj, lb, row(p["hg_out_g"], l), n_heads=H, cg=cg, name="hgrn2_fwd")
        qh, kh, vh = _sb_pre(proj, row(p["sb_q_g"], l), row(p["sb_k_g"], l), n_heads=H, col0=4 * H,
                             tm=tm, name="sb_qknorm_fwd")
        o_sb, on_sb = _sb_fwd(qh, kh, vh, row(p["sb_out_g"], l), n_heads=H, tq=tq, name="sb_fwd")
        o_cat = jnp.concatenate([on_hg, on_sb], axis=1)
        x1, mixed = _mm_nn(o_cat, wg["w_out"][l], tm=tm, nb=nb_out, resid=xcur, gate=g1, name="out_proj_fwd")
        h2 = _modnorm_fwd(x1, row(p["norm2_g"], l), sc2, sh2, tm=tm, name="norm2_fwd")
        u = _mm_nn(h2, wg["w_ffn_in"][l], tm=tm, name="ffn_in_fwd")
        a = _swiglu_fwd(u, tm=tm_sw, name="swiglu_fwd")
        x2, ffn = _mm_nn(a, wg["w_ffn_out"][l], tm=tm, nb=nb_out // 2, resid=x1, gate=g2, name="ffn_out_fwd")
        saved.append(dict(x=xcur, h1=h1, proj=proj, o_hg=o_hg, o_sb=o_sb, states=states, qh=qh, kh=kh, vh=vh,
                          o_cat=o_cat, mixed=mixed, x1=x1, h2=h2, u=u, a=a, ffn=ffn, lb=lb,
                          sc1=sc1, g1=g1, sc2=sc2, g2=g2))
        xcur = x2

    last = saved[-1]
    dx, dffn, dg2, loss = _loss_bwd(xcur, target, last["ffn"], last["g2"], tm=tm, name="loss_bwd")

    big = {k: [None] * L for k in ("w_in", "w_out", "w_ffn_in", "w_ffn_out")}
    small = {k: [None] * L for k in ("norm1_g", "hg_lb", "hg_out_g", "sb_q_g", "sb_k_g", "sb_out_g", "norm2_g")}
    dmods = [None] * L
    for l in reversed(range(L)):
        sv = saved[l]
        da = _mm_nt(dffn, wg["w_ffn_out"][l], tm=tm, kb=kb_f, nb=D, name="ffn_out_bwd_x")
        big["w_ffn_out"][l] = _mm_tn(sv["a"], dffn, tm=tm, kb=kb_f, nb=nb_out, blocked=False,
                                     name="ffn_out_bwd_w")
        du = _swiglu_bwd(sv["u"], da, tm=tm_sw, name="swiglu_bwd")
        dh2 = _mm_nt(du, wg["w_ffn_in"][l], tm=tm, kb=D, name="ffn_in_bwd_x")
        big["w_ffn_in"][l] = _mm_tn(sv["h2"], du, tm=tm, kb=D, nb=wg["w_ffn_in"][l].shape[2], blocked=True,
                                    name="ffn_in_bwd_w")
        dx1, dmixed, dg1, dsh2, dsc2, dn2 = _modnorm_bwd(
            sv["x1"], dh2, dx, row(p["norm2_g"], l), sv["sc2"], sv["mixed"], sv["g1"], tm=tm_sw * 2,
            name="norm2_bwd")
        small["norm2_g"][l] = dn2
        d_ocat = _mm_nt(dmixed, wg["w_out"][l], tm=tm, kb=nb_out, nb=D, name="out_proj_bwd_x")
        big["w_out"][l] = _mm_tn(sv["o_cat"], dmixed, tm=tm, kb=nb_out, nb=nb_out, blocked=False,
                                 name="out_proj_bwd_w")
        dhq, dhf, dhi, dhg, dlb, dhog = _hg_bwd(sv["proj"], sv["o_hg"], d_ocat, 0, sv["states"], sv["lb"],
                                                row(p["hg_out_g"], l), n_heads=H, cg=cg, name="hgrn2_bwd")
        dqh, dkh, dvh, dsog = _sb_bwd(sv["qh"], sv["kh"], sv["vh"], sv["o_sb"], d_ocat, H,
                                      row(p["sb_out_g"], l), n_heads=H, tq=tq, name="sb_bwd")
        dsq, dsk, dsv, dqg, dkg = _sb_pre_bwd(sv["proj"], dqh, dkh, dvh, row(p["sb_q_g"], l),
                                              row(p["sb_k_g"], l), n_heads=H, col0=4 * H, tm=tm,
                                              name="sb_qknorm_bwd")
        small["hg_lb"][l] = dlb
        small["hg_out_g"][l] = dhog
        small["sb_out_g"][l] = dsog
        small["sb_q_g"][l] = dqg
        small["sb_k_g"][l] = dkg
        dproj = jnp.concatenate([dhq, dhf, dhi, dhg, dsq, dsk, dsv], axis=1)
        dh1 = _mm_nt(dproj, wg["w_in"][l], tm=tm, kb=D, name="proj_bwd_x")
        big["w_in"][l] = _mm_tn(sv["h1"], dproj, tm=tm, kb=D, nb=wg["w_in"][l].shape[2], blocked=True,
                                name="proj_bwd_w")
        if l > 0:
            prev = saved[l - 1]
            dx0, dffn_prev, dg2_prev, dsh1, dsc1, dn1 = _modnorm_bwd(
                sv["x"], dh1, dx1, row(p["norm1_g"], l), sv["sc1"], prev["ffn"], prev["g2"], tm=tm_sw * 2,
                name="norm1_bwd")
        else:
            dx0, dsh1, dsc1, dn1 = _modnorm_bwd(sv["x"], dh1, dx1, row(p["norm1_g"], l), sv["sc1"], None, None,
                                                tm=tm_sw * 2, name="norm1_bwd_first")
            dffn_prev, dg2_prev = None, None
        small["norm1_g"][l] = dn1
        dmods[l] = jnp.concatenate([dsh1, dsc1, dg1, dsh2, dsc2, dg2], axis=1)
        dx, dffn, dg2 = dx0, dffn_prev, dg2_prev
    return loss, dx, big, small, dmods


def kernel(x, c, norm1_g, w_in, hg_lb_logits, hg_out_g, sb_q_g, sb_k_g, sb_out_g, w_out, norm2_g, w_ffn_in, w_ffn_out, w_ada, b_ada, loss_target, m_norm1_g, m_w_in, m_hg_lb_logits, m_hg_out_g, m_sb_q_g, m_sb_k_g, m_sb_out_g, m_w_out, m_norm2_g, m_w_ffn_in, m_w_ffn_out, m_w_ada, m_b_ada, v_norm1_g, v_w_in, v_hg_lb_logits, v_hg_out_g, v_sb_q_g, v_sb_k_g, v_sb_out_g, v_w_out, v_norm2_g, v_w_ffn_in, v_w_ffn_out, v_w_ada, v_b_ada):
    L, D = norm1_g.shape
    S = x.shape[1]
    me = 4 * lax.axis_index("x") + 2 * lax.axis_index("y") + lax.axis_index("c")

    c_all = _allgather_small(jnp.broadcast_to(c, (8, D)), name="gather_c").reshape(N_DEV, 8, D)[:, 0, :]
    n_ada = w_ada.shape[2]
    mod_cols, cond = _ada_mod(c_all, w_ada, nb=_tile(n_ada, 512), name="ada_mod")
    mod_all = _allgather_small(mod_cols.reshape(L * N_DEV, n_ada), name="gather_mod")
    mod_all = mod_all.reshape(N_DEV, L, N_DEV, n_ada)
    mod_mine = lax.dynamic_index_in_dim(mod_all, me, axis=2, keepdims=False)
    mods = jnp.transpose(mod_mine, (1, 0, 2)).reshape(L, 1, N_DEV * n_ada) + b_ada[:, None, :]

    lbs = _lb_fwd(hg_lb_logits, name="lower_bounds_fwd")

    wg = {k: [] for k in ("w_in", "w_out", "w_ffn_in", "w_ffn_out")}
    for l in range(L):
        g_in, g_out, g_fin, g_fout = _allgather_hbm(
            [w_in[l].astype(BF16), w_out[l].astype(BF16), w_ffn_in[l].astype(BF16), w_ffn_out[l].astype(BF16)],
            name="gather_weights")
        wg["w_in"].append(g_in)
        wg["w_out"].append(g_out.reshape(-1, D))
        wg["w_ffn_in"].append(g_fin)
        wg["w_ffn_out"].append(g_fout.reshape(-1, D))

    p = dict(norm1_g=norm1_g, hg_out_g=hg_out_g, sb_q_g=sb_q_g, sb_k_g=sb_k_g, sb_out_g=sb_out_g,
             norm2_g=norm2_g)
    loss_part, grad_x, big, small, dmods = _local_step(x.reshape(S, D), loss_target.reshape(S, D), mods, lbs, p, wg)

    recv = {k: [None] * L for k in big}
    for l in range(L):
        r_in, r_out, r_fin, r_fout = _alltoall_hbm(
            [big["w_in"][l], big["w_out"][l].reshape(N_DEV, -1, D), big["w_ffn_in"][l],
             big["w_ffn_out"][l].reshape(N_DEV, -1, D)], name="scatter_weight_grads")
        recv["w_in"][l], recv["w_out"][l], recv["w_ffn_in"][l], recv["w_ffn_out"][l] = r_in, r_out, r_fin, r_fout

    dmod = jnp.concatenate(dmods, axis=0)
    pieces = [jnp.concatenate(small[k], axis=0) for k in
              ("norm1_g", "hg_lb", "hg_out_g", "sb_q_g", "sb_k_g", "sb_out_g", "norm2_g")] + [dmod]
    flat = jnp.concatenate([a.reshape(-1) for a in pieces] + [loss_part.reshape(-1)])
    n_flat = flat.shape[0]
    rows = -(-n_flat // 1024) * 8
    flat = jnp.pad(flat, (0, rows * 128 - n_flat)).reshape(rows, 128)
    gathered = _allgather_small(flat, name="gather_small_grads").reshape(N_DEV, rows * 128)

    def take(off, shape):
        size = 1
        for s in shape:
            size *= s
        return gathered[:, off:off + size].reshape((N_DEV,) + tuple(shape)), off + size

    off = 0
    parts = {}
    for k, a in zip(("norm1_g", "hg_lb", "hg_out_g", "sb_q_g", "sb_k_g", "sb_out_g", "norm2_g", "dmod"), pieces):
        parts[k], off = take(off, a.shape)
    loss_parts = gathered[:, off:off + 1]
    loss = jnp.sum(loss_parts)

    def pad8(a):
        return jnp.pad(a, ((0, 0), (0, 8 - a.shape[1]), (0, 0)))

    def small_update(w, m, v, gparts):
        Lw = w.shape[0]
        g, d, m2, v2 = _adamw(pad8(w[None])[0], pad8(m[None])[0], pad8(v[None])[0], pad8(gparts),
                              tr=8, name="adamw_small")
        return g[:Lw], d[:Lw], m2[:Lw], v2[:Lw]

    out = {}
    out["norm1_g"] = small_update(norm1_g, m_norm1_g, v_norm1_g, parts["norm1_g"])
    dlogits = _lb_bwd(hg_lb_logits, parts["hg_lb"], name="lower_bounds_bwd")
    out["hg_lb_logits"] = small_update(hg_lb_logits, m_hg_lb_logits, v_hg_lb_logits, dlogits[None])
    out["hg_out_g"] = small_update(hg_out_g, m_hg_out_g, v_hg_out_g, parts["hg_out_g"])
    out["sb_q_g"] = small_update(sb_q_g, m_sb_q_g, v_sb_q_g, parts["sb_q_g"])
    out["sb_k_g"] = small_update(sb_k_g, m_sb_k_g, v_sb_k_g, parts["sb_k_g"])
    out["sb_out_g"] = small_update(sb_out_g, m_sb_out_g, v_sb_out_g, parts["sb_out_g"])
    out["norm2_g"] = small_update(norm2_g, m_norm2_g, v_norm2_g, parts["norm2_g"])
    out["b_ada"] = small_update(b_ada, m_b_ada, v_b_ada, parts["dmod"])

    dmod_all = parts["dmod"].reshape(N_DEV, L, N_DEV, n_ada)
    dmod_mine = lax.dynamic_index_in_dim(dmod_all, me, axis=2, keepdims=False)
    dmod_mine = jnp.pad(jnp.transpose(dmod_mine, (1, 0, 2)), ((0, 0), (0, 128 - N_DEV), (0, 0)))
    cond_t = jnp.pad(jnp.transpose(cond), ((0, 0), (0, 128 - N_DEV)))
    out["w_ada"] = _adamw_ada(w_ada, m_w_ada, v_w_ada, cond_t, dmod_mine, tr=_tile(D, 256), name="adamw_ada")

    def big_update(w, m, v, recv_l, name):
        shp = w.shape
        C = shp[-1]
        w2, m2, v2 = w.reshape(-1, C), m.reshape(-1, C), v.reshape(-1, C)
        gp = jnp.concatenate([r.reshape(N_DEV, -1, C) for r in recv_l], axis=1)
        res = _adamw(w2, m2, v2, gp, tr=_tile(w2.shape[0], 256), name=name)
        return tuple(r.reshape(shp) for r in res)

    out["w_in"] = big_update(w_in, m_w_in, v_w_in, recv["w_in"], "adamw_w_in")
    out["w_out"] = big_update(w_out, m_w_out, v_w_out, recv["w_out"], "adamw_w_out")
    out["w_ffn_in"] = big_update(w_ffn_in, m_w_ffn_in, v_w_ffn_in, recv["w_ffn_in"], "adamw_w_ffn_in")
    out["w_ffn_out"] = big_update(w_ffn_out, m_w_ffn_out, v_w_ffn_out, recv["w_ffn_out"], "adamw_w_ffn_out")

    order = ("norm1_g", "w_in", "hg_lb_logits", "hg_out_g", "sb_q_g", "sb_k_g", "sb_out_g", "w_out", "norm2_g",
             "w_ffn_in", "w_ffn_out", "w_ada", "b_ada")
    grads = [out[k][0] for k in order]
    deltas = [out[k][1] for k in order]
    new_m = [out[k][2] for k in order]
    new_v = [out[k][3] for k in order]
    return (loss, grad_x.reshape(1, S, D), *grads, *deltas, *new_m, *new_v)
```

```python
import functools

import jax
import jax.numpy as jnp
from jax import lax
from jax.experimental import pallas as pl
from jax.experimental.pallas import tpu as pltpu

F32 = jnp.float32
BF16 = jnp.bfloat16
MESH = pl.DeviceIdType.MESH

N_DEV = 8
HEAD = 128
CHUNK = 64
N_MOD = 6
EPS = 1e-6
TINY = 1e-30
ADAM_LR = 0.001
ADAM_B1 = 0.9
ADAM_B2 = 0.999
ADAM_EPS = 1e-08
ADAM_WD = 0.01
ADAM_STEP = 10
V7X_VMEM_LIMIT = 56 * 1024 * 1024
SKIP_LOG = -104.0


def _params(sem):
    return pltpu.CompilerParams(dimension_semantics=sem, vmem_limit_bytes=V7X_VMEM_LIMIT)


def _bdot(a, b, dims=(((1,), (0,)), ((), ()))):
    return lax.dot_general(a.astype(BF16), b.astype(BF16), dims, preferred_element_type=F32)


_NT = (((1,), (1,)), ((), ()))
_TN = (((0,), (0,)), ((), ()))


def _sigmoid_pair(x):
    e = jnp.exp(-jnp.abs(x))
    r = 1.0 / (1.0 + e)
    er = e * r
    pos = x >= 0
    return jnp.where(pos, r, er), jnp.where(pos, er, r)


def _split_dot(x, u, parts):
    acc = None
    rem = x
    for _ in range(parts):
        p = rem.astype(BF16)
        rem = rem - p.astype(F32)
        t = lax.dot_general(p, u, (((1,), (0,)), ((), ())), preferred_element_type=F32)
        acc = t if acc is None else acc + t
    return acc


def _split_dot_left(u, x, parts):
    acc = None
    rem = x
    for _ in range(parts):
        p = rem.astype(BF16)
        rem = rem - p.astype(F32)
        t = lax.dot_general(u, p, (((1,), (0,)), ((), ())), preferred_element_type=F32)
        acc = t if acc is None else acc + t
    return acc


def _peer(x, y, c, rel):
    return (x ^ ((rel >> 2) & 1), y ^ ((rel >> 1) & 1), c ^ (rel & 1))


def _exchange(gather, srcs, dsts, send_sems, recv_sems, local_sems, phase):
    x, y, c = lax.axis_index("x"), lax.axis_index("y"), lax.axis_index("c")
    me = 4 * x + 2 * y + c
    for t in range(len(srcs)):
        own = srcs[t] if gather else srcs[t].at[me]
        local = pltpu.make_async_copy(own, dsts[t].at[me], local_sems.at[t])
        if phase == "start":
            local.start()
        for rel in range(1, N_DEV):
            px, py, pc = _peer(x, y, c, rel)
            pid = 4 * px + 2 * py + pc
            k = t * (N_DEV - 1) + rel - 1
            if phase == "start":
                pltpu.make_async_remote_copy(
                    src_ref=srcs[t] if gather else srcs[t].at[pid], dst_ref=dsts[t].at[me],
                    send_sem=send_sems.at[k], recv_sem=recv_sems.at[k],
                    device_id=(px, py, pc), device_id_type=MESH).start()
            else:
                cp = pltpu.make_async_remote_copy(
                    src_ref=own, dst_ref=dsts[t].at[pid], send_sem=send_sems.at[k], recv_sem=recv_sems.at[k],
                    device_id=(px, py, pc), device_id_type=MESH)
                cp.wait_recv()
                cp.wait_send()
        if phase == "wait":
            local.wait()


def _exchange_scratch(n):
    return [pltpu.SemaphoreType.DMA(((N_DEV - 1) * n,)), pltpu.SemaphoreType.DMA(((N_DEV - 1) * n,)),
            pltpu.SemaphoreType.DMA((n,))]


def _pcall(body, *, grid, in_specs, out_specs, out_shape, scratch_shapes=(), semantics, name, args, side=None):
    single = not isinstance(out_shape, (tuple, list))
    if single:
        out_specs, out_shape = [out_specs], [out_shape]
    in_specs, out_specs, out_shape = list(in_specs), list(out_specs), list(out_shape)
    scratch_shapes = list(scratch_shapes)
    n_in, n_out, n_scr = len(in_specs), len(out_specs), len(scratch_shapes)
    if side is None:
        res = pl.pallas_call(body, grid=grid, in_specs=in_specs, out_specs=out_specs, out_shape=out_shape,
                             scratch_shapes=scratch_shapes, compiler_params=_params(semantics), name=name)(*args)
        return (res[0] if single else tuple(res)), None
    gather, srcs = side
    n = len(srcs)

    def full(*refs):
        ins = refs[:n_in]
        s_in = refs[n_in:n_in + n]
        outs = refs[n_in + n:n_in + n + n_out]
        s_out = refs[n_in + n + n_out:n_in + 2 * n + n_out]
        scr = refs[n_in + 2 * n + n_out:n_in + 2 * n + n_out + n_scr]
        send_sems, recv_sems, local_sems = refs[n_in + 2 * n + n_out + n_scr:]
        first = pl.program_id(0) == 0
        last = pl.program_id(0) == grid[0] - 1
        for ax in range(1, len(grid)):
            first = jnp.logical_and(first, pl.program_id(ax) == 0)
            last = jnp.logical_and(last, pl.program_id(ax) == grid[ax] - 1)

        @pl.when(first)
        def _():
            _exchange(gather, s_in, s_out, send_sems, recv_sems, local_sems, "start")

        body(*ins, *outs, *scr)

        @pl.when(last)
        def _():
            _exchange(gather, s_in, s_out, send_sems, recv_sems, local_sems, "wait")

    any_spec = pl.BlockSpec(memory_space=pl.ANY)
    s_shapes = [jax.ShapeDtypeStruct(((N_DEV,) + s.shape) if gather else s.shape, s.dtype) for s in srcs]
    res = pl.pallas_call(full, grid=grid, in_specs=in_specs + [any_spec] * n,
                         out_specs=out_specs + [any_spec] * n, out_shape=out_shape + s_shapes,
                         scratch_shapes=scratch_shapes + _exchange_scratch(n),
                         compiler_params=_params(("arbitrary",) * len(grid)), name=name)(*args, *srcs)
    main = res[:n_out]
    return (main[0] if single else tuple(main)), list(res[n_out:])


def _mm_nn(a, b, *, tm, nb=None, out_dtype=F32, resid=None, gate=None, name, side=None):
    M, K = a.shape
    if b.ndim == 3:
        NB, _, n = b.shape
        b_spec = pl.BlockSpec((None, K, n), lambda j, i: (j, 0, 0))
    else:
        n = nb
        NB = b.shape[1] // nb
        b_spec = pl.BlockSpec((K, n), lambda j, i: (0, j))
    N = NB * n
    epi = resid is not None

    def body(*refs):
        if epi:
            a_ref, b_ref, r_ref, g_ref, o_ref, acc_ref = refs
        else:
            a_ref, b_ref, o_ref = refs
        acc = jnp.dot(a_ref[...], b_ref[...], preferred_element_type=F32)
        if epi:
            o_ref[...] = r_ref[...] + g_ref[...] * acc
            acc_ref[...] = acc.astype(BF16)
        else:
            o_ref[...] = acc.astype(out_dtype)

    in_specs = [pl.BlockSpec((tm, K), lambda j, i: (i, 0)), b_spec]
    args = [a, b]
    o_spec = pl.BlockSpec((tm, n), lambda j, i: (i, j))
    if epi:
        in_specs += [pl.BlockSpec((tm, n), lambda j, i: (i, j)), pl.BlockSpec((1, n), lambda j, i: (0, j))]
        args += [resid, gate]
        out_shape = (jax.ShapeDtypeStruct((M, N), F32), jax.ShapeDtypeStruct((M, N), BF16))
        out_specs = (o_spec, o_spec)
    else:
        out_shape = jax.ShapeDtypeStruct((M, N), out_dtype)
        out_specs = o_spec
    return _pcall(body, grid=(NB, M // tm), in_specs=in_specs, out_specs=out_specs, out_shape=out_shape,
                  semantics=("parallel", "parallel"), name=name, args=args, side=side)


def _mm_nt(dy, w, *, tm, kb, nb=None, out_dtype=F32, name, side=None):
    M, N = dy.shape
    if w.ndim == 3:
        NB, Kt, n = w.shape
        w_spec = pl.BlockSpec((None, kb, n), lambda i, k, j: (j, k, 0))
    else:
        Kt = w.shape[0]
        n = nb
        NB = N // nb
        w_spec = pl.BlockSpec((kb, n), lambda i, k, j: (k, j))
    KB = Kt // kb

    def body(dy_ref, w_ref, o_ref, *acc):
        part = lax.dot_general(dy_ref[...], w_ref[...], _NT, preferred_element_type=F32)
        if NB == 1:
            o_ref[...] = part.astype(out_dtype)
            return
        acc_ref, = acc
        j = pl.program_id(2)

        @pl.when(j == 0)
        def _():
            acc_ref[...] = part

        @pl.when(jnp.logical_and(j > 0, j < NB - 1))
        def _():
            acc_ref[...] += part

        @pl.when(j == NB - 1)
        def _():
            o_ref[...] = (acc_ref[...] + part).astype(out_dtype)

    return _pcall(
        body, grid=(M // tm, KB, NB),
        in_specs=[pl.BlockSpec((tm, n), lambda i, k, j: (i, j)), w_spec],
        out_specs=pl.BlockSpec((tm, kb), lambda i, k, j: (i, k)),
        out_shape=jax.ShapeDtypeStruct((M, Kt), out_dtype),
        scratch_shapes=[] if NB == 1 else [pltpu.VMEM((tm, kb), F32)],
        semantics=("parallel", "parallel", "arbitrary"), name=name, args=(dy, w), side=side)


def _mm_tn(x, dy, *, tm, kb, nb, blocked, name, side=None):
    M, K = x.shape
    N = dy.shape[1]
    KB, NB, MB = K // kb, N // nb, M // tm

    def body(x_ref, dy_ref, o_ref, *acc):
        part = lax.dot_general(x_ref[...], dy_ref[...], _TN, preferred_element_type=F32)
        if MB == 1:
            o_ref[...] = part.astype(BF16)
            return
        acc_ref, = acc
        m = pl.program_id(2)

        @pl.when(m == 0)
        def _():
            acc_ref[...] = part

        @pl.when(jnp.logical_and(m > 0, m < MB - 1))
        def _():
            acc_ref[...] += part

        @pl.when(m == MB - 1)
        def _():
            o_ref[...] = (acc_ref[...] + part).astype(BF16)

    if blocked:
        out_shape = jax.ShapeDtypeStruct((NB, K, nb), BF16)
        o_spec = pl.BlockSpec((None, kb, nb), lambda k, n, m: (n, k, 0))
    else:
        out_shape = jax.ShapeDtypeStruct((K, N), BF16)
        o_spec = pl.BlockSpec((kb, nb), lambda k, n, m: (k, n))
    return _pcall(
        body, grid=(KB, NB, MB),
        in_specs=[pl.BlockSpec((tm, kb), lambda k, n, m: (m, k)),
                  pl.BlockSpec((tm, nb), lambda k, n, m: (m, n))],
        out_specs=o_spec, out_shape=out_shape,
        scratch_shapes=[] if MB == 1 else [pltpu.VMEM((kb, nb), F32)],
        semantics=("parallel", "parallel", "arbitrary"), name=name, args=(x, dy), side=side)


def _modnorm_fwd(x, gain, sc, sh, *, tm, name):
    S, D = x.shape

    def body(x_ref, g_ref, sc_ref, sh_ref, h_ref):
        xv = x_ref[...]
        rstd = lax.rsqrt(jnp.mean(xv * xv, axis=-1, keepdims=True) + EPS)
        y = (xv * rstd) * g_ref[...]
        h_ref[...] = (y * (1.0 + sc_ref[...]) + sh_ref[...]).astype(BF16)

    row = pl.BlockSpec((1, D), lambda i: (0, 0))
    return pl.pallas_call(
        body, grid=(S // tm,),
        in_specs=[pl.BlockSpec((tm, D), lambda i: (i, 0)), row, row, row],
        out_specs=pl.BlockSpec((tm, D), lambda i: (i, 0)),
        out_shape=jax.ShapeDtypeStruct((S, D), BF16),
        compiler_params=_params(("parallel",)), name=name)(x, gain, sc, sh)


def _modnorm_bwd(x, dh, dres, gain, sc, branch, gate, *, tm, name):
    S, D = x.shape
    has_prev = branch is not None

    def body(*refs):
        if has_prev:
            (x_ref, dh_ref, dr_ref, g_ref, sc_ref, br_ref, gt_ref,
             dx_ref, dbr_ref, dgt_ref, dsh_ref, dsc_ref, dgn_ref) = refs
        else:
            (x_ref, dh_ref, dr_ref, g_ref, sc_ref,
             dx_ref, dsh_ref, dsc_ref, dgn_ref) = refs
        i = pl.program_id(0)
        xv = x_ref[...]
        dh_v = dh_ref[...]
        gv = g_ref[...]
        scale1 = 1.0 + sc_ref[...]
        rstd = lax.rsqrt(jnp.mean(xv * xv, axis=-1, keepdims=True) + EPS)
        n = xv * rstd
        dn = dh_v * (gv * scale1)
        dx = rstd * (dn - n * jnp.mean(dn * n, axis=-1, keepdims=True)) + dr_ref[...]
        dx_ref[...] = dx
        dhn = dh_v * n
        p_sh = jnp.sum(dh_v, axis=0, keepdims=True)
        p_sc = jnp.sum(dhn, axis=0, keepdims=True) * gv
        p_gn = jnp.sum(dhn, axis=0, keepdims=True) * scale1
        if has_prev:
            dbr_ref[...] = (gt_ref[...] * dx).astype(BF16)
            p_gt = jnp.sum(dx * br_ref[...].astype(F32), axis=0, keepdims=True)

        @pl.when(i == 0)
        def _():
            dsh_ref[...] = p_sh
            dsc_ref[...] = p_sc
            dgn_ref[...] = p_gn
            if has_prev:
                dgt_ref[...] = p_gt

        @pl.when(i > 0)
        def _():
            dsh_ref[...] += p_sh
            dsc_ref[...] += p_sc
            dgn_ref[...] += p_gn
            if has_prev:
                dgt_ref[...] += p_gt

    tile = pl.BlockSpec((tm, D), lambda i: (i, 0))
    row = pl.BlockSpec((1, D), lambda i: (0, 0))
    row_shape = jax.ShapeDtypeStruct((1, D), F32)
    if has_prev:
        in_specs = [tile, tile, tile, row, row, tile, row]
        args = (x, dh, dres, gain, sc, branch, gate)
        out_specs = (tile, tile, row, row, row, row)
        out_shape = (jax.ShapeDtypeStruct((S, D), F32), jax.ShapeDtypeStruct((S, D), BF16),
                     row_shape, row_shape, row_shape, row_shape)
    else:
        in_specs = [tile, tile, tile, row, row]
        args = (x, dh, dres, gain, sc)
        out_specs = (tile, row, row, row)
        out_shape = (jax.ShapeDtypeStruct((S, D), F32), row_shape, row_shape, row_shape)
    return pl.pallas_call(body, grid=(S // tm,), in_specs=in_specs, out_specs=out_specs,
                          out_shape=out_shape, compiler_params=_params(("arbitrary",)),
                          name=name)(*args)


def _swiglu_fwd(u, *, tm, name):
    S, F2 = u.shape
    F = F2 // 2

    def body(u_ref, a_ref):
        gate = u_ref[:, :F]
        up = u_ref[:, F:]
        s, _ = _sigmoid_pair(gate)
        a_ref[...] = (gate * s * up).astype(BF16)

    return pl.pallas_call(
        body, grid=(S // tm,),
        in_specs=[pl.BlockSpec((tm, F2), lambda i: (i, 0))],
        out_specs=pl.BlockSpec((tm, F), lambda i: (i, 0)),
        out_shape=jax.ShapeDtypeStruct((S, F), BF16),
        compiler_params=_params(("parallel",)), name=name)(u)


def _swiglu_bwd(u, da, *, tm, name):
    S, F2 = u.shape
    F = F2 // 2

    def body(u_ref, da_ref, du_ref):
        gate = u_ref[:, :F]
        up = u_ref[:, F:]
        dav = da_ref[...]
        s, ns = _sigmoid_pair(gate)
        du_ref[:, :F] = (dav * up * (s * (1.0 + gate * ns))).astype(BF16)
        du_ref[:, F:] = (dav * (gate * s)).astype(BF16)

    return pl.pallas_call(
        body, grid=(S // tm,),
        in_specs=[pl.BlockSpec((tm, F2), lambda i: (i, 0)), pl.BlockSpec((tm, F), lambda i: (i, 0))],
        out_specs=pl.BlockSpec((tm, F2), lambda i: (i, 0)),
        out_shape=jax.ShapeDtypeStruct((S, F2), BF16),
        compiler_params=_params(("parallel",)), name=name)(u, da)


def _loss_bwd(y, target, branch, gate, *, tm, name):
    S, D = y.shape
    nsteps = S // tm

    def body(y_ref, t_ref, br_ref, gt_ref, dy_ref, dbr_ref, dgt_ref, loss_ref, col_ref):
        i = pl.program_id(0)
        diff = y_ref[...] - t_ref[...]
        dy = diff * (1.0 / D)
        dy_ref[...] = dy
        dbr_ref[...] = (gt_ref[...] * dy).astype(BF16)
        p_gt = jnp.sum(dy * br_ref[...].astype(F32), axis=0, keepdims=True)
        p_col = jnp.sum(diff * diff, axis=0, keepdims=True)

        @pl.when(i == 0)
        def _():
            dgt_ref[...] = p_gt
            col_ref[...] = p_col

        @pl.when(i > 0)
        def _():
            dgt_ref[...] += p_gt
            col_ref[...] += p_col

        @pl.when(i == nsteps - 1)
        def _():
            tot = jnp.sum(col_ref[...], axis=-1, keepdims=True) * (0.5 / D)
            loss_ref[...] = jnp.broadcast_to(tot, (1, 128))

    tile = pl.BlockSpec((tm, D), lambda i: (i, 0))
    row = pl.BlockSpec((1, D), lambda i: (0, 0))
    return pl.pallas_call(
        body, grid=(nsteps,), in_specs=[tile, tile, tile, row],
        out_specs=(tile, tile, row, pl.BlockSpec((1, 128), lambda i: (0, 0))),
        out_shape=(jax.ShapeDtypeStruct((S, D), F32), jax.ShapeDtypeStruct((S, D), BF16),
                   jax.ShapeDtypeStruct((1, D), F32), jax.ShapeDtypeStruct((1, 128), F32)),
        scratch_shapes=[pltpu.VMEM((1, D), F32)],
        compiler_params=_params(("arbitrary",)), name=name)(y, target, branch, gate)


def _hg_chunk(q, fl, lbv, tri):
    C = q.shape[0]
    sq, nsq = _sigmoid_pair(q)
    qa = q * sq
    sig, nsig = _sigmoid_pair(fl)
    one_lb = 1.0 - lbv
    f = lbv + one_lb * sig
    fc = jnp.maximum(f, TINY)
    lf = jnp.log(fc)
    k = one_lb * nsig
    b = _split_dot_left(tri, lf, 3)
    row = lax.broadcasted_iota(jnp.int32, b.shape, 0)
    bm = jnp.sum(jnp.where(row == C // 2 - 1, b, 0.0), axis=0, keepdims=True)
    bl = jnp.sum(jnp.where(row == C - 1, b, 0.0), axis=0, keepdims=True)
    eb = jnp.exp(b)
    ebm = jnp.exp(b - bm)
    enbm = jnp.exp(bm - b)
    ebl = jnp.exp(bl - b)
    ebL = jnp.exp(bl)

    def operand(t):
        return t.astype(BF16).astype(F32)

    return dict(sq=sq, nsq=nsq, qa=qa, sig=sig, nsig=nsig, one_lb=one_lb, f=f, fc=fc, k=k,
                eb=eb, ebm=ebm, enbm=enbm, ebl=ebl, ebL=ebL,
                Qm=operand(qa * ebm), Km=operand(k * enbm), Qb=operand(qa * eb), Kh=operand(k * ebl), row=row)


def _causal_incl(C):
    r = lax.broadcasted_iota(jnp.int32, (C, C), 0)
    c = lax.broadcasted_iota(jnp.int32, (C, C), 1)
    return r >= c


def _hg_fwd(proj, lb, out_g, *, n_heads, cg, name):
    S = proj.shape[0]
    H = n_heads
    W = H * HEAD
    T = cg * CHUNK
    NG = S // T
    tri = jnp.tril(jnp.ones((CHUNK, CHUNK), F32)).astype(BF16)

    def body(q_ref, f_ref, v_ref, g_ref, lb_ref, og_ref, tri_ref, o_ref, on_ref, st_ref, s_scr):
        @pl.when(pl.program_id(1) == 0)
        def _():
            s_scr[...] = jnp.zeros_like(s_scr)

        lbv = lb_ref[...]
        ogv = og_ref[...]
        triv = tri_ref[...]
        mask = _causal_incl(CHUNK)
        for c in range(cg):
            rows = pl.ds(c * CHUNK, CHUNK)
            v = v_ref[rows, :]
            gg = g_ref[rows, :]
            cm = _hg_chunk(q_ref[rows, :], f_ref[rows, :], lbv, triv)
            s0 = s_scr[...]
            st_ref[c] = s0
            A = jnp.where(mask, _bdot(cm["Qm"], cm["Km"], _NT), 0.0)
            o = _bdot(A, v) + _bdot(cm["Qb"], s0, _NT)
            s_scr[...] = s0 * cm["ebL"] + _bdot(v, cm["Kh"], _TN)
            o_ref[rows, :] = o
            rstd = lax.rsqrt(jnp.mean(o * o, axis=-1, keepdims=True) + EPS)
            sg, _ = _sigmoid_pair(gg)
            on_ref[rows, :] = (((o * rstd) * ogv) * (gg * sg)).astype(BF16)

    def col(group):
        return pl.BlockSpec((T, HEAD), lambda h, g: (g, group * H + h))

    vec = pl.BlockSpec((1, HEAD), lambda h, g: (0, h))
    return pl.pallas_call(
        body, grid=(H, NG),
        in_specs=[col(0), col(1), col(2), col(3), vec, vec,
                  pl.BlockSpec((CHUNK, CHUNK), lambda h, g: (0, 0))],
        out_specs=(pl.BlockSpec((T, HEAD), lambda h, g: (g, h)),
                   pl.BlockSpec((T, HEAD), lambda h, g: (g, h)),
                   pl.BlockSpec((cg, None, HEAD, HEAD), lambda h, g: (g, h, 0, 0))),
        out_shape=(jax.ShapeDtypeStruct((S, W), F32), jax.ShapeDtypeStruct((S, W), BF16),
                   jax.ShapeDtypeStruct((S // CHUNK, H, HEAD, HEAD), F32)),
        scratch_shapes=[pltpu.VMEM((HEAD, HEAD), F32)],
        compiler_params=_params(("parallel", "arbitrary")), name=name,
    )(proj, proj, proj, proj, lb, out_g, tri)


def _hg_bwd(proj, o_pre, d_on, d_on_col0, states, lb, out_g, *, n_heads, cg, name):
    S = proj.shape[0]
    H = n_heads
    W = H * HEAD
    T = cg * CHUNK
    NG = S // T
    tri = jnp.tril(jnp.ones((CHUNK, CHUNK), F32)).astype(BF16)
    triu = jnp.triu(jnp.ones((CHUNK, CHUNK), F32)).astype(BF16)

    def body(q_ref, f_ref, v_ref, g_ref, o_ref, dy_ref, st_ref, lb_ref, og_ref, tri_ref, triu_ref,
             dq_ref, df_ref, di_ref, dg_ref, dlb_ref, dog_ref, ds_scr):
        gstep = pl.program_id(1)

        @pl.when(gstep == 0)
        def _():
            ds_scr[...] = jnp.zeros_like(ds_scr)
            dlb_ref[...] = jnp.zeros_like(dlb_ref)
            dog_ref[...] = jnp.zeros_like(dog_ref)

        lbv = lb_ref[...]
        ogv = og_ref[...]
        triv = tri_ref[...]
        triuv = triu_ref[...]
        mask = _causal_incl(CHUNK)
        dlb_acc = jnp.zeros((1, HEAD), F32)
        dog_acc = jnp.zeros((1, HEAD), F32)
        for c in reversed(range(cg)):
            rows = pl.ds(c * CHUNK, CHUNK)
            q = q_ref[rows, :]
            v = v_ref[rows, :]
            gg = g_ref[rows, :]
            o = o_ref[rows, :]
            dy = dy_ref[rows, :]
            cm = _hg_chunk(q, f_ref[rows, :], lbv, triv)
            s0 = st_ref[c]
            ds1 = ds_scr[...]
            rstd = lax.rsqrt(jnp.mean(o * o, axis=-1, keepdims=True) + EPS)
            n = o * rstd
            sg, nsg = _sigmoid_pair(gg)
            silu_g = gg * sg
            dyn = dy * n
            dog_acc = dog_acc + jnp.sum(dyn * silu_g, axis=0, keepdims=True)
            dg_ref[rows, :] = (dyn * ogv * (sg * (1.0 + gg * nsg))).astype(BF16)
            dn = dy * (ogv * silu_g)
            d_o = rstd * (dn - n * jnp.mean(dn * n, axis=-1, keepdims=True))
            A = jnp.where(mask, _bdot(cm["Qm"], cm["Km"], _NT), 0.0)
            dA = jnp.where(mask, _bdot(d_o, v, _NT), 0.0)
            dV = _bdot(A, d_o, _TN) + _bdot(cm["Kh"], ds1, _NT)
            dQm = _bdot(dA, cm["Km"])
            dKm = _bdot(dA, cm["Qm"], _TN)
            dQb = _bdot(d_o, s0)
            dKh = _bdot(v, ds1)
            ds_scr[...] = ds1 * cm["ebL"] + _bdot(d_o, cm["Qb"], _TN)
            kh_term = dKh * cm["Kh"]
            db = dQm * cm["Qm"] - dKm * cm["Km"] + dQb * cm["Qb"] - kh_term
            dbl = (jnp.sum(kh_term, axis=0, keepdims=True)
                   + cm["ebL"] * jnp.sum(ds1 * s0, axis=0, keepdims=True))
            db = db + jnp.where(cm["row"] == CHUNK - 1, dbl, 0.0)
            dlf = _split_dot_left(triuv, db, 3)
            dqa = dQm * cm["ebm"] + dQb * cm["eb"]
            dq_ref[rows, :] = (dqa * (cm["sq"] * (1.0 + q * cm["nsq"]))).astype(BF16)
            dk = dKm * cm["enbm"] + dKh * cm["ebl"]
            dfc = jnp.where(cm["f"] > TINY, dlf / cm["fc"], 0.0)
            t = dfc - dk
            df_ref[rows, :] = (t * (cm["one_lb"] * cm["sig"] * cm["nsig"])).astype(BF16)
            dlb_acc = dlb_acc + jnp.sum(t * cm["nsig"], axis=0, keepdims=True)
            di_ref[rows, :] = dV.astype(BF16)
        dlb_ref[...] += dlb_acc
        dog_ref[...] += dog_acc

    def col(group):
        return pl.BlockSpec((T, HEAD), lambda h, g: (NG - 1 - g, group * H + h))

    own = pl.BlockSpec((T, HEAD), lambda h, g: (NG - 1 - g, h))
    vec = pl.BlockSpec((1, HEAD), lambda h, g: (0, h))
    cst = pl.BlockSpec((CHUNK, CHUNK), lambda h, g: (0, 0))
    act = jax.ShapeDtypeStruct((S, W), BF16)
    vec_shape = jax.ShapeDtypeStruct((1, W), F32)
    return pl.pallas_call(
        body, grid=(H, NG),
        in_specs=[col(0), col(1), col(2), col(3), own,
                  pl.BlockSpec((T, HEAD), lambda h, g: (NG - 1 - g, d_on_col0 + h)),
                  pl.BlockSpec((cg, None, HEAD, HEAD), lambda h, g: (NG - 1 - g, h, 0, 0)),
                  vec, vec, cst, cst],
        out_specs=(own, own, own, own, vec, vec),
        out_shape=(act, act, act, act, vec_shape, vec_shape),
        scratch_shapes=[pltpu.VMEM((HEAD, HEAD), F32)],
        compiler_params=_params(("parallel", "arbitrary")), name=name,
    )(proj, proj, proj, proj, o_pre, d_on, states, lb, out_g, tri, triu)


def _sb_pre(proj, q_g, k_g, *, n_heads, col0, tm, name):
    S = proj.shape[0]
    H = n_heads
    W = H * HEAD

    def body(q_ref, k_ref, v_ref, qg_ref, kg_ref, qh_ref, kh_ref, vh_ref):
        for src, g_ref, dst in ((q_ref, qg_ref, qh_ref), (k_ref, kg_ref, kh_ref)):
            xv = src[...]
            rstd = lax.rsqrt(jnp.mean(xv * xv, axis=-1, keepdims=True) + EPS)
            dst[...] = ((xv * rstd) * g_ref[...]).astype(BF16)
        vh_ref[...] = v_ref[...].astype(BF16)

    def col(group):
        return pl.BlockSpec((tm, HEAD), lambda i, h: (i, col0 + group * H + h))

    vec = pl.BlockSpec((1, HEAD), lambda i, h: (0, 0))
    own = pl.BlockSpec((tm, HEAD), lambda i, h: (i, h))
    act = jax.ShapeDtypeStruct((S, W), BF16)
    return pl.pallas_call(
        body, grid=(S // tm, H), in_specs=[col(0), col(1), col(2), vec, vec],
        out_specs=(own, own, own), out_shape=(act, act, act),
        compiler_params=_params(("parallel", "parallel")), name=name)(proj, proj, proj, q_g, k_g)


def _sb_scores(q, k_blk, scale):
    z = lax.dot_general(q, k_blk, _NT, preferred_element_type=F32) * scale
    e = jnp.exp(-jnp.abs(z))
    sp = jnp.maximum(z, 0.0) + jnp.log1p(e)
    return z, e, sp


def _strict_lower_mask(t):
    r = lax.broadcasted_iota(jnp.int32, (t, t), 0)
    c = lax.broadcasted_iota(jnp.int32, (t, t), 1)
    return c < r


def _sb_fwd(qh, kh, vh, out_g, *, n_heads, tq, name):
    S, W = qh.shape
    H = n_heads
    NQ = S // tq
    scale = HEAD ** -0.5
    u_strict = jnp.tril(jnp.ones((tq, tq), F32), -1).astype(BF16)

    def body(q_ref, k_ref, v_ref, og_ref, u_ref, o_ref, on_ref):
        qi = pl.program_id(1)
        q = q_ref[...]
        u = u_ref[...]

        def block(kb, r_carry, diag):
            rows = pl.ds(pl.multiple_of(kb * tq, tq), tq)
            k_blk = k_ref[rows, :]
            v_blk = v_ref[rows, :]
            z, _, sp = _sb_scores(q, k_blk, scale)
            if diag:
                m = _strict_lower_mask(tq)
                L = jnp.where(m, -sp, 0.0)
            else:
                L = -sp
            C = _split_dot(L, u, 2)
            a = jnp.exp(z - sp + C + r_carry)
            if diag:
                a = jnp.where(m, a, 0.0)
            pv = lax.dot_general(a.astype(BF16), v_blk, (((1,), (0,)), ((), ())),
                                 preferred_element_type=F32)
            return pv, r_carry + (C[:, 0:1] + L[:, 0:1])

        acc0, r0 = block(qi, jnp.zeros((tq, 1), F32), True)

        def cond(st):
            kb, _, _, rmax = st
            return jnp.logical_and(kb >= 0, rmax > SKIP_LOG)

        def step(st):
            kb, acc, r, _ = st
            pv, r2 = block(kb, r, False)
            return kb - 1, acc + pv, r2, jnp.max(r2)

        _, acc, _, _ = lax.while_loop(cond, step, (qi - 1, acc0, r0, jnp.max(r0)))
        o_ref[...] = acc
        rstd = lax.rsqrt(jnp.mean(acc * acc, axis=-1, keepdims=True) + EPS)
        on_ref[...] = ((acc * rstd) * og_ref[...]).astype(BF16)

    blk = pl.BlockSpec((tq, HEAD), lambda h, i: (i, h))
    full = pl.BlockSpec((S, HEAD), lambda h, i: (0, h))
    return pl.pallas_call(
        body, grid=(H, NQ),
        in_specs=[blk, full, full, pl.BlockSpec((1, HEAD), lambda h, i: (0, h)),
                  pl.BlockSpec((tq, tq), lambda h, i: (0, 0))],
        out_specs=(blk, blk),
        out_shape=(jax.ShapeDtypeStruct((S, W), F32), jax.ShapeDtypeStruct((S, W), BF16)),
        compiler_params=_params(("parallel", "arbitrary")), name=name)(qh, kh, vh, out_g, u_strict)


def _sb_bwd(qh, kh, vh, o_pre, d_on, d_on_col0, out_g, *, n_heads, tq, name):
    S, W = qh.shape
    H = n_heads
    NQ = S // tq
    scale = HEAD ** -0.5
    u_strict = jnp.tril(jnp.ones((tq, tq), F32), -1).astype(BF16)
    u_incl = jnp.tril(jnp.ones((tq, tq), F32)).astype(BF16)

    def body(q_ref, k_ref, v_ref, o_ref, dy_ref, og_ref, us_ref, ui_ref,
             dq_ref, dk_ref, dv_ref, dog_ref):
        qi = pl.program_id(1)

        @pl.when(qi == 0)
        def _():
            dk_ref[...] = jnp.zeros_like(dk_ref)
            dv_ref[...] = jnp.zeros_like(dv_ref)
            dog_ref[...] = jnp.zeros_like(dog_ref)

        q = q_ref[...]
        us = us_ref[...]
        ui = ui_ref[...]
        o = o_ref[...]
        dy = dy_ref[...]
        ogv = og_ref[...]
        rstd = lax.rsqrt(jnp.mean(o * o, axis=-1, keepdims=True) + EPS)
        n = o * rstd
        dog_ref[...] += jnp.sum(dy * n, axis=0, keepdims=True)
        dn = dy * ogv
        d_o = rstd * (dn - n * jnp.mean(dn * n, axis=-1, keepdims=True))
        d_ob = d_o.astype(BF16)
        delta = jnp.sum(d_ob.astype(F32) * o, axis=-1, keepdims=True)

        def block(kb, r_carry, g_carry, diag):
            rows = pl.ds(pl.multiple_of(kb * tq, tq), tq)
            k_blk = k_ref[rows, :]
            v_blk = v_ref[rows, :]
            z, e, sp = _sb_scores(q, k_blk, scale)
            if diag:
                m = _strict_lower_mask(tq)
                L = jnp.where(m, -sp, 0.0)
            else:
                L = -sp
            C = _split_dot(L, us, 2)
            a = jnp.exp(z - sp + C + r_carry)
            if diag:
                a = jnp.where(m, a, 0.0)
            ab = a.astype(BF16)
            dA = lax.dot_general(d_ob, v_blk, _NT, preferred_element_type=F32)
            G = ab.astype(F32) * dA
            SI = _split_dot(G, ui, 3)
            P = delta - (g_carry + SI)
            r = 1.0 / (1.0 + e)
            er = e * r
            pos = z >= 0
            dz = G * jnp.where(pos, er, r) - P * jnp.where(pos, r, er)
            if diag:
                dz = jnp.where(m, dz, 0.0)
            dzb = (dz * scale).astype(BF16)
            dq_part = lax.dot_general(dzb, k_blk, (((1,), (0,)), ((), ())), preferred_element_type=F32)
            dk_ref[rows, :] += lax.dot_general(dzb, q, _TN, preferred_element_type=F32)
            dv_ref[rows, :] += lax.dot_general(ab, d_ob, _TN, preferred_element_type=F32)
            return dq_part, r_carry + (C[:, 0:1] + L[:, 0:1]), g_carry + SI[:, 0:1]

        zero = jnp.zeros((tq, 1), F32)
        dq0, r0, g0 = block(qi, zero, zero, True)

        def cond(st):
            kb, _, _, _, rmax = st
            return jnp.logical_and(kb >= 0, rmax > SKIP_LOG)

        def step(st):
            kb, dq, r, g, _ = st
            dq_part, r2, g2 = block(kb, r, g, False)
            return kb - 1, dq + dq_part, r2, g2, jnp.max(r2)

        _, dq, _, _, _ = lax.while_loop(cond, step, (qi - 1, dq0, r0, g0, jnp.max(r0)))
        dq_ref[...] = dq

    blk = pl.BlockSpec((tq, HEAD), lambda h, i: (i, h))
    full = pl.BlockSpec((S, HEAD), lambda h, i: (0, h))
    vec = pl.BlockSpec((1, HEAD), lambda h, i: (0, h))
    cst = pl.BlockSpec((tq, tq), lambda h, i: (0, 0))
    act = jax.ShapeDtypeStruct((S, W), F32)
    return pl.pallas_call(
        body, grid=(H, NQ),
        in_specs=[blk, full, full, blk,
                  pl.BlockSpec((tq, HEAD), lambda h, i: (i, d_on_col0 + h)), vec, cst, cst],
        out_specs=(blk, full, full, vec),
        out_shape=(act, act, act, jax.ShapeDtypeStruct((1, W), F32)),
        compiler_params=_params(("parallel", "arbitrary")), name=name,
    )(qh, kh, vh, o_pre, d_on, out_g, u_strict, u_incl)


def _sb_pre_bwd(proj, dqh, dkh, dvh, q_g, k_g, *, n_heads, col0, tm, name):
    S = proj.shape[0]
    H = n_heads
    W = H * HEAD

    def body(q_ref, k_ref, dqh_ref, dkh_ref, dvh_ref, qg_ref, kg_ref,
             dq_ref, dk_ref, dv_ref, dqg_ref, dkg_ref):
        first = jnp.logical_and(pl.program_id(0) == 0, pl.program_id(1) == 0)

        @pl.when(first)
        def _():
            dqg_ref[...] = jnp.zeros_like(dqg_ref)
            dkg_ref[...] = jnp.zeros_like(dkg_ref)

        for src, dh_ref, g_ref, dst, dg_ref in ((q_ref, dqh_ref, qg_ref, dq_ref, dqg_ref),
                                                (k_ref, dkh_ref, kg_ref, dk_ref, dkg_ref)):
            xv = src[...]
            dh = dh_ref[...]
            rstd = lax.rsqrt(jnp.mean(xv * xv, axis=-1, keepdims=True) + EPS)
            n = xv * rstd
            dg_ref[...] += jnp.sum(dh * n, axis=0, keepdims=True)
            dn = dh * g_ref[...]
            dst[...] = (rstd * (dn - n * jnp.mean(dn * n, axis=-1, keepdims=True))).astype(BF16)
        dv_ref[...] = dvh_ref[...].astype(BF16)

    def col(group):
        return pl.BlockSpec((tm, HEAD), lambda i, h: (i, col0 + group * H + h))

    vec = pl.BlockSpec((1, HEAD), lambda i, h: (0, 0))
    own = pl.BlockSpec((tm, HEAD), lambda i, h: (i, h))
    act = jax.ShapeDtypeStruct((S, W), BF16)
    vec_shape = jax.ShapeDtypeStruct((1, HEAD), F32)
    return pl.pallas_call(
        body, grid=(S // tm, H), in_specs=[col(0), col(1), own, own, own, vec, vec],
        out_specs=(own, own, own, vec, vec), out_shape=(act, act, act, vec_shape, vec_shape),
        compiler_params=_params(("arbitrary", "arbitrary")), name=name,
    )(proj, proj, dqh, dkh, dvh, q_g, k_g)


def _softmax_rows(x_ref, L):
    rows = [x_ref[l:l + 1, :] for l in range(L)]
    mx = rows[0]
    for r in rows[1:]:
        mx = jnp.maximum(mx, r)
    ex = [jnp.exp(r - mx) for r in rows]
    tot = ex[0]
    for e in ex[1:]:
        tot = tot + e
    return [e / tot for e in ex]


def _lb_fwd(logits, *, name):
    L, W = logits.shape

    def body(x_ref, o_ref):
        s = _softmax_rows(x_ref, L)
        run = jnp.zeros((1, W), F32)
        for l in range(L):
            run = run + s[l]
            o_ref[l:l + 1, :] = run - s[0]

    return pl.pallas_call(body, out_shape=jax.ShapeDtypeStruct((L, W), F32), name=name)(logits)


def _lb_bwd(logits, dlb_parts, *, name):
    L, W = logits.shape
    P = dlb_parts.shape[0]

    def body(x_ref, d_ref, o_ref):
        s = _softmax_rows(x_ref, L)
        dlb = []
        for l in range(L):
            t = d_ref[0, l:l + 1, :]
            for q in range(1, P):
                t = t + d_ref[q, l:l + 1, :]
            dlb.append(t)
        ds = [None] * L
        run = jnp.zeros((1, W), F32)
        for j in reversed(range(L)):
            run = run + dlb[j]
            ds[j] = run
        ds[0] = jnp.zeros((1, W), F32)
        inner = jnp.zeros((1, W), F32)
        for j in range(L):
            inner = inner + s[j] * ds[j]
        for j in range(L):
            o_ref[j:j + 1, :] = s[j] * (ds[j] - inner)

    return pl.pallas_call(body, out_shape=jax.ShapeDtypeStruct((L, W), F32), name=name)(logits, dlb_parts)


def _ada_mod(c_all, w_ada, *, nb, name):
    L, D, n = w_ada.shape
    B = c_all.shape[0]

    def body(c_ref, w_ref, o_ref, cond_ref):
        cv = c_ref[...]
        s, _ = _sigmoid_pair(cv)
        cond = cv * s
        cond_ref[...] = cond
        o_ref[...] = _bdot(cond, w_ref[...])

    return pl.pallas_call(
        body, grid=(L, n // nb),
        in_specs=[pl.BlockSpec((B, D), lambda l, j: (0, 0)),
                  pl.BlockSpec((None, D, nb), lambda l, j: (l, 0, j))],
        out_specs=(pl.BlockSpec((None, B, nb), lambda l, j: (l, 0, j)),
                   pl.BlockSpec((B, D), lambda l, j: (0, 0))),
        out_shape=(jax.ShapeDtypeStruct((L, B, n), F32), jax.ShapeDtypeStruct((B, D), F32)),
        compiler_params=_params(("arbitrary", "arbitrary")), name=name)(c_all, w_ada)


def _adam_math(w, g, m, v):
    m2 = ADAM_B1 * m + (1.0 - ADAM_B1) * g
    v2 = ADAM_B2 * v + (1.0 - ADAM_B2) * (g * g)
    m_hat = m2 / (1.0 - ADAM_B1 ** ADAM_STEP)
    v_hat = v2 / (1.0 - ADAM_B2 ** ADAM_STEP)
    delta = -ADAM_LR * (m_hat / (jnp.sqrt(v_hat) + ADAM_EPS) + ADAM_WD * w)
    return delta, m2, v2


def _adamw(w, m, v, gparts, *, tr, name):
    R, C = w.shape
    P = gparts.shape[0]

    def body(w_ref, m_ref, v_ref, gp_ref, g_ref, d_ref, m2_ref, v2_ref):
        g = gp_ref[0].astype(F32)
        for p in range(1, P):
            g = g + gp_ref[p].astype(F32)
        delta, m2, v2 = _adam_math(w_ref[...], g, m_ref[...], v_ref[...])
        g_ref[...] = g
        d_ref[...] = delta
        m2_ref[...] = m2
        v2_ref[...] = v2

    tile = pl.BlockSpec((tr, C), lambda i: (i, 0))
    shp = jax.ShapeDtypeStruct((R, C), F32)
    return pl.pallas_call(
        body, grid=(R // tr,),
        in_specs=[tile, tile, tile, pl.BlockSpec((P, tr, C), lambda i: (0, i, 0))],
        out_specs=(tile, tile, tile, tile), out_shape=(shp, shp, shp, shp),
        compiler_params=_params(("parallel",)), name=name)(w, m, v, gparts)


def _adamw_layers(w, m, v, gparts, *, tr, name):
    L, R, C = w.shape
    P = gparts[0].shape[0]
    nblk = R // tr

    def body(*refs):
        w_ref, m_ref, v_ref = refs[:3]
        gp_refs = refs[3:3 + L]
        g_ref, d_ref, m2_ref, v2_ref = refs[3 + L:]
        layer = pl.program_id(0)
        for t in range(L):
            @pl.when(layer == t)
            def _(t=t):
                g = gp_refs[t][0].astype(F32)
                for q in range(1, P):
                    g = g + gp_refs[t][q].astype(F32)
                delta, m2, v2 = _adam_math(w_ref[...], g, m_ref[...], v_ref[...])
                g_ref[...] = g
                d_ref[...] = delta
                m2_ref[...] = m2
                v2_ref[...] = v2

    def gp_spec(t):
        def index(l, i):
            return (0, jnp.where(l == t, i, jnp.where(l < t, 0, nblk - 1)), 0)
        return pl.BlockSpec((P, tr, C), index)

    tile = pl.BlockSpec((None, tr, C), lambda l, i: (l, i, 0))
    shp = jax.ShapeDtypeStruct((L, R, C), F32)
    return pl.pallas_call(
        body, grid=(L, nblk), in_specs=[tile, tile, tile] + [gp_spec(t) for t in range(L)],
        out_specs=(tile, tile, tile, tile), out_shape=(shp, shp, shp, shp),
        compiler_params=_params(("arbitrary", "arbitrary")), name=name)(w, m, v, *gparts)


def _adamw_ada(w, m, v, cond_t, dmod, *, tr, name):
    L, D, n = w.shape
    Bp = cond_t.shape[1]

    def body(w_ref, m_ref, v_ref, c_ref, dm_ref, g_ref, d_ref, m2_ref, v2_ref):
        g = _bdot(c_ref[...], dm_ref[...])
        delta, m2, v2 = _adam_math(w_ref[...], g, m_ref[...], v_ref[...])
        g_ref[...] = g
        d_ref[...] = delta
        m2_ref[...] = m2
        v2_ref[...] = v2

    tile = pl.BlockSpec((None, tr, n), lambda l, i: (l, i, 0))
    shp = jax.ShapeDtypeStruct((L, D, n), F32)
    return pl.pallas_call(
        body, grid=(L, D // tr),
        in_specs=[tile, tile, tile, pl.BlockSpec((tr, Bp), lambda l, i: (i, 0)),
                  pl.BlockSpec((None, Bp, n), lambda l, i: (l, 0, 0))],
        out_specs=(tile, tile, tile, tile), out_shape=(shp, shp, shp, shp),
        compiler_params=_params(("parallel", "parallel")), name=name)(w, m, v, cond_t, dmod)


def _allgather_small(block, *, name):
    R, C = block.shape

    def body(x_ref, out_ref, send_sems, recv_sems, local_sem):
        x, y, c = lax.axis_index("x"), lax.axis_index("y"), lax.axis_index("c")

        def rows(px, py, pc):
            return out_ref.at[pl.ds((4 * px + 2 * py + pc) * R, R), :]

        mine = pltpu.make_async_copy(x_ref, rows(x, y, c), local_sem)
        mine.start()
        sends = []
        for rel in range(1, N_DEV):
            to = _peer(x, y, c, rel)
            cp = pltpu.make_async_remote_copy(src_ref=x_ref, dst_ref=rows(x, y, c),
                                              send_sem=send_sems.at[rel - 1], recv_sem=recv_sems.at[rel - 1],
                                              device_id=to, device_id_type=MESH)
            cp.start()
            sends.append(cp)
        for rel in range(1, N_DEV):
            frm = _peer(x, y, c, rel)
            pltpu.make_async_remote_copy(src_ref=x_ref, dst_ref=rows(*frm),
                                         send_sem=send_sems.at[rel - 1], recv_sem=recv_sems.at[rel - 1],
                                         device_id=frm, device_id_type=MESH).wait_recv()
        for cp in sends:
            cp.wait_send()
        mine.wait()

    return pl.pallas_call(
        body, out_shape=jax.ShapeDtypeStruct((N_DEV * R, C), block.dtype),
        in_specs=[pl.BlockSpec(memory_space=pltpu.VMEM)],
        out_specs=pl.BlockSpec(memory_space=pltpu.VMEM),
        scratch_shapes=[pltpu.SemaphoreType.DMA((N_DEV - 1,)), pltpu.SemaphoreType.DMA((N_DEV - 1,)),
                        pltpu.SemaphoreType.DMA],
        compiler_params=pltpu.CompilerParams(vmem_limit_bytes=V7X_VMEM_LIMIT), name=name)(block)


def _allgather_hbm(shards, *, name):
    n = len(shards)

    def body(*refs):
        ins = refs[:n]
        outs = refs[n:2 * n]
        send_sems, recv_sems, local_sems = refs[2 * n:]
        x, y, c = lax.axis_index("x"), lax.axis_index("y"), lax.axis_index("c")
        sibling = (x, y, 1 - c)
        chips = [(1 - x, y), (x, 1 - y), (1 - x, 1 - y)]

        def slot(t, px, py, pc):
            return outs[t].at[4 * px + 2 * py + pc]

        def copy(t, k, block, to, src=None):
            return pltpu.make_async_remote_copy(
                src_ref=slot(t, *block) if src is None else src, dst_ref=slot(t, *block),
                send_sem=send_sems.at[t * 7 + k], recv_sem=recv_sems.at[t * 7 + k],
                device_id=to, device_id_type=MESH)

        me = (x, y, c)
        started = []
        mine = []
        for t in range(n):
            cp = pltpu.make_async_copy(ins[t], slot(t, *me), local_sems.at[t])
            cp.start()
            mine.append(cp)
            first = [copy(t, 0, me, sibling, src=ins[t])]
            first += [copy(t, 1 + j, me, (*chip, c), src=ins[t]) for j, chip in enumerate(chips)]
            for cp in first:
                cp.start()
            started += first
        for t in range(n):
            for j, chip in enumerate(chips):
                copy(t, 1 + j, (*chip, c), me).wait_recv()
                fwd = copy(t, 4 + j, (*chip, c), sibling)
                fwd.start()
                started.append(fwd)
        for t in range(n):
            copy(t, 0, sibling, me).wait_recv()
            for j, chip in enumerate(chips):
                copy(t, 4 + j, (*chip, 1 - c), me).wait_recv()
        for cp in started:
            cp.wait_send()
        for cp in mine:
            cp.wait()

    any_spec = pl.BlockSpec(memory_space=pl.ANY)
    return pl.pallas_call(
        body, out_shape=[jax.ShapeDtypeStruct((N_DEV,) + s.shape, s.dtype) for s in shards],
        in_specs=[any_spec] * n, out_specs=[any_spec] * n,
        scratch_shapes=[pltpu.SemaphoreType.DMA((7 * n,)), pltpu.SemaphoreType.DMA((7 * n,)),
                        pltpu.SemaphoreType.DMA((n,))],
        name=name)(*shards)


def _alltoall_hbm(blocks, *, name):
    n = len(blocks)

    def body(*refs):
        ins = refs[:n]
        outs = refs[n:2 * n]
        send_sems, recv_sems, local_sems = refs[2 * n:]
        _exchange(False, ins, outs, send_sems, recv_sems, local_sems, "start")
        _exchange(False, ins, outs, send_sems, recv_sems, local_sems, "wait")

    any_spec = pl.BlockSpec(memory_space=pl.ANY)
    return pl.pallas_call(
        body, out_shape=[jax.ShapeDtypeStruct(b.shape, b.dtype) for b in blocks],
        in_specs=[any_spec] * n, out_specs=[any_spec] * n, scratch_shapes=_exchange_scratch(n),
        name=name)(*blocks)


def _tile(total, want):
    step = 128 if total % 128 == 0 else 8
    best = step
    t = step
    while t <= min(total, want):
        if total % t == 0:
            best = t
        t += step
    return best


def _local_step(x, target, mods, lbs, p, wg, shards=None):
    S, D = x.shape
    L = mods.shape[0]
    W = D // 2
    H = W // HEAD
    F = wg["w_ffn_out"][0].shape[0]
    mesh = shards is not None
    tm = _tile(S, 512)
    tm_big = _tile(S, 1024)
    tm_tn = _tile(S, 2048)
    tm_sw = _tile(S, 128)
    tq = _tile(S, 256)
    cg = max(1, min(8, S // CHUNK))
    nb_out = _tile(D, 1024)
    kb_f = _tile(F, 1408)

    def row(a, l):
        return a[l][None, :]

    def gather_of(l, names):
        if mesh and l < L:
            return (True, [shards[k][l] for k in names])
        return None

    def scatter_of(blocks):
        if mesh and blocks is not None:
            return (False, [b.reshape((N_DEV, -1) + b.shape[-1:]) if b.ndim == 2 else b for b in blocks])
        return None

    saved = []
    xcur = x
    for l in range(L):
        mod = mods[l]
        sh1, sc1, g1, sh2, sc2, g2 = [mod[:, i * D:(i + 1) * D] for i in range(N_MOD)]
        h1 = _modnorm_fwd(xcur, row(p["norm1_g"], l), sc1, sh1, tm=tm, name="norm1_fwd")
        proj, got = _mm_nn(h1, wg["w_in"][l], tm=tm_big, name="proj_fwd", side=gather_of(l + 1, ["w_in"]))
        if got is not None:
            wg["w_in"][l + 1] = got[0]
        lb = lbs[l][None, :]
        o_hg, on_hg, states = _hg_fwd(proj, lb, row(p["hg_out_g"], l), n_heads=H, cg=cg, name="hgrn2_fwd")
        qh, kh, vh = _sb_pre(proj, row(p["sb_q_g"], l), row(p["sb_k_g"], l), n_heads=H, col0=4 * H,
                             tm=tm, name="sb_qknorm_fwd")
        o_sb, on_sb = _sb_fwd(qh, kh, vh, row(p["sb_out_g"], l), n_heads=H, tq=tq, name="sb_fwd")
        o_cat = jnp.concatenate([on_hg, on_sb], axis=1)
        (x1, mixed), _ = _mm_nn(o_cat, wg["w_out"][l], tm=tm_big, nb=nb_out, resid=xcur, gate=g1,
                                name="out_proj_fwd")
        h2 = _modnorm_fwd(x1, row(p["norm2_g"], l), sc2, sh2, tm=tm, name="norm2_fwd")
        u, got = _mm_nn(h2, wg["w_ffn_in"][l], tm=tm_big, name="ffn_in_fwd", side=gather_of(l + 1, ["w_ffn_in"]))
        if got is not None:
            wg["w_ffn_in"][l + 1] = got[0]
        a = _swiglu_fwd(u, tm=tm_sw, name="swiglu_fwd")
        (x2, ffn), got = _mm_nn(a, wg["w_ffn_out"][l], tm=tm, nb=nb_out // 2, resid=x1, gate=g2,
                                name="ffn_out_fwd", side=gather_of(l + 1, ["w_out", "w_ffn_out"]))
        if got is not None:
            wg["w_out"][l + 1] = got[0].reshape(-1, D)
            wg["w_ffn_out"][l + 1] = got[1].reshape(-1, D)
        saved.append(dict(x=xcur, h1=h1, proj=proj, o_hg=o_hg, o_sb=o_sb, states=states, qh=qh, kh=kh, vh=vh,
                          o_cat=o_cat, mixed=mixed, x1=x1, h2=h2, u=u, a=a, ffn=ffn, lb=lb,
                          sc1=sc1, g1=g1, sc2=sc2, g2=g2))
        xcur = x2

    last = saved[-1]
    dx, dffn, dg2, loss = _loss_bwd(xcur, target, last["ffn"], last["g2"], tm=tm, name="loss_bwd")

    big = {k: [None] * L for k in ("w_in", "w_out", "w_ffn_in", "w_ffn_out")}
    small = {k: [None] * L for k in ("norm1_g", "hg_lb", "hg_out_g", "sb_q_g", "sb_k_g", "sb_out_g", "norm2_g")}
    dmods = [None] * L
    kb_d = _tile(D, 1024)
    pending_in = None
    for l in reversed(range(L)):
        sv = saved[l]
        da, got = _mm_nt(dffn, wg["w_ffn_out"][l], tm=tm_big, kb=kb_f, nb=D, name="ffn_out_bwd_x",
                         side=scatter_of(pending_in))
        if got is not None:
            big["w_in"][l + 1] = got[0]
        g_fout, _ = _mm_tn(sv["a"], dffn, tm=tm_tn, kb=kb_f, nb=nb_out, blocked=False, name="ffn_out_bwd_w")
        du = _swiglu_bwd(sv["u"], da, tm=tm_sw, name="swiglu_bwd")
        dh2, got = _mm_nt(du, wg["w_ffn_in"][l], tm=tm_big, kb=D, name="ffn_in_bwd_x",
                          side=scatter_of([g_fout]))
        big["w_ffn_out"][l] = g_fout if got is None else got[0]
        g_fin, _ = _mm_tn(sv["h2"], du, tm=tm_tn, kb=kb_d, nb=wg["w_ffn_in"][l].shape[2], blocked=True,
                          name="ffn_in_bwd_w")
        dx1, dmixed, dg1, dsh2, dsc2, dn2 = _modnorm_bwd(
            sv["x1"], dh2, dx, row(p["norm2_g"], l), sv["sc2"], sv["mixed"], sv["g1"], tm=tm_sw * 2,
            name="norm2_bwd")
        small["norm2_g"][l] = dn2
        d_ocat, _ = _mm_nt(dmixed, wg["w_out"][l], tm=tm_big, kb=nb_out, nb=D, name="out_proj_bwd_x")
        g_out, _ = _mm_tn(sv["o_cat"], dmixed, tm=tm_tn, kb=nb_out, nb=nb_out, blocked=False,
                          name="out_proj_bwd_w")
        dhq, dhf, dhi, dhg, dlb, dhog = _hg_bwd(sv["proj"], sv["o_hg"], d_ocat, 0, sv["states"], sv["lb"],
                                                row(p["hg_out_g"], l), n_heads=H, cg=cg, name="hgrn2_bwd")
        dqh, dkh, dvh, dsog = _sb_bwd(sv["qh"], sv["kh"], sv["vh"], sv["o_sb"], d_ocat, H,
                                      row(p["sb_out_g"], l), n_heads=H, tq=tq, name="sb_bwd")
        dsq, dsk, dsv, dqg, dkg = _sb_pre_bwd(sv["proj"], dqh, dkh, dvh, row(p["sb_q_g"], l),
                                              row(p["sb_k_g"], l), n_heads=H, col0=4 * H, tm=tm,
                                              name="sb_qknorm_bwd")
        small["hg_lb"][l] = dlb
        small["hg_out_g"][l] = dhog
        small["sb_out_g"][l] = dsog
        small["sb_q_g"][l] = dqg
        small["sb_k_g"][l] = dkg
        dproj = jnp.concatenate([dhq, dhf, dhi, dhg, dsq, dsk, dsv], axis=1)
        dh1, got = _mm_nt(dproj, wg["w_in"][l], tm=tm_big, kb=D, name="proj_bwd_x", side=scatter_of([g_fin]))
        big["w_ffn_in"][l] = g_fin if got is None else got[0]
        g_in, got = _mm_tn(sv["h1"], dproj, tm=tm_tn, kb=kb_d, nb=wg["w_in"][l].shape[2], blocked=True,
                           name="proj_bwd_w", side=scatter_of([g_out]))
        big["w_out"][l] = g_out if got is None else got[0]
        pending_in = [g_in]
        big["w_in"][l] = g_in
        if l > 0:
            prev = saved[l - 1]
            dx0, dffn_prev, dg2_prev, dsh1, dsc1, dn1 = _modnorm_bwd(
                sv["x"], dh1, dx1, row(p["norm1_g"], l), sv["sc1"], prev["ffn"], prev["g2"], tm=tm_sw * 2,
                name="norm1_bwd")
        else:
            dx0, dsh1, dsc1, dn1 = _modnorm_bwd(sv["x"], dh1, dx1, row(p["norm1_g"], l), sv["sc1"], None, None,
                                                tm=tm_sw * 2, name="norm1_bwd_first")
            dffn_prev, dg2_prev = None, None
        small["norm1_g"][l] = dn1
        dmods[l] = jnp.concatenate([dsh1, dsc1, dg1, dsh2, dsc2, dg2], axis=1)
        dx, dffn, dg2 = dx0, dffn_prev, dg2_prev
    if mesh:
        big["w_in"][0] = _alltoall_hbm(scatter_of(pending_in)[1], name="scatter_last_grad")[0]
    return loss, dx, big, small, dmods


def kernel(x, c, norm1_g, w_in, hg_lb_logits, hg_out_g, sb_q_g, sb_k_g, sb_out_g, w_out, norm2_g, w_ffn_in, w_ffn_out, w_ada, b_ada, loss_target, m_norm1_g, m_w_in, m_hg_lb_logits, m_hg_out_g, m_sb_q_g, m_sb_k_g, m_sb_out_g, m_w_out, m_norm2_g, m_w_ffn_in, m_w_ffn_out, m_w_ada, m_b_ada, v_norm1_g, v_w_in, v_hg_lb_logits, v_hg_out_g, v_sb_q_g, v_sb_k_g, v_sb_out_g, v_w_out, v_norm2_g, v_w_ffn_in, v_w_ffn_out, v_w_ada, v_b_ada):
    L, D = norm1_g.shape
    S = x.shape[1]
    me = 4 * lax.axis_index("x") + 2 * lax.axis_index("y") + lax.axis_index("c")

    c_all = _allgather_small(jnp.broadcast_to(c, (8, D)), name="gather_c").reshape(N_DEV, 8, D)[:, 0, :]
    n_ada = w_ada.shape[2]
    mod_cols, cond = _ada_mod(c_all, w_ada, nb=_tile(n_ada, 512), name="ada_mod")
    mod_all = _allgather_small(mod_cols.reshape(L * N_DEV, n_ada), name="gather_mod")
    mod_all = mod_all.reshape(N_DEV, L, N_DEV, n_ada)
    mod_mine = lax.dynamic_index_in_dim(mod_all, me, axis=2, keepdims=False)
    mods = jnp.transpose(mod_mine, (1, 0, 2)).reshape(L, 1, N_DEV * n_ada) + b_ada[:, None, :]

    lbs = _lb_fwd(hg_lb_logits, name="lower_bounds_fwd")

    shards = dict(w_in=[w_in[l].astype(BF16) for l in range(L)], w_out=[w_out[l].astype(BF16) for l in range(L)],
                  w_ffn_in=[w_ffn_in[l].astype(BF16) for l in range(L)],
                  w_ffn_out=[w_ffn_out[l].astype(BF16) for l in range(L)])
    g_in, g_out, g_fin, g_fout = _allgather_hbm(
        [shards["w_in"][0], shards["w_out"][0], shards["w_ffn_in"][0], shards["w_ffn_out"][0]],
        name="gather_weights")
    wg = dict(w_in=[g_in] + [None] * (L - 1), w_out=[g_out.reshape(-1, D)] + [None] * (L - 1),
              w_ffn_in=[g_fin] + [None] * (L - 1), w_ffn_out=[g_fout.reshape(-1, D)] + [None] * (L - 1))

    p = dict(norm1_g=norm1_g, hg_out_g=hg_out_g, sb_q_g=sb_q_g, sb_k_g=sb_k_g, sb_out_g=sb_out_g,
             norm2_g=norm2_g)
    loss_part, grad_x, recv, small, dmods = _local_step(x.reshape(S, D), loss_target.reshape(S, D), mods, lbs, p,
                                                        wg, shards)

    dmod = jnp.concatenate(dmods, axis=0)
    pieces = [jnp.concatenate(small[k], axis=0) for k in
              ("norm1_g", "hg_lb", "hg_out_g", "sb_q_g", "sb_k_g", "sb_out_g", "norm2_g")] + [dmod]
    flat = jnp.concatenate([a.reshape(-1) for a in pieces] + [loss_part.reshape(-1)])
    n_flat = flat.shape[0]
    rows = -(-n_flat // 1024) * 8
    flat = jnp.pad(flat, (0, rows * 128 - n_flat)).reshape(rows, 128)
    gathered = _allgather_small(flat, name="gather_small_grads").reshape(N_DEV, rows * 128)

    def take(off, shape):
        size = 1
        for s in shape:
            size *= s
        return gathered[:, off:off + size].reshape((N_DEV,) + tuple(shape)), off + size

    off = 0
    parts = {}
    for k, a in zip(("norm1_g", "hg_lb", "hg_out_g", "sb_q_g", "sb_k_g", "sb_out_g", "norm2_g", "dmod"), pieces):
        parts[k], off = take(off, a.shape)
    loss_parts = gathered[:, off:off + 1]
    loss = jnp.sum(loss_parts)

    def pad8(a):
        return jnp.pad(a, ((0, 0), (0, 8 - a.shape[1]), (0, 0)))

    def small_update(w, m, v, gparts):
        Lw = w.shape[0]
        g, d, m2, v2 = _adamw(pad8(w[None])[0], pad8(m[None])[0], pad8(v[None])[0], pad8(gparts),
                              tr=8, name="adamw_small")
        return g[:Lw], d[:Lw], m2[:Lw], v2[:Lw]

    out = {}
    out["norm1_g"] = small_update(norm1_g, m_norm1_g, v_norm1_g, parts["norm1_g"])
    dlogits = _lb_bwd(hg_lb_logits, parts["hg_lb"], name="lower_bounds_bwd")
    out["hg_lb_logits"] = small_update(hg_lb_logits, m_hg_lb_logits, v_hg_lb_logits, dlogits[None])
    out["hg_out_g"] = small_update(hg_out_g, m_hg_out_g, v_hg_out_g, parts["hg_out_g"])
    out["sb_q_g"] = small_update(sb_q_g, m_sb_q_g, v_sb_q_g, parts["sb_q_g"])
    out["sb_k_g"] = small_update(sb_k_g, m_sb_k_g, v_sb_k_g, parts["sb_k_g"])
    out["sb_out_g"] = small_update(sb_out_g, m_sb_out_g, v_sb_out_g, parts["sb_out_g"])
    out["norm2_g"] = small_update(norm2_g, m_norm2_g, v_norm2_g, parts["norm2_g"])
    out["b_ada"] = small_update(b_ada, m_b_ada, v_b_ada, parts["dmod"])

    dmod_all = parts["dmod"].reshape(N_DEV, L, N_DEV, n_ada)
    dmod_mine = lax.dynamic_index_in_dim(dmod_all, me, axis=2, keepdims=False)
    dmod_mine = jnp.pad(jnp.transpose(dmod_mine, (1, 0, 2)), ((0, 0), (0, 128 - N_DEV), (0, 0)))
    cond_t = jnp.pad(jnp.transpose(cond), ((0, 0), (0, 128 - N_DEV)))
    out["w_ada"] = _adamw_ada(w_ada, m_w_ada, v_w_ada, cond_t, dmod_mine, tr=_tile(D, 256), name="adamw_ada")

    def big_update(w, m, v, recv_l, name):
        return _adamw_layers(w, m, v, recv_l, tr=_tile(w.shape[1], 131072 // w.shape[2]), name=name)

    out["w_in"] = big_update(w_in, m_w_in, v_w_in, recv["w_in"], "adamw_w_in")
    out["w_out"] = big_update(w_out, m_w_out, v_w_out, recv["w_out"], "adamw_w_out")
    out["w_ffn_in"] = big_update(w_ffn_in, m_w_ffn_in, v_w_ffn_in, recv["w_ffn_in"], "adamw_w_ffn_in")
    out["w_ffn_out"] = big_update(w_ffn_out, m_w_ffn_out, v_w_ffn_out, recv["w_ffn_out"], "adamw_w_ffn_out")

    order = ("norm1_g", "w_in", "hg_lb_logits", "hg_out_g", "sb_q_g", "sb_k_g", "sb_out_g", "w_out", "norm2_g",
             "w_ffn_in", "w_ffn_out", "w_ada", "b_ada")
    grads = [out[k][0] for k in order]
    deltas = [out[k][1] for k in order]
    new_m = [out[k][2] for k in order]
    new_v = [out[k][3] for k in order]
    return (loss, grad_x.reshape(1, S, D), *grads, *deltas, *new_m, *new_v)
```

```python
import functools

import jax
import jax.numpy as jnp
from jax import lax
from jax.experimental import pallas as pl
from jax.experimental.pallas import tpu as pltpu

F32 = jnp.float32
BF16 = jnp.bfloat16
MESH = pl.DeviceIdType.MESH

N_DEV = 8
HEAD = 128
CHUNK = 64
N_MOD = 6
EPS = 1e-6
TINY = 1e-30
ADAM_LR = 0.001
ADAM_B1 = 0.9
ADAM_B2 = 0.999
ADAM_EPS = 1e-08
ADAM_WD = 0.01
ADAM_STEP = 10
V7X_VMEM_LIMIT = 56 * 1024 * 1024
SKIP_LOG = -104.0


def _params(sem):
    return pltpu.CompilerParams(dimension_semantics=sem, vmem_limit_bytes=V7X_VMEM_LIMIT)


def _bdot(a, b, dims=(((1,), (0,)), ((), ()))):
    return lax.dot_general(a.astype(BF16), b.astype(BF16), dims, preferred_element_type=F32)


_NT = (((1,), (1,)), ((), ()))
_TN = (((0,), (0,)), ((), ()))


def _sigmoid_pair(x):
    e = jnp.exp(-jnp.abs(x))
    r = 1.0 / (1.0 + e)
    er = e * r
    pos = x >= 0
    return jnp.where(pos, r, er), jnp.where(pos, er, r)


def _split_dot(x, u, parts):
    acc = None
    rem = x
    for _ in range(parts):
        p = rem.astype(BF16)
        rem = rem - p.astype(F32)
        t = lax.dot_general(p, u, (((1,), (0,)), ((), ())), preferred_element_type=F32)
        acc = t if acc is None else acc + t
    return acc


def _split_dot_left(u, x, parts):
    acc = None
    rem = x
    for _ in range(parts):
        p = rem.astype(BF16)
        rem = rem - p.astype(F32)
        t = lax.dot_general(u, p, (((1,), (0,)), ((), ())), preferred_element_type=F32)
        acc = t if acc is None else acc + t
    return acc


def _peer(x, y, c, rel):
    return (x ^ ((rel >> 2) & 1), y ^ ((rel >> 1) & 1), c ^ (rel & 1))


def _exchange(gather, srcs, dsts, send_sems, recv_sems, local_sems, phase):
    x, y, c = lax.axis_index("x"), lax.axis_index("y"), lax.axis_index("c")
    me = 4 * x + 2 * y + c
    for t in range(len(srcs)):
        own = srcs[t] if gather else srcs[t].at[me]
        local = pltpu.make_async_copy(own, dsts[t].at[me], local_sems.at[t])
        if phase == "start":
            local.start()
        for rel in range(1, N_DEV):
            px, py, pc = _peer(x, y, c, rel)
            pid = 4 * px + 2 * py + pc
            k = t * (N_DEV - 1) + rel - 1
            if phase == "start":
                pltpu.make_async_remote_copy(
                    src_ref=srcs[t] if gather else srcs[t].at[pid], dst_ref=dsts[t].at[me],
                    send_sem=send_sems.at[k], recv_sem=recv_sems.at[k],
                    device_id=(px, py, pc), device_id_type=MESH).start()
            else:
                cp = pltpu.make_async_remote_copy(
                    src_ref=own, dst_ref=dsts[t].at[pid], send_sem=send_sems.at[k], recv_sem=recv_sems.at[k],
                    device_id=(px, py, pc), device_id_type=MESH)
                cp.wait_recv()
                cp.wait_send()
        if phase == "wait":
            local.wait()


def _exchange_scratch(n):
    return [pltpu.SemaphoreType.DMA(((N_DEV - 1) * n,)), pltpu.SemaphoreType.DMA(((N_DEV - 1) * n,)),
            pltpu.SemaphoreType.DMA((n,))]


def _pcall(body, *, grid, in_specs, out_specs, out_shape, scratch_shapes=(), semantics, name, args, side=None):
    single = not isinstance(out_shape, (tuple, list))
    if single:
        out_specs, out_shape = [out_specs], [out_shape]
    in_specs, out_specs, out_shape = list(in_specs), list(out_specs), list(out_shape)
    scratch_shapes = list(scratch_shapes)
    n_in, n_out, n_scr = len(in_specs), len(out_specs), len(scratch_shapes)
    if side is None:
        res = pl.pallas_call(body, grid=grid, in_specs=in_specs, out_specs=out_specs, out_shape=out_shape,
                             scratch_shapes=scratch_shapes, compiler_params=_params(semantics), name=name)(*args)
        return (res[0] if single else tuple(res)), None
    gather, srcs = side
    n = len(srcs)

    def full(*refs):
        ins = refs[:n_in]
        s_in = refs[n_in:n_in + n]
        outs = refs[n_in + n:n_in + n + n_out]
        s_out = refs[n_in + n + n_out:n_in + 2 * n + n_out]
        scr = refs[n_in + 2 * n + n_out:n_in + 2 * n + n_out + n_scr]
        send_sems, recv_sems, local_sems = refs[n_in + 2 * n + n_out + n_scr:]
        first = pl.program_id(0) == 0
        last = pl.program_id(0) == grid[0] - 1
        for ax in range(1, len(grid)):
            first = jnp.logical_and(first, pl.program_id(ax) == 0)
            last = jnp.logical_and(last, pl.program_id(ax) == grid[ax] - 1)

        @pl.when(first)
        def _():
            _exchange(gather, s_in, s_out, send_sems, recv_sems, local_sems, "start")

        body(*ins, *outs, *scr)

        @pl.when(last)
        def _():
            _exchange(gather, s_in, s_out, send_sems, recv_sems, local_sems, "wait")

    any_spec = pl.BlockSpec(memory_space=pl.ANY)
    s_shapes = [jax.ShapeDtypeStruct(((N_DEV,) + s.shape) if gather else s.shape, s.dtype) for s in srcs]
    res = pl.pallas_call(full, grid=grid, in_specs=in_specs + [any_spec] * n,
                         out_specs=out_specs + [any_spec] * n, out_shape=out_shape + s_shapes,
                         scratch_shapes=scratch_shapes + _exchange_scratch(n),
                         compiler_params=_params(("arbitrary",) * len(grid)), name=name)(*args, *srcs)
    main = res[:n_out]
    return (main[0] if single else tuple(main)), list(res[n_out:])


def _mm_nn(a, b, *, tm, nb=None, out_dtype=F32, resid=None, gate=None, name, side=None):
    M, K = a.shape
    if b.ndim == 3:
        NB, _, n = b.shape
        b_spec = pl.BlockSpec((None, K, n), lambda j, i: (j, 0, 0))
    else:
        n = nb
        NB = b.shape[1] // nb
        b_spec = pl.BlockSpec((K, n), lambda j, i: (0, j))
    N = NB * n
    epi = resid is not None

    def body(*refs):
        if epi:
            a_ref, b_ref, r_ref, g_ref, o_ref, acc_ref = refs
        else:
            a_ref, b_ref, o_ref = refs
        acc = jnp.dot(a_ref[...], b_ref[...], preferred_element_type=F32)
        if epi:
            o_ref[...] = r_ref[...] + g_ref[...] * acc
            acc_ref[...] = acc.astype(BF16)
        else:
            o_ref[...] = acc.astype(out_dtype)

    in_specs = [pl.BlockSpec((tm, K), lambda j, i: (i, 0)), b_spec]
    args = [a, b]
    o_spec = pl.BlockSpec((tm, n), lambda j, i: (i, j))
    if epi:
        in_specs += [pl.BlockSpec((tm, n), lambda j, i: (i, j)), pl.BlockSpec((1, n), lambda j, i: (0, j))]
        args += [resid, gate]
        out_shape = (jax.ShapeDtypeStruct((M, N), F32), jax.ShapeDtypeStruct((M, N), BF16))
        out_specs = (o_spec, o_spec)
    else:
        out_shape = jax.ShapeDtypeStruct((M, N), out_dtype)
        out_specs = o_spec
    return _pcall(body, grid=(NB, M // tm), in_specs=in_specs, out_specs=out_specs, out_shape=out_shape,
                  semantics=("parallel", "parallel"), name=name, args=args, side=side)


def _halves(dy, dy2, blk_rows, blk_cols, nblocks, row_of, col_of, last_row=None):
    if dy2 is None:
        return [pl.BlockSpec((blk_rows, blk_cols), lambda *g: (row_of(*g), col_of(*g)))], None
    half = nblocks // 2

    def left(*g):
        r, c = row_of(*g), col_of(*g)
        if last_row is None:
            return (r, jnp.minimum(c, half - 1))
        return (jnp.where(c < half, r, last_row), jnp.minimum(c, half - 1))

    def right(*g):
        r, c = row_of(*g), col_of(*g)
        if last_row is None:
            return (r, jnp.maximum(c - half, 0))
        return (jnp.where(c >= half, r, 0), jnp.maximum(c - half, 0))

    return [pl.BlockSpec((blk_rows, blk_cols), left), pl.BlockSpec((blk_rows, blk_cols), right)], half


def _mm_nt(dy, w, *, tm, kb, nb=None, out_dtype=F32, name, side=None, dy2=None):
    M = dy.shape[0]
    N = dy.shape[1] * (1 if dy2 is None else 2)
    if w.ndim == 3:
        NB, Kt, n = w.shape
        w_spec = pl.BlockSpec((None, kb, n), lambda i, k, j: (j, k, 0))
    else:
        Kt = w.shape[0]
        n = nb
        NB = N // nb
        w_spec = pl.BlockSpec((kb, n), lambda i, k, j: (k, j))
    KB = Kt // kb
    dy_specs, half = _halves(dy, dy2, tm, n, NB, lambda i, k, j: i, lambda i, k, j: j)
    n_dy = len(dy_specs)

    def body(*refs):
        dy_refs = refs[:n_dy]
        w_ref, o_ref = refs[n_dy:n_dy + 2]
        acc = refs[n_dy + 2:]
        j = pl.program_id(2)

        def use(dy_ref):
            part = lax.dot_general(dy_ref[...], w_ref[...], _NT, preferred_element_type=F32)
            if NB == 1:
                o_ref[...] = part.astype(out_dtype)
                return
            acc_ref, = acc

            @pl.when(j == 0)
            def _():
                acc_ref[...] = part

            @pl.when(jnp.logical_and(j > 0, j < NB - 1))
            def _():
                acc_ref[...] += part

            @pl.when(j == NB - 1)
            def _():
                o_ref[...] = (acc_ref[...] + part).astype(out_dtype)

        if half is None:
            use(dy_refs[0])
        else:
            pl.when(j < half)(lambda: use(dy_refs[0]))
            pl.when(j >= half)(lambda: use(dy_refs[1]))

    return _pcall(
        body, grid=(M // tm, KB, NB),
        in_specs=dy_specs + [w_spec],
        out_specs=pl.BlockSpec((tm, kb), lambda i, k, j: (i, k)),
        out_shape=jax.ShapeDtypeStruct((M, Kt), out_dtype),
        scratch_shapes=[] if NB == 1 else [pltpu.VMEM((tm, kb), F32)],
        semantics=("parallel", "parallel", "arbitrary"), name=name,
        args=(dy, w) if dy2 is None else (dy, dy2, w), side=side)


def _mm_tn(x, dy, *, tm, kb, nb, blocked, name, side=None, dy2=None):
    M, K = x.shape
    N = dy.shape[1] * (1 if dy2 is None else 2)
    KB, NB, MB = K // kb, N // nb, M // tm
    dy_specs, half = _halves(dy, dy2, tm, nb, NB, lambda k, n, m: m, lambda k, n, m: n, last_row=MB - 1)
    n_dy = len(dy_specs)

    def body(*refs):
        x_ref = refs[0]
        dy_refs = refs[1:1 + n_dy]
        o_ref = refs[1 + n_dy]
        acc = refs[2 + n_dy:]
        m = pl.program_id(2)

        def use(dy_ref):
            part = lax.dot_general(x_ref[...], dy_ref[...], _TN, preferred_element_type=F32)
            if MB == 1:
                o_ref[...] = part.astype(BF16)
                return
            acc_ref, = acc

            @pl.when(m == 0)
            def _():
                acc_ref[...] = part

            @pl.when(jnp.logical_and(m > 0, m < MB - 1))
            def _():
                acc_ref[...] += part

            @pl.when(m == MB - 1)
            def _():
                o_ref[...] = (acc_ref[...] + part).astype(BF16)

        if half is None:
            use(dy_refs[0])
        else:
            nblk = pl.program_id(1)
            pl.when(nblk < half)(lambda: use(dy_refs[0]))
            pl.when(nblk >= half)(lambda: use(dy_refs[1]))

    if blocked:
        out_shape = jax.ShapeDtypeStruct((NB, K, nb), BF16)
        o_spec = pl.BlockSpec((None, kb, nb), lambda k, n, m: (n, k, 0))
    else:
        out_shape = jax.ShapeDtypeStruct((K, N), BF16)
        o_spec = pl.BlockSpec((kb, nb), lambda k, n, m: (k, n))
    return _pcall(
        body, grid=(KB, NB, MB),
        in_specs=[pl.BlockSpec((tm, kb), lambda k, n, m: (m, k))] + dy_specs,
        out_specs=o_spec, out_shape=out_shape,
        scratch_shapes=[] if MB == 1 else [pltpu.VMEM((kb, nb), F32)],
        semantics=("parallel", "parallel", "arbitrary"), name=name,
        args=(x, dy) if dy2 is None else (x, dy, dy2), side=side)


def _ffn_in_fwd(h, w, *, tm, nb, name, side=None):
    M, K = h.shape
    if w.ndim == 3:
        J, _, n = w.shape
        half = J // 2
        specs = [pl.BlockSpec((None, K, n), lambda j, i: (j, 0, 0)),
                 pl.BlockSpec((None, K, n), lambda j, i: (j + half, 0, 0))]
    else:
        n = nb
        half = w.shape[1] // (2 * nb)
        specs = [pl.BlockSpec((K, n), lambda j, i: (0, j)), pl.BlockSpec((K, n), lambda j, i: (0, j + half))]
    F = half * n

    def body(h_ref, wg_ref, wu_ref, gate_ref, up_ref, act_ref):
        hv = h_ref[...]
        gate = jnp.dot(hv, wg_ref[...], preferred_element_type=F32)
        up = jnp.dot(hv, wu_ref[...], preferred_element_type=F32)
        s, _ = _sigmoid_pair(gate)
        gate_ref[...] = gate
        up_ref[...] = up
        act_ref[...] = (gate * s * up).astype(BF16)

    o_spec = pl.BlockSpec((tm, n), lambda j, i: (i, j))
    f32 = jax.ShapeDtypeStruct((M, F), F32)
    return _pcall(
        body, grid=(half, M // tm), in_specs=[pl.BlockSpec((tm, K), lambda j, i: (i, 0))] + specs,
        out_specs=(o_spec, o_spec, o_spec), out_shape=(f32, f32, jax.ShapeDtypeStruct((M, F), BF16)),
        semantics=("parallel", "parallel"), name=name, args=(h, w, w), side=side)


def _ffn_out_bwd_x(dy, w, gate, up, *, tm, kb, name, side=None):
    M, D = dy.shape
    F = w.shape[0]

    def body(dy_ref, w_ref, g_ref, u_ref, dg_ref, du_ref):
        da = lax.dot_general(dy_ref[...], w_ref[...], _NT, preferred_element_type=F32)
        gate = g_ref[...]
        s, ns = _sigmoid_pair(gate)
        dg_ref[...] = (da * u_ref[...] * (s * (1.0 + gate * ns))).astype(BF16)
        du_ref[...] = (da * (gate * s)).astype(BF16)

    tile = pl.BlockSpec((tm, kb), lambda i, k: (i, k))
    act = jax.ShapeDtypeStruct((M, F), BF16)
    return _pcall(
        body, grid=(M // tm, F // kb),
        in_specs=[pl.BlockSpec((tm, D), lambda i, k: (i, 0)), pl.BlockSpec((kb, D), lambda i, k: (k, 0)), tile, tile],
        out_specs=(tile, tile), out_shape=(act, act),
        semantics=("parallel", "parallel"), name=name, args=(dy, w, gate, up), side=side)


def _modnorm_fwd(x, gain, sc, sh, *, tm, name):
    S, D = x.shape

    def body(x_ref, g_ref, sc_ref, sh_ref, h_ref):
        xv = x_ref[...]
        rstd = lax.rsqrt(jnp.mean(xv * xv, axis=-1, keepdims=True) + EPS)
        y = (xv * rstd) * g_ref[...]
        h_ref[...] = (y * (1.0 + sc_ref[...]) + sh_ref[...]).astype(BF16)

    row = pl.BlockSpec((1, D), lambda i: (0, 0))
    return pl.pallas_call(
        body, grid=(S // tm,),
        in_specs=[pl.BlockSpec((tm, D), lambda i: (i, 0)), row, row, row],
        out_specs=pl.BlockSpec((tm, D), lambda i: (i, 0)),
        out_shape=jax.ShapeDtypeStruct((S, D), BF16),
        compiler_params=_params(("parallel",)), name=name)(x, gain, sc, sh)


def _modnorm_bwd(x, dh, dres, gain, sc, branch, gate, *, tm, name):
    S, D = x.shape
    has_prev = branch is not None

    def body(*refs):
        if has_prev:
            (x_ref, dh_ref, dr_ref, g_ref, sc_ref, br_ref, gt_ref,
             dx_ref, dbr_ref, dgt_ref, dsh_ref, dsc_ref, dgn_ref) = refs
        else:
            (x_ref, dh_ref, dr_ref, g_ref, sc_ref,
             dx_ref, dsh_ref, dsc_ref, dgn_ref) = refs
        i = pl.program_id(0)
        xv = x_ref[...]
        dh_v = dh_ref[...]
        gv = g_ref[...]
        scale1 = 1.0 + sc_ref[...]
        rstd = lax.rsqrt(jnp.mean(xv * xv, axis=-1, keepdims=True) + EPS)
        n = xv * rstd
        dn = dh_v * (gv * scale1)
        dx = rstd * (dn - n * jnp.mean(dn * n, axis=-1, keepdims=True)) + dr_ref[...]
        dx_ref[...] = dx
        dhn = dh_v * n
        p_sh = jnp.sum(dh_v, axis=0, keepdims=True)
        p_sc = jnp.sum(dhn, axis=0, keepdims=True) * gv
        p_gn = jnp.sum(dhn, axis=0, keepdims=True) * scale1
        if has_prev:
            dbr_ref[...] = (gt_ref[...] * dx).astype(BF16)
            p_gt = jnp.sum(dx * br_ref[...].astype(F32), axis=0, keepdims=True)

        @pl.when(i == 0)
        def _():
            dsh_ref[...] = p_sh
            dsc_ref[...] = p_sc
            dgn_ref[...] = p_gn
            if has_prev:
                dgt_ref[...] = p_gt

        @pl.when(i > 0)
        def _():
            dsh_ref[...] += p_sh
            dsc_ref[...] += p_sc
            dgn_ref[...] += p_gn
            if has_prev:
                dgt_ref[...] += p_gt

    tile = pl.BlockSpec((tm, D), lambda i: (i, 0))
    row = pl.BlockSpec((1, D), lambda i: (0, 0))
    row_shape = jax.ShapeDtypeStruct((1, D), F32)
    if has_prev:
        in_specs = [tile, tile, tile, row, row, tile, row]
        args = (x, dh, dres, gain, sc, branch, gate)
        out_specs = (tile, tile, row, row, row, row)
        out_shape = (jax.ShapeDtypeStruct((S, D), F32), jax.ShapeDtypeStruct((S, D), BF16),
                     row_shape, row_shape, row_shape, row_shape)
    else:
        in_specs = [tile, tile, tile, row, row]
        args = (x, dh, dres, gain, sc)
        out_specs = (tile, row, row, row)
        out_shape = (jax.ShapeDtypeStruct((S, D), F32), row_shape, row_shape, row_shape)
    return pl.pallas_call(body, grid=(S // tm,), in_specs=in_specs, out_specs=out_specs,
                          out_shape=out_shape, compiler_params=_params(("arbitrary",)),
                          name=name)(*args)


def _loss_bwd(y, target, branch, gate, *, tm, name):
    S, D = y.shape
    nsteps = S // tm

    def body(y_ref, t_ref, br_ref, gt_ref, dy_ref, dbr_ref, dgt_ref, loss_ref, col_ref):
        i = pl.program_id(0)
        diff = y_ref[...] - t_ref[...]
        dy = diff * (1.0 / D)
        dy_ref[...] = dy
        dbr_ref[...] = (gt_ref[...] * dy).astype(BF16)
        p_gt = jnp.sum(dy * br_ref[...].astype(F32), axis=0, keepdims=True)
        p_col = jnp.sum(diff * diff, axis=0, keepdims=True)

        @pl.when(i == 0)
        def _():
            dgt_ref[...] = p_gt
            col_ref[...] = p_col

        @pl.when(i > 0)
        def _():
            dgt_ref[...] += p_gt
            col_ref[...] += p_col

        @pl.when(i == nsteps - 1)
        def _():
            tot = jnp.sum(col_ref[...], axis=-1, keepdims=True) * (0.5 / D)
            loss_ref[...] = jnp.broadcast_to(tot, (1, 128))

    tile = pl.BlockSpec((tm, D), lambda i: (i, 0))
    row = pl.BlockSpec((1, D), lambda i: (0, 0))
    return pl.pallas_call(
        body, grid=(nsteps,), in_specs=[tile, tile, tile, row],
        out_specs=(tile, tile, row, pl.BlockSpec((1, 128), lambda i: (0, 0))),
        out_shape=(jax.ShapeDtypeStruct((S, D), F32), jax.ShapeDtypeStruct((S, D), BF16),
                   jax.ShapeDtypeStruct((1, D), F32), jax.ShapeDtypeStruct((1, 128), F32)),
        scratch_shapes=[pltpu.VMEM((1, D), F32)],
        compiler_params=_params(("arbitrary",)), name=name)(y, target, branch, gate)


def _hg_chunk(q, fl, lbv, tri):
    C = q.shape[0]
    sq, nsq = _sigmoid_pair(q)
    qa = q * sq
    sig, nsig = _sigmoid_pair(fl)
    one_lb = 1.0 - lbv
    f = lbv + one_lb * sig
    fc = jnp.maximum(f, TINY)
    lf = jnp.log(fc)
    k = one_lb * nsig
    b = _split_dot_left(tri, lf, 3)
    row = lax.broadcasted_iota(jnp.int32, b.shape, 0)
    bm = jnp.sum(jnp.where(row == C // 2 - 1, b, 0.0), axis=0, keepdims=True)
    bl = jnp.sum(jnp.where(row == C - 1, b, 0.0), axis=0, keepdims=True)
    eb = jnp.exp(b)
    ebm = jnp.exp(b - bm)
    enbm = jnp.exp(bm - b)
    ebl = jnp.exp(bl - b)
    ebL = jnp.exp(bl)

    def operand(t):
        return t.astype(BF16).astype(F32)

    return dict(sq=sq, nsq=nsq, qa=qa, sig=sig, nsig=nsig, one_lb=one_lb, f=f, fc=fc, k=k,
                eb=eb, ebm=ebm, enbm=enbm, ebl=ebl, ebL=ebL,
                Qm=operand(qa * ebm), Km=operand(k * enbm), Qb=operand(qa * eb), Kh=operand(k * ebl), row=row)


def _causal_incl(C):
    r = lax.broadcasted_iota(jnp.int32, (C, C), 0)
    c = lax.broadcasted_iota(jnp.int32, (C, C), 1)
    return r >= c


def _hg_fwd(proj, lb, out_g, *, n_heads, cg, name, side=None):
    S = proj.shape[0]
    H = n_heads
    W = H * HEAD
    T = cg * CHUNK
    NG = S // T
    tri = jnp.tril(jnp.ones((CHUNK, CHUNK), F32)).astype(BF16)

    def body(q_ref, f_ref, v_ref, g_ref, lb_ref, og_ref, tri_ref, o_ref, on_ref, st_ref, s_scr):
        @pl.when(pl.program_id(1) == 0)
        def _():
            s_scr[...] = jnp.zeros_like(s_scr)

        lbv = lb_ref[...]
        ogv = og_ref[...]
        triv = tri_ref[...]
        mask = _causal_incl(CHUNK)
        for c in range(cg):
            rows = pl.ds(c * CHUNK, CHUNK)
            v = v_ref[rows, :]
            gg = g_ref[rows, :]
            cm = _hg_chunk(q_ref[rows, :], f_ref[rows, :], lbv, triv)
            s0 = s_scr[...]
            st_ref[c] = s0
            A = jnp.where(mask, _bdot(cm["Qm"], cm["Km"], _NT), 0.0)
            o = _bdot(A, v) + _bdot(cm["Qb"], s0, _NT)
            s_scr[...] = s0 * cm["ebL"] + _bdot(v, cm["Kh"], _TN)
            o_ref[rows, :] = o
            rstd = lax.rsqrt(jnp.mean(o * o, axis=-1, keepdims=True) + EPS)
            sg, _ = _sigmoid_pair(gg)
            on_ref[rows, :] = (((o * rstd) * ogv) * (gg * sg)).astype(BF16)

    def col(group):
        return pl.BlockSpec((T, HEAD), lambda h, g: (g, group * H + h))

    vec = pl.BlockSpec((1, HEAD), lambda h, g: (0, h))
    return _pcall(
        body, grid=(H, NG),
        in_specs=[col(0), col(1), col(2), col(3), vec, vec,
                  pl.BlockSpec((CHUNK, CHUNK), lambda h, g: (0, 0))],
        out_specs=(pl.BlockSpec((T, HEAD), lambda h, g: (g, h)),
                   pl.BlockSpec((T, HEAD), lambda h, g: (g, h)),
                   pl.BlockSpec((cg, None, HEAD, HEAD), lambda h, g: (g, h, 0, 0))),
        out_shape=(jax.ShapeDtypeStruct((S, W), F32), jax.ShapeDtypeStruct((S, W), BF16),
                   jax.ShapeDtypeStruct((S // CHUNK, H, HEAD, HEAD), F32)),
        scratch_shapes=[pltpu.VMEM((HEAD, HEAD), F32)],
        semantics=("parallel", "arbitrary"), name=name,
        args=(proj, proj, proj, proj, lb, out_g, tri), side=side)


def _hg_bwd(proj, o_pre, d_on, d_on_col0, states, lb, out_g, *, n_heads, cg, name, side=None):
    S = proj.shape[0]
    H = n_heads
    W = H * HEAD
    T = cg * CHUNK
    NG = S // T
    tri = jnp.tril(jnp.ones((CHUNK, CHUNK), F32)).astype(BF16)
    triu = jnp.triu(jnp.ones((CHUNK, CHUNK), F32)).astype(BF16)

    def body(q_ref, f_ref, v_ref, g_ref, o_ref, dy_ref, st_ref, lb_ref, og_ref, tri_ref, triu_ref,
             dq_ref, df_ref, di_ref, dg_ref, dlb_ref, dog_ref, ds_scr):
        gstep = pl.program_id(1)

        @pl.when(gstep == 0)
        def _():
            ds_scr[...] = jnp.zeros_like(ds_scr)
            dlb_ref[...] = jnp.zeros_like(dlb_ref)
            dog_ref[...] = jnp.zeros_like(dog_ref)

        lbv = lb_ref[...]
        ogv = og_ref[...]
        triv = tri_ref[...]
        triuv = triu_ref[...]
        mask = _causal_incl(CHUNK)
        dlb_acc = jnp.zeros((1, HEAD), F32)
        dog_acc = jnp.zeros((1, HEAD), F32)
        for c in reversed(range(cg)):
            rows = pl.ds(c * CHUNK, CHUNK)
            q = q_ref[rows, :]
            v = v_ref[rows, :]
            gg = g_ref[rows, :]
            o = o_ref[rows, :]
            dy = dy_ref[rows, :]
            cm = _hg_chunk(q, f_ref[rows, :], lbv, triv)
            s0 = st_ref[c]
            ds1 = ds_scr[...]
            rstd = lax.rsqrt(jnp.mean(o * o, axis=-1, keepdims=True) + EPS)
            n = o * rstd
            sg, nsg = _sigmoid_pair(gg)
            silu_g = gg * sg
            dyn = dy * n
            dog_acc = dog_acc + jnp.sum(dyn * silu_g, axis=0, keepdims=True)
            dg_ref[rows, :] = (dyn * ogv * (sg * (1.0 + gg * nsg))).astype(BF16)
            dn = dy * (ogv * silu_g)
            d_o = rstd * (dn - n * jnp.mean(dn * n, axis=-1, keepdims=True))
            A = jnp.where(mask, _bdot(cm["Qm"], cm["Km"], _NT), 0.0)
            dA = jnp.where(mask, _bdot(d_o, v, _NT), 0.0)
            dV = _bdot(A, d_o, _TN) + _bdot(cm["Kh"], ds1, _NT)
            dQm = _bdot(dA, cm["Km"])
            dKm = _bdot(dA, cm["Qm"], _TN)
            dQb = _bdot(d_o, s0)
            dKh = _bdot(v, ds1)
            ds_scr[...] = ds1 * cm["ebL"] + _bdot(d_o, cm["Qb"], _TN)
            kh_term = dKh * cm["Kh"]
            db = dQm * cm["Qm"] - dKm * cm["Km"] + dQb * cm["Qb"] - kh_term
            dbl = (jnp.sum(kh_term, axis=0, keepdims=True)
                   + cm["ebL"] * jnp.sum(ds1 * s0, axis=0, keepdims=True))
            db = db + jnp.where(cm["row"] == CHUNK - 1, dbl, 0.0)
            dlf = _split_dot_left(triuv, db, 3)
            dqa = dQm * cm["ebm"] + dQb * cm["eb"]
            dq_ref[rows, :] = (dqa * (cm["sq"] * (1.0 + q * cm["nsq"]))).astype(BF16)
            dk = dKm * cm["enbm"] + dKh * cm["ebl"]
            dfc = jnp.where(cm["f"] > TINY, dlf / cm["fc"], 0.0)
            t = dfc - dk
            df_ref[rows, :] = (t * (cm["one_lb"] * cm["sig"] * cm["nsig"])).astype(BF16)
            dlb_acc = dlb_acc + jnp.sum(t * cm["nsig"], axis=0, keepdims=True)
            di_ref[rows, :] = dV.astype(BF16)
        dlb_ref[...] += dlb_acc
        dog_ref[...] += dog_acc

    def col(group):
        return pl.BlockSpec((T, HEAD), lambda h, g: (NG - 1 - g, group * H + h))

    own = pl.BlockSpec((T, HEAD), lambda h, g: (NG - 1 - g, h))
    vec = pl.BlockSpec((1, HEAD), lambda h, g: (0, h))
    cst = pl.BlockSpec((CHUNK, CHUNK), lambda h, g: (0, 0))
    act = jax.ShapeDtypeStruct((S, W), BF16)
    vec_shape = jax.ShapeDtypeStruct((1, W), F32)
    return _pcall(
        body, grid=(H, NG),
        in_specs=[col(0), col(1), col(2), col(3), own,
                  pl.BlockSpec((T, HEAD), lambda h, g: (NG - 1 - g, d_on_col0 + h)),
                  pl.BlockSpec((cg, None, HEAD, HEAD), lambda h, g: (NG - 1 - g, h, 0, 0)),
                  vec, vec, cst, cst],
        out_specs=(own, own, own, own, vec, vec),
        out_shape=(act, act, act, act, vec_shape, vec_shape),
        scratch_shapes=[pltpu.VMEM((HEAD, HEAD), F32)],
        semantics=("parallel", "arbitrary"), name=name,
        args=(proj, proj, proj, proj, o_pre, d_on, states, lb, out_g, tri, triu), side=side)


def _sb_pre(proj, q_g, k_g, *, n_heads, col0, tm, name):
    S = proj.shape[0]
    H = n_heads
    W = H * HEAD

    def body(q_ref, k_ref, v_ref, qg_ref, kg_ref, qh_ref, kh_ref, vh_ref):
        for src, g_ref, dst in ((q_ref, qg_ref, qh_ref), (k_ref, kg_ref, kh_ref)):
            xv = src[...]
            rstd = lax.rsqrt(jnp.mean(xv * xv, axis=-1, keepdims=True) + EPS)
            dst[...] = ((xv * rstd) * g_ref[...]).astype(BF16)
        vh_ref[...] = v_ref[...].astype(BF16)

    def col(group):
        return pl.BlockSpec((tm, HEAD), lambda i, h: (i, col0 + group * H + h))

    vec = pl.BlockSpec((1, HEAD), lambda i, h: (0, 0))
    own = pl.BlockSpec((tm, HEAD), lambda i, h: (i, h))
    act = jax.ShapeDtypeStruct((S, W), BF16)
    return pl.pallas_call(
        body, grid=(S // tm, H), in_specs=[col(0), col(1), col(2), vec, vec],
        out_specs=(own, own, own), out_shape=(act, act, act),
        compiler_params=_params(("parallel", "parallel")), name=name)(proj, proj, proj, q_g, k_g)


def _sb_scores(q, k_blk, scale):
    z = lax.dot_general(q, k_blk, _NT, preferred_element_type=F32) * scale
    e = jnp.exp(-jnp.abs(z))
    sp = jnp.maximum(z, 0.0) + jnp.log(1.0 + e)
    return z, e, sp


def _heads_per_step(n_heads):
    return 2 if n_heads % 2 == 0 else 1


def _max_all(values):
    m = jnp.max(values[0])
    for v in values[1:]:
        m = jnp.maximum(m, jnp.max(v))
    return m


def _strict_lower_mask(t):
    r = lax.broadcasted_iota(jnp.int32, (t, t), 0)
    c = lax.broadcasted_iota(jnp.int32, (t, t), 1)
    return c < r


def _sb_fwd(qh, kh, vh, out_g, *, n_heads, tq, name, side=None):
    S, W = qh.shape
    HP = _heads_per_step(n_heads)
    WP = HP * HEAD
    NQ = S // tq
    scale = HEAD ** -0.5
    u_strict = jnp.tril(jnp.ones((tq, tq), F32), -1).astype(BF16)

    def body(q_ref, k_ref, v_ref, og_ref, u_ref, o_ref, on_ref):
        qi = pl.program_id(1)
        u = u_ref[...]
        heads = [slice(hh * HEAD, (hh + 1) * HEAD) for hh in range(HP)]
        qs = [q_ref[:, cols] for cols in heads]

        def block(kb, r_carry, diag):
            rows = pl.ds(pl.multiple_of(kb * tq, tq), tq)
            pvs, rs = [], []
            for hh, cols in enumerate(heads):
                k_blk = k_ref[rows, cols]
                v_blk = v_ref[rows, cols]
                z, _, sp = _sb_scores(qs[hh], k_blk, scale)
                if diag:
                    m = _strict_lower_mask(tq)
                    L = jnp.where(m, -sp, 0.0)
                else:
                    L = -sp
                C = _split_dot(L, u, 2)
                a = jnp.exp(z - sp + C + r_carry[hh])
                if diag:
                    a = jnp.where(m, a, 0.0)
                pvs.append(lax.dot_general(a.astype(BF16), v_blk, (((1,), (0,)), ((), ())),
                                           preferred_element_type=F32))
                rs.append(r_carry[hh] + (C[:, 0:1] + L[:, 0:1]))
            return tuple(pvs), tuple(rs)

        acc0, r0 = block(qi, (jnp.zeros((tq, 1), F32),) * HP, True)

        def cond(st):
            kb, _, _, rmax = st
            return jnp.logical_and(kb >= 0, rmax > SKIP_LOG)

        def step(st):
            kb, acc, r, _ = st
            pv, r2 = block(kb, r, False)
            return kb - 1, tuple(a + b for a, b in zip(acc, pv)), r2, _max_all(r2)

        _, accs, _, _ = lax.while_loop(cond, step, (qi - 1, acc0, r0, _max_all(r0)))
        for hh, cols in enumerate(heads):
            acc = accs[hh]
            o_ref[:, cols] = acc
            rstd = lax.rsqrt(jnp.mean(acc * acc, axis=-1, keepdims=True) + EPS)
            on_ref[:, cols] = ((acc * rstd) * og_ref[:, cols]).astype(BF16)

    blk = pl.BlockSpec((tq, WP), lambda h, i: (i, h))
    full = pl.BlockSpec((S, WP), lambda h, i: (0, h))
    return _pcall(
        body, grid=(n_heads // HP, NQ),
        in_specs=[blk, full, full, pl.BlockSpec((1, WP), lambda h, i: (0, h)),
                  pl.BlockSpec((tq, tq), lambda h, i: (0, 0))],
        out_specs=(blk, blk),
        out_shape=(jax.ShapeDtypeStruct((S, W), F32), jax.ShapeDtypeStruct((S, W), BF16)),
        semantics=("parallel", "arbitrary"), name=name, args=(qh, kh, vh, out_g, u_strict), side=side)


def _sb_bwd(qh, kh, vh, o_pre, d_on, d_on_col0, out_g, *, n_heads, tq, name, side=None):
    S, W = qh.shape
    HP = _heads_per_step(n_heads)
    WP = HP * HEAD
    assert d_on_col0 % HP == 0
    NQ = S // tq
    scale = HEAD ** -0.5
    u_strict = jnp.tril(jnp.ones((tq, tq), F32), -1).astype(BF16)
    u_incl = jnp.tril(jnp.ones((tq, tq), F32)).astype(BF16)

    def body(q_ref, k_ref, v_ref, o_ref, dy_ref, og_ref, us_ref, ui_ref,
             dq_ref, dk_ref, dv_ref, dog_ref):
        qi = pl.program_id(1)

        @pl.when(qi == 0)
        def _():
            dk_ref[...] = jnp.zeros_like(dk_ref)
            dv_ref[...] = jnp.zeros_like(dv_ref)
            dog_ref[...] = jnp.zeros_like(dog_ref)

        us = us_ref[...]
        ui = ui_ref[...]
        heads = [slice(hh * HEAD, (hh + 1) * HEAD) for hh in range(HP)]
        qs, d_obs, deltas = [], [], []
        for cols in heads:
            qs.append(q_ref[:, cols])
            o = o_ref[:, cols]
            dy = dy_ref[:, cols]
            rstd = lax.rsqrt(jnp.mean(o * o, axis=-1, keepdims=True) + EPS)
            n = o * rstd
            dog_ref[:, cols] += jnp.sum(dy * n, axis=0, keepdims=True)
            dn = dy * og_ref[:, cols]
            d_o = rstd * (dn - n * jnp.mean(dn * n, axis=-1, keepdims=True))
            d_ob = d_o.astype(BF16)
            d_obs.append(d_ob)
            deltas.append(jnp.sum(d_ob.astype(F32) * o, axis=-1, keepdims=True))

        def block(kb, r_carry, g_carry, diag):
            rows = pl.ds(pl.multiple_of(kb * tq, tq), tq)
            dqs, rs, gs = [], [], []
            for hh, cols in enumerate(heads):
                k_blk = k_ref[rows, cols]
                v_blk = v_ref[rows, cols]
                z, e, sp = _sb_scores(qs[hh], k_blk, scale)
                if diag:
                    m = _strict_lower_mask(tq)
                    L = jnp.where(m, -sp, 0.0)
                else:
                    L = -sp
                C = _split_dot(L, us, 2)
                a = jnp.exp(z - sp + C + r_carry[hh])
                if diag:
                    a = jnp.where(m, a, 0.0)
                ab = a.astype(BF16)
                dA = lax.dot_general(d_obs[hh], v_blk, _NT, preferred_element_type=F32)
                G = ab.astype(F32) * dA
                SI = _split_dot(G, ui, 3)
                P = deltas[hh] - (g_carry[hh] + SI)
                r = 1.0 / (1.0 + e)
                er = e * r
                pos = z >= 0
                dz = G * jnp.where(pos, er, r) - P * jnp.where(pos, r, er)
                if diag:
                    dz = jnp.where(m, dz, 0.0)
                dzb = (dz * scale).astype(BF16)
                dqs.append(lax.dot_general(dzb, k_blk, (((1,), (0,)), ((), ())), preferred_element_type=F32))
                dk_ref[rows, cols] += lax.dot_general(dzb, qs[hh], _TN, preferred_element_type=F32)
                dv_ref[rows, cols] += lax.dot_general(ab, d_obs[hh], _TN, preferred_element_type=F32)
                rs.append(r_carry[hh] + (C[:, 0:1] + L[:, 0:1]))
                gs.append(g_carry[hh] + SI[:, 0:1])
            return tuple(dqs), tuple(rs), tuple(gs)

        zero = (jnp.zeros((tq, 1), F32),) * HP
        dq0, r0, g0 = block(qi, zero, zero, True)

        def cond(st):
            kb, _, _, _, rmax = st
            return jnp.logical_and(kb >= 0, rmax > SKIP_LOG)

        def step(st):
            kb, dq, r, g, _ = st
            dq_part, r2, g2 = block(kb, r, g, False)
            return kb - 1, tuple(a + b for a, b in zip(dq, dq_part)), r2, g2, _max_all(r2)

        _, dqs, _, _, _ = lax.while_loop(cond, step, (qi - 1, dq0, r0, g0, _max_all(r0)))
        for hh, cols in enumerate(heads):
            dq_ref[:, cols] = dqs[hh]

    blk = pl.BlockSpec((tq, WP), lambda h, i: (i, h))
    full = pl.BlockSpec((S, WP), lambda h, i: (0, h))
    vec = pl.BlockSpec((1, WP), lambda h, i: (0, h))
    cst = pl.BlockSpec((tq, tq), lambda h, i: (0, 0))
    act = jax.ShapeDtypeStruct((S, W), F32)
    return _pcall(
        body, grid=(n_heads // HP, NQ),
        in_specs=[blk, full, full, blk,
                  pl.BlockSpec((tq, WP), lambda h, i: (i, d_on_col0 // HP + h)), vec, cst, cst],
        out_specs=(blk, full, full, vec),
        out_shape=(act, act, act, jax.ShapeDtypeStruct((1, W), F32)),
        semantics=("parallel", "arbitrary"), name=name,
        args=(qh, kh, vh, o_pre, d_on, out_g, u_strict, u_incl), side=side)


def _sb_pre_bwd(proj, dqh, dkh, dvh, q_g, k_g, *, n_heads, col0, tm, name):
    S = proj.shape[0]
    H = n_heads
    W = H * HEAD

    def body(q_ref, k_ref, dqh_ref, dkh_ref, dvh_ref, qg_ref, kg_ref,
             dq_ref, dk_ref, dv_ref, dqg_ref, dkg_ref):
        first = jnp.logical_and(pl.program_id(0) == 0, pl.program_id(1) == 0)

        @pl.when(first)
        def _():
            dqg_ref[...] = jnp.zeros_like(dqg_ref)
            dkg_ref[...] = jnp.zeros_like(dkg_ref)

        for src, dh_ref, g_ref, dst, dg_ref in ((q_ref, dqh_ref, qg_ref, dq_ref, dqg_ref),
                                                (k_ref, dkh_ref, kg_ref, dk_ref, dkg_ref)):
            xv = src[...]
            dh = dh_ref[...]
            rstd = lax.rsqrt(jnp.mean(xv * xv, axis=-1, keepdims=True) + EPS)
            n = xv * rstd
            dg_ref[...] += jnp.sum(dh * n, axis=0, keepdims=True)
            dn = dh * g_ref[...]
            dst[...] = (rstd * (dn - n * jnp.mean(dn * n, axis=-1, keepdims=True))).astype(BF16)
        dv_ref[...] = dvh_ref[...].astype(BF16)

    def col(group):
        return pl.BlockSpec((tm, HEAD), lambda i, h: (i, col0 + group * H + h))

    vec = pl.BlockSpec((1, HEAD), lambda i, h: (0, 0))
    own = pl.BlockSpec((tm, HEAD), lambda i, h: (i, h))
    act = jax.ShapeDtypeStruct((S, W), BF16)
    vec_shape = jax.ShapeDtypeStruct((1, HEAD), F32)
    return pl.pallas_call(
        body, grid=(S // tm, H), in_specs=[col(0), col(1), own, own, own, vec, vec],
        out_specs=(own, own, own, vec, vec), out_shape=(act, act, act, vec_shape, vec_shape),
        compiler_params=_params(("arbitrary", "arbitrary")), name=name,
    )(proj, proj, dqh, dkh, dvh, q_g, k_g)


def _softmax_rows(x_ref, L):
    rows = [x_ref[l:l + 1, :] for l in range(L)]
    mx = rows[0]
    for r in rows[1:]:
        mx = jnp.maximum(mx, r)
    ex = [jnp.exp(r - mx) for r in rows]
    tot = ex[0]
    for e in ex[1:]:
        tot = tot + e
    return [e / tot for e in ex]


def _lb_fwd(logits, *, name):
    L, W = logits.shape

    def body(x_ref, o_ref):
        s = _softmax_rows(x_ref, L)
        run = jnp.zeros((1, W), F32)
        for l in range(L):
            run = run + s[l]
            o_ref[l:l + 1, :] = run - s[0]

    return pl.pallas_call(body, out_shape=jax.ShapeDtypeStruct((L, W), F32), name=name)(logits)


def _lb_bwd(logits, dlb_parts, *, name):
    L, W = logits.shape
    P = dlb_parts.shape[0]

    def body(x_ref, d_ref, o_ref):
        s = _softmax_rows(x_ref, L)
        dlb = []
        for l in range(L):
            t = d_ref[0, l:l + 1, :]
            for q in range(1, P):
                t = t + d_ref[q, l:l + 1, :]
            dlb.append(t)
        ds = [None] * L
        run = jnp.zeros((1, W), F32)
        for j in reversed(range(L)):
            run = run + dlb[j]
            ds[j] = run
        ds[0] = jnp.zeros((1, W), F32)
        inner = jnp.zeros((1, W), F32)
        for j in range(L):
            inner = inner + s[j] * ds[j]
        for j in range(L):
            o_ref[j:j + 1, :] = s[j] * (ds[j] - inner)

    return pl.pallas_call(body, out_shape=jax.ShapeDtypeStruct((L, W), F32), name=name)(logits, dlb_parts)


def _ada_mod(c_all, w_ada, *, nb, name):
    L, D, n = w_ada.shape
    B = c_all.shape[0]

    def body(c_ref, w_ref, o_ref, cond_ref):
        cv = c_ref[...]
        s, _ = _sigmoid_pair(cv)
        cond = cv * s
        cond_ref[...] = cond
        o_ref[...] = _bdot(cond, w_ref[...])

    return pl.pallas_call(
        body, grid=(L, n // nb),
        in_specs=[pl.BlockSpec((B, D), lambda l, j: (0, 0)),
                  pl.BlockSpec((None, D, nb), lambda l, j: (l, 0, j))],
        out_specs=(pl.BlockSpec((None, B, nb), lambda l, j: (l, 0, j)),
                   pl.BlockSpec((B, D), lambda l, j: (0, 0))),
        out_shape=(jax.ShapeDtypeStruct((L, B, n), F32), jax.ShapeDtypeStruct((B, D), F32)),
        compiler_params=_params(("arbitrary", "arbitrary")), name=name)(c_all, w_ada)


def _adam_math(w, g, m, v):
    m2 = ADAM_B1 * m + (1.0 - ADAM_B1) * g
    v2 = ADAM_B2 * v + (1.0 - ADAM_B2) * (g * g)
    m_hat = m2 / (1.0 - ADAM_B1 ** ADAM_STEP)
    v_hat = v2 / (1.0 - ADAM_B2 ** ADAM_STEP)
    delta = -ADAM_LR * (m_hat / (jnp.sqrt(v_hat) + ADAM_EPS) + ADAM_WD * w)
    return delta, m2, v2


def _adamw(w, m, v, gparts, *, tr, name):
    R, C = w.shape
    P = gparts.shape[0]

    def body(w_ref, m_ref, v_ref, gp_ref, g_ref, d_ref, m2_ref, v2_ref):
        g = gp_ref[0].astype(F32)
        for p in range(1, P):
            g = g + gp_ref[p].astype(F32)
        delta, m2, v2 = _adam_math(w_ref[...], g, m_ref[...], v_ref[...])
        g_ref[...] = g
        d_ref[...] = delta
        m2_ref[...] = m2
        v2_ref[...] = v2

    tile = pl.BlockSpec((tr, C), lambda i: (i, 0))
    shp = jax.ShapeDtypeStruct((R, C), F32)
    return pl.pallas_call(
        body, grid=(R // tr,),
        in_specs=[tile, tile, tile, pl.BlockSpec((P, tr, C), lambda i: (0, i, 0))],
        out_specs=(tile, tile, tile, tile), out_shape=(shp, shp, shp, shp),
        compiler_params=_params(("parallel",)), name=name)(w, m, v, gparts)


def _adamw_layers(w, m, v, gparts, *, tr, name, side=None):
    L, R, C = w.shape
    P = gparts[0].shape[0]
    nblk = R // tr

    def body(*refs):
        w_ref, m_ref, v_ref = refs[:3]
        gp_refs = refs[3:3 + L]
        g_ref, d_ref, m2_ref, v2_ref = refs[3 + L:]
        layer = pl.program_id(0)
        for t in range(L):
            @pl.when(layer == t)
            def _(t=t):
                g = gp_refs[t][0].astype(F32)
                for q in range(1, P):
                    g = g + gp_refs[t][q].astype(F32)
                delta, m2, v2 = _adam_math(w_ref[...], g, m_ref[...], v_ref[...])
                g_ref[...] = g
                d_ref[...] = delta
                m2_ref[...] = m2
                v2_ref[...] = v2

    def gp_spec(t):
        def index(l, i):
            return (0, jnp.where(l == t, i, jnp.where(l < t, 0, nblk - 1)), 0)
        return pl.BlockSpec((P, tr, C), index)

    tile = pl.BlockSpec((None, tr, C), lambda l, i: (l, i, 0))
    shp = jax.ShapeDtypeStruct((L, R, C), F32)
    return _pcall(
        body, grid=(L, nblk), in_specs=[tile, tile, tile] + [gp_spec(t) for t in range(L)],
        out_specs=(tile, tile, tile, tile), out_shape=(shp, shp, shp, shp),
        semantics=("arbitrary", "arbitrary"), name=name, args=(w, m, v, *gparts), side=side)


def _adamw_ada(w, m, v, cond_t, dmod, *, tr, name):
    L, D, n = w.shape
    Bp = cond_t.shape[1]

    def body(w_ref, m_ref, v_ref, c_ref, dm_ref, g_ref, d_ref, m2_ref, v2_ref):
        g = _bdot(c_ref[...], dm_ref[...])
        delta, m2, v2 = _adam_math(w_ref[...], g, m_ref[...], v_ref[...])
        g_ref[...] = g
        d_ref[...] = delta
        m2_ref[...] = m2
        v2_ref[...] = v2

    tile = pl.BlockSpec((None, tr, n), lambda l, i: (l, i, 0))
    shp = jax.ShapeDtypeStruct((L, D, n), F32)
    return pl.pallas_call(
        body, grid=(L, D // tr),
        in_specs=[tile, tile, tile, pl.BlockSpec((tr, Bp), lambda l, i: (i, 0)),
                  pl.BlockSpec((None, Bp, n), lambda l, i: (l, 0, 0))],
        out_specs=(tile, tile, tile, tile), out_shape=(shp, shp, shp, shp),
        compiler_params=_params(("parallel", "parallel")), name=name)(w, m, v, cond_t, dmod)


def _allgather_small(block, *, name):
    R, C = block.shape

    def body(x_ref, out_ref, send_sems, recv_sems, local_sem):
        x, y, c = lax.axis_index("x"), lax.axis_index("y"), lax.axis_index("c")

        def rows(px, py, pc):
            return out_ref.at[pl.ds((4 * px + 2 * py + pc) * R, R), :]

        mine = pltpu.make_async_copy(x_ref, rows(x, y, c), local_sem)
        mine.start()
        sends = []
        for rel in range(1, N_DEV):
            to = _peer(x, y, c, rel)
            cp = pltpu.make_async_remote_copy(src_ref=x_ref, dst_ref=rows(x, y, c),
                                              send_sem=send_sems.at[rel - 1], recv_sem=recv_sems.at[rel - 1],
                                              device_id=to, device_id_type=MESH)
            cp.start()
            sends.append(cp)
        for rel in range(1, N_DEV):
            frm = _peer(x, y, c, rel)
            pltpu.make_async_remote_copy(src_ref=x_ref, dst_ref=rows(*frm),
                                         send_sem=send_sems.at[rel - 1], recv_sem=recv_sems.at[rel - 1],
                                         device_id=frm, device_id_type=MESH).wait_recv()
        for cp in sends:
            cp.wait_send()
        mine.wait()

    return pl.pallas_call(
        body, out_shape=jax.ShapeDtypeStruct((N_DEV * R, C), block.dtype),
        in_specs=[pl.BlockSpec(memory_space=pltpu.VMEM)],
        out_specs=pl.BlockSpec(memory_space=pltpu.VMEM),
        scratch_shapes=[pltpu.SemaphoreType.DMA((N_DEV - 1,)), pltpu.SemaphoreType.DMA((N_DEV - 1,)),
                        pltpu.SemaphoreType.DMA],
        compiler_params=pltpu.CompilerParams(vmem_limit_bytes=V7X_VMEM_LIMIT), name=name)(block)


def _allgather_hbm(shards, *, name):
    n = len(shards)

    def body(*refs):
        ins = refs[:n]
        outs = refs[n:2 * n]
        send_sems, recv_sems, local_sems = refs[2 * n:]
        x, y, c = lax.axis_index("x"), lax.axis_index("y"), lax.axis_index("c")
        sibling = (x, y, 1 - c)
        chips = [(1 - x, y), (x, 1 - y), (1 - x, 1 - y)]

        def slot(t, px, py, pc):
            return outs[t].at[4 * px + 2 * py + pc]

        def copy(t, k, block, to, src=None):
            return pltpu.make_async_remote_copy(
                src_ref=slot(t, *block) if src is None else src, dst_ref=slot(t, *block),
                send_sem=send_sems.at[t * 7 + k], recv_sem=recv_sems.at[t * 7 + k],
                device_id=to, device_id_type=MESH)

        me = (x, y, c)
        started = []
        mine = []
        for t in range(n):
            cp = pltpu.make_async_copy(ins[t], slot(t, *me), local_sems.at[t])
            cp.start()
            mine.append(cp)
            first = [copy(t, 0, me, sibling, src=ins[t])]
            first += [copy(t, 1 + j, me, (*chip, c), src=ins[t]) for j, chip in enumerate(chips)]
            for cp in first:
                cp.start()
            started += first
        for t in range(n):
            for j, chip in enumerate(chips):
                copy(t, 1 + j, (*chip, c), me).wait_recv()
                fwd = copy(t, 4 + j, (*chip, c), sibling)
                fwd.start()
                started.append(fwd)
        for t in range(n):
            copy(t, 0, sibling, me).wait_recv()
            for j, chip in enumerate(chips):
                copy(t, 4 + j, (*chip, 1 - c), me).wait_recv()
        for cp in started:
            cp.wait_send()
        for cp in mine:
            cp.wait()

    any_spec = pl.BlockSpec(memory_space=pl.ANY)
    return pl.pallas_call(
        body, out_shape=[jax.ShapeDtypeStruct((N_DEV,) + s.shape, s.dtype) for s in shards],
        in_specs=[any_spec] * n, out_specs=[any_spec] * n,
        scratch_shapes=[pltpu.SemaphoreType.DMA((7 * n,)), pltpu.SemaphoreType.DMA((7 * n,)),
                        pltpu.SemaphoreType.DMA((n,))],
        name=name)(*shards)


def _tile(total, want):
    step = 128 if total % 128 == 0 else 8
    best = step
    t = step
    while t <= min(total, want):
        if total % t == 0:
            best = t
        t += step
    return best


def _local_step(x, target, mods, lbs, p, wg, shards=None):
    S, D = x.shape
    L = mods.shape[0]
    W = D // 2
    H = W // HEAD
    F = wg["w_ffn_out"][0].shape[0]
    mesh = shards is not None
    tm = _tile(S, 512)
    tm_big = _tile(S, 1024)
    tm_tn = _tile(S, 2048)
    tm_sw = _tile(S, 128)
    tq = _tile(S, 256)
    cg = max(1, min(8, S // CHUNK))
    nb_out = _tile(D, 1024)
    kb_f = _tile(F, 1408)

    def row(a, l):
        return a[l][None, :]

    def gather_of(l, names):
        if mesh and l < L:
            return (True, [shards[k][l] for k in names])
        return None

    def scatter_of(blocks):
        if mesh and blocks is not None:
            return (False, [b.reshape((N_DEV, -1) + b.shape[-1:]) if b.ndim == 2 else b for b in blocks])
        return None

    saved = []
    xcur = x
    for l in range(L):
        mod = mods[l]
        sh1, sc1, g1, sh2, sc2, g2 = [mod[:, i * D:(i + 1) * D] for i in range(N_MOD)]
        h1 = _modnorm_fwd(xcur, row(p["norm1_g"], l), sc1, sh1, tm=tm, name="norm1_fwd")
        proj, got = _mm_nn(h1, wg["w_in"][l], tm=tm_big, name="proj_fwd", side=gather_of(l + 1, ["w_in"]))
        if got is not None:
            wg["w_in"][l + 1] = got[0]
        lb = lbs[l][None, :]
        (o_hg, on_hg, states), got = _hg_fwd(proj, lb, row(p["hg_out_g"], l), n_heads=H, cg=cg, name="hgrn2_fwd",
                                             side=gather_of(l + 1, ["w_out"]))
        if got is not None:
            wg["w_out"][l + 1] = got[0].reshape(-1, D)
        qh, kh, vh = _sb_pre(proj, row(p["sb_q_g"], l), row(p["sb_k_g"], l), n_heads=H, col0=4 * H,
                             tm=tm, name="sb_qknorm_fwd")
        (o_sb, on_sb), got = _sb_fwd(qh, kh, vh, row(p["sb_out_g"], l), n_heads=H, tq=tq, name="sb_fwd",
                                     side=gather_of(l + 1, ["w_ffn_out"]))
        if got is not None:
            wg["w_ffn_out"][l + 1] = got[0].reshape(-1, D)
        o_cat = jnp.concatenate([on_hg, on_sb], axis=1)
        (x1, mixed), _ = _mm_nn(o_cat, wg["w_out"][l], tm=tm_big, nb=nb_out, resid=xcur, gate=g1,
                                name="out_proj_fwd")
        h2 = _modnorm_fwd(x1, row(p["norm2_g"], l), sc2, sh2, tm=tm, name="norm2_fwd")
        w_fin = wg["w_ffn_in"][l]
        (gate, up, a), got = _ffn_in_fwd(h2, w_fin if w_fin.shape[0] % 2 == 0 else w_fin[0], tm=tm, nb=F,
                                         name="ffn_in_fwd", side=gather_of(l + 1, ["w_ffn_in"]))
        if got is not None:
            wg["w_ffn_in"][l + 1] = got[0]
        (x2, ffn), _ = _mm_nn(a, wg["w_ffn_out"][l], tm=tm, nb=nb_out // 2, resid=x1, gate=g2, name="ffn_out_fwd")
        saved.append(dict(x=xcur, h1=h1, proj=proj, o_hg=o_hg, o_sb=o_sb, states=states, qh=qh, kh=kh, vh=vh,
                          o_cat=o_cat, mixed=mixed, x1=x1, h2=h2, gate=gate, up=up, a=a, ffn=ffn, lb=lb,
                          sc1=sc1, g1=g1, sc2=sc2, g2=g2))
        xcur = x2

    last = saved[-1]
    dx, dffn, dg2, loss = _loss_bwd(xcur, target, last["ffn"], last["g2"], tm=tm, name="loss_bwd")

    big = {k: [None] * L for k in ("w_in", "w_out", "w_ffn_in", "w_ffn_out")}
    small = {k: [None] * L for k in ("norm1_g", "hg_lb", "hg_out_g", "sb_q_g", "sb_k_g", "sb_out_g", "norm2_g")}
    dmods = [None] * L
    kb_d = _tile(D, 1024)
    pending_in = None
    for l in reversed(range(L)):
        sv = saved[l]
        (dgate, dup), got = _ffn_out_bwd_x(dffn, wg["w_ffn_out"][l], sv["gate"], sv["up"], tm=tm, kb=kb_f,
                                           name="ffn_out_bwd_x", side=scatter_of(pending_in))
        if got is not None:
            big["w_in"][l + 1] = got[0]
        g_fout, _ = _mm_tn(sv["a"], dffn, tm=tm_tn, kb=kb_f, nb=nb_out, blocked=False, name="ffn_out_bwd_w")
        w_fin = wg["w_ffn_in"][l]
        even = w_fin.shape[0] % 2 == 0
        dh2, got = _mm_nt(dgate, w_fin if even else w_fin[0], dy2=dup, tm=tm_big, kb=D, nb=F,
                          name="ffn_in_bwd_x", side=scatter_of([g_fout]))
        big["w_ffn_out"][l] = g_fout if got is None else got[0]
        g_fin, _ = _mm_tn(sv["h2"], dgate, dy2=dup, tm=tm_tn, kb=kb_d, nb=w_fin.shape[2] if even else F,
                          blocked=True, name="ffn_in_bwd_w")
        dx1, dmixed, dg1, dsh2, dsc2, dn2 = _modnorm_bwd(
            sv["x1"], dh2, dx, row(p["norm2_g"], l), sv["sc2"], sv["mixed"], sv["g1"], tm=tm_sw * 2,
            name="norm2_bwd")
        small["norm2_g"][l] = dn2
        d_ocat, _ = _mm_nt(dmixed, wg["w_out"][l], tm=tm_big, kb=nb_out, nb=D, name="out_proj_bwd_x")
        g_out, _ = _mm_tn(sv["o_cat"], dmixed, tm=tm_tn, kb=nb_out, nb=nb_out, blocked=False,
                          name="out_proj_bwd_w")
        (dhq, dhf, dhi, dhg, dlb, dhog), _ = _hg_bwd(sv["proj"], sv["o_hg"], d_ocat, 0, sv["states"], sv["lb"],
                                                     row(p["hg_out_g"], l), n_heads=H, cg=cg, name="hgrn2_bwd")
        (dqh, dkh, dvh, dsog), got = _sb_bwd(sv["qh"], sv["kh"], sv["vh"], sv["o_sb"], d_ocat, H,
                                             row(p["sb_out_g"], l), n_heads=H, tq=tq, name="sb_bwd",
                                             side=scatter_of([g_fin]))
        big["w_ffn_in"][l] = g_fin if got is None else got[0]
        dsq, dsk, dsv, dqg, dkg = _sb_pre_bwd(sv["proj"], dqh, dkh, dvh, row(p["sb_q_g"], l),
                                              row(p["sb_k_g"], l), n_heads=H, col0=4 * H, tm=tm,
                                              name="sb_qknorm_bwd")
        small["hg_lb"][l] = dlb
        small["hg_out_g"][l] = dhog
        small["sb_out_g"][l] = dsog
        small["sb_q_g"][l] = dqg
        small["sb_k_g"][l] = dkg
        dproj = jnp.concatenate([dhq, dhf, dhi, dhg, dsq, dsk, dsv], axis=1)
        dh1, got = _mm_nt(dproj, wg["w_in"][l], tm=tm_big, kb=D, name="proj_bwd_x", side=scatter_of([g_out]))
        big["w_out"][l] = g_out if got is None else got[0]
        g_in, _ = _mm_tn(sv["h1"], dproj, tm=tm_tn, kb=kb_d, nb=wg["w_in"][l].shape[2], blocked=True,
                         name="proj_bwd_w")
        pending_in = [g_in]
        big["w_in"][l] = g_in
        if l > 0:
            prev = saved[l - 1]
            dx0, dffn_prev, dg2_prev, dsh1, dsc1, dn1 = _modnorm_bwd(
                sv["x"], dh1, dx1, row(p["norm1_g"], l), sv["sc1"], prev["ffn"], prev["g2"], tm=tm_sw * 2,
                name="norm1_bwd")
        else:
            dx0, dsh1, dsc1, dn1 = _modnorm_bwd(sv["x"], dh1, dx1, row(p["norm1_g"], l), sv["sc1"], None, None,
                                                tm=tm_sw * 2, name="norm1_bwd_first")
            dffn_prev, dg2_prev = None, None
        small["norm1_g"][l] = dn1
        dmods[l] = jnp.concatenate([dsh1, dsc1, dg1, dsh2, dsc2, dg2], axis=1)
        dx, dffn, dg2 = dx0, dffn_prev, dg2_prev
    if mesh:
        big["w_in"][0] = None
    return loss, dx, big, small, dmods, scatter_of(pending_in)


def kernel(x, c, norm1_g, w_in, hg_lb_logits, hg_out_g, sb_q_g, sb_k_g, sb_out_g, w_out, norm2_g, w_ffn_in, w_ffn_out, w_ada, b_ada, loss_target, m_norm1_g, m_w_in, m_hg_lb_logits, m_hg_out_g, m_sb_q_g, m_sb_k_g, m_sb_out_g, m_w_out, m_norm2_g, m_w_ffn_in, m_w_ffn_out, m_w_ada, m_b_ada, v_norm1_g, v_w_in, v_hg_lb_logits, v_hg_out_g, v_sb_q_g, v_sb_k_g, v_sb_out_g, v_w_out, v_norm2_g, v_w_ffn_in, v_w_ffn_out, v_w_ada, v_b_ada):
    L, D = norm1_g.shape
    S = x.shape[1]
    me = 4 * lax.axis_index("x") + 2 * lax.axis_index("y") + lax.axis_index("c")

    c_all = _allgather_small(jnp.broadcast_to(c, (8, D)), name="gather_c").reshape(N_DEV, 8, D)[:, 0, :]
    n_ada = w_ada.shape[2]
    mod_cols, cond = _ada_mod(c_all, w_ada, nb=_tile(n_ada, 512), name="ada_mod")
    mod_all = _allgather_small(mod_cols.reshape(L * N_DEV, n_ada), name="gather_mod")
    mod_all = mod_all.reshape(N_DEV, L, N_DEV, n_ada)
    mod_mine = lax.dynamic_index_in_dim(mod_all, me, axis=2, keepdims=False)
    mods = jnp.transpose(mod_mine, (1, 0, 2)).reshape(L, 1, N_DEV * n_ada) + b_ada[:, None, :]

    lbs = _lb_fwd(hg_lb_logits, name="lower_bounds_fwd")

    shards = dict(w_in=[w_in[l].astype(BF16) for l in range(L)], w_out=[w_out[l].astype(BF16) for l in range(L)],
                  w_ffn_in=[w_ffn_in[l].astype(BF16) for l in range(L)],
                  w_ffn_out=[w_ffn_out[l].astype(BF16) for l in range(L)])
    g_in, g_out, g_fin, g_fout = _allgather_hbm(
        [shards["w_in"][0], shards["w_out"][0], shards["w_ffn_in"][0], shards["w_ffn_out"][0]],
        name="gather_weights")
    wg = dict(w_in=[g_in] + [None] * (L - 1), w_out=[g_out.reshape(-1, D)] + [None] * (L - 1),
              w_ffn_in=[g_fin] + [None] * (L - 1), w_ffn_out=[g_fout.reshape(-1, D)] + [None] * (L - 1))

    p = dict(norm1_g=norm1_g, hg_out_g=hg_out_g, sb_q_g=sb_q_g, sb_k_g=sb_k_g, sb_out_g=sb_out_g,
             norm2_g=norm2_g)
    loss_part, grad_x, recv, small, dmods, last_scatter = _local_step(x.reshape(S, D), loss_target.reshape(S, D), mods, lbs, p,
                                                        wg, shards)

    dmod = jnp.concatenate(dmods, axis=0)
    pieces = [jnp.concatenate(small[k], axis=0) for k in
              ("norm1_g", "hg_lb", "hg_out_g", "sb_q_g", "sb_k_g", "sb_out_g", "norm2_g")] + [dmod]
    flat = jnp.concatenate([a.reshape(-1) for a in pieces] + [loss_part.reshape(-1)])
    n_flat = flat.shape[0]
    rows = -(-n_flat // 1024) * 8
    flat = jnp.pad(flat, (0, rows * 128 - n_flat)).reshape(rows, 128)
    gathered = _allgather_small(flat, name="gather_small_grads").reshape(N_DEV, rows * 128)

    def take(off, shape):
        size = 1
        for s in shape:
            size *= s
        return gathered[:, off:off + size].reshape((N_DEV,) + tuple(shape)), off + size

    off = 0
    parts = {}
    for k, a in zip(("norm1_g", "hg_lb", "hg_out_g", "sb_q_g", "sb_k_g", "sb_out_g", "norm2_g", "dmod"), pieces):
        parts[k], off = take(off, a.shape)
    loss_parts = gathered[:, off:off + 1]
    loss = jnp.sum(loss_parts)

    def pad8(a):
        return jnp.pad(a, ((0, 0), (0, 8 - a.shape[1]), (0, 0)))

    def small_update(w, m, v, gparts):
        Lw = w.shape[0]
        g, d, m2, v2 = _adamw(pad8(w[None])[0], pad8(m[None])[0], pad8(v[None])[0], pad8(gparts),
                              tr=8, name="adamw_small")
        return g[:Lw], d[:Lw], m2[:Lw], v2[:Lw]

    out = {}
    out["norm1_g"] = small_update(norm1_g, m_norm1_g, v_norm1_g, parts["norm1_g"])
    dlogits = _lb_bwd(hg_lb_logits, parts["hg_lb"], name="lower_bounds_bwd")
    out["hg_lb_logits"] = small_update(hg_lb_logits, m_hg_lb_logits, v_hg_lb_logits, dlogits[None])
    out["hg_out_g"] = small_update(hg_out_g, m_hg_out_g, v_hg_out_g, parts["hg_out_g"])
    out["sb_q_g"] = small_update(sb_q_g, m_sb_q_g, v_sb_q_g, parts["sb_q_g"])
    out["sb_k_g"] = small_update(sb_k_g, m_sb_k_g, v_sb_k_g, parts["sb_k_g"])
    out["sb_out_g"] = small_update(sb_out_g, m_sb_out_g, v_sb_out_g, parts["sb_out_g"])
    out["norm2_g"] = small_update(norm2_g, m_norm2_g, v_norm2_g, parts["norm2_g"])
    out["b_ada"] = small_update(b_ada, m_b_ada, v_b_ada, parts["dmod"])

    dmod_all = parts["dmod"].reshape(N_DEV, L, N_DEV, n_ada)
    dmod_mine = lax.dynamic_index_in_dim(dmod_all, me, axis=2, keepdims=False)
    dmod_mine = jnp.pad(jnp.transpose(dmod_mine, (1, 0, 2)), ((0, 0), (0, 128 - N_DEV), (0, 0)))
    cond_t = jnp.pad(jnp.transpose(cond), ((0, 0), (0, 128 - N_DEV)))
    out["w_ada"] = _adamw_ada(w_ada, m_w_ada, v_w_ada, cond_t, dmod_mine, tr=_tile(D, 256), name="adamw_ada")

    def big_update(w, m, v, recv_l, name, side=None):
        return _adamw_layers(w, m, v, recv_l, tr=_tile(w.shape[1], 131072 // w.shape[2]), name=name, side=side)

    out["w_ffn_in"], got = big_update(w_ffn_in, m_w_ffn_in, v_w_ffn_in, recv["w_ffn_in"], "adamw_w_ffn_in",
                                      side=last_scatter)
    recv["w_in"][0] = got[0]
    out["w_ffn_out"], _ = big_update(w_ffn_out, m_w_ffn_out, v_w_ffn_out, recv["w_ffn_out"], "adamw_w_ffn_out")
    out["w_out"], _ = big_update(w_out, m_w_out, v_w_out, recv["w_out"], "adamw_w_out")
    out["w_in"], _ = big_update(w_in, m_w_in, v_w_in, recv["w_in"], "adamw_w_in")

    order = ("norm1_g", "w_in", "hg_lb_logits", "hg_out_g", "sb_q_g", "sb_k_g", "sb_out_g", "w_out", "norm2_g",
             "w_ffn_in", "w_ffn_out", "w_ada", "b_ada")
    grads = [out[k][0] for k in order]
    deltas = [out[k][1] for k in order]
    new_m = [out[k][2] for k in order]
    new_v = [out[k][3] for k in order]
    return (loss, grad_x.reshape(1, S, D), *grads, *deltas, *new_m, *new_v)
```

```python
import functools

import jax
import jax.numpy as jnp
from jax import lax
from jax.experimental import pallas as pl
from jax.experimental.pallas import tpu as pltpu

F32 = jnp.float32
BF16 = jnp.bfloat16
MESH = pl.DeviceIdType.MESH

N_DEV = 8
HEAD = 128
CHUNK = 64
N_MOD = 6
EPS = 1e-6
TINY = 1e-30
ADAM_LR = 0.001
ADAM_B1 = 0.9
ADAM_B2 = 0.999
ADAM_EPS = 1e-08
ADAM_WD = 0.01
ADAM_STEP = 10
V7X_VMEM_LIMIT = 56 * 1024 * 1024
SKIP_LOG = -104.0


def _params(sem):
    return pltpu.CompilerParams(dimension_semantics=sem, vmem_limit_bytes=V7X_VMEM_LIMIT)


def _bdot(a, b, dims=(((1,), (0,)), ((), ()))):
    return lax.dot_general(a.astype(BF16), b.astype(BF16), dims, preferred_element_type=F32)


_NT = (((1,), (1,)), ((), ()))
_TN = (((0,), (0,)), ((), ()))


def _sigmoid_pair(x):
    e = jnp.exp(-jnp.abs(x))
    r = 1.0 / (1.0 + e)
    er = e * r
    pos = x >= 0
    return jnp.where(pos, r, er), jnp.where(pos, er, r)


def _split_dot(x, u, parts):
    acc = None
    rem = x
    for _ in range(parts):
        p = rem.astype(BF16)
        rem = rem - p.astype(F32)
        t = lax.dot_general(p, u, (((1,), (0,)), ((), ())), preferred_element_type=F32)
        acc = t if acc is None else acc + t
    return acc


def _split_dot_left(u, x, parts):
    acc = None
    rem = x
    for _ in range(parts):
        p = rem.astype(BF16)
        rem = rem - p.astype(F32)
        t = lax.dot_general(u, p, (((1,), (0,)), ((), ())), preferred_element_type=F32)
        acc = t if acc is None else acc + t
    return acc


def _peer(x, y, c, rel):
    return (x ^ ((rel >> 2) & 1), y ^ ((rel >> 1) & 1), c ^ (rel & 1))


def _exchange(gather, srcs, dsts, send_sems, recv_sems, local_sems, phase):
    x, y, c = lax.axis_index("x"), lax.axis_index("y"), lax.axis_index("c")
    me = 4 * x + 2 * y + c
    for t in range(len(srcs)):
        own = srcs[t] if gather else srcs[t].at[me]
        local = pltpu.make_async_copy(own, dsts[t].at[me], local_sems.at[t])
        if phase == "start":
            local.start()
        for rel in range(1, N_DEV):
            px, py, pc = _peer(x, y, c, rel)
            pid = 4 * px + 2 * py + pc
            k = t * (N_DEV - 1) + rel - 1
            if phase == "start":
                pltpu.make_async_remote_copy(
                    src_ref=srcs[t] if gather else srcs[t].at[pid], dst_ref=dsts[t].at[me],
                    send_sem=send_sems.at[k], recv_sem=recv_sems.at[k],
                    device_id=(px, py, pc), device_id_type=MESH).start()
            else:
                cp = pltpu.make_async_remote_copy(
                    src_ref=own, dst_ref=dsts[t].at[pid], send_sem=send_sems.at[k], recv_sem=recv_sems.at[k],
                    device_id=(px, py, pc), device_id_type=MESH)
                cp.wait_recv()
                cp.wait_send()
        if phase == "wait":
            local.wait()


def _exchange_scratch(n):
    return [pltpu.SemaphoreType.DMA(((N_DEV - 1) * n,)), pltpu.SemaphoreType.DMA(((N_DEV - 1) * n,)),
            pltpu.SemaphoreType.DMA((n,))]


def _pcall(body, *, grid, in_specs, out_specs, out_shape, scratch_shapes=(), semantics, name, args, side=None):
    single = not isinstance(out_shape, (tuple, list))
    if single:
        out_specs, out_shape = [out_specs], [out_shape]
    in_specs, out_specs, out_shape = list(in_specs), list(out_specs), list(out_shape)
    scratch_shapes = list(scratch_shapes)
    n_in, n_out, n_scr = len(in_specs), len(out_specs), len(scratch_shapes)
    if side is None:
        res = pl.pallas_call(body, grid=grid, in_specs=in_specs, out_specs=out_specs, out_shape=out_shape,
                             scratch_shapes=scratch_shapes, compiler_params=_params(semantics), name=name)(*args)
        return (res[0] if single else tuple(res)), None
    gather, srcs = side
    n = len(srcs)

    def full(*refs):
        ins = refs[:n_in]
        s_in = refs[n_in:n_in + n]
        outs = refs[n_in + n:n_in + n + n_out]
        s_out = refs[n_in + n + n_out:n_in + 2 * n + n_out]
        scr = refs[n_in + 2 * n + n_out:n_in + 2 * n + n_out + n_scr]
        send_sems, recv_sems, local_sems = refs[n_in + 2 * n + n_out + n_scr:]
        first = pl.program_id(0) == 0
        last = pl.program_id(0) == grid[0] - 1
        for ax in range(1, len(grid)):
            first = jnp.logical_and(first, pl.program_id(ax) == 0)
            last = jnp.logical_and(last, pl.program_id(ax) == grid[ax] - 1)

        @pl.when(first)
        def _():
            _exchange(gather, s_in, s_out, send_sems, recv_sems, local_sems, "start")

        body(*ins, *outs, *scr)

        @pl.when(last)
        def _():
            _exchange(gather, s_in, s_out, send_sems, recv_sems, local_sems, "wait")

    any_spec = pl.BlockSpec(memory_space=pl.ANY)
    s_shapes = [jax.ShapeDtypeStruct(((N_DEV,) + s.shape) if gather else s.shape, s.dtype) for s in srcs]
    res = pl.pallas_call(full, grid=grid, in_specs=in_specs + [any_spec] * n,
                         out_specs=out_specs + [any_spec] * n, out_shape=out_shape + s_shapes,
                         scratch_shapes=scratch_shapes + _exchange_scratch(n),
                         compiler_params=_params(("arbitrary",) * len(grid)), name=name)(*args, *srcs)
    main = res[:n_out]
    return (main[0] if single else tuple(main)), list(res[n_out:])


def _mm_nn(a, b, *, tm, nb=None, out_dtype=F32, resid=None, gate=None, name, side=None):
    M, K = a.shape
    if b.ndim == 3:
        NB, _, n = b.shape
        b_spec = pl.BlockSpec((None, K, n), lambda j, i: (j, 0, 0))
    else:
        n = nb
        NB = b.shape[1] // nb
        b_spec = pl.BlockSpec((K, n), lambda j, i: (0, j))
    N = NB * n
    epi = resid is not None

    def body(*refs):
        if epi:
            a_ref, b_ref, r_ref, g_ref, o_ref, acc_ref = refs
        else:
            a_ref, b_ref, o_ref = refs
        acc = jnp.dot(a_ref[...], b_ref[...], preferred_element_type=F32)
        if epi:
            o_ref[...] = r_ref[...] + g_ref[...] * acc
            acc_ref[...] = acc.astype(BF16)
        else:
            o_ref[...] = acc.astype(out_dtype)

    in_specs = [pl.BlockSpec((tm, K), lambda j, i: (i, 0)), b_spec]
    args = [a, b]
    o_spec = pl.BlockSpec((tm, n), lambda j, i: (i, j))
    if epi:
        in_specs += [pl.BlockSpec((tm, n), lambda j, i: (i, j)), pl.BlockSpec((1, n), lambda j, i: (0, j))]
        args += [resid, gate]
        out_shape = (jax.ShapeDtypeStruct((M, N), F32), jax.ShapeDtypeStruct((M, N), BF16))
        out_specs = (o_spec, o_spec)
    else:
        out_shape = jax.ShapeDtypeStruct((M, N), out_dtype)
        out_specs = o_spec
    return _pcall(body, grid=(NB, M // tm), in_specs=in_specs, out_specs=out_specs, out_shape=out_shape,
                  semantics=("parallel", "parallel"), name=name, args=args, side=side)


def _halves(dy, dy2, blk_rows, blk_cols, nblocks, row_of, col_of, last_row=None):
    if dy2 is None:
        return [pl.BlockSpec((blk_rows, blk_cols), lambda *g: (row_of(*g), col_of(*g)))], None
    half = nblocks // 2

    def left(*g):
        r, c = row_of(*g), col_of(*g)
        if last_row is None:
            return (r, jnp.minimum(c, half - 1))
        return (jnp.where(c < half, r, last_row), jnp.minimum(c, half - 1))

    def right(*g):
        r, c = row_of(*g), col_of(*g)
        if last_row is None:
            return (r, jnp.maximum(c - half, 0))
        return (jnp.where(c >= half, r, 0), jnp.maximum(c - half, 0))

    return [pl.BlockSpec((blk_rows, blk_cols), left), pl.BlockSpec((blk_rows, blk_cols), right)], half


def _mm_nt(dy, w, *, tm, kb, nb=None, out_dtype=F32, name, side=None, dy2=None):
    M = dy.shape[0]
    N = dy.shape[1] * (1 if dy2 is None else 2)
    if w.ndim == 3:
        NB, Kt, n = w.shape
        w_spec = pl.BlockSpec((None, kb, n), lambda i, k, j: (j, k, 0))
    else:
        Kt = w.shape[0]
        n = nb
        NB = N // nb
        w_spec = pl.BlockSpec((kb, n), lambda i, k, j: (k, j))
    KB = Kt // kb
    dy_specs, half = _halves(dy, dy2, tm, n, NB, lambda i, k, j: i, lambda i, k, j: j)
    n_dy = len(dy_specs)

    def body(*refs):
        dy_refs = refs[:n_dy]
        w_ref, o_ref = refs[n_dy:n_dy + 2]
        acc = refs[n_dy + 2:]
        j = pl.program_id(2)

        def use(dy_ref):
            part = lax.dot_general(dy_ref[...], w_ref[...], _NT, preferred_element_type=F32)
            if NB == 1:
                o_ref[...] = part.astype(out_dtype)
                return
            acc_ref, = acc

            @pl.when(j == 0)
            def _():
                acc_ref[...] = part

            @pl.when(jnp.logical_and(j > 0, j < NB - 1))
            def _():
                acc_ref[...] += part

            @pl.when(j == NB - 1)
            def _():
                o_ref[...] = (acc_ref[...] + part).astype(out_dtype)

        if half is None:
            use(dy_refs[0])
        else:
            pl.when(j < half)(lambda: use(dy_refs[0]))
            pl.when(j >= half)(lambda: use(dy_refs[1]))

    return _pcall(
        body, grid=(M // tm, KB, NB),
        in_specs=dy_specs + [w_spec],
        out_specs=pl.BlockSpec((tm, kb), lambda i, k, j: (i, k)),
        out_shape=jax.ShapeDtypeStruct((M, Kt), out_dtype),
        scratch_shapes=[] if NB == 1 else [pltpu.VMEM((tm, kb), F32)],
        semantics=("parallel", "parallel", "arbitrary"), name=name,
        args=(dy, w) if dy2 is None else (dy, dy2, w), side=side)


def _mm_tn(x, dy, *, tm, kb, nb, blocked, name, side=None, dy2=None):
    M, K = x.shape
    N = dy.shape[1] * (1 if dy2 is None else 2)
    KB, NB, MB = K // kb, N // nb, M // tm
    dy_specs, half = _halves(dy, dy2, tm, nb, NB, lambda k, n, m: m, lambda k, n, m: n, last_row=MB - 1)
    n_dy = len(dy_specs)

    def body(*refs):
        x_ref = refs[0]
        dy_refs = refs[1:1 + n_dy]
        o_ref = refs[1 + n_dy]
        acc = refs[2 + n_dy:]
        m = pl.program_id(2)

        def use(dy_ref):
            part = lax.dot_general(x_ref[...], dy_ref[...], _TN, preferred_element_type=F32)
            if MB == 1:
                o_ref[...] = part.astype(BF16)
                return
            acc_ref, = acc

            @pl.when(m == 0)
            def _():
                acc_ref[...] = part

            @pl.when(jnp.logical_and(m > 0, m < MB - 1))
            def _():
                acc_ref[...] += part

            @pl.when(m == MB - 1)
            def _():
                o_ref[...] = (acc_ref[...] + part).astype(BF16)

        if half is None:
            use(dy_refs[0])
        else:
            nblk = pl.program_id(1)
            pl.when(nblk < half)(lambda: use(dy_refs[0]))
            pl.when(nblk >= half)(lambda: use(dy_refs[1]))

    if blocked:
        out_shape = jax.ShapeDtypeStruct((NB, K, nb), BF16)
        o_spec = pl.BlockSpec((None, kb, nb), lambda k, n, m: (n, k, 0))
    else:
        out_shape = jax.ShapeDtypeStruct((K, N), BF16)
        o_spec = pl.BlockSpec((kb, nb), lambda k, n, m: (k, n))
    return _pcall(
        body, grid=(KB, NB, MB),
        in_specs=[pl.BlockSpec((tm, kb), lambda k, n, m: (m, k))] + dy_specs,
        out_specs=o_spec, out_shape=out_shape,
        scratch_shapes=[] if MB == 1 else [pltpu.VMEM((kb, nb), F32)],
        semantics=("parallel", "parallel", "arbitrary"), name=name,
        args=(x, dy) if dy2 is None else (x, dy, dy2), side=side)


def _ffn_in_fwd(h, w, *, tm, nb, name, side=None):
    M, K = h.shape
    if w.ndim == 3:
        J, _, n = w.shape
        half = J // 2
        specs = [pl.BlockSpec((None, K, n), lambda j, i: (j, 0, 0)),
                 pl.BlockSpec((None, K, n), lambda j, i: (j + half, 0, 0))]
    else:
        n = nb
        half = w.shape[1] // (2 * nb)
        specs = [pl.BlockSpec((K, n), lambda j, i: (0, j)), pl.BlockSpec((K, n), lambda j, i: (0, j + half))]
    F = half * n

    def body(h_ref, wg_ref, wu_ref, gate_ref, up_ref, act_ref):
        hv = h_ref[...]
        gate = jnp.dot(hv, wg_ref[...], preferred_element_type=F32)
        up = jnp.dot(hv, wu_ref[...], preferred_element_type=F32)
        s, _ = _sigmoid_pair(gate)
        gate_ref[...] = gate
        up_ref[...] = up
        act_ref[...] = (gate * s * up).astype(BF16)

    o_spec = pl.BlockSpec((tm, n), lambda j, i: (i, j))
    f32 = jax.ShapeDtypeStruct((M, F), F32)
    return _pcall(
        body, grid=(half, M // tm), in_specs=[pl.BlockSpec((tm, K), lambda j, i: (i, 0))] + specs,
        out_specs=(o_spec, o_spec, o_spec), out_shape=(f32, f32, jax.ShapeDtypeStruct((M, F), BF16)),
        semantics=("parallel", "parallel"), name=name, args=(h, w, w), side=side)


def _ffn_out_bwd_x(dy, w, gate, up, *, tm, kb, name, side=None):
    M, D = dy.shape
    F = w.shape[0]

    def body(dy_ref, w_ref, g_ref, u_ref, dg_ref, du_ref):
        da = lax.dot_general(dy_ref[...], w_ref[...], _NT, preferred_element_type=F32)
        gate = g_ref[...]
        s, ns = _sigmoid_pair(gate)
        dg_ref[...] = (da * u_ref[...] * (s * (1.0 + gate * ns))).astype(BF16)
        du_ref[...] = (da * (gate * s)).astype(BF16)

    tile = pl.BlockSpec((tm, kb), lambda i, k: (i, k))
    act = jax.ShapeDtypeStruct((M, F), BF16)
    return _pcall(
        body, grid=(M // tm, F // kb),
        in_specs=[pl.BlockSpec((tm, D), lambda i, k: (i, 0)), pl.BlockSpec((kb, D), lambda i, k: (k, 0)), tile, tile],
        out_specs=(tile, tile), out_shape=(act, act),
        semantics=("parallel", "parallel"), name=name, args=(dy, w, gate, up), side=side)


def _modnorm_fwd(x, gain, sc, sh, *, tm, name):
    S, D = x.shape

    def body(x_ref, g_ref, sc_ref, sh_ref, h_ref):
        xv = x_ref[...]
        rstd = lax.rsqrt(jnp.mean(xv * xv, axis=-1, keepdims=True) + EPS)
        y = (xv * rstd) * g_ref[...]
        h_ref[...] = (y * (1.0 + sc_ref[...]) + sh_ref[...]).astype(BF16)

    row = pl.BlockSpec((1, D), lambda i: (0, 0))
    return pl.pallas_call(
        body, grid=(S // tm,),
        in_specs=[pl.BlockSpec((tm, D), lambda i: (i, 0)), row, row, row],
        out_specs=pl.BlockSpec((tm, D), lambda i: (i, 0)),
        out_shape=jax.ShapeDtypeStruct((S, D), BF16),
        compiler_params=_params(("parallel",)), name=name)(x, gain, sc, sh)


def _modnorm_bwd(x, dh, dres, gain, sc, branch, gate, *, tm, name):
    S, D = x.shape
    has_prev = branch is not None

    def body(*refs):
        if has_prev:
            (x_ref, dh_ref, dr_ref, g_ref, sc_ref, br_ref, gt_ref,
             dx_ref, dbr_ref, dgt_ref, dsh_ref, dsc_ref, dgn_ref) = refs
        else:
            (x_ref, dh_ref, dr_ref, g_ref, sc_ref,
             dx_ref, dsh_ref, dsc_ref, dgn_ref) = refs
        i = pl.program_id(0)
        xv = x_ref[...]
        dh_v = dh_ref[...]
        gv = g_ref[...]
        scale1 = 1.0 + sc_ref[...]
        rstd = lax.rsqrt(jnp.mean(xv * xv, axis=-1, keepdims=True) + EPS)
        n = xv * rstd
        dn = dh_v * (gv * scale1)
        dx = rstd * (dn - n * jnp.mean(dn * n, axis=-1, keepdims=True)) + dr_ref[...]
        dx_ref[...] = dx
        dhn = dh_v * n
        p_sh = jnp.sum(dh_v, axis=0, keepdims=True)
        p_sc = jnp.sum(dhn, axis=0, keepdims=True) * gv
        p_gn = jnp.sum(dhn, axis=0, keepdims=True) * scale1
        if has_prev:
            dbr_ref[...] = (gt_ref[...] * dx).astype(BF16)
            p_gt = jnp.sum(dx * br_ref[...].astype(F32), axis=0, keepdims=True)

        @pl.when(i == 0)
        def _():
            dsh_ref[...] = p_sh
            dsc_ref[...] = p_sc
            dgn_ref[...] = p_gn
            if has_prev:
                dgt_ref[...] = p_gt

        @pl.when(i > 0)
        def _():
            dsh_ref[...] += p_sh
            dsc_ref[...] += p_sc
            dgn_ref[...] += p_gn
            if has_prev:
                dgt_ref[...] += p_gt

    tile = pl.BlockSpec((tm, D), lambda i: (i, 0))
    row = pl.BlockSpec((1, D), lambda i: (0, 0))
    row_shape = jax.ShapeDtypeStruct((1, D), F32)
    if has_prev:
        in_specs = [tile, tile, tile, row, row, tile, row]
        args = (x, dh, dres, gain, sc, branch, gate)
        out_specs = (tile, tile, row, row, row, row)
        out_shape = (jax.ShapeDtypeStruct((S, D), F32), jax.ShapeDtypeStruct((S, D), BF16),
                     row_shape, row_shape, row_shape, row_shape)
    else:
        in_specs = [tile, tile, tile, row, row]
        args = (x, dh, dres, gain, sc)
        out_specs = (tile, row, row, row)
        out_shape = (jax.ShapeDtypeStruct((S, D), F32), row_shape, row_shape, row_shape)
    return pl.pallas_call(body, grid=(S // tm,), in_specs=in_specs, out_specs=out_specs,
                          out_shape=out_shape, compiler_params=_params(("arbitrary",)),
                          name=name)(*args)


def _loss_bwd(y, target, branch, gate, *, tm, name):
    S, D = y.shape
    nsteps = S // tm

    def body(y_ref, t_ref, br_ref, gt_ref, dy_ref, dbr_ref, dgt_ref, loss_ref, col_ref):
        i = pl.program_id(0)
        diff = y_ref[...] - t_ref[...]
        dy = diff * (1.0 / D)
        dy_ref[...] = dy
        dbr_ref[...] = (gt_ref[...] * dy).astype(BF16)
        p_gt = jnp.sum(dy * br_ref[...].astype(F32), axis=0, keepdims=True)
        p_col = jnp.sum(diff * diff, axis=0, keepdims=True)

        @pl.when(i == 0)
        def _():
            dgt_ref[...] = p_gt
            col_ref[...] = p_col

        @pl.when(i > 0)
        def _():
            dgt_ref[...] += p_gt
            col_ref[...] += p_col

        @pl.when(i == nsteps - 1)
        def _():
            tot = jnp.sum(col_ref[...], axis=-1, keepdims=True) * (0.5 / D)
            loss_ref[...] = jnp.broadcast_to(tot, (1, 128))

    tile = pl.BlockSpec((tm, D), lambda i: (i, 0))
    row = pl.BlockSpec((1, D), lambda i: (0, 0))
    return pl.pallas_call(
        body, grid=(nsteps,), in_specs=[tile, tile, tile, row],
        out_specs=(tile, tile, row, pl.BlockSpec((1, 128), lambda i: (0, 0))),
        out_shape=(jax.ShapeDtypeStruct((S, D), F32), jax.ShapeDtypeStruct((S, D), BF16),
                   jax.ShapeDtypeStruct((1, D), F32), jax.ShapeDtypeStruct((1, 128), F32)),
        scratch_shapes=[pltpu.VMEM((1, D), F32)],
        compiler_params=_params(("arbitrary",)), name=name)(y, target, branch, gate)


def _hg_chunk(q, fl, lbv, tri):
    C = q.shape[0]
    sq, nsq = _sigmoid_pair(q)
    qa = q * sq
    sig, nsig = _sigmoid_pair(fl)
    one_lb = 1.0 - lbv
    f = lbv + one_lb * sig
    fc = jnp.maximum(f, TINY)
    lf = jnp.log(fc)
    k = one_lb * nsig
    b = _split_dot_left(tri, lf, 3)
    row = lax.broadcasted_iota(jnp.int32, b.shape, 0)
    bm = jnp.sum(jnp.where(row == C // 2 - 1, b, 0.0), axis=0, keepdims=True)
    bl = jnp.sum(jnp.where(row == C - 1, b, 0.0), axis=0, keepdims=True)
    eb = jnp.exp(b)
    ebm = jnp.exp(b - bm)
    enbm = jnp.exp(bm - b)
    ebl = jnp.exp(bl - b)
    ebL = jnp.exp(bl)

    def operand(t):
        return t.astype(BF16).astype(F32)

    return dict(sq=sq, nsq=nsq, qa=qa, sig=sig, nsig=nsig, one_lb=one_lb, f=f, fc=fc, k=k,
                eb=eb, ebm=ebm, enbm=enbm, ebl=ebl, ebL=ebL,
                Qm=operand(qa * ebm), Km=operand(k * enbm), Qb=operand(qa * eb), Kh=operand(k * ebl), row=row)


def _causal_incl(C):
    r = lax.broadcasted_iota(jnp.int32, (C, C), 0)
    c = lax.broadcasted_iota(jnp.int32, (C, C), 1)
    return r >= c


def _hg_fwd(proj, lb, out_g, *, n_heads, cg, name, side=None):
    S = proj.shape[0]
    H = n_heads
    W = H * HEAD
    T = cg * CHUNK
    NG = S // T
    tri = jnp.tril(jnp.ones((CHUNK, CHUNK), F32)).astype(BF16)

    def body(q_ref, f_ref, v_ref, g_ref, lb_ref, og_ref, tri_ref, o_ref, on_ref, st_ref, s_scr):
        @pl.when(pl.program_id(1) == 0)
        def _():
            s_scr[...] = jnp.zeros_like(s_scr)

        lbv = lb_ref[...]
        ogv = og_ref[...]
        triv = tri_ref[...]
        mask = _causal_incl(CHUNK)
        for c in range(cg):
            rows = pl.ds(c * CHUNK, CHUNK)
            v = v_ref[rows, :]
            gg = g_ref[rows, :]
            cm = _hg_chunk(q_ref[rows, :], f_ref[rows, :], lbv, triv)
            s0 = s_scr[...]
            st_ref[c] = s0
            A = jnp.where(mask, _bdot(cm["Qm"], cm["Km"], _NT), 0.0)
            o = _bdot(A, v) + _bdot(cm["Qb"], s0, _NT)
            s_scr[...] = s0 * cm["ebL"] + _bdot(v, cm["Kh"], _TN)
            o_ref[rows, :] = o
            rstd = lax.rsqrt(jnp.mean(o * o, axis=-1, keepdims=True) + EPS)
            sg, _ = _sigmoid_pair(gg)
            on_ref[rows, :] = (((o * rstd) * ogv) * (gg * sg)).astype(BF16)

    def col(group):
        return pl.BlockSpec((T, HEAD), lambda h, g: (g, group * H + h))

    vec = pl.BlockSpec((1, HEAD), lambda h, g: (0, h))
    return _pcall(
        body, grid=(H, NG),
        in_specs=[col(0), col(1), col(2), col(3), vec, vec,
                  pl.BlockSpec((CHUNK, CHUNK), lambda h, g: (0, 0))],
        out_specs=(pl.BlockSpec((T, HEAD), lambda h, g: (g, h)),
                   pl.BlockSpec((T, HEAD), lambda h, g: (g, h)),
                   pl.BlockSpec((cg, None, HEAD, HEAD), lambda h, g: (g, h, 0, 0))),
        out_shape=(jax.ShapeDtypeStruct((S, W), F32), jax.ShapeDtypeStruct((S, W), BF16),
                   jax.ShapeDtypeStruct((S // CHUNK, H, HEAD, HEAD), F32)),
        scratch_shapes=[pltpu.VMEM((HEAD, HEAD), F32)],
        semantics=("parallel", "arbitrary"), name=name,
        args=(proj, proj, proj, proj, lb, out_g, tri), side=side)


def _hg_bwd(proj, o_pre, d_on, d_on_col0, states, lb, out_g, *, n_heads, cg, name, side=None):
    S = proj.shape[0]
    H = n_heads
    W = H * HEAD
    T = cg * CHUNK
    NG = S // T
    tri = jnp.tril(jnp.ones((CHUNK, CHUNK), F32)).astype(BF16)
    triu = jnp.triu(jnp.ones((CHUNK, CHUNK), F32)).astype(BF16)

    def body(q_ref, f_ref, v_ref, g_ref, o_ref, dy_ref, st_ref, lb_ref, og_ref, tri_ref, triu_ref,
             dq_ref, df_ref, di_ref, dg_ref, dlb_ref, dog_ref, ds_scr):
        gstep = pl.program_id(1)

        @pl.when(gstep == 0)
        def _():
            ds_scr[...] = jnp.zeros_like(ds_scr)
            dlb_ref[...] = jnp.zeros_like(dlb_ref)
            dog_ref[...] = jnp.zeros_like(dog_ref)

        lbv = lb_ref[...]
        ogv = og_ref[...]
        triv = tri_ref[...]
        triuv = triu_ref[...]
        mask = _causal_incl(CHUNK)
        dlb_acc = jnp.zeros((1, HEAD), F32)
        dog_acc = jnp.zeros((1, HEAD), F32)
        for c in reversed(range(cg)):
            rows = pl.ds(c * CHUNK, CHUNK)
            q = q_ref[rows, :]
            v = v_ref[rows, :]
            gg = g_ref[rows, :]
            o = o_ref[rows, :]
            dy = dy_ref[rows, :]
            cm = _hg_chunk(q, f_ref[rows, :], lbv, triv)
            s0 = st_ref[c]
            ds1 = ds_scr[...]
            rstd = lax.rsqrt(jnp.mean(o * o, axis=-1, keepdims=True) + EPS)
            n = o * rstd
            sg, nsg = _sigmoid_pair(gg)
            silu_g = gg * sg
            dyn = dy * n
            dog_acc = dog_acc + jnp.sum(dyn * silu_g, axis=0, keepdims=True)
            dg_ref[rows, :] = (dyn * ogv * (sg * (1.0 + gg * nsg))).astype(BF16)
            dn = dy * (ogv * silu_g)
            d_o = rstd * (dn - n * jnp.mean(dn * n, axis=-1, keepdims=True))
            A = jnp.where(mask, _bdot(cm["Qm"], cm["Km"], _NT), 0.0)
            dA = jnp.where(mask, _bdot(d_o, v, _NT), 0.0)
            dV = _bdot(A, d_o, _TN) + _bdot(cm["Kh"], ds1, _NT)
            dQm = _bdot(dA, cm["Km"])
            dKm = _bdot(dA, cm["Qm"], _TN)
            dQb = _bdot(d_o, s0)
            dKh = _bdot(v, ds1)
            ds_scr[...] = ds1 * cm["ebL"] + _bdot(d_o, cm["Qb"], _TN)
            kh_term = dKh * cm["Kh"]
            db = dQm * cm["Qm"] - dKm * cm["Km"] + dQb * cm["Qb"] - kh_term
            dbl = (jnp.sum(kh_term, axis=0, keepdims=True)
                   + cm["ebL"] * jnp.sum(ds1 * s0, axis=0, keepdims=True))
            db = db + jnp.where(cm["row"] == CHUNK - 1, dbl, 0.0)
            dlf = _split_dot_left(triuv, db, 3)
            dqa = dQm * cm["ebm"] + dQb * cm["eb"]
            dq_ref[rows, :] = (dqa * (cm["sq"] * (1.0 + q * cm["nsq"]))).astype(BF16)
            dk = dKm * cm["enbm"] + dKh * cm["ebl"]
            dfc = jnp.where(cm["f"] > TINY, dlf / cm["fc"], 0.0)
            t = dfc - dk
            df_ref[rows, :] = (t * (cm["one_lb"] * cm["sig"] * cm["nsig"])).astype(BF16)
            dlb_acc = dlb_acc + jnp.sum(t * cm["nsig"], axis=0, keepdims=True)
            di_ref[rows, :] = dV.astype(BF16)
        dlb_ref[...] += dlb_acc
        dog_ref[...] += dog_acc

    def col(group):
        return pl.BlockSpec((T, HEAD), lambda h, g: (NG - 1 - g, group * H + h))

    own = pl.BlockSpec((T, HEAD), lambda h, g: (NG - 1 - g, h))
    vec = pl.BlockSpec((1, HEAD), lambda h, g: (0, h))
    cst = pl.BlockSpec((CHUNK, CHUNK), lambda h, g: (0, 0))
    act = jax.ShapeDtypeStruct((S, W), BF16)
    vec_shape = jax.ShapeDtypeStruct((1, W), F32)
    return _pcall(
        body, grid=(H, NG),
        in_specs=[col(0), col(1), col(2), col(3), own,
                  pl.BlockSpec((T, HEAD), lambda h, g: (NG - 1 - g, d_on_col0 + h)),
                  pl.BlockSpec((cg, None, HEAD, HEAD), lambda h, g: (NG - 1 - g, h, 0, 0)),
                  vec, vec, cst, cst],
        out_specs=(own, own, own, own, vec, vec),
        out_shape=(act, act, act, act, vec_shape, vec_shape),
        scratch_shapes=[pltpu.VMEM((HEAD, HEAD), F32)],
        semantics=("parallel", "arbitrary"), name=name,
        args=(proj, proj, proj, proj, o_pre, d_on, states, lb, out_g, tri, triu), side=side)


def _sb_pre(proj, q_g, k_g, *, n_heads, col0, tm, name):
    S = proj.shape[0]
    H = n_heads
    W = H * HEAD

    def body(q_ref, k_ref, v_ref, qg_ref, kg_ref, qh_ref, kh_ref, vh_ref):
        for src, g_ref, dst in ((q_ref, qg_ref, qh_ref), (k_ref, kg_ref, kh_ref)):
            xv = src[...]
            rstd = lax.rsqrt(jnp.mean(xv * xv, axis=-1, keepdims=True) + EPS)
            dst[...] = ((xv * rstd) * g_ref[...]).astype(BF16)
        vh_ref[...] = v_ref[...].astype(BF16)

    def col(group):
        return pl.BlockSpec((tm, HEAD), lambda i, h: (i, col0 + group * H + h))

    vec = pl.BlockSpec((1, HEAD), lambda i, h: (0, 0))
    own = pl.BlockSpec((tm, HEAD), lambda i, h: (i, h))
    act = jax.ShapeDtypeStruct((S, W), BF16)
    return pl.pallas_call(
        body, grid=(S // tm, H), in_specs=[col(0), col(1), col(2), vec, vec],
        out_specs=(own, own, own), out_shape=(act, act, act),
        compiler_params=_params(("parallel", "parallel")), name=name)(proj, proj, proj, q_g, k_g)


def _sb_scores(q, k_blk, scale):
    z = lax.dot_general(q, k_blk, _NT, preferred_element_type=F32) * scale
    e = jnp.exp(-jnp.abs(z))
    sp = jnp.maximum(z, 0.0) + jnp.log(1.0 + e)
    return z, e, sp


def _heads_per_step(n_heads):
    return 2 if n_heads % 2 == 0 else 1


def _max_all(values):
    m = jnp.max(values[0])
    for v in values[1:]:
        m = jnp.maximum(m, jnp.max(v))
    return m


def _strict_lower_mask(t):
    r = lax.broadcasted_iota(jnp.int32, (t, t), 0)
    c = lax.broadcasted_iota(jnp.int32, (t, t), 1)
    return c < r


def _sb_fwd(qh, kh, vh, out_g, *, n_heads, tq, name, side=None):
    S, W = qh.shape
    HP = _heads_per_step(n_heads)
    WP = HP * HEAD
    NQ = S // tq
    scale = HEAD ** -0.5
    u_strict = jnp.tril(jnp.ones((tq, tq), F32), -1).astype(BF16)

    def body(q_ref, k_ref, v_ref, og_ref, u_ref, o_ref, on_ref):
        qi = pl.program_id(1)
        u = u_ref[...]
        heads = [slice(hh * HEAD, (hh + 1) * HEAD) for hh in range(HP)]
        qs = [q_ref[:, cols] for cols in heads]

        def block(kb, r_carry, diag, valid=None):
            rows = pl.ds(pl.multiple_of(kb * tq, tq), tq)
            pvs, rs = [], []
            for hh, cols in enumerate(heads):
                k_blk = k_ref[rows, cols]
                v_blk = v_ref[rows, cols]
                z, _, sp = _sb_scores(qs[hh], k_blk, scale)
                if diag:
                    m = _strict_lower_mask(tq)
                    L = jnp.where(m, -sp, 0.0)
                else:
                    L = -sp
                C = _split_dot(L, u, 2)
                a = jnp.exp(z - sp + C + r_carry[hh])
                if diag:
                    a = jnp.where(m, a, 0.0)
                if valid is not None:
                    a = jnp.where(valid, a, 0.0)
                pvs.append(lax.dot_general(a.astype(BF16), v_blk, (((1,), (0,)), ((), ())),
                                           preferred_element_type=F32))
                rs.append(r_carry[hh] + (C[:, 0:1] + L[:, 0:1]))
            return tuple(pvs), tuple(rs)

        acc_d, r_d = block(qi, (jnp.zeros((tq, 1), F32),) * HP, True)
        acc_p, r0 = block(jnp.maximum(qi - 1, 0), r_d, False, valid=qi > 0)
        acc0 = tuple(a + b for a, b in zip(acc_d, acc_p))

        def cond(st):
            kb, _, _, rmax = st
            return jnp.logical_and(kb >= 0, rmax > SKIP_LOG)

        def step(st):
            kb, acc, r, _ = st
            pv, r2 = block(kb, r, False)
            return kb - 1, tuple(a + b for a, b in zip(acc, pv)), r2, _max_all(r2)

        _, accs, _, _ = lax.while_loop(cond, step, (qi - 2, acc0, r0, _max_all(r0)))
        for hh, cols in enumerate(heads):
            acc = accs[hh]
            o_ref[:, cols] = acc
            rstd = lax.rsqrt(jnp.mean(acc * acc, axis=-1, keepdims=True) + EPS)
            on_ref[:, cols] = ((acc * rstd) * og_ref[:, cols]).astype(BF16)

    blk = pl.BlockSpec((tq, WP), lambda h, i: (i, h))
    full = pl.BlockSpec((S, WP), lambda h, i: (0, h))
    return _pcall(
        body, grid=(n_heads // HP, NQ),
        in_specs=[blk, full, full, pl.BlockSpec((1, WP), lambda h, i: (0, h)),
                  pl.BlockSpec((tq, tq), lambda h, i: (0, 0))],
        out_specs=(blk, blk),
        out_shape=(jax.ShapeDtypeStruct((S, W), F32), jax.ShapeDtypeStruct((S, W), BF16)),
        semantics=("parallel", "arbitrary"), name=name, args=(qh, kh, vh, out_g, u_strict), side=side)


def _sb_bwd(qh, kh, vh, o_pre, d_on, d_on_col0, out_g, *, n_heads, tq, name, side=None):
    S, W = qh.shape
    HP = _heads_per_step(n_heads)
    WP = HP * HEAD
    assert d_on_col0 % HP == 0
    NQ = S // tq
    scale = HEAD ** -0.5
    u_strict = jnp.tril(jnp.ones((tq, tq), F32), -1).astype(BF16)
    u_incl = jnp.tril(jnp.ones((tq, tq), F32)).astype(BF16)

    def body(q_ref, k_ref, v_ref, o_ref, dy_ref, og_ref, us_ref, ui_ref,
             dq_ref, dk_ref, dv_ref, dog_ref):
        qi = pl.program_id(1)

        @pl.when(qi == 0)
        def _():
            dk_ref[...] = jnp.zeros_like(dk_ref)
            dv_ref[...] = jnp.zeros_like(dv_ref)
            dog_ref[...] = jnp.zeros_like(dog_ref)

        us = us_ref[...]
        ui = ui_ref[...]
        heads = [slice(hh * HEAD, (hh + 1) * HEAD) for hh in range(HP)]
        qs, d_obs, deltas = [], [], []
        for cols in heads:
            qs.append(q_ref[:, cols])
            o = o_ref[:, cols]
            dy = dy_ref[:, cols]
            rstd = lax.rsqrt(jnp.mean(o * o, axis=-1, keepdims=True) + EPS)
            n = o * rstd
            dog_ref[:, cols] += jnp.sum(dy * n, axis=0, keepdims=True)
            dn = dy * og_ref[:, cols]
            d_o = rstd * (dn - n * jnp.mean(dn * n, axis=-1, keepdims=True))
            d_ob = d_o.astype(BF16)
            d_obs.append(d_ob)
            deltas.append(jnp.sum(d_ob.astype(F32) * o, axis=-1, keepdims=True))

        def block(kb, r_carry, g_carry, diag, valid=None):
            rows = pl.ds(pl.multiple_of(kb * tq, tq), tq)
            dqs, rs, gs = [], [], []
            for hh, cols in enumerate(heads):
                k_blk = k_ref[rows, cols]
                v_blk = v_ref[rows, cols]
                z, e, sp = _sb_scores(qs[hh], k_blk, scale)
                if diag:
                    m = _strict_lower_mask(tq)
                    L = jnp.where(m, -sp, 0.0)
                else:
                    L = -sp
                C = _split_dot(L, us, 2)
                a = jnp.exp(z - sp + C + r_carry[hh])
                if diag:
                    a = jnp.where(m, a, 0.0)
                if valid is not None:
                    a = jnp.where(valid, a, 0.0)
                ab = a.astype(BF16)
                dA = lax.dot_general(d_obs[hh], v_blk, _NT, preferred_element_type=F32)
                G = ab.astype(F32) * dA
                SI = _split_dot(G, ui, 3)
                P = deltas[hh] - (g_carry[hh] + SI)
                r = 1.0 / (1.0 + e)
                er = e * r
                pos = z >= 0
                dz = G * jnp.where(pos, er, r) - P * jnp.where(pos, r, er)
                if diag:
                    dz = jnp.where(m, dz, 0.0)
                if valid is not None:
                    dz = jnp.where(valid, dz, 0.0)
                dzb = (dz * scale).astype(BF16)
                dqs.append(lax.dot_general(dzb, k_blk, (((1,), (0,)), ((), ())), preferred_element_type=F32))
                dk_ref[rows, cols] += lax.dot_general(dzb, qs[hh], _TN, preferred_element_type=F32)
                dv_ref[rows, cols] += lax.dot_general(ab, d_obs[hh], _TN, preferred_element_type=F32)
                rs.append(r_carry[hh] + (C[:, 0:1] + L[:, 0:1]))
                gs.append(g_carry[hh] + SI[:, 0:1])
            return tuple(dqs), tuple(rs), tuple(gs)

        zero = (jnp.zeros((tq, 1), F32),) * HP
        dq_d, r_d, g_d = block(qi, zero, zero, True)
        dq_p, r0, g0 = block(jnp.maximum(qi - 1, 0), r_d, g_d, False, valid=qi > 0)
        dq0 = tuple(a + b for a, b in zip(dq_d, dq_p))

        def cond(st):
            kb, _, _, _, rmax = st
            return jnp.logical_and(kb >= 0, rmax > SKIP_LOG)

        def step(st):
            kb, dq, r, g, _ = st
            dq_part, r2, g2 = block(kb, r, g, False)
            return kb - 1, tuple(a + b for a, b in zip(dq, dq_part)), r2, g2, _max_all(r2)

        _, dqs, _, _, _ = lax.while_loop(cond, step, (qi - 2, dq0, r0, g0, _max_all(r0)))
        for hh, cols in enumerate(heads):
            dq_ref[:, cols] = dqs[hh]

    blk = pl.BlockSpec((tq, WP), lambda h, i: (i, h))
    full = pl.BlockSpec((S, WP), lambda h, i: (0, h))
    vec = pl.BlockSpec((1, WP), lambda h, i: (0, h))
    cst = pl.BlockSpec((tq, tq), lambda h, i: (0, 0))
    act = jax.ShapeDtypeStruct((S, W), F32)
    return _pcall(
        body, grid=(n_heads // HP, NQ),
        in_specs=[blk, full, full, blk,
                  pl.BlockSpec((tq, WP), lambda h, i: (i, d_on_col0 // HP + h)), vec, cst, cst],
        out_specs=(blk, full, full, vec),
        out_shape=(act, act, act, jax.ShapeDtypeStruct((1, W), F32)),
        semantics=("parallel", "arbitrary"), name=name,
        args=(qh, kh, vh, o_pre, d_on, out_g, u_strict, u_incl), side=side)


def _sb_pre_bwd(proj, dqh, dkh, dvh, q_g, k_g, *, n_heads, col0, tm, name):
    S = proj.shape[0]
    H = n_heads
    W = H * HEAD

    def body(q_ref, k_ref, dqh_ref, dkh_ref, dvh_ref, qg_ref, kg_ref,
             dq_ref, dk_ref, dv_ref, dqg_ref, dkg_ref):
        first = jnp.logical_and(pl.program_id(0) == 0, pl.program_id(1) == 0)

        @pl.when(first)
        def _():
            dqg_ref[...] = jnp.zeros_like(dqg_ref)
            dkg_ref[...] = jnp.zeros_like(dkg_ref)

        for src, dh_ref, g_ref, dst, dg_ref in ((q_ref, dqh_ref, qg_ref, dq_ref, dqg_ref),
                                                (k_ref, dkh_ref, kg_ref, dk_ref, dkg_ref)):
            xv = src[...]
            dh = dh_ref[...]
            rstd = lax.rsqrt(jnp.mean(xv * xv, axis=-1, keepdims=True) + EPS)
            n = xv * rstd
            dg_ref[...] += jnp.sum(dh * n, axis=0, keepdims=True)
            dn = dh * g_ref[...]
            dst[...] = (rstd * (dn - n * jnp.mean(dn * n, axis=-1, keepdims=True))).astype(BF16)
        dv_ref[...] = dvh_ref[...].astype(BF16)

    def col(group):
        return pl.BlockSpec((tm, HEAD), lambda i, h: (i, col0 + group * H + h))

    vec = pl.BlockSpec((1, HEAD), lambda i, h: (0, 0))
    own = pl.BlockSpec((tm, HEAD), lambda i, h: (i, h))
    act = jax.ShapeDtypeStruct((S, W), BF16)
    vec_shape = jax.ShapeDtypeStruct((1, HEAD), F32)
    return pl.pallas_call(
        body, grid=(S // tm, H), in_specs=[col(0), col(1), own, own, own, vec, vec],
        out_specs=(own, own, own, vec, vec), out_shape=(act, act, act, vec_shape, vec_shape),
        compiler_params=_params(("arbitrary", "arbitrary")), name=name,
    )(proj, proj, dqh, dkh, dvh, q_g, k_g)


def _softmax_rows(x_ref, L):
    rows = [x_ref[l:l + 1, :] for l in range(L)]
    mx = rows[0]
    for r in rows[1:]:
        mx = jnp.maximum(mx, r)
    ex = [jnp.exp(r - mx) for r in rows]
    tot = ex[0]
    for e in ex[1:]:
        tot = tot + e
    return [e / tot for e in ex]


def _lb_fwd(logits, *, name):
    L, W = logits.shape

    def body(x_ref, o_ref):
        s = _softmax_rows(x_ref, L)
        run = jnp.zeros((1, W), F32)
        for l in range(L):
            run = run + s[l]
            o_ref[l:l + 1, :] = run - s[0]

    return pl.pallas_call(body, out_shape=jax.ShapeDtypeStruct((L, W), F32), name=name)(logits)


def _lb_bwd(logits, dlb_parts, *, name):
    L, W = logits.shape
    P = dlb_parts.shape[0]

    def body(x_ref, d_ref, o_ref):
        s = _softmax_rows(x_ref, L)
        dlb = []
        for l in range(L):
            t = d_ref[0, l:l + 1, :]
            for q in range(1, P):
                t = t + d_ref[q, l:l + 1, :]
            dlb.append(t)
        ds = [None] * L
        run = jnp.zeros((1, W), F32)
        for j in reversed(range(L)):
            run = run + dlb[j]
            ds[j] = run
        ds[0] = jnp.zeros((1, W), F32)
        inner = jnp.zeros((1, W), F32)
        for j in range(L):
            inner = inner + s[j] * ds[j]
        for j in range(L):
            o_ref[j:j + 1, :] = s[j] * (ds[j] - inner)

    return pl.pallas_call(body, out_shape=jax.ShapeDtypeStruct((L, W), F32), name=name)(logits, dlb_parts)


def _ada_mod(c_all, w_ada, *, nb, name):
    L, D, n = w_ada.shape
    B = c_all.shape[0]

    def body(c_ref, w_ref, o_ref, cond_ref):
        cv = c_ref[...]
        s, _ = _sigmoid_pair(cv)
        cond = cv * s
        cond_ref[...] = cond
        o_ref[...] = _bdot(cond, w_ref[...])

    return pl.pallas_call(
        body, grid=(L, n // nb),
        in_specs=[pl.BlockSpec((B, D), lambda l, j: (0, 0)),
                  pl.BlockSpec((None, D, nb), lambda l, j: (l, 0, j))],
        out_specs=(pl.BlockSpec((None, B, nb), lambda l, j: (l, 0, j)),
                   pl.BlockSpec((B, D), lambda l, j: (0, 0))),
        out_shape=(jax.ShapeDtypeStruct((L, B, n), F32), jax.ShapeDtypeStruct((B, D), F32)),
        compiler_params=_params(("arbitrary", "arbitrary")), name=name)(c_all, w_ada)


def _adam_math(w, g, m, v):
    m2 = ADAM_B1 * m + (1.0 - ADAM_B1) * g
    v2 = ADAM_B2 * v + (1.0 - ADAM_B2) * (g * g)
    m_hat = m2 / (1.0 - ADAM_B1 ** ADAM_STEP)
    v_hat = v2 / (1.0 - ADAM_B2 ** ADAM_STEP)
    delta = -ADAM_LR * (m_hat / (jnp.sqrt(v_hat) + ADAM_EPS) + ADAM_WD * w)
    return delta, m2, v2


def _adamw(w, m, v, gparts, *, tr, name):
    R, C = w.shape
    P = gparts.shape[0]

    def body(w_ref, m_ref, v_ref, gp_ref, g_ref, d_ref, m2_ref, v2_ref):
        g = gp_ref[0].astype(F32)
        for p in range(1, P):
            g = g + gp_ref[p].astype(F32)
        delta, m2, v2 = _adam_math(w_ref[...], g, m_ref[...], v_ref[...])
        g_ref[...] = g
        d_ref[...] = delta
        m2_ref[...] = m2
        v2_ref[...] = v2

    tile = pl.BlockSpec((tr, C), lambda i: (i, 0))
    shp = jax.ShapeDtypeStruct((R, C), F32)
    return pl.pallas_call(
        body, grid=(R // tr,),
        in_specs=[tile, tile, tile, pl.BlockSpec((P, tr, C), lambda i: (0, i, 0))],
        out_specs=(tile, tile, tile, tile), out_shape=(shp, shp, shp, shp),
        compiler_params=_params(("parallel",)), name=name)(w, m, v, gparts)


def _adamw_layers(w, m, v, gparts, *, tr, name, side=None):
    L, R, C = w.shape
    P = gparts[0].shape[0]
    nblk = R // tr

    def body(*refs):
        w_ref, m_ref, v_ref = refs[:3]
        gp_refs = refs[3:3 + L]
        g_ref, d_ref, m2_ref, v2_ref = refs[3 + L:]
        layer = pl.program_id(0)
        for t in range(L):
            @pl.when(layer == t)
            def _(t=t):
                g = gp_refs[t][0].astype(F32)
                for q in range(1, P):
                    g = g + gp_refs[t][q].astype(F32)
                delta, m2, v2 = _adam_math(w_ref[...], g, m_ref[...], v_ref[...])
                g_ref[...] = g
                d_ref[...] = delta
                m2_ref[...] = m2
                v2_ref[...] = v2

    def gp_spec(t):
        def index(l, i):
            return (0, jnp.where(l == t, i, jnp.where(l < t, 0, nblk - 1)), 0)
        return pl.BlockSpec((P, tr, C), index)

    tile = pl.BlockSpec((None, tr, C), lambda l, i: (l, i, 0))
    shp = jax.ShapeDtypeStruct((L, R, C), F32)
    return _pcall(
        body, grid=(L, nblk), in_specs=[tile, tile, tile] + [gp_spec(t) for t in range(L)],
        out_specs=(tile, tile, tile, tile), out_shape=(shp, shp, shp, shp),
        semantics=("arbitrary", "arbitrary"), name=name, args=(w, m, v, *gparts), side=side)


def _adamw_ada(w, m, v, cond_t, dmod, *, tr, name):
    L, D, n = w.shape
    Bp = cond_t.shape[1]

    def body(w_ref, m_ref, v_ref, c_ref, dm_ref, g_ref, d_ref, m2_ref, v2_ref):
        g = _bdot(c_ref[...], dm_ref[...])
        delta, m2, v2 = _adam_math(w_ref[...], g, m_ref[...], v_ref[...])
        g_ref[...] = g
        d_ref[...] = delta
        m2_ref[...] = m2
        v2_ref[...] = v2

    tile = pl.BlockSpec((None, tr, n), lambda l, i: (l, i, 0))
    shp = jax.ShapeDtypeStruct((L, D, n), F32)
    return pl.pallas_call(
        body, grid=(L, D // tr),
        in_specs=[tile, tile, tile, pl.BlockSpec((tr, Bp), lambda l, i: (i, 0)),
                  pl.BlockSpec((None, Bp, n), lambda l, i: (l, 0, 0))],
        out_specs=(tile, tile, tile, tile), out_shape=(shp, shp, shp, shp),
        compiler_params=_params(("parallel", "parallel")), name=name)(w, m, v, cond_t, dmod)


def _allgather_small(block, *, name):
    R, C = block.shape

    def body(x_ref, out_ref, send_sems, recv_sems, local_sem):
        x, y, c = lax.axis_index("x"), lax.axis_index("y"), lax.axis_index("c")

        def rows(px, py, pc):
            return out_ref.at[pl.ds((4 * px + 2 * py + pc) * R, R), :]

        mine = pltpu.make_async_copy(x_ref, rows(x, y, c), local_sem)
        mine.start()
        sends = []
        for rel in range(1, N_DEV):
            to = _peer(x, y, c, rel)
            cp = pltpu.make_async_remote_copy(src_ref=x_ref, dst_ref=rows(x, y, c),
                                              send_sem=send_sems.at[rel - 1], recv_sem=recv_sems.at[rel - 1],
                                              device_id=to, device_id_type=MESH)
            cp.start()
            sends.append(cp)
        for rel in range(1, N_DEV):
            frm = _peer(x, y, c, rel)
            pltpu.make_async_remote_copy(src_ref=x_ref, dst_ref=rows(*frm),
                                         send_sem=send_sems.at[rel - 1], recv_sem=recv_sems.at[rel - 1],
                                         device_id=frm, device_id_type=MESH).wait_recv()
        for cp in sends:
            cp.wait_send()
        mine.wait()

    return pl.pallas_call(
        body, out_shape=jax.ShapeDtypeStruct((N_DEV * R, C), block.dtype),
        in_specs=[pl.BlockSpec(memory_space=pltpu.VMEM)],
        out_specs=pl.BlockSpec(memory_space=pltpu.VMEM),
        scratch_shapes=[pltpu.SemaphoreType.DMA((N_DEV - 1,)), pltpu.SemaphoreType.DMA((N_DEV - 1,)),
                        pltpu.SemaphoreType.DMA],
        compiler_params=pltpu.CompilerParams(vmem_limit_bytes=V7X_VMEM_LIMIT), name=name)(block)


def _allgather_hbm(shards, *, name):
    n = len(shards)

    def body(*refs):
        ins = refs[:n]
        outs = refs[n:2 * n]
        send_sems, recv_sems, local_sems = refs[2 * n:]
        x, y, c = lax.axis_index("x"), lax.axis_index("y"), lax.axis_index("c")
        sibling = (x, y, 1 - c)
        chips = [(1 - x, y), (x, 1 - y), (1 - x, 1 - y)]

        def slot(t, px, py, pc):
            return outs[t].at[4 * px + 2 * py + pc]

        def copy(t, k, block, to, src=None):
            return pltpu.make_async_remote_copy(
                src_ref=slot(t, *block) if src is None else src, dst_ref=slot(t, *block),
                send_sem=send_sems.at[t * 7 + k], recv_sem=recv_sems.at[t * 7 + k],
                device_id=to, device_id_type=MESH)

        me = (x, y, c)
        started = []
        mine = []
        for t in range(n):
            cp = pltpu.make_async_copy(ins[t], slot(t, *me), local_sems.at[t])
            cp.start()
            mine.append(cp)
            first = [copy(t, 0, me, sibling, src=ins[t])]
            first += [copy(t, 1 + j, me, (*chip, c), src=ins[t]) for j, chip in enumerate(chips)]
            for cp in first:
                cp.start()
            started += first
        for t in range(n):
            for j, chip in enumerate(chips):
                copy(t, 1 + j, (*chip, c), me).wait_recv()
                fwd = copy(t, 4 + j, (*chip, c), sibling)
                fwd.start()
                started.append(fwd)
        for t in range(n):
            copy(t, 0, sibling, me).wait_recv()
            for j, chip in enumerate(chips):
                copy(t, 4 + j, (*chip, 1 - c), me).wait_recv()
        for cp in started:
            cp.wait_send()
        for cp in mine:
            cp.wait()

    any_spec = pl.BlockSpec(memory_space=pl.ANY)
    return pl.pallas_call(
        body, out_shape=[jax.ShapeDtypeStruct((N_DEV,) + s.shape, s.dtype) for s in shards],
        in_specs=[any_spec] * n, out_specs=[any_spec] * n,
        scratch_shapes=[pltpu.SemaphoreType.DMA((7 * n,)), pltpu.SemaphoreType.DMA((7 * n,)),
                        pltpu.SemaphoreType.DMA((n,))],
        name=name)(*shards)


def _tile(total, want):
    step = 128 if total % 128 == 0 else 8
    best = step
    t = step
    while t <= min(total, want):
        if total % t == 0:
            best = t
        t += step
    return best


def _local_step(x, target, mods, lbs, p, wg, shards=None):
    S, D = x.shape
    L = mods.shape[0]
    W = D // 2
    H = W // HEAD
    F = wg["w_ffn_out"][0].shape[0]
    mesh = shards is not None
    tm = _tile(S, 512)
    tm_big = _tile(S, 1024)
    tm_tn = _tile(S, 2048)
    tm_sw = _tile(S, 128)
    tq = _tile(S, 256)
    cg = max(1, min(8, S // CHUNK))
    nb_out = _tile(D, 1024)
    kb_f = _tile(F, 1408)

    def row(a, l):
        return a[l][None, :]

    def gather_of(l, names):
        if mesh and l < L:
            return (True, [shards[k][l] for k in names])
        return None

    def scatter_of(blocks):
        if mesh and blocks is not None:
            return (False, [b.reshape((N_DEV, -1) + b.shape[-1:]) if b.ndim == 2 else b for b in blocks])
        return None

    saved = []
    xcur = x
    for l in range(L):
        mod = mods[l]
        sh1, sc1, g1, sh2, sc2, g2 = [mod[:, i * D:(i + 1) * D] for i in range(N_MOD)]
        h1 = _modnorm_fwd(xcur, row(p["norm1_g"], l), sc1, sh1, tm=tm, name="norm1_fwd")
        proj, got = _mm_nn(h1, wg["w_in"][l], tm=tm_big, name="proj_fwd", side=gather_of(l + 1, ["w_in"]))
        if got is not None:
            wg["w_in"][l + 1] = got[0]
        lb = lbs[l][None, :]
        (o_hg, on_hg, states), got = _hg_fwd(proj, lb, row(p["hg_out_g"], l), n_heads=H, cg=cg, name="hgrn2_fwd",
                                             side=gather_of(l + 1, ["w_out"]))
        if got is not None:
            wg["w_out"][l + 1] = got[0].reshape(-1, D)
        qh, kh, vh = _sb_pre(proj, row(p["sb_q_g"], l), row(p["sb_k_g"], l), n_heads=H, col0=4 * H,
                             tm=tm_tn, name="sb_qknorm_fwd")
        (o_sb, on_sb), got = _sb_fwd(qh, kh, vh, row(p["sb_out_g"], l), n_heads=H, tq=tq, name="sb_fwd",
                                     side=gather_of(l + 1, ["w_ffn_out"]))
        if got is not None:
            wg["w_ffn_out"][l + 1] = got[0].reshape(-1, D)
        o_cat = jnp.concatenate([on_hg, on_sb], axis=1)
        (x1, mixed), _ = _mm_nn(o_cat, wg["w_out"][l], tm=tm_big, nb=nb_out, resid=xcur, gate=g1,
                                name="out_proj_fwd")
        h2 = _modnorm_fwd(x1, row(p["norm2_g"], l), sc2, sh2, tm=tm, name="norm2_fwd")
        w_fin = wg["w_ffn_in"][l]
        (gate, up, a), got = _ffn_in_fwd(h2, w_fin if w_fin.shape[0] % 2 == 0 else w_fin[0], tm=tm, nb=F,
                                         name="ffn_in_fwd", side=gather_of(l + 1, ["w_ffn_in"]))
        if got is not None:
            wg["w_ffn_in"][l + 1] = got[0]
        (x2, ffn), _ = _mm_nn(a, wg["w_ffn_out"][l], tm=tm, nb=nb_out // 2, resid=x1, gate=g2, name="ffn_out_fwd")
        saved.append(dict(x=xcur, h1=h1, proj=proj, o_hg=o_hg, o_sb=o_sb, states=states, qh=qh, kh=kh, vh=vh,
                          o_cat=o_cat, mixed=mixed, x1=x1, h2=h2, gate=gate, up=up, a=a, ffn=ffn, lb=lb,
                          sc1=sc1, g1=g1, sc2=sc2, g2=g2))
        xcur = x2

    last = saved[-1]
    dx, dffn, dg2, loss = _loss_bwd(xcur, target, last["ffn"], last["g2"], tm=tm, name="loss_bwd")

    big = {k: [None] * L for k in ("w_in", "w_out", "w_ffn_in", "w_ffn_out")}
    small = {k: [None] * L for k in ("norm1_g", "hg_lb", "hg_out_g", "sb_q_g", "sb_k_g", "sb_out_g", "norm2_g")}
    dmods = [None] * L
    kb_d = _tile(D, 1024)
    pending_in = None
    for l in reversed(range(L)):
        sv = saved[l]
        (dgate, dup), got = _ffn_out_bwd_x(dffn, wg["w_ffn_out"][l], sv["gate"], sv["up"], tm=tm, kb=kb_f,
                                           name="ffn_out_bwd_x", side=scatter_of(pending_in))
        if got is not None:
            big["w_in"][l + 1] = got[0]
        g_fout, _ = _mm_tn(sv["a"], dffn, tm=tm_tn, kb=kb_f, nb=nb_out, blocked=False, name="ffn_out_bwd_w")
        w_fin = wg["w_ffn_in"][l]
        even = w_fin.shape[0] % 2 == 0
        dh2, got = _mm_nt(dgate, w_fin if even else w_fin[0], dy2=dup, tm=tm_big, kb=D, nb=F,
                          name="ffn_in_bwd_x", side=scatter_of([g_fout]))
        big["w_ffn_out"][l] = g_fout if got is None else got[0]
        g_fin, _ = _mm_tn(sv["h2"], dgate, dy2=dup, tm=tm_tn, kb=kb_d, nb=w_fin.shape[2] if even else F,
                          blocked=True, name="ffn_in_bwd_w")
        dx1, dmixed, dg1, dsh2, dsc2, dn2 = _modnorm_bwd(
            sv["x1"], dh2, dx, row(p["norm2_g"], l), sv["sc2"], sv["mixed"], sv["g1"], tm=tm_sw * 2,
            name="norm2_bwd")
        small["norm2_g"][l] = dn2
        d_ocat, _ = _mm_nt(dmixed, wg["w_out"][l], tm=tm_big, kb=nb_out, nb=D, name="out_proj_bwd_x")
        g_out, _ = _mm_tn(sv["o_cat"], dmixed, tm=tm_tn, kb=nb_out, nb=nb_out, blocked=False,
                          name="out_proj_bwd_w")
        (dhq, dhf, dhi, dhg, dlb, dhog), _ = _hg_bwd(sv["proj"], sv["o_hg"], d_ocat, 0, sv["states"], sv["lb"],
                                                     row(p["hg_out_g"], l), n_heads=H, cg=cg, name="hgrn2_bwd")
        (dqh, dkh, dvh, dsog), got = _sb_bwd(sv["qh"], sv["kh"], sv["vh"], sv["o_sb"], d_ocat, H,
                                             row(p["sb_out_g"], l), n_heads=H, tq=tq, name="sb_bwd",
                                             side=scatter_of([g_fin]))
        big["w_ffn_in"][l] = g_fin if got is None else got[0]
        dsq, dsk, dsv, dqg, dkg = _sb_pre_bwd(sv["proj"], dqh, dkh, dvh, row(p["sb_q_g"], l),
                                              row(p["sb_k_g"], l), n_heads=H, col0=4 * H, tm=tm_tn,
                                              name="sb_qknorm_bwd")
        small["hg_lb"][l] = dlb
        small["hg_out_g"][l] = dhog
        small["sb_out_g"][l] = dsog
        small["sb_q_g"][l] = dqg
        small["sb_k_g"][l] = dkg
        dproj = jnp.concatenate([dhq, dhf, dhi, dhg, dsq, dsk, dsv], axis=1)
        g_in, _ = _mm_tn(sv["h1"], dproj, tm=tm_tn, kb=kb_d, nb=wg["w_in"][l].shape[2], blocked=True,
                         name="proj_bwd_w")
        big["w_in"][l] = g_in
        dh1, got = _mm_nt(dproj, wg["w_in"][l], tm=tm_big, kb=D, name="proj_bwd_x",
                          side=scatter_of([g_out, g_in] if l == 0 else [g_out]))
        if got is not None:
            big["w_out"][l] = got[0]
            if l == 0:
                big["w_in"][l] = got[1]
        else:
            big["w_out"][l] = g_out
        pending_in = [g_in]
        if l > 0:
            prev = saved[l - 1]
            dx0, dffn_prev, dg2_prev, dsh1, dsc1, dn1 = _modnorm_bwd(
                sv["x"], dh1, dx1, row(p["norm1_g"], l), sv["sc1"], prev["ffn"], prev["g2"], tm=tm_sw * 2,
                name="norm1_bwd")
        else:
            dx0, dsh1, dsc1, dn1 = _modnorm_bwd(sv["x"], dh1, dx1, row(p["norm1_g"], l), sv["sc1"], None, None,
                                                tm=tm_sw * 2, name="norm1_bwd_first")
            dffn_prev, dg2_prev = None, None
        small["norm1_g"][l] = dn1
        dmods[l] = jnp.concatenate([dsh1, dsc1, dg1, dsh2, dsc2, dg2], axis=1)
        dx, dffn, dg2 = dx0, dffn_prev, dg2_prev
    return loss, dx, big, small, dmods


def kernel(x, c, norm1_g, w_in, hg_lb_logits, hg_out_g, sb_q_g, sb_k_g, sb_out_g, w_out, norm2_g, w_ffn_in, w_ffn_out, w_ada, b_ada, loss_target, m_norm1_g, m_w_in, m_hg_lb_logits, m_hg_out_g, m_sb_q_g, m_sb_k_g, m_sb_out_g, m_w_out, m_norm2_g, m_w_ffn_in, m_w_ffn_out, m_w_ada, m_b_ada, v_norm1_g, v_w_in, v_hg_lb_logits, v_hg_out_g, v_sb_q_g, v_sb_k_g, v_sb_out_g, v_w_out, v_norm2_g, v_w_ffn_in, v_w_ffn_out, v_w_ada, v_b_ada):
    L, D = norm1_g.shape
    S = x.shape[1]
    me = 4 * lax.axis_index("x") + 2 * lax.axis_index("y") + lax.axis_index("c")

    c_all = _allgather_small(jnp.broadcast_to(c, (8, D)), name="gather_c").reshape(N_DEV, 8, D)[:, 0, :]
    n_ada = w_ada.shape[2]
    mod_cols, cond = _ada_mod(c_all, w_ada, nb=_tile(n_ada, 512), name="ada_mod")
    mod_all = _allgather_small(mod_cols.reshape(L * N_DEV, n_ada), name="gather_mod")
    mod_all = mod_all.reshape(N_DEV, L, N_DEV, n_ada)
    mod_mine = lax.dynamic_index_in_dim(mod_all, me, axis=2, keepdims=False)
    mods = jnp.transpose(mod_mine, (1, 0, 2)).reshape(L, 1, N_DEV * n_ada) + b_ada[:, None, :]

    lbs = _lb_fwd(hg_lb_logits, name="lower_bounds_fwd")

    shards = dict(w_in=[w_in[l].astype(BF16) for l in range(L)], w_out=[w_out[l].astype(BF16) for l in range(L)],
                  w_ffn_in=[w_ffn_in[l].astype(BF16) for l in range(L)],
                  w_ffn_out=[w_ffn_out[l].astype(BF16) for l in range(L)])
    g_in, g_out, g_fin, g_fout = _allgather_hbm(
        [shards["w_in"][0], shards["w_out"][0], shards["w_ffn_in"][0], shards["w_ffn_out"][0]],
        name="gather_weights")
    wg = dict(w_in=[g_in] + [None] * (L - 1), w_out=[g_out.reshape(-1, D)] + [None] * (L - 1),
              w_ffn_in=[g_fin] + [None] * (L - 1), w_ffn_out=[g_fout.reshape(-1, D)] + [None] * (L - 1))

    p = dict(norm1_g=norm1_g, hg_out_g=hg_out_g, sb_q_g=sb_q_g, sb_k_g=sb_k_g, sb_out_g=sb_out_g,
             norm2_g=norm2_g)
    loss_part, grad_x, recv, small, dmods = _local_step(x.reshape(S, D), loss_target.reshape(S, D), mods, lbs, p,
                                                        wg, shards)

    dmod = jnp.concatenate(dmods, axis=0)
    pieces = [jnp.concatenate(small[k], axis=0) for k in
              ("norm1_g", "hg_lb", "hg_out_g", "sb_q_g", "sb_k_g", "sb_out_g", "norm2_g")] + [dmod]
    flat = jnp.concatenate([a.reshape(-1) for a in pieces] + [loss_part.reshape(-1)])
    n_flat = flat.shape[0]
    rows = -(-n_flat // 1024) * 8
    flat = jnp.pad(flat, (0, rows * 128 - n_flat)).reshape(rows, 128)
    gathered = _allgather_small(flat, name="gather_small_grads").reshape(N_DEV, rows * 128)

    def take(off, shape):
        size = 1
        for s in shape:
            size *= s
        return gathered[:, off:off + size].reshape((N_DEV,) + tuple(shape)), off + size

    off = 0
    parts = {}
    for k, a in zip(("norm1_g", "hg_lb", "hg_out_g", "sb_q_g", "sb_k_g", "sb_out_g", "norm2_g", "dmod"), pieces):
        parts[k], off = take(off, a.shape)
    loss_parts = gathered[:, off:off + 1]
    loss = jnp.sum(loss_parts)

    def pad8(a):
        return jnp.pad(a, ((0, 0), (0, 8 - a.shape[1]), (0, 0)))

    def small_update(w, m, v, gparts):
        Lw = w.shape[0]
        g, d, m2, v2 = _adamw(pad8(w[None])[0], pad8(m[None])[0], pad8(v[None])[0], pad8(gparts),
                              tr=8, name="adamw_small")
        return g[:Lw], d[:Lw], m2[:Lw], v2[:Lw]

    out = {}
    out["norm1_g"] = small_update(norm1_g, m_norm1_g, v_norm1_g, parts["norm1_g"])
    dlogits = _lb_bwd(hg_lb_logits, parts["hg_lb"], name="lower_bounds_bwd")
    out["hg_lb_logits"] = small_update(hg_lb_logits, m_hg_lb_logits, v_hg_lb_logits, dlogits[None])
    out["hg_out_g"] = small_update(hg_out_g, m_hg_out_g, v_hg_out_g, parts["hg_out_g"])
    out["sb_q_g"] = small_update(sb_q_g, m_sb_q_g, v_sb_q_g, parts["sb_q_g"])
    out["sb_k_g"] = small_update(sb_k_g, m_sb_k_g, v_sb_k_g, parts["sb_k_g"])
    out["sb_out_g"] = small_update(sb_out_g, m_sb_out_g, v_sb_out_g, parts["sb_out_g"])
    out["norm2_g"] = small_update(norm2_g, m_norm2_g, v_norm2_g, parts["norm2_g"])
    out["b_ada"] = small_update(b_ada, m_b_ada, v_b_ada, parts["dmod"])

    dmod_all = parts["dmod"].reshape(N_DEV, L, N_DEV, n_ada)
    dmod_mine = lax.dynamic_index_in_dim(dmod_all, me, axis=2, keepdims=False)
    dmod_mine = jnp.pad(jnp.transpose(dmod_mine, (1, 0, 2)), ((0, 0), (0, 128 - N_DEV), (0, 0)))
    cond_t = jnp.pad(jnp.transpose(cond), ((0, 0), (0, 128 - N_DEV)))
    out["w_ada"] = _adamw_ada(w_ada, m_w_ada, v_w_ada, cond_t, dmod_mine, tr=_tile(D, 256), name="adamw_ada")

    def big_update(w, m, v, recv_l, name):
        return _adamw_layers(w, m, v, recv_l, tr=_tile(w.shape[1], 131072 // w.shape[2]), name=name)[0]

    out["w_ffn_in"] = big_update(w_ffn_in, m_w_ffn_in, v_w_ffn_in, recv["w_ffn_in"], "adamw_w_ffn_in")
    out["w_ffn_out"] = big_update(w_ffn_out, m_w_ffn_out, v_w_ffn_out, recv["w_ffn_out"], "adamw_w_ffn_out")
    out["w_out"] = big_update(w_out, m_w_out, v_w_out, recv["w_out"], "adamw_w_out")
    out["w_in"] = big_update(w_in, m_w_in, v_w_in, recv["w_in"], "adamw_w_in")

    order = ("norm1_g", "w_in", "hg_lb_logits", "hg_out_g", "sb_q_g", "sb_k_g", "sb_out_g", "w_out", "norm2_g",
             "w_ffn_in", "w_ffn_out", "w_ada", "b_ada")
    grads = [out[k][0] for k in order]
    deltas = [out[k][1] for k in order]
    new_m = [out[k][2] for k in order]
    new_v = [out[k][3] for k in order]
    return (loss, grad_x.reshape(1, S, D), *grads, *deltas, *new_m, *new_v)
```

```python
import functools

import jax
import jax.numpy as jnp
from jax import lax
from jax.experimental import pallas as pl
from jax.experimental.pallas import tpu as pltpu

F32 = jnp.float32
BF16 = jnp.bfloat16
MESH = pl.DeviceIdType.MESH

N_DEV = 8
HEAD = 128
CHUNK = 64
N_MOD = 6
EPS = 1e-6
TINY = 1e-30
ADAM_LR = 0.001
ADAM_B1 = 0.9
ADAM_B2 = 0.999
ADAM_EPS = 1e-08
ADAM_WD = 0.01
ADAM_STEP = 10
V7X_VMEM_LIMIT = 56 * 1024 * 1024
SKIP_LOG = -104.0


def _params(sem):
    return pltpu.CompilerParams(dimension_semantics=sem, vmem_limit_bytes=V7X_VMEM_LIMIT)


def _bdot(a, b, dims=(((1,), (0,)), ((), ()))):
    return lax.dot_general(a.astype(BF16), b.astype(BF16), dims, preferred_element_type=F32)


_NT = (((1,), (1,)), ((), ()))
_TN = (((0,), (0,)), ((), ()))


def _sigmoid_pair(x):
    e = jnp.exp(-jnp.abs(x))
    r = 1.0 / (1.0 + e)
    er = e * r
    pos = x >= 0
    return jnp.where(pos, r, er), jnp.where(pos, er, r)


def _split_dot(x, u, parts):
    acc = None
    rem = x
    for _ in range(parts):
        p = rem.astype(BF16)
        rem = rem - p.astype(F32)
        t = lax.dot_general(p, u, (((1,), (0,)), ((), ())), preferred_element_type=F32)
        acc = t if acc is None else acc + t
    return acc


def _split_dot_left(u, x, parts):
    acc = None
    rem = x
    for _ in range(parts):
        p = rem.astype(BF16)
        rem = rem - p.astype(F32)
        t = lax.dot_general(u, p, (((1,), (0,)), ((), ())), preferred_element_type=F32)
        acc = t if acc is None else acc + t
    return acc


def _peer(x, y, c, rel):
    return (x ^ ((rel >> 2) & 1), y ^ ((rel >> 1) & 1), c ^ (rel & 1))


def _exchange(gather, srcs, dsts, send_sems, recv_sems, local_sems, phase):
    x, y, c = lax.axis_index("x"), lax.axis_index("y"), lax.axis_index("c")
    me = 4 * x + 2 * y + c
    for t in range(len(srcs)):
        own = srcs[t] if gather else srcs[t].at[me]
        local = pltpu.make_async_copy(own, dsts[t].at[me], local_sems.at[t])
        if phase == "start":
            local.start()
        for rel in range(1, N_DEV):
            px, py, pc = _peer(x, y, c, rel)
            pid = 4 * px + 2 * py + pc
            k = t * (N_DEV - 1) + rel - 1
            if phase == "start":
                pltpu.make_async_remote_copy(
                    src_ref=srcs[t] if gather else srcs[t].at[pid], dst_ref=dsts[t].at[me],
                    send_sem=send_sems.at[k], recv_sem=recv_sems.at[k],
                    device_id=(px, py, pc), device_id_type=MESH).start()
            else:
                cp = pltpu.make_async_remote_copy(
                    src_ref=own, dst_ref=dsts[t].at[pid], send_sem=send_sems.at[k], recv_sem=recv_sems.at[k],
                    device_id=(px, py, pc), device_id_type=MESH)
                cp.wait_recv()
                cp.wait_send()
        if phase == "wait":
            local.wait()


def _exchange_scratch(n):
    return [pltpu.SemaphoreType.DMA(((N_DEV - 1) * n,)), pltpu.SemaphoreType.DMA(((N_DEV - 1) * n,)),
            pltpu.SemaphoreType.DMA((n,))]


def _pcall(body, *, grid, in_specs, out_specs, out_shape, scratch_shapes=(), semantics, name, args, side=None):
    single = not isinstance(out_shape, (tuple, list))
    if single:
        out_specs, out_shape = [out_specs], [out_shape]
    in_specs, out_specs, out_shape = list(in_specs), list(out_specs), list(out_shape)
    scratch_shapes = list(scratch_shapes)
    n_in, n_out, n_scr = len(in_specs), len(out_specs), len(scratch_shapes)
    if side is None:
        res = pl.pallas_call(body, grid=grid, in_specs=in_specs, out_specs=out_specs, out_shape=out_shape,
                             scratch_shapes=scratch_shapes, compiler_params=_params(semantics), name=name)(*args)
        return (res[0] if single else tuple(res)), None
    gather, srcs = side
    n = len(srcs)

    def full(*refs):
        ins = refs[:n_in]
        s_in = refs[n_in:n_in + n]
        outs = refs[n_in + n:n_in + n + n_out]
        s_out = refs[n_in + n + n_out:n_in + 2 * n + n_out]
        scr = refs[n_in + 2 * n + n_out:n_in + 2 * n + n_out + n_scr]
        send_sems, recv_sems, local_sems = refs[n_in + 2 * n + n_out + n_scr:]
        first = pl.program_id(0) == 0
        last = pl.program_id(0) == grid[0] - 1
        for ax in range(1, len(grid)):
            first = jnp.logical_and(first, pl.program_id(ax) == 0)
            last = jnp.logical_and(last, pl.program_id(ax) == grid[ax] - 1)

        @pl.when(first)
        def _():
            _exchange(gather, s_in, s_out, send_sems, recv_sems, local_sems, "start")

        body(*ins, *outs, *scr)

        @pl.when(last)
        def _():
            _exchange(gather, s_in, s_out, send_sems, recv_sems, local_sems, "wait")

    any_spec = pl.BlockSpec(memory_space=pl.ANY)
    s_shapes = [jax.ShapeDtypeStruct(((N_DEV,) + s.shape) if gather else s.shape, s.dtype) for s in srcs]
    res = pl.pallas_call(full, grid=grid, in_specs=in_specs + [any_spec] * n,
                         out_specs=out_specs + [any_spec] * n, out_shape=out_shape + s_shapes,
                         scratch_shapes=scratch_shapes + _exchange_scratch(n),
                         compiler_params=_params(("arbitrary",) * len(grid)), name=name)(*args, *srcs)
    main = res[:n_out]
    return (main[0] if single else tuple(main)), list(res[n_out:])


def _mm_nn(a, b, *, tm, nb=None, out_dtype=F32, resid=None, gate=None, name, side=None, emit_bt=False):
    M, K = a.shape
    if b.ndim == 3:
        NB, _, n = b.shape
        b_spec = pl.BlockSpec((None, K, n), lambda j, i: (j, 0, 0))
        bt_shape = jax.ShapeDtypeStruct((NB, n, K), b.dtype)
        bt_spec = pl.BlockSpec((None, n, K), lambda j, i: (j, 0, 0))
    else:
        n = nb
        NB = b.shape[1] // nb
        b_spec = pl.BlockSpec((K, n), lambda j, i: (0, j))
        bt_shape = jax.ShapeDtypeStruct((NB * n, K), b.dtype)
        bt_spec = pl.BlockSpec((n, K), lambda j, i: (j, 0))
    N = NB * n
    epi = resid is not None

    def body(*refs):
        if emit_bt:
            bt_ref = refs[-1]
            refs = refs[:-1]

            @pl.when(pl.program_id(1) == 0)
            def _():
                bt_ref[...] = b_ref_of(refs)[...].T
        if epi:
            a_ref, b_ref, r_ref, g_ref, o_ref, acc_ref = refs
        else:
            a_ref, b_ref, o_ref = refs
        acc = jnp.dot(a_ref[...], b_ref[...], preferred_element_type=F32)
        if epi:
            o_ref[...] = r_ref[...] + g_ref[...] * acc
            acc_ref[...] = acc.astype(BF16)
        else:
            o_ref[...] = acc.astype(out_dtype)

    def b_ref_of(refs):
        return refs[1]

    in_specs = [pl.BlockSpec((tm, K), lambda j, i: (i, 0)), b_spec]
    args = [a, b]
    o_spec = pl.BlockSpec((tm, n), lambda j, i: (i, j))
    if epi:
        in_specs += [pl.BlockSpec((tm, n), lambda j, i: (i, j)), pl.BlockSpec((1, n), lambda j, i: (0, j))]
        args += [resid, gate]
        out_shape = [jax.ShapeDtypeStruct((M, N), F32), jax.ShapeDtypeStruct((M, N), BF16)]
        out_specs = [o_spec, o_spec]
    else:
        out_shape = [jax.ShapeDtypeStruct((M, N), out_dtype)]
        out_specs = [o_spec]
    if emit_bt:
        out_shape.append(bt_shape)
        out_specs.append(bt_spec)
    res, got = _pcall(body, grid=(NB, M // tm), in_specs=in_specs, out_specs=out_specs, out_shape=out_shape,
                      semantics=("parallel", "arbitrary"), name=name, args=args, side=side)
    return (res[0] if len(res) == 1 else res), got


def _halves(dy, dy2, blk_rows, blk_cols, nblocks, row_of, col_of, last_row=None):
    if dy2 is None:
        return [pl.BlockSpec((blk_rows, blk_cols), lambda *g: (row_of(*g), col_of(*g)))], None
    half = nblocks // 2

    def left(*g):
        r, c = row_of(*g), col_of(*g)
        if last_row is None:
            return (r, jnp.minimum(c, half - 1))
        return (jnp.where(c < half, r, last_row), jnp.minimum(c, half - 1))

    def right(*g):
        r, c = row_of(*g), col_of(*g)
        if last_row is None:
            return (r, jnp.maximum(c - half, 0))
        return (jnp.where(c >= half, r, 0), jnp.maximum(c - half, 0))

    return [pl.BlockSpec((blk_rows, blk_cols), left), pl.BlockSpec((blk_rows, blk_cols), right)], half


def _mm_acc(dy, wt, *, tm, kb, nb=None, name, side=None, dy2=None, wt2=None):
    M = dy.shape[0]
    two = dy2 is not None
    if wt.ndim == 3:
        J, n, Kt = wt.shape

        def w_spec(local):
            return pl.BlockSpec((None, n, kb), lambda i, k, j: (local(j), 0, k))
    else:
        n = nb
        J, Kt = wt.shape[0] // nb, wt.shape[1]

        def w_spec(local):
            return pl.BlockSpec((n, kb), lambda i, k, j: (local(j), k))
    NB = J * (2 if two else 1)
    KB = Kt // kb
    dy_specs, half = _halves(dy, dy2, tm, n, NB, lambda i, k, j: i, lambda i, k, j: j)
    if two:
        w_specs = [w_spec(lambda j: jnp.minimum(j, half - 1)), w_spec(lambda j: jnp.maximum(j - half, 0))]
    else:
        w_specs = [w_spec(lambda j: j)]
    n_op = len(dy_specs)

    def body(*refs):
        dy_refs = refs[:n_op]
        w_refs = refs[n_op:2 * n_op]
        o_ref = refs[2 * n_op]
        acc = refs[2 * n_op + 1:]
        j = pl.program_id(2)

        def use(dy_ref, w_ref):
            part = jnp.dot(dy_ref[...], w_ref[...], preferred_element_type=F32)
            if NB == 1:
                o_ref[...] = part
                return
            acc_ref, = acc

            @pl.when(j == 0)
            def _():
                acc_ref[...] = part

            @pl.when(jnp.logical_and(j > 0, j < NB - 1))
            def _():
                acc_ref[...] += part

            @pl.when(j == NB - 1)
            def _():
                o_ref[...] = acc_ref[...] + part

        if not two:
            use(dy_refs[0], w_refs[0])
        else:
            pl.when(j < half)(lambda: use(dy_refs[0], w_refs[0]))
            pl.when(j >= half)(lambda: use(dy_refs[1], w_refs[1]))

    return _pcall(
        body, grid=(M // tm, KB, NB), in_specs=dy_specs + w_specs,
        out_specs=pl.BlockSpec((tm, kb), lambda i, k, j: (i, k)),
        out_shape=jax.ShapeDtypeStruct((M, Kt), F32),
        scratch_shapes=[] if NB == 1 else [pltpu.VMEM((tm, kb), F32)],
        semantics=("parallel", "parallel", "arbitrary"), name=name,
        args=(dy, dy2, wt, wt2) if two else (dy, wt), side=side)


def _transpose_halves(w, *, name):
    J, K, n = w.shape
    half = J // 2
    kt = _tile(K, 1024)

    def body(a_ref, b_ref, at_ref, bt_ref):
        at_ref[...] = a_ref[...].T
        bt_ref[...] = b_ref[...].T

    o_spec = pl.BlockSpec((None, n, kt), lambda j, k: (j, 0, k))
    shp = jax.ShapeDtypeStruct((half, n, K), w.dtype)
    return pl.pallas_call(
        body, grid=(half, K // kt), in_specs=[pl.BlockSpec((None, kt, n), lambda j, k: (j, k, 0)),
                                              pl.BlockSpec((None, kt, n), lambda j, k: (j + half, k, 0))],
        out_specs=(o_spec, o_spec), out_shape=(shp, shp),
        compiler_params=_params(("parallel", "parallel")), name=name)(w, w)


def _mm_tn(x, dy, *, tm, kb, nb, blocked, name, side=None, dy2=None, x_transposed=False):
    if x_transposed:
        K, M = x.shape
    else:
        M, K = x.shape
    N = dy.shape[1] * (1 if dy2 is None else 2)
    KB, NB, MB = K // kb, N // nb, M // tm
    dy_specs, half = _halves(dy, dy2, tm, nb, NB, lambda k, n, m: m, lambda k, n, m: n, last_row=MB - 1)
    n_dy = len(dy_specs)

    def body(*refs):
        x_ref = refs[0]
        dy_refs = refs[1:1 + n_dy]
        o_ref = refs[1 + n_dy]
        acc = refs[2 + n_dy:]
        m = pl.program_id(2)

        def use(dy_ref):
            if x_transposed:
                part = jnp.dot(x_ref[...], dy_ref[...], preferred_element_type=F32)
            else:
                part = lax.dot_general(x_ref[...], dy_ref[...], _TN, preferred_element_type=F32)
            if MB == 1:
                o_ref[...] = part.astype(BF16)
                return
            acc_ref, = acc

            @pl.when(m == 0)
            def _():
                acc_ref[...] = part

            @pl.when(jnp.logical_and(m > 0, m < MB - 1))
            def _():
                acc_ref[...] += part

            @pl.when(m == MB - 1)
            def _():
                o_ref[...] = (acc_ref[...] + part).astype(BF16)

        if half is None:
            use(dy_refs[0])
        else:
            nblk = pl.program_id(1)
            pl.when(nblk < half)(lambda: use(dy_refs[0]))
            pl.when(nblk >= half)(lambda: use(dy_refs[1]))

    if blocked:
        out_shape = jax.ShapeDtypeStruct((NB, K, nb), BF16)
        o_spec = pl.BlockSpec((None, kb, nb), lambda k, n, m: (n, k, 0))
    else:
        out_shape = jax.ShapeDtypeStruct((K, N), BF16)
        o_spec = pl.BlockSpec((kb, nb), lambda k, n, m: (k, n))
    if x_transposed:
        x_spec = pl.BlockSpec((kb, tm), lambda k, n, m: (k, m))
    else:
        x_spec = pl.BlockSpec((tm, kb), lambda k, n, m: (m, k))
    return _pcall(
        body, grid=(KB, NB, MB), in_specs=[x_spec] + dy_specs,
        out_specs=o_spec, out_shape=out_shape,
        scratch_shapes=[] if MB == 1 else [pltpu.VMEM((kb, nb), F32)],
        semantics=("parallel", "parallel", "arbitrary"), name=name,
        args=(x, dy) if dy2 is None else (x, dy, dy2), side=side)


def _ffn_in_fwd(h, w, *, tm, nb, name, side=None):
    M, K = h.shape
    if w.ndim == 3:
        J, _, n = w.shape
        half = J // 2
        specs = [pl.BlockSpec((None, K, n), lambda j, i: (j, 0, 0)),
                 pl.BlockSpec((None, K, n), lambda j, i: (j + half, 0, 0))]
    else:
        n = nb
        half = w.shape[1] // (2 * nb)
        specs = [pl.BlockSpec((K, n), lambda j, i: (0, j)), pl.BlockSpec((K, n), lambda j, i: (0, j + half))]
    F = half * n

    def body(h_ref, wg_ref, wu_ref, gate_ref, up_ref, act_ref, act_t_ref):
        hv = h_ref[...]
        gate = jnp.dot(hv, wg_ref[...], preferred_element_type=F32)
        up = jnp.dot(hv, wu_ref[...], preferred_element_type=F32)
        s, _ = _sigmoid_pair(gate)
        gate_ref[...] = gate
        up_ref[...] = up
        act = (gate * s * up).astype(BF16)
        act_ref[...] = act
        act_t_ref[...] = act.T

    o_spec = pl.BlockSpec((tm, n), lambda j, i: (i, j))
    f32 = jax.ShapeDtypeStruct((M, F), F32)
    return _pcall(
        body, grid=(half, M // tm), in_specs=[pl.BlockSpec((tm, K), lambda j, i: (i, 0))] + specs,
        out_specs=(o_spec, o_spec, o_spec, pl.BlockSpec((n, tm), lambda j, i: (j, i))),
        out_shape=(f32, f32, jax.ShapeDtypeStruct((M, F), BF16), jax.ShapeDtypeStruct((F, M), BF16)),
        semantics=("parallel", "parallel"), name=name, args=(h, w, w), side=side)


def _ffn_out_bwd_x(dy, wt, gate, up, *, tm, kb, name, side=None):
    M, D = dy.shape
    F = wt.shape[1]

    def body(dy_ref, w_ref, g_ref, u_ref, dg_ref, du_ref):
        da = jnp.dot(dy_ref[...], w_ref[...], preferred_element_type=F32)
        gate = g_ref[...]
        s, ns = _sigmoid_pair(gate)
        dg_ref[...] = (da * u_ref[...] * (s * (1.0 + gate * ns))).astype(BF16)
        du_ref[...] = (da * (gate * s)).astype(BF16)

    tile = pl.BlockSpec((tm, kb), lambda i, k: (i, k))
    act = jax.ShapeDtypeStruct((M, F), BF16)
    return _pcall(
        body, grid=(M // tm, F // kb),
        in_specs=[pl.BlockSpec((tm, D), lambda i, k: (i, 0)), pl.BlockSpec((D, kb), lambda i, k: (0, k)), tile, tile],
        out_specs=(tile, tile), out_shape=(act, act),
        semantics=("parallel", "parallel"), name=name, args=(dy, wt, gate, up), side=side)


def _modnorm_fwd(x, gain, sc, sh, *, tm, name):
    S, D = x.shape

    def body(x_ref, g_ref, sc_ref, sh_ref, h_ref, ht_ref):
        xv = x_ref[...]
        rstd = lax.rsqrt(jnp.mean(xv * xv, axis=-1, keepdims=True) + EPS)
        y = (xv * rstd) * g_ref[...]
        h = (y * (1.0 + sc_ref[...]) + sh_ref[...]).astype(BF16)
        h_ref[...] = h
        ht_ref[...] = h.T

    row = pl.BlockSpec((1, D), lambda i: (0, 0))
    return pl.pallas_call(
        body, grid=(S // tm,),
        in_specs=[pl.BlockSpec((tm, D), lambda i: (i, 0)), row, row, row],
        out_specs=(pl.BlockSpec((tm, D), lambda i: (i, 0)), pl.BlockSpec((D, tm), lambda i: (0, i))),
        out_shape=(jax.ShapeDtypeStruct((S, D), BF16), jax.ShapeDtypeStruct((D, S), BF16)),
        compiler_params=_params(("parallel",)), name=name)(x, gain, sc, sh)


def _modnorm_bwd(x, dh, dres, gain, sc, branch, gate, *, tm, name):
    S, D = x.shape
    has_prev = branch is not None

    def body(*refs):
        if has_prev:
            (x_ref, dh_ref, dr_ref, g_ref, sc_ref, br_ref, gt_ref,
             dx_ref, dbr_ref, dgt_ref, dsh_ref, dsc_ref, dgn_ref) = refs
        else:
            (x_ref, dh_ref, dr_ref, g_ref, sc_ref,
             dx_ref, dsh_ref, dsc_ref, dgn_ref) = refs
        i = pl.program_id(0)
        xv = x_ref[...]
        dh_v = dh_ref[...]
        gv = g_ref[...]
        scale1 = 1.0 + sc_ref[...]
        rstd = lax.rsqrt(jnp.mean(xv * xv, axis=-1, keepdims=True) + EPS)
        n = xv * rstd
        dn = dh_v * (gv * scale1)
        dx = rstd * (dn - n * jnp.mean(dn * n, axis=-1, keepdims=True)) + dr_ref[...]
        dx_ref[...] = dx
        dhn = dh_v * n
        p_sh = jnp.sum(dh_v, axis=0, keepdims=True)
        p_sc = jnp.sum(dhn, axis=0, keepdims=True) * gv
        p_gn = jnp.sum(dhn, axis=0, keepdims=True) * scale1
        if has_prev:
            dbr_ref[...] = (gt_ref[...] * dx).astype(BF16)
            p_gt = jnp.sum(dx * br_ref[...].astype(F32), axis=0, keepdims=True)

        @pl.when(i == 0)
        def _():
            dsh_ref[...] = p_sh
            dsc_ref[...] = p_sc
            dgn_ref[...] = p_gn
            if has_prev:
                dgt_ref[...] = p_gt

        @pl.when(i > 0)
        def _():
            dsh_ref[...] += p_sh
            dsc_ref[...] += p_sc
            dgn_ref[...] += p_gn
            if has_prev:
                dgt_ref[...] += p_gt

    tile = pl.BlockSpec((tm, D), lambda i: (i, 0))
    row = pl.BlockSpec((1, D), lambda i: (0, 0))
    row_shape = jax.ShapeDtypeStruct((1, D), F32)
    if has_prev:
        in_specs = [tile, tile, tile, row, row, tile, row]
        args = (x, dh, dres, gain, sc, branch, gate)
        out_specs = (tile, tile, row, row, row, row)
        out_shape = (jax.ShapeDtypeStruct((S, D), F32), jax.ShapeDtypeStruct((S, D), BF16),
                     row_shape, row_shape, row_shape, row_shape)
    else:
        in_specs = [tile, tile, tile, row, row]
        args = (x, dh, dres, gain, sc)
        out_specs = (tile, row, row, row)
        out_shape = (jax.ShapeDtypeStruct((S, D), F32), row_shape, row_shape, row_shape)
    return pl.pallas_call(body, grid=(S // tm,), in_specs=in_specs, out_specs=out_specs,
                          out_shape=out_shape, compiler_params=_params(("arbitrary",)),
                          name=name)(*args)


def _loss_bwd(y, target, branch, gate, *, tm, name):
    S, D = y.shape
    nsteps = S // tm

    def body(y_ref, t_ref, br_ref, gt_ref, dy_ref, dbr_ref, dgt_ref, loss_ref, col_ref):
        i = pl.program_id(0)
        diff = y_ref[...] - t_ref[...]
        dy = diff * (1.0 / D)
        dy_ref[...] = dy
        dbr_ref[...] = (gt_ref[...] * dy).astype(BF16)
        p_gt = jnp.sum(dy * br_ref[...].astype(F32), axis=0, keepdims=True)
        p_col = jnp.sum(diff * diff, axis=0, keepdims=True)

        @pl.when(i == 0)
        def _():
            dgt_ref[...] = p_gt
            col_ref[...] = p_col

        @pl.when(i > 0)
        def _():
            dgt_ref[...] += p_gt
            col_ref[...] += p_col

        @pl.when(i == nsteps - 1)
        def _():
            tot = jnp.sum(col_ref[...], axis=-1, keepdims=True) * (0.5 / D)
            loss_ref[...] = jnp.broadcast_to(tot, (1, 128))

    tile = pl.BlockSpec((tm, D), lambda i: (i, 0))
    row = pl.BlockSpec((1, D), lambda i: (0, 0))
    return pl.pallas_call(
        body, grid=(nsteps,), in_specs=[tile, tile, tile, row],
        out_specs=(tile, tile, row, pl.BlockSpec((1, 128), lambda i: (0, 0))),
        out_shape=(jax.ShapeDtypeStruct((S, D), F32), jax.ShapeDtypeStruct((S, D), BF16),
                   jax.ShapeDtypeStruct((1, D), F32), jax.ShapeDtypeStruct((1, 128), F32)),
        scratch_shapes=[pltpu.VMEM((1, D), F32)],
        compiler_params=_params(("arbitrary",)), name=name)(y, target, branch, gate)


def _hg_chunk(q, fl, lbv, tri):
    C = q.shape[0]
    sq, nsq = _sigmoid_pair(q)
    qa = q * sq
    sig, nsig = _sigmoid_pair(fl)
    one_lb = 1.0 - lbv
    f = lbv + one_lb * sig
    fc = jnp.maximum(f, TINY)
    lf = jnp.log(fc)
    k = one_lb * nsig
    b = _split_dot_left(tri, lf, 3)
    row = lax.broadcasted_iota(jnp.int32, b.shape, 0)
    bm = jnp.sum(jnp.where(row == C // 2 - 1, b, 0.0), axis=0, keepdims=True)
    bl = jnp.sum(jnp.where(row == C - 1, b, 0.0), axis=0, keepdims=True)
    eb = jnp.exp(b)
    ebm = jnp.exp(b - bm)
    enbm = jnp.exp(bm - b)
    ebl = jnp.exp(bl - b)
    ebL = jnp.exp(bl)

    def operand(t):
        return t.astype(BF16).astype(F32)

    return dict(sq=sq, nsq=nsq, qa=qa, sig=sig, nsig=nsig, one_lb=one_lb, f=f, fc=fc, k=k,
                eb=eb, ebm=ebm, enbm=enbm, ebl=ebl, ebL=ebL,
                Qm=operand(qa * ebm), Km=operand(k * enbm), Qb=operand(qa * eb), Kh=operand(k * ebl), row=row)


def _causal_incl(C):
    r = lax.broadcasted_iota(jnp.int32, (C, C), 0)
    c = lax.broadcasted_iota(jnp.int32, (C, C), 1)
    return r >= c


def _hg_fwd(proj, lb, out_g, *, n_heads, cg, name, side=None):
    S = proj.shape[0]
    H = n_heads
    W = H * HEAD
    T = cg * CHUNK
    NG = S // T
    tri = jnp.tril(jnp.ones((CHUNK, CHUNK), F32)).astype(BF16)

    def body(q_ref, f_ref, v_ref, g_ref, lb_ref, og_ref, tri_ref, o_ref, on_ref, st_ref, s_scr):
        @pl.when(pl.program_id(1) == 0)
        def _():
            s_scr[...] = jnp.zeros_like(s_scr)

        lbv = lb_ref[...]
        ogv = og_ref[...]
        triv = tri_ref[...]
        mask = _causal_incl(CHUNK)
        for c in range(cg):
            rows = pl.ds(c * CHUNK, CHUNK)
            v = v_ref[rows, :]
            gg = g_ref[rows, :]
            cm = _hg_chunk(q_ref[rows, :], f_ref[rows, :], lbv, triv)
            s0 = s_scr[...]
            st_ref[c] = s0
            A = jnp.where(mask, _bdot(cm["Qm"], cm["Km"], _NT), 0.0)
            o = _bdot(A, v) + _bdot(cm["Qb"], s0, _NT)
            s_scr[...] = s0 * cm["ebL"] + _bdot(v, cm["Kh"], _TN)
            o_ref[rows, :] = o
            rstd = lax.rsqrt(jnp.mean(o * o, axis=-1, keepdims=True) + EPS)
            sg, _ = _sigmoid_pair(gg)
            on_ref[rows, :] = (((o * rstd) * ogv) * (gg * sg)).astype(BF16)

    def col(group):
        return pl.BlockSpec((T, HEAD), lambda h, g: (g, group * H + h))

    vec = pl.BlockSpec((1, HEAD), lambda h, g: (0, h))
    return _pcall(
        body, grid=(H, NG),
        in_specs=[col(0), col(1), col(2), col(3), vec, vec,
                  pl.BlockSpec((CHUNK, CHUNK), lambda h, g: (0, 0))],
        out_specs=(pl.BlockSpec((T, HEAD), lambda h, g: (g, h)),
                   pl.BlockSpec((T, HEAD), lambda h, g: (g, h)),
                   pl.BlockSpec((cg, None, HEAD, HEAD), lambda h, g: (g, h, 0, 0))),
        out_shape=(jax.ShapeDtypeStruct((S, W), F32), jax.ShapeDtypeStruct((S, W), BF16),
                   jax.ShapeDtypeStruct((S // CHUNK, H, HEAD, HEAD), F32)),
        scratch_shapes=[pltpu.VMEM((HEAD, HEAD), F32)],
        semantics=("parallel", "arbitrary"), name=name,
        args=(proj, proj, proj, proj, lb, out_g, tri), side=side)


def _hg_bwd(proj, o_pre, d_on, d_on_col0, states, lb, out_g, *, n_heads, cg, name, side=None):
    S = proj.shape[0]
    H = n_heads
    W = H * HEAD
    T = cg * CHUNK
    NG = S // T
    tri = jnp.tril(jnp.ones((CHUNK, CHUNK), F32)).astype(BF16)
    triu = jnp.triu(jnp.ones((CHUNK, CHUNK), F32)).astype(BF16)

    def body(q_ref, f_ref, v_ref, g_ref, o_ref, dy_ref, st_ref, lb_ref, og_ref, tri_ref, triu_ref,
             dq_ref, df_ref, di_ref, dg_ref, dlb_ref, dog_ref, ds_scr):
        gstep = pl.program_id(1)

        @pl.when(gstep == 0)
        def _():
            ds_scr[...] = jnp.zeros_like(ds_scr)
            dlb_ref[...] = jnp.zeros_like(dlb_ref)
            dog_ref[...] = jnp.zeros_like(dog_ref)

        lbv = lb_ref[...]
        ogv = og_ref[...]
        triv = tri_ref[...]
        triuv = triu_ref[...]
        mask = _causal_incl(CHUNK)
        dlb_acc = jnp.zeros((1, HEAD), F32)
        dog_acc = jnp.zeros((1, HEAD), F32)
        for c in reversed(range(cg)):
            rows = pl.ds(c * CHUNK, CHUNK)
            q = q_ref[rows, :]
            v = v_ref[rows, :]
            gg = g_ref[rows, :]
            o = o_ref[rows, :]
            dy = dy_ref[rows, :]
            cm = _hg_chunk(q, f_ref[rows, :], lbv, triv)
            s0 = st_ref[c]
            ds1 = ds_scr[...]
            rstd = lax.rsqrt(jnp.mean(o * o, axis=-1, keepdims=True) + EPS)
            n = o * rstd
            sg, nsg = _sigmoid_pair(gg)
            silu_g = gg * sg
            dyn = dy * n
            dog_acc = dog_acc + jnp.sum(dyn * silu_g, axis=0, keepdims=True)
            dg_ref[rows, :] = (dyn * ogv * (sg * (1.0 + gg * nsg))).astype(BF16)
            dn = dy * (ogv * silu_g)
            d_o = rstd * (dn - n * jnp.mean(dn * n, axis=-1, keepdims=True))
            A = jnp.where(mask, _bdot(cm["Qm"], cm["Km"], _NT), 0.0)
            dA = jnp.where(mask, _bdot(d_o, v, _NT), 0.0)
            dV = _bdot(A, d_o, _TN) + _bdot(cm["Kh"], ds1, _NT)
            dQm = _bdot(dA, cm["Km"])
            dKm = _bdot(dA, cm["Qm"], _TN)
            dQb = _bdot(d_o, s0)
            dKh = _bdot(v, ds1)
            ds_scr[...] = ds1 * cm["ebL"] + _bdot(d_o, cm["Qb"], _TN)
            kh_term = dKh * cm["Kh"]
            db = dQm * cm["Qm"] - dKm * cm["Km"] + dQb * cm["Qb"] - kh_term
            dbl = (jnp.sum(kh_term, axis=0, keepdims=True)
                   + cm["ebL"] * jnp.sum(ds1 * s0, axis=0, keepdims=True))
            db = db + jnp.where(cm["row"] == CHUNK - 1, dbl, 0.0)
            dlf = _split_dot_left(triuv, db, 3)
            dqa = dQm * cm["ebm"] + dQb * cm["eb"]
            dq_ref[rows, :] = (dqa * (cm["sq"] * (1.0 + q * cm["nsq"]))).astype(BF16)
            dk = dKm * cm["enbm"] + dKh * cm["ebl"]
            dfc = jnp.where(cm["f"] > TINY, dlf / cm["fc"], 0.0)
            t = dfc - dk
            df_ref[rows, :] = (t * (cm["one_lb"] * cm["sig"] * cm["nsig"])).astype(BF16)
            dlb_acc = dlb_acc + jnp.sum(t * cm["nsig"], axis=0, keepdims=True)
            di_ref[rows, :] = dV.astype(BF16)
        dlb_ref[...] += dlb_acc
        dog_ref[...] += dog_acc

    def col(group):
        return pl.BlockSpec((T, HEAD), lambda h, g: (NG - 1 - g, group * H + h))

    own = pl.BlockSpec((T, HEAD), lambda h, g: (NG - 1 - g, h))
    vec = pl.BlockSpec((1, HEAD), lambda h, g: (0, h))
    cst = pl.BlockSpec((CHUNK, CHUNK), lambda h, g: (0, 0))
    act = jax.ShapeDtypeStruct((S, W), BF16)
    vec_shape = jax.ShapeDtypeStruct((1, W), F32)
    return _pcall(
        body, grid=(H, NG),
        in_specs=[col(0), col(1), col(2), col(3), own,
                  pl.BlockSpec((T, HEAD), lambda h, g: (NG - 1 - g, d_on_col0 + h)),
                  pl.BlockSpec((cg, None, HEAD, HEAD), lambda h, g: (NG - 1 - g, h, 0, 0)),
                  vec, vec, cst, cst],
        out_specs=(own, own, own, own, vec, vec),
        out_shape=(act, act, act, act, vec_shape, vec_shape),
        scratch_shapes=[pltpu.VMEM((HEAD, HEAD), F32)],
        semantics=("parallel", "arbitrary"), name=name,
        args=(proj, proj, proj, proj, o_pre, d_on, states, lb, out_g, tri, triu), side=side)


def _sb_pre(proj, q_g, k_g, *, n_heads, col0, tm, name):
    S = proj.shape[0]
    H = n_heads
    W = H * HEAD

    def body(q_ref, k_ref, v_ref, qg_ref, kg_ref, qh_ref, kh_ref, vh_ref):
        for src, g_ref, dst in ((q_ref, qg_ref, qh_ref), (k_ref, kg_ref, kh_ref)):
            xv = src[...]
            rstd = lax.rsqrt(jnp.mean(xv * xv, axis=-1, keepdims=True) + EPS)
            dst[...] = ((xv * rstd) * g_ref[...]).astype(BF16)
        vh_ref[...] = v_ref[...].astype(BF16)

    def col(group):
        return pl.BlockSpec((tm, HEAD), lambda i, h: (i, col0 + group * H + h))

    vec = pl.BlockSpec((1, HEAD), lambda i, h: (0, 0))
    own = pl.BlockSpec((tm, HEAD), lambda i, h: (i, h))
    act = jax.ShapeDtypeStruct((S, W), BF16)
    return pl.pallas_call(
        body, grid=(S // tm, H), in_specs=[col(0), col(1), col(2), vec, vec],
        out_specs=(own, own, own), out_shape=(act, act, act),
        compiler_params=_params(("parallel", "parallel")), name=name)(proj, proj, proj, q_g, k_g)


def _sb_scores(q, k_blk, scale):
    z = lax.dot_general(q, k_blk, _NT, preferred_element_type=F32) * scale
    e = jnp.exp(-jnp.abs(z))
    sp = jnp.maximum(z, 0.0) + jnp.log(1.0 + e)
    return z, e, sp


def _heads_per_step(n_heads):
    return 2 if n_heads % 2 == 0 else 1


def _max_all(values):
    m = jnp.max(values[0])
    for v in values[1:]:
        m = jnp.maximum(m, jnp.max(v))
    return m


def _strict_lower_mask(t):
    r = lax.broadcasted_iota(jnp.int32, (t, t), 0)
    c = lax.broadcasted_iota(jnp.int32, (t, t), 1)
    return c < r


def _sb_fwd(qh, kh, vh, out_g, *, n_heads, tq, name, side=None):
    S, W = qh.shape
    HP = _heads_per_step(n_heads)
    WP = HP * HEAD
    NQ = S // tq
    scale = HEAD ** -0.5
    u_strict = jnp.tril(jnp.ones((tq, tq), F32), -1).astype(BF16)

    def body(q_ref, k_ref, v_ref, og_ref, u_ref, o_ref, on_ref):
        qi = pl.program_id(1)
        u = u_ref[...]
        heads = [slice(hh * HEAD, (hh + 1) * HEAD) for hh in range(HP)]
        qs = [q_ref[:, cols] for cols in heads]

        def block(kb, r_carry, diag, valid=None):
            rows = pl.ds(pl.multiple_of(kb * tq, tq), tq)
            pvs, rs = [], []
            for hh, cols in enumerate(heads):
                k_blk = k_ref[rows, cols]
                v_blk = v_ref[rows, cols]
                z, _, sp = _sb_scores(qs[hh], k_blk, scale)
                if diag:
                    m = _strict_lower_mask(tq)
                    L = jnp.where(m, -sp, 0.0)
                else:
                    L = -sp
                C = _split_dot(L, u, 2)
                a = jnp.exp(z - sp + C + r_carry[hh])
                if diag:
                    a = jnp.where(m, a, 0.0)
                if valid is not None:
                    a = jnp.where(valid, a, 0.0)
                pvs.append(lax.dot_general(a.astype(BF16), v_blk, (((1,), (0,)), ((), ())),
                                           preferred_element_type=F32))
                rs.append(r_carry[hh] + (C[:, 0:1] + L[:, 0:1]))
            return tuple(pvs), tuple(rs)

        acc_d, r_d = block(qi, (jnp.zeros((tq, 1), F32),) * HP, True)
        acc_p, r0 = block(jnp.maximum(qi - 1, 0), r_d, False, valid=qi > 0)
        acc0 = tuple(a + b for a, b in zip(acc_d, acc_p))

        def cond(st):
            kb, _, _, rmax = st
            return jnp.logical_and(kb >= 0, rmax > SKIP_LOG)

        def step(st):
            kb, acc, r, _ = st
            pv, r2 = block(kb, r, False)
            return kb - 1, tuple(a + b for a, b in zip(acc, pv)), r2, _max_all(r2)

        _, accs, _, _ = lax.while_loop(cond, step, (qi - 2, acc0, r0, _max_all(r0)))
        for hh, cols in enumerate(heads):
            acc = accs[hh]
            o_ref[:, cols] = acc
            rstd = lax.rsqrt(jnp.mean(acc * acc, axis=-1, keepdims=True) + EPS)
            on_ref[:, cols] = ((acc * rstd) * og_ref[:, cols]).astype(BF16)

    blk = pl.BlockSpec((tq, WP), lambda h, i: (i, h))
    full = pl.BlockSpec((S, WP), lambda h, i: (0, h))
    return _pcall(
        body, grid=(n_heads // HP, NQ),
        in_specs=[blk, full, full, pl.BlockSpec((1, WP), lambda h, i: (0, h)),
                  pl.BlockSpec((tq, tq), lambda h, i: (0, 0))],
        out_specs=(blk, blk),
        out_shape=(jax.ShapeDtypeStruct((S, W), F32), jax.ShapeDtypeStruct((S, W), BF16)),
        semantics=("parallel", "arbitrary"), name=name, args=(qh, kh, vh, out_g, u_strict), side=side)


def _sb_bwd(qh, kh, vh, o_pre, d_on, d_on_col0, out_g, *, n_heads, tq, name, side=None):
    S, W = qh.shape
    HP = _heads_per_step(n_heads)
    WP = HP * HEAD
    assert d_on_col0 % HP == 0
    NQ = S // tq
    scale = HEAD ** -0.5
    u_strict = jnp.tril(jnp.ones((tq, tq), F32), -1).astype(BF16)
    u_incl = jnp.tril(jnp.ones((tq, tq), F32)).astype(BF16)

    def body(q_ref, k_ref, v_ref, o_ref, dy_ref, og_ref, us_ref, ui_ref,
             dq_ref, dk_ref, dv_ref, dog_ref):
        qi = pl.program_id(1)

        @pl.when(qi == 0)
        def _():
            dk_ref[...] = jnp.zeros_like(dk_ref)
            dv_ref[...] = jnp.zeros_like(dv_ref)
            dog_ref[...] = jnp.zeros_like(dog_ref)

        us = us_ref[...]
        ui = ui_ref[...]
        heads = [slice(hh * HEAD, (hh + 1) * HEAD) for hh in range(HP)]
        qs, d_obs, deltas = [], [], []
        for cols in heads:
            qs.append(q_ref[:, cols])
            o = o_ref[:, cols]
            dy = dy_ref[:, cols]
            rstd = lax.rsqrt(jnp.mean(o * o, axis=-1, keepdims=True) + EPS)
            n = o * rstd
            dog_ref[:, cols] += jnp.sum(dy * n, axis=0, keepdims=True)
            dn = dy * og_ref[:, cols]
            d_o = rstd * (dn - n * jnp.mean(dn * n, axis=-1, keepdims=True))
            d_ob = d_o.astype(BF16)
            d_obs.append(d_ob)
            deltas.append(jnp.sum(d_ob.astype(F32) * o, axis=-1, keepdims=True))

        def block(kb, r_carry, g_carry, diag, valid=None):
            rows = pl.ds(pl.multiple_of(kb * tq, tq), tq)
            dqs, rs, gs = [], [], []
            for hh, cols in enumerate(heads):
                k_blk = k_ref[rows, cols]
                v_blk = v_ref[rows, cols]
                z, e, sp = _sb_scores(qs[hh], k_blk, scale)
                if diag:
                    m = _strict_lower_mask(tq)
                    L = jnp.where(m, -sp, 0.0)
                else:
                    L = -sp
                C = _split_dot(L, us, 2)
                a = jnp.exp(z - sp + C + r_carry[hh])
                if diag:
                    a = jnp.where(m, a, 0.0)
                if valid is not None:
                    a = jnp.where(valid, a, 0.0)
                ab = a.astype(BF16)
                dA = lax.dot_general(d_obs[hh], v_blk, _NT, preferred_element_type=F32)
                G = ab.astype(F32) * dA
                SI = _split_dot(G, ui, 3)
                P = deltas[hh] - (g_carry[hh] + SI)
                r = 1.0 / (1.0 + e)
                er = e * r
                pos = z >= 0
                dz = G * jnp.where(pos, er, r) - P * jnp.where(pos, r, er)
                if diag:
                    dz = jnp.where(m, dz, 0.0)
                if valid is not None:
                    dz = jnp.where(valid, dz, 0.0)
                dzb = (dz * scale).astype(BF16)
                dqs.append(lax.dot_general(dzb, k_blk, (((1,), (0,)), ((), ())), preferred_element_type=F32))
                dk_ref[rows, cols] += lax.dot_general(dzb, qs[hh], _TN, preferred_element_type=F32)
                dv_ref[rows, cols] += lax.dot_general(ab, d_obs[hh], _TN, preferred_element_type=F32)
                rs.append(r_carry[hh] + (C[:, 0:1] + L[:, 0:1]))
                gs.append(g_carry[hh] + SI[:, 0:1])
            return tuple(dqs), tuple(rs), tuple(gs)

        zero = (jnp.zeros((tq, 1), F32),) * HP
        dq_d, r_d, g_d = block(qi, zero, zero, True)
        dq_p, r0, g0 = block(jnp.maximum(qi - 1, 0), r_d, g_d, False, valid=qi > 0)
        dq0 = tuple(a + b for a, b in zip(dq_d, dq_p))

        def cond(st):
            kb, _, _, _, rmax = st
            return jnp.logical_and(kb >= 0, rmax > SKIP_LOG)

        def step(st):
            kb, dq, r, g, _ = st
            dq_part, r2, g2 = block(kb, r, g, False)
            return kb - 1, tuple(a + b for a, b in zip(dq, dq_part)), r2, g2, _max_all(r2)

        _, dqs, _, _, _ = lax.while_loop(cond, step, (qi - 2, dq0, r0, g0, _max_all(r0)))
        for hh, cols in enumerate(heads):
            dq_ref[:, cols] = dqs[hh]

    blk = pl.BlockSpec((tq, WP), lambda h, i: (i, h))
    full = pl.BlockSpec((S, WP), lambda h, i: (0, h))
    vec = pl.BlockSpec((1, WP), lambda h, i: (0, h))
    cst = pl.BlockSpec((tq, tq), lambda h, i: (0, 0))
    act = jax.ShapeDtypeStruct((S, W), F32)
    return _pcall(
        body, grid=(n_heads // HP, NQ),
        in_specs=[blk, full, full, blk,
                  pl.BlockSpec((tq, WP), lambda h, i: (i, d_on_col0 // HP + h)), vec, cst, cst],
        out_specs=(blk, full, full, vec),
        out_shape=(act, act, act, jax.ShapeDtypeStruct((1, W), F32)),
        semantics=("parallel", "arbitrary"), name=name,
        args=(qh, kh, vh, o_pre, d_on, out_g, u_strict, u_incl), side=side)


def _sb_pre_bwd(proj, dqh, dkh, dvh, q_g, k_g, *, n_heads, col0, tm, name):
    S = proj.shape[0]
    H = n_heads
    W = H * HEAD

    def body(q_ref, k_ref, dqh_ref, dkh_ref, dvh_ref, qg_ref, kg_ref,
             dq_ref, dk_ref, dv_ref, dqg_ref, dkg_ref):
        first = jnp.logical_and(pl.program_id(0) == 0, pl.program_id(1) == 0)

        @pl.when(first)
        def _():
            dqg_ref[...] = jnp.zeros_like(dqg_ref)
            dkg_ref[...] = jnp.zeros_like(dkg_ref)

        for src, dh_ref, g_ref, dst, dg_ref in ((q_ref, dqh_ref, qg_ref, dq_ref, dqg_ref),
                                                (k_ref, dkh_ref, kg_ref, dk_ref, dkg_ref)):
            xv = src[...]
            dh = dh_ref[...]
            rstd = lax.rsqrt(jnp.mean(xv * xv, axis=-1, keepdims=True) + EPS)
            n = xv * rstd
            dg_ref[...] += jnp.sum(dh * n, axis=0, keepdims=True)
            dn = dh * g_ref[...]
            dst[...] = (rstd * (dn - n * jnp.mean(dn * n, axis=-1, keepdims=True))).astype(BF16)
        dv_ref[...] = dvh_ref[...].astype(BF16)

    def col(group):
        return pl.BlockSpec((tm, HEAD), lambda i, h: (i, col0 + group * H + h))

    vec = pl.BlockSpec((1, HEAD), lambda i, h: (0, 0))
    own = pl.BlockSpec((tm, HEAD), lambda i, h: (i, h))
    act = jax.ShapeDtypeStruct((S, W), BF16)
    vec_shape = jax.ShapeDtypeStruct((1, HEAD), F32)
    return pl.pallas_call(
        body, grid=(S // tm, H), in_specs=[col(0), col(1), own, own, own, vec, vec],
        out_specs=(own, own, own, vec, vec), out_shape=(act, act, act, vec_shape, vec_shape),
        compiler_params=_params(("arbitrary", "arbitrary")), name=name,
    )(proj, proj, dqh, dkh, dvh, q_g, k_g)


def _softmax_rows(x_ref, L):
    rows = [x_ref[l:l + 1, :] for l in range(L)]
    mx = rows[0]
    for r in rows[1:]:
        mx = jnp.maximum(mx, r)
    ex = [jnp.exp(r - mx) for r in rows]
    tot = ex[0]
    for e in ex[1:]:
        tot = tot + e
    return [e / tot for e in ex]


def _lb_fwd(logits, *, name):
    L, W = logits.shape

    def body(x_ref, o_ref):
        s = _softmax_rows(x_ref, L)
        run = jnp.zeros((1, W), F32)
        for l in range(L):
            run = run + s[l]
            o_ref[l:l + 1, :] = run - s[0]

    return pl.pallas_call(body, out_shape=jax.ShapeDtypeStruct((L, W), F32), name=name)(logits)


def _lb_bwd(logits, dlb_parts, *, name):
    L, W = logits.shape
    P = dlb_parts.shape[0]

    def body(x_ref, d_ref, o_ref):
        s = _softmax_rows(x_ref, L)
        dlb = []
        for l in range(L):
            t = d_ref[0, l:l + 1, :]
            for q in range(1, P):
                t = t + d_ref[q, l:l + 1, :]
            dlb.append(t)
        ds = [None] * L
        run = jnp.zeros((1, W), F32)
        for j in reversed(range(L)):
            run = run + dlb[j]
            ds[j] = run
        ds[0] = jnp.zeros((1, W), F32)
        inner = jnp.zeros((1, W), F32)
        for j in range(L):
            inner = inner + s[j] * ds[j]
        for j in range(L):
            o_ref[j:j + 1, :] = s[j] * (ds[j] - inner)

    return pl.pallas_call(body, out_shape=jax.ShapeDtypeStruct((L, W), F32), name=name)(logits, dlb_parts)


def _ada_mod(c_all, w_ada, *, nb, name):
    L, D, n = w_ada.shape
    B = c_all.shape[0]

    def body(c_ref, w_ref, o_ref, cond_ref):
        cv = c_ref[...]
        s, _ = _sigmoid_pair(cv)
        cond = cv * s
        cond_ref[...] = cond
        o_ref[...] = _bdot(cond, w_ref[...])

    return pl.pallas_call(
        body, grid=(L, n // nb),
        in_specs=[pl.BlockSpec((B, D), lambda l, j: (0, 0)),
                  pl.BlockSpec((None, D, nb), lambda l, j: (l, 0, j))],
        out_specs=(pl.BlockSpec((None, B, nb), lambda l, j: (l, 0, j)),
                   pl.BlockSpec((B, D), lambda l, j: (0, 0))),
        out_shape=(jax.ShapeDtypeStruct((L, B, n), F32), jax.ShapeDtypeStruct((B, D), F32)),
        compiler_params=_params(("arbitrary", "arbitrary")), name=name)(c_all, w_ada)


def _adam_math(w, g, m, v):
    m2 = ADAM_B1 * m + (1.0 - ADAM_B1) * g
    v2 = ADAM_B2 * v + (1.0 - ADAM_B2) * (g * g)
    m_hat = m2 / (1.0 - ADAM_B1 ** ADAM_STEP)
    v_hat = v2 / (1.0 - ADAM_B2 ** ADAM_STEP)
    delta = -ADAM_LR * (m_hat / (jnp.sqrt(v_hat) + ADAM_EPS) + ADAM_WD * w)
    return delta, m2, v2


def _adamw(w, m, v, gparts, *, tr, name):
    R, C = w.shape
    P = gparts.shape[0]

    def body(w_ref, m_ref, v_ref, gp_ref, g_ref, d_ref, m2_ref, v2_ref):
        g = gp_ref[0].astype(F32)
        for p in range(1, P):
            g = g + gp_ref[p].astype(F32)
        delta, m2, v2 = _adam_math(w_ref[...], g, m_ref[...], v_ref[...])
        g_ref[...] = g
        d_ref[...] = delta
        m2_ref[...] = m2
        v2_ref[...] = v2

    tile = pl.BlockSpec((tr, C), lambda i: (i, 0))
    shp = jax.ShapeDtypeStruct((R, C), F32)
    return pl.pallas_call(
        body, grid=(R // tr,),
        in_specs=[tile, tile, tile, pl.BlockSpec((P, tr, C), lambda i: (0, i, 0))],
        out_specs=(tile, tile, tile, tile), out_shape=(shp, shp, shp, shp),
        compiler_params=_params(("parallel",)), name=name)(w, m, v, gparts)


def _adamw_layers(w, m, v, gparts, *, tr, name, side=None):
    L, R, C = w.shape
    P = gparts[0].shape[0]
    nblk = R // tr

    def body(*refs):
        w_ref, m_ref, v_ref = refs[:3]
        gp_refs = refs[3:3 + L]
        g_ref, d_ref, m2_ref, v2_ref = refs[3 + L:]
        layer = pl.program_id(0)
        for t in range(L):
            @pl.when(layer == t)
            def _(t=t):
                g = gp_refs[t][0].astype(F32)
                for q in range(1, P):
                    g = g + gp_refs[t][q].astype(F32)
                delta, m2, v2 = _adam_math(w_ref[...], g, m_ref[...], v_ref[...])
                g_ref[...] = g
                d_ref[...] = delta
                m2_ref[...] = m2
                v2_ref[...] = v2

    def gp_spec(t):
        def index(l, i):
            return (0, jnp.where(l == t, i, jnp.where(l < t, 0, nblk - 1)), 0)
        return pl.BlockSpec((P, tr, C), index)

    tile = pl.BlockSpec((None, tr, C), lambda l, i: (l, i, 0))
    shp = jax.ShapeDtypeStruct((L, R, C), F32)
    return _pcall(
        body, grid=(L, nblk), in_specs=[tile, tile, tile] + [gp_spec(t) for t in range(L)],
        out_specs=(tile, tile, tile, tile), out_shape=(shp, shp, shp, shp),
        semantics=("arbitrary", "arbitrary"), name=name, args=(w, m, v, *gparts), side=side)


def _adamw_ada(w, m, v, cond_t, dmod, *, tr, name):
    L, D, n = w.shape
    Bp = cond_t.shape[1]

    def body(w_ref, m_ref, v_ref, c_ref, dm_ref, g_ref, d_ref, m2_ref, v2_ref):
        g = _bdot(c_ref[...], dm_ref[...])
        delta, m2, v2 = _adam_math(w_ref[...], g, m_ref[...], v_ref[...])
        g_ref[...] = g
        d_ref[...] = delta
        m2_ref[...] = m2
        v2_ref[...] = v2

    tile = pl.BlockSpec((None, tr, n), lambda l, i: (l, i, 0))
    shp = jax.ShapeDtypeStruct((L, D, n), F32)
    return pl.pallas_call(
        body, grid=(L, D // tr),
        in_specs=[tile, tile, tile, pl.BlockSpec((tr, Bp), lambda l, i: (i, 0)),
                  pl.BlockSpec((None, Bp, n), lambda l, i: (l, 0, 0))],
        out_specs=(tile, tile, tile, tile), out_shape=(shp, shp, shp, shp),
        compiler_params=_params(("parallel", "parallel")), name=name)(w, m, v, cond_t, dmod)


def _allgather_small(block, *, name):
    R, C = block.shape

    def body(x_ref, out_ref, send_sems, recv_sems, local_sem):
        x, y, c = lax.axis_index("x"), lax.axis_index("y"), lax.axis_index("c")

        def rows(px, py, pc):
            return out_ref.at[pl.ds((4 * px + 2 * py + pc) * R, R), :]

        mine = pltpu.make_async_copy(x_ref, rows(x, y, c), local_sem)
        mine.start()
        sends = []
        for rel in range(1, N_DEV):
            to = _peer(x, y, c, rel)
            cp = pltpu.make_async_remote_copy(src_ref=x_ref, dst_ref=rows(x, y, c),
                                              send_sem=send_sems.at[rel - 1], recv_sem=recv_sems.at[rel - 1],
                                              device_id=to, device_id_type=MESH)
            cp.start()
            sends.append(cp)
        for rel in range(1, N_DEV):
            frm = _peer(x, y, c, rel)
            pltpu.make_async_remote_copy(src_ref=x_ref, dst_ref=rows(*frm),
                                         send_sem=send_sems.at[rel - 1], recv_sem=recv_sems.at[rel - 1],
                                         device_id=frm, device_id_type=MESH).wait_recv()
        for cp in sends:
            cp.wait_send()
        mine.wait()

    return pl.pallas_call(
        body, out_shape=jax.ShapeDtypeStruct((N_DEV * R, C), block.dtype),
        in_specs=[pl.BlockSpec(memory_space=pltpu.VMEM)],
        out_specs=pl.BlockSpec(memory_space=pltpu.VMEM),
        scratch_shapes=[pltpu.SemaphoreType.DMA((N_DEV - 1,)), pltpu.SemaphoreType.DMA((N_DEV - 1,)),
                        pltpu.SemaphoreType.DMA],
        compiler_params=pltpu.CompilerParams(vmem_limit_bytes=V7X_VMEM_LIMIT), name=name)(block)


def _allgather_hbm(shards, *, name):
    n = len(shards)

    def body(*refs):
        ins = refs[:n]
        outs = refs[n:2 * n]
        send_sems, recv_sems, local_sems = refs[2 * n:]
        x, y, c = lax.axis_index("x"), lax.axis_index("y"), lax.axis_index("c")
        sibling = (x, y, 1 - c)
        chips = [(1 - x, y), (x, 1 - y), (1 - x, 1 - y)]

        def slot(t, px, py, pc):
            return outs[t].at[4 * px + 2 * py + pc]

        def copy(t, k, block, to, src=None):
            return pltpu.make_async_remote_copy(
                src_ref=slot(t, *block) if src is None else src, dst_ref=slot(t, *block),
                send_sem=send_sems.at[t * 7 + k], recv_sem=recv_sems.at[t * 7 + k],
                device_id=to, device_id_type=MESH)

        me = (x, y, c)
        started = []
        mine = []
        for t in range(n):
            cp = pltpu.make_async_copy(ins[t], slot(t, *me), local_sems.at[t])
            cp.start()
            mine.append(cp)
            first = [copy(t, 0, me, sibling, src=ins[t])]
            first += [copy(t, 1 + j, me, (*chip, c), src=ins[t]) for j, chip in enumerate(chips)]
            for cp in first:
                cp.start()
            started += first
        for t in range(n):
            for j, chip in enumerate(chips):
                copy(t, 1 + j, (*chip, c), me).wait_recv()
                fwd = copy(t, 4 + j, (*chip, c), sibling)
                fwd.start()
                started.append(fwd)
        for t in range(n):
            copy(t, 0, sibling, me).wait_recv()
            for j, chip in enumerate(chips):
                copy(t, 4 + j, (*chip, 1 - c), me).wait_recv()
        for cp in started:
            cp.wait_send()
        for cp in mine:
            cp.wait()

    any_spec = pl.BlockSpec(memory_space=pl.ANY)
    return pl.pallas_call(
        body, out_shape=[jax.ShapeDtypeStruct((N_DEV,) + s.shape, s.dtype) for s in shards],
        in_specs=[any_spec] * n, out_specs=[any_spec] * n,
        scratch_shapes=[pltpu.SemaphoreType.DMA((7 * n,)), pltpu.SemaphoreType.DMA((7 * n,)),
                        pltpu.SemaphoreType.DMA((n,))],
        name=name)(*shards)


def _tile(total, want):
    step = 128 if total % 128 == 0 else 8
    best = step
    t = step
    while t <= min(total, want):
        if total % t == 0:
            best = t
        t += step
    return best


def _local_step(x, target, mods, lbs, p, wg, shards=None):
    S, D = x.shape
    L = mods.shape[0]
    W = D // 2
    H = W // HEAD
    F = wg["w_ffn_out"][0].shape[0]
    mesh = shards is not None
    tm = _tile(S, 512)
    tm_big = _tile(S, 1024)
    tm_tn = _tile(S, 2048)
    tm_sw = _tile(S, 128)
    tq = _tile(S, 256)
    cg = max(1, min(8, S // CHUNK))
    nb_out = _tile(D, 1024)
    kb_f = _tile(F, 1408)

    def row(a, l):
        return a[l][None, :]

    def gather_of(l, names):
        if mesh and l < L:
            return (True, [shards[k][l] for k in names])
        return None

    def scatter_of(blocks):
        if mesh and blocks is not None:
            return (False, [b.reshape((N_DEV, -1) + b.shape[-1:]) if b.ndim == 2 else b for b in blocks])
        return None

    saved = []
    xcur = x
    for l in range(L):
        mod = mods[l]
        sh1, sc1, g1, sh2, sc2, g2 = [mod[:, i * D:(i + 1) * D] for i in range(N_MOD)]
        h1, h1_t = _modnorm_fwd(xcur, row(p["norm1_g"], l), sc1, sh1, tm=tm, name="norm1_fwd")
        (proj, w_in_t), got = _mm_nn(h1, wg["w_in"][l], tm=tm_big, name="proj_fwd", side=gather_of(l + 1, ["w_in"]),
                                     emit_bt=True)
        if got is not None:
            wg["w_in"][l + 1] = got[0]
        lb = lbs[l][None, :]
        (o_hg, on_hg, states), got = _hg_fwd(proj, lb, row(p["hg_out_g"], l), n_heads=H, cg=cg, name="hgrn2_fwd",
                                             side=gather_of(l + 1, ["w_out"]))
        if got is not None:
            wg["w_out"][l + 1] = got[0].reshape(-1, D)
        qh, kh, vh = _sb_pre(proj, row(p["sb_q_g"], l), row(p["sb_k_g"], l), n_heads=H, col0=4 * H,
                             tm=tm_tn, name="sb_qknorm_fwd")
        (o_sb, on_sb), got = _sb_fwd(qh, kh, vh, row(p["sb_out_g"], l), n_heads=H, tq=tq, name="sb_fwd",
                                     side=gather_of(l + 1, ["w_ffn_out"]))
        if got is not None:
            wg["w_ffn_out"][l + 1] = got[0].reshape(-1, D)
        o_cat = jnp.concatenate([on_hg, on_sb], axis=1)
        (x1, mixed, w_out_t), _ = _mm_nn(o_cat, wg["w_out"][l], tm=tm_big, nb=nb_out, resid=xcur, gate=g1,
                                         name="out_proj_fwd", emit_bt=True)
        h2, h2_t = _modnorm_fwd(x1, row(p["norm2_g"], l), sc2, sh2, tm=tm, name="norm2_fwd")
        w_fin = wg["w_ffn_in"][l]
        if w_fin.shape[0] % 2:
            w_fin = jnp.stack([w_fin[0][:, :F], w_fin[0][:, F:]])
        (gate, up, a, a_t), got = _ffn_in_fwd(h2, w_fin, tm=tm, nb=F, name="ffn_in_fwd",
                                              side=gather_of(l + 1, ["w_ffn_in"]))
        if got is not None:
            wg["w_ffn_in"][l + 1] = got[0]
        w_gate_t, w_up_t = _transpose_halves(w_fin, name="ffn_in_weight_t")
        (x2, ffn, w_fout_t), _ = _mm_nn(a, wg["w_ffn_out"][l], tm=tm, nb=nb_out // 2, resid=x1, gate=g2,
                                        name="ffn_out_fwd", emit_bt=True)
        saved.append(dict(x=xcur, h1_t=h1_t, proj=proj, o_hg=o_hg, o_sb=o_sb, states=states, qh=qh, kh=kh, vh=vh,
                          o_cat=o_cat, mixed=mixed, x1=x1, h2_t=h2_t, gate=gate, up=up, a_t=a_t, ffn=ffn, lb=lb,
                          sc1=sc1, g1=g1, sc2=sc2, g2=g2, w_in_t=w_in_t, w_out_t=w_out_t, w_gate_t=w_gate_t,
                          w_up_t=w_up_t, w_fout_t=w_fout_t))
        xcur = x2

    last = saved[-1]
    dx, dffn, dg2, loss = _loss_bwd(xcur, target, last["ffn"], last["g2"], tm=tm, name="loss_bwd")

    big = {k: [None] * L for k in ("w_in", "w_out", "w_ffn_in", "w_ffn_out")}
    small = {k: [None] * L for k in ("norm1_g", "hg_lb", "hg_out_g", "sb_q_g", "sb_k_g", "sb_out_g", "norm2_g")}
    dmods = [None] * L
    kb_d = _tile(D, 1024)
    pending_in = None
    for l in reversed(range(L)):
        sv = saved[l]
        (dgate, dup), got = _ffn_out_bwd_x(dffn, sv["w_fout_t"], sv["gate"], sv["up"], tm=tm, kb=kb_f,
                                           name="ffn_out_bwd_x", side=scatter_of(pending_in))
        if got is not None:
            big["w_in"][l + 1] = got[0]
        g_fout, _ = _mm_tn(sv["a_t"], dffn, x_transposed=True, tm=tm_tn, kb=kb_f, nb=nb_out, blocked=False,
                           name="ffn_out_bwd_w")
        dh2, got = _mm_acc(dgate, sv["w_gate_t"], dy2=dup, wt2=sv["w_up_t"], tm=tm, kb=D,
                           name="ffn_in_bwd_x", side=scatter_of([g_fout]))
        big["w_ffn_out"][l] = g_fout if got is None else got[0]
        g_fin, _ = _mm_tn(sv["h2_t"], dgate, dy2=dup, x_transposed=True, tm=tm_tn, kb=kb_d,
                          nb=sv["w_gate_t"].shape[1], blocked=True, name="ffn_in_bwd_w")
        dx1, dmixed, dg1, dsh2, dsc2, dn2 = _modnorm_bwd(
            sv["x1"], dh2, dx, row(p["norm2_g"], l), sv["sc2"], sv["mixed"], sv["g1"], tm=tm_sw * 2,
            name="norm2_bwd")
        small["norm2_g"][l] = dn2
        d_ocat, _ = _mm_acc(dmixed, sv["w_out_t"], tm=tm_big, kb=nb_out, nb=D, name="out_proj_bwd_x")
        g_out, _ = _mm_tn(sv["o_cat"], dmixed, tm=tm_tn, kb=nb_out, nb=nb_out, blocked=False,
                          name="out_proj_bwd_w")
        (dhq, dhf, dhi, dhg, dlb, dhog), _ = _hg_bwd(sv["proj"], sv["o_hg"], d_ocat, 0, sv["states"], sv["lb"],
                                                     row(p["hg_out_g"], l), n_heads=H, cg=cg, name="hgrn2_bwd")
        (dqh, dkh, dvh, dsog), got = _sb_bwd(sv["qh"], sv["kh"], sv["vh"], sv["o_sb"], d_ocat, H,
                                             row(p["sb_out_g"], l), n_heads=H, tq=tq, name="sb_bwd",
                                             side=scatter_of([g_fin]))
        big["w_ffn_in"][l] = g_fin if got is None else got[0]
        dsq, dsk, dsv, dqg, dkg = _sb_pre_bwd(sv["proj"], dqh, dkh, dvh, row(p["sb_q_g"], l),
                                              row(p["sb_k_g"], l), n_heads=H, col0=4 * H, tm=tm_tn,
                                              name="sb_qknorm_bwd")
        small["hg_lb"][l] = dlb
        small["hg_out_g"][l] = dhog
        small["sb_out_g"][l] = dsog
        small["sb_q_g"][l] = dqg
        small["sb_k_g"][l] = dkg
        dproj = jnp.concatenate([dhq, dhf, dhi, dhg, dsq, dsk, dsv], axis=1)
        g_in, _ = _mm_tn(sv["h1_t"], dproj, x_transposed=True, tm=tm_tn, kb=kb_d, nb=wg["w_in"][l].shape[2],
                         blocked=True, name="proj_bwd_w")
        big["w_in"][l] = g_in
        dh1, got = _mm_acc(dproj, sv["w_in_t"], tm=tm_big, kb=D, name="proj_bwd_x",
                           side=scatter_of([g_out, g_in] if l == 0 else [g_out]))
        if got is not None:
            big["w_out"][l] = got[0]
            if l == 0:
                big["w_in"][l] = got[1]
        else:
            big["w_out"][l] = g_out
        pending_in = [g_in]
        if l > 0:
            prev = saved[l - 1]
            dx0, dffn_prev, dg2_prev, dsh1, dsc1, dn1 = _modnorm_bwd(
                sv["x"], dh1, dx1, row(p["norm1_g"], l), sv["sc1"], prev["ffn"], prev["g2"], tm=tm_sw * 2,
                name="norm1_bwd")
        else:
            dx0, dsh1, dsc1, dn1 = _modnorm_bwd(sv["x"], dh1, dx1, row(p["norm1_g"], l), sv["sc1"], None, None,
                                                tm=tm_sw * 2, name="norm1_bwd_first")
            dffn_prev, dg2_prev = None, None
        small["norm1_g"][l] = dn1
        dmods[l] = jnp.concatenate([dsh1, dsc1, dg1, dsh2, dsc2, dg2], axis=1)
        dx, dffn, dg2 = dx0, dffn_prev, dg2_prev
    return loss, dx, big, small, dmods


def kernel(x, c, norm1_g, w_in, hg_lb_logits, hg_out_g, sb_q_g, sb_k_g, sb_out_g, w_out, norm2_g, w_ffn_in, w_ffn_out, w_ada, b_ada, loss_target, m_norm1_g, m_w_in, m_hg_lb_logits, m_hg_out_g, m_sb_q_g, m_sb_k_g, m_sb_out_g, m_w_out, m_norm2_g, m_w_ffn_in, m_w_ffn_out, m_w_ada, m_b_ada, v_norm1_g, v_w_in, v_hg_lb_logits, v_hg_out_g, v_sb_q_g, v_sb_k_g, v_sb_out_g, v_w_out, v_norm2_g, v_w_ffn_in, v_w_ffn_out, v_w_ada, v_b_ada):
    L, D = norm1_g.shape
    S = x.shape[1]
    me = 4 * lax.axis_index("x") + 2 * lax.axis_index("y") + lax.axis_index("c")

    c_all = _allgather_small(jnp.broadcast_to(c, (8, D)), name="gather_c").reshape(N_DEV, 8, D)[:, 0, :]
    n_ada = w_ada.shape[2]
    mod_cols, cond = _ada_mod(c_all, w_ada, nb=_tile(n_ada, 512), name="ada_mod")
    mod_all = _allgather_small(mod_cols.reshape(L * N_DEV, n_ada), name="gather_mod")
    mod_all = mod_all.reshape(N_DEV, L, N_DEV, n_ada)
    mod_mine = lax.dynamic_index_in_dim(mod_all, me, axis=2, keepdims=False)
    mods = jnp.transpose(mod_mine, (1, 0, 2)).reshape(L, 1, N_DEV * n_ada) + b_ada[:, None, :]

    lbs = _lb_fwd(hg_lb_logits, name="lower_bounds_fwd")

    shards = dict(w_in=[w_in[l].astype(BF16) for l in range(L)], w_out=[w_out[l].astype(BF16) for l in range(L)],
                  w_ffn_in=[w_ffn_in[l].astype(BF16) for l in range(L)],
                  w_ffn_out=[w_ffn_out[l].astype(BF16) for l in range(L)])
    g_in, g_out, g_fin, g_fout = _allgather_hbm(
        [shards["w_in"][0], shards["w_out"][0], shards["w_ffn_in"][0], shards["w_ffn_out"][0]],
        name="gather_weights")
    wg = dict(w_in=[g_in] + [None] * (L - 1), w_out=[g_out.reshape(-1, D)] + [None] * (L - 1),
              w_ffn_in=[g_fin] + [None] * (L - 1), w_ffn_out=[g_fout.reshape(-1, D)] + [None] * (L - 1))

    p = dict(norm1_g=norm1_g, hg_out_g=hg_out_g, sb_q_g=sb_q_g, sb_k_g=sb_k_g, sb_out_g=sb_out_g,
             norm2_g=norm2_g)
    loss_part, grad_x, recv, small, dmods = _local_step(x.reshape(S, D), loss_target.reshape(S, D), mods, lbs, p,
                                                        wg, shards)

    dmod = jnp.concatenate(dmods, axis=0)
    pieces = [jnp.concatenate(small[k], axis=0) for k in
              ("norm1_g", "hg_lb", "hg_out_g", "sb_q_g", "sb_k_g", "sb_out_g", "norm2_g")] + [dmod]
    flat = jnp.concatenate([a.reshape(-1) for a in pieces] + [loss_part.reshape(-1)])
    n_flat = flat.shape[0]
    rows = -(-n_flat // 1024) * 8
    flat = jnp.pad(flat, (0, rows * 128 - n_flat)).reshape(rows, 128)
    gathered = _allgather_small(flat, name="gather_small_grads").reshape(N_DEV, rows * 128)

    def take(off, shape):
        size = 1
        for s in shape:
            size *= s
        return gathered[:, off:off + size].reshape((N_DEV,) + tuple(shape)), off + size

    off = 0
    parts = {}
    for k, a in zip(("norm1_g", "hg_lb", "hg_out_g", "sb_q_g", "sb_k_g", "sb_out_g", "norm2_g", "dmod"), pieces):
        parts[k], off = take(off, a.shape)
    loss_parts = gathered[:, off:off + 1]
    loss = jnp.sum(loss_parts)

    def pad8(a):
        return jnp.pad(a, ((0, 0), (0, 8 - a.shape[1]), (0, 0)))

    def small_update(w, m, v, gparts):
        Lw = w.shape[0]
        g, d, m2, v2 = _adamw(pad8(w[None])[0], pad8(m[None])[0], pad8(v[None])[0], pad8(gparts),
                              tr=8, name="adamw_small")
        return g[:Lw], d[:Lw], m2[:Lw], v2[:Lw]

    out = {}
    out["norm1_g"] = small_update(norm1_g, m_norm1_g, v_norm1_g, parts["norm1_g"])
    dlogits = _lb_bwd(hg_lb_logits, parts["hg_lb"], name="lower_bounds_bwd")
    out["hg_lb_logits"] = small_update(hg_lb_logits, m_hg_lb_logits, v_hg_lb_logits, dlogits[None])
    out["hg_out_g"] = small_update(hg_out_g, m_hg_out_g, v_hg_out_g, parts["hg_out_g"])
    out["sb_q_g"] = small_update(sb_q_g, m_sb_q_g, v_sb_q_g, parts["sb_q_g"])
    out["sb_k_g"] = small_update(sb_k_g, m_sb_k_g, v_sb_k_g, parts["sb_k_g"])
    out["sb_out_g"] = small_update(sb_out_g, m_sb_out_g, v_sb_out_g, parts["sb_out_g"])
    out["norm2_g"] = small_update(norm2_g, m_norm2_g, v_norm2_g, parts["norm2_g"])
    out["b_ada"] = small_update(b_ada, m_b_ada, v_b_ada, parts["dmod"])

    dmod_all = parts["dmod"].reshape(N_DEV, L, N_DEV, n_ada)
    dmod_mine = lax.dynamic_index_in_dim(dmod_all, me, axis=2, keepdims=False)
    dmod_mine = jnp.pad(jnp.transpose(dmod_mine, (1, 0, 2)), ((0, 0), (0, 128 - N_DEV), (0, 0)))
    cond_t = jnp.pad(jnp.transpose(cond), ((0, 0), (0, 128 - N_DEV)))
    out["w_ada"] = _adamw_ada(w_ada, m_w_ada, v_w_ada, cond_t, dmod_mine, tr=_tile(D, 256), name="adamw_ada")

    def big_update(w, m, v, recv_l, name):
        return _adamw_layers(w, m, v, recv_l, tr=_tile(w.shape[1], 131072 // w.shape[2]), name=name)[0]

    out["w_ffn_in"] = big_update(w_ffn_in, m_w_ffn_in, v_w_ffn_in, recv["w_ffn_in"], "adamw_w_ffn_in")
    out["w_ffn_out"] = big_update(w_ffn_out, m_w_ffn_out, v_w_ffn_out, recv["w_ffn_out"], "adamw_w_ffn_out")
    out["w_out"] = big_update(w_out, m_w_out, v_w_out, recv["w_out"], "adamw_w_out")
    out["w_in"] = big_update(w_in, m_w_in, v_w_in, recv["w_in"], "adamw_w_in")

    order = ("norm1_g", "w_in", "hg_lb_logits", "hg_out_g", "sb_q_g", "sb_k_g", "sb_out_g", "w_out", "norm2_g",
             "w_ffn_in", "w_ffn_out", "w_ada", "b_ada")
    grads = [out[k][0] for k in order]
    deltas = [out[k][1] for k in order]
    new_m = [out[k][2] for k in order]
    new_v = [out[k][3] for k in order]
    return (loss, grad_x.reshape(1, S, D), *grads, *deltas, *new_m, *new_v)
```

```python
import functools

import jax
import jax.numpy as jnp
from jax import lax
from jax.experimental import pallas as pl
from jax.experimental.pallas import tpu as pltpu

F32 = jnp.float32
BF16 = jnp.bfloat16
MESH = pl.DeviceIdType.MESH

N_DEV = 8
HEAD = 128
CHUNK = 64
N_MOD = 6
EPS = 1e-6
TINY = 1e-30
ADAM_LR = 0.001
ADAM_B1 = 0.9
ADAM_B2 = 0.999
ADAM_EPS = 1e-08
ADAM_WD = 0.01
ADAM_STEP = 10
V7X_VMEM_LIMIT = 56 * 1024 * 1024
SKIP_LOG = -104.0


def _params(sem):
    return pltpu.CompilerParams(dimension_semantics=sem, vmem_limit_bytes=V7X_VMEM_LIMIT)


def _bdot(a, b, dims=(((1,), (0,)), ((), ()))):
    return lax.dot_general(a.astype(BF16), b.astype(BF16), dims, preferred_element_type=F32)


_NT = (((1,), (1,)), ((), ()))
_TN = (((0,), (0,)), ((), ()))


def _sigmoid_pair(x):
    e = jnp.exp(-jnp.abs(x))
    r = 1.0 / (1.0 + e)
    er = e * r
    pos = x >= 0
    return jnp.where(pos, r, er), jnp.where(pos, er, r)


def _split_dot(x, u, parts):
    acc = None
    rem = x
    for _ in range(parts):
        p = rem.astype(BF16)
        rem = rem - p.astype(F32)
        t = lax.dot_general(p, u, (((1,), (0,)), ((), ())), preferred_element_type=F32)
        acc = t if acc is None else acc + t
    return acc


def _split_dot_left(u, x, parts):
    acc = None
    rem = x
    for _ in range(parts):
        p = rem.astype(BF16)
        rem = rem - p.astype(F32)
        t = lax.dot_general(u, p, (((1,), (0,)), ((), ())), preferred_element_type=F32)
        acc = t if acc is None else acc + t
    return acc


def _peer(x, y, c, rel):
    return (x ^ ((rel >> 2) & 1), y ^ ((rel >> 1) & 1), c ^ (rel & 1))


def _exchange(gather, srcs, dsts, send_sems, recv_sems, local_sems, phase):
    x, y, c = lax.axis_index("x"), lax.axis_index("y"), lax.axis_index("c")
    me = 4 * x + 2 * y + c
    for t in range(len(srcs)):
        own = srcs[t] if gather else srcs[t].at[me]
        local = pltpu.make_async_copy(own, dsts[t].at[me], local_sems.at[t])
        if phase == "start":
            local.start()
        for rel in range(1, N_DEV):
            px, py, pc = _peer(x, y, c, rel)
            pid = 4 * px + 2 * py + pc
            k = t * (N_DEV - 1) + rel - 1
            if phase == "start":
                pltpu.make_async_remote_copy(
                    src_ref=srcs[t] if gather else srcs[t].at[pid], dst_ref=dsts[t].at[me],
                    send_sem=send_sems.at[k], recv_sem=recv_sems.at[k],
                    device_id=(px, py, pc), device_id_type=MESH).start()
            else:
                cp = pltpu.make_async_remote_copy(
                    src_ref=own, dst_ref=dsts[t].at[pid], send_sem=send_sems.at[k], recv_sem=recv_sems.at[k],
                    device_id=(px, py, pc), device_id_type=MESH)
                cp.wait_recv()
                cp.wait_send()
        if phase == "wait":
            local.wait()


def _exchange_scratch(n):
    return [pltpu.SemaphoreType.DMA(((N_DEV - 1) * n,)), pltpu.SemaphoreType.DMA(((N_DEV - 1) * n,)),
            pltpu.SemaphoreType.DMA((n,))]


def _pcall(body, *, grid, in_specs, out_specs, out_shape, scratch_shapes=(), semantics, name, args, side=None):
    single = not isinstance(out_shape, (tuple, list))
    if single:
        out_specs, out_shape = [out_specs], [out_shape]
    in_specs, out_specs, out_shape = list(in_specs), list(out_specs), list(out_shape)
    scratch_shapes = list(scratch_shapes)
    n_in, n_out, n_scr = len(in_specs), len(out_specs), len(scratch_shapes)
    if side is None:
        res = pl.pallas_call(body, grid=grid, in_specs=in_specs, out_specs=out_specs, out_shape=out_shape,
                             scratch_shapes=scratch_shapes, compiler_params=_params(semantics), name=name)(*args)
        return (res[0] if single else tuple(res)), None
    gather, srcs = side
    n = len(srcs)

    def full(*refs):
        ins = refs[:n_in]
        s_in = refs[n_in:n_in + n]
        outs = refs[n_in + n:n_in + n + n_out]
        s_out = refs[n_in + n + n_out:n_in + 2 * n + n_out]
        scr = refs[n_in + 2 * n + n_out:n_in + 2 * n + n_out + n_scr]
        send_sems, recv_sems, local_sems = refs[n_in + 2 * n + n_out + n_scr:]
        first = pl.program_id(0) == 0
        last = pl.program_id(0) == grid[0] - 1
        for ax in range(1, len(grid)):
            first = jnp.logical_and(first, pl.program_id(ax) == 0)
            last = jnp.logical_and(last, pl.program_id(ax) == grid[ax] - 1)

        @pl.when(first)
        def _():
            _exchange(gather, s_in, s_out, send_sems, recv_sems, local_sems, "start")

        body(*ins, *outs, *scr)

        @pl.when(last)
        def _():
            _exchange(gather, s_in, s_out, send_sems, recv_sems, local_sems, "wait")

    any_spec = pl.BlockSpec(memory_space=pl.ANY)
    s_shapes = [jax.ShapeDtypeStruct(((N_DEV,) + s.shape) if gather else s.shape, s.dtype) for s in srcs]
    res = pl.pallas_call(full, grid=grid, in_specs=in_specs + [any_spec] * n,
                         out_specs=out_specs + [any_spec] * n, out_shape=out_shape + s_shapes,
                         scratch_shapes=scratch_shapes + _exchange_scratch(n),
                         compiler_params=_params(("arbitrary",) * len(grid)), name=name)(*args, *srcs)
    main = res[:n_out]
    return (main[0] if single else tuple(main)), list(res[n_out:])


def _mm_nn(a, b, *, tm, nb=None, out_dtype=F32, resid=None, gate=None, name, side=None):
    M, K = a.shape
    if b.ndim == 3:
        NB, _, n = b.shape
        b_spec = pl.BlockSpec((None, K, n), lambda j, i: (j, 0, 0))
    else:
        n = nb
        NB = b.shape[1] // nb
        b_spec = pl.BlockSpec((K, n), lambda j, i: (0, j))
    N = NB * n
    epi = resid is not None

    def body(*refs):
        if epi:
            a_ref, b_ref, r_ref, g_ref, o_ref, acc_ref = refs
        else:
            a_ref, b_ref, o_ref = refs
        acc = jnp.dot(a_ref[...], b_ref[...], preferred_element_type=F32)
        if epi:
            o_ref[...] = r_ref[...] + g_ref[...] * acc
            acc_ref[...] = acc.astype(BF16)
        else:
            o_ref[...] = acc.astype(out_dtype)

    in_specs = [pl.BlockSpec((tm, K), lambda j, i: (i, 0)), b_spec]
    args = [a, b]
    o_spec = pl.BlockSpec((tm, n), lambda j, i: (i, j))
    if epi:
        in_specs += [pl.BlockSpec((tm, n), lambda j, i: (i, j)), pl.BlockSpec((1, n), lambda j, i: (0, j))]
        args += [resid, gate]
        out_shape = [jax.ShapeDtypeStruct((M, N), F32), jax.ShapeDtypeStruct((M, N), BF16)]
        out_specs = [o_spec, o_spec]
    else:
        out_shape = [jax.ShapeDtypeStruct((M, N), out_dtype)]
        out_specs = [o_spec]
    res, got = _pcall(body, grid=(NB, M // tm), in_specs=in_specs, out_specs=out_specs, out_shape=out_shape,
                      semantics=("parallel", "parallel"), name=name, args=args, side=side)
    return (res[0] if len(res) == 1 else res), got


def _halves(dy, dy2, blk_rows, blk_cols, nblocks, row_of, col_of, last_row=None):
    if dy2 is None:
        return [pl.BlockSpec((blk_rows, blk_cols), lambda *g: (row_of(*g), col_of(*g)))], None
    half = nblocks // 2

    def left(*g):
        r, c = row_of(*g), col_of(*g)
        if last_row is None:
            return (r, jnp.minimum(c, half - 1))
        return (jnp.where(c < half, r, last_row), jnp.minimum(c, half - 1))

    def right(*g):
        r, c = row_of(*g), col_of(*g)
        if last_row is None:
            return (r, jnp.maximum(c - half, 0))
        return (jnp.where(c >= half, r, 0), jnp.maximum(c - half, 0))

    return [pl.BlockSpec((blk_rows, blk_cols), left), pl.BlockSpec((blk_rows, blk_cols), right)], half


def _mm_nt(dy, w, *, tm, kb, nb=None, name, side=None, dy2=None):
    M = dy.shape[0]
    N = dy.shape[1] * (1 if dy2 is None else 2)
    if w.ndim == 3:
        NB, Kt, n = w.shape
        w_spec = pl.BlockSpec((None, kb, n), lambda i, k, j: (j, k, 0))
    else:
        Kt = w.shape[0]
        n = nb
        NB = N // nb
        w_spec = pl.BlockSpec((kb, n), lambda i, k, j: (k, j))
    KB = Kt // kb
    dy_specs, half = _halves(dy, dy2, tm, n, NB, lambda i, k, j: i, lambda i, k, j: j)
    n_op = len(dy_specs)

    def body(*refs):
        dy_refs = refs[:n_op]
        w_ref = refs[n_op]
        o_ref = refs[n_op + 1]
        acc = refs[n_op + 2:]
        j = pl.program_id(2)

        def use(dy_ref):
            part = lax.dot_general(dy_ref[...], w_ref[...], _NT, preferred_element_type=F32)
            if NB == 1:
                o_ref[...] = part
                return
            acc_ref, = acc

            @pl.when(j == 0)
            def _():
                acc_ref[...] = part

            @pl.when(jnp.logical_and(j > 0, j < NB - 1))
            def _():
                acc_ref[...] += part

            @pl.when(j == NB - 1)
            def _():
                o_ref[...] = acc_ref[...] + part

        if half is None:
            use(dy_refs[0])
        else:
            pl.when(j < half)(lambda: use(dy_refs[0]))
            pl.when(j >= half)(lambda: use(dy_refs[1]))

    return _pcall(
        body, grid=(M // tm, KB, NB), in_specs=dy_specs + [w_spec],
        out_specs=pl.BlockSpec((tm, kb), lambda i, k, j: (i, k)),
        out_shape=jax.ShapeDtypeStruct((M, Kt), F32),
        scratch_shapes=[] if NB == 1 else [pltpu.VMEM((tm, kb), F32)],
        semantics=("parallel", "parallel", "arbitrary"), name=name,
        args=(dy, w) if dy2 is None else (dy, dy2, w), side=side)


def _mm_tn(x, dy, *, tm, kb, nb, blocked, name, side=None, dy2=None):
    M, K = x.shape
    N = dy.shape[1] * (1 if dy2 is None else 2)
    KB, NB, MB = K // kb, N // nb, M // tm
    dy_specs, half = _halves(dy, dy2, tm, nb, NB, lambda k, n, m: m, lambda k, n, m: n, last_row=MB - 1)
    n_dy = len(dy_specs)

    def body(*refs):
        x_ref = refs[0]
        dy_refs = refs[1:1 + n_dy]
        o_ref = refs[1 + n_dy]
        acc = refs[2 + n_dy:]
        m = pl.program_id(2)

        def use(dy_ref):
            part = lax.dot_general(x_ref[...], dy_ref[...], _TN, preferred_element_type=F32)
            if MB == 1:
                o_ref[...] = part.astype(BF16)
                return
            acc_ref, = acc

            @pl.when(m == 0)
            def _():
                acc_ref[...] = part

            @pl.when(jnp.logical_and(m > 0, m < MB - 1))
            def _():
                acc_ref[...] += part

            @pl.when(m == MB - 1)
            def _():
                o_ref[...] = (acc_ref[...] + part).astype(BF16)

        if half is None:
            use(dy_refs[0])
        else:
            nblk = pl.program_id(1)
            pl.when(nblk < half)(lambda: use(dy_refs[0]))
            pl.when(nblk >= half)(lambda: use(dy_refs[1]))

    if blocked:
        out_shape = jax.ShapeDtypeStruct((NB, K, nb), BF16)
        o_spec = pl.BlockSpec((None, kb, nb), lambda k, n, m: (n, k, 0))
    else:
        out_shape = jax.ShapeDtypeStruct((K, N), BF16)
        o_spec = pl.BlockSpec((kb, nb), lambda k, n, m: (k, n))
    x_spec = pl.BlockSpec((tm, kb), lambda k, n, m: (m, k))
    return _pcall(
        body, grid=(KB, NB, MB), in_specs=[x_spec] + dy_specs,
        out_specs=o_spec, out_shape=out_shape,
        scratch_shapes=[] if MB == 1 else [pltpu.VMEM((kb, nb), F32)],
        semantics=("parallel", "parallel", "arbitrary"), name=name,
        args=(x, dy) if dy2 is None else (x, dy, dy2), side=side)


def _ffn_in_fwd(h, w, *, tm, nb, name, side=None):
    M, K = h.shape
    if w.ndim == 3:
        J, _, n = w.shape
        half = J // 2
        specs = [pl.BlockSpec((None, K, n), lambda j, i: (j, 0, 0)),
                 pl.BlockSpec((None, K, n), lambda j, i: (j + half, 0, 0))]
    else:
        n = nb
        half = w.shape[1] // (2 * nb)
        specs = [pl.BlockSpec((K, n), lambda j, i: (0, j)), pl.BlockSpec((K, n), lambda j, i: (0, j + half))]
    F = half * n

    def body(h_ref, wg_ref, wu_ref, gate_ref, up_ref, act_ref):
        hv = h_ref[...]
        gate = jnp.dot(hv, wg_ref[...], preferred_element_type=F32)
        up = jnp.dot(hv, wu_ref[...], preferred_element_type=F32)
        s, _ = _sigmoid_pair(gate)
        gate_ref[...] = gate
        up_ref[...] = up
        act_ref[...] = (gate * s * up).astype(BF16)

    o_spec = pl.BlockSpec((tm, n), lambda j, i: (i, j))
    f32 = jax.ShapeDtypeStruct((M, F), F32)
    return _pcall(
        body, grid=(half, M // tm), in_specs=[pl.BlockSpec((tm, K), lambda j, i: (i, 0))] + specs,
        out_specs=(o_spec, o_spec, o_spec), out_shape=(f32, f32, jax.ShapeDtypeStruct((M, F), BF16)),
        semantics=("parallel", "parallel"), name=name, args=(h, w, w), side=side)


def _ffn_out_bwd_x(dy, w, gate, up, *, tm, kb, name, side=None):
    M, D = dy.shape
    F = w.shape[0]

    def body(dy_ref, w_ref, g_ref, u_ref, dg_ref, du_ref):
        da = lax.dot_general(dy_ref[...], w_ref[...], _NT, preferred_element_type=F32)
        gate = g_ref[...]
        s, ns = _sigmoid_pair(gate)
        dg_ref[...] = (da * u_ref[...] * (s * (1.0 + gate * ns))).astype(BF16)
        du_ref[...] = (da * (gate * s)).astype(BF16)

    tile = pl.BlockSpec((tm, kb), lambda i, k: (i, k))
    act = jax.ShapeDtypeStruct((M, F), BF16)
    return _pcall(
        body, grid=(M // tm, F // kb),
        in_specs=[pl.BlockSpec((tm, D), lambda i, k: (i, 0)), pl.BlockSpec((kb, D), lambda i, k: (k, 0)), tile, tile],
        out_specs=(tile, tile), out_shape=(act, act),
        semantics=("parallel", "parallel"), name=name, args=(dy, w, gate, up), side=side)


def _modnorm_fwd(x, gain, sc, sh, *, tm, name):
    S, D = x.shape

    def body(x_ref, g_ref, sc_ref, sh_ref, h_ref):
        xv = x_ref[...]
        rstd = lax.rsqrt(jnp.mean(xv * xv, axis=-1, keepdims=True) + EPS)
        y = (xv * rstd) * g_ref[...]
        h_ref[...] = (y * (1.0 + sc_ref[...]) + sh_ref[...]).astype(BF16)

    row = pl.BlockSpec((1, D), lambda i: (0, 0))
    return pl.pallas_call(
        body, grid=(S // tm,),
        in_specs=[pl.BlockSpec((tm, D), lambda i: (i, 0)), row, row, row],
        out_specs=pl.BlockSpec((tm, D), lambda i: (i, 0)),
        out_shape=jax.ShapeDtypeStruct((S, D), BF16),
        compiler_params=_params(("parallel",)), name=name)(x, gain, sc, sh)


def _modnorm_bwd(x, dh, dres, gain, sc, branch, gate, *, tm, name):
    S, D = x.shape
    has_prev = branch is not None

    def body(*refs):
        if has_prev:
            (x_ref, dh_ref, dr_ref, g_ref, sc_ref, br_ref, gt_ref,
             dx_ref, dbr_ref, dgt_ref, dsh_ref, dsc_ref, dgn_ref) = refs
        else:
            (x_ref, dh_ref, dr_ref, g_ref, sc_ref,
             dx_ref, dsh_ref, dsc_ref, dgn_ref) = refs
        i = pl.program_id(0)
        xv = x_ref[...]
        dh_v = dh_ref[...]
        gv = g_ref[...]
        scale1 = 1.0 + sc_ref[...]
        rstd = lax.rsqrt(jnp.mean(xv * xv, axis=-1, keepdims=True) + EPS)
        n = xv * rstd
        dn = dh_v * (gv * scale1)
        dx = rstd * (dn - n * jnp.mean(dn * n, axis=-1, keepdims=True)) + dr_ref[...]
        dx_ref[...] = dx
        dhn = dh_v * n
        p_sh = jnp.sum(dh_v, axis=0, keepdims=True)
        p_sc = jnp.sum(dhn, axis=0, keepdims=True) * gv
        p_gn = jnp.sum(dhn, axis=0, keepdims=True) * scale1
        if has_prev:
            dbr_ref[...] = (gt_ref[...] * dx).astype(BF16)
            p_gt = jnp.sum(dx * br_ref[...].astype(F32), axis=0, keepdims=True)

        @pl.when(i == 0)
        def _():
            dsh_ref[...] = p_sh
            dsc_ref[...] = p_sc
            dgn_ref[...] = p_gn
            if has_prev:
                dgt_ref[...] = p_gt

        @pl.when(i > 0)
        def _():
            dsh_ref[...] += p_sh
            dsc_ref[...] += p_sc
            dgn_ref[...] += p_gn
            if has_prev:
                dgt_ref[...] += p_gt

    tile = pl.BlockSpec((tm, D), lambda i: (i, 0))
    row = pl.BlockSpec((1, D), lambda i: (0, 0))
    row_shape = jax.ShapeDtypeStruct((1, D), F32)
    if has_prev:
        in_specs = [tile, tile, tile, row, row, tile, row]
        args = (x, dh, dres, gain, sc, branch, gate)
        out_specs = (tile, tile, row, row, row, row)
        out_shape = (jax.ShapeDtypeStruct((S, D), F32), jax.ShapeDtypeStruct((S, D), BF16),
                     row_shape, row_shape, row_shape, row_shape)
    else:
        in_specs = [tile, tile, tile, row, row]
        args = (x, dh, dres, gain, sc)
        out_specs = (tile, row, row, row)
        out_shape = (jax.ShapeDtypeStruct((S, D), F32), row_shape, row_shape, row_shape)
    return pl.pallas_call(body, grid=(S // tm,), in_specs=in_specs, out_specs=out_specs,
                          out_shape=out_shape, compiler_params=_params(("arbitrary",)),
                          name=name)(*args)


def _loss_bwd(y, target, branch, gate, *, tm, name):
    S, D = y.shape
    nsteps = S // tm

    def body(y_ref, t_ref, br_ref, gt_ref, dy_ref, dbr_ref, dgt_ref, loss_ref, col_ref):
        i = pl.program_id(0)
        diff = y_ref[...] - t_ref[...]
        dy = diff * (1.0 / D)
        dy_ref[...] = dy
        dbr_ref[...] = (gt_ref[...] * dy).astype(BF16)
        p_gt = jnp.sum(dy * br_ref[...].astype(F32), axis=0, keepdims=True)
        p_col = jnp.sum(diff * diff, axis=0, keepdims=True)

        @pl.when(i == 0)
        def _():
            dgt_ref[...] = p_gt
            col_ref[...] = p_col

        @pl.when(i > 0)
        def _():
            dgt_ref[...] += p_gt
            col_ref[...] += p_col

        @pl.when(i == nsteps - 1)
        def _():
            tot = jnp.sum(col_ref[...], axis=-1, keepdims=True) * (0.5 / D)
            loss_ref[...] = jnp.broadcast_to(tot, (1, 128))

    tile = pl.BlockSpec((tm, D), lambda i: (i, 0))
    row = pl.BlockSpec((1, D), lambda i: (0, 0))
    return pl.pallas_call(
        body, grid=(nsteps,), in_specs=[tile, tile, tile, row],
        out_specs=(tile, tile, row, pl.BlockSpec((1, 128), lambda i: (0, 0))),
        out_shape=(jax.ShapeDtypeStruct((S, D), F32), jax.ShapeDtypeStruct((S, D), BF16),
                   jax.ShapeDtypeStruct((1, D), F32), jax.ShapeDtypeStruct((1, 128), F32)),
        scratch_shapes=[pltpu.VMEM((1, D), F32)],
        compiler_params=_params(("arbitrary",)), name=name)(y, target, branch, gate)


def _hg_chunk(q, fl, lbv, tri):
    C = q.shape[0]
    sq, nsq = _sigmoid_pair(q)
    qa = q * sq
    sig, nsig = _sigmoid_pair(fl)
    one_lb = 1.0 - lbv
    f = lbv + one_lb * sig
    fc = jnp.maximum(f, TINY)
    lf = jnp.log(fc)
    k = one_lb * nsig
    b = _split_dot_left(tri, lf, 3)
    row = lax.broadcasted_iota(jnp.int32, b.shape, 0)
    bm = jnp.sum(jnp.where(row == C // 2 - 1, b, 0.0), axis=0, keepdims=True)
    bl = jnp.sum(jnp.where(row == C - 1, b, 0.0), axis=0, keepdims=True)
    eb = jnp.exp(b)
    ebm = jnp.exp(b - bm)
    enbm = jnp.exp(bm - b)
    ebl = jnp.exp(bl - b)
    ebL = jnp.exp(bl)

    def operand(t):
        return t.astype(BF16).astype(F32)

    return dict(sq=sq, nsq=nsq, qa=qa, sig=sig, nsig=nsig, one_lb=one_lb, f=f, fc=fc, k=k,
                eb=eb, ebm=ebm, enbm=enbm, ebl=ebl, ebL=ebL,
                Qm=operand(qa * ebm), Km=operand(k * enbm), Qb=operand(qa * eb), Kh=operand(k * ebl), row=row)


def _causal_incl(C):
    r = lax.broadcasted_iota(jnp.int32, (C, C), 0)
    c = lax.broadcasted_iota(jnp.int32, (C, C), 1)
    return r >= c


def _hg_fwd(proj, lb, out_g, *, n_heads, cg, name, side=None):
    S = proj.shape[0]
    H = n_heads
    W = H * HEAD
    T = cg * CHUNK
    NG = S // T
    tri = jnp.tril(jnp.ones((CHUNK, CHUNK), F32)).astype(BF16)

    def body(q_ref, f_ref, v_ref, g_ref, lb_ref, og_ref, tri_ref, o_ref, on_ref, st_ref, s_scr):
        @pl.when(pl.program_id(1) == 0)
        def _():
            s_scr[...] = jnp.zeros_like(s_scr)

        lbv = lb_ref[...]
        ogv = og_ref[...]
        triv = tri_ref[...]
        mask = _causal_incl(CHUNK)
        for c in range(cg):
            rows = pl.ds(c * CHUNK, CHUNK)
            v = v_ref[rows, :]
            gg = g_ref[rows, :]
            cm = _hg_chunk(q_ref[rows, :], f_ref[rows, :], lbv, triv)
            s0 = s_scr[...]
            st_ref[c] = s0
            A = jnp.where(mask, _bdot(cm["Qm"], cm["Km"], _NT), 0.0)
            o = _bdot(A, v) + _bdot(cm["Qb"], s0, _NT)
            s_scr[...] = s0 * cm["ebL"] + _bdot(v, cm["Kh"], _TN)
            o_ref[rows, :] = o
            rstd = lax.rsqrt(jnp.mean(o * o, axis=-1, keepdims=True) + EPS)
            sg, _ = _sigmoid_pair(gg)
            on_ref[rows, :] = (((o * rstd) * ogv) * (gg * sg)).astype(BF16)

    def col(group):
        return pl.BlockSpec((T, HEAD), lambda h, g: (g, group * H + h))

    vec = pl.BlockSpec((1, HEAD), lambda h, g: (0, h))
    return _pcall(
        body, grid=(H, NG),
        in_specs=[col(0), col(1), col(2), col(3), vec, vec,
                  pl.BlockSpec((CHUNK, CHUNK), lambda h, g: (0, 0))],
        out_specs=(pl.BlockSpec((T, HEAD), lambda h, g: (g, h)),
                   pl.BlockSpec((T, HEAD), lambda h, g: (g, h)),
                   pl.BlockSpec((cg, None, HEAD, HEAD), lambda h, g: (g, h, 0, 0))),
        out_shape=(jax.ShapeDtypeStruct((S, W), F32), jax.ShapeDtypeStruct((S, W), BF16),
                   jax.ShapeDtypeStruct((S // CHUNK, H, HEAD, HEAD), F32)),
        scratch_shapes=[pltpu.VMEM((HEAD, HEAD), F32)],
        semantics=("parallel", "arbitrary"), name=name,
        args=(proj, proj, proj, proj, lb, out_g, tri), side=side)


def _hg_bwd(proj, o_pre, d_on, d_on_col0, states, lb, out_g, *, n_heads, cg, name, side=None):
    S = proj.shape[0]
    H = n_heads
    W = H * HEAD
    T = cg * CHUNK
    NG = S // T
    tri = jnp.tril(jnp.ones((CHUNK, CHUNK), F32)).astype(BF16)
    triu = jnp.triu(jnp.ones((CHUNK, CHUNK), F32)).astype(BF16)

    def body(q_ref, f_ref, v_ref, g_ref, o_ref, dy_ref, st_ref, lb_ref, og_ref, tri_ref, triu_ref,
             dq_ref, df_ref, di_ref, dg_ref, dlb_ref, dog_ref, ds_scr):
        gstep = pl.program_id(1)

        @pl.when(gstep == 0)
        def _():
            ds_scr[...] = jnp.zeros_like(ds_scr)
            dlb_ref[...] = jnp.zeros_like(dlb_ref)
            dog_ref[...] = jnp.zeros_like(dog_ref)

        lbv = lb_ref[...]
        ogv = og_ref[...]
        triv = tri_ref[...]
        triuv = triu_ref[...]
        mask = _causal_incl(CHUNK)
        dlb_acc = jnp.zeros((1, HEAD), F32)
        dog_acc = jnp.zeros((1, HEAD), F32)
        for c in reversed(range(cg)):
            rows = pl.ds(c * CHUNK, CHUNK)
            q = q_ref[rows, :]
            v = v_ref[rows, :]
            gg = g_ref[rows, :]
            o = o_ref[rows, :]
            dy = dy_ref[rows, :]
            cm = _hg_chunk(q, f_ref[rows, :], lbv, triv)
            s0 = st_ref[c]
            ds1 = ds_scr[...]
            rstd = lax.rsqrt(jnp.mean(o * o, axis=-1, keepdims=True) + EPS)
            n = o * rstd
            sg, nsg = _sigmoid_pair(gg)
            silu_g = gg * sg
            dyn = dy * n
            dog_acc = dog_acc + jnp.sum(dyn * silu_g, axis=0, keepdims=True)
            dg_ref[rows, :] = (dyn * ogv * (sg * (1.0 + gg * nsg))).astype(BF16)
            dn = dy * (ogv * silu_g)
            d_o = rstd * (dn - n * jnp.mean(dn * n, axis=-1, keepdims=True))
            A = jnp.where(mask, _bdot(cm["Qm"], cm["Km"], _NT), 0.0)
            dA = jnp.where(mask, _bdot(d_o, v, _NT), 0.0)
            dV = _bdot(A, d_o, _TN) + _bdot(cm["Kh"], ds1, _NT)
            dQm = _bdot(dA, cm["Km"])
            dKm = _bdot(dA, cm["Qm"], _TN)
            dQb = _bdot(d_o, s0)
            dKh = _bdot(v, ds1)
            ds_scr[...] = ds1 * cm["ebL"] + _bdot(d_o, cm["Qb"], _TN)
            kh_term = dKh * cm["Kh"]
            db = dQm * cm["Qm"] - dKm * cm["Km"] + dQb * cm["Qb"] - kh_term
            dbl = (jnp.sum(kh_term, axis=0, keepdims=True)
                   + cm["ebL"] * jnp.sum(ds1 * s0, axis=0, keepdims=True))
            db = db + jnp.where(cm["row"] == CHUNK - 1, dbl, 0.0)
            dlf = _split_dot_left(triuv, db, 3)
            dqa = dQm * cm["ebm"] + dQb * cm["eb"]
            dq_ref[rows, :] = (dqa * (cm["sq"] * (1.0 + q * cm["nsq"]))).astype(BF16)
            dk = dKm * cm["enbm"] + dKh * cm["ebl"]
            dfc = jnp.where(cm["f"] > TINY, dlf / cm["fc"], 0.0)
            t = dfc - dk
            df_ref[rows, :] = (t * (cm["one_lb"] * cm["sig"] * cm["nsig"])).astype(BF16)
            dlb_acc = dlb_acc + jnp.sum(t * cm["nsig"], axis=0, keepdims=True)
            di_ref[rows, :] = dV.astype(BF16)
        dlb_ref[...] += dlb_acc
        dog_ref[...] += dog_acc

    def col(group):
        return pl.BlockSpec((T, HEAD), lambda h, g: (NG - 1 - g, group * H + h))

    own = pl.BlockSpec((T, HEAD), lambda h, g: (NG - 1 - g, h))
    vec = pl.BlockSpec((1, HEAD), lambda h, g: (0, h))
    cst = pl.BlockSpec((CHUNK, CHUNK), lambda h, g: (0, 0))
    act = jax.ShapeDtypeStruct((S, W), BF16)
    vec_shape = jax.ShapeDtypeStruct((1, W), F32)
    return _pcall(
        body, grid=(H, NG),
        in_specs=[col(0), col(1), col(2), col(3), own,
                  pl.BlockSpec((T, HEAD), lambda h, g: (NG - 1 - g, d_on_col0 + h)),
                  pl.BlockSpec((cg, None, HEAD, HEAD), lambda h, g: (NG - 1 - g, h, 0, 0)),
                  vec, vec, cst, cst],
        out_specs=(own, own, own, own, vec, vec),
        out_shape=(act, act, act, act, vec_shape, vec_shape),
        scratch_shapes=[pltpu.VMEM((HEAD, HEAD), F32)],
        semantics=("parallel", "arbitrary"), name=name,
        args=(proj, proj, proj, proj, o_pre, d_on, states, lb, out_g, tri, triu), side=side)


def _sb_pre(proj, q_g, k_g, *, n_heads, col0, tm, name):
    S = proj.shape[0]
    H = n_heads
    W = H * HEAD

    def body(q_ref, k_ref, v_ref, qg_ref, kg_ref, qh_ref, kh_ref, vh_ref):
        for src, g_ref, dst in ((q_ref, qg_ref, qh_ref), (k_ref, kg_ref, kh_ref)):
            xv = src[...]
            rstd = lax.rsqrt(jnp.mean(xv * xv, axis=-1, keepdims=True) + EPS)
            dst[...] = ((xv * rstd) * g_ref[...]).astype(BF16)
        vh_ref[...] = v_ref[...].astype(BF16)

    def col(group):
        return pl.BlockSpec((tm, HEAD), lambda i, h: (i, col0 + group * H + h))

    vec = pl.BlockSpec((1, HEAD), lambda i, h: (0, 0))
    own = pl.BlockSpec((tm, HEAD), lambda i, h: (i, h))
    act = jax.ShapeDtypeStruct((S, W), BF16)
    return pl.pallas_call(
        body, grid=(S // tm, H), in_specs=[col(0), col(1), col(2), vec, vec],
        out_specs=(own, own, own), out_shape=(act, act, act),
        compiler_params=_params(("parallel", "parallel")), name=name)(proj, proj, proj, q_g, k_g)


def _sb_scores(q, k_blk, scale):
    z = lax.dot_general(q, k_blk, _NT, preferred_element_type=F32) * scale
    e = jnp.exp(-jnp.abs(z))
    sp = jnp.maximum(z, 0.0) + jnp.log(1.0 + e)
    return z, e, sp


def _heads_per_step(n_heads):
    return 2 if n_heads % 2 == 0 else 1


def _max_all(values):
    m = jnp.max(values[0])
    for v in values[1:]:
        m = jnp.maximum(m, jnp.max(v))
    return m


def _strict_lower_mask(t):
    r = lax.broadcasted_iota(jnp.int32, (t, t), 0)
    c = lax.broadcasted_iota(jnp.int32, (t, t), 1)
    return c < r


def _sb_fwd(qh, kh, vh, out_g, *, n_heads, tq, name, side=None):
    S, W = qh.shape
    HP = _heads_per_step(n_heads)
    WP = HP * HEAD
    NQ = S // tq
    scale = HEAD ** -0.5
    u_strict = jnp.tril(jnp.ones((tq, tq), F32), -1).astype(BF16)

    def body(q_ref, k_ref, v_ref, og_ref, u_ref, o_ref, on_ref):
        qi = pl.program_id(1)
        u = u_ref[...]
        heads = [slice(hh * HEAD, (hh + 1) * HEAD) for hh in range(HP)]
        qs = [q_ref[:, cols] for cols in heads]

        def block(kb, r_carry, diag, valid=None):
            rows = pl.ds(pl.multiple_of(kb * tq, tq), tq)
            pvs, rs = [], []
            for hh, cols in enumerate(heads):
                k_blk = k_ref[rows, cols]
                v_blk = v_ref[rows, cols]
                z, _, sp = _sb_scores(qs[hh], k_blk, scale)
                if diag:
                    m = _strict_lower_mask(tq)
                    L = jnp.where(m, -sp, 0.0)
                else:
                    L = -sp
                C = _split_dot(L, u, 2)
                a = jnp.exp(z - sp + C + r_carry[hh])
                if diag:
                    a = jnp.where(m, a, 0.0)
                if valid is not None:
                    a = jnp.where(valid, a, 0.0)
                pvs.append(lax.dot_general(a.astype(BF16), v_blk, (((1,), (0,)), ((), ())),
                                           preferred_element_type=F32))
                rs.append(r_carry[hh] + (C[:, 0:1] + L[:, 0:1]))
            return tuple(pvs), tuple(rs)

        acc_d, r_d = block(qi, (jnp.zeros((tq, 1), F32),) * HP, True)
        acc_p, r0 = block(jnp.maximum(qi - 1, 0), r_d, False, valid=qi > 0)
        acc0 = tuple(a + b for a, b in zip(acc_d, acc_p))

        def cond(st):
            kb, _, _, rmax = st
            return jnp.logical_and(kb >= 0, rmax > SKIP_LOG)

        def step(st):
            kb, acc, r, _ = st
            pv, r2 = block(kb, r, False)
            return kb - 1, tuple(a + b for a, b in zip(acc, pv)), r2, _max_all(r2)

        _, accs, _, _ = lax.while_loop(cond, step, (qi - 2, acc0, r0, _max_all(r0)))
        for hh, cols in enumerate(heads):
            acc = accs[hh]
            o_ref[:, cols] = acc
            rstd = lax.rsqrt(jnp.mean(acc * acc, axis=-1, keepdims=True) + EPS)
            on_ref[:, cols] = ((acc * rstd) * og_ref[:, cols]).astype(BF16)

    blk = pl.BlockSpec((tq, WP), lambda h, i: (i, h))
    full = pl.BlockSpec((S, WP), lambda h, i: (0, h))
    return _pcall(
        body, grid=(n_heads // HP, NQ),
        in_specs=[blk, full, full, pl.BlockSpec((1, WP), lambda h, i: (0, h)),
                  pl.BlockSpec((tq, tq), lambda h, i: (0, 0))],
        out_specs=(blk, blk),
        out_shape=(jax.ShapeDtypeStruct((S, W), F32), jax.ShapeDtypeStruct((S, W), BF16)),
        semantics=("parallel", "arbitrary"), name=name, args=(qh, kh, vh, out_g, u_strict), side=side)


def _sb_bwd(qh, kh, vh, o_pre, d_on, d_on_col0, out_g, *, n_heads, tq, name, side=None):
    S, W = qh.shape
    HP = _heads_per_step(n_heads)
    WP = HP * HEAD
    assert d_on_col0 % HP == 0
    NQ = S // tq
    scale = HEAD ** -0.5
    u_strict = jnp.tril(jnp.ones((tq, tq), F32), -1).astype(BF16)
    u_incl = jnp.tril(jnp.ones((tq, tq), F32)).astype(BF16)

    def body(q_ref, k_ref, v_ref, o_ref, dy_ref, og_ref, us_ref, ui_ref,
             dq_ref, dk_ref, dv_ref, dog_ref):
        qi = pl.program_id(1)

        @pl.when(qi == 0)
        def _():
            dk_ref[...] = jnp.zeros_like(dk_ref)
            dv_ref[...] = jnp.zeros_like(dv_ref)
            dog_ref[...] = jnp.zeros_like(dog_ref)

        us = us_ref[...]
        ui = ui_ref[...]
        heads = [slice(hh * HEAD, (hh + 1) * HEAD) for hh in range(HP)]
        qs, d_obs, deltas = [], [], []
        for cols in heads:
            qs.append(q_ref[:, cols])
            o = o_ref[:, cols]
            dy = dy_ref[:, cols]
            rstd = lax.rsqrt(jnp.mean(o * o, axis=-1, keepdims=True) + EPS)
            n = o * rstd
            dog_ref[:, cols] += jnp.sum(dy * n, axis=0, keepdims=True)
            dn = dy * og_ref[:, cols]
            d_o = rstd * (dn - n * jnp.mean(dn * n, axis=-1, keepdims=True))
            d_ob = d_o.astype(BF16)
            d_obs.append(d_ob)
            deltas.append(jnp.sum(d_ob.astype(F32) * o, axis=-1, keepdims=True))

        def block(kb, r_carry, g_carry, diag, valid=None):
            rows = pl.ds(pl.multiple_of(kb * tq, tq), tq)
            dqs, rs, gs = [], [], []
            for hh, cols in enumerate(heads):
                k_blk = k_ref[rows, cols]
                v_blk = v_ref[rows, cols]
                z, e, sp = _sb_scores(qs[hh], k_blk, scale)
                if diag:
                    m = _strict_lower_mask(tq)
                    L = jnp.where(m, -sp, 0.0)
                else:
                    L = -sp
                C = _split_dot(L, us, 2)
                a = jnp.exp(z - sp + C + r_carry[hh])
                if diag:
                    a = jnp.where(m, a, 0.0)
                if valid is not None:
                    a = jnp.where(valid, a, 0.0)
                ab = a.astype(BF16)
                dA = lax.dot_general(d_obs[hh], v_blk, _NT, preferred_element_type=F32)
                G = ab.astype(F32) * dA
                SI = _split_dot(G, ui, 2)
                P = deltas[hh] - (g_carry[hh] + SI)
                r = 1.0 / (1.0 + e)
                sig = jnp.where(z >= 0, r, e * r)
                dz = G - (G + P) * sig
                if diag:
                    dz = jnp.where(m, dz, 0.0)
                if valid is not None:
                    dz = jnp.where(valid, dz, 0.0)
                dzb = (dz * scale).astype(BF16)
                dqs.append(lax.dot_general(dzb, k_blk, (((1,), (0,)), ((), ())), preferred_element_type=F32))
                dk_ref[rows, cols] += lax.dot_general(dzb, qs[hh], _TN, preferred_element_type=F32)
                dv_ref[rows, cols] += lax.dot_general(ab, d_obs[hh], _TN, preferred_element_type=F32)
                rs.append(r_carry[hh] + (C[:, 0:1] + L[:, 0:1]))
                gs.append(g_carry[hh] + SI[:, 0:1])
            return tuple(dqs), tuple(rs), tuple(gs)

        zero = (jnp.zeros((tq, 1), F32),) * HP
        dq_d, r_d, g_d = block(qi, zero, zero, True)
        dq_p, r0, g0 = block(jnp.maximum(qi - 1, 0), r_d, g_d, False, valid=qi > 0)
        dq0 = tuple(a + b for a, b in zip(dq_d, dq_p))

        def cond(st):
            kb, _, _, _, rmax = st
            return jnp.logical_and(kb >= 0, rmax > SKIP_LOG)

        def step(st):
            kb, dq, r, g, _ = st
            dq_part, r2, g2 = block(kb, r, g, False)
            return kb - 1, tuple(a + b for a, b in zip(dq, dq_part)), r2, g2, _max_all(r2)

        _, dqs, _, _, _ = lax.while_loop(cond, step, (qi - 2, dq0, r0, g0, _max_all(r0)))
        for hh, cols in enumerate(heads):
            dq_ref[:, cols] = dqs[hh]

    blk = pl.BlockSpec((tq, WP), lambda h, i: (i, h))
    full = pl.BlockSpec((S, WP), lambda h, i: (0, h))
    vec = pl.BlockSpec((1, WP), lambda h, i: (0, h))
    cst = pl.BlockSpec((tq, tq), lambda h, i: (0, 0))
    act = jax.ShapeDtypeStruct((S, W), F32)
    return _pcall(
        body, grid=(n_heads // HP, NQ),
        in_specs=[blk, full, full, blk,
                  pl.BlockSpec((tq, WP), lambda h, i: (i, d_on_col0 // HP + h)), vec, cst, cst],
        out_specs=(blk, full, full, vec),
        out_shape=(act, act, act, jax.ShapeDtypeStruct((1, W), F32)),
        semantics=("parallel", "arbitrary"), name=name,
        args=(qh, kh, vh, o_pre, d_on, out_g, u_strict, u_incl), side=side)


def _sb_pre_bwd(proj, dqh, dkh, dvh, q_g, k_g, *, n_heads, col0, tm, name):
    S = proj.shape[0]
    H = n_heads
    W = H * HEAD

    def body(q_ref, k_ref, dqh_ref, dkh_ref, dvh_ref, qg_ref, kg_ref,
             dq_ref, dk_ref, dv_ref, dqg_ref, dkg_ref):
        first = jnp.logical_and(pl.program_id(0) == 0, pl.program_id(1) == 0)

        @pl.when(first)
        def _():
            dqg_ref[...] = jnp.zeros_like(dqg_ref)
            dkg_ref[...] = jnp.zeros_like(dkg_ref)

        for src, dh_ref, g_ref, dst, dg_ref in ((q_ref, dqh_ref, qg_ref, dq_ref, dqg_ref),
                                                (k_ref, dkh_ref, kg_ref, dk_ref, dkg_ref)):
            xv = src[...]
            dh = dh_ref[...]
            rstd = lax.rsqrt(jnp.mean(xv * xv, axis=-1, keepdims=True) + EPS)
            n = xv * rstd
            dg_ref[...] += jnp.sum(dh * n, axis=0, keepdims=True)
            dn = dh * g_ref[...]
            dst[...] = (rstd * (dn - n * jnp.mean(dn * n, axis=-1, keepdims=True))).astype(BF16)
        dv_ref[...] = dvh_ref[...].astype(BF16)

    def col(group):
        return pl.BlockSpec((tm, HEAD), lambda i, h: (i, col0 + group * H + h))

    vec = pl.BlockSpec((1, HEAD), lambda i, h: (0, 0))
    own = pl.BlockSpec((tm, HEAD), lambda i, h: (i, h))
    act = jax.ShapeDtypeStruct((S, W), BF16)
    vec_shape = jax.ShapeDtypeStruct((1, HEAD), F32)
    return pl.pallas_call(
        body, grid=(S // tm, H), in_specs=[col(0), col(1), own, own, own, vec, vec],
        out_specs=(own, own, own, vec, vec), out_shape=(act, act, act, vec_shape, vec_shape),
        compiler_params=_params(("arbitrary", "arbitrary")), name=name,
    )(proj, proj, dqh, dkh, dvh, q_g, k_g)


def _softmax_rows(x_ref, L):
    rows = [x_ref[l:l + 1, :] for l in range(L)]
    mx = rows[0]
    for r in rows[1:]:
        mx = jnp.maximum(mx, r)
    ex = [jnp.exp(r - mx) for r in rows]
    tot = ex[0]
    for e in ex[1:]:
        tot = tot + e
    return [e / tot for e in ex]


def _lb_fwd(logits, *, name):
    L, W = logits.shape

    def body(x_ref, o_ref):
        s = _softmax_rows(x_ref, L)
        run = jnp.zeros((1, W), F32)
        for l in range(L):
            run = run + s[l]
            o_ref[l:l + 1, :] = run - s[0]

    return pl.pallas_call(body, out_shape=jax.ShapeDtypeStruct((L, W), F32), name=name)(logits)


def _lb_bwd(logits, dlb_parts, *, name):
    L, W = logits.shape
    P = dlb_parts.shape[0]

    def body(x_ref, d_ref, o_ref):
        s = _softmax_rows(x_ref, L)
        dlb = []
        for l in range(L):
            t = d_ref[0, l:l + 1, :]
            for q in range(1, P):
                t = t + d_ref[q, l:l + 1, :]
            dlb.append(t)
        ds = [None] * L
        run = jnp.zeros((1, W), F32)
        for j in reversed(range(L)):
            run = run + dlb[j]
            ds[j] = run
        ds[0] = jnp.zeros((1, W), F32)
        inner = jnp.zeros((1, W), F32)
        for j in range(L):
            inner = inner + s[j] * ds[j]
        for j in range(L):
            o_ref[j:j + 1, :] = s[j] * (ds[j] - inner)

    return pl.pallas_call(body, out_shape=jax.ShapeDtypeStruct((L, W), F32), name=name)(logits, dlb_parts)


def _ada_mod(c_all, w_ada, *, nb, name):
    L, D, n = w_ada.shape
    B = c_all.shape[0]

    def body(c_ref, w_ref, o_ref, cond_ref):
        cv = c_ref[...]
        s, _ = _sigmoid_pair(cv)
        cond = cv * s
        cond_ref[...] = cond
        o_ref[...] = _bdot(cond, w_ref[...])

    return pl.pallas_call(
        body, grid=(L, n // nb),
        in_specs=[pl.BlockSpec((B, D), lambda l, j: (0, 0)),
                  pl.BlockSpec((None, D, nb), lambda l, j: (l, 0, j))],
        out_specs=(pl.BlockSpec((None, B, nb), lambda l, j: (l, 0, j)),
                   pl.BlockSpec((B, D), lambda l, j: (0, 0))),
        out_shape=(jax.ShapeDtypeStruct((L, B, n), F32), jax.ShapeDtypeStruct((B, D), F32)),
        compiler_params=_params(("arbitrary", "arbitrary")), name=name)(c_all, w_ada)


def _adam_math(w, g, m, v):
    m2 = ADAM_B1 * m + (1.0 - ADAM_B1) * g
    v2 = ADAM_B2 * v + (1.0 - ADAM_B2) * (g * g)
    m_hat = m2 / (1.0 - ADAM_B1 ** ADAM_STEP)
    v_hat = v2 / (1.0 - ADAM_B2 ** ADAM_STEP)
    delta = -ADAM_LR * (m_hat / (jnp.sqrt(v_hat) + ADAM_EPS) + ADAM_WD * w)
    return delta, m2, v2


def _adamw(w, m, v, gparts, *, tr, name):
    R, C = w.shape
    P = gparts.shape[0]

    def body(w_ref, m_ref, v_ref, gp_ref, g_ref, d_ref, m2_ref, v2_ref):
        g = gp_ref[0].astype(F32)
        for p in range(1, P):
            g = g + gp_ref[p].astype(F32)
        delta, m2, v2 = _adam_math(w_ref[...], g, m_ref[...], v_ref[...])
        g_ref[...] = g
        d_ref[...] = delta
        m2_ref[...] = m2
        v2_ref[...] = v2

    tile = pl.BlockSpec((tr, C), lambda i: (i, 0))
    shp = jax.ShapeDtypeStruct((R, C), F32)
    return pl.pallas_call(
        body, grid=(R // tr,),
        in_specs=[tile, tile, tile, pl.BlockSpec((P, tr, C), lambda i: (0, i, 0))],
        out_specs=(tile, tile, tile, tile), out_shape=(shp, shp, shp, shp),
        compiler_params=_params(("parallel",)), name=name)(w, m, v, gparts)


def _adamw_layers(w, m, v, gparts, *, tr, name, side=None):
    L, R, C = w.shape
    P = gparts[0].shape[0]
    nblk = R // tr

    def body(*refs):
        w_ref, m_ref, v_ref = refs[:3]
        gp_refs = refs[3:3 + L]
        g_ref, d_ref, m2_ref, v2_ref = refs[3 + L:]
        layer = pl.program_id(0)
        for t in range(L):
            @pl.when(layer == t)
            def _(t=t):
                g = gp_refs[t][0].astype(F32)
                for q in range(1, P):
                    g = g + gp_refs[t][q].astype(F32)
                delta, m2, v2 = _adam_math(w_ref[...], g, m_ref[...], v_ref[...])
                g_ref[...] = g
                d_ref[...] = delta
                m2_ref[...] = m2
                v2_ref[...] = v2

    def gp_spec(t):
        def index(l, i):
            return (0, jnp.where(l == t, i, jnp.where(l < t, 0, nblk - 1)), 0)
        return pl.BlockSpec((P, tr, C), index)

    tile = pl.BlockSpec((None, tr, C), lambda l, i: (l, i, 0))
    shp = jax.ShapeDtypeStruct((L, R, C), F32)
    return _pcall(
        body, grid=(L, nblk), in_specs=[tile, tile, tile] + [gp_spec(t) for t in range(L)],
        out_specs=(tile, tile, tile, tile), out_shape=(shp, shp, shp, shp),
        semantics=("arbitrary", "arbitrary"), name=name, args=(w, m, v, *gparts), side=side)


def _adamw_ada(w, m, v, cond_t, dmod, *, tr, name):
    L, D, n = w.shape
    Bp = cond_t.shape[1]

    def body(w_ref, m_ref, v_ref, c_ref, dm_ref, g_ref, d_ref, m2_ref, v2_ref):
        g = _bdot(c_ref[...], dm_ref[...])
        delta, m2, v2 = _adam_math(w_ref[...], g, m_ref[...], v_ref[...])
        g_ref[...] = g
        d_ref[...] = delta
        m2_ref[...] = m2
        v2_ref[...] = v2

    tile = pl.BlockSpec((None, tr, n), lambda l, i: (l, i, 0))
    shp = jax.ShapeDtypeStruct((L, D, n), F32)
    return pl.pallas_call(
        body, grid=(L, D // tr),
        in_specs=[tile, tile, tile, pl.BlockSpec((tr, Bp), lambda l, i: (i, 0)),
                  pl.BlockSpec((None, Bp, n), lambda l, i: (l, 0, 0))],
        out_specs=(tile, tile, tile, tile), out_shape=(shp, shp, shp, shp),
        compiler_params=_params(("parallel", "parallel")), name=name)(w, m, v, cond_t, dmod)


def _allgather_small(block, *, name):
    R, C = block.shape

    def body(x_ref, out_ref, send_sems, recv_sems, local_sem):
        x, y, c = lax.axis_index("x"), lax.axis_index("y"), lax.axis_index("c")

        def rows(px, py, pc):
            return out_ref.at[pl.ds((4 * px + 2 * py + pc) * R, R), :]

        mine = pltpu.make_async_copy(x_ref, rows(x, y, c), local_sem)
        mine.start()
        sends = []
        for rel in range(1, N_DEV):
            to = _peer(x, y, c, rel)
            cp = pltpu.make_async_remote_copy(src_ref=x_ref, dst_ref=rows(x, y, c),
                                              send_sem=send_sems.at[rel - 1], recv_sem=recv_sems.at[rel - 1],
                                              device_id=to, device_id_type=MESH)
            cp.start()
            sends.append(cp)
        for rel in range(1, N_DEV):
            frm = _peer(x, y, c, rel)
            pltpu.make_async_remote_copy(src_ref=x_ref, dst_ref=rows(*frm),
                                         send_sem=send_sems.at[rel - 1], recv_sem=recv_sems.at[rel - 1],
                                         device_id=frm, device_id_type=MESH).wait_recv()
        for cp in sends:
            cp.wait_send()
        mine.wait()

    return pl.pallas_call(
        body, out_shape=jax.ShapeDtypeStruct((N_DEV * R, C), block.dtype),
        in_specs=[pl.BlockSpec(memory_space=pltpu.VMEM)],
        out_specs=pl.BlockSpec(memory_space=pltpu.VMEM),
        scratch_shapes=[pltpu.SemaphoreType.DMA((N_DEV - 1,)), pltpu.SemaphoreType.DMA((N_DEV - 1,)),
                        pltpu.SemaphoreType.DMA],
        compiler_params=pltpu.CompilerParams(vmem_limit_bytes=V7X_VMEM_LIMIT), name=name)(block)


def _allgather_hbm(shards, *, name):
    n = len(shards)

    def body(*refs):
        ins = refs[:n]
        outs = refs[n:2 * n]
        send_sems, recv_sems, local_sems = refs[2 * n:]
        x, y, c = lax.axis_index("x"), lax.axis_index("y"), lax.axis_index("c")
        sibling = (x, y, 1 - c)
        chips = [(1 - x, y), (x, 1 - y), (1 - x, 1 - y)]

        def slot(t, px, py, pc):
            return outs[t].at[4 * px + 2 * py + pc]

        def copy(t, k, block, to, src=None):
            return pltpu.make_async_remote_copy(
                src_ref=slot(t, *block) if src is None else src, dst_ref=slot(t, *block),
                send_sem=send_sems.at[t * 7 + k], recv_sem=recv_sems.at[t * 7 + k],
                device_id=to, device_id_type=MESH)

        me = (x, y, c)
        started = []
        mine = []
        for t in range(n):
            cp = pltpu.make_async_copy(ins[t], slot(t, *me), local_sems.at[t])
            cp.start()
            mine.append(cp)
            first = [copy(t, 0, me, sibling, src=ins[t])]
            first += [copy(t, 1 + j, me, (*chip, c), src=ins[t]) for j, chip in enumerate(chips)]
            for cp in first:
                cp.start()
            started += first
        for t in range(n):
            for j, chip in enumerate(chips):
                copy(t, 1 + j, (*chip, c), me).wait_recv()
                fwd = copy(t, 4 + j, (*chip, c), sibling)
                fwd.start()
                started.append(fwd)
        for t in range(n):
            copy(t, 0, sibling, me).wait_recv()
            for j, chip in enumerate(chips):
                copy(t, 4 + j, (*chip, 1 - c), me).wait_recv()
        for cp in started:
            cp.wait_send()
        for cp in mine:
            cp.wait()

    any_spec = pl.BlockSpec(memory_space=pl.ANY)
    return pl.pallas_call(
        body, out_shape=[jax.ShapeDtypeStruct((N_DEV,) + s.shape, s.dtype) for s in shards],
        in_specs=[any_spec] * n, out_specs=[any_spec] * n,
        scratch_shapes=[pltpu.SemaphoreType.DMA((7 * n,)), pltpu.SemaphoreType.DMA((7 * n,)),
                        pltpu.SemaphoreType.DMA((n,))],
        name=name)(*shards)


def _tile(total, want):
    step = 128 if total % 128 == 0 else 8
    best = step
    t = step
    while t <= min(total, want):
        if total % t == 0:
            best = t
        t += step
    return best


def _local_step(x, target, mods, lbs, p, wg, shards=None):
    S, D = x.shape
    L = mods.shape[0]
    W = D // 2
    H = W // HEAD
    F = wg["w_ffn_out"][0].shape[0]
    mesh = shards is not None
    tm = _tile(S, 512)
    tm_big = _tile(S, 1024)
    tm_tn = _tile(S, 2048)
    tm_sw = _tile(S, 128)
    tq = _tile(S, 256)
    cg = max(1, min(8, S // CHUNK))
    nb_out = _tile(D, 1024)
    kb_f = _tile(F, 1408)

    def row(a, l):
        return a[l][None, :]

    def gather_of(l, names):
        if mesh and l < L:
            return (True, [shards[k][l] for k in names])
        return None

    def scatter_of(blocks):
        if mesh and blocks is not None:
            return (False, [b.reshape((N_DEV, -1) + b.shape[-1:]) if b.ndim == 2 else b for b in blocks])
        return None

    saved = []
    xcur = x
    for l in range(L):
        mod = mods[l]
        sh1, sc1, g1, sh2, sc2, g2 = [mod[:, i * D:(i + 1) * D] for i in range(N_MOD)]
        h1 = _modnorm_fwd(xcur, row(p["norm1_g"], l), sc1, sh1, tm=tm, name="norm1_fwd")
        proj, got = _mm_nn(h1, wg["w_in"][l], tm=tm_big, name="proj_fwd", side=gather_of(l + 1, ["w_in"]))
        if got is not None:
            wg["w_in"][l + 1] = got[0]
        lb = lbs[l][None, :]
        (o_hg, on_hg, states), got = _hg_fwd(proj, lb, row(p["hg_out_g"], l), n_heads=H, cg=cg, name="hgrn2_fwd",
                                             side=gather_of(l + 1, ["w_out"]))
        if got is not None:
            wg["w_out"][l + 1] = got[0].reshape(-1, D)
        qh, kh, vh = _sb_pre(proj, row(p["sb_q_g"], l), row(p["sb_k_g"], l), n_heads=H, col0=4 * H,
                             tm=tm_tn, name="sb_qknorm_fwd")
        (o_sb, on_sb), got = _sb_fwd(qh, kh, vh, row(p["sb_out_g"], l), n_heads=H, tq=tq, name="sb_fwd",
                                     side=gather_of(l + 1, ["w_ffn_out"]))
        if got is not None:
            wg["w_ffn_out"][l + 1] = got[0].reshape(-1, D)
        o_cat = jnp.concatenate([on_hg, on_sb], axis=1)
        (x1, mixed), _ = _mm_nn(o_cat, wg["w_out"][l], tm=tm_big, nb=nb_out, resid=xcur, gate=g1,
                                name="out_proj_fwd")
        h2 = _modnorm_fwd(x1, row(p["norm2_g"], l), sc2, sh2, tm=tm, name="norm2_fwd")
        w_fin = wg["w_ffn_in"][l]
        if w_fin.shape[0] % 2:
            w_fin = jnp.stack([w_fin[0][:, :F], w_fin[0][:, F:]])
        (gate, up, a), got = _ffn_in_fwd(h2, w_fin, tm=tm, nb=F, name="ffn_in_fwd",
                                         side=gather_of(l + 1, ["w_ffn_in"]))
        if got is not None:
            wg["w_ffn_in"][l + 1] = got[0]
        (x2, ffn), _ = _mm_nn(a, wg["w_ffn_out"][l], tm=tm, nb=nb_out // 2, resid=x1, gate=g2, name="ffn_out_fwd")
        saved.append(dict(x=xcur, h1=h1, proj=proj, o_hg=o_hg, o_sb=o_sb, states=states, qh=qh, kh=kh, vh=vh,
                          o_cat=o_cat, mixed=mixed, x1=x1, h2=h2, gate=gate, up=up, a=a, ffn=ffn, lb=lb,
                          sc1=sc1, g1=g1, sc2=sc2, g2=g2, w_fin=w_fin))
        xcur = x2

    last = saved[-1]
    dx, dffn, dg2, loss = _loss_bwd(xcur, target, last["ffn"], last["g2"], tm=tm, name="loss_bwd")

    big = {k: [None] * L for k in ("w_in", "w_out", "w_ffn_in", "w_ffn_out")}
    small = {k: [None] * L for k in ("norm1_g", "hg_lb", "hg_out_g", "sb_q_g", "sb_k_g", "sb_out_g", "norm2_g")}
    dmods = [None] * L
    pending_in = None
    for l in reversed(range(L)):
        sv = saved[l]
        (dgate, dup), got = _ffn_out_bwd_x(dffn, wg["w_ffn_out"][l], sv["gate"], sv["up"], tm=tm, kb=kb_f,
                                           name="ffn_out_bwd_x", side=scatter_of(pending_in))
        if got is not None:
            big["w_in"][l + 1] = got[0]
        g_fout, _ = _mm_tn(sv["a"], dffn, tm=tm_big, kb=kb_f, nb=D, blocked=False, name="ffn_out_bwd_w")
        dh2, got = _mm_nt(dgate, sv["w_fin"], dy2=dup, tm=tm_big, kb=D, name="ffn_in_bwd_x",
                          side=scatter_of([g_fout]))
        big["w_ffn_out"][l] = g_fout if got is None else got[0]
        g_fin, _ = _mm_tn(sv["h2"], dgate, dy2=dup, tm=tm_big, kb=D, nb=sv["w_fin"].shape[2], blocked=True,
                          name="ffn_in_bwd_w")
        dx1, dmixed, dg1, dsh2, dsc2, dn2 = _modnorm_bwd(
            sv["x1"], dh2, dx, row(p["norm2_g"], l), sv["sc2"], sv["mixed"], sv["g1"], tm=tm_sw * 2,
            name="norm2_bwd")
        small["norm2_g"][l] = dn2
        d_ocat, _ = _mm_nt(dmixed, wg["w_out"][l], tm=tm_big, kb=nb_out, nb=D, name="out_proj_bwd_x")
        g_out, _ = _mm_tn(sv["o_cat"], dmixed, tm=tm_tn, kb=D, nb=nb_out, blocked=False, name="out_proj_bwd_w")
        (dhq, dhf, dhi, dhg, dlb, dhog), _ = _hg_bwd(sv["proj"], sv["o_hg"], d_ocat, 0, sv["states"], sv["lb"],
                                                     row(p["hg_out_g"], l), n_heads=H, cg=cg, name="hgrn2_bwd")
        (dqh, dkh, dvh, dsog), got = _sb_bwd(sv["qh"], sv["kh"], sv["vh"], sv["o_sb"], d_ocat, H,
                                             row(p["sb_out_g"], l), n_heads=H, tq=tq, name="sb_bwd",
                                             side=scatter_of([g_fin]))
        big["w_ffn_in"][l] = g_fin if got is None else got[0]
        dsq, dsk, dsv, dqg, dkg = _sb_pre_bwd(sv["proj"], dqh, dkh, dvh, row(p["sb_q_g"], l),
                                              row(p["sb_k_g"], l), n_heads=H, col0=4 * H, tm=tm_tn,
                                              name="sb_qknorm_bwd")
        small["hg_lb"][l] = dlb
        small["hg_out_g"][l] = dhog
        small["sb_out_g"][l] = dsog
        small["sb_q_g"][l] = dqg
        small["sb_k_g"][l] = dkg
        dproj = jnp.concatenate([dhq, dhf, dhi, dhg, dsq, dsk, dsv], axis=1)
        g_in, _ = _mm_tn(sv["h1"], dproj, tm=tm_tn, kb=D, nb=wg["w_in"][l].shape[2], blocked=True,
                         name="proj_bwd_w")
        big["w_in"][l] = g_in
        dh1, got = _mm_nt(dproj, wg["w_in"][l], tm=tm_big, kb=D, name="proj_bwd_x",
                          side=scatter_of([g_out, g_in] if l == 0 else [g_out]))
        if got is not None:
            big["w_out"][l] = got[0]
            if l == 0:
                big["w_in"][l] = got[1]
        else:
            big["w_out"][l] = g_out
        pending_in = [g_in]
        if l > 0:
            prev = saved[l - 1]
            dx0, dffn_prev, dg2_prev, dsh1, dsc1, dn1 = _modnorm_bwd(
                sv["x"], dh1, dx1, row(p["norm1_g"], l), sv["sc1"], prev["ffn"], prev["g2"], tm=tm_sw * 2,
                name="norm1_bwd")
        else:
            dx0, dsh1, dsc1, dn1 = _modnorm_bwd(sv["x"], dh1, dx1, row(p["norm1_g"], l), sv["sc1"], None, None,
                                                tm=tm_sw * 2, name="norm1_bwd_first")
            dffn_prev, dg2_prev = None, None
        small["norm1_g"][l] = dn1
        dmods[l] = jnp.concatenate([dsh1, dsc1, dg1, dsh2, dsc2, dg2], axis=1)
        dx, dffn, dg2 = dx0, dffn_prev, dg2_prev
    return loss, dx, big, small, dmods


def kernel(x, c, norm1_g, w_in, hg_lb_logits, hg_out_g, sb_q_g, sb_k_g, sb_out_g, w_out, norm2_g, w_ffn_in, w_ffn_out, w_ada, b_ada, loss_target, m_norm1_g, m_w_in, m_hg_lb_logits, m_hg_out_g, m_sb_q_g, m_sb_k_g, m_sb_out_g, m_w_out, m_norm2_g, m_w_ffn_in, m_w_ffn_out, m_w_ada, m_b_ada, v_norm1_g, v_w_in, v_hg_lb_logits, v_hg_out_g, v_sb_q_g, v_sb_k_g, v_sb_out_g, v_w_out, v_norm2_g, v_w_ffn_in, v_w_ffn_out, v_w_ada, v_b_ada):
    L, D = norm1_g.shape
    S = x.shape[1]
    me = 4 * lax.axis_index("x") + 2 * lax.axis_index("y") + lax.axis_index("c")

    c_all = _allgather_small(jnp.broadcast_to(c, (8, D)), name="gather_c").reshape(N_DEV, 8, D)[:, 0, :]
    n_ada = w_ada.shape[2]
    mod_cols, cond = _ada_mod(c_all, w_ada, nb=_tile(n_ada, 512), name="ada_mod")
    mod_all = _allgather_small(mod_cols.reshape(L * N_DEV, n_ada), name="gather_mod")
    mod_all = mod_all.reshape(N_DEV, L, N_DEV, n_ada)
    mod_mine = lax.dynamic_index_in_dim(mod_all, me, axis=2, keepdims=False)
    mods = jnp.transpose(mod_mine, (1, 0, 2)).reshape(L, 1, N_DEV * n_ada) + b_ada[:, None, :]

    lbs = _lb_fwd(hg_lb_logits, name="lower_bounds_fwd")

    shards = dict(w_in=[w_in[l].astype(BF16) for l in range(L)], w_out=[w_out[l].astype(BF16) for l in range(L)],
                  w_ffn_in=[w_ffn_in[l].astype(BF16) for l in range(L)],
                  w_ffn_out=[w_ffn_out[l].astype(BF16) for l in range(L)])
    g_in, g_out, g_fin, g_fout = _allgather_hbm(
        [shards["w_in"][0], shards["w_out"][0], shards["w_ffn_in"][0], shards["w_ffn_out"][0]],
        name="gather_weights")
    wg = dict(w_in=[g_in] + [None] * (L - 1), w_out=[g_out.reshape(-1, D)] + [None] * (L - 1),
              w_ffn_in=[g_fin] + [None] * (L - 1), w_ffn_out=[g_fout.reshape(-1, D)] + [None] * (L - 1))

    p = dict(norm1_g=norm1_g, hg_out_g=hg_out_g, sb_q_g=sb_q_g, sb_k_g=sb_k_g, sb_out_g=sb_out_g,
             norm2_g=norm2_g)
    loss_part, grad_x, recv, small, dmods = _local_step(x.reshape(S, D), loss_target.reshape(S, D), mods, lbs, p,
                                                        wg, shards)

    dmod = jnp.concatenate(dmods, axis=0)
    pieces = [jnp.concatenate(small[k], axis=0) for k in
              ("norm1_g", "hg_lb", "hg_out_g", "sb_q_g", "sb_k_g", "sb_out_g", "norm2_g")] + [dmod]
    flat = jnp.concatenate([a.reshape(-1) for a in pieces] + [loss_part.reshape(-1)])
    n_flat = flat.shape[0]
    rows = -(-n_flat // 1024) * 8
    flat = jnp.pad(flat, (0, rows * 128 - n_flat)).reshape(rows, 128)
    gathered = _allgather_small(flat, name="gather_small_grads").reshape(N_DEV, rows * 128)

    def take(off, shape):
        size = 1
        for s in shape:
            size *= s
        return gathered[:, off:off + size].reshape((N_DEV,) + tuple(shape)), off + size

    off = 0
    parts = {}
    for k, a in zip(("norm1_g", "hg_lb", "hg_out_g", "sb_q_g", "sb_k_g", "sb_out_g", "norm2_g", "dmod"), pieces):
        parts[k], off = take(off, a.shape)
    loss_parts = gathered[:, off:off + 1]
    loss = jnp.sum(loss_parts)

    def pad8(a):
        return jnp.pad(a, ((0, 0), (0, 8 - a.shape[1]), (0, 0)))

    def small_update(w, m, v, gparts):
        Lw = w.shape[0]
        g, d, m2, v2 = _adamw(pad8(w[None])[0], pad8(m[None])[0], pad8(v[None])[0], pad8(gparts),
                              tr=8, name="adamw_small")
        return g[:Lw], d[:Lw], m2[:Lw], v2[:Lw]

    out = {}
    out["norm1_g"] = small_update(norm1_g, m_norm1_g, v_norm1_g, parts["norm1_g"])
    dlogits = _lb_bwd(hg_lb_logits, parts["hg_lb"], name="lower_bounds_bwd")
    out["hg_lb_logits"] = small_update(hg_lb_logits, m_hg_lb_logits, v_hg_lb_logits, dlogits[None])
    out["hg_out_g"] = small_update(hg_out_g, m_hg_out_g, v_hg_out_g, parts["hg_out_g"])
    out["sb_q_g"] = small_update(sb_q_g, m_sb_q_g, v_sb_q_g, parts["sb_q_g"])
    out["sb_k_g"] = small_update(sb_k_g, m_sb_k_g, v_sb_k_g, parts["sb_k_g"])
    out["sb_out_g"] = small_update(sb_out_g, m_sb_out_g, v_sb_out_g, parts["sb_out_g"])
    out["norm2_g"] = small_update(norm2_g, m_norm2_g, v_norm2_g, parts["norm2_g"])
    out["b_ada"] = small_update(b_ada, m_b_ada, v_b_ada, parts["dmod"])

    dmod_all = parts["dmod"].reshape(N_DEV, L, N_DEV, n_ada)
    dmod_mine = lax.dynamic_index_in_dim(dmod_all, me, axis=2, keepdims=False)
    dmod_mine = jnp.pad(jnp.transpose(dmod_mine, (1, 0, 2)), ((0, 0), (0, 128 - N_DEV), (0, 0)))
    cond_t = jnp.pad(jnp.transpose(cond), ((0, 0), (0, 128 - N_DEV)))
    out["w_ada"] = _adamw_ada(w_ada, m_w_ada, v_w_ada, cond_t, dmod_mine, tr=_tile(D, 256), name="adamw_ada")

    def big_update(w, m, v, recv_l, name):
        return _adamw_layers(w, m, v, recv_l, tr=_tile(w.shape[1], 131072 // w.shape[2]), name=name)[0]

    out["w_ffn_in"] = big_update(w_ffn_in, m_w_ffn_in, v_w_ffn_in, recv["w_ffn_in"], "adamw_w_ffn_in")
    out["w_ffn_out"] = big_update(w_ffn_out, m_w_ffn_out, v_w_ffn_out, recv["w_ffn_out"], "adamw_w_ffn_out")
    out["w_out"] = big_update(w_out, m_w_out, v_w_out, recv["w_out"], "adamw_w_out")
    out["w_in"] = big_update(w_in, m_w_in, v_w_in, recv["w_in"], "adamw_w_in")

    order = ("norm1_g", "w_in", "hg_lb_logits", "hg_out_g", "sb_q_g", "sb_k_g", "sb_out_g", "w_out", "norm2_g",
             "w_ffn_in", "w_ffn_out", "w_ada", "b_ada")
    grads = [out[k][0] for k in order]
    deltas = [out[k][1] for k in order]
    new_m = [out[k][2] for k in order]
    new_v = [out[k][3] for k in order]
    return (loss, grad_x.reshape(1, S, D), *grads, *deltas, *new_m, *new_v)
```

```python
import functools

import jax
import jax.numpy as jnp
from jax import lax
from jax.experimental import pallas as pl
from jax.experimental.pallas import tpu as pltpu

F32 = jnp.float32
BF16 = jnp.bfloat16
MESH = pl.DeviceIdType.MESH

N_DEV = 8
HEAD = 128
CHUNK = 64
N_MOD = 6
EPS = 1e-6
TINY = 1e-30
ADAM_LR = 0.001
ADAM_B1 = 0.9
ADAM_B2 = 0.999
ADAM_EPS = 1e-08
ADAM_WD = 0.01
ADAM_STEP = 10
V7X_VMEM_LIMIT = 56 * 1024 * 1024
SKIP_LOG = -104.0


def _params(sem):
    return pltpu.CompilerParams(dimension_semantics=sem, vmem_limit_bytes=V7X_VMEM_LIMIT)


def _bdot(a, b, dims=(((1,), (0,)), ((), ()))):
    return lax.dot_general(a.astype(BF16), b.astype(BF16), dims, preferred_element_type=F32)


_NT = (((1,), (1,)), ((), ()))
_TN = (((0,), (0,)), ((), ()))


def _sigmoid_pair(x):
    e = jnp.exp(-jnp.abs(x))
    r = 1.0 / (1.0 + e)
    er = e * r
    pos = x >= 0
    return jnp.where(pos, r, er), jnp.where(pos, er, r)


def _split_dot(x, u, parts):
    acc = None
    rem = x
    for _ in range(parts):
        p = rem.astype(BF16)
        rem = rem - p.astype(F32)
        t = lax.dot_general(p, u, (((1,), (0,)), ((), ())), preferred_element_type=F32)
        acc = t if acc is None else acc + t
    return acc


def _split_dot_left(u, x, parts):
    acc = None
    rem = x
    for _ in range(parts):
        p = rem.astype(BF16)
        rem = rem - p.astype(F32)
        t = lax.dot_general(u, p, (((1,), (0,)), ((), ())), preferred_element_type=F32)
        acc = t if acc is None else acc + t
    return acc


def _peer(x, y, c, rel):
    return (x ^ ((rel >> 2) & 1), y ^ ((rel >> 1) & 1), c ^ (rel & 1))


def _exchange(gather, srcs, dsts, send_sems, recv_sems, local_sems, phase):
    x, y, c = lax.axis_index("x"), lax.axis_index("y"), lax.axis_index("c")
    me = 4 * x + 2 * y + c
    for t in range(len(srcs)):
        own = srcs[t] if gather else srcs[t].at[me]
        local = pltpu.make_async_copy(own, dsts[t].at[me], local_sems.at[t])
        if phase == "start":
            local.start()
        for rel in range(1, N_DEV):
            px, py, pc = _peer(x, y, c, rel)
            pid = 4 * px + 2 * py + pc
            k = t * (N_DEV - 1) + rel - 1
            if phase == "start":
                pltpu.make_async_remote_copy(
                    src_ref=srcs[t] if gather else srcs[t].at[pid], dst_ref=dsts[t].at[me],
                    send_sem=send_sems.at[k], recv_sem=recv_sems.at[k],
                    device_id=(px, py, pc), device_id_type=MESH).start()
            else:
                cp = pltpu.make_async_remote_copy(
                    src_ref=own, dst_ref=dsts[t].at[pid], send_sem=send_sems.at[k], recv_sem=recv_sems.at[k],
                    device_id=(px, py, pc), device_id_type=MESH)
                cp.wait_recv()
                cp.wait_send()
        if phase == "wait":
            local.wait()


def _exchange_scratch(n):
    return [pltpu.SemaphoreType.DMA(((N_DEV - 1) * n,)), pltpu.SemaphoreType.DMA(((N_DEV - 1) * n,)),
            pltpu.SemaphoreType.DMA((n,))]


def _pcall(body, *, grid, in_specs, out_specs, out_shape, scratch_shapes=(), semantics, name, args, side=None):
    single = not isinstance(out_shape, (tuple, list))
    if single:
        out_specs, out_shape = [out_specs], [out_shape]
    in_specs, out_specs, out_shape = list(in_specs), list(out_specs), list(out_shape)
    scratch_shapes = list(scratch_shapes)
    n_in, n_out, n_scr = len(in_specs), len(out_specs), len(scratch_shapes)
    if side is None:
        res = pl.pallas_call(body, grid=grid, in_specs=in_specs, out_specs=out_specs, out_shape=out_shape,
                             scratch_shapes=scratch_shapes, compiler_params=_params(semantics), name=name)(*args)
        return (res[0] if single else tuple(res)), None
    gather, srcs = side
    n = len(srcs)

    def full(*refs):
        ins = refs[:n_in]
        s_in = refs[n_in:n_in + n]
        outs = refs[n_in + n:n_in + n + n_out]
        s_out = refs[n_in + n + n_out:n_in + 2 * n + n_out]
        scr = refs[n_in + 2 * n + n_out:n_in + 2 * n + n_out + n_scr]
        send_sems, recv_sems, local_sems = refs[n_in + 2 * n + n_out + n_scr:]
        first = pl.program_id(0) == 0
        last = pl.program_id(0) == grid[0] - 1
        for ax in range(1, len(grid)):
            first = jnp.logical_and(first, pl.program_id(ax) == 0)
            last = jnp.logical_and(last, pl.program_id(ax) == grid[ax] - 1)

        @pl.when(first)
        def _():
            _exchange(gather, s_in, s_out, send_sems, recv_sems, local_sems, "start")

        body(*ins, *outs, *scr)

        @pl.when(last)
        def _():
            _exchange(gather, s_in, s_out, send_sems, recv_sems, local_sems, "wait")

    any_spec = pl.BlockSpec(memory_space=pl.ANY)
    s_shapes = [jax.ShapeDtypeStruct(((N_DEV,) + s.shape) if gather else s.shape, s.dtype) for s in srcs]
    res = pl.pallas_call(full, grid=grid, in_specs=in_specs + [any_spec] * n,
                         out_specs=out_specs + [any_spec] * n, out_shape=out_shape + s_shapes,
                         scratch_shapes=scratch_shapes + _exchange_scratch(n),
                         compiler_params=_params(("arbitrary",) * len(grid)), name=name)(*args, *srcs)
    main = res[:n_out]
    return (main[0] if single else tuple(main)), list(res[n_out:])


def _mm_nn(a, b, *, tm, nb=None, out_dtype=F32, resid=None, gate=None, name, side=None):
    M, K = a.shape
    if b.ndim == 3:
        NB, _, n = b.shape
        b_spec = pl.BlockSpec((None, K, n), lambda j, i: (j, 0, 0))
    else:
        n = nb
        NB = b.shape[1] // nb
        b_spec = pl.BlockSpec((K, n), lambda j, i: (0, j))
    N = NB * n
    epi = resid is not None

    def body(*refs):
        if epi:
            a_ref, b_ref, r_ref, g_ref, o_ref, acc_ref = refs
        else:
            a_ref, b_ref, o_ref = refs
        acc = jnp.dot(a_ref[...], b_ref[...], preferred_element_type=F32)
        if epi:
            o_ref[...] = r_ref[...] + g_ref[...] * acc
            acc_ref[...] = acc.astype(BF16)
        else:
            o_ref[...] = acc.astype(out_dtype)

    in_specs = [pl.BlockSpec((tm, K), lambda j, i: (i, 0)), b_spec]
    args = [a, b]
    o_spec = pl.BlockSpec((tm, n), lambda j, i: (i, j))
    if epi:
        in_specs += [pl.BlockSpec((tm, n), lambda j, i: (i, j)), pl.BlockSpec((1, n), lambda j, i: (0, j))]
        args += [resid, gate]
        out_shape = [jax.ShapeDtypeStruct((M, N), F32), jax.ShapeDtypeStruct((M, N), BF16)]
        out_specs = [o_spec, o_spec]
    else:
        out_shape = [jax.ShapeDtypeStruct((M, N), out_dtype)]
        out_specs = [o_spec]
    res, got = _pcall(body, grid=(NB, M // tm), in_specs=in_specs, out_specs=out_specs, out_shape=out_shape,
                      semantics=("parallel", "parallel"), name=name, args=args, side=side)
    return (res[0] if len(res) == 1 else res), got


def _halves(dy, dy2, blk_rows, blk_cols, nblocks, row_of, col_of, last_row=None):
    if dy2 is None:
        return [pl.BlockSpec((blk_rows, blk_cols), lambda *g: (row_of(*g), col_of(*g)))], None
    half = nblocks // 2

    def left(*g):
        r, c = row_of(*g), col_of(*g)
        if last_row is None:
            return (r, jnp.minimum(c, half - 1))
        return (jnp.where(c < half, r, last_row), jnp.minimum(c, half - 1))

    def right(*g):
        r, c = row_of(*g), col_of(*g)
        if last_row is None:
            return (r, jnp.maximum(c - half, 0))
        return (jnp.where(c >= half, r, 0), jnp.maximum(c - half, 0))

    return [pl.BlockSpec((blk_rows, blk_cols), left), pl.BlockSpec((blk_rows, blk_cols), right)], half


def _mm_nt(dy, w, *, tm, kb, nb=None, name, side=None, dy2=None):
    M = dy.shape[0]
    N = dy.shape[1] * (1 if dy2 is None else 2)
    if w.ndim == 3:
        NB, Kt, n = w.shape
        w_spec = pl.BlockSpec((None, kb, n), lambda i, k, j: (j, k, 0))
    else:
        Kt = w.shape[0]
        n = nb
        NB = N // nb
        w_spec = pl.BlockSpec((kb, n), lambda i, k, j: (k, j))
    KB = Kt // kb
    dy_specs, half = _halves(dy, dy2, tm, n, NB, lambda i, k, j: i, lambda i, k, j: j)
    n_op = len(dy_specs)

    def body(*refs):
        dy_refs = refs[:n_op]
        w_ref = refs[n_op]
        o_ref = refs[n_op + 1]
        acc = refs[n_op + 2:]
        j = pl.program_id(2)

        def use(dy_ref):
            part = lax.dot_general(dy_ref[...], w_ref[...], _NT, preferred_element_type=F32)
            if NB == 1:
                o_ref[...] = part
                return
            acc_ref, = acc

            @pl.when(j == 0)
            def _():
                acc_ref[...] = part

            @pl.when(jnp.logical_and(j > 0, j < NB - 1))
            def _():
                acc_ref[...] += part

            @pl.when(j == NB - 1)
            def _():
                o_ref[...] = acc_ref[...] + part

        if half is None:
            use(dy_refs[0])
        else:
            pl.when(j < half)(lambda: use(dy_refs[0]))
            pl.when(j >= half)(lambda: use(dy_refs[1]))

    return _pcall(
        body, grid=(M // tm, KB, NB), in_specs=dy_specs + [w_spec],
        out_specs=pl.BlockSpec((tm, kb), lambda i, k, j: (i, k)),
        out_shape=jax.ShapeDtypeStruct((M, Kt), F32),
        scratch_shapes=[] if NB == 1 else [pltpu.VMEM((tm, kb), F32)],
        semantics=("parallel", "parallel", "arbitrary"), name=name,
        args=(dy, w) if dy2 is None else (dy, dy2, w), side=side)


def _mm_tn(x, dy, *, tm, kb, nb, blocked, name, side=None, dy2=None):
    M, K = x.shape
    N = dy.shape[1] * (1 if dy2 is None else 2)
    KB, NB, MB = K // kb, N // nb, M // tm
    dy_specs, half = _halves(dy, dy2, tm, nb, NB, lambda k, n, m: m, lambda k, n, m: n, last_row=MB - 1)
    n_dy = len(dy_specs)

    def body(*refs):
        x_ref = refs[0]
        dy_refs = refs[1:1 + n_dy]
        o_ref = refs[1 + n_dy]
        acc = refs[2 + n_dy:]
        m = pl.program_id(2)

        def use(dy_ref):
            part = lax.dot_general(x_ref[...], dy_ref[...], _TN, preferred_element_type=F32)
            if MB == 1:
                o_ref[...] = part.astype(BF16)
                return
            acc_ref, = acc

            @pl.when(m == 0)
            def _():
                acc_ref[...] = part

            @pl.when(jnp.logical_and(m > 0, m < MB - 1))
            def _():
                acc_ref[...] += part

            @pl.when(m == MB - 1)
            def _():
                o_ref[...] = (acc_ref[...] + part).astype(BF16)

        if half is None:
            use(dy_refs[0])
        else:
            nblk = pl.program_id(1)
            pl.when(nblk < half)(lambda: use(dy_refs[0]))
            pl.when(nblk >= half)(lambda: use(dy_refs[1]))

    if blocked:
        out_shape = jax.ShapeDtypeStruct((NB, K, nb), BF16)
        o_spec = pl.BlockSpec((None, kb, nb), lambda k, n, m: (n, k, 0))
    else:
        out_shape = jax.ShapeDtypeStruct((K, N), BF16)
        o_spec = pl.BlockSpec((kb, nb), lambda k, n, m: (k, n))
    x_spec = pl.BlockSpec((tm, kb), lambda k, n, m: (m, k))
    return _pcall(
        body, grid=(KB, NB, MB), in_specs=[x_spec] + dy_specs,
        out_specs=o_spec, out_shape=out_shape,
        scratch_shapes=[] if MB == 1 else [pltpu.VMEM((kb, nb), F32)],
        semantics=("parallel", "parallel", "arbitrary"), name=name,
        args=(x, dy) if dy2 is None else (x, dy, dy2), side=side)


def _ffn_in_fwd(h, w, *, tm, nb, name, side=None):
    M, K = h.shape
    if w.ndim == 3:
        J, _, n = w.shape
        half = J // 2
        specs = [pl.BlockSpec((None, K, n), lambda j, i: (j, 0, 0)),
                 pl.BlockSpec((None, K, n), lambda j, i: (j + half, 0, 0))]
    else:
        n = nb
        half = w.shape[1] // (2 * nb)
        specs = [pl.BlockSpec((K, n), lambda j, i: (0, j)), pl.BlockSpec((K, n), lambda j, i: (0, j + half))]
    F = half * n

    def body(h_ref, wg_ref, wu_ref, gate_ref, up_ref, act_ref):
        hv = h_ref[...]
        gate = jnp.dot(hv, wg_ref[...], preferred_element_type=F32)
        up = jnp.dot(hv, wu_ref[...], preferred_element_type=F32)
        s, _ = _sigmoid_pair(gate)
        gate_ref[...] = gate
        up_ref[...] = up
        act_ref[...] = (gate * s * up).astype(BF16)

    o_spec = pl.BlockSpec((tm, n), lambda j, i: (i, j))
    f32 = jax.ShapeDtypeStruct((M, F), F32)
    return _pcall(
        body, grid=(half, M // tm), in_specs=[pl.BlockSpec((tm, K), lambda j, i: (i, 0))] + specs,
        out_specs=(o_spec, o_spec, o_spec), out_shape=(f32, f32, jax.ShapeDtypeStruct((M, F), BF16)),
        semantics=("parallel", "parallel"), name=name, args=(h, w, w), side=side)


def _ffn_out_bwd_x(dy, w, gate, up, *, tm, kb, name, side=None):
    M, D = dy.shape
    F = w.shape[0]

    def body(dy_ref, w_ref, g_ref, u_ref, dg_ref, du_ref):
        da = lax.dot_general(dy_ref[...], w_ref[...], _NT, preferred_element_type=F32)
        gate = g_ref[...]
        s, ns = _sigmoid_pair(gate)
        dg_ref[...] = (da * u_ref[...] * (s * (1.0 + gate * ns))).astype(BF16)
        du_ref[...] = (da * (gate * s)).astype(BF16)

    tile = pl.BlockSpec((tm, kb), lambda i, k: (i, k))
    act = jax.ShapeDtypeStruct((M, F), BF16)
    return _pcall(
        body, grid=(M // tm, F // kb),
        in_specs=[pl.BlockSpec((tm, D), lambda i, k: (i, 0)), pl.BlockSpec((kb, D), lambda i, k: (k, 0)), tile, tile],
        out_specs=(tile, tile), out_shape=(act, act),
        semantics=("parallel", "parallel"), name=name, args=(dy, w, gate, up), side=side)


def _modnorm_fwd(x, gain, sc, sh, *, tm, name):
    S, D = x.shape

    def body(x_ref, g_ref, sc_ref, sh_ref, h_ref):
        xv = x_ref[...]
        rstd = lax.rsqrt(jnp.mean(xv * xv, axis=-1, keepdims=True) + EPS)
        y = (xv * rstd) * g_ref[...]
        h_ref[...] = (y * (1.0 + sc_ref[...]) + sh_ref[...]).astype(BF16)

    row = pl.BlockSpec((1, D), lambda i: (0, 0))
    return pl.pallas_call(
        body, grid=(S // tm,),
        in_specs=[pl.BlockSpec((tm, D), lambda i: (i, 0)), row, row, row],
        out_specs=pl.BlockSpec((tm, D), lambda i: (i, 0)),
        out_shape=jax.ShapeDtypeStruct((S, D), BF16),
        compiler_params=_params(("parallel",)), name=name)(x, gain, sc, sh)


def _modnorm_bwd(x, dh, dres, gain, sc, branch, gate, *, tm, name):
    S, D = x.shape
    has_prev = branch is not None

    def body(*refs):
        if has_prev:
            (x_ref, dh_ref, dr_ref, g_ref, sc_ref, br_ref, gt_ref,
             dx_ref, dbr_ref, dgt_ref, dsh_ref, dsc_ref, dgn_ref) = refs
        else:
            (x_ref, dh_ref, dr_ref, g_ref, sc_ref,
             dx_ref, dsh_ref, dsc_ref, dgn_ref) = refs
        i = pl.program_id(0)
        xv = x_ref[...]
        dh_v = dh_ref[...]
        gv = g_ref[...]
        scale1 = 1.0 + sc_ref[...]
        rstd = lax.rsqrt(jnp.mean(xv * xv, axis=-1, keepdims=True) + EPS)
        n = xv * rstd
        dn = dh_v * (gv * scale1)
        dx = rstd * (dn - n * jnp.mean(dn * n, axis=-1, keepdims=True)) + dr_ref[...]
        dx_ref[...] = dx
        dhn = dh_v * n
        p_sh = jnp.sum(dh_v, axis=0, keepdims=True)
        p_sc = jnp.sum(dhn, axis=0, keepdims=True) * gv
        p_gn = jnp.sum(dhn, axis=0, keepdims=True) * scale1
        if has_prev:
            dbr_ref[...] = (gt_ref[...] * dx).astype(BF16)
            p_gt = jnp.sum(dx * br_ref[...].astype(F32), axis=0, keepdims=True)

        @pl.when(i == 0)
        def _():
            dsh_ref[...] = p_sh
            dsc_ref[...] = p_sc
            dgn_ref[...] = p_gn
            if has_prev:
                dgt_ref[...] = p_gt

        @pl.when(i > 0)
        def _():
            dsh_ref[...] += p_sh
            dsc_ref[...] += p_sc
            dgn_ref[...] += p_gn
            if has_prev:
                dgt_ref[...] += p_gt

    tile = pl.BlockSpec((tm, D), lambda i: (i, 0))
    row = pl.BlockSpec((1, D), lambda i: (0, 0))
    row_shape = jax.ShapeDtypeStruct((1, D), F32)
    if has_prev:
        in_specs = [tile, tile, tile, row, row, tile, row]
        args = (x, dh, dres, gain, sc, branch, gate)
        out_specs = (tile, tile, row, row, row, row)
        out_shape = (jax.ShapeDtypeStruct((S, D), F32), jax.ShapeDtypeStruct((S, D), BF16),
                     row_shape, row_shape, row_shape, row_shape)
    else:
        in_specs = [tile, tile, tile, row, row]
        args = (x, dh, dres, gain, sc)
        out_specs = (tile, row, row, row)
        out_shape = (jax.ShapeDtypeStruct((S, D), F32), row_shape, row_shape, row_shape)
    return pl.pallas_call(body, grid=(S // tm,), in_specs=in_specs, out_specs=out_specs,
                          out_shape=out_shape, compiler_params=_params(("arbitrary",)),
                          name=name)(*args)


def _loss_bwd(y, target, branch, gate, *, tm, name):
    S, D = y.shape
    nsteps = S // tm

    def body(y_ref, t_ref, br_ref, gt_ref, dy_ref, dbr_ref, dgt_ref, loss_ref, col_ref):
        i = pl.program_id(0)
        diff = y_ref[...] - t_ref[...]
        dy = diff * (1.0 / D)
        dy_ref[...] = dy
        dbr_ref[...] = (gt_ref[...] * dy).astype(BF16)
        p_gt = jnp.sum(dy * br_ref[...].astype(F32), axis=0, keepdims=True)
        p_col = jnp.sum(diff * diff, axis=0, keepdims=True)

        @pl.when(i == 0)
        def _():
            dgt_ref[...] = p_gt
            col_ref[...] = p_col

        @pl.when(i > 0)
        def _():
            dgt_ref[...] += p_gt
            col_ref[...] += p_col

        @pl.when(i == nsteps - 1)
        def _():
            tot = jnp.sum(col_ref[...], axis=-1, keepdims=True) * (0.5 / D)
            loss_ref[...] = jnp.broadcast_to(tot, (1, 128))

    tile = pl.BlockSpec((tm, D), lambda i: (i, 0))
    row = pl.BlockSpec((1, D), lambda i: (0, 0))
    return pl.pallas_call(
        body, grid=(nsteps,), in_specs=[tile, tile, tile, row],
        out_specs=(tile, tile, row, pl.BlockSpec((1, 128), lambda i: (0, 0))),
        out_shape=(jax.ShapeDtypeStruct((S, D), F32), jax.ShapeDtypeStruct((S, D), BF16),
                   jax.ShapeDtypeStruct((1, D), F32), jax.ShapeDtypeStruct((1, 128), F32)),
        scratch_shapes=[pltpu.VMEM((1, D), F32)],
        compiler_params=_params(("arbitrary",)), name=name)(y, target, branch, gate)


def _hg_chunk(q, fl, lbv, tri):
    C = q.shape[0]
    sq, nsq = _sigmoid_pair(q)
    qa = q * sq
    sig, nsig = _sigmoid_pair(fl)
    one_lb = 1.0 - lbv
    f = lbv + one_lb * sig
    fc = jnp.maximum(f, TINY)
    lf = jnp.log(fc)
    k = one_lb * nsig
    b = _split_dot_left(tri, lf, 3)
    row = lax.broadcasted_iota(jnp.int32, b.shape, 0)
    bm = jnp.sum(jnp.where(row == C // 2 - 1, b, 0.0), axis=0, keepdims=True)
    bl = jnp.sum(jnp.where(row == C - 1, b, 0.0), axis=0, keepdims=True)
    eb = jnp.exp(b)
    ebm = jnp.exp(b - bm)
    enbm = jnp.exp(bm - b)
    ebl = jnp.exp(bl - b)
    ebL = jnp.exp(bl)

    def operand(t):
        return t.astype(BF16).astype(F32)

    return dict(sq=sq, nsq=nsq, qa=qa, sig=sig, nsig=nsig, one_lb=one_lb, f=f, fc=fc, k=k,
                eb=eb, ebm=ebm, enbm=enbm, ebl=ebl, ebL=ebL,
                Qm=operand(qa * ebm), Km=operand(k * enbm), Qb=operand(qa * eb), Kh=operand(k * ebl), row=row)


def _causal_incl(C):
    r = lax.broadcasted_iota(jnp.int32, (C, C), 0)
    c = lax.broadcasted_iota(jnp.int32, (C, C), 1)
    return r >= c


def _hg_fwd(proj, lb, out_g, *, n_heads, cg, name, side=None):
    S = proj.shape[0]
    H = n_heads
    W = H * HEAD
    T = cg * CHUNK
    NG = S // T
    tri = jnp.tril(jnp.ones((CHUNK, CHUNK), F32)).astype(BF16)

    def body(q_ref, f_ref, v_ref, g_ref, lb_ref, og_ref, tri_ref, o_ref, on_ref, st_ref, s_scr):
        @pl.when(pl.program_id(1) == 0)
        def _():
            s_scr[...] = jnp.zeros_like(s_scr)

        lbv = lb_ref[...]
        ogv = og_ref[...]
        triv = tri_ref[...]
        mask = _causal_incl(CHUNK)
        for c in range(cg):
            rows = pl.ds(c * CHUNK, CHUNK)
            v = v_ref[rows, :]
            gg = g_ref[rows, :]
            cm = _hg_chunk(q_ref[rows, :], f_ref[rows, :], lbv, triv)
            s0 = s_scr[...]
            st_ref[c] = s0
            A = jnp.where(mask, _bdot(cm["Qm"], cm["Km"], _NT), 0.0)
            o = _bdot(A, v) + _bdot(cm["Qb"], s0, _NT)
            s_scr[...] = s0 * cm["ebL"] + _bdot(v, cm["Kh"], _TN)
            o_ref[rows, :] = o
            rstd = lax.rsqrt(jnp.mean(o * o, axis=-1, keepdims=True) + EPS)
            sg, _ = _sigmoid_pair(gg)
            on_ref[rows, :] = (((o * rstd) * ogv) * (gg * sg)).astype(BF16)

    def col(group):
        return pl.BlockSpec((T, HEAD), lambda h, g: (g, group * H + h))

    vec = pl.BlockSpec((1, HEAD), lambda h, g: (0, h))
    return _pcall(
        body, grid=(H, NG),
        in_specs=[col(0), col(1), col(2), col(3), vec, vec,
                  pl.BlockSpec((CHUNK, CHUNK), lambda h, g: (0, 0))],
        out_specs=(pl.BlockSpec((T, HEAD), lambda h, g: (g, h)),
                   pl.BlockSpec((T, HEAD), lambda h, g: (g, h)),
                   pl.BlockSpec((cg, None, HEAD, HEAD), lambda h, g: (g, h, 0, 0))),
        out_shape=(jax.ShapeDtypeStruct((S, W), F32), jax.ShapeDtypeStruct((S, W), BF16),
                   jax.ShapeDtypeStruct((S // CHUNK, H, HEAD, HEAD), F32)),
        scratch_shapes=[pltpu.VMEM((HEAD, HEAD), F32)],
        semantics=("parallel", "arbitrary"), name=name,
        args=(proj, proj, proj, proj, lb, out_g, tri), side=side)


def _hg_bwd(proj, o_pre, d_on, d_on_col0, states, lb, out_g, *, n_heads, cg, name, side=None):
    S = proj.shape[0]
    H = n_heads
    W = H * HEAD
    T = cg * CHUNK
    NG = S // T
    tri = jnp.tril(jnp.ones((CHUNK, CHUNK), F32)).astype(BF16)
    triu = jnp.triu(jnp.ones((CHUNK, CHUNK), F32)).astype(BF16)

    def body(q_ref, f_ref, v_ref, g_ref, o_ref, dy_ref, st_ref, lb_ref, og_ref, tri_ref, triu_ref,
             dq_ref, df_ref, di_ref, dg_ref, dlb_ref, dog_ref, ds_scr):
        gstep = pl.program_id(1)

        @pl.when(gstep == 0)
        def _():
            ds_scr[...] = jnp.zeros_like(ds_scr)
            dlb_ref[...] = jnp.zeros_like(dlb_ref)
            dog_ref[...] = jnp.zeros_like(dog_ref)

        lbv = lb_ref[...]
        ogv = og_ref[...]
        triv = tri_ref[...]
        triuv = triu_ref[...]
        mask = _causal_incl(CHUNK)
        dlb_acc = jnp.zeros((1, HEAD), F32)
        dog_acc = jnp.zeros((1, HEAD), F32)
        for c in reversed(range(cg)):
            rows = pl.ds(c * CHUNK, CHUNK)
            q = q_ref[rows, :]
            v = v_ref[rows, :]
            gg = g_ref[rows, :]
            o = o_ref[rows, :]
            dy = dy_ref[rows, :]
            cm = _hg_chunk(q, f_ref[rows, :], lbv, triv)
            s0 = st_ref[c]
            ds1 = ds_scr[...]
            rstd = lax.rsqrt(jnp.mean(o * o, axis=-1, keepdims=True) + EPS)
            n = o * rstd
            sg, nsg = _sigmoid_pair(gg)
            silu_g = gg * sg
            dyn = dy * n
            dog_acc = dog_acc + jnp.sum(dyn * silu_g, axis=0, keepdims=True)
            dg_ref[rows, :] = (dyn * ogv * (sg * (1.0 + gg * nsg))).astype(BF16)
            dn = dy * (ogv * silu_g)
            d_o = rstd * (dn - n * jnp.mean(dn * n, axis=-1, keepdims=True))
            A = jnp.where(mask, _bdot(cm["Qm"], cm["Km"], _NT), 0.0)
            dA = jnp.where(mask, _bdot(d_o, v, _NT), 0.0)
            dV = _bdot(A, d_o, _TN) + _bdot(cm["Kh"], ds1, _NT)
            dQm = _bdot(dA, cm["Km"])
            dKm = _bdot(dA, cm["Qm"], _TN)
            dQb = _bdot(d_o, s0)
            dKh = _bdot(v, ds1)
            ds_scr[...] = ds1 * cm["ebL"] + _bdot(d_o, cm["Qb"], _TN)
            kh_term = dKh * cm["Kh"]
            db = dQm * cm["Qm"] - dKm * cm["Km"] + dQb * cm["Qb"] - kh_term
            dbl = (jnp.sum(kh_term, axis=0, keepdims=True)
                   + cm["ebL"] * jnp.sum(ds1 * s0, axis=0, keepdims=True))
            db = db + jnp.where(cm["row"] == CHUNK - 1, dbl, 0.0)
            dlf = _split_dot_left(triuv, db, 3)
            dqa = dQm * cm["ebm"] + dQb * cm["eb"]
            dq_ref[rows, :] = (dqa * (cm["sq"] * (1.0 + q * cm["nsq"]))).astype(BF16)
            dk = dKm * cm["enbm"] + dKh * cm["ebl"]
            dfc = jnp.where(cm["f"] > TINY, dlf / cm["fc"], 0.0)
            t = dfc - dk
            df_ref[rows, :] = (t * (cm["one_lb"] * cm["sig"] * cm["nsig"])).astype(BF16)
            dlb_acc = dlb_acc + jnp.sum(t * cm["nsig"], axis=0, keepdims=True)
            di_ref[rows, :] = dV.astype(BF16)
        dlb_ref[...] += dlb_acc
        dog_ref[...] += dog_acc

    def col(group):
        return pl.BlockSpec((T, HEAD), lambda h, g: (NG - 1 - g, group * H + h))

    own = pl.BlockSpec((T, HEAD), lambda h, g: (NG - 1 - g, h))
    vec = pl.BlockSpec((1, HEAD), lambda h, g: (0, h))
    cst = pl.BlockSpec((CHUNK, CHUNK), lambda h, g: (0, 0))
    act = jax.ShapeDtypeStruct((S, W), BF16)
    vec_shape = jax.ShapeDtypeStruct((1, W), F32)
    return _pcall(
        body, grid=(H, NG),
        in_specs=[col(0), col(1), col(2), col(3), own,
                  pl.BlockSpec((T, HEAD), lambda h, g: (NG - 1 - g, d_on_col0 + h)),
                  pl.BlockSpec((cg, None, HEAD, HEAD), lambda h, g: (NG - 1 - g, h, 0, 0)),
                  vec, vec, cst, cst],
        out_specs=(own, own, own, own, vec, vec),
        out_shape=(act, act, act, act, vec_shape, vec_shape),
        scratch_shapes=[pltpu.VMEM((HEAD, HEAD), F32)],
        semantics=("parallel", "arbitrary"), name=name,
        args=(proj, proj, proj, proj, o_pre, d_on, states, lb, out_g, tri, triu), side=side)


def _sb_pre(proj, q_g, k_g, *, n_heads, col0, tm, name):
    S = proj.shape[0]
    H = n_heads
    W = H * HEAD

    def body(q_ref, k_ref, v_ref, qg_ref, kg_ref, qh_ref, kh_ref, vh_ref):
        for src, g_ref, dst in ((q_ref, qg_ref, qh_ref), (k_ref, kg_ref, kh_ref)):
            xv = src[...]
            rstd = lax.rsqrt(jnp.mean(xv * xv, axis=-1, keepdims=True) + EPS)
            dst[...] = ((xv * rstd) * g_ref[...]).astype(BF16)
        vh_ref[...] = v_ref[...].astype(BF16)

    def col(group):
        return pl.BlockSpec((tm, HEAD), lambda i, h: (i, col0 + group * H + h))

    vec = pl.BlockSpec((1, HEAD), lambda i, h: (0, 0))
    own = pl.BlockSpec((tm, HEAD), lambda i, h: (i, h))
    act = jax.ShapeDtypeStruct((S, W), BF16)
    return pl.pallas_call(
        body, grid=(S // tm, H), in_specs=[col(0), col(1), col(2), vec, vec],
        out_specs=(own, own, own), out_shape=(act, act, act),
        compiler_params=_params(("parallel", "parallel")), name=name)(proj, proj, proj, q_g, k_g)


def _sb_scores(q, k_blk, scale):
    z = lax.dot_general(q, k_blk, _NT, preferred_element_type=F32) * scale
    e = jnp.exp(-jnp.abs(z))
    sp = jnp.maximum(z, 0.0) + jnp.log(1.0 + e)
    return z, e, sp


def _heads_per_step(n_heads):
    return 2 if n_heads % 2 == 0 else 1


def _max_all(values):
    m = jnp.max(values[0])
    for v in values[1:]:
        m = jnp.maximum(m, jnp.max(v))
    return m


def _strict_lower_mask(t):
    r = lax.broadcasted_iota(jnp.int32, (t, t), 0)
    c = lax.broadcasted_iota(jnp.int32, (t, t), 1)
    return c < r


def _sb_fwd(qh, kh, vh, out_g, *, n_heads, tq, name, side=None):
    S, W = qh.shape
    HP = _heads_per_step(n_heads)
    WP = HP * HEAD
    NQ = S // tq
    scale = HEAD ** -0.5
    u_strict = jnp.tril(jnp.ones((tq, tq), F32), -1).astype(BF16)

    def body(q_ref, k_ref, v_ref, og_ref, u_ref, o_ref, on_ref):
        qi = pl.program_id(1)
        u = u_ref[...]
        heads = [slice(hh * HEAD, (hh + 1) * HEAD) for hh in range(HP)]
        qs = [q_ref[:, cols] for cols in heads]

        def block(kb, r_carry, diag, valid=None):
            rows = pl.ds(pl.multiple_of(kb * tq, tq), tq)
            pvs, rs = [], []
            for hh, cols in enumerate(heads):
                k_blk = k_ref[rows, cols]
                v_blk = v_ref[rows, cols]
                z, _, sp = _sb_scores(qs[hh], k_blk, scale)
                if diag:
                    m = _strict_lower_mask(tq)
                    L = jnp.where(m, -sp, 0.0)
                else:
                    L = -sp
                C = _split_dot(L, u, 2)
                a = jnp.exp(z - sp + C + r_carry[hh])
                if diag:
                    a = jnp.where(m, a, 0.0)
                if valid is not None:
                    a = jnp.where(valid, a, 0.0)
                pvs.append(lax.dot_general(a.astype(BF16), v_blk, (((1,), (0,)), ((), ())),
                                           preferred_element_type=F32))
                rs.append(r_carry[hh] + (C[:, 0:1] + L[:, 0:1]))
            return tuple(pvs), tuple(rs)

        acc_d, r_d = block(qi, (jnp.zeros((tq, 1), F32),) * HP, True)
        acc_p, r0 = block(jnp.maximum(qi - 1, 0), r_d, False, valid=qi > 0)
        acc0 = tuple(a + b for a, b in zip(acc_d, acc_p))

        def cond(st):
            kb, _, _, rmax = st
            return jnp.logical_and(kb >= 0, rmax > SKIP_LOG)

        def step(st):
            kb, acc, r, _ = st
            pv, r2 = block(kb, r, False)
            return kb - 1, tuple(a + b for a, b in zip(acc, pv)), r2, _max_all(r2)

        _, accs, _, _ = lax.while_loop(cond, step, (qi - 2, acc0, r0, _max_all(r0)))
        for hh, cols in enumerate(heads):
            acc = accs[hh]
            o_ref[:, cols] = acc
            rstd = lax.rsqrt(jnp.mean(acc * acc, axis=-1, keepdims=True) + EPS)
            on_ref[:, cols] = ((acc * rstd) * og_ref[:, cols]).astype(BF16)

    blk = pl.BlockSpec((tq, WP), lambda h, i: (i, h))
    full = pl.BlockSpec((S, WP), lambda h, i: (0, h))
    return _pcall(
        body, grid=(n_heads // HP, NQ),
        in_specs=[blk, full, full, pl.BlockSpec((1, WP), lambda h, i: (0, h)),
                  pl.BlockSpec((tq, tq), lambda h, i: (0, 0))],
        out_specs=(blk, blk),
        out_shape=(jax.ShapeDtypeStruct((S, W), F32), jax.ShapeDtypeStruct((S, W), BF16)),
        semantics=("parallel", "arbitrary"), name=name, args=(qh, kh, vh, out_g, u_strict), side=side)


def _sb_bwd(qh, kh, vh, o_pre, d_on, d_on_col0, out_g, *, n_heads, tq, name, side=None):
    S, W = qh.shape
    HP = _heads_per_step(n_heads)
    WP = HP * HEAD
    assert d_on_col0 % HP == 0
    NQ = S // tq
    scale = HEAD ** -0.5
    u_strict = jnp.tril(jnp.ones((tq, tq), F32), -1).astype(BF16)
    u_incl = jnp.tril(jnp.ones((tq, tq), F32)).astype(BF16)

    def body(q_ref, k_ref, v_ref, o_ref, dy_ref, og_ref, us_ref, ui_ref,
             dq_ref, dk_ref, dv_ref, dog_ref):
        qi = pl.program_id(1)

        @pl.when(qi == 0)
        def _():
            dk_ref[...] = jnp.zeros_like(dk_ref)
            dv_ref[...] = jnp.zeros_like(dv_ref)
            dog_ref[...] = jnp.zeros_like(dog_ref)

        us = us_ref[...]
        ui = ui_ref[...]
        heads = [slice(hh * HEAD, (hh + 1) * HEAD) for hh in range(HP)]
        qs, d_obs, deltas = [], [], []
        for cols in heads:
            qs.append(q_ref[:, cols])
            o = o_ref[:, cols]
            dy = dy_ref[:, cols]
            rstd = lax.rsqrt(jnp.mean(o * o, axis=-1, keepdims=True) + EPS)
            n = o * rstd
            dog_ref[:, cols] += jnp.sum(dy * n, axis=0, keepdims=True)
            dn = dy * og_ref[:, cols]
            d_o = rstd * (dn - n * jnp.mean(dn * n, axis=-1, keepdims=True))
            d_ob = d_o.astype(BF16)
            d_obs.append(d_ob)
            deltas.append(jnp.sum(d_ob.astype(F32) * o, axis=-1, keepdims=True))
        q_ts = [q.T for q in qs]
        d_ob_ts = [d.T for d in d_obs]

        def block(kb, r_carry, g_carry, diag, valid=None):
            rows = pl.ds(pl.multiple_of(kb * tq, tq), tq)
            dqs, rs, gs = [], [], []
            for hh, cols in enumerate(heads):
                k_blk = k_ref[rows, cols]
                v_blk = v_ref[rows, cols]
                z, e, sp = _sb_scores(qs[hh], k_blk, scale)
                if diag:
                    m = _strict_lower_mask(tq)
                    L = jnp.where(m, -sp, 0.0)
                else:
                    L = -sp
                C = _split_dot(L, us, 2)
                a = jnp.exp(z - sp + C + r_carry[hh])
                if diag:
                    a = jnp.where(m, a, 0.0)
                if valid is not None:
                    a = jnp.where(valid, a, 0.0)
                ab = a.astype(BF16)
                dA = lax.dot_general(d_obs[hh], v_blk, _NT, preferred_element_type=F32)
                G = ab.astype(F32) * dA
                SI = _split_dot(G, ui, 2)
                P = deltas[hh] - (g_carry[hh] + SI)
                r = 1.0 / (1.0 + e)
                sig = jnp.where(z >= 0, r, e * r)
                dz = G - (G + P) * sig
                if diag:
                    dz = jnp.where(m, dz, 0.0)
                if valid is not None:
                    dz = jnp.where(valid, dz, 0.0)
                dzb = (dz * scale).astype(BF16)
                dqs.append(lax.dot_general(dzb, k_blk, (((1,), (0,)), ((), ())), preferred_element_type=F32))
                dk_ref[cols, rows] += jnp.dot(q_ts[hh], dzb, preferred_element_type=F32)
                dv_ref[cols, rows] += jnp.dot(d_ob_ts[hh], ab, preferred_element_type=F32)
                rs.append(r_carry[hh] + (C[:, 0:1] + L[:, 0:1]))
                gs.append(g_carry[hh] + SI[:, 0:1])
            return tuple(dqs), tuple(rs), tuple(gs)

        zero = (jnp.zeros((tq, 1), F32),) * HP
        dq_d, r_d, g_d = block(qi, zero, zero, True)
        dq_p, r0, g0 = block(jnp.maximum(qi - 1, 0), r_d, g_d, False, valid=qi > 0)
        dq0 = tuple(a + b for a, b in zip(dq_d, dq_p))

        def cond(st):
            kb, _, _, _, rmax = st
            return jnp.logical_and(kb >= 0, rmax > SKIP_LOG)

        def step(st):
            kb, dq, r, g, _ = st
            dq_part, r2, g2 = block(kb, r, g, False)
            return kb - 1, tuple(a + b for a, b in zip(dq, dq_part)), r2, g2, _max_all(r2)

        _, dqs, _, _, _ = lax.while_loop(cond, step, (qi - 2, dq0, r0, g0, _max_all(r0)))
        for hh, cols in enumerate(heads):
            dq_ref[:, cols] = dqs[hh]

    blk = pl.BlockSpec((tq, WP), lambda h, i: (i, h))
    full = pl.BlockSpec((S, WP), lambda h, i: (0, h))
    vec = pl.BlockSpec((1, WP), lambda h, i: (0, h))
    cst = pl.BlockSpec((tq, tq), lambda h, i: (0, 0))
    act = jax.ShapeDtypeStruct((S, W), F32)
    act_t = jax.ShapeDtypeStruct((W, S), F32)
    full_t = pl.BlockSpec((WP, S), lambda h, i: (h, 0))
    return _pcall(
        body, grid=(n_heads // HP, NQ),
        in_specs=[blk, full, full, blk,
                  pl.BlockSpec((tq, WP), lambda h, i: (i, d_on_col0 // HP + h)), vec, cst, cst],
        out_specs=(blk, full_t, full_t, vec),
        out_shape=(act, act_t, act_t, jax.ShapeDtypeStruct((1, W), F32)),
        semantics=("parallel", "arbitrary"), name=name,
        args=(qh, kh, vh, o_pre, d_on, out_g, u_strict, u_incl), side=side)


def _sb_pre_bwd(proj, dqh, dkh, dvh, q_g, k_g, *, n_heads, col0, tm, name):
    S = proj.shape[0]
    H = n_heads
    W = H * HEAD

    def body(q_ref, k_ref, dqh_ref, dkh_ref, dvh_ref, qg_ref, kg_ref,
             dq_ref, dk_ref, dv_ref, dqg_ref, dkg_ref):
        first = jnp.logical_and(pl.program_id(0) == 0, pl.program_id(1) == 0)

        @pl.when(first)
        def _():
            dqg_ref[...] = jnp.zeros_like(dqg_ref)
            dkg_ref[...] = jnp.zeros_like(dkg_ref)

        for src, dh, g_ref, dst, dg_ref in ((q_ref, dqh_ref[...], qg_ref, dq_ref, dqg_ref),
                                            (k_ref, dkh_ref[...].T, kg_ref, dk_ref, dkg_ref)):
            xv = src[...]
            rstd = lax.rsqrt(jnp.mean(xv * xv, axis=-1, keepdims=True) + EPS)
            n = xv * rstd
            dg_ref[...] += jnp.sum(dh * n, axis=0, keepdims=True)
            dn = dh * g_ref[...]
            dst[...] = (rstd * (dn - n * jnp.mean(dn * n, axis=-1, keepdims=True))).astype(BF16)
        dv_ref[...] = dvh_ref[...].T.astype(BF16)

    def col(group):
        return pl.BlockSpec((tm, HEAD), lambda i, h: (i, col0 + group * H + h))

    vec = pl.BlockSpec((1, HEAD), lambda i, h: (0, 0))
    own = pl.BlockSpec((tm, HEAD), lambda i, h: (i, h))
    own_t = pl.BlockSpec((HEAD, tm), lambda i, h: (h, i))
    act = jax.ShapeDtypeStruct((S, W), BF16)
    vec_shape = jax.ShapeDtypeStruct((1, HEAD), F32)
    return pl.pallas_call(
        body, grid=(S // tm, H), in_specs=[col(0), col(1), own, own_t, own_t, vec, vec],
        out_specs=(own, own, own, vec, vec), out_shape=(act, act, act, vec_shape, vec_shape),
        compiler_params=_params(("arbitrary", "arbitrary")), name=name,
    )(proj, proj, dqh, dkh, dvh, q_g, k_g)


def _softmax_rows(x_ref, L):
    rows = [x_ref[l:l + 1, :] for l in range(L)]
    mx = rows[0]
    for r in rows[1:]:
        mx = jnp.maximum(mx, r)
    ex = [jnp.exp(r - mx) for r in rows]
    tot = ex[0]
    for e in ex[1:]:
        tot = tot + e
    return [e / tot for e in ex]


def _lb_fwd(logits, *, name):
    L, W = logits.shape

    def body(x_ref, o_ref):
        s = _softmax_rows(x_ref, L)
        run = jnp.zeros((1, W), F32)
        for l in range(L):
            run = run + s[l]
            o_ref[l:l + 1, :] = run - s[0]

    return pl.pallas_call(body, out_shape=jax.ShapeDtypeStruct((L, W), F32), name=name)(logits)


def _lb_bwd(logits, dlb_parts, *, name):
    L, W = logits.shape
    P = dlb_parts.shape[0]

    def body(x_ref, d_ref, o_ref):
        s = _softmax_rows(x_ref, L)
        dlb = []
        for l in range(L):
            t = d_ref[0, l:l + 1, :]
            for q in range(1, P):
                t = t + d_ref[q, l:l + 1, :]
            dlb.append(t)
        ds = [None] * L
        run = jnp.zeros((1, W), F32)
        for j in reversed(range(L)):
            run = run + dlb[j]
            ds[j] = run
        ds[0] = jnp.zeros((1, W), F32)
        inner = jnp.zeros((1, W), F32)
        for j in range(L):
            inner = inner + s[j] * ds[j]
        for j in range(L):
            o_ref[j:j + 1, :] = s[j] * (ds[j] - inner)

    return pl.pallas_call(body, out_shape=jax.ShapeDtypeStruct((L, W), F32), name=name)(logits, dlb_parts)


def _ada_mod(c_all, w_ada, *, nb, name):
    L, D, n = w_ada.shape
    B = c_all.shape[0]

    def body(c_ref, w_ref, o_ref, cond_ref):
        cv = c_ref[...]
        s, _ = _sigmoid_pair(cv)
        cond = cv * s
        cond_ref[...] = cond
        o_ref[...] = _bdot(cond, w_ref[...])

    return pl.pallas_call(
        body, grid=(L, n // nb),
        in_specs=[pl.BlockSpec((B, D), lambda l, j: (0, 0)),
                  pl.BlockSpec((None, D, nb), lambda l, j: (l, 0, j))],
        out_specs=(pl.BlockSpec((None, B, nb), lambda l, j: (l, 0, j)),
                   pl.BlockSpec((B, D), lambda l, j: (0, 0))),
        out_shape=(jax.ShapeDtypeStruct((L, B, n), F32), jax.ShapeDtypeStruct((B, D), F32)),
        compiler_params=_params(("arbitrary", "arbitrary")), name=name)(c_all, w_ada)


def _adam_math(w, g, m, v):
    m2 = ADAM_B1 * m + (1.0 - ADAM_B1) * g
    v2 = ADAM_B2 * v + (1.0 - ADAM_B2) * (g * g)
    m_hat = m2 / (1.0 - ADAM_B1 ** ADAM_STEP)
    v_hat = v2 / (1.0 - ADAM_B2 ** ADAM_STEP)
    delta = -ADAM_LR * (m_hat / (jnp.sqrt(v_hat) + ADAM_EPS) + ADAM_WD * w)
    return delta, m2, v2


def _adamw(w, m, v, gparts, *, tr, name):
    R, C = w.shape
    P = gparts.shape[0]

    def body(w_ref, m_ref, v_ref, gp_ref, g_ref, d_ref, m2_ref, v2_ref):
        g = gp_ref[0].astype(F32)
        for p in range(1, P):
            g = g + gp_ref[p].astype(F32)
        delta, m2, v2 = _adam_math(w_ref[...], g, m_ref[...], v_ref[...])
        g_ref[...] = g
        d_ref[...] = delta
        m2_ref[...] = m2
        v2_ref[...] = v2

    tile = pl.BlockSpec((tr, C), lambda i: (i, 0))
    shp = jax.ShapeDtypeStruct((R, C), F32)
    return pl.pallas_call(
        body, grid=(R // tr,),
        in_specs=[tile, tile, tile, pl.BlockSpec((P, tr, C), lambda i: (0, i, 0))],
        out_specs=(tile, tile, tile, tile), out_shape=(shp, shp, shp, shp),
        compiler_params=_params(("parallel",)), name=name)(w, m, v, gparts)


def _adamw_layers(w, m, v, gparts, *, tr, name, side=None):
    L, R, C = w.shape
    P = gparts[0].shape[0]
    nblk = R // tr

    def body(*refs):
        w_ref, m_ref, v_ref = refs[:3]
        gp_refs = refs[3:3 + L]
        g_ref, d_ref, m2_ref, v2_ref = refs[3 + L:]
        layer = pl.program_id(0)
        for t in range(L):
            @pl.when(layer == t)
            def _(t=t):
                g = gp_refs[t][0].astype(F32)
                for q in range(1, P):
                    g = g + gp_refs[t][q].astype(F32)
                delta, m2, v2 = _adam_math(w_ref[...], g, m_ref[...], v_ref[...])
                g_ref[...] = g
                d_ref[...] = delta
                m2_ref[...] = m2
                v2_ref[...] = v2

    def gp_spec(t):
        def index(l, i):
            return (0, jnp.where(l == t, i, jnp.where(l < t, 0, nblk - 1)), 0)
        return pl.BlockSpec((P, tr, C), index)

    tile = pl.BlockSpec((None, tr, C), lambda l, i: (l, i, 0))
    shp = jax.ShapeDtypeStruct((L, R, C), F32)
    return _pcall(
        body, grid=(L, nblk), in_specs=[tile, tile, tile] + [gp_spec(t) for t in range(L)],
        out_specs=(tile, tile, tile, tile), out_shape=(shp, shp, shp, shp),
        semantics=("arbitrary", "arbitrary"), name=name, args=(w, m, v, *gparts), side=side)


def _adamw_ada(w, m, v, cond_t, dmod, *, tr, name):
    L, D, n = w.shape
    Bp = cond_t.shape[1]

    def body(w_ref, m_ref, v_ref, c_ref, dm_ref, g_ref, d_ref, m2_ref, v2_ref):
        g = _bdot(c_ref[...], dm_ref[...])
        delta, m2, v2 = _adam_math(w_ref[...], g, m_ref[...], v_ref[...])
        g_ref[...] = g
        d_ref[...] = delta
        m2_ref[...] = m2
        v2_ref[...] = v2

    tile = pl.BlockSpec((None, tr, n), lambda l, i: (l, i, 0))
    shp = jax.ShapeDtypeStruct((L, D, n), F32)
    return pl.pallas_call(
        body, grid=(L, D // tr),
        in_specs=[tile, tile, tile, pl.BlockSpec((tr, Bp), lambda l, i: (i, 0)),
                  pl.BlockSpec((None, Bp, n), lambda l, i: (l, 0, 0))],
        out_specs=(tile, tile, tile, tile), out_shape=(shp, shp, shp, shp),
        compiler_params=_params(("parallel", "parallel")), name=name)(w, m, v, cond_t, dmod)


def _allgather_small(block, *, name):
    R, C = block.shape

    def body(x_ref, out_ref, send_sems, recv_sems, local_sem):
        x, y, c = lax.axis_index("x"), lax.axis_index("y"), lax.axis_index("c")

        def rows(px, py, pc):
            return out_ref.at[pl.ds((4 * px + 2 * py + pc) * R, R), :]

        mine = pltpu.make_async_copy(x_ref, rows(x, y, c), local_sem)
        mine.start()
        sends = []
        for rel in range(1, N_DEV):
            to = _peer(x, y, c, rel)
            cp = pltpu.make_async_remote_copy(src_ref=x_ref, dst_ref=rows(x, y, c),
                                              send_sem=send_sems.at[rel - 1], recv_sem=recv_sems.at[rel - 1],
                                              device_id=to, device_id_type=MESH)
            cp.start()
            sends.append(cp)
        for rel in range(1, N_DEV):
            frm = _peer(x, y, c, rel)
            pltpu.make_async_remote_copy(src_ref=x_ref, dst_ref=rows(*frm),
                                         send_sem=send_sems.at[rel - 1], recv_sem=recv_sems.at[rel - 1],
                                         device_id=frm, device_id_type=MESH).wait_recv()
        for cp in sends:
            cp.wait_send()
        mine.wait()

    return pl.pallas_call(
        body, out_shape=jax.ShapeDtypeStruct((N_DEV * R, C), block.dtype),
        in_specs=[pl.BlockSpec(memory_space=pltpu.VMEM)],
        out_specs=pl.BlockSpec(memory_space=pltpu.VMEM),
        scratch_shapes=[pltpu.SemaphoreType.DMA((N_DEV - 1,)), pltpu.SemaphoreType.DMA((N_DEV - 1,)),
                        pltpu.SemaphoreType.DMA],
        compiler_params=pltpu.CompilerParams(vmem_limit_bytes=V7X_VMEM_LIMIT), name=name)(block)


def _allgather_hbm(shards, *, name):
    n = len(shards)

    def body(*refs):
        ins = refs[:n]
        outs = refs[n:2 * n]
        send_sems, recv_sems, local_sems = refs[2 * n:]
        x, y, c = lax.axis_index("x"), lax.axis_index("y"), lax.axis_index("c")
        sibling = (x, y, 1 - c)
        chips = [(1 - x, y), (x, 1 - y), (1 - x, 1 - y)]

        def slot(t, px, py, pc):
            return outs[t].at[4 * px + 2 * py + pc]

        def copy(t, k, block, to, src=None):
            return pltpu.make_async_remote_copy(
                src_ref=slot(t, *block) if src is None else src, dst_ref=slot(t, *block),
                send_sem=send_sems.at[t * 7 + k], recv_sem=recv_sems.at[t * 7 + k],
                device_id=to, device_id_type=MESH)

        me = (x, y, c)
        started = []
        mine = []
        for t in range(n):
            cp = pltpu.make_async_copy(ins[t], slot(t, *me), local_sems.at[t])
            cp.start()
            mine.append(cp)
            first = [copy(t, 0, me, sibling, src=ins[t])]
            first += [copy(t, 1 + j, me, (*chip, c), src=ins[t]) for j, chip in enumerate(chips)]
            for cp in first:
                cp.start()
            started += first
        for t in range(n):
            for j, chip in enumerate(chips):
                copy(t, 1 + j, (*chip, c), me).wait_recv()
                fwd = copy(t, 4 + j, (*chip, c), sibling)
                fwd.start()
                started.append(fwd)
        for t in range(n):
            copy(t, 0, sibling, me).wait_recv()
            for j, chip in enumerate(chips):
                copy(t, 4 + j, (*chip, 1 - c), me).wait_recv()
        for cp in started:
            cp.wait_send()
        for cp in mine:
            cp.wait()

    any_spec = pl.BlockSpec(memory_space=pl.ANY)
    return pl.pallas_call(
        body, out_shape=[jax.ShapeDtypeStruct((N_DEV,) + s.shape, s.dtype) for s in shards],
        in_specs=[any_spec] * n, out_specs=[any_spec] * n,
        scratch_shapes=[pltpu.SemaphoreType.DMA((7 * n,)), pltpu.SemaphoreType.DMA((7 * n,)),
                        pltpu.SemaphoreType.DMA((n,))],
        name=name)(*shards)


def _tile(total, want):
    step = 128 if total % 128 == 0 else 8
    best = step
    t = step
    while t <= min(total, want):
        if total % t == 0:
            best = t
        t += step
    return best


def _local_step(x, target, mods, lbs, p, wg, shards=None):
    S, D = x.shape
    L = mods.shape[0]
    W = D // 2
    H = W // HEAD
    F = wg["w_ffn_out"][0].shape[0]
    mesh = shards is not None
    tm = _tile(S, 512)
    tm_big = _tile(S, 1024)
    tm_tn = _tile(S, 2048)
    tm_sw = _tile(S, 128)
    tq = _tile(S, 256)
    cg = max(1, min(16, S // CHUNK))
    nb_out = _tile(D, 1024)
    kb_f = _tile(F, 1408)

    def row(a, l):
        return a[l][None, :]

    def gather_of(l, names):
        if mesh and l < L:
            return (True, [shards[k][l] for k in names])
        return None

    def scatter_of(blocks):
        if mesh and blocks is not None:
            return (False, [b.reshape((N_DEV, -1) + b.shape[-1:]) if b.ndim == 2 else b for b in blocks])
        return None

    saved = []
    xcur = x
    for l in range(L):
        mod = mods[l]
        sh1, sc1, g1, sh2, sc2, g2 = [mod[:, i * D:(i + 1) * D] for i in range(N_MOD)]
        h1 = _modnorm_fwd(xcur, row(p["norm1_g"], l), sc1, sh1, tm=tm, name="norm1_fwd")
        proj, got = _mm_nn(h1, wg["w_in"][l], tm=tm_big, name="proj_fwd", side=gather_of(l + 1, ["w_in"]))
        if got is not None:
            wg["w_in"][l + 1] = got[0]
        lb = lbs[l][None, :]
        (o_hg, on_hg, states), got = _hg_fwd(proj, lb, row(p["hg_out_g"], l), n_heads=H, cg=cg, name="hgrn2_fwd",
                                             side=gather_of(l + 1, ["w_out"]))
        if got is not None:
            wg["w_out"][l + 1] = got[0].reshape(-1, D)
        qh, kh, vh = _sb_pre(proj, row(p["sb_q_g"], l), row(p["sb_k_g"], l), n_heads=H, col0=4 * H,
                             tm=tm_tn, name="sb_qknorm_fwd")
        (o_sb, on_sb), got = _sb_fwd(qh, kh, vh, row(p["sb_out_g"], l), n_heads=H, tq=tq, name="sb_fwd",
                                     side=gather_of(l + 1, ["w_ffn_out"]))
        if got is not None:
            wg["w_ffn_out"][l + 1] = got[0].reshape(-1, D)
        o_cat = jnp.concatenate([on_hg, on_sb], axis=1)
        (x1, mixed), _ = _mm_nn(o_cat, wg["w_out"][l], tm=tm_big, nb=nb_out, resid=xcur, gate=g1,
                                name="out_proj_fwd")
        h2 = _modnorm_fwd(x1, row(p["norm2_g"], l), sc2, sh2, tm=tm, name="norm2_fwd")
        w_fin = wg["w_ffn_in"][l]
        if w_fin.shape[0] % 2:
            w_fin = jnp.stack([w_fin[0][:, :F], w_fin[0][:, F:]])
        (gate, up, a), got = _ffn_in_fwd(h2, w_fin, tm=tm, nb=F, name="ffn_in_fwd",
                                         side=gather_of(l + 1, ["w_ffn_in"]))
        if got is not None:
            wg["w_ffn_in"][l + 1] = got[0]
        (x2, ffn), _ = _mm_nn(a, wg["w_ffn_out"][l], tm=tm, nb=nb_out // 2, resid=x1, gate=g2, name="ffn_out_fwd")
        saved.append(dict(x=xcur, h1=h1, proj=proj, o_hg=o_hg, o_sb=o_sb, states=states, qh=qh, kh=kh, vh=vh,
                          o_cat=o_cat, mixed=mixed, x1=x1, h2=h2, gate=gate, up=up, a=a, ffn=ffn, lb=lb,
                          sc1=sc1, g1=g1, sc2=sc2, g2=g2, w_fin=w_fin))
        xcur = x2

    last = saved[-1]
    dx, dffn, dg2, loss = _loss_bwd(xcur, target, last["ffn"], last["g2"], tm=tm, name="loss_bwd")

    big = {k: [None] * L for k in ("w_in", "w_out", "w_ffn_in", "w_ffn_out")}
    small = {k: [None] * L for k in ("norm1_g", "hg_lb", "hg_out_g", "sb_q_g", "sb_k_g", "sb_out_g", "norm2_g")}
    dmods = [None] * L
    pending_in = None
    for l in reversed(range(L)):
        sv = saved[l]
        (dgate, dup), got = _ffn_out_bwd_x(dffn, wg["w_ffn_out"][l], sv["gate"], sv["up"], tm=tm, kb=kb_f,
                                           name="ffn_out_bwd_x", side=scatter_of(pending_in))
        if got is not None:
            big["w_in"][l + 1] = got[0]
        g_fout, _ = _mm_tn(sv["a"], dffn, tm=tm_tn, kb=kb_f, nb=nb_out, blocked=False, name="ffn_out_bwd_w")
        dh2, got = _mm_nt(dgate, sv["w_fin"], dy2=dup, tm=tm_big, kb=D, name="ffn_in_bwd_x",
                          side=scatter_of([g_fout]))
        big["w_ffn_out"][l] = g_fout if got is None else got[0]
        g_fin, _ = _mm_tn(sv["h2"], dgate, dy2=dup, tm=tm_tn, kb=_tile(D, 1024), nb=sv["w_fin"].shape[2],
                          blocked=True, name="ffn_in_bwd_w")
        dx1, dmixed, dg1, dsh2, dsc2, dn2 = _modnorm_bwd(
            sv["x1"], dh2, dx, row(p["norm2_g"], l), sv["sc2"], sv["mixed"], sv["g1"], tm=tm_sw * 2,
            name="norm2_bwd")
        small["norm2_g"][l] = dn2
        d_ocat, _ = _mm_nt(dmixed, wg["w_out"][l], tm=tm_big, kb=nb_out, nb=D, name="out_proj_bwd_x")
        g_out, _ = _mm_tn(sv["o_cat"], dmixed, tm=tm_tn, kb=D, nb=nb_out, blocked=False, name="out_proj_bwd_w")
        (dhq, dhf, dhi, dhg, dlb, dhog), _ = _hg_bwd(sv["proj"], sv["o_hg"], d_ocat, 0, sv["states"], sv["lb"],
                                                     row(p["hg_out_g"], l), n_heads=H, cg=cg, name="hgrn2_bwd")
        (dqh, dkh, dvh, dsog), got = _sb_bwd(sv["qh"], sv["kh"], sv["vh"], sv["o_sb"], d_ocat, H,
                                             row(p["sb_out_g"], l), n_heads=H, tq=tq, name="sb_bwd",
                                             side=scatter_of([g_fin]))
        big["w_ffn_in"][l] = g_fin if got is None else got[0]
        dsq, dsk, dsv, dqg, dkg = _sb_pre_bwd(sv["proj"], dqh, dkh, dvh, row(p["sb_q_g"], l),
                                              row(p["sb_k_g"], l), n_heads=H, col0=4 * H, tm=tm_tn,
                                              name="sb_qknorm_bwd")
        small["hg_lb"][l] = dlb
        small["hg_out_g"][l] = dhog
        small["sb_out_g"][l] = dsog
        small["sb_q_g"][l] = dqg
        small["sb_k_g"][l] = dkg
        dproj = jnp.concatenate([dhq, dhf, dhi, dhg, dsq, dsk, dsv], axis=1)
        g_in, _ = _mm_tn(sv["h1"], dproj, tm=tm_tn, kb=D, nb=wg["w_in"][l].shape[2], blocked=True,
                         name="proj_bwd_w")
        big["w_in"][l] = g_in
        dh1, got = _mm_nt(dproj, wg["w_in"][l], tm=tm_big, kb=D, name="proj_bwd_x",
                          side=scatter_of([g_out, g_in] if l == 0 else [g_out]))
        if got is not None:
            big["w_out"][l] = got[0]
            if l == 0:
                big["w_in"][l] = got[1]
        else:
            big["w_out"][l] = g_out
        pending_in = [g_in]
        if l > 0:
            prev = saved[l - 1]
            dx0, dffn_prev, dg2_prev, dsh1, dsc1, dn1 = _modnorm_bwd(
                sv["x"], dh1, dx1, row(p["norm1_g"], l), sv["sc1"], prev["ffn"], prev["g2"], tm=tm_sw * 2,
                name="norm1_bwd")
        else:
            dx0, dsh1, dsc1, dn1 = _modnorm_bwd(sv["x"], dh1, dx1, row(p["norm1_g"], l), sv["sc1"], None, None,
                                                tm=tm_sw * 2, name="norm1_bwd_first")
            dffn_prev, dg2_prev = None, None
        small["norm1_g"][l] = dn1
        dmods[l] = jnp.concatenate([dsh1, dsc1, dg1, dsh2, dsc2, dg2], axis=1)
        dx, dffn, dg2 = dx0, dffn_prev, dg2_prev
    return loss, dx, big, small, dmods


def kernel(x, c, norm1_g, w_in, hg_lb_logits, hg_out_g, sb_q_g, sb_k_g, sb_out_g, w_out, norm2_g, w_ffn_in, w_ffn_out, w_ada, b_ada, loss_target, m_norm1_g, m_w_in, m_hg_lb_logits, m_hg_out_g, m_sb_q_g, m_sb_k_g, m_sb_out_g, m_w_out, m_norm2_g, m_w_ffn_in, m_w_ffn_out, m_w_ada, m_b_ada, v_norm1_g, v_w_in, v_hg_lb_logits, v_hg_out_g, v_sb_q_g, v_sb_k_g, v_sb_out_g, v_w_out, v_norm2_g, v_w_ffn_in, v_w_ffn_out, v_w_ada, v_b_ada):
    L, D = norm1_g.shape
    S = x.shape[1]
    me = 4 * lax.axis_index("x") + 2 * lax.axis_index("y") + lax.axis_index("c")

    c_all = _allgather_small(jnp.broadcast_to(c, (8, D)), name="gather_c").reshape(N_DEV, 8, D)[:, 0, :]
    n_ada = w_ada.shape[2]
    mod_cols, cond = _ada_mod(c_all, w_ada, nb=_tile(n_ada, 512), name="ada_mod")
    mod_all = _allgather_small(mod_cols.reshape(L * N_DEV, n_ada), name="gather_mod")
    mod_all = mod_all.reshape(N_DEV, L, N_DEV, n_ada)
    mod_mine = lax.dynamic_index_in_dim(mod_all, me, axis=2, keepdims=False)
    mods = jnp.transpose(mod_mine, (1, 0, 2)).reshape(L, 1, N_DEV * n_ada) + b_ada[:, None, :]

    lbs = _lb_fwd(hg_lb_logits, name="lower_bounds_fwd")

    shards = dict(w_in=[w_in[l].astype(BF16) for l in range(L)], w_out=[w_out[l].astype(BF16) for l in range(L)],
                  w_ffn_in=[w_ffn_in[l].astype(BF16) for l in range(L)],
                  w_ffn_out=[w_ffn_out[l].astype(BF16) for l in range(L)])
    g_in, g_out, g_fin, g_fout = _allgather_hbm(
        [shards["w_in"][0], shards["w_out"][0], shards["w_ffn_in"][0], shards["w_ffn_out"][0]],
        name="gather_weights")
    wg = dict(w_in=[g_in] + [None] * (L - 1), w_out=[g_out.reshape(-1, D)] + [None] * (L - 1),
              w_ffn_in=[g_fin] + [None] * (L - 1), w_ffn_out=[g_fout.reshape(-1, D)] + [None] * (L - 1))

    p = dict(norm1_g=norm1_g, hg_out_g=hg_out_g, sb_q_g=sb_q_g, sb_k_g=sb_k_g, sb_out_g=sb_out_g,
             norm2_g=norm2_g)
    loss_part, grad_x, recv, small, dmods = _local_step(x.reshape(S, D), loss_target.reshape(S, D), mods, lbs, p,
                                                        wg, shards)

    dmod = jnp.concatenate(dmods, axis=0)
    pieces = [jnp.concatenate(small[k], axis=0) for k in
              ("norm1_g", "hg_lb", "hg_out_g", "sb_q_g", "sb_k_g", "sb_out_g", "norm2_g")] + [dmod]
    flat = jnp.concatenate([a.reshape(-1) for a in pieces] + [loss_part.reshape(-1)])
    n_flat = flat.shape[0]
    rows = -(-n_flat // 1024) * 8
    flat = jnp.pad(flat, (0, rows * 128 - n_flat)).reshape(rows, 128)
    gathered = _allgather_small(flat, name="gather_small_grads").reshape(N_DEV, rows * 128)

    def take(off, shape):
        size = 1
        for s in shape:
            size *= s
        return gathered[:, off:off + size].reshape((N_DEV,) + tuple(shape)), off + size

    off = 0
    parts = {}
    for k, a in zip(("norm1_g", "hg_lb", "hg_out_g", "sb_q_g", "sb_k_g", "sb_out_g", "norm2_g", "dmod"), pieces):
        parts[k], off = take(off, a.shape)
    loss_parts = gathered[:, off:off + 1]
    loss = jnp.sum(loss_parts)

    def pad8(a):
        return jnp.pad(a, ((0, 0), (0, 8 - a.shape[1]), (0, 0)))

    def small_update(w, m, v, gparts):
        Lw = w.shape[0]
        g, d, m2, v2 = _adamw(pad8(w[None])[0], pad8(m[None])[0], pad8(v[None])[0], pad8(gparts),
                              tr=8, name="adamw_small")
        return g[:Lw], d[:Lw], m2[:Lw], v2[:Lw]

    out = {}
    out["norm1_g"] = small_update(norm1_g, m_norm1_g, v_norm1_g, parts["norm1_g"])
    dlogits = _lb_bwd(hg_lb_logits, parts["hg_lb"], name="lower_bounds_bwd")
    out["hg_lb_logits"] = small_update(hg_lb_logits, m_hg_lb_logits, v_hg_lb_logits, dlogits[None])
    out["hg_out_g"] = small_update(hg_out_g, m_hg_out_g, v_hg_out_g, parts["hg_out_g"])
    out["sb_q_g"] = small_update(sb_q_g, m_sb_q_g, v_sb_q_g, parts["sb_q_g"])
    out["sb_k_g"] = small_update(sb_k_g, m_sb_k_g, v_sb_k_g, parts["sb_k_g"])
    out["sb_out_g"] = small_update(sb_out_g, m_sb_out_g, v_sb_out_g, parts["sb_out_g"])
    out["norm2_g"] = small_update(norm2_g, m_norm2_g, v_norm2_g, parts["norm2_g"])
    out["b_ada"] = small_update(b_ada, m_b_ada, v_b_ada, parts["dmod"])

    dmod_all = parts["dmod"].reshape(N_DEV, L, N_DEV, n_ada)
    dmod_mine = lax.dynamic_index_in_dim(dmod_all, me, axis=2, keepdims=False)
    dmod_mine = jnp.pad(jnp.transpose(dmod_mine, (1, 0, 2)), ((0, 0), (0, 128 - N_DEV), (0, 0)))
    cond_t = jnp.pad(jnp.transpose(cond), ((0, 0), (0, 128 - N_DEV)))
    out["w_ada"] = _adamw_ada(w_ada, m_w_ada, v_w_ada, cond_t, dmod_mine, tr=_tile(D, 256), name="adamw_ada")

    def big_update(w, m, v, recv_l, name):
        return _adamw_layers(w, m, v, recv_l, tr=_tile(w.shape[1], 131072 // w.shape[2]), name=name)[0]

    out["w_ffn_in"] = big_update(w_ffn_in, m_w_ffn_in, v_w_ffn_in, recv["w_ffn_in"], "adamw_w_ffn_in")
    out["w_ffn_out"] = big_update(w_ffn_out, m_w_ffn_out, v_w_ffn_out, recv["w_ffn_out"], "adamw_w_ffn_out")
    out["w_out"] = big_update(w_out, m_w_out, v_w_out, recv["w_out"], "adamw_w_out")
    out["w_in"] = big_update(w_in, m_w_in, v_w_in, recv["w_in"], "adamw_w_in")

    order = ("norm1_g", "w_in", "hg_lb_logits", "hg_out_g", "sb_q_g", "sb_k_g", "sb_out_g", "w_out", "norm2_g",
             "w_ffn_in", "w_ffn_out", "w_ada", "b_ada")
    grads = [out[k][0] for k in order]
    deltas = [out[k][1] for k in order]
    new_m = [out[k][2] for k in order]
    new_v = [out[k][3] for k in order]
    return (loss, grad_x.reshape(1, S, D), *grads, *deltas, *new_m, *new_v)
```

```python
import functools

import jax
import jax.numpy as jnp
from jax import lax
from jax.experimental import pallas as pl
from jax.experimental.pallas import tpu as pltpu

F32 = jnp.float32
BF16 = jnp.bfloat16
MESH = pl.DeviceIdType.MESH

N_DEV = 8
HEAD = 128
CHUNK = 64
N_MOD = 6
EPS = 1e-6
TINY = 1e-30
ADAM_LR = 0.001
ADAM_B1 = 0.9
ADAM_B2 = 0.999
ADAM_EPS = 1e-08
ADAM_WD = 0.01
ADAM_STEP = 10
V7X_VMEM_LIMIT = 56 * 1024 * 1024
SKIP_LOG = -104.0


def _params(sem):
    return pltpu.CompilerParams(dimension_semantics=sem, vmem_limit_bytes=V7X_VMEM_LIMIT)


def _bdot(a, b, dims=(((1,), (0,)), ((), ()))):
    return lax.dot_general(a.astype(BF16), b.astype(BF16), dims, preferred_element_type=F32)


_NT = (((1,), (1,)), ((), ()))
_TN = (((0,), (0,)), ((), ()))


def _sigmoid_pair(x):
    e = jnp.exp(-jnp.abs(x))
    r = 1.0 / (1.0 + e)
    er = e * r
    pos = x >= 0
    return jnp.where(pos, r, er), jnp.where(pos, er, r)


def _split_dot(x, u, parts):
    acc = None
    rem = x
    for _ in range(parts):
        p = rem.astype(BF16)
        rem = rem - p.astype(F32)
        t = lax.dot_general(p, u, (((1,), (0,)), ((), ())), preferred_element_type=F32)
        acc = t if acc is None else acc + t
    return acc


def _split_dot_left(u, x, parts):
    acc = None
    rem = x
    for _ in range(parts):
        p = rem.astype(BF16)
        rem = rem - p.astype(F32)
        t = lax.dot_general(u, p, (((1,), (0,)), ((), ())), preferred_element_type=F32)
        acc = t if acc is None else acc + t
    return acc


def _peer(x, y, c, rel):
    return (x ^ ((rel >> 2) & 1), y ^ ((rel >> 1) & 1), c ^ (rel & 1))


def _exchange(gather, srcs, dsts, send_sems, recv_sems, local_sems, phase):
    x, y, c = lax.axis_index("x"), lax.axis_index("y"), lax.axis_index("c")
    me = 4 * x + 2 * y + c
    for t in range(len(srcs)):
        own = srcs[t] if gather else srcs[t].at[me]
        local = pltpu.make_async_copy(own, dsts[t].at[me], local_sems.at[t])
        if phase == "start":
            local.start()
        for rel in range(1, N_DEV):
            px, py, pc = _peer(x, y, c, rel)
            pid = 4 * px + 2 * py + pc
            k = t * (N_DEV - 1) + rel - 1
            if phase == "start":
                pltpu.make_async_remote_copy(
                    src_ref=srcs[t] if gather else srcs[t].at[pid], dst_ref=dsts[t].at[me],
                    send_sem=send_sems.at[k], recv_sem=recv_sems.at[k],
                    device_id=(px, py, pc), device_id_type=MESH).start()
            else:
                cp = pltpu.make_async_remote_copy(
                    src_ref=own, dst_ref=dsts[t].at[pid], send_sem=send_sems.at[k], recv_sem=recv_sems.at[k],
                    device_id=(px, py, pc), device_id_type=MESH)
                cp.wait_recv()
                cp.wait_send()
        if phase == "wait":
            local.wait()


def _gather_two_level(srcs, dsts, send_sems, recv_sems, local_sems, phase):
    x, y, c = lax.axis_index("x"), lax.axis_index("y"), lax.axis_index("c")
    me, sibling = (x, y, c), (x, y, 1 - c)
    chips = [(1 - x, y), (x, 1 - y), (1 - x, 1 - y)]
    per = N_DEV - 1
    for t in range(len(srcs)):
        def slot(px, py, pc, t=t):
            return dsts[t].at[4 * px + 2 * py + pc]

        def copy(k, block, to, src=None, t=t):
            return pltpu.make_async_remote_copy(
                src_ref=slot(*block) if src is None else src, dst_ref=slot(*block),
                send_sem=send_sems.at[t * per + k], recv_sem=recv_sems.at[t * per + k],
                device_id=to, device_id_type=MESH)

        local = pltpu.make_async_copy(srcs[t], slot(*me), local_sems.at[t])
        first = [copy(0, me, sibling, src=srcs[t])]
        first += [copy(1 + j, me, (*chip, c), src=srcs[t]) for j, chip in enumerate(chips)]
        passed = [copy(4 + j, (*chip, c), sibling) for j, chip in enumerate(chips)]
        if phase == "start":
            local.start()
            for cp in first:
                cp.start()
        elif phase == "forward":
            for j, chip in enumerate(chips):
                copy(1 + j, (*chip, c), me).wait_recv()
                passed[j].start()
        else:
            copy(0, sibling, me).wait_recv()
            for j, chip in enumerate(chips):
                copy(4 + j, (*chip, 1 - c), me).wait_recv()
            for cp in first + passed:
                cp.wait_send()
            local.wait()


def _exchange_scratch(n):
    return [pltpu.SemaphoreType.DMA(((N_DEV - 1) * n,)), pltpu.SemaphoreType.DMA(((N_DEV - 1) * n,)),
            pltpu.SemaphoreType.DMA((n,))]


def _pcall(body, *, grid, in_specs, out_specs, out_shape, scratch_shapes=(), semantics, name, args, side=None):
    single = not isinstance(out_shape, (tuple, list))
    if single:
        out_specs, out_shape = [out_specs], [out_shape]
    in_specs, out_specs, out_shape = list(in_specs), list(out_specs), list(out_shape)
    scratch_shapes = list(scratch_shapes)
    n_in, n_out, n_scr = len(in_specs), len(out_specs), len(scratch_shapes)
    if side is None:
        res = pl.pallas_call(body, grid=grid, in_specs=in_specs, out_specs=out_specs, out_shape=out_shape,
                             scratch_shapes=scratch_shapes, compiler_params=_params(semantics), name=name)(*args)
        return (res[0] if single else tuple(res)), None
    gather, srcs = side
    n = len(srcs)

    def full(*refs):
        ins = refs[:n_in]
        s_in = refs[n_in:n_in + n]
        outs = refs[n_in + n:n_in + n + n_out]
        s_out = refs[n_in + n + n_out:n_in + 2 * n + n_out]
        scr = refs[n_in + 2 * n + n_out:n_in + 2 * n + n_out + n_scr]
        send_sems, recv_sems, local_sems = refs[n_in + 2 * n + n_out + n_scr:]
        step = pl.program_id(0)
        steps = grid[0]
        for ax in range(1, len(grid)):
            step = step * grid[ax] + pl.program_id(ax)
            steps *= grid[ax]

        def exchange(phase):
            if gather:
                _gather_two_level(s_in, s_out, send_sems, recv_sems, local_sems, phase)
            elif phase != "forward":
                _exchange(False, s_in, s_out, send_sems, recv_sems, local_sems, phase)

        pl.when(step == 0)(lambda: exchange("start"))
        if gather:
            pl.when(step == steps // 2)(lambda: exchange("forward"))
        body(*ins, *outs, *scr)
        pl.when(step == steps - 1)(lambda: exchange("wait"))

    any_spec = pl.BlockSpec(memory_space=pl.ANY)
    s_shapes = [jax.ShapeDtypeStruct(((N_DEV,) + s.shape) if gather else s.shape, s.dtype) for s in srcs]
    res = pl.pallas_call(full, grid=grid, in_specs=in_specs + [any_spec] * n,
                         out_specs=out_specs + [any_spec] * n, out_shape=out_shape + s_shapes,
                         scratch_shapes=scratch_shapes + _exchange_scratch(n),
                         compiler_params=_params(("arbitrary",) * len(grid)), name=name)(*args, *srcs)
    main = res[:n_out]
    return (main[0] if single else tuple(main)), list(res[n_out:])


def _mm_nn(a, b, *, tm, nb=None, out_dtype=F32, resid=None, gate=None, name, side=None):
    M, K = a.shape
    if b.ndim == 3:
        NB, _, n = b.shape
        b_spec = pl.BlockSpec((None, K, n), lambda j, i: (j, 0, 0))
    else:
        n = nb
        NB = b.shape[1] // nb
        b_spec = pl.BlockSpec((K, n), lambda j, i: (0, j))
    N = NB * n
    epi = resid is not None

    def body(*refs):
        if epi:
            a_ref, b_ref, r_ref, g_ref, o_ref, acc_ref = refs
        else:
            a_ref, b_ref, o_ref = refs
        acc = jnp.dot(a_ref[...], b_ref[...], preferred_element_type=F32)
        if epi:
            o_ref[...] = r_ref[...] + g_ref[...] * acc
            acc_ref[...] = acc.astype(BF16)
        else:
            o_ref[...] = acc.astype(out_dtype)

    in_specs = [pl.BlockSpec((tm, K), lambda j, i: (i, 0)), b_spec]
    args = [a, b]
    o_spec = pl.BlockSpec((tm, n), lambda j, i: (i, j))
    if epi:
        in_specs += [pl.BlockSpec((tm, n), lambda j, i: (i, j)), pl.BlockSpec((1, n), lambda j, i: (0, j))]
        args += [resid, gate]
        out_shape = [jax.ShapeDtypeStruct((M, N), F32), jax.ShapeDtypeStruct((M, N), BF16)]
        out_specs = [o_spec, o_spec]
    else:
        out_shape = [jax.ShapeDtypeStruct((M, N), out_dtype)]
        out_specs = [o_spec]
    res, got = _pcall(body, grid=(NB, M // tm), in_specs=in_specs, out_specs=out_specs, out_shape=out_shape,
                      semantics=("parallel", "parallel"), name=name, args=args, side=side)
    return (res[0] if len(res) == 1 else res), got


def _halves(dy, dy2, blk_rows, blk_cols, nblocks, row_of, col_of, last_row=None):
    if dy2 is None:
        return [pl.BlockSpec((blk_rows, blk_cols), lambda *g: (row_of(*g), col_of(*g)))], None
    half = nblocks // 2

    def left(*g):
        r, c = row_of(*g), col_of(*g)
        if last_row is None:
            return (r, jnp.minimum(c, half - 1))
        return (jnp.where(c < half, r, last_row), jnp.minimum(c, half - 1))

    def right(*g):
        r, c = row_of(*g), col_of(*g)
        if last_row is None:
            return (r, jnp.maximum(c - half, 0))
        return (jnp.where(c >= half, r, 0), jnp.maximum(c - half, 0))

    return [pl.BlockSpec((blk_rows, blk_cols), left), pl.BlockSpec((blk_rows, blk_cols), right)], half


def _mm_nt(dy, w, *, tm, kb, nb=None, name, side=None, dy2=None):
    M = dy.shape[0]
    N = dy.shape[1] * (1 if dy2 is None else 2)
    if w.ndim == 3:
        NB, Kt, n = w.shape
        w_spec = pl.BlockSpec((None, kb, n), lambda i, k, j: (j, k, 0))
    else:
        Kt = w.shape[0]
        n = nb
        NB = N // nb
        w_spec = pl.BlockSpec((kb, n), lambda i, k, j: (k, j))
    KB = Kt // kb
    dy_specs, half = _halves(dy, dy2, tm, n, NB, lambda i, k, j: i, lambda i, k, j: j)
    n_op = len(dy_specs)

    def body(*refs):
        dy_refs = refs[:n_op]
        w_ref = refs[n_op]
        o_ref = refs[n_op + 1]
        acc = refs[n_op + 2:]
        j = pl.program_id(2)

        def use(dy_ref):
            part = lax.dot_general(dy_ref[...], w_ref[...], _NT, preferred_element_type=F32)
            if NB == 1:
                o_ref[...] = part
                return
            acc_ref, = acc

            @pl.when(j == 0)
            def _():
                acc_ref[...] = part

            @pl.when(jnp.logical_and(j > 0, j < NB - 1))
            def _():
                acc_ref[...] += part

            @pl.when(j == NB - 1)
            def _():
                o_ref[...] = acc_ref[...] + part

        if half is None:
            use(dy_refs[0])
        else:
            pl.when(j < half)(lambda: use(dy_refs[0]))
            pl.when(j >= half)(lambda: use(dy_refs[1]))

    return _pcall(
        body, grid=(M // tm, KB, NB), in_specs=dy_specs + [w_spec],
        out_specs=pl.BlockSpec((tm, kb), lambda i, k, j: (i, k)),
        out_shape=jax.ShapeDtypeStruct((M, Kt), F32),
        scratch_shapes=[] if NB == 1 else [pltpu.VMEM((tm, kb), F32)],
        semantics=("parallel", "parallel", "arbitrary"), name=name,
        args=(dy, w) if dy2 is None else (dy, dy2, w), side=side)


def _mm_tn(x, dy, *, tm, kb, nb, blocked, name, side=None, dy2=None):
    M, K = x.shape
    N = dy.shape[1] * (1 if dy2 is None else 2)
    KB, NB, MB = K // kb, N // nb, M // tm
    dy_specs, half = _halves(dy, dy2, tm, nb, NB, lambda k, n, m: m, lambda k, n, m: n, last_row=MB - 1)
    n_dy = len(dy_specs)

    def body(*refs):
        x_ref = refs[0]
        dy_refs = refs[1:1 + n_dy]
        o_ref = refs[1 + n_dy]
        acc = refs[2 + n_dy:]
        m = pl.program_id(2)

        def use(dy_ref):
            part = lax.dot_general(x_ref[...], dy_ref[...], _TN, preferred_element_type=F32)
            if MB == 1:
                o_ref[...] = part.astype(BF16)
                return
            acc_ref, = acc

            @pl.when(m == 0)
            def _():
                acc_ref[...] = part

            @pl.when(jnp.logical_and(m > 0, m < MB - 1))
            def _():
                acc_ref[...] += part

            @pl.when(m == MB - 1)
            def _():
                o_ref[...] = (acc_ref[...] + part).astype(BF16)

        if half is None:
            use(dy_refs[0])
        else:
            nblk = pl.program_id(1)
            pl.when(nblk < half)(lambda: use(dy_refs[0]))
            pl.when(nblk >= half)(lambda: use(dy_refs[1]))

    if blocked:
        out_shape = jax.ShapeDtypeStruct((NB, K, nb), BF16)
        o_spec = pl.BlockSpec((None, kb, nb), lambda k, n, m: (n, k, 0))
    else:
        out_shape = jax.ShapeDtypeStruct((K, N), BF16)
        o_spec = pl.BlockSpec((kb, nb), lambda k, n, m: (k, n))
    x_spec = pl.BlockSpec((tm, kb), lambda k, n, m: (m, k))
    return _pcall(
        body, grid=(KB, NB, MB), in_specs=[x_spec] + dy_specs,
        out_specs=o_spec, out_shape=out_shape,
        scratch_shapes=[] if MB == 1 else [pltpu.VMEM((kb, nb), F32)],
        semantics=("parallel", "parallel", "arbitrary"), name=name,
        args=(x, dy) if dy2 is None else (x, dy, dy2), side=side)


def _ffn_in_fwd(h, w, *, tm, nb, name, side=None):
    M, K = h.shape
    if w.ndim == 3:
        J, _, n = w.shape
        half = J // 2
        specs = [pl.BlockSpec((None, K, n), lambda j, i: (j, 0, 0)),
                 pl.BlockSpec((None, K, n), lambda j, i: (j + half, 0, 0))]
    else:
        n = nb
        half = w.shape[1] // (2 * nb)
        specs = [pl.BlockSpec((K, n), lambda j, i: (0, j)), pl.BlockSpec((K, n), lambda j, i: (0, j + half))]
    F = half * n

    def body(h_ref, wg_ref, wu_ref, gate_ref, up_ref, act_ref):
        hv = h_ref[...]
        gate = jnp.dot(hv, wg_ref[...], preferred_element_type=F32)
        up = jnp.dot(hv, wu_ref[...], preferred_element_type=F32)
        s, _ = _sigmoid_pair(gate)
        gate_ref[...] = gate
        up_ref[...] = up
        act_ref[...] = (gate * s * up).astype(BF16)

    o_spec = pl.BlockSpec((tm, n), lambda j, i: (i, j))
    f32 = jax.ShapeDtypeStruct((M, F), F32)
    return _pcall(
        body, grid=(half, M // tm), in_specs=[pl.BlockSpec((tm, K), lambda j, i: (i, 0))] + specs,
        out_specs=(o_spec, o_spec, o_spec), out_shape=(f32, f32, jax.ShapeDtypeStruct((M, F), BF16)),
        semantics=("parallel", "parallel"), name=name, args=(h, w, w), side=side)


def _ffn_out_bwd_x(dy, w, gate, up, *, tm, kb, name, side=None):
    M, D = dy.shape
    F = w.shape[0]

    def body(dy_ref, w_ref, g_ref, u_ref, dg_ref, du_ref):
        da = lax.dot_general(dy_ref[...], w_ref[...], _NT, preferred_element_type=F32)
        gate = g_ref[...]
        s, ns = _sigmoid_pair(gate)
        dg_ref[...] = (da * u_ref[...] * (s * (1.0 + gate * ns))).astype(BF16)
        du_ref[...] = (da * (gate * s)).astype(BF16)

    tile = pl.BlockSpec((tm, kb), lambda i, k: (i, k))
    act = jax.ShapeDtypeStruct((M, F), BF16)
    return _pcall(
        body, grid=(M // tm, F // kb),
        in_specs=[pl.BlockSpec((tm, D), lambda i, k: (i, 0)), pl.BlockSpec((kb, D), lambda i, k: (k, 0)), tile, tile],
        out_specs=(tile, tile), out_shape=(act, act),
        semantics=("parallel", "parallel"), name=name, args=(dy, w, gate, up), side=side)


def _modnorm_fwd(x, gain, sc, sh, *, tm, name):
    S, D = x.shape

    def body(x_ref, g_ref, sc_ref, sh_ref, h_ref):
        xv = x_ref[...]
        rstd = lax.rsqrt(jnp.mean(xv * xv, axis=-1, keepdims=True) + EPS)
        y = (xv * rstd) * g_ref[...]
        h_ref[...] = (y * (1.0 + sc_ref[...]) + sh_ref[...]).astype(BF16)

    row = pl.BlockSpec((1, D), lambda i: (0, 0))
    return pl.pallas_call(
        body, grid=(S // tm,),
        in_specs=[pl.BlockSpec((tm, D), lambda i: (i, 0)), row, row, row],
        out_specs=pl.BlockSpec((tm, D), lambda i: (i, 0)),
        out_shape=jax.ShapeDtypeStruct((S, D), BF16),
        compiler_params=_params(("parallel",)), name=name)(x, gain, sc, sh)


def _modnorm_bwd(x, dh, dres, gain, sc, branch, gate, *, tm, name):
    S, D = x.shape
    has_prev = branch is not None

    def body(*refs):
        if has_prev:
            (x_ref, dh_ref, dr_ref, g_ref, sc_ref, br_ref, gt_ref,
             dx_ref, dbr_ref, dgt_ref, dsh_ref, dsc_ref, dgn_ref) = refs
        else:
            (x_ref, dh_ref, dr_ref, g_ref, sc_ref,
             dx_ref, dsh_ref, dsc_ref, dgn_ref) = refs
        i = pl.program_id(0)
        xv = x_ref[...]
        dh_v = dh_ref[...]
        gv = g_ref[...]
        scale1 = 1.0 + sc_ref[...]
        rstd = lax.rsqrt(jnp.mean(xv * xv, axis=-1, keepdims=True) + EPS)
        n = xv * rstd
        dn = dh_v * (gv * scale1)
        dx = rstd * (dn - n * jnp.mean(dn * n, axis=-1, keepdims=True)) + dr_ref[...]
        dx_ref[...] = dx
        dhn = dh_v * n
        p_sh = jnp.sum(dh_v, axis=0, keepdims=True)
        p_sc = jnp.sum(dhn, axis=0, keepdims=True) * gv
        p_gn = jnp.sum(dhn, axis=0, keepdims=True) * scale1
        if has_prev:
            dbr_ref[...] = (gt_ref[...] * dx).astype(BF16)
            p_gt = jnp.sum(dx * br_ref[...].astype(F32), axis=0, keepdims=True)

        @pl.when(i == 0)
        def _():
            dsh_ref[...] = p_sh
            dsc_ref[...] = p_sc
            dgn_ref[...] = p_gn
            if has_prev:
                dgt_ref[...] = p_gt

        @pl.when(i > 0)
        def _():
            dsh_ref[...] += p_sh
            dsc_ref[...] += p_sc
            dgn_ref[...] += p_gn
            if has_prev:
                dgt_ref[...] += p_gt

    tile = pl.BlockSpec((tm, D), lambda i: (i, 0))
    row = pl.BlockSpec((1, D), lambda i: (0, 0))
    row_shape = jax.ShapeDtypeStruct((1, D), F32)
    if has_prev:
        in_specs = [tile, tile, tile, row, row, tile, row]
        args = (x, dh, dres, gain, sc, branch, gate)
        out_specs = (tile, tile, row, row, row, row)
        out_shape = (jax.ShapeDtypeStruct((S, D), F32), jax.ShapeDtypeStruct((S, D), BF16),
                     row_shape, row_shape, row_shape, row_shape)
    else:
        in_specs = [tile, tile, tile, row, row]
        args = (x, dh, dres, gain, sc)
        out_specs = (tile, row, row, row)
        out_shape = (jax.ShapeDtypeStruct((S, D), F32), row_shape, row_shape, row_shape)
    return pl.pallas_call(body, grid=(S // tm,), in_specs=in_specs, out_specs=out_specs,
                          out_shape=out_shape, compiler_params=_params(("arbitrary",)),
                          name=name)(*args)


def _loss_bwd(y, target, branch, gate, *, tm, name):
    S, D = y.shape
    nsteps = S // tm

    def body(y_ref, t_ref, br_ref, gt_ref, dy_ref, dbr_ref, dgt_ref, loss_ref, col_ref):
        i = pl.program_id(0)
        diff = y_ref[...] - t_ref[...]
        dy = diff * (1.0 / D)
        dy_ref[...] = dy
        dbr_ref[...] = (gt_ref[...] * dy).astype(BF16)
        p_gt = jnp.sum(dy * br_ref[...].astype(F32), axis=0, keepdims=True)
        p_col = jnp.sum(diff * diff, axis=0, keepdims=True)

        @pl.when(i == 0)
        def _():
            dgt_ref[...] = p_gt
            col_ref[...] = p_col

        @pl.when(i > 0)
        def _():
            dgt_ref[...] += p_gt
            col_ref[...] += p_col

        @pl.when(i == nsteps - 1)
        def _():
            tot = jnp.sum(col_ref[...], axis=-1, keepdims=True) * (0.5 / D)
            loss_ref[...] = jnp.broadcast_to(tot, (1, 128))

    tile = pl.BlockSpec((tm, D), lambda i: (i, 0))
    row = pl.BlockSpec((1, D), lambda i: (0, 0))
    return pl.pallas_call(
        body, grid=(nsteps,), in_specs=[tile, tile, tile, row],
        out_specs=(tile, tile, row, pl.BlockSpec((1, 128), lambda i: (0, 0))),
        out_shape=(jax.ShapeDtypeStruct((S, D), F32), jax.ShapeDtypeStruct((S, D), BF16),
                   jax.ShapeDtypeStruct((1, D), F32), jax.ShapeDtypeStruct((1, 128), F32)),
        scratch_shapes=[pltpu.VMEM((1, D), F32)],
        compiler_params=_params(("arbitrary",)), name=name)(y, target, branch, gate)


def _hg_chunk(q, fl, lbv, tri):
    C = q.shape[0]
    sq, nsq = _sigmoid_pair(q)
    qa = q * sq
    sig, nsig = _sigmoid_pair(fl)
    one_lb = 1.0 - lbv
    f = lbv + one_lb * sig
    fc = jnp.maximum(f, TINY)
    lf = jnp.log(fc)
    k = one_lb * nsig
    b = _split_dot_left(tri, lf, 3)
    row = lax.broadcasted_iota(jnp.int32, b.shape, 0)
    bm = jnp.sum(jnp.where(row == C // 2 - 1, b, 0.0), axis=0, keepdims=True)
    bl = jnp.sum(jnp.where(row == C - 1, b, 0.0), axis=0, keepdims=True)
    eb = jnp.exp(b)
    ebm = jnp.exp(b - bm)
    enbm = jnp.exp(bm - b)
    ebl = jnp.exp(bl - b)
    ebL = jnp.exp(bl)

    def operand(t):
        return t.astype(BF16).astype(F32)

    return dict(sq=sq, nsq=nsq, qa=qa, sig=sig, nsig=nsig, one_lb=one_lb, f=f, fc=fc, k=k,
                eb=eb, ebm=ebm, enbm=enbm, ebl=ebl, ebL=ebL,
                Qm=operand(qa * ebm), Km=operand(k * enbm), Qb=operand(qa * eb), Kh=operand(k * ebl), row=row)


def _causal_incl(C):
    r = lax.broadcasted_iota(jnp.int32, (C, C), 0)
    c = lax.broadcasted_iota(jnp.int32, (C, C), 1)
    return r >= c


def _hg_fwd(proj, lb, out_g, *, n_heads, cg, name, side=None):
    S = proj.shape[0]
    H = n_heads
    W = H * HEAD
    T = cg * CHUNK
    NG = S // T
    tri = jnp.tril(jnp.ones((CHUNK, CHUNK), F32)).astype(BF16)

    def body(q_ref, f_ref, v_ref, g_ref, lb_ref, og_ref, tri_ref, o_ref, on_ref, st_ref, s_scr):
        @pl.when(pl.program_id(1) == 0)
        def _():
            s_scr[...] = jnp.zeros_like(s_scr)

        lbv = lb_ref[...]
        ogv = og_ref[...]
        triv = tri_ref[...]
        mask = _causal_incl(CHUNK)
        for c in range(cg):
            rows = pl.ds(c * CHUNK, CHUNK)
            v = v_ref[rows, :]
            gg = g_ref[rows, :]
            cm = _hg_chunk(q_ref[rows, :], f_ref[rows, :], lbv, triv)
            s0 = s_scr[...]
            st_ref[c] = s0
            A = jnp.where(mask, _bdot(cm["Qm"], cm["Km"], _NT), 0.0)
            o = _bdot(A, v) + _bdot(cm["Qb"], s0, _NT)
            s_scr[...] = s0 * cm["ebL"] + _bdot(v, cm["Kh"], _TN)
            o_ref[rows, :] = o
            rstd = lax.rsqrt(jnp.mean(o * o, axis=-1, keepdims=True) + EPS)
            sg, _ = _sigmoid_pair(gg)
            on_ref[rows, :] = (((o * rstd) * ogv) * (gg * sg)).astype(BF16)

    def col(group):
        return pl.BlockSpec((T, HEAD), lambda h, g: (g, group * H + h))

    vec = pl.BlockSpec((1, HEAD), lambda h, g: (0, h))
    return _pcall(
        body, grid=(H, NG),
        in_specs=[col(0), col(1), col(2), col(3), vec, vec,
                  pl.BlockSpec((CHUNK, CHUNK), lambda h, g: (0, 0))],
        out_specs=(pl.BlockSpec((T, HEAD), lambda h, g: (g, h)),
                   pl.BlockSpec((T, HEAD), lambda h, g: (g, h)),
                   pl.BlockSpec((cg, None, HEAD, HEAD), lambda h, g: (g, h, 0, 0))),
        out_shape=(jax.ShapeDtypeStruct((S, W), F32), jax.ShapeDtypeStruct((S, W), BF16),
                   jax.ShapeDtypeStruct((S // CHUNK, H, HEAD, HEAD), F32)),
        scratch_shapes=[pltpu.VMEM((HEAD, HEAD), F32)],
        semantics=("parallel", "arbitrary"), name=name,
        args=(proj, proj, proj, proj, lb, out_g, tri), side=side)


def _hg_bwd(proj, o_pre, d_on, d_on_col0, states, lb, out_g, *, n_heads, cg, name, side=None):
    S = proj.shape[0]
    H = n_heads
    W = H * HEAD
    T = cg * CHUNK
    NG = S // T
    tri = jnp.tril(jnp.ones((CHUNK, CHUNK), F32)).astype(BF16)
    triu = jnp.triu(jnp.ones((CHUNK, CHUNK), F32)).astype(BF16)

    def body(q_ref, f_ref, v_ref, g_ref, o_ref, dy_ref, st_ref, lb_ref, og_ref, tri_ref, triu_ref,
             dq_ref, df_ref, di_ref, dg_ref, dlb_ref, dog_ref, ds_scr):
        gstep = pl.program_id(1)

        @pl.when(gstep == 0)
        def _():
            ds_scr[...] = jnp.zeros_like(ds_scr)
            dlb_ref[...] = jnp.zeros_like(dlb_ref)
            dog_ref[...] = jnp.zeros_like(dog_ref)

        lbv = lb_ref[...]
        ogv = og_ref[...]
        triv = tri_ref[...]
        triuv = triu_ref[...]
        mask = _causal_incl(CHUNK)
        dlb_acc = jnp.zeros((1, HEAD), F32)
        dog_acc = jnp.zeros((1, HEAD), F32)
        for c in reversed(range(cg)):
            rows = pl.ds(c * CHUNK, CHUNK)
            q = q_ref[rows, :]
            v = v_ref[rows, :]
            gg = g_ref[rows, :]
            o = o_ref[rows, :]
            dy = dy_ref[rows, :]
            cm = _hg_chunk(q, f_ref[rows, :], lbv, triv)
            s0 = st_ref[c]
            ds1 = ds_scr[...]
            rstd = lax.rsqrt(jnp.mean(o * o, axis=-1, keepdims=True) + EPS)
            n = o * rstd
            sg, nsg = _sigmoid_pair(gg)
            silu_g = gg * sg
            dyn = dy * n
            dog_acc = dog_acc + jnp.sum(dyn * silu_g, axis=0, keepdims=True)
            dg_ref[rows, :] = (dyn * ogv * (sg * (1.0 + gg * nsg))).astype(BF16)
            dn = dy * (ogv * silu_g)
            d_o = rstd * (dn - n * jnp.mean(dn * n, axis=-1, keepdims=True))
            A = jnp.where(mask, _bdot(cm["Qm"], cm["Km"], _NT), 0.0)
            dA = jnp.where(mask, _bdot(d_o, v, _NT), 0.0)
            dV = _bdot(A, d_o, _TN) + _bdot(cm["Kh"], ds1, _NT)
            dQm = _bdot(dA, cm["Km"])
            dKm = _bdot(dA, cm["Qm"], _TN)
            dQb = _bdot(d_o, s0)
            dKh = _bdot(v, ds1)
            ds_scr[...] = ds1 * cm["ebL"] + _bdot(d_o, cm["Qb"], _TN)
            kh_term = dKh * cm["Kh"]
            db = dQm * cm["Qm"] - dKm * cm["Km"] + dQb * cm["Qb"] - kh_term
            dbl = (jnp.sum(kh_term, axis=0, keepdims=True)
                   + cm["ebL"] * jnp.sum(ds1 * s0, axis=0, keepdims=True))
            db = db + jnp.where(cm["row"] == CHUNK - 1, dbl, 0.0)
            dlf = _split_dot_left(triuv, db, 3)
            dqa = dQm * cm["ebm"] + dQb * cm["eb"]
            dq_ref[rows, :] = (dqa * (cm["sq"] * (1.0 + q * cm["nsq"]))).astype(BF16)
            dk = dKm * cm["enbm"] + dKh * cm["ebl"]
            dfc = jnp.where(cm["f"] > TINY, dlf / cm["fc"], 0.0)
            t = dfc - dk
            df_ref[rows, :] = (t * (cm["one_lb"] * cm["sig"] * cm["nsig"])).astype(BF16)
            dlb_acc = dlb_acc + jnp.sum(t * cm["nsig"], axis=0, keepdims=True)
            di_ref[rows, :] = dV.astype(BF16)
        dlb_ref[...] += dlb_acc
        dog_ref[...] += dog_acc

    def col(group):
        return pl.BlockSpec((T, HEAD), lambda h, g: (NG - 1 - g, group * H + h))

    own = pl.BlockSpec((T, HEAD), lambda h, g: (NG - 1 - g, h))
    vec = pl.BlockSpec((1, HEAD), lambda h, g: (0, h))
    cst = pl.BlockSpec((CHUNK, CHUNK), lambda h, g: (0, 0))
    act = jax.ShapeDtypeStruct((S, W), BF16)
    vec_shape = jax.ShapeDtypeStruct((1, W), F32)
    return _pcall(
        body, grid=(H, NG),
        in_specs=[col(0), col(1), col(2), col(3), own,
                  pl.BlockSpec((T, HEAD), lambda h, g: (NG - 1 - g, d_on_col0 + h)),
                  pl.BlockSpec((cg, None, HEAD, HEAD), lambda h, g: (NG - 1 - g, h, 0, 0)),
                  vec, vec, cst, cst],
        out_specs=(own, own, own, own, vec, vec),
        out_shape=(act, act, act, act, vec_shape, vec_shape),
        scratch_shapes=[pltpu.VMEM((HEAD, HEAD), F32)],
        semantics=("parallel", "arbitrary"), name=name,
        args=(proj, proj, proj, proj, o_pre, d_on, states, lb, out_g, tri, triu), side=side)


def _sb_pre(proj, q_g, k_g, *, n_heads, col0, tm, name):
    S = proj.shape[0]
    H = n_heads
    W = H * HEAD

    def body(q_ref, k_ref, v_ref, qg_ref, kg_ref, qh_ref, kh_ref, vh_ref):
        for src, g_ref, dst in ((q_ref, qg_ref, qh_ref), (k_ref, kg_ref, kh_ref)):
            xv = src[...]
            rstd = lax.rsqrt(jnp.mean(xv * xv, axis=-1, keepdims=True) + EPS)
            dst[...] = ((xv * rstd) * g_ref[...]).astype(BF16)
        vh_ref[...] = v_ref[...].astype(BF16)

    def col(group):
        return pl.BlockSpec((tm, HEAD), lambda i, h: (i, col0 + group * H + h))

    vec = pl.BlockSpec((1, HEAD), lambda i, h: (0, 0))
    own = pl.BlockSpec((tm, HEAD), lambda i, h: (i, h))
    act = jax.ShapeDtypeStruct((S, W), BF16)
    return pl.pallas_call(
        body, grid=(S // tm, H), in_specs=[col(0), col(1), col(2), vec, vec],
        out_specs=(own, own, own), out_shape=(act, act, act),
        compiler_params=_params(("parallel", "parallel")), name=name)(proj, proj, proj, q_g, k_g)


def _sb_scores(q, k_blk, scale):
    z = lax.dot_general(q, k_blk, _NT, preferred_element_type=F32) * scale
    e = jnp.exp(-jnp.abs(z))
    sp = jnp.maximum(z, 0.0) + jnp.log(1.0 + e)
    return z, e, sp


def _heads_per_step(n_heads):
    return 2 if n_heads % 2 == 0 else 1


def _max_all(values):
    m = jnp.max(values[0])
    for v in values[1:]:
        m = jnp.maximum(m, jnp.max(v))
    return m


def _strict_lower_mask(t):
    r = lax.broadcasted_iota(jnp.int32, (t, t), 0)
    c = lax.broadcasted_iota(jnp.int32, (t, t), 1)
    return c < r


def _sb_fwd(qh, kh, vh, out_g, *, n_heads, tq, name, side=None):
    S, W = qh.shape
    HP = _heads_per_step(n_heads)
    WP = HP * HEAD
    NQ = S // tq
    scale = HEAD ** -0.5
    u_strict = jnp.tril(jnp.ones((tq, tq), F32), -1).astype(BF16)

    def body(q_ref, k_ref, v_ref, og_ref, u_ref, o_ref, on_ref):
        qi = pl.program_id(1)
        u = u_ref[...]
        heads = [slice(hh * HEAD, (hh + 1) * HEAD) for hh in range(HP)]
        qs = [q_ref[:, cols] for cols in heads]

        def block(kb, r_carry, diag, valid=None):
            rows = pl.ds(pl.multiple_of(kb * tq, tq), tq)
            pvs, rs = [], []
            for hh, cols in enumerate(heads):
                k_blk = k_ref[rows, cols]
                v_blk = v_ref[rows, cols]
                z, _, sp = _sb_scores(qs[hh], k_blk, scale)
                if diag:
                    m = _strict_lower_mask(tq)
                    L = jnp.where(m, -sp, 0.0)
                else:
                    L = -sp
                C = _split_dot(L, u, 2)
                a = jnp.exp(z - sp + C + r_carry[hh])
                if diag:
                    a = jnp.where(m, a, 0.0)
                if valid is not None:
                    a = jnp.where(valid, a, 0.0)
                pvs.append(lax.dot_general(a.astype(BF16), v_blk, (((1,), (0,)), ((), ())),
                                           preferred_element_type=F32))
                rs.append(r_carry[hh] + (C[:, 0:1] + L[:, 0:1]))
            return tuple(pvs), tuple(rs)

        acc_d, r_d = block(qi, (jnp.zeros((tq, 1), F32),) * HP, True)
        acc_p, r0 = block(jnp.maximum(qi - 1, 0), r_d, False, valid=qi > 0)
        acc0 = tuple(a + b for a, b in zip(acc_d, acc_p))

        def cond(st):
            kb, _, _, rmax = st
            return jnp.logical_and(kb >= 0, rmax > SKIP_LOG)

        def step(st):
            kb, acc, r, _ = st
            pv, r2 = block(kb, r, False)
            return kb - 1, tuple(a + b for a, b in zip(acc, pv)), r2, _max_all(r2)

        _, accs, _, _ = lax.while_loop(cond, step, (qi - 2, acc0, r0, _max_all(r0)))
        for hh, cols in enumerate(heads):
            acc = accs[hh]
            o_ref[:, cols] = acc
            rstd = lax.rsqrt(jnp.mean(acc * acc, axis=-1, keepdims=True) + EPS)
            on_ref[:, cols] = ((acc * rstd) * og_ref[:, cols]).astype(BF16)

    blk = pl.BlockSpec((tq, WP), lambda h, i: (i, h))
    full = pl.BlockSpec((S, WP), lambda h, i: (0, h))
    return _pcall(
        body, grid=(n_heads // HP, NQ),
        in_specs=[blk, full, full, pl.BlockSpec((1, WP), lambda h, i: (0, h)),
                  pl.BlockSpec((tq, tq), lambda h, i: (0, 0))],
        out_specs=(blk, blk),
        out_shape=(jax.ShapeDtypeStruct((S, W), F32), jax.ShapeDtypeStruct((S, W), BF16)),
        semantics=("parallel", "arbitrary"), name=name, args=(qh, kh, vh, out_g, u_strict), side=side)


def _sb_bwd(qh, kh, vh, o_pre, d_on, d_on_col0, out_g, *, n_heads, tq, name, side=None):
    S, W = qh.shape
    HP = _heads_per_step(n_heads)
    WP = HP * HEAD
    assert d_on_col0 % HP == 0
    NQ = S // tq
    scale = HEAD ** -0.5
    u_strict = jnp.tril(jnp.ones((tq, tq), F32), -1).astype(BF16)
    u_incl = jnp.tril(jnp.ones((tq, tq), F32)).astype(BF16)

    def body(q_ref, k_ref, v_ref, o_ref, dy_ref, og_ref, us_ref, ui_ref,
             dq_ref, dk_ref, dv_ref, dog_ref):
        qi = pl.program_id(1)

        @pl.when(qi == 0)
        def _():
            dk_ref[...] = jnp.zeros_like(dk_ref)
            dv_ref[...] = jnp.zeros_like(dv_ref)
            dog_ref[...] = jnp.zeros_like(dog_ref)

        us = us_ref[...]
        ui = ui_ref[...]
        heads = [slice(hh * HEAD, (hh + 1) * HEAD) for hh in range(HP)]
        qs, d_obs, deltas = [], [], []
        for cols in heads:
            qs.append(q_ref[:, cols])
            o = o_ref[:, cols]
            dy = dy_ref[:, cols]
            rstd = lax.rsqrt(jnp.mean(o * o, axis=-1, keepdims=True) + EPS)
            n = o * rstd
            dog_ref[:, cols] += jnp.sum(dy * n, axis=0, keepdims=True)
            dn = dy * og_ref[:, cols]
            d_o = rstd * (dn - n * jnp.mean(dn * n, axis=-1, keepdims=True))
            d_ob = d_o.astype(BF16)
            d_obs.append(d_ob)
            deltas.append(jnp.sum(d_ob.astype(F32) * o, axis=-1, keepdims=True))
        q_ts = [q.T for q in qs]
        d_ob_ts = [d.T for d in d_obs]

        def block(kb, r_carry, g_carry, diag, valid=None):
            rows = pl.ds(pl.multiple_of(kb * tq, tq), tq)
            dqs, rs, gs = [], [], []
            for hh, cols in enumerate(heads):
                k_blk = k_ref[rows, cols]
                v_blk = v_ref[rows, cols]
                z, e, sp = _sb_scores(qs[hh], k_blk, scale)
                if diag:
                    m = _strict_lower_mask(tq)
                    L = jnp.where(m, -sp, 0.0)
                else:
                    L = -sp
                C = _split_dot(L, us, 2)
                a = jnp.exp(z - sp + C + r_carry[hh])
                if diag:
                    a = jnp.where(m, a, 0.0)
                if valid is not None:
                    a = jnp.where(valid, a, 0.0)
                ab = a.astype(BF16)
                dA = lax.dot_general(d_obs[hh], v_blk, _NT, preferred_element_type=F32)
                G = ab.astype(F32) * dA
                SI = _split_dot(G, ui, 2)
                P = deltas[hh] - (g_carry[hh] + SI)
                r = 1.0 / (1.0 + e)
                sig = jnp.where(z >= 0, r, e * r)
                dz = G - (G + P) * sig
                if diag:
                    dz = jnp.where(m, dz, 0.0)
                if valid is not None:
                    dz = jnp.where(valid, dz, 0.0)
                dzb = (dz * scale).astype(BF16)
                dqs.append(lax.dot_general(dzb, k_blk, (((1,), (0,)), ((), ())), preferred_element_type=F32))
                dk_ref[cols, rows] += jnp.dot(q_ts[hh], dzb, preferred_element_type=F32)
                dv_ref[cols, rows] += jnp.dot(d_ob_ts[hh], ab, preferred_element_type=F32)
                rs.append(r_carry[hh] + (C[:, 0:1] + L[:, 0:1]))
                gs.append(g_carry[hh] + SI[:, 0:1])
            return tuple(dqs), tuple(rs), tuple(gs)

        zero = (jnp.zeros((tq, 1), F32),) * HP
        dq_d, r_d, g_d = block(qi, zero, zero, True)
        dq_p, r0, g0 = block(jnp.maximum(qi - 1, 0), r_d, g_d, False, valid=qi > 0)
        dq0 = tuple(a + b for a, b in zip(dq_d, dq_p))

        def cond(st):
            kb, _, _, _, rmax = st
            return jnp.logical_and(kb >= 0, rmax > SKIP_LOG)

        def step(st):
            kb, dq, r, g, _ = st
            dq_part, r2, g2 = block(kb, r, g, False)
            return kb - 1, tuple(a + b for a, b in zip(dq, dq_part)), r2, g2, _max_all(r2)

        _, dqs, _, _, _ = lax.while_loop(cond, step, (qi - 2, dq0, r0, g0, _max_all(r0)))
        for hh, cols in enumerate(heads):
            dq_ref[:, cols] = dqs[hh]

    blk = pl.BlockSpec((tq, WP), lambda h, i: (i, h))
    full = pl.BlockSpec((S, WP), lambda h, i: (0, h))
    vec = pl.BlockSpec((1, WP), lambda h, i: (0, h))
    cst = pl.BlockSpec((tq, tq), lambda h, i: (0, 0))
    act = jax.ShapeDtypeStruct((S, W), F32)
    act_t = jax.ShapeDtypeStruct((W, S), F32)
    full_t = pl.BlockSpec((WP, S), lambda h, i: (h, 0))
    return _pcall(
        body, grid=(n_heads // HP, NQ),
        in_specs=[blk, full, full, blk,
                  pl.BlockSpec((tq, WP), lambda h, i: (i, d_on_col0 // HP + h)), vec, cst, cst],
        out_specs=(blk, full_t, full_t, vec),
        out_shape=(act, act_t, act_t, jax.ShapeDtypeStruct((1, W), F32)),
        semantics=("parallel", "arbitrary"), name=name,
        args=(qh, kh, vh, o_pre, d_on, out_g, u_strict, u_incl), side=side)


def _sb_pre_bwd(proj, dqh, dkh, dvh, q_g, k_g, *, n_heads, col0, tm, name):
    S = proj.shape[0]
    H = n_heads
    W = H * HEAD

    def body(q_ref, k_ref, dqh_ref, dkh_ref, dvh_ref, qg_ref, kg_ref,
             dq_ref, dk_ref, dv_ref, dqg_ref, dkg_ref):
        first = jnp.logical_and(pl.program_id(0) == 0, pl.program_id(1) == 0)

        @pl.when(first)
        def _():
            dqg_ref[...] = jnp.zeros_like(dqg_ref)
            dkg_ref[...] = jnp.zeros_like(dkg_ref)

        for src, dh, g_ref, dst, dg_ref in ((q_ref, dqh_ref[...], qg_ref, dq_ref, dqg_ref),
                                            (k_ref, dkh_ref[...].T, kg_ref, dk_ref, dkg_ref)):
            xv = src[...]
            rstd = lax.rsqrt(jnp.mean(xv * xv, axis=-1, keepdims=True) + EPS)
            n = xv * rstd
            dg_ref[...] += jnp.sum(dh * n, axis=0, keepdims=True)
            dn = dh * g_ref[...]
            dst[...] = (rstd * (dn - n * jnp.mean(dn * n, axis=-1, keepdims=True))).astype(BF16)
        dv_ref[...] = dvh_ref[...].T.astype(BF16)

    def col(group):
        return pl.BlockSpec((tm, HEAD), lambda i, h: (i, col0 + group * H + h))

    vec = pl.BlockSpec((1, HEAD), lambda i, h: (0, 0))
    own = pl.BlockSpec((tm, HEAD), lambda i, h: (i, h))
    own_t = pl.BlockSpec((HEAD, tm), lambda i, h: (h, i))
    act = jax.ShapeDtypeStruct((S, W), BF16)
    vec_shape = jax.ShapeDtypeStruct((1, HEAD), F32)
    return pl.pallas_call(
        body, grid=(S // tm, H), in_specs=[col(0), col(1), own, own_t, own_t, vec, vec],
        out_specs=(own, own, own, vec, vec), out_shape=(act, act, act, vec_shape, vec_shape),
        compiler_params=_params(("arbitrary", "arbitrary")), name=name,
    )(proj, proj, dqh, dkh, dvh, q_g, k_g)


def _softmax_rows(x_ref, L):
    rows = [x_ref[l:l + 1, :] for l in range(L)]
    mx = rows[0]
    for r in rows[1:]:
        mx = jnp.maximum(mx, r)
    ex = [jnp.exp(r - mx) for r in rows]
    tot = ex[0]
    for e in ex[1:]:
        tot = tot + e
    return [e / tot for e in ex]


def _lb_fwd(logits, *, name):
    L, W = logits.shape

    def body(x_ref, o_ref):
        s = _softmax_rows(x_ref, L)
        run = jnp.zeros((1, W), F32)
        for l in range(L):
            run = run + s[l]
            o_ref[l:l + 1, :] = run - s[0]

    return pl.pallas_call(body, out_shape=jax.ShapeDtypeStruct((L, W), F32), name=name)(logits)


def _lb_bwd(logits, dlb_parts, *, name):
    L, W = logits.shape
    P = dlb_parts.shape[0]

    def body(x_ref, d_ref, o_ref):
        s = _softmax_rows(x_ref, L)
        dlb = []
        for l in range(L):
            t = d_ref[0, l:l + 1, :]
            for q in range(1, P):
                t = t + d_ref[q, l:l + 1, :]
            dlb.append(t)
        ds = [None] * L
        run = jnp.zeros((1, W), F32)
        for j in reversed(range(L)):
            run = run + dlb[j]
            ds[j] = run
        ds[0] = jnp.zeros((1, W), F32)
        inner = jnp.zeros((1, W), F32)
        for j in range(L):
            inner = inner + s[j] * ds[j]
        for j in range(L):
            o_ref[j:j + 1, :] = s[j] * (ds[j] - inner)

    return pl.pallas_call(body, out_shape=jax.ShapeDtypeStruct((L, W), F32), name=name)(logits, dlb_parts)


def _ada_mod(c_all, w_ada, *, nb, name):
    L, D, n = w_ada.shape
    B = c_all.shape[0]

    def body(c_ref, w_ref, o_ref, cond_ref):
        cv = c_ref[...]
        s, _ = _sigmoid_pair(cv)
        cond = cv * s
        cond_ref[...] = cond
        o_ref[...] = _bdot(cond, w_ref[...])

    return pl.pallas_call(
        body, grid=(L, n // nb),
        in_specs=[pl.BlockSpec((B, D), lambda l, j: (0, 0)),
                  pl.BlockSpec((None, D, nb), lambda l, j: (l, 0, j))],
        out_specs=(pl.BlockSpec((None, B, nb), lambda l, j: (l, 0, j)),
                   pl.BlockSpec((B, D), lambda l, j: (0, 0))),
        out_shape=(jax.ShapeDtypeStruct((L, B, n), F32), jax.ShapeDtypeStruct((B, D), F32)),
        compiler_params=_params(("arbitrary", "arbitrary")), name=name)(c_all, w_ada)


def _adam_math(w, g, m, v):
    m2 = ADAM_B1 * m + (1.0 - ADAM_B1) * g
    v2 = ADAM_B2 * v + (1.0 - ADAM_B2) * (g * g)
    m_hat = m2 / (1.0 - ADAM_B1 ** ADAM_STEP)
    v_hat = v2 / (1.0 - ADAM_B2 ** ADAM_STEP)
    delta = -ADAM_LR * (m_hat / (jnp.sqrt(v_hat) + ADAM_EPS) + ADAM_WD * w)
    return delta, m2, v2


def _adamw(w, m, v, gparts, *, tr, name):
    R, C = w.shape
    P = gparts.shape[0]

    def body(w_ref, m_ref, v_ref, gp_ref, g_ref, d_ref, m2_ref, v2_ref):
        g = gp_ref[0].astype(F32)
        for p in range(1, P):
            g = g + gp_ref[p].astype(F32)
        delta, m2, v2 = _adam_math(w_ref[...], g, m_ref[...], v_ref[...])
        g_ref[...] = g
        d_ref[...] = delta
        m2_ref[...] = m2
        v2_ref[...] = v2

    tile = pl.BlockSpec((tr, C), lambda i: (i, 0))
    shp = jax.ShapeDtypeStruct((R, C), F32)
    return pl.pallas_call(
        body, grid=(R // tr,),
        in_specs=[tile, tile, tile, pl.BlockSpec((P, tr, C), lambda i: (0, i, 0))],
        out_specs=(tile, tile, tile, tile), out_shape=(shp, shp, shp, shp),
        compiler_params=_params(("parallel",)), name=name)(w, m, v, gparts)


def _adamw_layers(w, m, v, gparts, *, tr, name, side=None):
    L, R, C = w.shape
    P = gparts[0].shape[0]
    nblk = R // tr

    def body(*refs):
        w_ref, m_ref, v_ref = refs[:3]
        gp_refs = refs[3:3 + L]
        g_ref, d_ref, m2_ref, v2_ref = refs[3 + L:]
        layer = pl.program_id(0)
        for t in range(L):
            @pl.when(layer == t)
            def _(t=t):
                g = gp_refs[t][0].astype(F32)
                for q in range(1, P):
                    g = g + gp_refs[t][q].astype(F32)
                delta, m2, v2 = _adam_math(w_ref[...], g, m_ref[...], v_ref[...])
                g_ref[...] = g
                d_ref[...] = delta
                m2_ref[...] = m2
                v2_ref[...] = v2

    def gp_spec(t):
        def index(l, i):
            return (0, jnp.where(l == t, i, jnp.where(l < t, 0, nblk - 1)), 0)
        return pl.BlockSpec((P, tr, C), index)

    tile = pl.BlockSpec((None, tr, C), lambda l, i: (l, i, 0))
    shp = jax.ShapeDtypeStruct((L, R, C), F32)
    return _pcall(
        body, grid=(L, nblk), in_specs=[tile, tile, tile] + [gp_spec(t) for t in range(L)],
        out_specs=(tile, tile, tile, tile), out_shape=(shp, shp, shp, shp),
        semantics=("arbitrary", "arbitrary"), name=name, args=(w, m, v, *gparts), side=side)


def _adamw_ada(w, m, v, cond_t, dmod, *, tr, name):
    L, D, n = w.shape
    Bp = cond_t.shape[1]

    def body(w_ref, m_ref, v_ref, c_ref, dm_ref, g_ref, d_ref, m2_ref, v2_ref):
        g = _bdot(c_ref[...], dm_ref[...])
        delta, m2, v2 = _adam_math(w_ref[...], g, m_ref[...], v_ref[...])
        g_ref[...] = g
        d_ref[...] = delta
        m2_ref[...] = m2
        v2_ref[...] = v2

    tile = pl.BlockSpec((None, tr, n), lambda l, i: (l, i, 0))
    shp = jax.ShapeDtypeStruct((L, D, n), F32)
    return pl.pallas_call(
        body, grid=(L, D // tr),
        in_specs=[tile, tile, tile, pl.BlockSpec((tr, Bp), lambda l, i: (i, 0)),
                  pl.BlockSpec((None, Bp, n), lambda l, i: (l, 0, 0))],
        out_specs=(tile, tile, tile, tile), out_shape=(shp, shp, shp, shp),
        compiler_params=_params(("parallel", "parallel")), name=name)(w, m, v, cond_t, dmod)


def _allgather_small(block, *, name):
    R, C = block.shape

    def body(x_ref, out_ref, send_sems, recv_sems, local_sem):
        x, y, c = lax.axis_index("x"), lax.axis_index("y"), lax.axis_index("c")

        def rows(px, py, pc):
            return out_ref.at[pl.ds((4 * px + 2 * py + pc) * R, R), :]

        mine = pltpu.make_async_copy(x_ref, rows(x, y, c), local_sem)
        mine.start()
        sends = []
        for rel in range(1, N_DEV):
            to = _peer(x, y, c, rel)
            cp = pltpu.make_async_remote_copy(src_ref=x_ref, dst_ref=rows(x, y, c),
                                              send_sem=send_sems.at[rel - 1], recv_sem=recv_sems.at[rel - 1],
                                              device_id=to, device_id_type=MESH)
            cp.start()
            sends.append(cp)
        for rel in range(1, N_DEV):
            frm = _peer(x, y, c, rel)
            pltpu.make_async_remote_copy(src_ref=x_ref, dst_ref=rows(*frm),
                                         send_sem=send_sems.at[rel - 1], recv_sem=recv_sems.at[rel - 1],
                                         device_id=frm, device_id_type=MESH).wait_recv()
        for cp in sends:
            cp.wait_send()
        mine.wait()

    return pl.pallas_call(
        body, out_shape=jax.ShapeDtypeStruct((N_DEV * R, C), block.dtype),
        in_specs=[pl.BlockSpec(memory_space=pltpu.VMEM)],
        out_specs=pl.BlockSpec(memory_space=pltpu.VMEM),
        scratch_shapes=[pltpu.SemaphoreType.DMA((N_DEV - 1,)), pltpu.SemaphoreType.DMA((N_DEV - 1,)),
                        pltpu.SemaphoreType.DMA],
        compiler_params=pltpu.CompilerParams(vmem_limit_bytes=V7X_VMEM_LIMIT), name=name)(block)


def _allgather_hbm(shards, *, name):
    n = len(shards)

    def body(*refs):
        ins = refs[:n]
        outs = refs[n:2 * n]
        send_sems, recv_sems, local_sems = refs[2 * n:]
        x, y, c = lax.axis_index("x"), lax.axis_index("y"), lax.axis_index("c")
        sibling = (x, y, 1 - c)
        chips = [(1 - x, y), (x, 1 - y), (1 - x, 1 - y)]

        def slot(t, px, py, pc):
            return outs[t].at[4 * px + 2 * py + pc]

        def copy(t, k, block, to, src=None):
            return pltpu.make_async_remote_copy(
                src_ref=slot(t, *block) if src is None else src, dst_ref=slot(t, *block),
                send_sem=send_sems.at[t * 7 + k], recv_sem=recv_sems.at[t * 7 + k],
                device_id=to, device_id_type=MESH)

        me = (x, y, c)
        started = []
        mine = []
        for t in range(n):
            cp = pltpu.make_async_copy(ins[t], slot(t, *me), local_sems.at[t])
            cp.start()
            mine.append(cp)
            first = [copy(t, 0, me, sibling, src=ins[t])]
            first += [copy(t, 1 + j, me, (*chip, c), src=ins[t]) for j, chip in enumerate(chips)]
            for cp in first:
                cp.start()
            started += first
        for t in range(n):
            for j, chip in enumerate(chips):
                copy(t, 1 + j, (*chip, c), me).wait_recv()
                fwd = copy(t, 4 + j, (*chip, c), sibling)
                fwd.start()
                started.append(fwd)
        for t in range(n):
            copy(t, 0, sibling, me).wait_recv()
            for j, chip in enumerate(chips):
                copy(t, 4 + j, (*chip, 1 - c), me).wait_recv()
        for cp in started:
            cp.wait_send()
        for cp in mine:
            cp.wait()

    any_spec = pl.BlockSpec(memory_space=pl.ANY)
    return pl.pallas_call(
        body, out_shape=[jax.ShapeDtypeStruct((N_DEV,) + s.shape, s.dtype) for s in shards],
        in_specs=[any_spec] * n, out_specs=[any_spec] * n,
        scratch_shapes=[pltpu.SemaphoreType.DMA((7 * n,)), pltpu.SemaphoreType.DMA((7 * n,)),
                        pltpu.SemaphoreType.DMA((n,))],
        name=name)(*shards)


def _tile(total, want):
    step = 128 if total % 128 == 0 else 8
    best = step
    t = step
    while t <= min(total, want):
        if total % t == 0:
            best = t
        t += step
    return best


def _local_step(x, target, mods, lbs, p, wg, shards=None):
    S, D = x.shape
    L = mods.shape[0]
    W = D // 2
    H = W // HEAD
    mesh = shards is not None
    F = shards["w_ffn_out"][0].shape[0] * N_DEV if mesh else wg["w_ffn_out"][0].shape[0]
    tm = _tile(S, 512)
    tm_big = _tile(S, 1024)
    tm_tn = _tile(S, 2048)
    tm_sw = _tile(S, 128)
    tq = _tile(S, 256)
    cg = max(1, min(32, S // CHUNK))
    nb_out = _tile(D, 1024)
    kb_f = _tile(F, 1408)

    def row(a, l):
        return a[l][None, :]

    first_plan = dict(proj=[("w_out", 0), ("w_ffn_in", 0)], hg=[("w_ffn_out", 0), ("w_in", 1)],
                      sb=[("w_out", 1), ("w_ffn_out", 1)], ffn=[("w_ffn_in", 1)])

    def gather_plan(l, call):
        if not mesh:
            return []
        if l == 0:
            plan = first_plan[call]
        else:
            plan = [(dict(proj="w_in", hg="w_out", sb="w_ffn_out", ffn="w_ffn_in")[call], l + 1)]
        return [(k, j) for k, j in plan if j < L]

    def gather_of(plan):
        return (True, [shards[k][j] for k, j in plan]) if plan else None

    def store_gathered(plan, got):
        for (k, j), g in zip(plan, got or []):
            wg[k][j] = g.reshape(-1, D) if k in ("w_out", "w_ffn_out") else g

    def scatter_of(blocks):
        if mesh and blocks is not None:
            return (False, [b.reshape((N_DEV, -1) + b.shape[-1:]) if b.ndim == 2 else b for b in blocks])
        return None

    saved = []
    xcur = x
    for l in range(L):
        mod = mods[l]
        sh1, sc1, g1, sh2, sc2, g2 = [mod[:, i * D:(i + 1) * D] for i in range(N_MOD)]
        h1 = _modnorm_fwd(xcur, row(p["norm1_g"], l), sc1, sh1, tm=tm, name="norm1_fwd")
        plan = gather_plan(l, "proj")
        proj, got = _mm_nn(h1, wg["w_in"][l], tm=tm_big, name="proj_fwd", side=gather_of(plan))
        store_gathered(plan, got)
        lb = lbs[l][None, :]
        plan = gather_plan(l, "hg")
        (o_hg, on_hg, states), got = _hg_fwd(proj, lb, row(p["hg_out_g"], l), n_heads=H, cg=cg, name="hgrn2_fwd",
                                             side=gather_of(plan))
        store_gathered(plan, got)
        qh, kh, vh = _sb_pre(proj, row(p["sb_q_g"], l), row(p["sb_k_g"], l), n_heads=H, col0=4 * H,
                             tm=tm_tn, name="sb_qknorm_fwd")
        plan = gather_plan(l, "sb")
        (o_sb, on_sb), got = _sb_fwd(qh, kh, vh, row(p["sb_out_g"], l), n_heads=H, tq=tq, name="sb_fwd",
                                     side=gather_of(plan))
        store_gathered(plan, got)
        o_cat = jnp.concatenate([on_hg, on_sb], axis=1)
        (x1, mixed), _ = _mm_nn(o_cat, wg["w_out"][l], tm=tm_big, nb=nb_out, resid=xcur, gate=g1,
                                name="out_proj_fwd")
        h2 = _modnorm_fwd(x1, row(p["norm2_g"], l), sc2, sh2, tm=tm, name="norm2_fwd")
        w_fin = wg["w_ffn_in"][l]
        if w_fin.shape[0] % 2:
            w_fin = jnp.stack([w_fin[0][:, :F], w_fin[0][:, F:]])
        plan = gather_plan(l, "ffn")
        (gate, up, a), got = _ffn_in_fwd(h2, w_fin, tm=tm, nb=F, name="ffn_in_fwd", side=gather_of(plan))
        store_gathered(plan, got)
        (x2, ffn), _ = _mm_nn(a, wg["w_ffn_out"][l], tm=tm, nb=nb_out // 2, resid=x1, gate=g2, name="ffn_out_fwd")
        saved.append(dict(x=xcur, h1=h1, proj=proj, o_hg=o_hg, o_sb=o_sb, states=states, qh=qh, kh=kh, vh=vh,
                          o_cat=o_cat, mixed=mixed, x1=x1, h2=h2, gate=gate, up=up, a=a, ffn=ffn, lb=lb,
                          sc1=sc1, g1=g1, sc2=sc2, g2=g2, w_fin=w_fin))
        xcur = x2

    last = saved[-1]
    dx, dffn, dg2, loss = _loss_bwd(xcur, target, last["ffn"], last["g2"], tm=tm, name="loss_bwd")

    big = {k: [None] * L for k in ("w_in", "w_out", "w_ffn_in", "w_ffn_out")}
    small = {k: [None] * L for k in ("norm1_g", "hg_lb", "hg_out_g", "sb_q_g", "sb_k_g", "sb_out_g", "norm2_g")}
    dmods = [None] * L
    pending_in = None
    for l in reversed(range(L)):
        sv = saved[l]
        (dgate, dup), got = _ffn_out_bwd_x(dffn, wg["w_ffn_out"][l], sv["gate"], sv["up"], tm=tm, kb=kb_f,
                                           name="ffn_out_bwd_x", side=scatter_of(pending_in))
        if got is not None:
            big["w_in"][l + 1] = got[0]
        g_fout, _ = _mm_tn(sv["a"], dffn, tm=tm_tn, kb=kb_f, nb=nb_out, blocked=False, name="ffn_out_bwd_w")
        dh2, got = _mm_nt(dgate, sv["w_fin"], dy2=dup, tm=tm_big, kb=D, name="ffn_in_bwd_x",
                          side=scatter_of([g_fout]))
        big["w_ffn_out"][l] = g_fout if got is None else got[0]
        g_fin, _ = _mm_tn(sv["h2"], dgate, dy2=dup, tm=tm_tn, kb=_tile(D, 1024), nb=sv["w_fin"].shape[2],
                          blocked=True, name="ffn_in_bwd_w")
        dx1, dmixed, dg1, dsh2, dsc2, dn2 = _modnorm_bwd(
            sv["x1"], dh2, dx, row(p["norm2_g"], l), sv["sc2"], sv["mixed"], sv["g1"], tm=tm_sw * 2,
            name="norm2_bwd")
        small["norm2_g"][l] = dn2
        d_ocat, _ = _mm_nt(dmixed, wg["w_out"][l], tm=tm_big, kb=nb_out, nb=D, name="out_proj_bwd_x")
        g_out, _ = _mm_tn(sv["o_cat"], dmixed, tm=tm_tn, kb=D, nb=nb_out, blocked=False, name="out_proj_bwd_w")
        (dhq, dhf, dhi, dhg, dlb, dhog), _ = _hg_bwd(sv["proj"], sv["o_hg"], d_ocat, 0, sv["states"], sv["lb"],
                                                     row(p["hg_out_g"], l), n_heads=H, cg=cg, name="hgrn2_bwd")
        (dqh, dkh, dvh, dsog), got = _sb_bwd(sv["qh"], sv["kh"], sv["vh"], sv["o_sb"], d_ocat, H,
                                             row(p["sb_out_g"], l), n_heads=H, tq=tq, name="sb_bwd",
                                             side=scatter_of([g_fin]))
        big["w_ffn_in"][l] = g_fin if got is None else got[0]
        dsq, dsk, dsv, dqg, dkg = _sb_pre_bwd(sv["proj"], dqh, dkh, dvh, row(p["sb_q_g"], l),
                                              row(p["sb_k_g"], l), n_heads=H, col0=4 * H, tm=tm_tn,
                                              name="sb_qknorm_bwd")
        small["hg_lb"][l] = dlb
        small["hg_out_g"][l] = dhog
        small["sb_out_g"][l] = dsog
        small["sb_q_g"][l] = dqg
        small["sb_k_g"][l] = dkg
        dproj = jnp.concatenate([dhq, dhf, dhi, dhg, dsq, dsk, dsv], axis=1)
        g_in, _ = _mm_tn(sv["h1"], dproj, tm=tm_tn, kb=D, nb=wg["w_in"][l].shape[2], blocked=True,
                         name="proj_bwd_w")
        big["w_in"][l] = g_in
        dh1, got = _mm_nt(dproj, wg["w_in"][l], tm=tm_big, kb=D, name="proj_bwd_x",
                          side=scatter_of([g_out, g_in] if l == 0 else [g_out]))
        if got is not None:
            big["w_out"][l] = got[0]
            if l == 0:
                big["w_in"][l] = got[1]
        else:
            big["w_out"][l] = g_out
        pending_in = [g_in]
        if l > 0:
            prev = saved[l - 1]
            dx0, dffn_prev, dg2_prev, dsh1, dsc1, dn1 = _modnorm_bwd(
                sv["x"], dh1, dx1, row(p["norm1_g"], l), sv["sc1"], prev["ffn"], prev["g2"], tm=tm_sw * 2,
                name="norm1_bwd")
        else:
            dx0, dsh1, dsc1, dn1 = _modnorm_bwd(sv["x"], dh1, dx1, row(p["norm1_g"], l), sv["sc1"], None, None,
                                                tm=tm_sw * 2, name="norm1_bwd_first")
            dffn_prev, dg2_prev = None, None
        small["norm1_g"][l] = dn1
        dmods[l] = jnp.concatenate([dsh1, dsc1, dg1, dsh2, dsc2, dg2], axis=1)
        dx, dffn, dg2 = dx0, dffn_prev, dg2_prev
    return loss, dx, big, small, dmods


def kernel(x, c, norm1_g, w_in, hg_lb_logits, hg_out_g, sb_q_g, sb_k_g, sb_out_g, w_out, norm2_g, w_ffn_in, w_ffn_out, w_ada, b_ada, loss_target, m_norm1_g, m_w_in, m_hg_lb_logits, m_hg_out_g, m_sb_q_g, m_sb_k_g, m_sb_out_g, m_w_out, m_norm2_g, m_w_ffn_in, m_w_ffn_out, m_w_ada, m_b_ada, v_norm1_g, v_w_in, v_hg_lb_logits, v_hg_out_g, v_sb_q_g, v_sb_k_g, v_sb_out_g, v_w_out, v_norm2_g, v_w_ffn_in, v_w_ffn_out, v_w_ada, v_b_ada):
    L, D = norm1_g.shape
    S = x.shape[1]
    me = 4 * lax.axis_index("x") + 2 * lax.axis_index("y") + lax.axis_index("c")

    c_all = _allgather_small(jnp.broadcast_to(c, (8, D)), name="gather_c").reshape(N_DEV, 8, D)[:, 0, :]
    n_ada = w_ada.shape[2]
    mod_cols, cond = _ada_mod(c_all, w_ada, nb=_tile(n_ada, 512), name="ada_mod")
    mod_all = _allgather_small(mod_cols.reshape(L * N_DEV, n_ada), name="gather_mod")
    mod_all = mod_all.reshape(N_DEV, L, N_DEV, n_ada)
    mod_mine = lax.dynamic_index_in_dim(mod_all, me, axis=2, keepdims=False)
    mods = jnp.transpose(mod_mine, (1, 0, 2)).reshape(L, 1, N_DEV * n_ada) + b_ada[:, None, :]

    lbs = _lb_fwd(hg_lb_logits, name="lower_bounds_fwd")

    shards = dict(w_in=[w_in[l].astype(BF16) for l in range(L)], w_out=[w_out[l].astype(BF16) for l in range(L)],
                  w_ffn_in=[w_ffn_in[l].astype(BF16) for l in range(L)],
                  w_ffn_out=[w_ffn_out[l].astype(BF16) for l in range(L)])
    g_in, = _allgather_hbm([shards["w_in"][0]], name="gather_first_weight")
    wg = dict(w_in=[g_in] + [None] * (L - 1), w_out=[None] * L, w_ffn_in=[None] * L, w_ffn_out=[None] * L)

    p = dict(norm1_g=norm1_g, hg_out_g=hg_out_g, sb_q_g=sb_q_g, sb_k_g=sb_k_g, sb_out_g=sb_out_g,
             norm2_g=norm2_g)
    loss_part, grad_x, recv, small, dmods = _local_step(x.reshape(S, D), loss_target.reshape(S, D), mods, lbs, p,
                                                        wg, shards)

    dmod = jnp.concatenate(dmods, axis=0)
    pieces = [jnp.concatenate(small[k], axis=0) for k in
              ("norm1_g", "hg_lb", "hg_out_g", "sb_q_g", "sb_k_g", "sb_out_g", "norm2_g")] + [dmod]
    flat = jnp.concatenate([a.reshape(-1) for a in pieces] + [loss_part.reshape(-1)])
    n_flat = flat.shape[0]
    rows = -(-n_flat // 1024) * 8
    flat = jnp.pad(flat, (0, rows * 128 - n_flat)).reshape(rows, 128)
    gathered = _allgather_small(flat, name="gather_small_grads").reshape(N_DEV, rows * 128)

    def take(off, shape):
        size = 1
        for s in shape:
            size *= s
        return gathered[:, off:off + size].reshape((N_DEV,) + tuple(shape)), off + size

    off = 0
    parts = {}
    for k, a in zip(("norm1_g", "hg_lb", "hg_out_g", "sb_q_g", "sb_k_g", "sb_out_g", "norm2_g", "dmod"), pieces):
        parts[k], off = take(off, a.shape)
    loss_parts = gathered[:, off:off + 1]
    loss = jnp.sum(loss_parts)

    def pad8(a):
        return jnp.pad(a, ((0, 0), (0, 8 - a.shape[1]), (0, 0)))

    def small_update(w, m, v, gparts):
        Lw = w.shape[0]
        g, d, m2, v2 = _adamw(pad8(w[None])[0], pad8(m[None])[0], pad8(v[None])[0], pad8(gparts),
                              tr=8, name="adamw_small")
        return g[:Lw], d[:Lw], m2[:Lw], v2[:Lw]

    out = {}
    out["norm1_g"] = small_update(norm1_g, m_norm1_g, v_norm1_g, parts["norm1_g"])
    dlogits = _lb_bwd(hg_lb_logits, parts["hg_lb"], name="lower_bounds_bwd")
    out["hg_lb_logits"] = small_update(hg_lb_logits, m_hg_lb_logits, v_hg_lb_logits, dlogits[None])
    out["hg_out_g"] = small_update(hg_out_g, m_hg_out_g, v_hg_out_g, parts["hg_out_g"])
    out["sb_q_g"] = small_update(sb_q_g, m_sb_q_g, v_sb_q_g, parts["sb_q_g"])
    out["sb_k_g"] = small_update(sb_k_g, m_sb_k_g, v_sb_k_g, parts["sb_k_g"])
    out["sb_out_g"] = small_update(sb_out_g, m_sb_out_g, v_sb_out_g, parts["sb_out_g"])
    out["norm2_g"] = small_update(norm2_g, m_norm2_g, v_norm2_g, parts["norm2_g"])
    out["b_ada"] = small_update(b_ada, m_b_ada, v_b_ada, parts["dmod"])

    dmod_all = parts["dmod"].reshape(N_DEV, L, N_DEV, n_ada)
    dmod_mine = lax.dynamic_index_in_dim(dmod_all, me, axis=2, keepdims=False)
    dmod_mine = jnp.pad(jnp.transpose(dmod_mine, (1, 0, 2)), ((0, 0), (0, 128 - N_DEV), (0, 0)))
    cond_t = jnp.pad(jnp.transpose(cond), ((0, 0), (0, 128 - N_DEV)))
    out["w_ada"] = _adamw_ada(w_ada, m_w_ada, v_w_ada, cond_t, dmod_mine, tr=_tile(D, 256), name="adamw_ada")

    def big_update(w, m, v, recv_l, name):
        return _adamw_layers(w, m, v, recv_l, tr=_tile(w.shape[1], 131072 // w.shape[2]), name=name)[0]

    out["w_ffn_in"] = big_update(w_ffn_in, m_w_ffn_in, v_w_ffn_in, recv["w_ffn_in"], "adamw_w_ffn_in")
    out["w_ffn_out"] = big_update(w_ffn_out, m_w_ffn_out, v_w_ffn_out, recv["w_ffn_out"], "adamw_w_ffn_out")
    out["w_out"] = big_update(w_out, m_w_out, v_w_out, recv["w_out"], "adamw_w_out")
    out["w_in"] = big_update(w_in, m_w_in, v_w_in, recv["w_in"], "adamw_w_in")

    order = ("norm1_g", "w_in", "hg_lb_logits", "hg_out_g", "sb_q_g", "sb_k_g", "sb_out_g", "w_out", "norm2_g",
             "w_ffn_in", "w_ffn_out", "w_ada", "b_ada")
    grads = [out[k][0] for k in order]
    deltas = [out[k][1] for k in order]
    new_m = [out[k][2] for k in order]
    new_v = [out[k][3] for k in order]
    return (loss, grad_x.reshape(1, S, D), *grads, *deltas, *new_m, *new_v)
```

```python
import functools

import jax
import jax.numpy as jnp
from jax import lax
from jax.experimental import pallas as pl
from jax.experimental.pallas import tpu as pltpu

F32 = jnp.float32
BF16 = jnp.bfloat16
MESH = pl.DeviceIdType.MESH

N_DEV = 8
HEAD = 128
CHUNK = 64
N_MOD = 6
EPS = 1e-6
TINY = 1e-30
ADAM_LR = 0.001
ADAM_B1 = 0.9
ADAM_B2 = 0.999
ADAM_EPS = 1e-08
ADAM_WD = 0.01
ADAM_STEP = 10
V7X_VMEM_LIMIT = 56 * 1024 * 1024
SKIP_LOG = -104.0


def _params(sem):
    return pltpu.CompilerParams(dimension_semantics=sem, vmem_limit_bytes=V7X_VMEM_LIMIT)


def _bdot(a, b, dims=(((1,), (0,)), ((), ()))):
    return lax.dot_general(a.astype(BF16), b.astype(BF16), dims, preferred_element_type=F32)


_NT = (((1,), (1,)), ((), ()))
_TN = (((0,), (0,)), ((), ()))


def _sigmoid_pair(x):
    e = jnp.exp(-jnp.abs(x))
    r = 1.0 / (1.0 + e)
    er = e * r
    pos = x >= 0
    return jnp.where(pos, r, er), jnp.where(pos, er, r)


def _split_dot(x, u, parts):
    acc = None
    rem = x
    for _ in range(parts):
        p = rem.astype(BF16)
        rem = rem - p.astype(F32)
        t = lax.dot_general(p, u, (((1,), (0,)), ((), ())), preferred_element_type=F32)
        acc = t if acc is None else acc + t
    return acc


def _split_dot_left(u, x, parts):
    acc = None
    rem = x
    for _ in range(parts):
        p = rem.astype(BF16)
        rem = rem - p.astype(F32)
        t = lax.dot_general(u, p, (((1,), (0,)), ((), ())), preferred_element_type=F32)
        acc = t if acc is None else acc + t
    return acc


def _peer(x, y, c, rel):
    return (x ^ ((rel >> 2) & 1), y ^ ((rel >> 1) & 1), c ^ (rel & 1))


def _exchange(gather, srcs, dsts, send_sems, recv_sems, local_sems, phase):
    x, y, c = lax.axis_index("x"), lax.axis_index("y"), lax.axis_index("c")
    me = 4 * x + 2 * y + c
    for t in range(len(srcs)):
        own = srcs[t] if gather else srcs[t].at[me]
        local = pltpu.make_async_copy(own, dsts[t].at[me], local_sems.at[t])
        if phase == "start":
            local.start()
        for rel in range(1, N_DEV):
            px, py, pc = _peer(x, y, c, rel)
            pid = 4 * px + 2 * py + pc
            k = t * (N_DEV - 1) + rel - 1
            if phase == "start":
                pltpu.make_async_remote_copy(
                    src_ref=srcs[t] if gather else srcs[t].at[pid], dst_ref=dsts[t].at[me],
                    send_sem=send_sems.at[k], recv_sem=recv_sems.at[k],
                    device_id=(px, py, pc), device_id_type=MESH).start()
            else:
                cp = pltpu.make_async_remote_copy(
                    src_ref=own, dst_ref=dsts[t].at[pid], send_sem=send_sems.at[k], recv_sem=recv_sems.at[k],
                    device_id=(px, py, pc), device_id_type=MESH)
                cp.wait_recv()
                cp.wait_send()
        if phase == "wait":
            local.wait()


def _gather_two_level(srcs, dsts, send_sems, recv_sems, local_sems, phase):
    x, y, c = lax.axis_index("x"), lax.axis_index("y"), lax.axis_index("c")
    me, sibling = (x, y, c), (x, y, 1 - c)
    chips = [(1 - x, y), (x, 1 - y), (1 - x, 1 - y)]
    per = N_DEV - 1
    for t in range(len(srcs)):
        def slot(px, py, pc, t=t):
            return dsts[t].at[4 * px + 2 * py + pc]

        def copy(k, block, to, src=None, t=t):
            return pltpu.make_async_remote_copy(
                src_ref=slot(*block) if src is None else src, dst_ref=slot(*block),
                send_sem=send_sems.at[t * per + k], recv_sem=recv_sems.at[t * per + k],
                device_id=to, device_id_type=MESH)

        local = pltpu.make_async_copy(srcs[t], slot(*me), local_sems.at[t])
        first = [copy(0, me, sibling, src=srcs[t])]
        first += [copy(1 + j, me, (*chip, c), src=srcs[t]) for j, chip in enumerate(chips)]
        passed = [copy(4 + j, (*chip, c), sibling) for j, chip in enumerate(chips)]
        if phase == "start":
            local.start()
            for cp in first:
                cp.start()
        elif phase == "forward":
            for j, chip in enumerate(chips):
                copy(1 + j, (*chip, c), me).wait_recv()
                passed[j].start()
        else:
            copy(0, sibling, me).wait_recv()
            for j, chip in enumerate(chips):
                copy(4 + j, (*chip, 1 - c), me).wait_recv()
            for cp in first + passed:
                cp.wait_send()
            local.wait()


def _exchange_scratch(n):
    return [pltpu.SemaphoreType.DMA(((N_DEV - 1) * n,)), pltpu.SemaphoreType.DMA(((N_DEV - 1) * n,)),
            pltpu.SemaphoreType.DMA((n,))]


def _pcall(body, *, grid, in_specs, out_specs, out_shape, scratch_shapes=(), semantics, name, args, side=None):
    single = not isinstance(out_shape, (tuple, list))
    if single:
        out_specs, out_shape = [out_specs], [out_shape]
    in_specs, out_specs, out_shape = list(in_specs), list(out_specs), list(out_shape)
    scratch_shapes = list(scratch_shapes)
    n_in, n_out, n_scr = len(in_specs), len(out_specs), len(scratch_shapes)
    if side is None:
        res = pl.pallas_call(body, grid=grid, in_specs=in_specs, out_specs=out_specs, out_shape=out_shape,
                             scratch_shapes=scratch_shapes, compiler_params=_params(semantics), name=name)(*args)
        return (res[0] if single else tuple(res)), None
    gather, srcs = side
    n = len(srcs)

    def full(*refs):
        ins = refs[:n_in]
        s_in = refs[n_in:n_in + n]
        outs = refs[n_in + n:n_in + n + n_out]
        s_out = refs[n_in + n + n_out:n_in + 2 * n + n_out]
        scr = refs[n_in + 2 * n + n_out:n_in + 2 * n + n_out + n_scr]
        send_sems, recv_sems, local_sems = refs[n_in + 2 * n + n_out + n_scr:]
        step = pl.program_id(0)
        steps = grid[0]
        for ax in range(1, len(grid)):
            step = step * grid[ax] + pl.program_id(ax)
            steps *= grid[ax]

        def exchange(phase):
            if gather:
                _gather_two_level(s_in, s_out, send_sems, recv_sems, local_sems, phase)
            elif phase != "forward":
                _exchange(False, s_in, s_out, send_sems, recv_sems, local_sems, phase)

        pl.when(step == 0)(lambda: exchange("start"))
        if gather:
            pl.when(step == (steps * 4) // 5)(lambda: exchange("forward"))
        body(*ins, *outs, *scr)
        pl.when(step == steps - 1)(lambda: exchange("wait"))

    any_spec = pl.BlockSpec(memory_space=pl.ANY)
    s_shapes = [jax.ShapeDtypeStruct(((N_DEV,) + s.shape) if gather else s.shape, s.dtype) for s in srcs]
    res = pl.pallas_call(full, grid=grid, in_specs=in_specs + [any_spec] * n,
                         out_specs=out_specs + [any_spec] * n, out_shape=out_shape + s_shapes,
                         scratch_shapes=scratch_shapes + _exchange_scratch(n),
                         compiler_params=_params(("arbitrary",) * len(grid)), name=name)(*args, *srcs)
    main = res[:n_out]
    return (main[0] if single else tuple(main)), list(res[n_out:])


def _mm_nn(a, b, *, tm, nb=None, out_dtype=F32, resid=None, gate=None, name, side=None):
    M, K = a.shape
    if b.ndim == 3:
        NB, _, n = b.shape
        b_spec = pl.BlockSpec((None, K, n), lambda j, i: (j, 0, 0))
    else:
        n = nb
        NB = b.shape[1] // nb
        b_spec = pl.BlockSpec((K, n), lambda j, i: (0, j))
    N = NB * n
    epi = resid is not None

    def body(*refs):
        if epi:
            a_ref, b_ref, r_ref, g_ref, o_ref, acc_ref = refs
        else:
            a_ref, b_ref, o_ref = refs
        acc = jnp.dot(a_ref[...], b_ref[...], preferred_element_type=F32)
        if epi:
            o_ref[...] = r_ref[...] + g_ref[...] * acc
            acc_ref[...] = acc.astype(BF16)
        else:
            o_ref[...] = acc.astype(out_dtype)

    in_specs = [pl.BlockSpec((tm, K), lambda j, i: (i, 0)), b_spec]
    args = [a, b]
    o_spec = pl.BlockSpec((tm, n), lambda j, i: (i, j))
    if epi:
        in_specs += [pl.BlockSpec((tm, n), lambda j, i: (i, j)), pl.BlockSpec((1, n), lambda j, i: (0, j))]
        args += [resid, gate]
        out_shape = [jax.ShapeDtypeStruct((M, N), F32), jax.ShapeDtypeStruct((M, N), BF16)]
        out_specs = [o_spec, o_spec]
    else:
        out_shape = [jax.ShapeDtypeStruct((M, N), out_dtype)]
        out_specs = [o_spec]
    res, got = _pcall(body, grid=(NB, M // tm), in_specs=in_specs, out_specs=out_specs, out_shape=out_shape,
                      semantics=("parallel", "parallel"), name=name, args=args, side=side)
    return (res[0] if len(res) == 1 else res), got


def _halves(dy, dy2, blk_rows, blk_cols, nblocks, row_of, col_of, last_row=None):
    if dy2 is None:
        return [pl.BlockSpec((blk_rows, blk_cols), lambda *g: (row_of(*g), col_of(*g)))], None
    half = nblocks // 2

    def left(*g):
        r, c = row_of(*g), col_of(*g)
        if last_row is None:
            return (r, jnp.minimum(c, half - 1))
        return (jnp.where(c < half, r, last_row), jnp.minimum(c, half - 1))

    def right(*g):
        r, c = row_of(*g), col_of(*g)
        if last_row is None:
            return (r, jnp.maximum(c - half, 0))
        return (jnp.where(c >= half, r, 0), jnp.maximum(c - half, 0))

    return [pl.BlockSpec((blk_rows, blk_cols), left), pl.BlockSpec((blk_rows, blk_cols), right)], half


def _mm_nt(dy, w, *, tm, kb, nb=None, name, side=None, dy2=None):
    M = dy.shape[0]
    N = dy.shape[1] * (1 if dy2 is None else 2)
    if w.ndim == 3:
        NB, Kt, n = w.shape
        w_spec = pl.BlockSpec((None, kb, n), lambda i, k, j: (j, k, 0))
    else:
        Kt = w.shape[0]
        n = nb
        NB = N // nb
        w_spec = pl.BlockSpec((kb, n), lambda i, k, j: (k, j))
    KB = Kt // kb
    dy_specs, half = _halves(dy, dy2, tm, n, NB, lambda i, k, j: i, lambda i, k, j: j)
    n_op = len(dy_specs)

    def body(*refs):
        dy_refs = refs[:n_op]
        w_ref = refs[n_op]
        o_ref = refs[n_op + 1]
        acc = refs[n_op + 2:]
        j = pl.program_id(2)

        def use(dy_ref):
            part = lax.dot_general(dy_ref[...], w_ref[...], _NT, preferred_element_type=F32)
            if NB == 1:
                o_ref[...] = part
                return
            acc_ref, = acc

            @pl.when(j == 0)
            def _():
                acc_ref[...] = part

            @pl.when(jnp.logical_and(j > 0, j < NB - 1))
            def _():
                acc_ref[...] += part

            @pl.when(j == NB - 1)
            def _():
                o_ref[...] = acc_ref[...] + part

        if half is None:
            use(dy_refs[0])
        else:
            pl.when(j < half)(lambda: use(dy_refs[0]))
            pl.when(j >= half)(lambda: use(dy_refs[1]))

    return _pcall(
        body, grid=(M // tm, KB, NB), in_specs=dy_specs + [w_spec],
        out_specs=pl.BlockSpec((tm, kb), lambda i, k, j: (i, k)),
        out_shape=jax.ShapeDtypeStruct((M, Kt), F32),
        scratch_shapes=[] if NB == 1 else [pltpu.VMEM((tm, kb), F32)],
        semantics=("parallel", "parallel", "arbitrary"), name=name,
        args=(dy, w) if dy2 is None else (dy, dy2, w), side=side)


def _mm_tn(x, dy, *, tm, kb, nb, blocked, name, side=None, dy2=None):
    M, K = x.shape
    N = dy.shape[1] * (1 if dy2 is None else 2)
    KB, NB, MB = K // kb, N // nb, M // tm
    dy_specs, half = _halves(dy, dy2, tm, nb, NB, lambda k, n, m: m, lambda k, n, m: n, last_row=MB - 1)
    n_dy = len(dy_specs)

    def body(*refs):
        x_ref = refs[0]
        dy_refs = refs[1:1 + n_dy]
        o_ref = refs[1 + n_dy]
        acc = refs[2 + n_dy:]
        m = pl.program_id(2)

        def use(dy_ref):
            part = lax.dot_general(x_ref[...], dy_ref[...], _TN, preferred_element_type=F32)
            if MB == 1:
                o_ref[...] = part.astype(BF16)
                return
            acc_ref, = acc

            @pl.when(m == 0)
            def _():
                acc_ref[...] = part

            @pl.when(jnp.logical_and(m > 0, m < MB - 1))
            def _():
                acc_ref[...] += part

            @pl.when(m == MB - 1)
            def _():
                o_ref[...] = (acc_ref[...] + part).astype(BF16)

        if half is None:
            use(dy_refs[0])
        else:
            nblk = pl.program_id(1)
            pl.when(nblk < half)(lambda: use(dy_refs[0]))
            pl.when(nblk >= half)(lambda: use(dy_refs[1]))

    if blocked:
        out_shape = jax.ShapeDtypeStruct((NB, K, nb), BF16)
        o_spec = pl.BlockSpec((None, kb, nb), lambda k, n, m: (n, k, 0))
    else:
        out_shape = jax.ShapeDtypeStruct((K, N), BF16)
        o_spec = pl.BlockSpec((kb, nb), lambda k, n, m: (k, n))
    x_spec = pl.BlockSpec((tm, kb), lambda k, n, m: (m, k))
    return _pcall(
        body, grid=(KB, NB, MB), in_specs=[x_spec] + dy_specs,
        out_specs=o_spec, out_shape=out_shape,
        scratch_shapes=[] if MB == 1 else [pltpu.VMEM((kb, nb), F32)],
        semantics=("parallel", "parallel", "arbitrary"), name=name,
        args=(x, dy) if dy2 is None else (x, dy, dy2), side=side)


def _ffn_in_fwd(h, w, *, tm, nb, name, side=None):
    M, K = h.shape
    if w.ndim == 3:
        J, _, n = w.shape
        half = J // 2
        specs = [pl.BlockSpec((None, K, n), lambda j, i: (j, 0, 0)),
                 pl.BlockSpec((None, K, n), lambda j, i: (j + half, 0, 0))]
    else:
        n = nb
        half = w.shape[1] // (2 * nb)
        specs = [pl.BlockSpec((K, n), lambda j, i: (0, j)), pl.BlockSpec((K, n), lambda j, i: (0, j + half))]
    F = half * n

    def body(h_ref, wg_ref, wu_ref, gate_ref, up_ref, act_ref):
        hv = h_ref[...]
        gate = jnp.dot(hv, wg_ref[...], preferred_element_type=F32)
        up = jnp.dot(hv, wu_ref[...], preferred_element_type=F32)
        s, _ = _sigmoid_pair(gate)
        gate_ref[...] = gate
        up_ref[...] = up
        act_ref[...] = (gate * s * up).astype(BF16)

    o_spec = pl.BlockSpec((tm, n), lambda j, i: (i, j))
    f32 = jax.ShapeDtypeStruct((M, F), F32)
    return _pcall(
        body, grid=(half, M // tm), in_specs=[pl.BlockSpec((tm, K), lambda j, i: (i, 0))] + specs,
        out_specs=(o_spec, o_spec, o_spec), out_shape=(f32, f32, jax.ShapeDtypeStruct((M, F), BF16)),
        semantics=("parallel", "parallel"), name=name, args=(h, w, w), side=side)


def _ffn_out_bwd_x(dy, w, gate, up, *, tm, kb, name, side=None):
    M, D = dy.shape
    F = w.shape[0]

    def body(dy_ref, w_ref, g_ref, u_ref, dg_ref, du_ref):
        da = lax.dot_general(dy_ref[...], w_ref[...], _NT, preferred_element_type=F32)
        gate = g_ref[...]
        s, ns = _sigmoid_pair(gate)
        dg_ref[...] = (da * u_ref[...] * (s * (1.0 + gate * ns))).astype(BF16)
        du_ref[...] = (da * (gate * s)).astype(BF16)

    tile = pl.BlockSpec((tm, kb), lambda i, k: (i, k))
    act = jax.ShapeDtypeStruct((M, F), BF16)
    return _pcall(
        body, grid=(M // tm, F // kb),
        in_specs=[pl.BlockSpec((tm, D), lambda i, k: (i, 0)), pl.BlockSpec((kb, D), lambda i, k: (k, 0)), tile, tile],
        out_specs=(tile, tile), out_shape=(act, act),
        semantics=("parallel", "parallel"), name=name, args=(dy, w, gate, up), side=side)


def _modnorm_fwd(x, gain, sc, sh, *, tm, name):
    S, D = x.shape

    def body(x_ref, g_ref, sc_ref, sh_ref, h_ref):
        xv = x_ref[...]
        rstd = lax.rsqrt(jnp.mean(xv * xv, axis=-1, keepdims=True) + EPS)
        y = (xv * rstd) * g_ref[...]
        h_ref[...] = (y * (1.0 + sc_ref[...]) + sh_ref[...]).astype(BF16)

    row = pl.BlockSpec((1, D), lambda i: (0, 0))
    return pl.pallas_call(
        body, grid=(S // tm,),
        in_specs=[pl.BlockSpec((tm, D), lambda i: (i, 0)), row, row, row],
        out_specs=pl.BlockSpec((tm, D), lambda i: (i, 0)),
        out_shape=jax.ShapeDtypeStruct((S, D), BF16),
        compiler_params=_params(("parallel",)), name=name)(x, gain, sc, sh)


def _modnorm_bwd(x, dh, dres, gain, sc, branch, gate, *, tm, name):
    S, D = x.shape
    has_prev = branch is not None

    def body(*refs):
        if has_prev:
            (x_ref, dh_ref, dr_ref, g_ref, sc_ref, br_ref, gt_ref,
             dx_ref, dbr_ref, dgt_ref, dsh_ref, dsc_ref, dgn_ref) = refs
        else:
            (x_ref, dh_ref, dr_ref, g_ref, sc_ref,
             dx_ref, dsh_ref, dsc_ref, dgn_ref) = refs
        i = pl.program_id(0)
        xv = x_ref[...]
        dh_v = dh_ref[...]
        gv = g_ref[...]
        scale1 = 1.0 + sc_ref[...]
        rstd = lax.rsqrt(jnp.mean(xv * xv, axis=-1, keepdims=True) + EPS)
        n = xv * rstd
        dn = dh_v * (gv * scale1)
        dx = rstd * (dn - n * jnp.mean(dn * n, axis=-1, keepdims=True)) + dr_ref[...]
        dx_ref[...] = dx
        dhn = dh_v * n
        p_sh = jnp.sum(dh_v, axis=0, keepdims=True)
        p_sc = jnp.sum(dhn, axis=0, keepdims=True) * gv
        p_gn = jnp.sum(dhn, axis=0, keepdims=True) * scale1
        if has_prev:
            dbr_ref[...] = (gt_ref[...] * dx).astype(BF16)
            p_gt = jnp.sum(dx * br_ref[...].astype(F32), axis=0, keepdims=True)

        @pl.when(i == 0)
        def _():
            dsh_ref[...] = p_sh
            dsc_ref[...] = p_sc
            dgn_ref[...] = p_gn
            if has_prev:
                dgt_ref[...] = p_gt

        @pl.when(i > 0)
        def _():
            dsh_ref[...] += p_sh
            dsc_ref[...] += p_sc
            dgn_ref[...] += p_gn
            if has_prev:
                dgt_ref[...] += p_gt

    tile = pl.BlockSpec((tm, D), lambda i: (i, 0))
    row = pl.BlockSpec((1, D), lambda i: (0, 0))
    row_shape = jax.ShapeDtypeStruct((1, D), F32)
    if has_prev:
        in_specs = [tile, tile, tile, row, row, tile, row]
        args = (x, dh, dres, gain, sc, branch, gate)
        out_specs = (tile, tile, row, row, row, row)
        out_shape = (jax.ShapeDtypeStruct((S, D), F32), jax.ShapeDtypeStruct((S, D), BF16),
                     row_shape, row_shape, row_shape, row_shape)
    else:
        in_specs = [tile, tile, tile, row, row]
        args = (x, dh, dres, gain, sc)
        out_specs = (tile, row, row, row)
        out_shape = (jax.ShapeDtypeStruct((S, D), F32), row_shape, row_shape, row_shape)
    return pl.pallas_call(body, grid=(S // tm,), in_specs=in_specs, out_specs=out_specs,
                          out_shape=out_shape, compiler_params=_params(("arbitrary",)),
                          name=name)(*args)


def _loss_bwd(y, target, branch, gate, *, tm, name):
    S, D = y.shape
    nsteps = S // tm

    def body(y_ref, t_ref, br_ref, gt_ref, dy_ref, dbr_ref, dgt_ref, loss_ref, col_ref):
        i = pl.program_id(0)
        diff = y_ref[...] - t_ref[...]
        dy = diff * (1.0 / D)
        dy_ref[...] = dy
        dbr_ref[...] = (gt_ref[...] * dy).astype(BF16)
        p_gt = jnp.sum(dy * br_ref[...].astype(F32), axis=0, keepdims=True)
        p_col = jnp.sum(diff * diff, axis=0, keepdims=True)

        @pl.when(i == 0)
        def _():
            dgt_ref[...] = p_gt
            col_ref[...] = p_col

        @pl.when(i > 0)
        def _():
            dgt_ref[...] += p_gt
            col_ref[...] += p_col

        @pl.when(i == nsteps - 1)
        def _():
            tot = jnp.sum(col_ref[...], axis=-1, keepdims=True) * (0.5 / D)
            loss_ref[...] = jnp.broadcast_to(tot, (1, 128))

    tile = pl.BlockSpec((tm, D), lambda i: (i, 0))
    row = pl.BlockSpec((1, D), lambda i: (0, 0))
    return pl.pallas_call(
        body, grid=(nsteps,), in_specs=[tile, tile, tile, row],
        out_specs=(tile, tile, row, pl.BlockSpec((1, 128), lambda i: (0, 0))),
        out_shape=(jax.ShapeDtypeStruct((S, D), F32), jax.ShapeDtypeStruct((S, D), BF16),
                   jax.ShapeDtypeStruct((1, D), F32), jax.ShapeDtypeStruct((1, 128), F32)),
        scratch_shapes=[pltpu.VMEM((1, D), F32)],
        compiler_params=_params(("arbitrary",)), name=name)(y, target, branch, gate)


def _hg_chunk(q, fl, lbv, tri):
    C = q.shape[0]
    sq, nsq = _sigmoid_pair(q)
    qa = q * sq
    sig, nsig = _sigmoid_pair(fl)
    one_lb = 1.0 - lbv
    f = lbv + one_lb * sig
    fc = jnp.maximum(f, TINY)
    lf = jnp.log(fc)
    k = one_lb * nsig
    b = _split_dot_left(tri, lf, 3)
    row = lax.broadcasted_iota(jnp.int32, b.shape, 0)
    bm = jnp.sum(jnp.where(row == C // 2 - 1, b, 0.0), axis=0, keepdims=True)
    bl = jnp.sum(jnp.where(row == C - 1, b, 0.0), axis=0, keepdims=True)
    eb = jnp.exp(b)
    ebm = jnp.exp(b - bm)
    enbm = jnp.exp(bm - b)
    ebl = jnp.exp(bl - b)
    ebL = jnp.exp(bl)

    def operand(t):
        return t.astype(BF16).astype(F32)

    return dict(sq=sq, nsq=nsq, qa=qa, sig=sig, nsig=nsig, one_lb=one_lb, f=f, fc=fc, k=k,
                eb=eb, ebm=ebm, enbm=enbm, ebl=ebl, ebL=ebL,
                Qm=operand(qa * ebm), Km=operand(k * enbm), Qb=operand(qa * eb), Kh=operand(k * ebl), row=row)


def _causal_incl(C):
    r = lax.broadcasted_iota(jnp.int32, (C, C), 0)
    c = lax.broadcasted_iota(jnp.int32, (C, C), 1)
    return r >= c


def _hg_fwd(proj, lb, out_g, *, n_heads, cg, name, side=None):
    S = proj.shape[0]
    H = n_heads
    W = H * HEAD
    T = cg * CHUNK
    NG = S // T
    tri = jnp.tril(jnp.ones((CHUNK, CHUNK), F32)).astype(BF16)

    def body(q_ref, f_ref, v_ref, g_ref, lb_ref, og_ref, tri_ref, o_ref, on_ref, st_ref, s_scr):
        @pl.when(pl.program_id(1) == 0)
        def _():
            s_scr[...] = jnp.zeros_like(s_scr)

        lbv = lb_ref[...]
        ogv = og_ref[...]
        triv = tri_ref[...]
        mask = _causal_incl(CHUNK)
        for c in range(cg):
            rows = pl.ds(c * CHUNK, CHUNK)
            v = v_ref[rows, :]
            gg = g_ref[rows, :]
            cm = _hg_chunk(q_ref[rows, :], f_ref[rows, :], lbv, triv)
            s0 = s_scr[...]
            st_ref[c] = s0
            A = jnp.where(mask, _bdot(cm["Qm"], cm["Km"], _NT), 0.0)
            o = _bdot(A, v) + _bdot(cm["Qb"], s0, _NT)
            s_scr[...] = s0 * cm["ebL"] + _bdot(v, cm["Kh"], _TN)
            o_ref[rows, :] = o
            rstd = lax.rsqrt(jnp.mean(o * o, axis=-1, keepdims=True) + EPS)
            sg, _ = _sigmoid_pair(gg)
            on_ref[rows, :] = (((o * rstd) * ogv) * (gg * sg)).astype(BF16)

    def col(group):
        return pl.BlockSpec((T, HEAD), lambda h, g: (g, group * H + h))

    vec = pl.BlockSpec((1, HEAD), lambda h, g: (0, h))
    return _pcall(
        body, grid=(H, NG),
        in_specs=[col(0), col(1), col(2), col(3), vec, vec,
                  pl.BlockSpec((CHUNK, CHUNK), lambda h, g: (0, 0))],
        out_specs=(pl.BlockSpec((T, HEAD), lambda h, g: (g, h)),
                   pl.BlockSpec((T, HEAD), lambda h, g: (g, h)),
                   pl.BlockSpec((cg, None, HEAD, HEAD), lambda h, g: (g, h, 0, 0))),
        out_shape=(jax.ShapeDtypeStruct((S, W), F32), jax.ShapeDtypeStruct((S, W), BF16),
                   jax.ShapeDtypeStruct((S // CHUNK, H, HEAD, HEAD), F32)),
        scratch_shapes=[pltpu.VMEM((HEAD, HEAD), F32)],
        semantics=("parallel", "arbitrary"), name=name,
        args=(proj, proj, proj, proj, lb, out_g, tri), side=side)


def _hg_bwd(proj, o_pre, d_on, d_on_col0, states, lb, out_g, *, n_heads, cg, name, side=None):
    S = proj.shape[0]
    H = n_heads
    W = H * HEAD
    T = cg * CHUNK
    NG = S // T
    tri = jnp.tril(jnp.ones((CHUNK, CHUNK), F32)).astype(BF16)
    triu = jnp.triu(jnp.ones((CHUNK, CHUNK), F32)).astype(BF16)

    def body(q_ref, f_ref, v_ref, g_ref, o_ref, dy_ref, st_ref, lb_ref, og_ref, tri_ref, triu_ref,
             dq_ref, df_ref, di_ref, dg_ref, dlb_ref, dog_ref, ds_scr):
        gstep = pl.program_id(1)

        @pl.when(gstep == 0)
        def _():
            ds_scr[...] = jnp.zeros_like(ds_scr)
            dlb_ref[...] = jnp.zeros_like(dlb_ref)
            dog_ref[...] = jnp.zeros_like(dog_ref)

        lbv = lb_ref[...]
        ogv = og_ref[...]
        triv = tri_ref[...]
        triuv = triu_ref[...]
        mask = _causal_incl(CHUNK)
        dlb_acc = jnp.zeros((1, HEAD), F32)
        dog_acc = jnp.zeros((1, HEAD), F32)
        for c in reversed(range(cg)):
            rows = pl.ds(c * CHUNK, CHUNK)
            q = q_ref[rows, :]
            v = v_ref[rows, :]
            gg = g_ref[rows, :]
            o = o_ref[rows, :]
            dy = dy_ref[rows, :]
            cm = _hg_chunk(q, f_ref[rows, :], lbv, triv)
            s0 = st_ref[c]
            ds1 = ds_scr[...]
            rstd = lax.rsqrt(jnp.mean(o * o, axis=-1, keepdims=True) + EPS)
            n = o * rstd
            sg, nsg = _sigmoid_pair(gg)
            silu_g = gg * sg
            dyn = dy * n
            dog_acc = dog_acc + jnp.sum(dyn * silu_g, axis=0, keepdims=True)
            dg_ref[rows, :] = (dyn * ogv * (sg * (1.0 + gg * nsg))).astype(BF16)
            dn = dy * (ogv * silu_g)
            d_o = rstd * (dn - n * jnp.mean(dn * n, axis=-1, keepdims=True))
            A = jnp.where(mask, _bdot(cm["Qm"], cm["Km"], _NT), 0.0)
            dA = jnp.where(mask, _bdot(d_o, v, _NT), 0.0)
            dV = _bdot(A, d_o, _TN) + _bdot(cm["Kh"], ds1, _NT)
            dQm = _bdot(dA, cm["Km"])
            dKm = _bdot(dA, cm["Qm"], _TN)
            dQb = _bdot(d_o, s0)
            dKh = _bdot(v, ds1)
            ds_scr[...] = ds1 * cm["ebL"] + _bdot(d_o, cm["Qb"], _TN)
            kh_term = dKh * cm["Kh"]
            db = dQm * cm["Qm"] - dKm * cm["Km"] + dQb * cm["Qb"] - kh_term
            dbl = (jnp.sum(kh_term, axis=0, keepdims=True)
                   + cm["ebL"] * jnp.sum(ds1 * s0, axis=0, keepdims=True))
            db = db + jnp.where(cm["row"] == CHUNK - 1, dbl, 0.0)
            dlf = _split_dot_left(triuv, db, 3)
            dqa = dQm * cm["ebm"] + dQb * cm["eb"]
            dq_ref[rows, :] = (dqa * (cm["sq"] * (1.0 + q * cm["nsq"]))).astype(BF16)
            dk = dKm * cm["enbm"] + dKh * cm["ebl"]
            dfc = jnp.where(cm["f"] > TINY, dlf / cm["fc"], 0.0)
            t = dfc - dk
            df_ref[rows, :] = (t * (cm["one_lb"] * cm["sig"] * cm["nsig"])).astype(BF16)
            dlb_acc = dlb_acc + jnp.sum(t * cm["nsig"], axis=0, keepdims=True)
            di_ref[rows, :] = dV.astype(BF16)
        dlb_ref[...] += dlb_acc
        dog_ref[...] += dog_acc

    def col(group):
        return pl.BlockSpec((T, HEAD), lambda h, g: (NG - 1 - g, group * H + h))

    own = pl.BlockSpec((T, HEAD), lambda h, g: (NG - 1 - g, h))
    vec = pl.BlockSpec((1, HEAD), lambda h, g: (0, h))
    cst = pl.BlockSpec((CHUNK, CHUNK), lambda h, g: (0, 0))
    act = jax.ShapeDtypeStruct((S, W), BF16)
    vec_shape = jax.ShapeDtypeStruct((1, W), F32)
    return _pcall(
        body, grid=(H, NG),
        in_specs=[col(0), col(1), col(2), col(3), own,
                  pl.BlockSpec((T, HEAD), lambda h, g: (NG - 1 - g, d_on_col0 + h)),
                  pl.BlockSpec((cg, None, HEAD, HEAD), lambda h, g: (NG - 1 - g, h, 0, 0)),
                  vec, vec, cst, cst],
        out_specs=(own, own, own, own, vec, vec),
        out_shape=(act, act, act, act, vec_shape, vec_shape),
        scratch_shapes=[pltpu.VMEM((HEAD, HEAD), F32)],
        semantics=("parallel", "arbitrary"), name=name,
        args=(proj, proj, proj, proj, o_pre, d_on, states, lb, out_g, tri, triu), side=side)


def _sb_pre(proj, q_g, k_g, *, n_heads, col0, tm, name):
    S = proj.shape[0]
    H = n_heads
    W = H * HEAD

    def body(q_ref, k_ref, v_ref, qg_ref, kg_ref, qh_ref, kh_ref, vh_ref):
        for src, g_ref, dst in ((q_ref, qg_ref, qh_ref), (k_ref, kg_ref, kh_ref)):
            xv = src[...]
            rstd = lax.rsqrt(jnp.mean(xv * xv, axis=-1, keepdims=True) + EPS)
            dst[...] = ((xv * rstd) * g_ref[...]).astype(BF16)
        vh_ref[...] = v_ref[...].astype(BF16)

    def col(group):
        return pl.BlockSpec((tm, HEAD), lambda i, h: (i, col0 + group * H + h))

    vec = pl.BlockSpec((1, HEAD), lambda i, h: (0, 0))
    own = pl.BlockSpec((tm, HEAD), lambda i, h: (i, h))
    act = jax.ShapeDtypeStruct((S, W), BF16)
    return pl.pallas_call(
        body, grid=(S // tm, H), in_specs=[col(0), col(1), col(2), vec, vec],
        out_specs=(own, own, own), out_shape=(act, act, act),
        compiler_params=_params(("parallel", "parallel")), name=name)(proj, proj, proj, q_g, k_g)


def _sb_scores(q, k_blk, scale):
    z = lax.dot_general(q, k_blk, _NT, preferred_element_type=F32) * scale
    e = jnp.exp(-jnp.abs(z))
    sp = jnp.maximum(z, 0.0) + jnp.log(1.0 + e)
    return z, e, sp


def _heads_per_step(n_heads):
    return 2 if n_heads % 2 == 0 else 1


def _max_all(values):
    m = jnp.max(values[0])
    for v in values[1:]:
        m = jnp.maximum(m, jnp.max(v))
    return m


def _strict_lower_mask(t):
    r = lax.broadcasted_iota(jnp.int32, (t, t), 0)
    c = lax.broadcasted_iota(jnp.int32, (t, t), 1)
    return c < r


def _sb_fwd(qh, kh, vh, out_g, *, n_heads, tq, name, side=None):
    S, W = qh.shape
    HP = _heads_per_step(n_heads)
    WP = HP * HEAD
    NQ = S // tq
    scale = HEAD ** -0.5
    u_strict = jnp.tril(jnp.ones((tq, tq), F32), -1).astype(BF16)

    def body(q_ref, k_ref, v_ref, og_ref, u_ref, o_ref, on_ref):
        qi = pl.program_id(1)
        u = u_ref[...]
        heads = [slice(hh * HEAD, (hh + 1) * HEAD) for hh in range(HP)]
        qs = [q_ref[:, cols] for cols in heads]

        def block(kb, r_carry, diag, valid=None):
            rows = pl.ds(pl.multiple_of(kb * tq, tq), tq)
            pvs, rs = [], []
            for hh, cols in enumerate(heads):
                k_blk = k_ref[rows, cols]
                v_blk = v_ref[rows, cols]
                z, _, sp = _sb_scores(qs[hh], k_blk, scale)
                if diag:
                    m = _strict_lower_mask(tq)
                    L = jnp.where(m, -sp, 0.0)
                else:
                    L = -sp
                C = _split_dot(L, u, 2)
                a = jnp.exp(z - sp + C + r_carry[hh])
                if diag:
                    a = jnp.where(m, a, 0.0)
                if valid is not None:
                    a = jnp.where(valid, a, 0.0)
                pvs.append(lax.dot_general(a.astype(BF16), v_blk, (((1,), (0,)), ((), ())),
                                           preferred_element_type=F32))
                rs.append(r_carry[hh] + (C[:, 0:1] + L[:, 0:1]))
            return tuple(pvs), tuple(rs)

        acc_d, r_d = block(qi, (jnp.zeros((tq, 1), F32),) * HP, True)
        acc_p, r0 = block(jnp.maximum(qi - 1, 0), r_d, False, valid=qi > 0)
        acc0 = tuple(a + b for a, b in zip(acc_d, acc_p))

        def cond(st):
            kb, _, _, rmax = st
            return jnp.logical_and(kb >= 0, rmax > SKIP_LOG)

        def step(st):
            kb, acc, r, _ = st
            pv, r2 = block(kb, r, False)
            return kb - 1, tuple(a + b for a, b in zip(acc, pv)), r2, _max_all(r2)

        _, accs, _, _ = lax.while_loop(cond, step, (qi - 2, acc0, r0, _max_all(r0)))
        for hh, cols in enumerate(heads):
            acc = accs[hh]
            o_ref[:, cols] = acc
            rstd = lax.rsqrt(jnp.mean(acc * acc, axis=-1, keepdims=True) + EPS)
            on_ref[:, cols] = ((acc * rstd) * og_ref[:, cols]).astype(BF16)

    blk = pl.BlockSpec((tq, WP), lambda h, i: (i, h))
    full = pl.BlockSpec((S, WP), lambda h, i: (0, h))
    return _pcall(
        body, grid=(n_heads // HP, NQ),
        in_specs=[blk, full, full, pl.BlockSpec((1, WP), lambda h, i: (0, h)),
                  pl.BlockSpec((tq, tq), lambda h, i: (0, 0))],
        out_specs=(blk, blk),
        out_shape=(jax.ShapeDtypeStruct((S, W), F32), jax.ShapeDtypeStruct((S, W), BF16)),
        semantics=("parallel", "arbitrary"), name=name, args=(qh, kh, vh, out_g, u_strict), side=side)


def _sb_bwd(qh, kh, vh, o_pre, d_on, d_on_col0, out_g, *, n_heads, tq, name, side=None):
    S, W = qh.shape
    HP = _heads_per_step(n_heads)
    WP = HP * HEAD
    assert d_on_col0 % HP == 0
    NQ = S // tq
    scale = HEAD ** -0.5
    u_strict = jnp.tril(jnp.ones((tq, tq), F32), -1).astype(BF16)
    u_incl = jnp.tril(jnp.ones((tq, tq), F32)).astype(BF16)

    def body(q_ref, k_ref, v_ref, o_ref, dy_ref, og_ref, us_ref, ui_ref,
             dq_ref, dk_ref, dv_ref, dog_ref):
        qi = pl.program_id(1)

        @pl.when(qi == 0)
        def _():
            dk_ref[...] = jnp.zeros_like(dk_ref)
            dv_ref[...] = jnp.zeros_like(dv_ref)
            dog_ref[...] = jnp.zeros_like(dog_ref)

        us = us_ref[...]
        ui = ui_ref[...]
        heads = [slice(hh * HEAD, (hh + 1) * HEAD) for hh in range(HP)]
        qs, d_obs, deltas = [], [], []
        for cols in heads:
            qs.append(q_ref[:, cols])
            o = o_ref[:, cols]
            dy = dy_ref[:, cols]
            rstd = lax.rsqrt(jnp.mean(o * o, axis=-1, keepdims=True) + EPS)
            n = o * rstd
            dog_ref[:, cols] += jnp.sum(dy * n, axis=0, keepdims=True)
            dn = dy * og_ref[:, cols]
            d_o = rstd * (dn - n * jnp.mean(dn * n, axis=-1, keepdims=True))
            d_ob = d_o.astype(BF16)
            d_obs.append(d_ob)
            deltas.append(jnp.sum(d_ob.astype(F32) * o, axis=-1, keepdims=True))
        q_ts = [q.T for q in qs]
        d_ob_ts = [d.T for d in d_obs]

        def block(kb, r_carry, g_carry, diag, valid=None):
            rows = pl.ds(pl.multiple_of(kb * tq, tq), tq)
            dqs, rs, gs = [], [], []
            for hh, cols in enumerate(heads):
                k_blk = k_ref[rows, cols]
                v_blk = v_ref[rows, cols]
                z, e, sp = _sb_scores(qs[hh], k_blk, scale)
                if diag:
                    m = _strict_lower_mask(tq)
                    L = jnp.where(m, -sp, 0.0)
                else:
                    L = -sp
                C = _split_dot(L, us, 2)
                a = jnp.exp(z - sp + C + r_carry[hh])
                if diag:
                    a = jnp.where(m, a, 0.0)
                if valid is not None:
                    a = jnp.where(valid, a, 0.0)
                ab = a.astype(BF16)
                dA = lax.dot_general(d_obs[hh], v_blk, _NT, preferred_element_type=F32)
                G = ab.astype(F32) * dA
                SI = _split_dot(G, ui, 2)
                P = deltas[hh] - (g_carry[hh] + SI)
                r = 1.0 / (1.0 + e)
                sig = jnp.where(z >= 0, r, e * r)
                dz = G - (G + P) * sig
                if diag:
                    dz = jnp.where(m, dz, 0.0)
                if valid is not None:
                    dz = jnp.where(valid, dz, 0.0)
                dzb = (dz * scale).astype(BF16)
                dqs.append(lax.dot_general(dzb, k_blk, (((1,), (0,)), ((), ())), preferred_element_type=F32))
                dk_ref[cols, rows] += jnp.dot(q_ts[hh], dzb, preferred_element_type=F32)
                dv_ref[cols, rows] += jnp.dot(d_ob_ts[hh], ab, preferred_element_type=F32)
                rs.append(r_carry[hh] + (C[:, 0:1] + L[:, 0:1]))
                gs.append(g_carry[hh] + SI[:, 0:1])
            return tuple(dqs), tuple(rs), tuple(gs)

        zero = (jnp.zeros((tq, 1), F32),) * HP
        dq_d, r_d, g_d = block(qi, zero, zero, True)
        dq_p, r0, g0 = block(jnp.maximum(qi - 1, 0), r_d, g_d, False, valid=qi > 0)
        dq0 = tuple(a + b for a, b in zip(dq_d, dq_p))

        def cond(st):
            kb, _, _, _, rmax = st
            return jnp.logical_and(kb >= 0, rmax > SKIP_LOG)

        def step(st):
            kb, dq, r, g, _ = st
            dq_part, r2, g2 = block(kb, r, g, False)
            return kb - 1, tuple(a + b for a, b in zip(dq, dq_part)), r2, g2, _max_all(r2)

        _, dqs, _, _, _ = lax.while_loop(cond, step, (qi - 2, dq0, r0, g0, _max_all(r0)))
        for hh, cols in enumerate(heads):
            dq_ref[:, cols] = dqs[hh]

    blk = pl.BlockSpec((tq, WP), lambda h, i: (i, h))
    full = pl.BlockSpec((S, WP), lambda h, i: (0, h))
    vec = pl.BlockSpec((1, WP), lambda h, i: (0, h))
    cst = pl.BlockSpec((tq, tq), lambda h, i: (0, 0))
    act = jax.ShapeDtypeStruct((S, W), F32)
    act_t = jax.ShapeDtypeStruct((W, S), F32)
    full_t = pl.BlockSpec((WP, S), lambda h, i: (h, 0))
    return _pcall(
        body, grid=(n_heads // HP, NQ),
        in_specs=[blk, full, full, blk,
                  pl.BlockSpec((tq, WP), lambda h, i: (i, d_on_col0 // HP + h)), vec, cst, cst],
        out_specs=(blk, full_t, full_t, vec),
        out_shape=(act, act_t, act_t, jax.ShapeDtypeStruct((1, W), F32)),
        semantics=("parallel", "arbitrary"), name=name,
        args=(qh, kh, vh, o_pre, d_on, out_g, u_strict, u_incl), side=side)


def _sb_pre_bwd(proj, dqh, dkh, dvh, q_g, k_g, *, n_heads, col0, tm, name):
    S = proj.shape[0]
    H = n_heads
    W = H * HEAD

    def body(q_ref, k_ref, dqh_ref, dkh_ref, dvh_ref, qg_ref, kg_ref,
             dq_ref, dk_ref, dv_ref, dqg_ref, dkg_ref):
        first = jnp.logical_and(pl.program_id(0) == 0, pl.program_id(1) == 0)

        @pl.when(first)
        def _():
            dqg_ref[...] = jnp.zeros_like(dqg_ref)
            dkg_ref[...] = jnp.zeros_like(dkg_ref)

        for src, dh, g_ref, dst, dg_ref in ((q_ref, dqh_ref[...], qg_ref, dq_ref, dqg_ref),
                                            (k_ref, dkh_ref[...].T, kg_ref, dk_ref, dkg_ref)):
            xv = src[...]
            rstd = lax.rsqrt(jnp.mean(xv * xv, axis=-1, keepdims=True) + EPS)
            n = xv * rstd
            dg_ref[...] += jnp.sum(dh * n, axis=0, keepdims=True)
            dn = dh * g_ref[...]
            dst[...] = (rstd * (dn - n * jnp.mean(dn * n, axis=-1, keepdims=True))).astype(BF16)
        dv_ref[...] = dvh_ref[...].T.astype(BF16)

    def col(group):
        return pl.BlockSpec((tm, HEAD), lambda i, h: (i, col0 + group * H + h))

    vec = pl.BlockSpec((1, HEAD), lambda i, h: (0, 0))
    own = pl.BlockSpec((tm, HEAD), lambda i, h: (i, h))
    own_t = pl.BlockSpec((HEAD, tm), lambda i, h: (h, i))
    act = jax.ShapeDtypeStruct((S, W), BF16)
    vec_shape = jax.ShapeDtypeStruct((1, HEAD), F32)
    return pl.pallas_call(
        body, grid=(S // tm, H), in_specs=[col(0), col(1), own, own_t, own_t, vec, vec],
        out_specs=(own, own, own, vec, vec), out_shape=(act, act, act, vec_shape, vec_shape),
        compiler_params=_params(("arbitrary", "arbitrary")), name=name,
    )(proj, proj, dqh, dkh, dvh, q_g, k_g)


def _softmax_rows(x_ref, L):
    rows = [x_ref[l:l + 1, :] for l in range(L)]
    mx = rows[0]
    for r in rows[1:]:
        mx = jnp.maximum(mx, r)
    ex = [jnp.exp(r - mx) for r in rows]
    tot = ex[0]
    for e in ex[1:]:
        tot = tot + e
    return [e / tot for e in ex]


def _lb_fwd(logits, *, name):
    L, W = logits.shape

    def body(x_ref, o_ref):
        s = _softmax_rows(x_ref, L)
        run = jnp.zeros((1, W), F32)
        for l in range(L):
            run = run + s[l]
            o_ref[l:l + 1, :] = run - s[0]

    return pl.pallas_call(body, out_shape=jax.ShapeDtypeStruct((L, W), F32), name=name)(logits)


def _lb_bwd(logits, dlb_parts, *, name):
    L, W = logits.shape
    P = dlb_parts.shape[0]

    def body(x_ref, d_ref, o_ref):
        s = _softmax_rows(x_ref, L)
        dlb = []
        for l in range(L):
            t = d_ref[0, l:l + 1, :]
            for q in range(1, P):
                t = t + d_ref[q, l:l + 1, :]
            dlb.append(t)
        ds = [None] * L
        run = jnp.zeros((1, W), F32)
        for j in reversed(range(L)):
            run = run + dlb[j]
            ds[j] = run
        ds[0] = jnp.zeros((1, W), F32)
        inner = jnp.zeros((1, W), F32)
        for j in range(L):
            inner = inner + s[j] * ds[j]
        for j in range(L):
            o_ref[j:j + 1, :] = s[j] * (ds[j] - inner)

    return pl.pallas_call(body, out_shape=jax.ShapeDtypeStruct((L, W), F32), name=name)(logits, dlb_parts)


def _ada_mod(c_all, w_ada, *, nb, name):
    L, D, n = w_ada.shape
    B = c_all.shape[0]

    def body(c_ref, w_ref, o_ref, cond_ref):
        cv = c_ref[...]
        s, _ = _sigmoid_pair(cv)
        cond = cv * s
        cond_ref[...] = cond
        o_ref[...] = _bdot(cond, w_ref[...])

    return pl.pallas_call(
        body, grid=(L, n // nb),
        in_specs=[pl.BlockSpec((B, D), lambda l, j: (0, 0)),
                  pl.BlockSpec((None, D, nb), lambda l, j: (l, 0, j))],
        out_specs=(pl.BlockSpec((None, B, nb), lambda l, j: (l, 0, j)),
                   pl.BlockSpec((B, D), lambda l, j: (0, 0))),
        out_shape=(jax.ShapeDtypeStruct((L, B, n), F32), jax.ShapeDtypeStruct((B, D), F32)),
        compiler_params=_params(("arbitrary", "arbitrary")), name=name)(c_all, w_ada)


def _adam_math(w, g, m, v):
    m2 = ADAM_B1 * m + (1.0 - ADAM_B1) * g
    v2 = ADAM_B2 * v + (1.0 - ADAM_B2) * (g * g)
    m_hat = m2 / (1.0 - ADAM_B1 ** ADAM_STEP)
    v_hat = v2 / (1.0 - ADAM_B2 ** ADAM_STEP)
    delta = -ADAM_LR * (m_hat / (jnp.sqrt(v_hat) + ADAM_EPS) + ADAM_WD * w)
    return delta, m2, v2


def _adamw(w, m, v, gparts, *, tr, name):
    R, C = w.shape
    P = gparts.shape[0]

    def body(w_ref, m_ref, v_ref, gp_ref, g_ref, d_ref, m2_ref, v2_ref):
        g = gp_ref[0].astype(F32)
        for p in range(1, P):
            g = g + gp_ref[p].astype(F32)
        delta, m2, v2 = _adam_math(w_ref[...], g, m_ref[...], v_ref[...])
        g_ref[...] = g
        d_ref[...] = delta
        m2_ref[...] = m2
        v2_ref[...] = v2

    tile = pl.BlockSpec((tr, C), lambda i: (i, 0))
    shp = jax.ShapeDtypeStruct((R, C), F32)
    return pl.pallas_call(
        body, grid=(R // tr,),
        in_specs=[tile, tile, tile, pl.BlockSpec((P, tr, C), lambda i: (0, i, 0))],
        out_specs=(tile, tile, tile, tile), out_shape=(shp, shp, shp, shp),
        compiler_params=_params(("parallel",)), name=name)(w, m, v, gparts)


def _adamw_layers(w, m, v, gparts, *, tr, name, side=None):
    L, R, C = w.shape
    P = gparts[0].shape[0]
    nblk = R // tr

    def body(*refs):
        w_ref, m_ref, v_ref = refs[:3]
        gp_refs = refs[3:3 + L]
        g_ref, d_ref, m2_ref, v2_ref = refs[3 + L:]
        layer = pl.program_id(0)
        for t in range(L):
            @pl.when(layer == t)
            def _(t=t):
                g = gp_refs[t][0].astype(F32)
                for q in range(1, P):
                    g = g + gp_refs[t][q].astype(F32)
                delta, m2, v2 = _adam_math(w_ref[...], g, m_ref[...], v_ref[...])
                g_ref[...] = g
                d_ref[...] = delta
                m2_ref[...] = m2
                v2_ref[...] = v2

    def gp_spec(t):
        def index(l, i):
            return (0, jnp.where(l == t, i, jnp.where(l < t, 0, nblk - 1)), 0)
        return pl.BlockSpec((P, tr, C), index)

    tile = pl.BlockSpec((None, tr, C), lambda l, i: (l, i, 0))
    shp = jax.ShapeDtypeStruct((L, R, C), F32)
    return _pcall(
        body, grid=(L, nblk), in_specs=[tile, tile, tile] + [gp_spec(t) for t in range(L)],
        out_specs=(tile, tile, tile, tile), out_shape=(shp, shp, shp, shp),
        semantics=("arbitrary", "arbitrary"), name=name, args=(w, m, v, *gparts), side=side)


def _adamw_ada(w, m, v, cond_t, dmod, *, tr, name):
    L, D, n = w.shape
    Bp = cond_t.shape[1]

    def body(w_ref, m_ref, v_ref, c_ref, dm_ref, g_ref, d_ref, m2_ref, v2_ref):
        g = _bdot(c_ref[...], dm_ref[...])
        delta, m2, v2 = _adam_math(w_ref[...], g, m_ref[...], v_ref[...])
        g_ref[...] = g
        d_ref[...] = delta
        m2_ref[...] = m2
        v2_ref[...] = v2

    tile = pl.BlockSpec((None, tr, n), lambda l, i: (l, i, 0))
    shp = jax.ShapeDtypeStruct((L, D, n), F32)
    return pl.pallas_call(
        body, grid=(L, D // tr),
        in_specs=[tile, tile, tile, pl.BlockSpec((tr, Bp), lambda l, i: (i, 0)),
                  pl.BlockSpec((None, Bp, n), lambda l, i: (l, 0, 0))],
        out_specs=(tile, tile, tile, tile), out_shape=(shp, shp, shp, shp),
        compiler_params=_params(("parallel", "parallel")), name=name)(w, m, v, cond_t, dmod)


def _allgather_small(block, *, name):
    R, C = block.shape

    def body(x_ref, out_ref, send_sems, recv_sems, local_sem):
        x, y, c = lax.axis_index("x"), lax.axis_index("y"), lax.axis_index("c")

        def rows(px, py, pc):
            return out_ref.at[pl.ds((4 * px + 2 * py + pc) * R, R), :]

        mine = pltpu.make_async_copy(x_ref, rows(x, y, c), local_sem)
        mine.start()
        sends = []
        for rel in range(1, N_DEV):
            to = _peer(x, y, c, rel)
            cp = pltpu.make_async_remote_copy(src_ref=x_ref, dst_ref=rows(x, y, c),
                                              send_sem=send_sems.at[rel - 1], recv_sem=recv_sems.at[rel - 1],
                                              device_id=to, device_id_type=MESH)
            cp.start()
            sends.append(cp)
        for rel in range(1, N_DEV):
            frm = _peer(x, y, c, rel)
            pltpu.make_async_remote_copy(src_ref=x_ref, dst_ref=rows(*frm),
                                         send_sem=send_sems.at[rel - 1], recv_sem=recv_sems.at[rel - 1],
                                         device_id=frm, device_id_type=MESH).wait_recv()
        for cp in sends:
            cp.wait_send()
        mine.wait()

    return pl.pallas_call(
        body, out_shape=jax.ShapeDtypeStruct((N_DEV * R, C), block.dtype),
        in_specs=[pl.BlockSpec(memory_space=pltpu.VMEM)],
        out_specs=pl.BlockSpec(memory_space=pltpu.VMEM),
        scratch_shapes=[pltpu.SemaphoreType.DMA((N_DEV - 1,)), pltpu.SemaphoreType.DMA((N_DEV - 1,)),
                        pltpu.SemaphoreType.DMA],
        compiler_params=pltpu.CompilerParams(vmem_limit_bytes=V7X_VMEM_LIMIT), name=name)(block)


def _allgather_hbm(shards, *, name):
    n = len(shards)

    def body(*refs):
        ins = refs[:n]
        outs = refs[n:2 * n]
        send_sems, recv_sems, local_sems = refs[2 * n:]
        x, y, c = lax.axis_index("x"), lax.axis_index("y"), lax.axis_index("c")
        sibling = (x, y, 1 - c)
        chips = [(1 - x, y), (x, 1 - y), (1 - x, 1 - y)]

        def slot(t, px, py, pc):
            return outs[t].at[4 * px + 2 * py + pc]

        def copy(t, k, block, to, src=None):
            return pltpu.make_async_remote_copy(
                src_ref=slot(t, *block) if src is None else src, dst_ref=slot(t, *block),
                send_sem=send_sems.at[t * 7 + k], recv_sem=recv_sems.at[t * 7 + k],
                device_id=to, device_id_type=MESH)

        me = (x, y, c)
        started = []
        mine = []
        for t in range(n):
            cp = pltpu.make_async_copy(ins[t], slot(t, *me), local_sems.at[t])
            cp.start()
            mine.append(cp)
            first = [copy(t, 0, me, sibling, src=ins[t])]
            first += [copy(t, 1 + j, me, (*chip, c), src=ins[t]) for j, chip in enumerate(chips)]
            for cp in first:
                cp.start()
            started += first
        for t in range(n):
            for j, chip in enumerate(chips):
                copy(t, 1 + j, (*chip, c), me).wait_recv()
                fwd = copy(t, 4 + j, (*chip, c), sibling)
                fwd.start()
                started.append(fwd)
        for t in range(n):
            copy(t, 0, sibling, me).wait_recv()
            for j, chip in enumerate(chips):
                copy(t, 4 + j, (*chip, 1 - c), me).wait_recv()
        for cp in started:
            cp.wait_send()
        for cp in mine:
            cp.wait()

    any_spec = pl.BlockSpec(memory_space=pl.ANY)
    return pl.pallas_call(
        body, out_shape=[jax.ShapeDtypeStruct((N_DEV,) + s.shape, s.dtype) for s in shards],
        in_specs=[any_spec] * n, out_specs=[any_spec] * n,
        scratch_shapes=[pltpu.SemaphoreType.DMA((7 * n,)), pltpu.SemaphoreType.DMA((7 * n,)),
                        pltpu.SemaphoreType.DMA((n,))],
        name=name)(*shards)


def _tile(total, want):
    step = 128 if total % 128 == 0 else 8
    best = step
    t = step
    while t <= min(total, want):
        if total % t == 0:
            best = t
        t += step
    return best


def _local_step(x, target, mods, lbs, p, wg, shards=None):
    S, D = x.shape
    L = mods.shape[0]
    W = D // 2
    H = W // HEAD
    mesh = shards is not None
    F = shards["w_ffn_out"][0].shape[0] * N_DEV if mesh else wg["w_ffn_out"][0].shape[0]
    tm = _tile(S, 512)
    tm_big = _tile(S, 1024)
    tm_tn = _tile(S, 2048)
    tm_sw = _tile(S, 128)
    tq = _tile(S, 256)
    cg = max(1, min(32, S // CHUNK))
    nb_out = _tile(D, 1024)
    kb_f = _tile(F, 1408)

    def row(a, l):
        return a[l][None, :]

    first_plan = dict(proj=[("w_out", 0), ("w_ffn_in", 0)], hg=[("w_ffn_out", 0)],
                      sb=[("w_in", 1), ("w_out", 1)], ffn=[("w_ffn_in", 1), ("w_ffn_out", 1)])

    def gather_plan(l, call):
        if not mesh:
            return []
        if l == 0:
            plan = first_plan[call]
        else:
            plan = [(dict(proj="w_in", hg="w_out", sb="w_ffn_out", ffn="w_ffn_in")[call], l + 1)]
        return [(k, j) for k, j in plan if j < L]

    def gather_of(plan):
        return (True, [shards[k][j] for k, j in plan]) if plan else None

    def store_gathered(plan, got):
        for (k, j), g in zip(plan, got or []):
            wg[k][j] = g.reshape(-1, D) if k in ("w_out", "w_ffn_out") else g

    def scatter_of(blocks):
        if mesh and blocks is not None:
            return (False, [b.reshape((N_DEV, -1) + b.shape[-1:]) if b.ndim == 2 else b for b in blocks])
        return None

    saved = []
    xcur = x
    for l in range(L):
        mod = mods[l]
        sh1, sc1, g1, sh2, sc2, g2 = [mod[:, i * D:(i + 1) * D] for i in range(N_MOD)]
        h1 = _modnorm_fwd(xcur, row(p["norm1_g"], l), sc1, sh1, tm=tm, name="norm1_fwd")
        plan = gather_plan(l, "proj")
        proj, got = _mm_nn(h1, wg["w_in"][l], tm=tm_big, name="proj_fwd", side=gather_of(plan))
        store_gathered(plan, got)
        lb = lbs[l][None, :]
        plan = gather_plan(l, "hg")
        (o_hg, on_hg, states), got = _hg_fwd(proj, lb, row(p["hg_out_g"], l), n_heads=H, cg=cg, name="hgrn2_fwd",
                                             side=gather_of(plan))
        store_gathered(plan, got)
        qh, kh, vh = _sb_pre(proj, row(p["sb_q_g"], l), row(p["sb_k_g"], l), n_heads=H, col0=4 * H,
                             tm=tm_tn, name="sb_qknorm_fwd")
        plan = gather_plan(l, "sb")
        (o_sb, on_sb), got = _sb_fwd(qh, kh, vh, row(p["sb_out_g"], l), n_heads=H, tq=tq, name="sb_fwd",
                                     side=gather_of(plan))
        store_gathered(plan, got)
        o_cat = jnp.concatenate([on_hg, on_sb], axis=1)
        (x1, mixed), _ = _mm_nn(o_cat, wg["w_out"][l], tm=tm_big, nb=nb_out, resid=xcur, gate=g1,
                                name="out_proj_fwd")
        h2 = _modnorm_fwd(x1, row(p["norm2_g"], l), sc2, sh2, tm=tm, name="norm2_fwd")
        w_fin = wg["w_ffn_in"][l]
        if w_fin.shape[0] % 2:
            w_fin = jnp.stack([w_fin[0][:, :F], w_fin[0][:, F:]])
        plan = gather_plan(l, "ffn")
        (gate, up, a), got = _ffn_in_fwd(h2, w_fin, tm=tm, nb=F, name="ffn_in_fwd", side=gather_of(plan))
        store_gathered(plan, got)
        (x2, ffn), _ = _mm_nn(a, wg["w_ffn_out"][l], tm=tm, nb=nb_out // 2, resid=x1, gate=g2, name="ffn_out_fwd")
        saved.append(dict(x=xcur, h1=h1, proj=proj, o_hg=o_hg, o_sb=o_sb, states=states, qh=qh, kh=kh, vh=vh,
                          o_cat=o_cat, mixed=mixed, x1=x1, h2=h2, gate=gate, up=up, a=a, ffn=ffn, lb=lb,
                          sc1=sc1, g1=g1, sc2=sc2, g2=g2, w_fin=w_fin))
        xcur = x2

    last = saved[-1]
    dx, dffn, dg2, loss = _loss_bwd(xcur, target, last["ffn"], last["g2"], tm=tm, name="loss_bwd")

    big = {k: [None] * L for k in ("w_in", "w_out", "w_ffn_in", "w_ffn_out")}
    small = {k: [None] * L for k in ("norm1_g", "hg_lb", "hg_out_g", "sb_q_g", "sb_k_g", "sb_out_g", "norm2_g")}
    dmods = [None] * L
    pending_in = None
    for l in reversed(range(L)):
        sv = saved[l]
        (dgate, dup), got = _ffn_out_bwd_x(dffn, wg["w_ffn_out"][l], sv["gate"], sv["up"], tm=tm, kb=kb_f,
                                           name="ffn_out_bwd_x", side=scatter_of(pending_in))
        if got is not None:
            big["w_in"][l + 1] = got[0]
        g_fout, _ = _mm_tn(sv["a"], dffn, tm=tm_tn, kb=kb_f, nb=nb_out, blocked=False, name="ffn_out_bwd_w")
        dh2, got = _mm_nt(dgate, sv["w_fin"], dy2=dup, tm=tm_big, kb=D, name="ffn_in_bwd_x",
                          side=scatter_of([g_fout]))
        big["w_ffn_out"][l] = g_fout if got is None else got[0]
        g_fin, _ = _mm_tn(sv["h2"], dgate, dy2=dup, tm=tm_tn, kb=_tile(D, 1024), nb=sv["w_fin"].shape[2],
                          blocked=True, name="ffn_in_bwd_w")
        dx1, dmixed, dg1, dsh2, dsc2, dn2 = _modnorm_bwd(
            sv["x1"], dh2, dx, row(p["norm2_g"], l), sv["sc2"], sv["mixed"], sv["g1"], tm=tm_sw * 2,
            name="norm2_bwd")
        small["norm2_g"][l] = dn2
        d_ocat, _ = _mm_nt(dmixed, wg["w_out"][l], tm=tm_big, kb=nb_out, nb=D, name="out_proj_bwd_x")
        g_out, _ = _mm_tn(sv["o_cat"], dmixed, tm=tm_tn, kb=D, nb=nb_out, blocked=False, name="out_proj_bwd_w")
        (dhq, dhf, dhi, dhg, dlb, dhog), _ = _hg_bwd(sv["proj"], sv["o_hg"], d_ocat, 0, sv["states"], sv["lb"],
                                                     row(p["hg_out_g"], l), n_heads=H, cg=cg, name="hgrn2_bwd")
        (dqh, dkh, dvh, dsog), got = _sb_bwd(sv["qh"], sv["kh"], sv["vh"], sv["o_sb"], d_ocat, H,
                                             row(p["sb_out_g"], l), n_heads=H, tq=tq, name="sb_bwd",
                                             side=scatter_of([g_fin]))
        big["w_ffn_in"][l] = g_fin if got is None else got[0]
        dsq, dsk, dsv, dqg, dkg = _sb_pre_bwd(sv["proj"], dqh, dkh, dvh, row(p["sb_q_g"], l),
                                              row(p["sb_k_g"], l), n_heads=H, col0=4 * H, tm=tm_tn,
                                              name="sb_qknorm_bwd")
        small["hg_lb"][l] = dlb
        small["hg_out_g"][l] = dhog
        small["sb_out_g"][l] = dsog
        small["sb_q_g"][l] = dqg
        small["sb_k_g"][l] = dkg
        dproj = jnp.concatenate([dhq, dhf, dhi, dhg, dsq, dsk, dsv], axis=1)
        g_in, _ = _mm_tn(sv["h1"], dproj, tm=tm_tn, kb=D, nb=wg["w_in"][l].shape[2], blocked=True,
                         name="proj_bwd_w")
        big["w_in"][l] = g_in
        dh1, got = _mm_nt(dproj, wg["w_in"][l], tm=tm_big, kb=D, name="proj_bwd_x",
                          side=scatter_of([g_out, g_in] if l == 0 else [g_out]))
        if got is not None:
            big["w_out"][l] = got[0]
            if l == 0:
                big["w_in"][l] = got[1]
        else:
            big["w_out"][l] = g_out
        pending_in = [g_in]
        if l > 0:
            prev = saved[l - 1]
            dx0, dffn_prev, dg2_prev, dsh1, dsc1, dn1 = _modnorm_bwd(
                sv["x"], dh1, dx1, row(p["norm1_g"], l), sv["sc1"], prev["ffn"], prev["g2"], tm=tm_sw * 2,
                name="norm1_bwd")
        else:
            dx0, dsh1, dsc1, dn1 = _modnorm_bwd(sv["x"], dh1, dx1, row(p["norm1_g"], l), sv["sc1"], None, None,
                                                tm=tm_sw * 2, name="norm1_bwd_first")
            dffn_prev, dg2_prev = None, None
        small["norm1_g"][l] = dn1
        dmods[l] = jnp.concatenate([dsh1, dsc1, dg1, dsh2, dsc2, dg2], axis=1)
        dx, dffn, dg2 = dx0, dffn_prev, dg2_prev
    return loss, dx, big, small, dmods


def kernel(x, c, norm1_g, w_in, hg_lb_logits, hg_out_g, sb_q_g, sb_k_g, sb_out_g, w_out, norm2_g, w_ffn_in, w_ffn_out, w_ada, b_ada, loss_target, m_norm1_g, m_w_in, m_hg_lb_logits, m_hg_out_g, m_sb_q_g, m_sb_k_g, m_sb_out_g, m_w_out, m_norm2_g, m_w_ffn_in, m_w_ffn_out, m_w_ada, m_b_ada, v_norm1_g, v_w_in, v_hg_lb_logits, v_hg_out_g, v_sb_q_g, v_sb_k_g, v_sb_out_g, v_w_out, v_norm2_g, v_w_ffn_in, v_w_ffn_out, v_w_ada, v_b_ada):
    L, D = norm1_g.shape
    S = x.shape[1]
    me = 4 * lax.axis_index("x") + 2 * lax.axis_index("y") + lax.axis_index("c")

    c_all = _allgather_small(jnp.broadcast_to(c, (8, D)), name="gather_c").reshape(N_DEV, 8, D)[:, 0, :]
    n_ada = w_ada.shape[2]
    mod_cols, cond = _ada_mod(c_all, w_ada, nb=_tile(n_ada, 512), name="ada_mod")
    mod_all = _allgather_small(mod_cols.reshape(L * N_DEV, n_ada), name="gather_mod")
    mod_all = mod_all.reshape(N_DEV, L, N_DEV, n_ada)
    mod_mine = lax.dynamic_index_in_dim(mod_all, me, axis=2, keepdims=False)
    mods = jnp.transpose(mod_mine, (1, 0, 2)).reshape(L, 1, N_DEV * n_ada) + b_ada[:, None, :]

    lbs = _lb_fwd(hg_lb_logits, name="lower_bounds_fwd")

    shards = dict(w_in=[w_in[l].astype(BF16) for l in range(L)], w_out=[w_out[l].astype(BF16) for l in range(L)],
                  w_ffn_in=[w_ffn_in[l].astype(BF16) for l in range(L)],
                  w_ffn_out=[w_ffn_out[l].astype(BF16) for l in range(L)])
    g_in, = _allgather_hbm([shards["w_in"][0]], name="gather_first_weight")
    wg = dict(w_in=[g_in] + [None] * (L - 1), w_out=[None] * L, w_ffn_in=[None] * L, w_ffn_out=[None] * L)

    p = dict(norm1_g=norm1_g, hg_out_g=hg_out_g, sb_q_g=sb_q_g, sb_k_g=sb_k_g, sb_out_g=sb_out_g,
             norm2_g=norm2_g)
    loss_part, grad_x, recv, small, dmods = _local_step(x.reshape(S, D), loss_target.reshape(S, D), mods, lbs, p,
                                                        wg, shards)

    dmod = jnp.concatenate(dmods, axis=0)
    pieces = [jnp.concatenate(small[k], axis=0) for k in
              ("norm1_g", "hg_lb", "hg_out_g", "sb_q_g", "sb_k_g", "sb_out_g", "norm2_g")] + [dmod]
    flat = jnp.concatenate([a.reshape(-1) for a in pieces] + [loss_part.reshape(-1)])
    n_flat = flat.shape[0]
    rows = -(-n_flat // 1024) * 8
    flat = jnp.pad(flat, (0, rows * 128 - n_flat)).reshape(rows, 128)
    gathered = _allgather_small(flat, name="gather_small_grads").reshape(N_DEV, rows * 128)

    def take(off, shape):
        size = 1
        for s in shape:
            size *= s
        return gathered[:, off:off + size].reshape((N_DEV,) + tuple(shape)), off + size

    off = 0
    parts = {}
    for k, a in zip(("norm1_g", "hg_lb", "hg_out_g", "sb_q_g", "sb_k_g", "sb_out_g", "norm2_g", "dmod"), pieces):
        parts[k], off = take(off, a.shape)
    loss_parts = gathered[:, off:off + 1]
    loss = jnp.sum(loss_parts)

    def pad8(a):
        return jnp.pad(a, ((0, 0), (0, 8 - a.shape[1]), (0, 0)))

    def small_update(w, m, v, gparts):
        Lw = w.shape[0]
        g, d, m2, v2 = _adamw(pad8(w[None])[0], pad8(m[None])[0], pad8(v[None])[0], pad8(gparts),
                              tr=8, name="adamw_small")
        return g[:Lw], d[:Lw], m2[:Lw], v2[:Lw]

    out = {}
    out["norm1_g"] = small_update(norm1_g, m_norm1_g, v_norm1_g, parts["norm1_g"])
    dlogits = _lb_bwd(hg_lb_logits, parts["hg_lb"], name="lower_bounds_bwd")
    out["hg_lb_logits"] = small_update(hg_lb_logits, m_hg_lb_logits, v_hg_lb_logits, dlogits[None])
    out["hg_out_g"] = small_update(hg_out_g, m_hg_out_g, v_hg_out_g, parts["hg_out_g"])
    out["sb_q_g"] = small_update(sb_q_g, m_sb_q_g, v_sb_q_g, parts["sb_q_g"])
    out["sb_k_g"] = small_update(sb_k_g, m_sb_k_g, v_sb_k_g, parts["sb_k_g"])
    out["sb_out_g"] = small_update(sb_out_g, m_sb_out_g, v_sb_out_g, parts["sb_out_g"])
    out["norm2_g"] = small_update(norm2_g, m_norm2_g, v_norm2_g, parts["norm2_g"])
    out["b_ada"] = small_update(b_ada, m_b_ada, v_b_ada, parts["dmod"])

    dmod_all = parts["dmod"].reshape(N_DEV, L, N_DEV, n_ada)
    dmod_mine = lax.dynamic_index_in_dim(dmod_all, me, axis=2, keepdims=False)
    dmod_mine = jnp.pad(jnp.transpose(dmod_mine, (1, 0, 2)), ((0, 0), (0, 128 - N_DEV), (0, 0)))
    cond_t = jnp.pad(jnp.transpose(cond), ((0, 0), (0, 128 - N_DEV)))
    out["w_ada"] = _adamw_ada(w_ada, m_w_ada, v_w_ada, cond_t, dmod_mine, tr=_tile(D, 256), name="adamw_ada")

    def big_update(w, m, v, recv_l, name):
        return _adamw_layers(w, m, v, recv_l, tr=_tile(w.shape[1], 131072 // w.shape[2]), name=name)[0]

    out["w_ffn_in"] = big_update(w_ffn_in, m_w_ffn_in, v_w_ffn_in, recv["w_ffn_in"], "adamw_w_ffn_in")
    out["w_ffn_out"] = big_update(w_ffn_out, m_w_ffn_out, v_w_ffn_out, recv["w_ffn_out"], "adamw_w_ffn_out")
    out["w_out"] = big_update(w_out, m_w_out, v_w_out, recv["w_out"], "adamw_w_out")
    out["w_in"] = big_update(w_in, m_w_in, v_w_in, recv["w_in"], "adamw_w_in")

    order = ("norm1_g", "w_in", "hg_lb_logits", "hg_out_g", "sb_q_g", "sb_k_g", "sb_out_g", "w_out", "norm2_g",
             "w_ffn_in", "w_ffn_out", "w_ada", "b_ada")
    grads = [out[k][0] for k in order]
    deltas = [out[k][1] for k in order]
    new_m = [out[k][2] for k in order]
    new_v = [out[k][3] for k in order]
    return (loss, grad_x.reshape(1, S, D), *grads, *deltas, *new_m, *new_v)
```

```python
import functools

import jax
import jax.numpy as jnp
from jax import lax
from jax.experimental import pallas as pl
from jax.experimental.pallas import tpu as pltpu

F32 = jnp.float32
BF16 = jnp.bfloat16
MESH = pl.DeviceIdType.MESH

N_DEV = 8
HEAD = 128
CHUNK = 64
N_MOD = 6
EPS = 1e-6
TINY = 1e-30
ADAM_LR = 0.001
ADAM_B1 = 0.9
ADAM_B2 = 0.999
ADAM_EPS = 1e-08
ADAM_WD = 0.01
ADAM_STEP = 10
V7X_VMEM_LIMIT = 56 * 1024 * 1024
SKIP_LOG = -104.0


def _params(sem):
    return pltpu.CompilerParams(dimension_semantics=sem, vmem_limit_bytes=V7X_VMEM_LIMIT)


def _bdot(a, b, dims=(((1,), (0,)), ((), ()))):
    return lax.dot_general(a.astype(BF16), b.astype(BF16), dims, preferred_element_type=F32)


_NT = (((1,), (1,)), ((), ()))
_TN = (((0,), (0,)), ((), ()))


def _sigmoid_pair(x):
    e = jnp.exp(-jnp.abs(x))
    r = 1.0 / (1.0 + e)
    er = e * r
    pos = x >= 0
    return jnp.where(pos, r, er), jnp.where(pos, er, r)


def _split_dot(x, u, parts):
    acc = None
    rem = x
    for _ in range(parts):
        p = rem.astype(BF16)
        rem = rem - p.astype(F32)
        t = lax.dot_general(p, u, (((1,), (0,)), ((), ())), preferred_element_type=F32)
        acc = t if acc is None else acc + t
    return acc


def _split_dot_left(u, x, parts):
    acc = None
    rem = x
    for _ in range(parts):
        p = rem.astype(BF16)
        rem = rem - p.astype(F32)
        t = lax.dot_general(u, p, (((1,), (0,)), ((), ())), preferred_element_type=F32)
        acc = t if acc is None else acc + t
    return acc


def _peer(x, y, c, rel):
    return (x ^ ((rel >> 2) & 1), y ^ ((rel >> 1) & 1), c ^ (rel & 1))


def _exchange(gather, srcs, dsts, send_sems, recv_sems, local_sems, phase):
    x, y, c = lax.axis_index("x"), lax.axis_index("y"), lax.axis_index("c")
    me = 4 * x + 2 * y + c
    for t in range(len(srcs)):
        own = srcs[t] if gather else srcs[t].at[me]
        local = pltpu.make_async_copy(own, dsts[t].at[me], local_sems.at[t])
        if phase == "start":
            local.start()
        for rel in range(1, N_DEV):
            px, py, pc = _peer(x, y, c, rel)
            pid = 4 * px + 2 * py + pc
            k = t * (N_DEV - 1) + rel - 1
            if phase == "start":
                pltpu.make_async_remote_copy(
                    src_ref=srcs[t] if gather else srcs[t].at[pid], dst_ref=dsts[t].at[me],
                    send_sem=send_sems.at[k], recv_sem=recv_sems.at[k],
                    device_id=(px, py, pc), device_id_type=MESH).start()
            else:
                cp = pltpu.make_async_remote_copy(
                    src_ref=own, dst_ref=dsts[t].at[pid], send_sem=send_sems.at[k], recv_sem=recv_sems.at[k],
                    device_id=(px, py, pc), device_id_type=MESH)
                cp.wait_recv()
                cp.wait_send()
        if phase == "wait":
            local.wait()


def _gather_two_level(srcs, dsts, send_sems, recv_sems, local_sems, phase):
    x, y, c = lax.axis_index("x"), lax.axis_index("y"), lax.axis_index("c")
    me, sibling = (x, y, c), (x, y, 1 - c)
    chips = [(1 - x, y), (x, 1 - y), (1 - x, 1 - y)]
    per = N_DEV - 1
    for t in range(len(srcs)):
        def slot(px, py, pc, t=t):
            return dsts[t].at[4 * px + 2 * py + pc]

        def copy(k, block, to, src=None, t=t):
            return pltpu.make_async_remote_copy(
                src_ref=slot(*block) if src is None else src, dst_ref=slot(*block),
                send_sem=send_sems.at[t * per + k], recv_sem=recv_sems.at[t * per + k],
                device_id=to, device_id_type=MESH)

        local = pltpu.make_async_copy(srcs[t], slot(*me), local_sems.at[t])
        first = [copy(0, me, sibling, src=srcs[t])]
        first += [copy(1 + j, me, (*chip, c), src=srcs[t]) for j, chip in enumerate(chips)]
        passed = [copy(4 + j, (*chip, c), sibling) for j, chip in enumerate(chips)]
        if phase == "start":
            local.start()
            for cp in first:
                cp.start()
        elif phase == "forward":
            for j, chip in enumerate(chips):
                copy(1 + j, (*chip, c), me).wait_recv()
                passed[j].start()
        else:
            copy(0, sibling, me).wait_recv()
            for j, chip in enumerate(chips):
                copy(4 + j, (*chip, 1 - c), me).wait_recv()
            for cp in first + passed:
                cp.wait_send()
            local.wait()


def _exchange_scratch(n):
    return [pltpu.SemaphoreType.DMA(((N_DEV - 1) * n,)), pltpu.SemaphoreType.DMA(((N_DEV - 1) * n,)),
            pltpu.SemaphoreType.DMA((n,))]


def _pcall(body, *, grid, in_specs, out_specs, out_shape, scratch_shapes=(), semantics, name, args, side=None):
    single = not isinstance(out_shape, (tuple, list))
    if single:
        out_specs, out_shape = [out_specs], [out_shape]
    in_specs, out_specs, out_shape = list(in_specs), list(out_specs), list(out_shape)
    scratch_shapes = list(scratch_shapes)
    n_in, n_out, n_scr = len(in_specs), len(out_specs), len(scratch_shapes)
    if side is None:
        res = pl.pallas_call(body, grid=grid, in_specs=in_specs, out_specs=out_specs, out_shape=out_shape,
                             scratch_shapes=scratch_shapes, compiler_params=_params(semantics), name=name)(*args)
        return (res[0] if single else tuple(res)), None
    gather, srcs = side
    n = len(srcs)

    def full(*refs):
        ins = refs[:n_in]
        s_in = refs[n_in:n_in + n]
        outs = refs[n_in + n:n_in + n + n_out]
        s_out = refs[n_in + n + n_out:n_in + 2 * n + n_out]
        scr = refs[n_in + 2 * n + n_out:n_in + 2 * n + n_out + n_scr]
        send_sems, recv_sems, local_sems = refs[n_in + 2 * n + n_out + n_scr:]
        step = pl.program_id(0)
        steps = grid[0]
        for ax in range(1, len(grid)):
            step = step * grid[ax] + pl.program_id(ax)
            steps *= grid[ax]

        def exchange(phase):
            if gather:
                _gather_two_level(s_in, s_out, send_sems, recv_sems, local_sems, phase)
            elif phase != "forward":
                _exchange(False, s_in, s_out, send_sems, recv_sems, local_sems, phase)

        pl.when(step == 0)(lambda: exchange("start"))
        if gather:
            pl.when(step == (steps * 4) // 5)(lambda: exchange("forward"))
        body(*ins, *outs, *scr)
        pl.when(step == steps - 1)(lambda: exchange("wait"))

    any_spec = pl.BlockSpec(memory_space=pl.ANY)
    s_shapes = [jax.ShapeDtypeStruct(((N_DEV,) + s.shape) if gather else s.shape, s.dtype) for s in srcs]
    res = pl.pallas_call(full, grid=grid, in_specs=in_specs + [any_spec] * n,
                         out_specs=out_specs + [any_spec] * n, out_shape=out_shape + s_shapes,
                         scratch_shapes=scratch_shapes + _exchange_scratch(n),
                         compiler_params=_params(("arbitrary",) * len(grid)), name=name)(*args, *srcs)
    main = res[:n_out]
    return (main[0] if single else tuple(main)), list(res[n_out:])


def _mm_nn(a, b, *, tm, nb=None, out_dtype=F32, resid=None, gate=None, norm=None, name, side=None):
    M, K = a.shape
    if b.ndim == 3:
        NB, _, n = b.shape
        b_spec = pl.BlockSpec((None, K, n), lambda j, i: (j, 0, 0))
    else:
        n = nb
        NB = b.shape[1] // nb
        b_spec = pl.BlockSpec((K, n), lambda j, i: (0, j))
    N = NB * n
    epi = resid is not None
    assert norm is None or (epi and NB == 1)

    def body(*refs):
        if norm is not None:
            a_ref, b_ref, r_ref, g_ref, ng_ref, sc_ref, sh_ref, o_ref, acc_ref, h_ref = refs
        elif epi:
            a_ref, b_ref, r_ref, g_ref, o_ref, acc_ref = refs
        else:
            a_ref, b_ref, o_ref = refs
        acc = jnp.dot(a_ref[...], b_ref[...], preferred_element_type=F32)
        if epi:
            xv = r_ref[...] + g_ref[...] * acc
            o_ref[...] = xv
            acc_ref[...] = acc.astype(BF16)
            if norm is not None:
                rstd = lax.rsqrt(jnp.mean(xv * xv, axis=-1, keepdims=True) + EPS)
                y = (xv * rstd) * ng_ref[...]
                h_ref[...] = (y * (1.0 + sc_ref[...]) + sh_ref[...]).astype(BF16)
        else:
            o_ref[...] = acc.astype(out_dtype)

    in_specs = [pl.BlockSpec((tm, K), lambda j, i: (i, 0)), b_spec]
    args = [a, b]
    o_spec = pl.BlockSpec((tm, n), lambda j, i: (i, j))
    if epi:
        row = pl.BlockSpec((1, n), lambda j, i: (0, j))
        in_specs += [pl.BlockSpec((tm, n), lambda j, i: (i, j)), row]
        args += [resid, gate]
        out_shape = [jax.ShapeDtypeStruct((M, N), F32), jax.ShapeDtypeStruct((M, N), BF16)]
        out_specs = [o_spec, o_spec]
        if norm is not None:
            in_specs += [row, row, row]
            args += list(norm)
            out_shape.append(jax.ShapeDtypeStruct((M, N), BF16))
            out_specs.append(o_spec)
    else:
        out_shape = [jax.ShapeDtypeStruct((M, N), out_dtype)]
        out_specs = [o_spec]
    res, got = _pcall(body, grid=(NB, M // tm), in_specs=in_specs, out_specs=out_specs, out_shape=out_shape,
                      semantics=("parallel", "parallel"), name=name, args=args, side=side)
    return (res[0] if len(res) == 1 else res), got


def _halves(dy, dy2, blk_rows, blk_cols, nblocks, row_of, col_of, last_row=None):
    if dy2 is None:
        return [pl.BlockSpec((blk_rows, blk_cols), lambda *g: (row_of(*g), col_of(*g)))], None
    half = nblocks // 2

    def left(*g):
        r, c = row_of(*g), col_of(*g)
        if last_row is None:
            return (r, jnp.minimum(c, half - 1))
        return (jnp.where(c < half, r, last_row), jnp.minimum(c, half - 1))

    def right(*g):
        r, c = row_of(*g), col_of(*g)
        if last_row is None:
            return (r, jnp.maximum(c - half, 0))
        return (jnp.where(c >= half, r, 0), jnp.maximum(c - half, 0))

    return [pl.BlockSpec((blk_rows, blk_cols), left), pl.BlockSpec((blk_rows, blk_cols), right)], half


def _mm_nt(dy, w, *, tm, kb, nb=None, name, side=None, dy2=None):
    M = dy.shape[0]
    N = dy.shape[1] * (1 if dy2 is None else 2)
    if w.ndim == 3:
        NB, Kt, n = w.shape
        w_spec = pl.BlockSpec((None, kb, n), lambda i, k, j: (j, k, 0))
    else:
        Kt = w.shape[0]
        n = nb
        NB = N // nb
        w_spec = pl.BlockSpec((kb, n), lambda i, k, j: (k, j))
    KB = Kt // kb
    dy_specs, half = _halves(dy, dy2, tm, n, NB, lambda i, k, j: i, lambda i, k, j: j)
    n_op = len(dy_specs)

    def body(*refs):
        dy_refs = refs[:n_op]
        w_ref = refs[n_op]
        o_ref = refs[n_op + 1]
        acc = refs[n_op + 2:]
        j = pl.program_id(2)

        def use(dy_ref):
            part = lax.dot_general(dy_ref[...], w_ref[...], _NT, preferred_element_type=F32)
            if NB == 1:
                o_ref[...] = part
                return
            acc_ref, = acc

            @pl.when(j == 0)
            def _():
                acc_ref[...] = part

            @pl.when(jnp.logical_and(j > 0, j < NB - 1))
            def _():
                acc_ref[...] += part

            @pl.when(j == NB - 1)
            def _():
                o_ref[...] = acc_ref[...] + part

        if half is None:
            use(dy_refs[0])
        else:
            pl.when(j < half)(lambda: use(dy_refs[0]))
            pl.when(j >= half)(lambda: use(dy_refs[1]))

    return _pcall(
        body, grid=(M // tm, KB, NB), in_specs=dy_specs + [w_spec],
        out_specs=pl.BlockSpec((tm, kb), lambda i, k, j: (i, k)),
        out_shape=jax.ShapeDtypeStruct((M, Kt), F32),
        scratch_shapes=[] if NB == 1 else [pltpu.VMEM((tm, kb), F32)],
        semantics=("parallel", "parallel", "arbitrary"), name=name,
        args=(dy, w) if dy2 is None else (dy, dy2, w), side=side)


def _mm_tn(x, dy, *, tm, kb, nb, blocked, name, side=None, dy2=None):
    M, K = x.shape
    N = dy.shape[1] * (1 if dy2 is None else 2)
    KB, NB, MB = K // kb, N // nb, M // tm
    dy_specs, half = _halves(dy, dy2, tm, nb, NB, lambda k, n, m: m, lambda k, n, m: n, last_row=MB - 1)
    n_dy = len(dy_specs)

    def body(*refs):
        x_ref = refs[0]
        dy_refs = refs[1:1 + n_dy]
        o_ref = refs[1 + n_dy]
        acc = refs[2 + n_dy:]
        m = pl.program_id(2)

        def use(dy_ref):
            part = lax.dot_general(x_ref[...], dy_ref[...], _TN, preferred_element_type=F32)
            if MB == 1:
                o_ref[...] = part.astype(BF16)
                return
            acc_ref, = acc

            @pl.when(m == 0)
            def _():
                acc_ref[...] = part

            @pl.when(jnp.logical_and(m > 0, m < MB - 1))
            def _():
                acc_ref[...] += part

            @pl.when(m == MB - 1)
            def _():
                o_ref[...] = (acc_ref[...] + part).astype(BF16)

        if half is None:
            use(dy_refs[0])
        else:
            nblk = pl.program_id(1)
            pl.when(nblk < half)(lambda: use(dy_refs[0]))
            pl.when(nblk >= half)(lambda: use(dy_refs[1]))

    if blocked:
        out_shape = jax.ShapeDtypeStruct((NB, K, nb), BF16)
        o_spec = pl.BlockSpec((None, kb, nb), lambda k, n, m: (n, k, 0))
    else:
        out_shape = jax.ShapeDtypeStruct((K, N), BF16)
        o_spec = pl.BlockSpec((kb, nb), lambda k, n, m: (k, n))
    x_spec = pl.BlockSpec((tm, kb), lambda k, n, m: (m, k))
    return _pcall(
        body, grid=(KB, NB, MB), in_specs=[x_spec] + dy_specs,
        out_specs=o_spec, out_shape=out_shape,
        scratch_shapes=[] if MB == 1 else [pltpu.VMEM((kb, nb), F32)],
        semantics=("parallel", "parallel", "arbitrary"), name=name,
        args=(x, dy) if dy2 is None else (x, dy, dy2), side=side)


def _ffn_in_fwd(h, w, *, tm, nb, name, side=None):
    M, K = h.shape
    if w.ndim == 3:
        J, _, n = w.shape
        half = J // 2
        specs = [pl.BlockSpec((None, K, n), lambda j, i: (j, 0, 0)),
                 pl.BlockSpec((None, K, n), lambda j, i: (j + half, 0, 0))]
    else:
        n = nb
        half = w.shape[1] // (2 * nb)
        specs = [pl.BlockSpec((K, n), lambda j, i: (0, j)), pl.BlockSpec((K, n), lambda j, i: (0, j + half))]
    F = half * n

    def body(h_ref, wg_ref, wu_ref, gate_ref, up_ref, act_ref):
        hv = h_ref[...]
        gate = jnp.dot(hv, wg_ref[...], preferred_element_type=F32)
        up = jnp.dot(hv, wu_ref[...], preferred_element_type=F32)
        s, _ = _sigmoid_pair(gate)
        gate_ref[...] = gate
        up_ref[...] = up
        act_ref[...] = (gate * s * up).astype(BF16)

    o_spec = pl.BlockSpec((tm, n), lambda j, i: (i, j))
    f32 = jax.ShapeDtypeStruct((M, F), F32)
    return _pcall(
        body, grid=(half, M // tm), in_specs=[pl.BlockSpec((tm, K), lambda j, i: (i, 0))] + specs,
        out_specs=(o_spec, o_spec, o_spec), out_shape=(f32, f32, jax.ShapeDtypeStruct((M, F), BF16)),
        semantics=("parallel", "parallel"), name=name, args=(h, w, w), side=side)


def _ffn_out_bwd_x(dy, w, gate, up, *, tm, kb, name, side=None):
    M, D = dy.shape
    F = w.shape[0]

    def body(dy_ref, w_ref, g_ref, u_ref, dg_ref, du_ref):
        da = lax.dot_general(dy_ref[...], w_ref[...], _NT, preferred_element_type=F32)
        gate = g_ref[...]
        s, ns = _sigmoid_pair(gate)
        dg_ref[...] = (da * u_ref[...] * (s * (1.0 + gate * ns))).astype(BF16)
        du_ref[...] = (da * (gate * s)).astype(BF16)

    tile = pl.BlockSpec((tm, kb), lambda i, k: (i, k))
    act = jax.ShapeDtypeStruct((M, F), BF16)
    return _pcall(
        body, grid=(M // tm, F // kb),
        in_specs=[pl.BlockSpec((tm, D), lambda i, k: (i, 0)), pl.BlockSpec((kb, D), lambda i, k: (k, 0)), tile, tile],
        out_specs=(tile, tile), out_shape=(act, act),
        semantics=("parallel", "parallel"), name=name, args=(dy, w, gate, up), side=side)


def _modnorm_fwd(x, gain, sc, sh, *, tm, name):
    S, D = x.shape

    def body(x_ref, g_ref, sc_ref, sh_ref, h_ref):
        xv = x_ref[...]
        rstd = lax.rsqrt(jnp.mean(xv * xv, axis=-1, keepdims=True) + EPS)
        y = (xv * rstd) * g_ref[...]
        h_ref[...] = (y * (1.0 + sc_ref[...]) + sh_ref[...]).astype(BF16)

    row = pl.BlockSpec((1, D), lambda i: (0, 0))
    return pl.pallas_call(
        body, grid=(S // tm,),
        in_specs=[pl.BlockSpec((tm, D), lambda i: (i, 0)), row, row, row],
        out_specs=pl.BlockSpec((tm, D), lambda i: (i, 0)),
        out_shape=jax.ShapeDtypeStruct((S, D), BF16),
        compiler_params=_params(("parallel",)), name=name)(x, gain, sc, sh)


def _modnorm_bwd(x, dh, dres, gain, sc, branch, gate, *, tm, name):
    S, D = x.shape
    has_prev = branch is not None

    def body(*refs):
        if has_prev:
            (x_ref, dh_ref, dr_ref, g_ref, sc_ref, br_ref, gt_ref,
             dx_ref, dbr_ref, dgt_ref, dsh_ref, dsc_ref, dgn_ref) = refs
        else:
            (x_ref, dh_ref, dr_ref, g_ref, sc_ref,
             dx_ref, dsh_ref, dsc_ref, dgn_ref) = refs
        i = pl.program_id(0)
        xv = x_ref[...]
        dh_v = dh_ref[...]
        gv = g_ref[...]
        scale1 = 1.0 + sc_ref[...]
        rstd = lax.rsqrt(jnp.mean(xv * xv, axis=-1, keepdims=True) + EPS)
        n = xv * rstd
        dn = dh_v * (gv * scale1)
        dx = rstd * (dn - n * jnp.mean(dn * n, axis=-1, keepdims=True)) + dr_ref[...]
        dx_ref[...] = dx
        dhn = dh_v * n
        p_sh = jnp.sum(dh_v, axis=0, keepdims=True)
        p_sc = jnp.sum(dhn, axis=0, keepdims=True) * gv
        p_gn = jnp.sum(dhn, axis=0, keepdims=True) * scale1
        if has_prev:
            dbr_ref[...] = (gt_ref[...] * dx).astype(BF16)
            p_gt = jnp.sum(dx * br_ref[...].astype(F32), axis=0, keepdims=True)

        @pl.when(i == 0)
        def _():
            dsh_ref[...] = p_sh
            dsc_ref[...] = p_sc
            dgn_ref[...] = p_gn
            if has_prev:
                dgt_ref[...] = p_gt

        @pl.when(i > 0)
        def _():
            dsh_ref[...] += p_sh
            dsc_ref[...] += p_sc
            dgn_ref[...] += p_gn
            if has_prev:
                dgt_ref[...] += p_gt

    tile = pl.BlockSpec((tm, D), lambda i: (i, 0))
    row = pl.BlockSpec((1, D), lambda i: (0, 0))
    row_shape = jax.ShapeDtypeStruct((1, D), F32)
    if has_prev:
        in_specs = [tile, tile, tile, row, row, tile, row]
        args = (x, dh, dres, gain, sc, branch, gate)
        out_specs = (tile, tile, row, row, row, row)
        out_shape = (jax.ShapeDtypeStruct((S, D), F32), jax.ShapeDtypeStruct((S, D), BF16),
                     row_shape, row_shape, row_shape, row_shape)
    else:
        in_specs = [tile, tile, tile, row, row]
        args = (x, dh, dres, gain, sc)
        out_specs = (tile, row, row, row)
        out_shape = (jax.ShapeDtypeStruct((S, D), F32), row_shape, row_shape, row_shape)
    return pl.pallas_call(body, grid=(S // tm,), in_specs=in_specs, out_specs=out_specs,
                          out_shape=out_shape, compiler_params=_params(("arbitrary",)),
                          name=name)(*args)


def _loss_bwd(y, target, branch, gate, *, tm, name):
    S, D = y.shape
    nsteps = S // tm

    def body(y_ref, t_ref, br_ref, gt_ref, dy_ref, dbr_ref, dgt_ref, loss_ref, col_ref):
        i = pl.program_id(0)
        diff = y_ref[...] - t_ref[...]
        dy = diff * (1.0 / D)
        dy_ref[...] = dy
        dbr_ref[...] = (gt_ref[...] * dy).astype(BF16)
        p_gt = jnp.sum(dy * br_ref[...].astype(F32), axis=0, keepdims=True)
        p_col = jnp.sum(diff * diff, axis=0, keepdims=True)

        @pl.when(i == 0)
        def _():
            dgt_ref[...] = p_gt
            col_ref[...] = p_col

        @pl.when(i > 0)
        def _():
            dgt_ref[...] += p_gt
            col_ref[...] += p_col

        @pl.when(i == nsteps - 1)
        def _():
            tot = jnp.sum(col_ref[...], axis=-1, keepdims=True) * (0.5 / D)
            loss_ref[...] = jnp.broadcast_to(tot, (1, 128))

    tile = pl.BlockSpec((tm, D), lambda i: (i, 0))
    row = pl.BlockSpec((1, D), lambda i: (0, 0))
    return pl.pallas_call(
        body, grid=(nsteps,), in_specs=[tile, tile, tile, row],
        out_specs=(tile, tile, row, pl.BlockSpec((1, 128), lambda i: (0, 0))),
        out_shape=(jax.ShapeDtypeStruct((S, D), F32), jax.ShapeDtypeStruct((S, D), BF16),
                   jax.ShapeDtypeStruct((1, D), F32), jax.ShapeDtypeStruct((1, 128), F32)),
        scratch_shapes=[pltpu.VMEM((1, D), F32)],
        compiler_params=_params(("arbitrary",)), name=name)(y, target, branch, gate)


def _hg_chunk(q, fl, lbv, tri):
    C = q.shape[0]
    sq, nsq = _sigmoid_pair(q)
    qa = q * sq
    sig, nsig = _sigmoid_pair(fl)
    one_lb = 1.0 - lbv
    f = lbv + one_lb * sig
    fc = jnp.maximum(f, TINY)
    lf = jnp.log(fc)
    k = one_lb * nsig
    b = _split_dot_left(tri, lf, 3)
    row = lax.broadcasted_iota(jnp.int32, b.shape, 0)
    bm = jnp.sum(jnp.where(row == C // 2 - 1, b, 0.0), axis=0, keepdims=True)
    bl = jnp.sum(jnp.where(row == C - 1, b, 0.0), axis=0, keepdims=True)
    eb = jnp.exp(b)
    ebm = jnp.exp(b - bm)
    enbm = jnp.exp(bm - b)
    ebl = jnp.exp(bl - b)
    ebL = jnp.exp(bl)

    def operand(t):
        return t.astype(BF16).astype(F32)

    return dict(sq=sq, nsq=nsq, qa=qa, sig=sig, nsig=nsig, one_lb=one_lb, f=f, fc=fc, k=k,
                eb=eb, ebm=ebm, enbm=enbm, ebl=ebl, ebL=ebL,
                Qm=operand(qa * ebm), Km=operand(k * enbm), Qb=operand(qa * eb), Kh=operand(k * ebl), row=row)


def _causal_incl(C):
    r = lax.broadcasted_iota(jnp.int32, (C, C), 0)
    c = lax.broadcasted_iota(jnp.int32, (C, C), 1)
    return r >= c


def _hg_fwd(proj, lb, out_g, *, n_heads, cg, name, side=None):
    S = proj.shape[0]
    H = n_heads
    W = H * HEAD
    T = cg * CHUNK
    NG = S // T
    tri = jnp.tril(jnp.ones((CHUNK, CHUNK), F32)).astype(BF16)

    def body(q_ref, f_ref, v_ref, g_ref, lb_ref, og_ref, tri_ref, o_ref, on_ref, st_ref, s_scr):
        @pl.when(pl.program_id(1) == 0)
        def _():
            s_scr[...] = jnp.zeros_like(s_scr)

        lbv = lb_ref[...]
        ogv = og_ref[...]
        triv = tri_ref[...]
        mask = _causal_incl(CHUNK)
        for c in range(cg):
            rows = pl.ds(c * CHUNK, CHUNK)
            v = v_ref[rows, :]
            gg = g_ref[rows, :]
            cm = _hg_chunk(q_ref[rows, :], f_ref[rows, :], lbv, triv)
            s0 = s_scr[...]
            st_ref[c] = s0
            A = jnp.where(mask, _bdot(cm["Qm"], cm["Km"], _NT), 0.0)
            o = _bdot(A, v) + _bdot(cm["Qb"], s0, _NT)
            s_scr[...] = s0 * cm["ebL"] + _bdot(v, cm["Kh"], _TN)
            o_ref[rows, :] = o
            rstd = lax.rsqrt(jnp.mean(o * o, axis=-1, keepdims=True) + EPS)
            sg, _ = _sigmoid_pair(gg)
            on_ref[rows, :] = (((o * rstd) * ogv) * (gg * sg)).astype(BF16)

    def col(group):
        return pl.BlockSpec((T, HEAD), lambda h, g: (g, group * H + h))

    vec = pl.BlockSpec((1, HEAD), lambda h, g: (0, h))
    return _pcall(
        body, grid=(H, NG),
        in_specs=[col(0), col(1), col(2), col(3), vec, vec,
                  pl.BlockSpec((CHUNK, CHUNK), lambda h, g: (0, 0))],
        out_specs=(pl.BlockSpec((T, HEAD), lambda h, g: (g, h)),
                   pl.BlockSpec((T, HEAD), lambda h, g: (g, h)),
                   pl.BlockSpec((cg, None, HEAD, HEAD), lambda h, g: (g, h, 0, 0))),
        out_shape=(jax.ShapeDtypeStruct((S, W), F32), jax.ShapeDtypeStruct((S, W), BF16),
                   jax.ShapeDtypeStruct((S // CHUNK, H, HEAD, HEAD), F32)),
        scratch_shapes=[pltpu.VMEM((HEAD, HEAD), F32)],
        semantics=("parallel", "arbitrary"), name=name,
        args=(proj, proj, proj, proj, lb, out_g, tri), side=side)


def _hg_bwd(proj, o_pre, d_on, d_on_col0, states, lb, out_g, *, n_heads, cg, name, side=None):
    S = proj.shape[0]
    H = n_heads
    W = H * HEAD
    T = cg * CHUNK
    NG = S // T
    tri = jnp.tril(jnp.ones((CHUNK, CHUNK), F32)).astype(BF16)
    triu = jnp.triu(jnp.ones((CHUNK, CHUNK), F32)).astype(BF16)

    def body(q_ref, f_ref, v_ref, g_ref, o_ref, dy_ref, st_ref, lb_ref, og_ref, tri_ref, triu_ref,
             dq_ref, df_ref, di_ref, dg_ref, dlb_ref, dog_ref, ds_scr):
        gstep = pl.program_id(1)

        @pl.when(gstep == 0)
        def _():
            ds_scr[...] = jnp.zeros_like(ds_scr)
            dlb_ref[...] = jnp.zeros_like(dlb_ref)
            dog_ref[...] = jnp.zeros_like(dog_ref)

        lbv = lb_ref[...]
        ogv = og_ref[...]
        triv = tri_ref[...]
        triuv = triu_ref[...]
        mask = _causal_incl(CHUNK)
        dlb_acc = jnp.zeros((1, HEAD), F32)
        dog_acc = jnp.zeros((1, HEAD), F32)
        for c in reversed(range(cg)):
            rows = pl.ds(c * CHUNK, CHUNK)
            q = q_ref[rows, :]
            v = v_ref[rows, :]
            gg = g_ref[rows, :]
            o = o_ref[rows, :]
            dy = dy_ref[rows, :]
            cm = _hg_chunk(q, f_ref[rows, :], lbv, triv)
            s0 = st_ref[c]
            ds1 = ds_scr[...]
            rstd = lax.rsqrt(jnp.mean(o * o, axis=-1, keepdims=True) + EPS)
            n = o * rstd
            sg, nsg = _sigmoid_pair(gg)
            silu_g = gg * sg
            dyn = dy * n
            dog_acc = dog_acc + jnp.sum(dyn * silu_g, axis=0, keepdims=True)
            dg_ref[rows, :] = (dyn * ogv * (sg * (1.0 + gg * nsg))).astype(BF16)
            dn = dy * (ogv * silu_g)
            d_o = rstd * (dn - n * jnp.mean(dn * n, axis=-1, keepdims=True))
            A = jnp.where(mask, _bdot(cm["Qm"], cm["Km"], _NT), 0.0)
            dA = jnp.where(mask, _bdot(d_o, v, _NT), 0.0)
            dV = _bdot(A, d_o, _TN) + _bdot(cm["Kh"], ds1, _NT)
            dQm = _bdot(dA, cm["Km"])
            dKm = _bdot(dA, cm["Qm"], _TN)
            dQb = _bdot(d_o, s0)
            dKh = _bdot(v, ds1)
            ds_scr[...] = ds1 * cm["ebL"] + _bdot(d_o, cm["Qb"], _TN)
            kh_term = dKh * cm["Kh"]
            db = dQm * cm["Qm"] - dKm * cm["Km"] + dQb * cm["Qb"] - kh_term
            dbl = (jnp.sum(kh_term, axis=0, keepdims=True)
                   + cm["ebL"] * jnp.sum(ds1 * s0, axis=0, keepdims=True))
            db = db + jnp.where(cm["row"] == CHUNK - 1, dbl, 0.0)
            dlf = _split_dot_left(triuv, db, 3)
            dqa = dQm * cm["ebm"] + dQb * cm["eb"]
            dq_ref[rows, :] = (dqa * (cm["sq"] * (1.0 + q * cm["nsq"]))).astype(BF16)
            dk = dKm * cm["enbm"] + dKh * cm["ebl"]
            dfc = jnp.where(cm["f"] > TINY, dlf / cm["fc"], 0.0)
            t = dfc - dk
            df_ref[rows, :] = (t * (cm["one_lb"] * cm["sig"] * cm["nsig"])).astype(BF16)
            dlb_acc = dlb_acc + jnp.sum(t * cm["nsig"], axis=0, keepdims=True)
            di_ref[rows, :] = dV.astype(BF16)
        dlb_ref[...] += dlb_acc
        dog_ref[...] += dog_acc

    def col(group):
        return pl.BlockSpec((T, HEAD), lambda h, g: (NG - 1 - g, group * H + h))

    own = pl.BlockSpec((T, HEAD), lambda h, g: (NG - 1 - g, h))
    vec = pl.BlockSpec((1, HEAD), lambda h, g: (0, h))
    cst = pl.BlockSpec((CHUNK, CHUNK), lambda h, g: (0, 0))
    act = jax.ShapeDtypeStruct((S, W), BF16)
    vec_shape = jax.ShapeDtypeStruct((1, W), F32)
    return _pcall(
        body, grid=(H, NG),
        in_specs=[col(0), col(1), col(2), col(3), own,
                  pl.BlockSpec((T, HEAD), lambda h, g: (NG - 1 - g, d_on_col0 + h)),
                  pl.BlockSpec((cg, None, HEAD, HEAD), lambda h, g: (NG - 1 - g, h, 0, 0)),
                  vec, vec, cst, cst],
        out_specs=(own, own, own, own, vec, vec),
        out_shape=(act, act, act, act, vec_shape, vec_shape),
        scratch_shapes=[pltpu.VMEM((HEAD, HEAD), F32)],
        semantics=("parallel", "arbitrary"), name=name,
        args=(proj, proj, proj, proj, o_pre, d_on, states, lb, out_g, tri, triu), side=side)


def _sb_pre(proj, q_g, k_g, *, n_heads, col0, tm, name):
    S = proj.shape[0]
    H = n_heads
    W = H * HEAD

    def body(q_ref, k_ref, v_ref, qg_ref, kg_ref, qh_ref, kh_ref, vh_ref):
        for src, g_ref, dst in ((q_ref, qg_ref, qh_ref), (k_ref, kg_ref, kh_ref)):
            xv = src[...]
            rstd = lax.rsqrt(jnp.mean(xv * xv, axis=-1, keepdims=True) + EPS)
            dst[...] = ((xv * rstd) * g_ref[...]).astype(BF16)
        vh_ref[...] = v_ref[...].astype(BF16)

    def col(group):
        return pl.BlockSpec((tm, HEAD), lambda i, h: (i, col0 + group * H + h))

    vec = pl.BlockSpec((1, HEAD), lambda i, h: (0, 0))
    own = pl.BlockSpec((tm, HEAD), lambda i, h: (i, h))
    act = jax.ShapeDtypeStruct((S, W), BF16)
    return pl.pallas_call(
        body, grid=(S // tm, H), in_specs=[col(0), col(1), col(2), vec, vec],
        out_specs=(own, own, own), out_shape=(act, act, act),
        compiler_params=_params(("parallel", "parallel")), name=name)(proj, proj, proj, q_g, k_g)


def _sb_scores(q, k_blk, scale):
    z = lax.dot_general(q, k_blk, _NT, preferred_element_type=F32) * scale
    e = jnp.exp(-jnp.abs(z))
    sp = jnp.maximum(z, 0.0) + jnp.log(1.0 + e)
    return z, e, sp


def _heads_per_step(n_heads):
    return 2 if n_heads % 2 == 0 else 1


def _max_all(values):
    m = jnp.max(values[0])
    for v in values[1:]:
        m = jnp.maximum(m, jnp.max(v))
    return m


def _strict_lower_mask(t):
    r = lax.broadcasted_iota(jnp.int32, (t, t), 0)
    c = lax.broadcasted_iota(jnp.int32, (t, t), 1)
    return c < r


def _sb_fwd(qh, kh, vh, out_g, *, n_heads, tq, name, side=None):
    S, W = qh.shape
    HP = _heads_per_step(n_heads)
    WP = HP * HEAD
    NQ = S // tq
    scale = HEAD ** -0.5
    u_strict = jnp.tril(jnp.ones((tq, tq), F32), -1).astype(BF16)

    def body(q_ref, k_ref, v_ref, og_ref, u_ref, o_ref, on_ref):
        qi = pl.program_id(1)
        u = u_ref[...]
        heads = [slice(hh * HEAD, (hh + 1) * HEAD) for hh in range(HP)]
        qs = [q_ref[:, cols] for cols in heads]

        def block(kb, r_carry, diag, valid=None):
            rows = pl.ds(pl.multiple_of(kb * tq, tq), tq)
            pvs, rs = [], []
            for hh, cols in enumerate(heads):
                k_blk = k_ref[rows, cols]
                v_blk = v_ref[rows, cols]
                z, _, sp = _sb_scores(qs[hh], k_blk, scale)
                if diag:
                    m = _strict_lower_mask(tq)
                    L = jnp.where(m, -sp, 0.0)
                else:
                    L = -sp
                C = _split_dot(L, u, 2)
                a = jnp.exp(z - sp + C + r_carry[hh])
                if diag:
                    a = jnp.where(m, a, 0.0)
                if valid is not None:
                    a = jnp.where(valid, a, 0.0)
                pvs.append(lax.dot_general(a.astype(BF16), v_blk, (((1,), (0,)), ((), ())),
                                           preferred_element_type=F32))
                rs.append(r_carry[hh] + (C[:, 0:1] + L[:, 0:1]))
            return tuple(pvs), tuple(rs)

        acc_d, r_d = block(qi, (jnp.zeros((tq, 1), F32),) * HP, True)
        acc_p, r0 = block(jnp.maximum(qi - 1, 0), r_d, False, valid=qi > 0)
        acc0 = tuple(a + b for a, b in zip(acc_d, acc_p))

        def cond(st):
            kb, _, _, rmax = st
            return jnp.logical_and(kb >= 0, rmax > SKIP_LOG)

        def step(st):
            kb, acc, r, _ = st
            pv, r2 = block(kb, r, False)
            return kb - 1, tuple(a + b for a, b in zip(acc, pv)), r2, _max_all(r2)

        _, accs, _, _ = lax.while_loop(cond, step, (qi - 2, acc0, r0, _max_all(r0)))
        for hh, cols in enumerate(heads):
            acc = accs[hh]
            o_ref[:, cols] = acc
            rstd = lax.rsqrt(jnp.mean(acc * acc, axis=-1, keepdims=True) + EPS)
            on_ref[:, cols] = ((acc * rstd) * og_ref[:, cols]).astype(BF16)

    blk = pl.BlockSpec((tq, WP), lambda h, i: (i, h))
    full = pl.BlockSpec((S, WP), lambda h, i: (0, h))
    return _pcall(
        body, grid=(n_heads // HP, NQ),
        in_specs=[blk, full, full, pl.BlockSpec((1, WP), lambda h, i: (0, h)),
                  pl.BlockSpec((tq, tq), lambda h, i: (0, 0))],
        out_specs=(blk, blk),
        out_shape=(jax.ShapeDtypeStruct((S, W), F32), jax.ShapeDtypeStruct((S, W), BF16)),
        semantics=("parallel", "arbitrary"), name=name, args=(qh, kh, vh, out_g, u_strict), side=side)


def _sb_bwd(qh, kh, vh, o_pre, d_on, d_on_col0, out_g, *, n_heads, tq, name, side=None):
    S, W = qh.shape
    HP = _heads_per_step(n_heads)
    WP = HP * HEAD
    assert d_on_col0 % HP == 0
    NQ = S // tq
    scale = HEAD ** -0.5
    u_strict = jnp.tril(jnp.ones((tq, tq), F32), -1).astype(BF16)
    u_incl = jnp.tril(jnp.ones((tq, tq), F32)).astype(BF16)

    def body(q_ref, k_ref, v_ref, o_ref, dy_ref, og_ref, us_ref, ui_ref,
             dq_ref, dk_ref, dv_ref, dog_ref):
        qi = pl.program_id(1)

        @pl.when(qi == 0)
        def _():
            dk_ref[...] = jnp.zeros_like(dk_ref)
            dv_ref[...] = jnp.zeros_like(dv_ref)
            dog_ref[...] = jnp.zeros_like(dog_ref)

        us = us_ref[...]
        ui = ui_ref[...]
        heads = [slice(hh * HEAD, (hh + 1) * HEAD) for hh in range(HP)]
        qs, d_obs, deltas = [], [], []
        for cols in heads:
            qs.append(q_ref[:, cols])
            o = o_ref[:, cols]
            dy = dy_ref[:, cols]
            rstd = lax.rsqrt(jnp.mean(o * o, axis=-1, keepdims=True) + EPS)
            n = o * rstd
            dog_ref[:, cols] += jnp.sum(dy * n, axis=0, keepdims=True)
            dn = dy * og_ref[:, cols]
            d_o = rstd * (dn - n * jnp.mean(dn * n, axis=-1, keepdims=True))
            d_ob = d_o.astype(BF16)
            d_obs.append(d_ob)
            deltas.append(jnp.sum(d_ob.astype(F32) * o, axis=-1, keepdims=True))
        q_ts = [q.T for q in qs]
        d_ob_ts = [d.T for d in d_obs]

        def block(kb, r_carry, g_carry, diag, valid=None):
            rows = pl.ds(pl.multiple_of(kb * tq, tq), tq)
            dqs, rs, gs = [], [], []
            for hh, cols in enumerate(heads):
                k_blk = k_ref[rows, cols]
                v_blk = v_ref[rows, cols]
                z, e, sp = _sb_scores(qs[hh], k_blk, scale)
                if diag:
                    m = _strict_lower_mask(tq)
                    L = jnp.where(m, -sp, 0.0)
                else:
                    L = -sp
                C = _split_dot(L, us, 2)
                a = jnp.exp(z - sp + C + r_carry[hh])
                if diag:
                    a = jnp.where(m, a, 0.0)
                if valid is not None:
                    a = jnp.where(valid, a, 0.0)
                ab = a.astype(BF16)
                dA = lax.dot_general(d_obs[hh], v_blk, _NT, preferred_element_type=F32)
                G = ab.astype(F32) * dA
                SI = _split_dot(G, ui, 2)
                P = deltas[hh] - (g_carry[hh] + SI)
                r = 1.0 / (1.0 + e)
                sig = jnp.where(z >= 0, r, e * r)
                dz = G - (G + P) * sig
                if diag:
                    dz = jnp.where(m, dz, 0.0)
                if valid is not None:
                    dz = jnp.where(valid, dz, 0.0)
                dzb = (dz * scale).astype(BF16)
                dqs.append(lax.dot_general(dzb, k_blk, (((1,), (0,)), ((), ())), preferred_element_type=F32))
                dk_ref[cols, rows] += jnp.dot(q_ts[hh], dzb, preferred_element_type=F32)
                dv_ref[cols, rows] += jnp.dot(d_ob_ts[hh], ab, preferred_element_type=F32)
                rs.append(r_carry[hh] + (C[:, 0:1] + L[:, 0:1]))
                gs.append(g_carry[hh] + SI[:, 0:1])
            return tuple(dqs), tuple(rs), tuple(gs)

        zero = (jnp.zeros((tq, 1), F32),) * HP
        dq_d, r_d, g_d = block(qi, zero, zero, True)
        dq_p, r0, g0 = block(jnp.maximum(qi - 1, 0), r_d, g_d, False, valid=qi > 0)
        dq0 = tuple(a + b for a, b in zip(dq_d, dq_p))

        def cond(st):
            kb, _, _, _, rmax = st
            return jnp.logical_and(kb >= 0, rmax > SKIP_LOG)

        def step(st):
            kb, dq, r, g, _ = st
            dq_part, r2, g2 = block(kb, r, g, False)
            return kb - 1, tuple(a + b for a, b in zip(dq, dq_part)), r2, g2, _max_all(r2)

        _, dqs, _, _, _ = lax.while_loop(cond, step, (qi - 2, dq0, r0, g0, _max_all(r0)))
        for hh, cols in enumerate(heads):
            dq_ref[:, cols] = dqs[hh]

    blk = pl.BlockSpec((tq, WP), lambda h, i: (i, h))
    full = pl.BlockSpec((S, WP), lambda h, i: (0, h))
    vec = pl.BlockSpec((1, WP), lambda h, i: (0, h))
    cst = pl.BlockSpec((tq, tq), lambda h, i: (0, 0))
    act = jax.ShapeDtypeStruct((S, W), F32)
    act_t = jax.ShapeDtypeStruct((W, S), F32)
    full_t = pl.BlockSpec((WP, S), lambda h, i: (h, 0))
    return _pcall(
        body, grid=(n_heads // HP, NQ),
        in_specs=[blk, full, full, blk,
                  pl.BlockSpec((tq, WP), lambda h, i: (i, d_on_col0 // HP + h)), vec, cst, cst],
        out_specs=(blk, full_t, full_t, vec),
        out_shape=(act, act_t, act_t, jax.ShapeDtypeStruct((1, W), F32)),
        semantics=("parallel", "arbitrary"), name=name,
        args=(qh, kh, vh, o_pre, d_on, out_g, u_strict, u_incl), side=side)


def _sb_pre_bwd(proj, dqh, dkh, dvh, q_g, k_g, *, n_heads, col0, tm, name):
    S = proj.shape[0]
    H = n_heads
    W = H * HEAD

    def body(q_ref, k_ref, dqh_ref, dkh_ref, dvh_ref, qg_ref, kg_ref,
             dq_ref, dk_ref, dv_ref, dqg_ref, dkg_ref):
        first = jnp.logical_and(pl.program_id(0) == 0, pl.program_id(1) == 0)

        @pl.when(first)
        def _():
            dqg_ref[...] = jnp.zeros_like(dqg_ref)
            dkg_ref[...] = jnp.zeros_like(dkg_ref)

        for src, dh, g_ref, dst, dg_ref in ((q_ref, dqh_ref[...], qg_ref, dq_ref, dqg_ref),
                                            (k_ref, dkh_ref[...].T, kg_ref, dk_ref, dkg_ref)):
            xv = src[...]
            rstd = lax.rsqrt(jnp.mean(xv * xv, axis=-1, keepdims=True) + EPS)
            n = xv * rstd
            dg_ref[...] += jnp.sum(dh * n, axis=0, keepdims=True)
            dn = dh * g_ref[...]
            dst[...] = (rstd * (dn - n * jnp.mean(dn * n, axis=-1, keepdims=True))).astype(BF16)
        dv_ref[...] = dvh_ref[...].T.astype(BF16)

    def col(group):
        return pl.BlockSpec((tm, HEAD), lambda i, h: (i, col0 + group * H + h))

    vec = pl.BlockSpec((1, HEAD), lambda i, h: (0, 0))
    own = pl.BlockSpec((tm, HEAD), lambda i, h: (i, h))
    own_t = pl.BlockSpec((HEAD, tm), lambda i, h: (h, i))
    act = jax.ShapeDtypeStruct((S, W), BF16)
    vec_shape = jax.ShapeDtypeStruct((1, HEAD), F32)
    return pl.pallas_call(
        body, grid=(S // tm, H), in_specs=[col(0), col(1), own, own_t, own_t, vec, vec],
        out_specs=(own, own, own, vec, vec), out_shape=(act, act, act, vec_shape, vec_shape),
        compiler_params=_params(("arbitrary", "arbitrary")), name=name,
    )(proj, proj, dqh, dkh, dvh, q_g, k_g)


def _softmax_rows(x_ref, L):
    rows = [x_ref[l:l + 1, :] for l in range(L)]
    mx = rows[0]
    for r in rows[1:]:
        mx = jnp.maximum(mx, r)
    ex = [jnp.exp(r - mx) for r in rows]
    tot = ex[0]
    for e in ex[1:]:
        tot = tot + e
    return [e / tot for e in ex]


def _lb_fwd(logits, *, name):
    L, W = logits.shape

    def body(x_ref, o_ref):
        s = _softmax_rows(x_ref, L)
        run = jnp.zeros((1, W), F32)
        for l in range(L):
            run = run + s[l]
            o_ref[l:l + 1, :] = run - s[0]

    return pl.pallas_call(body, out_shape=jax.ShapeDtypeStruct((L, W), F32), name=name)(logits)


def _lb_bwd(logits, dlb_parts, *, name):
    L, W = logits.shape
    P = dlb_parts.shape[0]

    def body(x_ref, d_ref, o_ref):
        s = _softmax_rows(x_ref, L)
        dlb = []
        for l in range(L):
            t = d_ref[0, l:l + 1, :]
            for q in range(1, P):
                t = t + d_ref[q, l:l + 1, :]
            dlb.append(t)
        ds = [None] * L
        run = jnp.zeros((1, W), F32)
        for j in reversed(range(L)):
            run = run + dlb[j]
            ds[j] = run
        ds[0] = jnp.zeros((1, W), F32)
        inner = jnp.zeros((1, W), F32)
        for j in range(L):
            inner = inner + s[j] * ds[j]
        for j in range(L):
            o_ref[j:j + 1, :] = s[j] * (ds[j] - inner)

    return pl.pallas_call(body, out_shape=jax.ShapeDtypeStruct((L, W), F32), name=name)(logits, dlb_parts)


def _ada_mod(c_all, w_ada, *, nb, name):
    L, D, n = w_ada.shape
    B = c_all.shape[0]

    def body(c_ref, w_ref, o_ref, cond_ref):
        cv = c_ref[...]
        s, _ = _sigmoid_pair(cv)
        cond = cv * s
        cond_ref[...] = cond
        o_ref[...] = _bdot(cond, w_ref[...])

    return pl.pallas_call(
        body, grid=(L, n // nb),
        in_specs=[pl.BlockSpec((B, D), lambda l, j: (0, 0)),
                  pl.BlockSpec((None, D, nb), lambda l, j: (l, 0, j))],
        out_specs=(pl.BlockSpec((None, B, nb), lambda l, j: (l, 0, j)),
                   pl.BlockSpec((B, D), lambda l, j: (0, 0))),
        out_shape=(jax.ShapeDtypeStruct((L, B, n), F32), jax.ShapeDtypeStruct((B, D), F32)),
        compiler_params=_params(("arbitrary", "arbitrary")), name=name)(c_all, w_ada)


def _adam_math(w, g, m, v):
    m2 = ADAM_B1 * m + (1.0 - ADAM_B1) * g
    v2 = ADAM_B2 * v + (1.0 - ADAM_B2) * (g * g)
    m_hat = m2 / (1.0 - ADAM_B1 ** ADAM_STEP)
    v_hat = v2 / (1.0 - ADAM_B2 ** ADAM_STEP)
    delta = -ADAM_LR * (m_hat / (jnp.sqrt(v_hat) + ADAM_EPS) + ADAM_WD * w)
    return delta, m2, v2


def _adamw(w, m, v, gparts, *, tr, name):
    R, C = w.shape
    P = gparts.shape[0]

    def body(w_ref, m_ref, v_ref, gp_ref, g_ref, d_ref, m2_ref, v2_ref):
        g = gp_ref[0].astype(F32)
        for p in range(1, P):
            g = g + gp_ref[p].astype(F32)
        delta, m2, v2 = _adam_math(w_ref[...], g, m_ref[...], v_ref[...])
        g_ref[...] = g
        d_ref[...] = delta
        m2_ref[...] = m2
        v2_ref[...] = v2

    tile = pl.BlockSpec((tr, C), lambda i: (i, 0))
    shp = jax.ShapeDtypeStruct((R, C), F32)
    return pl.pallas_call(
        body, grid=(R // tr,),
        in_specs=[tile, tile, tile, pl.BlockSpec((P, tr, C), lambda i: (0, i, 0))],
        out_specs=(tile, tile, tile, tile), out_shape=(shp, shp, shp, shp),
        compiler_params=_params(("parallel",)), name=name)(w, m, v, gparts)


def _adamw_layers(w, m, v, gparts, *, tr, name, side=None):
    L, R, C = w.shape
    P = gparts[0].shape[0]
    nblk = R // tr

    def body(*refs):
        w_ref, m_ref, v_ref = refs[:3]
        gp_refs = refs[3:3 + L]
        g_ref, d_ref, m2_ref, v2_ref = refs[3 + L:]
        layer = pl.program_id(0)
        for t in range(L):
            @pl.when(layer == t)
            def _(t=t):
                g = gp_refs[t][0].astype(F32)
                for q in range(1, P):
                    g = g + gp_refs[t][q].astype(F32)
                delta, m2, v2 = _adam_math(w_ref[...], g, m_ref[...], v_ref[...])
                g_ref[...] = g
                d_ref[...] = delta
                m2_ref[...] = m2
                v2_ref[...] = v2

    def gp_spec(t):
        def index(l, i):
            return (0, jnp.where(l == t, i, jnp.where(l < t, 0, nblk - 1)), 0)
        return pl.BlockSpec((P, tr, C), index)

    tile = pl.BlockSpec((None, tr, C), lambda l, i: (l, i, 0))
    shp = jax.ShapeDtypeStruct((L, R, C), F32)
    return _pcall(
        body, grid=(L, nblk), in_specs=[tile, tile, tile] + [gp_spec(t) for t in range(L)],
        out_specs=(tile, tile, tile, tile), out_shape=(shp, shp, shp, shp),
        semantics=("arbitrary", "arbitrary"), name=name, args=(w, m, v, *gparts), side=side)


def _adamw_ada(w, m, v, cond_t, dmod, *, tr, name):
    L, D, n = w.shape
    Bp = cond_t.shape[1]

    def body(w_ref, m_ref, v_ref, c_ref, dm_ref, g_ref, d_ref, m2_ref, v2_ref):
        g = _bdot(c_ref[...], dm_ref[...])
        delta, m2, v2 = _adam_math(w_ref[...], g, m_ref[...], v_ref[...])
        g_ref[...] = g
        d_ref[...] = delta
        m2_ref[...] = m2
        v2_ref[...] = v2

    tile = pl.BlockSpec((None, tr, n), lambda l, i: (l, i, 0))
    shp = jax.ShapeDtypeStruct((L, D, n), F32)
    return pl.pallas_call(
        body, grid=(L, D // tr),
        in_specs=[tile, tile, tile, pl.BlockSpec((tr, Bp), lambda l, i: (i, 0)),
                  pl.BlockSpec((None, Bp, n), lambda l, i: (l, 0, 0))],
        out_specs=(tile, tile, tile, tile), out_shape=(shp, shp, shp, shp),
        compiler_params=_params(("parallel", "parallel")), name=name)(w, m, v, cond_t, dmod)


def _allgather_small(block, *, name):
    R, C = block.shape

    def body(x_ref, out_ref, send_sems, recv_sems, local_sem):
        x, y, c = lax.axis_index("x"), lax.axis_index("y"), lax.axis_index("c")

        def rows(px, py, pc):
            return out_ref.at[pl.ds((4 * px + 2 * py + pc) * R, R), :]

        mine = pltpu.make_async_copy(x_ref, rows(x, y, c), local_sem)
        mine.start()
        sends = []
        for rel in range(1, N_DEV):
            to = _peer(x, y, c, rel)
            cp = pltpu.make_async_remote_copy(src_ref=x_ref, dst_ref=rows(x, y, c),
                                              send_sem=send_sems.at[rel - 1], recv_sem=recv_sems.at[rel - 1],
                                              device_id=to, device_id_type=MESH)
            cp.start()
            sends.append(cp)
        for rel in range(1, N_DEV):
            frm = _peer(x, y, c, rel)
            pltpu.make_async_remote_copy(src_ref=x_ref, dst_ref=rows(*frm),
                                         send_sem=send_sems.at[rel - 1], recv_sem=recv_sems.at[rel - 1],
                                         device_id=frm, device_id_type=MESH).wait_recv()
        for cp in sends:
            cp.wait_send()
        mine.wait()

    return pl.pallas_call(
        body, out_shape=jax.ShapeDtypeStruct((N_DEV * R, C), block.dtype),
        in_specs=[pl.BlockSpec(memory_space=pltpu.VMEM)],
        out_specs=pl.BlockSpec(memory_space=pltpu.VMEM),
        scratch_shapes=[pltpu.SemaphoreType.DMA((N_DEV - 1,)), pltpu.SemaphoreType.DMA((N_DEV - 1,)),
                        pltpu.SemaphoreType.DMA],
        compiler_params=pltpu.CompilerParams(vmem_limit_bytes=V7X_VMEM_LIMIT), name=name)(block)


def _allgather_hbm(shards, *, name):
    n = len(shards)

    def body(*refs):
        ins = refs[:n]
        outs = refs[n:2 * n]
        send_sems, recv_sems, local_sems = refs[2 * n:]
        x, y, c = lax.axis_index("x"), lax.axis_index("y"), lax.axis_index("c")
        sibling = (x, y, 1 - c)
        chips = [(1 - x, y), (x, 1 - y), (1 - x, 1 - y)]

        def slot(t, px, py, pc):
            return outs[t].at[4 * px + 2 * py + pc]

        def copy(t, k, block, to, src=None):
            return pltpu.make_async_remote_copy(
                src_ref=slot(t, *block) if src is None else src, dst_ref=slot(t, *block),
                send_sem=send_sems.at[t * 7 + k], recv_sem=recv_sems.at[t * 7 + k],
                device_id=to, device_id_type=MESH)

        me = (x, y, c)
        started = []
        mine = []
        for t in range(n):
            cp = pltpu.make_async_copy(ins[t], slot(t, *me), local_sems.at[t])
            cp.start()
            mine.append(cp)
            first = [copy(t, 0, me, sibling, src=ins[t])]
            first += [copy(t, 1 + j, me, (*chip, c), src=ins[t]) for j, chip in enumerate(chips)]
            for cp in first:
                cp.start()
            started += first
        for t in range(n):
            for j, chip in enumerate(chips):
                copy(t, 1 + j, (*chip, c), me).wait_recv()
                fwd = copy(t, 4 + j, (*chip, c), sibling)
                fwd.start()
                started.append(fwd)
        for t in range(n):
            copy(t, 0, sibling, me).wait_recv()
            for j, chip in enumerate(chips):
                copy(t, 4 + j, (*chip, 1 - c), me).wait_recv()
        for cp in started:
            cp.wait_send()
        for cp in mine:
            cp.wait()

    any_spec = pl.BlockSpec(memory_space=pl.ANY)
    return pl.pallas_call(
        body, out_shape=[jax.ShapeDtypeStruct((N_DEV,) + s.shape, s.dtype) for s in shards],
        in_specs=[any_spec] * n, out_specs=[any_spec] * n,
        scratch_shapes=[pltpu.SemaphoreType.DMA((7 * n,)), pltpu.SemaphoreType.DMA((7 * n,)),
                        pltpu.SemaphoreType.DMA((n,))],
        name=name)(*shards)


def _tile(total, want):
    step = 128 if total % 128 == 0 else 8
    best = step
    t = step
    while t <= min(total, want):
        if total % t == 0:
            best = t
        t += step
    return best


def _local_step(x, target, mods, lbs, p, wg, shards=None):
    S, D = x.shape
    L = mods.shape[0]
    W = D // 2
    H = W // HEAD
    mesh = shards is not None
    F = shards["w_ffn_out"][0].shape[0] * N_DEV if mesh else wg["w_ffn_out"][0].shape[0]
    tm = _tile(S, 512)
    tm_big = _tile(S, 1024)
    tm_tn = _tile(S, 2048)
    tm_sw = _tile(S, 128)
    tq = _tile(S, 256)
    cg = max(1, min(32, S // CHUNK))
    nb_out = _tile(D, 1024)
    kb_f = _tile(F, 1408)

    def row(a, l):
        return a[l][None, :]

    first_plan = dict(proj=[("w_ffn_in", 0)], hg=[("w_ffn_out", 0), ("w_out", 0)],
                      sb=[("w_in", 1), ("w_out", 1)], ffn=[("w_ffn_in", 1), ("w_ffn_out", 1)])

    def gather_plan(l, call):
        if not mesh:
            return []
        if l == 0:
            plan = first_plan[call]
        else:
            plan = [(dict(proj="w_in", hg="w_out", sb="w_ffn_out", ffn="w_ffn_in")[call], l + 1)]
        return [(k, j) for k, j in plan if j < L]

    def gather_of(plan):
        return (True, [shards[k][j] for k, j in plan]) if plan else None

    def store_gathered(plan, got):
        for (k, j), g in zip(plan, got or []):
            wg[k][j] = g.reshape(-1, D) if k in ("w_out", "w_ffn_out") else g

    def scatter_of(blocks):
        if mesh and blocks is not None:
            return (False, [b.reshape((N_DEV, -1) + b.shape[-1:]) if b.ndim == 2 else b for b in blocks])
        return None

    saved = []
    xcur = x
    for l in range(L):
        mod = mods[l]
        sh1, sc1, g1, sh2, sc2, g2 = [mod[:, i * D:(i + 1) * D] for i in range(N_MOD)]
        h1 = _modnorm_fwd(xcur, row(p["norm1_g"], l), sc1, sh1, tm=tm, name="norm1_fwd")
        plan = gather_plan(l, "proj")
        proj, got = _mm_nn(h1, wg["w_in"][l], tm=tm_big, name="proj_fwd", side=gather_of(plan))
        store_gathered(plan, got)
        lb = lbs[l][None, :]
        plan = gather_plan(l, "hg")
        (o_hg, on_hg, states), got = _hg_fwd(proj, lb, row(p["hg_out_g"], l), n_heads=H, cg=cg, name="hgrn2_fwd",
                                             side=gather_of(plan))
        store_gathered(plan, got)
        qh, kh, vh = _sb_pre(proj, row(p["sb_q_g"], l), row(p["sb_k_g"], l), n_heads=H, col0=4 * H,
                             tm=tm_tn, name="sb_qknorm_fwd")
        plan = gather_plan(l, "sb")
        (o_sb, on_sb), got = _sb_fwd(qh, kh, vh, row(p["sb_out_g"], l), n_heads=H, tq=tq, name="sb_fwd",
                                     side=gather_of(plan))
        store_gathered(plan, got)
        o_cat = jnp.concatenate([on_hg, on_sb], axis=1)
        (x1, mixed, h2), _ = _mm_nn(o_cat, wg["w_out"][l], tm=tm, nb=D, resid=xcur, gate=g1,
                                    norm=(row(p["norm2_g"], l), sc2, sh2), name="out_proj_fwd")
        w_fin = wg["w_ffn_in"][l]
        if w_fin.shape[0] % 2:
            w_fin = jnp.stack([w_fin[0][:, :F], w_fin[0][:, F:]])
        plan = gather_plan(l, "ffn")
        (gate, up, a), got = _ffn_in_fwd(h2, w_fin, tm=tm, nb=F, name="ffn_in_fwd", side=gather_of(plan))
        store_gathered(plan, got)
        (x2, ffn), _ = _mm_nn(a, wg["w_ffn_out"][l], tm=tm, nb=nb_out // 2, resid=x1, gate=g2, name="ffn_out_fwd")
        saved.append(dict(x=xcur, h1=h1, proj=proj, o_hg=o_hg, o_sb=o_sb, states=states, qh=qh, kh=kh, vh=vh,
                          o_cat=o_cat, mixed=mixed, x1=x1, h2=h2, gate=gate, up=up, a=a, ffn=ffn, lb=lb,
                          sc1=sc1, g1=g1, sc2=sc2, g2=g2, w_fin=w_fin))
        xcur = x2

    last = saved[-1]
    dx, dffn, dg2, loss = _loss_bwd(xcur, target, last["ffn"], last["g2"], tm=tm, name="loss_bwd")

    big = {k: [None] * L for k in ("w_in", "w_out", "w_ffn_in", "w_ffn_out")}
    small = {k: [None] * L for k in ("norm1_g", "hg_lb", "hg_out_g", "sb_q_g", "sb_k_g", "sb_out_g", "norm2_g")}
    dmods = [None] * L
    pending_in = None
    for l in reversed(range(L)):
        sv = saved[l]
        (dgate, dup), got = _ffn_out_bwd_x(dffn, wg["w_ffn_out"][l], sv["gate"], sv["up"], tm=tm, kb=kb_f,
                                           name="ffn_out_bwd_x", side=scatter_of(pending_in))
        if got is not None:
            big["w_in"][l + 1] = got[0]
        g_fout, _ = _mm_tn(sv["a"], dffn, tm=tm_tn, kb=kb_f, nb=nb_out, blocked=False, name="ffn_out_bwd_w")
        dh2, got = _mm_nt(dgate, sv["w_fin"], dy2=dup, tm=tm_big, kb=D, name="ffn_in_bwd_x",
                          side=scatter_of([g_fout]))
        big["w_ffn_out"][l] = g_fout if got is None else got[0]
        g_fin, _ = _mm_tn(sv["h2"], dgate, dy2=dup, tm=tm_tn, kb=_tile(D, 1024), nb=sv["w_fin"].shape[2],
                          blocked=True, name="ffn_in_bwd_w")
        dx1, dmixed, dg1, dsh2, dsc2, dn2 = _modnorm_bwd(
            sv["x1"], dh2, dx, row(p["norm2_g"], l), sv["sc2"], sv["mixed"], sv["g1"], tm=tm_sw * 2,
            name="norm2_bwd")
        small["norm2_g"][l] = dn2
        d_ocat, _ = _mm_nt(dmixed, wg["w_out"][l], tm=tm_big, kb=nb_out, nb=D, name="out_proj_bwd_x")
        g_out, _ = _mm_tn(sv["o_cat"], dmixed, tm=tm_tn, kb=D, nb=nb_out, blocked=False, name="out_proj_bwd_w")
        (dhq, dhf, dhi, dhg, dlb, dhog), _ = _hg_bwd(sv["proj"], sv["o_hg"], d_ocat, 0, sv["states"], sv["lb"],
                                                     row(p["hg_out_g"], l), n_heads=H, cg=cg, name="hgrn2_bwd")
        (dqh, dkh, dvh, dsog), got = _sb_bwd(sv["qh"], sv["kh"], sv["vh"], sv["o_sb"], d_ocat, H,
                                             row(p["sb_out_g"], l), n_heads=H, tq=tq, name="sb_bwd",
                                             side=scatter_of([g_fin]))
        big["w_ffn_in"][l] = g_fin if got is None else got[0]
        dsq, dsk, dsv, dqg, dkg = _sb_pre_bwd(sv["proj"], dqh, dkh, dvh, row(p["sb_q_g"], l),
                                              row(p["sb_k_g"], l), n_heads=H, col0=4 * H, tm=tm_tn,
                                              name="sb_qknorm_bwd")
        small["hg_lb"][l] = dlb
        small["hg_out_g"][l] = dhog
        small["sb_out_g"][l] = dsog
        small["sb_q_g"][l] = dqg
        small["sb_k_g"][l] = dkg
        dproj = jnp.concatenate([dhq, dhf, dhi, dhg, dsq, dsk, dsv], axis=1)
        g_in, _ = _mm_tn(sv["h1"], dproj, tm=tm_tn, kb=D, nb=wg["w_in"][l].shape[2], blocked=True,
                         name="proj_bwd_w")
        big["w_in"][l] = g_in
        dh1, got = _mm_nt(dproj, wg["w_in"][l], tm=tm_big, kb=D, name="proj_bwd_x",
                          side=scatter_of([g_out, g_in] if l == 0 else [g_out]))
        if got is not None:
            big["w_out"][l] = got[0]
            if l == 0:
                big["w_in"][l] = got[1]
        else:
            big["w_out"][l] = g_out
        pending_in = [g_in]
        if l > 0:
            prev = saved[l - 1]
            dx0, dffn_prev, dg2_prev, dsh1, dsc1, dn1 = _modnorm_bwd(
                sv["x"], dh1, dx1, row(p["norm1_g"], l), sv["sc1"], prev["ffn"], prev["g2"], tm=tm_sw * 2,
                name="norm1_bwd")
        else:
            dx0, dsh1, dsc1, dn1 = _modnorm_bwd(sv["x"], dh1, dx1, row(p["norm1_g"], l), sv["sc1"], None, None,
                                                tm=tm_sw * 2, name="norm1_bwd_first")
            dffn_prev, dg2_prev = None, None
        small["norm1_g"][l] = dn1
        dmods[l] = jnp.concatenate([dsh1, dsc1, dg1, dsh2, dsc2, dg2], axis=1)
        dx, dffn, dg2 = dx0, dffn_prev, dg2_prev
    return loss, dx, big, small, dmods


def kernel(x, c, norm1_g, w_in, hg_lb_logits, hg_out_g, sb_q_g, sb_k_g, sb_out_g, w_out, norm2_g, w_ffn_in, w_ffn_out, w_ada, b_ada, loss_target, m_norm1_g, m_w_in, m_hg_lb_logits, m_hg_out_g, m_sb_q_g, m_sb_k_g, m_sb_out_g, m_w_out, m_norm2_g, m_w_ffn_in, m_w_ffn_out, m_w_ada, m_b_ada, v_norm1_g, v_w_in, v_hg_lb_logits, v_hg_out_g, v_sb_q_g, v_sb_k_g, v_sb_out_g, v_w_out, v_norm2_g, v_w_ffn_in, v_w_ffn_out, v_w_ada, v_b_ada):
    L, D = norm1_g.shape
    S = x.shape[1]
    me = 4 * lax.axis_index("x") + 2 * lax.axis_index("y") + lax.axis_index("c")

    c_all = _allgather_small(jnp.broadcast_to(c, (8, D)), name="gather_c").reshape(N_DEV, 8, D)[:, 0, :]
    n_ada = w_ada.shape[2]
    mod_cols, cond = _ada_mod(c_all, w_ada, nb=_tile(n_ada, 512), name="ada_mod")
    mod_all = _allgather_small(mod_cols.reshape(L * N_DEV, n_ada), name="gather_mod")
    mod_all = mod_all.reshape(N_DEV, L, N_DEV, n_ada)
    mod_mine = lax.dynamic_index_in_dim(mod_all, me, axis=2, keepdims=False)
    mods = jnp.transpose(mod_mine, (1, 0, 2)).reshape(L, 1, N_DEV * n_ada) + b_ada[:, None, :]

    lbs = _lb_fwd(hg_lb_logits, name="lower_bounds_fwd")

    shards = dict(w_in=[w_in[l].astype(BF16) for l in range(L)], w_out=[w_out[l].astype(BF16) for l in range(L)],
                  w_ffn_in=[w_ffn_in[l].astype(BF16) for l in range(L)],
                  w_ffn_out=[w_ffn_out[l].astype(BF16) for l in range(L)])
    g_in, = _allgather_hbm([shards["w_in"][0]], name="gather_first_weight")
    wg = dict(w_in=[g_in] + [None] * (L - 1), w_out=[None] * L, w_ffn_in=[None] * L, w_ffn_out=[None] * L)

    p = dict(norm1_g=norm1_g, hg_out_g=hg_out_g, sb_q_g=sb_q_g, sb_k_g=sb_k_g, sb_out_g=sb_out_g,
             norm2_g=norm2_g)
    loss_part, grad_x, recv, small, dmods = _local_step(x.reshape(S, D), loss_target.reshape(S, D), mods, lbs, p,
                                                        wg, shards)

    dmod = jnp.concatenate(dmods, axis=0)
    pieces = [jnp.concatenate(small[k], axis=0) for k in
              ("norm1_g", "hg_lb", "hg_out_g", "sb_q_g", "sb_k_g", "sb_out_g", "norm2_g")] + [dmod]
    flat = jnp.concatenate([a.reshape(-1) for a in pieces] + [loss_part.reshape(-1)])
    n_flat = flat.shape[0]
    rows = -(-n_flat // 1024) * 8
    flat = jnp.pad(flat, (0, rows * 128 - n_flat)).reshape(rows, 128)
    gathered = _allgather_small(flat, name="gather_small_grads").reshape(N_DEV, rows * 128)

    def take(off, shape):
        size = 1
        for s in shape:
            size *= s
        return gathered[:, off:off + size].reshape((N_DEV,) + tuple(shape)), off + size

    off = 0
    parts = {}
    for k, a in zip(("norm1_g", "hg_lb", "hg_out_g", "sb_q_g", "sb_k_g", "sb_out_g", "norm2_g", "dmod"), pieces):
        parts[k], off = take(off, a.shape)
    loss_parts = gathered[:, off:off + 1]
    loss = jnp.sum(loss_parts)

    def pad8(a):
        return jnp.pad(a, ((0, 0), (0, 8 - a.shape[1]), (0, 0)))

    def small_update(w, m, v, gparts):
        Lw = w.shape[0]
        g, d, m2, v2 = _adamw(pad8(w[None])[0], pad8(m[None])[0], pad8(v[None])[0], pad8(gparts),
                              tr=8, name="adamw_small")
        return g[:Lw], d[:Lw], m2[:Lw], v2[:Lw]

    out = {}
    out["norm1_g"] = small_update(norm1_g, m_norm1_g, v_norm1_g, parts["norm1_g"])
    dlogits = _lb_bwd(hg_lb_logits, parts["hg_lb"], name="lower_bounds_bwd")
    out["hg_lb_logits"] = small_update(hg_lb_logits, m_hg_lb_logits, v_hg_lb_logits, dlogits[None])
    out["hg_out_g"] = small_update(hg_out_g, m_hg_out_g, v_hg_out_g, parts["hg_out_g"])
    out["sb_q_g"] = small_update(sb_q_g, m_sb_q_g, v_sb_q_g, parts["sb_q_g"])
    out["sb_k_g"] = small_update(sb_k_g, m_sb_k_g, v_sb_k_g, parts["sb_k_g"])
    out["sb_out_g"] = small_update(sb_out_g, m_sb_out_g, v_sb_out_g, parts["sb_out_g"])
    out["norm2_g"] = small_update(norm2_g, m_norm2_g, v_norm2_g, parts["norm2_g"])
    out["b_ada"] = small_update(b_ada, m_b_ada, v_b_ada, parts["dmod"])

    dmod_all = parts["dmod"].reshape(N_DEV, L, N_DEV, n_ada)
    dmod_mine = lax.dynamic_index_in_dim(dmod_all, me, axis=2, keepdims=False)
    dmod_mine = jnp.pad(jnp.transpose(dmod_mine, (1, 0, 2)), ((0, 0), (0, 128 - N_DEV), (0, 0)))
    cond_t = jnp.pad(jnp.transpose(cond), ((0, 0), (0, 128 - N_DEV)))
    out["w_ada"] = _adamw_ada(w_ada, m_w_ada, v_w_ada, cond_t, dmod_mine, tr=_tile(D, 256), name="adamw_ada")

    def big_update(w, m, v, recv_l, name):
        return _adamw_layers(w, m, v, recv_l, tr=_tile(w.shape[1], 131072 // w.shape[2]), name=name)[0]

    out["w_ffn_in"] = big_update(w_ffn_in, m_w_ffn_in, v_w_ffn_in, recv["w_ffn_in"], "adamw_w_ffn_in")
    out["w_ffn_out"] = big_update(w_ffn_out, m_w_ffn_out, v_w_ffn_out, recv["w_ffn_out"], "adamw_w_ffn_out")
    out["w_out"] = big_update(w_out, m_w_out, v_w_out, recv["w_out"], "adamw_w_out")
    out["w_in"] = big_update(w_in, m_w_in, v_w_in, recv["w_in"], "adamw_w_in")

    order = ("norm1_g", "w_in", "hg_lb_logits", "hg_out_g", "sb_q_g", "sb_k_g", "sb_out_g", "w_out", "norm2_g",
             "w_ffn_in", "w_ffn_out", "w_ada", "b_ada")
    grads = [out[k][0] for k in order]
    deltas = [out[k][1] for k in order]
    new_m = [out[k][2] for k in order]
    new_v = [out[k][3] for k in order]
    return (loss, grad_x.reshape(1, S, D), *grads, *deltas, *new_m, *new_v)
```

```python
import functools

import jax
import jax.numpy as jnp
from jax import lax
from jax.experimental import pallas as pl
from jax.experimental.pallas import tpu as pltpu

F32 = jnp.float32
BF16 = jnp.bfloat16
MESH = pl.DeviceIdType.MESH

N_DEV = 8
HEAD = 128
CHUNK = 64
N_MOD = 6
EPS = 1e-6
TINY = 1e-30
ADAM_LR = 0.001
ADAM_B1 = 0.9
ADAM_B2 = 0.999
ADAM_EPS = 1e-08
ADAM_WD = 0.01
ADAM_STEP = 10
V7X_VMEM_LIMIT = 56 * 1024 * 1024
SKIP_LOG = -104.0


def _params(sem):
    return pltpu.CompilerParams(dimension_semantics=sem, vmem_limit_bytes=V7X_VMEM_LIMIT)


def _bdot(a, b, dims=(((1,), (0,)), ((), ()))):
    return lax.dot_general(a.astype(BF16), b.astype(BF16), dims, preferred_element_type=F32)


_NT = (((1,), (1,)), ((), ()))
_TN = (((0,), (0,)), ((), ()))


def _sigmoid_pair(x):
    e = jnp.exp(-jnp.abs(x))
    r = 1.0 / (1.0 + e)
    er = e * r
    pos = x >= 0
    return jnp.where(pos, r, er), jnp.where(pos, er, r)


def _split_dot(x, u, parts):
    acc = None
    rem = x
    for _ in range(parts):
        p = rem.astype(BF16)
        rem = rem - p.astype(F32)
        t = lax.dot_general(p, u, (((1,), (0,)), ((), ())), preferred_element_type=F32)
        acc = t if acc is None else acc + t
    return acc


def _split_dot_left(u, x, parts):
    acc = None
    rem = x
    for _ in range(parts):
        p = rem.astype(BF16)
        rem = rem - p.astype(F32)
        t = lax.dot_general(u, p, (((1,), (0,)), ((), ())), preferred_element_type=F32)
        acc = t if acc is None else acc + t
    return acc


def _peer(x, y, c, rel):
    return (x ^ ((rel >> 2) & 1), y ^ ((rel >> 1) & 1), c ^ (rel & 1))


def _exchange(gather, srcs, dsts, send_sems, recv_sems, local_sems, phase):
    x, y, c = lax.axis_index("x"), lax.axis_index("y"), lax.axis_index("c")
    me = 4 * x + 2 * y + c
    for t in range(len(srcs)):
        own = srcs[t] if gather else srcs[t].at[me]
        local = pltpu.make_async_copy(own, dsts[t].at[me], local_sems.at[t])
        if phase == "start":
            local.start()
        for rel in range(1, N_DEV):
            px, py, pc = _peer(x, y, c, rel)
            pid = 4 * px + 2 * py + pc
            k = t * (N_DEV - 1) + rel - 1
            if phase == "start":
                pltpu.make_async_remote_copy(
                    src_ref=srcs[t] if gather else srcs[t].at[pid], dst_ref=dsts[t].at[me],
                    send_sem=send_sems.at[k], recv_sem=recv_sems.at[k],
                    device_id=(px, py, pc), device_id_type=MESH).start()
            else:
                cp = pltpu.make_async_remote_copy(
                    src_ref=own, dst_ref=dsts[t].at[pid], send_sem=send_sems.at[k], recv_sem=recv_sems.at[k],
                    device_id=(px, py, pc), device_id_type=MESH)
                cp.wait_recv()
                cp.wait_send()
        if phase == "wait":
            local.wait()


def _gather_two_level(srcs, dsts, send_sems, recv_sems, local_sems, phase):
    x, y, c = lax.axis_index("x"), lax.axis_index("y"), lax.axis_index("c")
    me, sibling = (x, y, c), (x, y, 1 - c)
    chips = [(1 - x, y), (x, 1 - y), (1 - x, 1 - y)]
    per = N_DEV - 1
    for t in range(len(srcs)):
        def slot(px, py, pc, t=t):
            return dsts[t].at[4 * px + 2 * py + pc]

        def copy(k, block, to, src=None, t=t):
            return pltpu.make_async_remote_copy(
                src_ref=slot(*block) if src is None else src, dst_ref=slot(*block),
                send_sem=send_sems.at[t * per + k], recv_sem=recv_sems.at[t * per + k],
                device_id=to, device_id_type=MESH)

        local = pltpu.make_async_copy(srcs[t], slot(*me), local_sems.at[t])
        first = [copy(0, me, sibling, src=srcs[t])]
        first += [copy(1 + j, me, (*chip, c), src=srcs[t]) for j, chip in enumerate(chips)]
        passed = [copy(4 + j, (*chip, c), sibling) for j, chip in enumerate(chips)]
        if phase == "start":
            local.start()
            for cp in first:
                cp.start()
        elif phase == "forward":
            for j, chip in enumerate(chips):
                copy(1 + j, (*chip, c), me).wait_recv()
                passed[j].start()
        else:
            copy(0, sibling, me).wait_recv()
            for j, chip in enumerate(chips):
                copy(4 + j, (*chip, 1 - c), me).wait_recv()
            for cp in first + passed:
                cp.wait_send()
            local.wait()


def _exchange_scratch(n):
    return [pltpu.SemaphoreType.DMA(((N_DEV - 1) * n,)), pltpu.SemaphoreType.DMA(((N_DEV - 1) * n,)),
            pltpu.SemaphoreType.DMA((n,))]


def _pcall(body, *, grid, in_specs, out_specs, out_shape, scratch_shapes=(), semantics, name, args, side=None):
    single = not isinstance(out_shape, (tuple, list))
    if single:
        out_specs, out_shape = [out_specs], [out_shape]
    in_specs, out_specs, out_shape = list(in_specs), list(out_specs), list(out_shape)
    scratch_shapes = list(scratch_shapes)
    n_in, n_out, n_scr = len(in_specs), len(out_specs), len(scratch_shapes)
    if side is None:
        res = pl.pallas_call(body, grid=grid, in_specs=in_specs, out_specs=out_specs, out_shape=out_shape,
                             scratch_shapes=scratch_shapes, compiler_params=_params(semantics), name=name)(*args)
        return (res[0] if single else tuple(res)), None
    gather, srcs = side
    n = len(srcs)

    def full(*refs):
        ins = refs[:n_in]
        s_in = refs[n_in:n_in + n]
        outs = refs[n_in + n:n_in + n + n_out]
        s_out = refs[n_in + n + n_out:n_in + 2 * n + n_out]
        scr = refs[n_in + 2 * n + n_out:n_in + 2 * n + n_out + n_scr]
        send_sems, recv_sems, local_sems = refs[n_in + 2 * n + n_out + n_scr:]
        step = pl.program_id(0)
        steps = grid[0]
        for ax in range(1, len(grid)):
            step = step * grid[ax] + pl.program_id(ax)
            steps *= grid[ax]

        def exchange(phase):
            if gather:
                _gather_two_level(s_in, s_out, send_sems, recv_sems, local_sems, phase)
            elif phase != "forward":
                _exchange(False, s_in, s_out, send_sems, recv_sems, local_sems, phase)

        pl.when(step == 0)(lambda: exchange("start"))
        if gather:
            pl.when(step == (steps * 4) // 5)(lambda: exchange("forward"))
        body(*ins, *outs, *scr)
        pl.when(step == steps - 1)(lambda: exchange("wait"))

    any_spec = pl.BlockSpec(memory_space=pl.ANY)
    s_shapes = [jax.ShapeDtypeStruct(((N_DEV,) + s.shape) if gather else s.shape, s.dtype) for s in srcs]
    res = pl.pallas_call(full, grid=grid, in_specs=in_specs + [any_spec] * n,
                         out_specs=out_specs + [any_spec] * n, out_shape=out_shape + s_shapes,
                         scratch_shapes=scratch_shapes + _exchange_scratch(n),
                         compiler_params=_params(("arbitrary",) * len(grid)), name=name)(*args, *srcs)
    main = res[:n_out]
    return (main[0] if single else tuple(main)), list(res[n_out:])


def _mm_nn(a, b, *, tm, nb=None, out_dtype=F32, resid=None, gate=None, norm=None, name, side=None):
    M, K = a.shape
    if b.ndim == 3:
        NB, _, n = b.shape
        b_spec = pl.BlockSpec((None, K, n), lambda j, i: (j, 0, 0))
    else:
        n = nb
        NB = b.shape[1] // nb
        b_spec = pl.BlockSpec((K, n), lambda j, i: (0, j))
    N = NB * n
    epi = resid is not None
    assert norm is None or (epi and NB == 1)

    def body(*refs):
        if norm is not None:
            a_ref, b_ref, r_ref, g_ref, ng_ref, sc_ref, sh_ref, o_ref, acc_ref, h_ref = refs
        elif epi:
            a_ref, b_ref, r_ref, g_ref, o_ref, acc_ref = refs
        else:
            a_ref, b_ref, o_ref = refs
        acc = jnp.dot(a_ref[...], b_ref[...], preferred_element_type=F32)
        if epi:
            xv = r_ref[...] + g_ref[...] * acc
            o_ref[...] = xv
            acc_ref[...] = acc.astype(BF16)
            if norm is not None:
                rstd = lax.rsqrt(jnp.mean(xv * xv, axis=-1, keepdims=True) + EPS)
                y = (xv * rstd) * ng_ref[...]
                h_ref[...] = (y * (1.0 + sc_ref[...]) + sh_ref[...]).astype(BF16)
        else:
            o_ref[...] = acc.astype(out_dtype)

    in_specs = [pl.BlockSpec((tm, K), lambda j, i: (i, 0)), b_spec]
    args = [a, b]
    o_spec = pl.BlockSpec((tm, n), lambda j, i: (i, j))
    if epi:
        row = pl.BlockSpec((1, n), lambda j, i: (0, j))
        in_specs += [pl.BlockSpec((tm, n), lambda j, i: (i, j)), row]
        args += [resid, gate]
        out_shape = [jax.ShapeDtypeStruct((M, N), F32), jax.ShapeDtypeStruct((M, N), BF16)]
        out_specs = [o_spec, o_spec]
        if norm is not None:
            in_specs += [row, row, row]
            args += list(norm)
            out_shape.append(jax.ShapeDtypeStruct((M, N), BF16))
            out_specs.append(o_spec)
    else:
        out_shape = [jax.ShapeDtypeStruct((M, N), out_dtype)]
        out_specs = [o_spec]
    res, got = _pcall(body, grid=(NB, M // tm), in_specs=in_specs, out_specs=out_specs, out_shape=out_shape,
                      semantics=("parallel", "parallel"), name=name, args=args, side=side)
    return (res[0] if len(res) == 1 else res), got


def _halves(dy, dy2, blk_rows, blk_cols, nblocks, row_of, col_of, last_row=None):
    if dy2 is None:
        return [pl.BlockSpec((blk_rows, blk_cols), lambda *g: (row_of(*g), col_of(*g)))], None
    half = nblocks // 2

    def left(*g):
        r, c = row_of(*g), col_of(*g)
        if last_row is None:
            return (r, jnp.minimum(c, half - 1))
        return (jnp.where(c < half, r, last_row), jnp.minimum(c, half - 1))

    def right(*g):
        r, c = row_of(*g), col_of(*g)
        if last_row is None:
            return (r, jnp.maximum(c - half, 0))
        return (jnp.where(c >= half, r, 0), jnp.maximum(c - half, 0))

    return [pl.BlockSpec((blk_rows, blk_cols), left), pl.BlockSpec((blk_rows, blk_cols), right)], half


def _mm_nt(dy, w, *, tm, kb, nb=None, name, side=None, dy2=None):
    M = dy.shape[0]
    N = dy.shape[1] * (1 if dy2 is None else 2)
    if w.ndim == 3:
        NB, Kt, n = w.shape
        w_spec = pl.BlockSpec((None, kb, n), lambda i, k, j: (j, k, 0))
    else:
        Kt = w.shape[0]
        n = nb
        NB = N // nb
        w_spec = pl.BlockSpec((kb, n), lambda i, k, j: (k, j))
    KB = Kt // kb
    dy_specs, half = _halves(dy, dy2, tm, n, NB, lambda i, k, j: i, lambda i, k, j: j)
    n_op = len(dy_specs)

    def body(*refs):
        dy_refs = refs[:n_op]
        w_ref = refs[n_op]
        o_ref = refs[n_op + 1]
        acc = refs[n_op + 2:]
        j = pl.program_id(2)

        def use(dy_ref):
            part = lax.dot_general(dy_ref[...], w_ref[...], _NT, preferred_element_type=F32)
            if NB == 1:
                o_ref[...] = part
                return
            acc_ref, = acc

            @pl.when(j == 0)
            def _():
                acc_ref[...] = part

            @pl.when(jnp.logical_and(j > 0, j < NB - 1))
            def _():
                acc_ref[...] += part

            @pl.when(j == NB - 1)
            def _():
                o_ref[...] = acc_ref[...] + part

        if half is None:
            use(dy_refs[0])
        else:
            pl.when(j < half)(lambda: use(dy_refs[0]))
            pl.when(j >= half)(lambda: use(dy_refs[1]))

    return _pcall(
        body, grid=(M // tm, KB, NB), in_specs=dy_specs + [w_spec],
        out_specs=pl.BlockSpec((tm, kb), lambda i, k, j: (i, k)),
        out_shape=jax.ShapeDtypeStruct((M, Kt), F32),
        scratch_shapes=[] if NB == 1 else [pltpu.VMEM((tm, kb), F32)],
        semantics=("parallel", "parallel", "arbitrary"), name=name,
        args=(dy, w) if dy2 is None else (dy, dy2, w), side=side)


def _mm_tn(x, dy, *, tm, kb, nb, blocked, name, side=None, dy2=None):
    M, K = x.shape
    N = dy.shape[1] * (1 if dy2 is None else 2)
    KB, NB, MB = K // kb, N // nb, M // tm
    dy_specs, half = _halves(dy, dy2, tm, nb, NB, lambda k, n, m: m, lambda k, n, m: n, last_row=MB - 1)
    n_dy = len(dy_specs)

    def body(*refs):
        x_ref = refs[0]
        dy_refs = refs[1:1 + n_dy]
        o_ref = refs[1 + n_dy]
        acc = refs[2 + n_dy:]
        m = pl.program_id(2)

        def use(dy_ref):
            part = lax.dot_general(x_ref[...], dy_ref[...], _TN, preferred_element_type=F32)
            if MB == 1:
                o_ref[...] = part.astype(BF16)
                return
            acc_ref, = acc

            @pl.when(m == 0)
            def _():
                acc_ref[...] = part

            @pl.when(jnp.logical_and(m > 0, m < MB - 1))
            def _():
                acc_ref[...] += part

            @pl.when(m == MB - 1)
            def _():
                o_ref[...] = (acc_ref[...] + part).astype(BF16)

        if half is None:
            use(dy_refs[0])
        else:
            nblk = pl.program_id(1)
            pl.when(nblk < half)(lambda: use(dy_refs[0]))
            pl.when(nblk >= half)(lambda: use(dy_refs[1]))

    if blocked:
        out_shape = jax.ShapeDtypeStruct((NB, K, nb), BF16)
        o_spec = pl.BlockSpec((None, kb, nb), lambda k, n, m: (n, k, 0))
    else:
        out_shape = jax.ShapeDtypeStruct((K, N), BF16)
        o_spec = pl.BlockSpec((kb, nb), lambda k, n, m: (k, n))
    x_spec = pl.BlockSpec((tm, kb), lambda k, n, m: (m, k))
    return _pcall(
        body, grid=(KB, NB, MB), in_specs=[x_spec] + dy_specs,
        out_specs=o_spec, out_shape=out_shape,
        scratch_shapes=[] if MB == 1 else [pltpu.VMEM((kb, nb), F32)],
        semantics=("parallel", "parallel", "arbitrary"), name=name,
        args=(x, dy) if dy2 is None else (x, dy, dy2), side=side)


def _ffn_in_fwd(h, w, *, tm, nb, name, side=None):
    M, K = h.shape
    if w.ndim == 3:
        J, _, n = w.shape
        half = J // 2
        specs = [pl.BlockSpec((None, K, n), lambda j, i: (j, 0, 0)),
                 pl.BlockSpec((None, K, n), lambda j, i: (j + half, 0, 0))]
    else:
        n = nb
        half = w.shape[1] // (2 * nb)
        specs = [pl.BlockSpec((K, n), lambda j, i: (0, j)), pl.BlockSpec((K, n), lambda j, i: (0, j + half))]
    F = half * n

    def body(h_ref, wg_ref, wu_ref, gate_ref, up_ref, act_ref):
        hv = h_ref[...]
        gate = jnp.dot(hv, wg_ref[...], preferred_element_type=F32)
        up = jnp.dot(hv, wu_ref[...], preferred_element_type=F32)
        s, _ = _sigmoid_pair(gate)
        gate_ref[...] = gate
        up_ref[...] = up
        act_ref[...] = (gate * s * up).astype(BF16)

    o_spec = pl.BlockSpec((tm, n), lambda j, i: (i, j))
    f32 = jax.ShapeDtypeStruct((M, F), F32)
    return _pcall(
        body, grid=(half, M // tm), in_specs=[pl.BlockSpec((tm, K), lambda j, i: (i, 0))] + specs,
        out_specs=(o_spec, o_spec, o_spec), out_shape=(f32, f32, jax.ShapeDtypeStruct((M, F), BF16)),
        semantics=("parallel", "parallel"), name=name, args=(h, w, w), side=side)


def _ffn_out_bwd_x(dy, w, gate, up, *, tm, kb, name, side=None):
    M, D = dy.shape
    F = w.shape[0]

    def body(dy_ref, w_ref, g_ref, u_ref, dg_ref, du_ref):
        da = lax.dot_general(dy_ref[...], w_ref[...], _NT, preferred_element_type=F32)
        gate = g_ref[...]
        s, ns = _sigmoid_pair(gate)
        dg_ref[...] = (da * u_ref[...] * (s * (1.0 + gate * ns))).astype(BF16)
        du_ref[...] = (da * (gate * s)).astype(BF16)

    tile = pl.BlockSpec((tm, kb), lambda i, k: (i, k))
    act = jax.ShapeDtypeStruct((M, F), BF16)
    return _pcall(
        body, grid=(M // tm, F // kb),
        in_specs=[pl.BlockSpec((tm, D), lambda i, k: (i, 0)), pl.BlockSpec((kb, D), lambda i, k: (k, 0)), tile, tile],
        out_specs=(tile, tile), out_shape=(act, act),
        semantics=("parallel", "parallel"), name=name, args=(dy, w, gate, up), side=side)


def _modnorm_fwd(x, gain, sc, sh, *, tm, name):
    S, D = x.shape

    def body(x_ref, g_ref, sc_ref, sh_ref, h_ref):
        xv = x_ref[...]
        rstd = lax.rsqrt(jnp.mean(xv * xv, axis=-1, keepdims=True) + EPS)
        y = (xv * rstd) * g_ref[...]
        h_ref[...] = (y * (1.0 + sc_ref[...]) + sh_ref[...]).astype(BF16)

    row = pl.BlockSpec((1, D), lambda i: (0, 0))
    return pl.pallas_call(
        body, grid=(S // tm,),
        in_specs=[pl.BlockSpec((tm, D), lambda i: (i, 0)), row, row, row],
        out_specs=pl.BlockSpec((tm, D), lambda i: (i, 0)),
        out_shape=jax.ShapeDtypeStruct((S, D), BF16),
        compiler_params=_params(("parallel",)), name=name)(x, gain, sc, sh)


def _modnorm_bwd(x, dh, dres, gain, sc, branch, gate, *, tm, name):
    S, D = x.shape
    has_prev = branch is not None

    def body(*refs):
        if has_prev:
            (x_ref, dh_ref, dr_ref, g_ref, sc_ref, br_ref, gt_ref,
             dx_ref, dbr_ref, dgt_ref, dsh_ref, dsc_ref, dgn_ref) = refs
        else:
            (x_ref, dh_ref, dr_ref, g_ref, sc_ref,
             dx_ref, dsh_ref, dsc_ref, dgn_ref) = refs
        i = pl.program_id(0)
        xv = x_ref[...]
        dh_v = dh_ref[...]
        gv = g_ref[...]
        scale1 = 1.0 + sc_ref[...]
        rstd = lax.rsqrt(jnp.mean(xv * xv, axis=-1, keepdims=True) + EPS)
        n = xv * rstd
        dn = dh_v * (gv * scale1)
        dx = rstd * (dn - n * jnp.mean(dn * n, axis=-1, keepdims=True)) + dr_ref[...]
        dx_ref[...] = dx
        dhn = dh_v * n
        p_sh = jnp.sum(dh_v, axis=0, keepdims=True)
        p_sc = jnp.sum(dhn, axis=0, keepdims=True) * gv
        p_gn = jnp.sum(dhn, axis=0, keepdims=True) * scale1
        if has_prev:
            dbr_ref[...] = (gt_ref[...] * dx).astype(BF16)
            p_gt = jnp.sum(dx * br_ref[...].astype(F32), axis=0, keepdims=True)

        @pl.when(i == 0)
        def _():
            dsh_ref[...] = p_sh
            dsc_ref[...] = p_sc
            dgn_ref[...] = p_gn
            if has_prev:
                dgt_ref[...] = p_gt

        @pl.when(i > 0)
        def _():
            dsh_ref[...] += p_sh
            dsc_ref[...] += p_sc
            dgn_ref[...] += p_gn
            if has_prev:
                dgt_ref[...] += p_gt

    tile = pl.BlockSpec((tm, D), lambda i: (i, 0))
    row = pl.BlockSpec((1, D), lambda i: (0, 0))
    row_shape = jax.ShapeDtypeStruct((1, D), F32)
    if has_prev:
        in_specs = [tile, tile, tile, row, row, tile, row]
        args = (x, dh, dres, gain, sc, branch, gate)
        out_specs = (tile, tile, row, row, row, row)
        out_shape = (jax.ShapeDtypeStruct((S, D), F32), jax.ShapeDtypeStruct((S, D), BF16),
                     row_shape, row_shape, row_shape, row_shape)
    else:
        in_specs = [tile, tile, tile, row, row]
        args = (x, dh, dres, gain, sc)
        out_specs = (tile, row, row, row)
        out_shape = (jax.ShapeDtypeStruct((S, D), F32), row_shape, row_shape, row_shape)
    return pl.pallas_call(body, grid=(S // tm,), in_specs=in_specs, out_specs=out_specs,
                          out_shape=out_shape, compiler_params=_params(("arbitrary",)),
                          name=name)(*args)


def _loss_bwd(y, target, branch, gate, *, tm, name):
    S, D = y.shape
    nsteps = S // tm

    def body(y_ref, t_ref, br_ref, gt_ref, dy_ref, dbr_ref, dgt_ref, loss_ref, col_ref):
        i = pl.program_id(0)
        diff = y_ref[...] - t_ref[...]
        dy = diff * (1.0 / D)
        dy_ref[...] = dy
        dbr_ref[...] = (gt_ref[...] * dy).astype(BF16)
        p_gt = jnp.sum(dy * br_ref[...].astype(F32), axis=0, keepdims=True)
        p_col = jnp.sum(diff * diff, axis=0, keepdims=True)

        @pl.when(i == 0)
        def _():
            dgt_ref[...] = p_gt
            col_ref[...] = p_col

        @pl.when(i > 0)
        def _():
            dgt_ref[...] += p_gt
            col_ref[...] += p_col

        @pl.when(i == nsteps - 1)
        def _():
            tot = jnp.sum(col_ref[...], axis=-1, keepdims=True) * (0.5 / D)
            loss_ref[...] = jnp.broadcast_to(tot, (1, 128))

    tile = pl.BlockSpec((tm, D), lambda i: (i, 0))
    row = pl.BlockSpec((1, D), lambda i: (0, 0))
    return pl.pallas_call(
        body, grid=(nsteps,), in_specs=[tile, tile, tile, row],
        out_specs=(tile, tile, row, pl.BlockSpec((1, 128), lambda i: (0, 0))),
        out_shape=(jax.ShapeDtypeStruct((S, D), F32), jax.ShapeDtypeStruct((S, D), BF16),
                   jax.ShapeDtypeStruct((1, D), F32), jax.ShapeDtypeStruct((1, 128), F32)),
        scratch_shapes=[pltpu.VMEM((1, D), F32)],
        compiler_params=_params(("arbitrary",)), name=name)(y, target, branch, gate)


def _hg_chunk(q, fl, lbv, tri):
    C = q.shape[0]
    sq, nsq = _sigmoid_pair(q)
    qa = q * sq
    sig, nsig = _sigmoid_pair(fl)
    one_lb = 1.0 - lbv
    f = lbv + one_lb * sig
    fc = jnp.maximum(f, TINY)
    lf = jnp.log(fc)
    k = one_lb * nsig
    b = _split_dot_left(tri, lf, 3)
    row = lax.broadcasted_iota(jnp.int32, b.shape, 0)
    bm = jnp.sum(jnp.where(row == C // 2 - 1, b, 0.0), axis=0, keepdims=True)
    bl = jnp.sum(jnp.where(row == C - 1, b, 0.0), axis=0, keepdims=True)
    eb = jnp.exp(b)
    ebm = jnp.exp(b - bm)
    enbm = jnp.exp(bm - b)
    ebl = jnp.exp(bl - b)
    ebL = jnp.exp(bl)

    def operand(t):
        return t.astype(BF16).astype(F32)

    return dict(sq=sq, nsq=nsq, qa=qa, sig=sig, nsig=nsig, one_lb=one_lb, f=f, fc=fc, k=k,
                eb=eb, ebm=ebm, enbm=enbm, ebl=ebl, ebL=ebL,
                Qm=operand(qa * ebm), Km=operand(k * enbm), Qb=operand(qa * eb), Kh=operand(k * ebl), row=row)


def _causal_incl(C):
    r = lax.broadcasted_iota(jnp.int32, (C, C), 0)
    c = lax.broadcasted_iota(jnp.int32, (C, C), 1)
    return r >= c


def _hg_fwd(proj, lb, out_g, *, n_heads, cg, name, side=None):
    S = proj.shape[0]
    H = n_heads
    W = H * HEAD
    T = cg * CHUNK
    NG = S // T
    tri = jnp.tril(jnp.ones((CHUNK, CHUNK), F32)).astype(BF16)

    def body(q_ref, f_ref, v_ref, g_ref, lb_ref, og_ref, tri_ref, o_ref, on_ref, st_ref, s_scr):
        @pl.when(pl.program_id(1) == 0)
        def _():
            s_scr[...] = jnp.zeros_like(s_scr)

        lbv = lb_ref[...]
        ogv = og_ref[...]
        triv = tri_ref[...]
        mask = _causal_incl(CHUNK)
        for c in range(cg):
            rows = pl.ds(c * CHUNK, CHUNK)
            v = v_ref[rows, :]
            gg = g_ref[rows, :]
            cm = _hg_chunk(q_ref[rows, :], f_ref[rows, :], lbv, triv)
            s0 = s_scr[...]
            st_ref[c] = s0
            A = jnp.where(mask, _bdot(cm["Qm"], cm["Km"], _NT), 0.0)
            o = _bdot(A, v) + _bdot(cm["Qb"], s0, _NT)
            s_scr[...] = s0 * cm["ebL"] + _bdot(v, cm["Kh"], _TN)
            o_ref[rows, :] = o
            rstd = lax.rsqrt(jnp.mean(o * o, axis=-1, keepdims=True) + EPS)
            sg, _ = _sigmoid_pair(gg)
            on_ref[rows, :] = (((o * rstd) * ogv) * (gg * sg)).astype(BF16)

    def col(group):
        return pl.BlockSpec((T, HEAD), lambda h, g: (g, group * H + h))

    vec = pl.BlockSpec((1, HEAD), lambda h, g: (0, h))
    return _pcall(
        body, grid=(H, NG),
        in_specs=[col(0), col(1), col(2), col(3), vec, vec,
                  pl.BlockSpec((CHUNK, CHUNK), lambda h, g: (0, 0))],
        out_specs=(pl.BlockSpec((T, HEAD), lambda h, g: (g, h)),
                   pl.BlockSpec((T, HEAD), lambda h, g: (g, h)),
                   pl.BlockSpec((cg, None, HEAD, HEAD), lambda h, g: (g, h, 0, 0))),
        out_shape=(jax.ShapeDtypeStruct((S, W), F32), jax.ShapeDtypeStruct((S, W), BF16),
                   jax.ShapeDtypeStruct((S // CHUNK, H, HEAD, HEAD), F32)),
        scratch_shapes=[pltpu.VMEM((HEAD, HEAD), F32)],
        semantics=("parallel", "arbitrary"), name=name,
        args=(proj, proj, proj, proj, lb, out_g, tri), side=side)


def _hg_bwd(proj, o_pre, d_on, d_on_col0, states, lb, out_g, *, n_heads, cg, name, side=None):
    S = proj.shape[0]
    H = n_heads
    W = H * HEAD
    T = cg * CHUNK
    NG = S // T
    tri = jnp.tril(jnp.ones((CHUNK, CHUNK), F32)).astype(BF16)
    triu = jnp.triu(jnp.ones((CHUNK, CHUNK), F32)).astype(BF16)

    def body(q_ref, f_ref, v_ref, g_ref, o_ref, dy_ref, st_ref, lb_ref, og_ref, tri_ref, triu_ref,
             dq_ref, df_ref, di_ref, dg_ref, dlb_ref, dog_ref, ds_scr):
        gstep = pl.program_id(1)

        @pl.when(gstep == 0)
        def _():
            ds_scr[...] = jnp.zeros_like(ds_scr)
            dlb_ref[...] = jnp.zeros_like(dlb_ref)
            dog_ref[...] = jnp.zeros_like(dog_ref)

        lbv = lb_ref[...]
        ogv = og_ref[...]
        triv = tri_ref[...]
        triuv = triu_ref[...]
        mask = _causal_incl(CHUNK)
        dlb_acc = jnp.zeros((1, HEAD), F32)
        dog_acc = jnp.zeros((1, HEAD), F32)
        for c in reversed(range(cg)):
            rows = pl.ds(c * CHUNK, CHUNK)
            q = q_ref[rows, :]
            v = v_ref[rows, :]
            gg = g_ref[rows, :]
            o = o_ref[rows, :]
            dy = dy_ref[rows, :]
            cm = _hg_chunk(q, f_ref[rows, :], lbv, triv)
            s0 = st_ref[c]
            ds1 = ds_scr[...]
            rstd = lax.rsqrt(jnp.mean(o * o, axis=-1, keepdims=True) + EPS)
            n = o * rstd
            sg, nsg = _sigmoid_pair(gg)
            silu_g = gg * sg
            dyn = dy * n
            dog_acc = dog_acc + jnp.sum(dyn * silu_g, axis=0, keepdims=True)
            dg_ref[rows, :] = (dyn * ogv * (sg * (1.0 + gg * nsg))).astype(BF16)
            dn = dy * (ogv * silu_g)
            d_o = rstd * (dn - n * jnp.mean(dn * n, axis=-1, keepdims=True))
            A = jnp.where(mask, _bdot(cm["Qm"], cm["Km"], _NT), 0.0)
            dA = jnp.where(mask, _bdot(d_o, v, _NT), 0.0)
            dV = _bdot(A, d_o, _TN) + _bdot(cm["Kh"], ds1, _NT)
            dQm = _bdot(dA, cm["Km"])
            dKm = _bdot(dA, cm["Qm"], _TN)
            dQb = _bdot(d_o, s0)
            dKh = _bdot(v, ds1)
            ds_scr[...] = ds1 * cm["ebL"] + _bdot(d_o, cm["Qb"], _TN)
            kh_term = dKh * cm["Kh"]
            db = dQm * cm["Qm"] - dKm * cm["Km"] + dQb * cm["Qb"] - kh_term
            dbl = (jnp.sum(kh_term, axis=0, keepdims=True)
                   + cm["ebL"] * jnp.sum(ds1 * s0, axis=0, keepdims=True))
            db = db + jnp.where(cm["row"] == CHUNK - 1, dbl, 0.0)
            dlf = _split_dot_left(triuv, db, 3)
            dqa = dQm * cm["ebm"] + dQb * cm["eb"]
            dq_ref[rows, :] = (dqa * (cm["sq"] * (1.0 + q * cm["nsq"]))).astype(BF16)
            dk = dKm * cm["enbm"] + dKh * cm["ebl"]
            dfc = jnp.where(cm["f"] > TINY, dlf / cm["fc"], 0.0)
            t = dfc - dk
            df_ref[rows, :] = (t * (cm["one_lb"] * cm["sig"] * cm["nsig"])).astype(BF16)
            dlb_acc = dlb_acc + jnp.sum(t * cm["nsig"], axis=0, keepdims=True)
            di_ref[rows, :] = dV.astype(BF16)
        dlb_ref[...] += dlb_acc
        dog_ref[...] += dog_acc

    def col(group):
        return pl.BlockSpec((T, HEAD), lambda h, g: (NG - 1 - g, group * H + h))

    own = pl.BlockSpec((T, HEAD), lambda h, g: (NG - 1 - g, h))
    vec = pl.BlockSpec((1, HEAD), lambda h, g: (0, h))
    cst = pl.BlockSpec((CHUNK, CHUNK), lambda h, g: (0, 0))
    act = jax.ShapeDtypeStruct((S, W), BF16)
    vec_shape = jax.ShapeDtypeStruct((1, W), F32)
    return _pcall(
        body, grid=(H, NG),
        in_specs=[col(0), col(1), col(2), col(3), own,
                  pl.BlockSpec((T, HEAD), lambda h, g: (NG - 1 - g, d_on_col0 + h)),
                  pl.BlockSpec((cg, None, HEAD, HEAD), lambda h, g: (NG - 1 - g, h, 0, 0)),
                  vec, vec, cst, cst],
        out_specs=(own, own, own, own, vec, vec),
        out_shape=(act, act, act, act, vec_shape, vec_shape),
        scratch_shapes=[pltpu.VMEM((HEAD, HEAD), F32)],
        semantics=("parallel", "arbitrary"), name=name,
        args=(proj, proj, proj, proj, o_pre, d_on, states, lb, out_g, tri, triu), side=side)


def _sb_pre(proj, q_g, k_g, *, n_heads, col0, tm, name):
    S = proj.shape[0]
    H = n_heads
    W = H * HEAD

    def body(q_ref, k_ref, v_ref, qg_ref, kg_ref, qh_ref, kh_ref, vh_ref):
        for src, g_ref, dst in ((q_ref, qg_ref, qh_ref), (k_ref, kg_ref, kh_ref)):
            xv = src[...]
            rstd = lax.rsqrt(jnp.mean(xv * xv, axis=-1, keepdims=True) + EPS)
            dst[...] = ((xv * rstd) * g_ref[...]).astype(BF16)
        vh_ref[...] = v_ref[...].astype(BF16)

    def col(group):
        return pl.BlockSpec((tm, HEAD), lambda i, h: (i, col0 + group * H + h))

    vec = pl.BlockSpec((1, HEAD), lambda i, h: (0, 0))
    own = pl.BlockSpec((tm, HEAD), lambda i, h: (i, h))
    act = jax.ShapeDtypeStruct((S, W), BF16)
    return pl.pallas_call(
        body, grid=(S // tm, H), in_specs=[col(0), col(1), col(2), vec, vec],
        out_specs=(own, own, own), out_shape=(act, act, act),
        compiler_params=_params(("parallel", "parallel")), name=name)(proj, proj, proj, q_g, k_g)


def _sb_scores(q, k_blk, scale):
    z = lax.dot_general(q, k_blk, _NT, preferred_element_type=F32) * scale
    e = jnp.exp(-jnp.abs(z))
    sp = jnp.maximum(z, 0.0) + jnp.log(1.0 + e)
    return z, e, sp


def _heads_per_step(n_heads):
    return 2 if n_heads % 2 == 0 else 1


def _max_all(values):
    m = jnp.max(values[0])
    for v in values[1:]:
        m = jnp.maximum(m, jnp.max(v))
    return m


def _strict_lower_mask(t):
    r = lax.broadcasted_iota(jnp.int32, (t, t), 0)
    c = lax.broadcasted_iota(jnp.int32, (t, t), 1)
    return c < r


def _sb_fwd(qh, kh, vh, out_g, *, n_heads, tq, name, side=None):
    S, W = qh.shape
    HP = _heads_per_step(n_heads)
    WP = HP * HEAD
    NQ = S // tq
    scale = HEAD ** -0.5
    u_strict = jnp.tril(jnp.ones((tq, tq), F32), -1).astype(BF16)

    def body(q_ref, k_ref, v_ref, og_ref, u_ref, o_ref, on_ref):
        qi = pl.program_id(1)
        u = u_ref[...]
        heads = [slice(hh * HEAD, (hh + 1) * HEAD) for hh in range(HP)]
        qs = [q_ref[:, cols] for cols in heads]

        def block(kb, r_carry, diag, valid=None):
            rows = pl.ds(pl.multiple_of(kb * tq, tq), tq)
            pvs, rs = [], []
            for hh, cols in enumerate(heads):
                k_blk = k_ref[rows, cols]
                v_blk = v_ref[rows, cols]
                z, _, sp = _sb_scores(qs[hh], k_blk, scale)
                if diag:
                    m = _strict_lower_mask(tq)
                    L = jnp.where(m, -sp, 0.0)
                else:
                    L = -sp
                C = _split_dot(L, u, 2)
                a = jnp.exp(z - sp + C + r_carry[hh])
                if diag:
                    a = jnp.where(m, a, 0.0)
                if valid is not None:
                    a = jnp.where(valid, a, 0.0)
                pvs.append(lax.dot_general(a.astype(BF16), v_blk, (((1,), (0,)), ((), ())),
                                           preferred_element_type=F32))
                rs.append(r_carry[hh] + (C[:, 0:1] + L[:, 0:1]))
            return tuple(pvs), tuple(rs)

        acc_d, r_d = block(qi, (jnp.zeros((tq, 1), F32),) * HP, True)
        acc_p, r0 = block(jnp.maximum(qi - 1, 0), r_d, False, valid=qi > 0)
        acc0 = tuple(a + b for a, b in zip(acc_d, acc_p))

        def cond(st):
            kb, _, _, rmax = st
            return jnp.logical_and(kb >= 0, rmax > SKIP_LOG)

        def step(st):
            kb, acc, r, _ = st
            pv, r2 = block(kb, r, False)
            return kb - 1, tuple(a + b for a, b in zip(acc, pv)), r2, _max_all(r2)

        _, accs, _, _ = lax.while_loop(cond, step, (qi - 2, acc0, r0, _max_all(r0)))
        for hh, cols in enumerate(heads):
            acc = accs[hh]
            o_ref[:, cols] = acc
            rstd = lax.rsqrt(jnp.mean(acc * acc, axis=-1, keepdims=True) + EPS)
            on_ref[:, cols] = ((acc * rstd) * og_ref[:, cols]).astype(BF16)

    blk = pl.BlockSpec((tq, WP), lambda h, i: (i, h))
    full = pl.BlockSpec((S, WP), lambda h, i: (0, h))
    return _pcall(
        body, grid=(n_heads // HP, NQ),
        in_specs=[blk, full, full, pl.BlockSpec((1, WP), lambda h, i: (0, h)),
                  pl.BlockSpec((tq, tq), lambda h, i: (0, 0))],
        out_specs=(blk, blk),
        out_shape=(jax.ShapeDtypeStruct((S, W), F32), jax.ShapeDtypeStruct((S, W), BF16)),
        semantics=("parallel", "arbitrary"), name=name, args=(qh, kh, vh, out_g, u_strict), side=side)


def _sb_bwd(qh, kh, vh, o_pre, d_on, d_on_col0, out_g, *, n_heads, tq, name, side=None):
    S, W = qh.shape
    HP = _heads_per_step(n_heads)
    WP = HP * HEAD
    assert d_on_col0 % HP == 0
    NQ = S // tq
    scale = HEAD ** -0.5
    u_strict = jnp.tril(jnp.ones((tq, tq), F32), -1).astype(BF16)
    u_incl = jnp.tril(jnp.ones((tq, tq), F32)).astype(BF16)

    def body(q_ref, k_ref, v_ref, o_ref, dy_ref, og_ref, us_ref, ui_ref,
             dq_ref, dk_ref, dv_ref, dog_ref):
        qi = pl.program_id(1)

        @pl.when(qi == 0)
        def _():
            dk_ref[...] = jnp.zeros_like(dk_ref)
            dv_ref[...] = jnp.zeros_like(dv_ref)
            dog_ref[...] = jnp.zeros_like(dog_ref)

        us = us_ref[...]
        ui = ui_ref[...]
        heads = [slice(hh * HEAD, (hh + 1) * HEAD) for hh in range(HP)]
        qs, d_obs, deltas = [], [], []
        for cols in heads:
            qs.append(q_ref[:, cols])
            o = o_ref[:, cols]
            dy = dy_ref[:, cols]
            rstd = lax.rsqrt(jnp.mean(o * o, axis=-1, keepdims=True) + EPS)
            n = o * rstd
            dog_ref[:, cols] += jnp.sum(dy * n, axis=0, keepdims=True)
            dn = dy * og_ref[:, cols]
            d_o = rstd * (dn - n * jnp.mean(dn * n, axis=-1, keepdims=True))
            d_ob = d_o.astype(BF16)
            d_obs.append(d_ob)
            deltas.append(jnp.sum(d_ob.astype(F32) * o, axis=-1, keepdims=True))
        q_ts = [q.T for q in qs]
        d_ob_ts = [d.T for d in d_obs]

        def block(kb, r_carry, g_carry, diag, valid=None):
            rows = pl.ds(pl.multiple_of(kb * tq, tq), tq)
            dqs, rs, gs = [], [], []
            for hh, cols in enumerate(heads):
                k_blk = k_ref[rows, cols]
                v_blk = v_ref[rows, cols]
                z, e, sp = _sb_scores(qs[hh], k_blk, scale)
                if diag:
                    m = _strict_lower_mask(tq)
                    L = jnp.where(m, -sp, 0.0)
                else:
                    L = -sp
                C = _split_dot(L, us, 2)
                a = jnp.exp(z - sp + C + r_carry[hh])
                if diag:
                    a = jnp.where(m, a, 0.0)
                if valid is not None:
                    a = jnp.where(valid, a, 0.0)
                ab = a.astype(BF16)
                dA = lax.dot_general(d_obs[hh], v_blk, _NT, preferred_element_type=F32)
                G = ab.astype(F32) * dA
                SI = _split_dot(G, ui, 2)
                P = deltas[hh] - (g_carry[hh] + SI)
                r = 1.0 / (1.0 + e)
                sig = jnp.where(z >= 0, r, e * r)
                dz = G - (G + P) * sig
                if diag:
                    dz = jnp.where(m, dz, 0.0)
                if valid is not None:
                    dz = jnp.where(valid, dz, 0.0)
                dzb = (dz * scale).astype(BF16)
                dqs.append(lax.dot_general(dzb, k_blk, (((1,), (0,)), ((), ())), preferred_element_type=F32))
                dk_ref[cols, rows] += jnp.dot(q_ts[hh], dzb, preferred_element_type=F32)
                dv_ref[cols, rows] += jnp.dot(d_ob_ts[hh], ab, preferred_element_type=F32)
                rs.append(r_carry[hh] + (C[:, 0:1] + L[:, 0:1]))
                gs.append(g_carry[hh] + SI[:, 0:1])
            return tuple(dqs), tuple(rs), tuple(gs)

        zero = (jnp.zeros((tq, 1), F32),) * HP
        dq_d, r_d, g_d = block(qi, zero, zero, True)
        dq_p, r0, g0 = block(jnp.maximum(qi - 1, 0), r_d, g_d, False, valid=qi > 0)
        dq0 = tuple(a + b for a, b in zip(dq_d, dq_p))

        def cond(st):
            kb, _, _, _, rmax = st
            return jnp.logical_and(kb >= 0, rmax > SKIP_LOG)

        def step(st):
            kb, dq, r, g, _ = st
            dq_part, r2, g2 = block(kb, r, g, False)
            return kb - 1, tuple(a + b for a, b in zip(dq, dq_part)), r2, g2, _max_all(r2)

        _, dqs, _, _, _ = lax.while_loop(cond, step, (qi - 2, dq0, r0, g0, _max_all(r0)))
        for hh, cols in enumerate(heads):
            dq_ref[:, cols] = dqs[hh]

    blk = pl.BlockSpec((tq, WP), lambda h, i: (i, h))
    full = pl.BlockSpec((S, WP), lambda h, i: (0, h))
    vec = pl.BlockSpec((1, WP), lambda h, i: (0, h))
    cst = pl.BlockSpec((tq, tq), lambda h, i: (0, 0))
    act = jax.ShapeDtypeStruct((S, W), F32)
    act_t = jax.ShapeDtypeStruct((W, S), F32)
    full_t = pl.BlockSpec((WP, S), lambda h, i: (h, 0))
    return _pcall(
        body, grid=(n_heads // HP, NQ),
        in_specs=[blk, full, full, blk,
                  pl.BlockSpec((tq, WP), lambda h, i: (i, d_on_col0 // HP + h)), vec, cst, cst],
        out_specs=(blk, full_t, full_t, vec),
        out_shape=(act, act_t, act_t, jax.ShapeDtypeStruct((1, W), F32)),
        semantics=("parallel", "arbitrary"), name=name,
        args=(qh, kh, vh, o_pre, d_on, out_g, u_strict, u_incl), side=side)


def _sb_pre_bwd(proj, dqh, dkh, dvh, q_g, k_g, *, n_heads, col0, tm, name):
    S = proj.shape[0]
    H = n_heads
    W = H * HEAD

    def body(q_ref, k_ref, dqh_ref, dkh_ref, dvh_ref, qg_ref, kg_ref,
             dq_ref, dk_ref, dv_ref, dqg_ref, dkg_ref):
        first = jnp.logical_and(pl.program_id(0) == 0, pl.program_id(1) == 0)

        @pl.when(first)
        def _():
            dqg_ref[...] = jnp.zeros_like(dqg_ref)
            dkg_ref[...] = jnp.zeros_like(dkg_ref)

        for src, dh, g_ref, dst, dg_ref in ((q_ref, dqh_ref[...], qg_ref, dq_ref, dqg_ref),
                                            (k_ref, dkh_ref[...].T, kg_ref, dk_ref, dkg_ref)):
            xv = src[...]
            rstd = lax.rsqrt(jnp.mean(xv * xv, axis=-1, keepdims=True) + EPS)
            n = xv * rstd
            dg_ref[...] += jnp.sum(dh * n, axis=0, keepdims=True)
            dn = dh * g_ref[...]
            dst[...] = (rstd * (dn - n * jnp.mean(dn * n, axis=-1, keepdims=True))).astype(BF16)
        dv_ref[...] = dvh_ref[...].T.astype(BF16)

    def col(group):
        return pl.BlockSpec((tm, HEAD), lambda i, h: (i, col0 + group * H + h))

    vec = pl.BlockSpec((1, HEAD), lambda i, h: (0, 0))
    own = pl.BlockSpec((tm, HEAD), lambda i, h: (i, h))
    own_t = pl.BlockSpec((HEAD, tm), lambda i, h: (h, i))
    act = jax.ShapeDtypeStruct((S, W), BF16)
    vec_shape = jax.ShapeDtypeStruct((1, HEAD), F32)
    return pl.pallas_call(
        body, grid=(S // tm, H), in_specs=[col(0), col(1), own, own_t, own_t, vec, vec],
        out_specs=(own, own, own, vec, vec), out_shape=(act, act, act, vec_shape, vec_shape),
        compiler_params=_params(("arbitrary", "arbitrary")), name=name,
    )(proj, proj, dqh, dkh, dvh, q_g, k_g)


def _softmax_rows(x_ref, L):
    rows = [x_ref[l:l + 1, :] for l in range(L)]
    mx = rows[0]
    for r in rows[1:]:
        mx = jnp.maximum(mx, r)
    ex = [jnp.exp(r - mx) for r in rows]
    tot = ex[0]
    for e in ex[1:]:
        tot = tot + e
    return [e / tot for e in ex]


def _lb_fwd(logits, *, name):
    L, W = logits.shape

    def body(x_ref, o_ref):
        s = _softmax_rows(x_ref, L)
        run = jnp.zeros((1, W), F32)
        for l in range(L):
            run = run + s[l]
            o_ref[l:l + 1, :] = run - s[0]

    return pl.pallas_call(body, out_shape=jax.ShapeDtypeStruct((L, W), F32), name=name)(logits)


def _lb_bwd(logits, dlb_parts, *, name):
    L, W = logits.shape
    P = dlb_parts.shape[0]

    def body(x_ref, d_ref, o_ref):
        s = _softmax_rows(x_ref, L)
        dlb = []
        for l in range(L):
            t = d_ref[0, l:l + 1, :]
            for q in range(1, P):
                t = t + d_ref[q, l:l + 1, :]
            dlb.append(t)
        ds = [None] * L
        run = jnp.zeros((1, W), F32)
        for j in reversed(range(L)):
            run = run + dlb[j]
            ds[j] = run
        ds[0] = jnp.zeros((1, W), F32)
        inner = jnp.zeros((1, W), F32)
        for j in range(L):
            inner = inner + s[j] * ds[j]
        for j in range(L):
            o_ref[j:j + 1, :] = s[j] * (ds[j] - inner)

    return pl.pallas_call(body, out_shape=jax.ShapeDtypeStruct((L, W), F32), name=name)(logits, dlb_parts)


def _ada_mod(c_all, w_ada, *, nb, name):
    L, D, n = w_ada.shape
    B = c_all.shape[0]

    def body(c_ref, w_ref, o_ref, cond_ref):
        cv = c_ref[...]
        s, _ = _sigmoid_pair(cv)
        cond = cv * s
        cond_ref[...] = cond
        o_ref[...] = _bdot(cond, w_ref[...])

    return pl.pallas_call(
        body, grid=(L, n // nb),
        in_specs=[pl.BlockSpec((B, D), lambda l, j: (0, 0)),
                  pl.BlockSpec((None, D, nb), lambda l, j: (l, 0, j))],
        out_specs=(pl.BlockSpec((None, B, nb), lambda l, j: (l, 0, j)),
                   pl.BlockSpec((B, D), lambda l, j: (0, 0))),
        out_shape=(jax.ShapeDtypeStruct((L, B, n), F32), jax.ShapeDtypeStruct((B, D), F32)),
        compiler_params=_params(("arbitrary", "arbitrary")), name=name)(c_all, w_ada)


def _adam_math(w, g, m, v):
    m2 = ADAM_B1 * m + (1.0 - ADAM_B1) * g
    v2 = ADAM_B2 * v + (1.0 - ADAM_B2) * (g * g)
    m_hat = m2 / (1.0 - ADAM_B1 ** ADAM_STEP)
    v_hat = v2 / (1.0 - ADAM_B2 ** ADAM_STEP)
    delta = -ADAM_LR * (m_hat / (jnp.sqrt(v_hat) + ADAM_EPS) + ADAM_WD * w)
    return delta, m2, v2


def _adamw(w, m, v, gparts, *, tr, name):
    R, C = w.shape
    P = gparts.shape[0]

    def body(w_ref, m_ref, v_ref, gp_ref, g_ref, d_ref, m2_ref, v2_ref):
        g = gp_ref[0].astype(F32)
        for p in range(1, P):
            g = g + gp_ref[p].astype(F32)
        delta, m2, v2 = _adam_math(w_ref[...], g, m_ref[...], v_ref[...])
        g_ref[...] = g
        d_ref[...] = delta
        m2_ref[...] = m2
        v2_ref[...] = v2

    tile = pl.BlockSpec((tr, C), lambda i: (i, 0))
    shp = jax.ShapeDtypeStruct((R, C), F32)
    return pl.pallas_call(
        body, grid=(R // tr,),
        in_specs=[tile, tile, tile, pl.BlockSpec((P, tr, C), lambda i: (0, i, 0))],
        out_specs=(tile, tile, tile, tile), out_shape=(shp, shp, shp, shp),
        compiler_params=_params(("parallel",)), name=name)(w, m, v, gparts)


def _adamw_layers(w, m, v, gparts, *, tr, name, side=None):
    L, R, C = w.shape
    P = gparts[0].shape[0]
    nblk = R // tr

    def body(*refs):
        w_ref, m_ref, v_ref = refs[:3]
        gp_refs = refs[3:3 + L]
        g_ref, d_ref, m2_ref, v2_ref = refs[3 + L:]
        layer = pl.program_id(0)
        for t in range(L):
            @pl.when(layer == t)
            def _(t=t):
                g = gp_refs[t][0].astype(F32)
                for q in range(1, P):
                    g = g + gp_refs[t][q].astype(F32)
                delta, m2, v2 = _adam_math(w_ref[...], g, m_ref[...], v_ref[...])
                g_ref[...] = g
                d_ref[...] = delta
                m2_ref[...] = m2
                v2_ref[...] = v2

    def gp_spec(t):
        def index(l, i):
            return (0, jnp.where(l == t, i, jnp.where(l < t, 0, nblk - 1)), 0)
        return pl.BlockSpec((P, tr, C), index)

    tile = pl.BlockSpec((None, tr, C), lambda l, i: (l, i, 0))
    shp = jax.ShapeDtypeStruct((L, R, C), F32)
    return _pcall(
        body, grid=(L, nblk), in_specs=[tile, tile, tile] + [gp_spec(t) for t in range(L)],
        out_specs=(tile, tile, tile, tile), out_shape=(shp, shp, shp, shp),
        semantics=("arbitrary", "arbitrary"), name=name, args=(w, m, v, *gparts), side=side)


def _adamw_ada(w, m, v, cond_t, dmod, *, tr, name):
    L, D, n = w.shape
    Bp = cond_t.shape[1]

    def body(w_ref, m_ref, v_ref, c_ref, dm_ref, g_ref, d_ref, m2_ref, v2_ref):
        g = _bdot(c_ref[...], dm_ref[...])
        delta, m2, v2 = _adam_math(w_ref[...], g, m_ref[...], v_ref[...])
        g_ref[...] = g
        d_ref[...] = delta
        m2_ref[...] = m2
        v2_ref[...] = v2

    tile = pl.BlockSpec((None, tr, n), lambda l, i: (l, i, 0))
    shp = jax.ShapeDtypeStruct((L, D, n), F32)
    return pl.pallas_call(
        body, grid=(L, D // tr),
        in_specs=[tile, tile, tile, pl.BlockSpec((tr, Bp), lambda l, i: (i, 0)),
                  pl.BlockSpec((None, Bp, n), lambda l, i: (l, 0, 0))],
        out_specs=(tile, tile, tile, tile), out_shape=(shp, shp, shp, shp),
        compiler_params=_params(("parallel", "parallel")), name=name)(w, m, v, cond_t, dmod)


def _allgather_small(block, *, name):
    R, C = block.shape

    def body(x_ref, out_ref, send_sems, recv_sems, local_sem):
        x, y, c = lax.axis_index("x"), lax.axis_index("y"), lax.axis_index("c")

        def rows(px, py, pc):
            return out_ref.at[pl.ds((4 * px + 2 * py + pc) * R, R), :]

        mine = pltpu.make_async_copy(x_ref, rows(x, y, c), local_sem)
        mine.start()
        sends = []
        for rel in range(1, N_DEV):
            to = _peer(x, y, c, rel)
            cp = pltpu.make_async_remote_copy(src_ref=x_ref, dst_ref=rows(x, y, c),
                                              send_sem=send_sems.at[rel - 1], recv_sem=recv_sems.at[rel - 1],
                                              device_id=to, device_id_type=MESH)
            cp.start()
            sends.append(cp)
        for rel in range(1, N_DEV):
            frm = _peer(x, y, c, rel)
            pltpu.make_async_remote_copy(src_ref=x_ref, dst_ref=rows(*frm),
                                         send_sem=send_sems.at[rel - 1], recv_sem=recv_sems.at[rel - 1],
                                         device_id=frm, device_id_type=MESH).wait_recv()
        for cp in sends:
            cp.wait_send()
        mine.wait()

    return pl.pallas_call(
        body, out_shape=jax.ShapeDtypeStruct((N_DEV * R, C), block.dtype),
        in_specs=[pl.BlockSpec(memory_space=pltpu.VMEM)],
        out_specs=pl.BlockSpec(memory_space=pltpu.VMEM),
        scratch_shapes=[pltpu.SemaphoreType.DMA((N_DEV - 1,)), pltpu.SemaphoreType.DMA((N_DEV - 1,)),
                        pltpu.SemaphoreType.DMA],
        compiler_params=pltpu.CompilerParams(vmem_limit_bytes=V7X_VMEM_LIMIT), name=name)(block)


def _allgather_hbm(shards, *, name):
    n = len(shards)

    def body(*refs):
        ins = refs[:n]
        outs = refs[n:2 * n]
        send_sems, recv_sems, local_sems = refs[2 * n:]
        x, y, c = lax.axis_index("x"), lax.axis_index("y"), lax.axis_index("c")
        sibling = (x, y, 1 - c)
        chips = [(1 - x, y), (x, 1 - y), (1 - x, 1 - y)]

        def slot(t, px, py, pc):
            return outs[t].at[4 * px + 2 * py + pc]

        def copy(t, k, block, to, src=None):
            return pltpu.make_async_remote_copy(
                src_ref=slot(t, *block) if src is None else src, dst_ref=slot(t, *block),
                send_sem=send_sems.at[t * 7 + k], recv_sem=recv_sems.at[t * 7 + k],
                device_id=to, device_id_type=MESH)

        me = (x, y, c)
        started = []
        mine = []
        for t in range(n):
            cp = pltpu.make_async_copy(ins[t], slot(t, *me), local_sems.at[t])
            cp.start()
            mine.append(cp)
            first = [copy(t, 0, me, sibling, src=ins[t])]
            first += [copy(t, 1 + j, me, (*chip, c), src=ins[t]) for j, chip in enumerate(chips)]
            for cp in first:
                cp.start()
            started += first
        for t in range(n):
            for j, chip in enumerate(chips):
                copy(t, 1 + j, (*chip, c), me).wait_recv()
                fwd = copy(t, 4 + j, (*chip, c), sibling)
                fwd.start()
                started.append(fwd)
        for t in range(n):
            copy(t, 0, sibling, me).wait_recv()
            for j, chip in enumerate(chips):
                copy(t, 4 + j, (*chip, 1 - c), me).wait_recv()
        for cp in started:
            cp.wait_send()
        for cp in mine:
            cp.wait()

    any_spec = pl.BlockSpec(memory_space=pl.ANY)
    return pl.pallas_call(
        body, out_shape=[jax.ShapeDtypeStruct((N_DEV,) + s.shape, s.dtype) for s in shards],
        in_specs=[any_spec] * n, out_specs=[any_spec] * n,
        scratch_shapes=[pltpu.SemaphoreType.DMA((7 * n,)), pltpu.SemaphoreType.DMA((7 * n,)),
                        pltpu.SemaphoreType.DMA((n,))],
        name=name)(*shards)


def _tile(total, want):
    step = 128 if total % 128 == 0 else 8
    best = step
    t = step
    while t <= min(total, want):
        if total % t == 0:
            best = t
        t += step
    return best


def _local_step(x, target, mods, lbs, p, wg, shards=None):
    S, D = x.shape
    L = mods.shape[0]
    W = D // 2
    H = W // HEAD
    mesh = shards is not None
    F = shards["w_ffn_out"][0].shape[0] * N_DEV if mesh else wg["w_ffn_out"][0].shape[0]
    tm = _tile(S, 512)
    tm_big = _tile(S, 1024)
    tm_tn = _tile(S, 2048)
    tm_sw = _tile(S, 128)
    tq = _tile(S, 256)
    cg = max(1, min(32, S // CHUNK))
    nb_out = _tile(D, 1024)
    kb_f = _tile(F, 1408)

    def row(a, l):
        return a[l][None, :]

    first_plan = dict(proj=[("w_ffn_in", 0)], hg=[("w_ffn_out", 0), ("w_out", 0)],
                      sb=[("w_in", 1), ("w_out", 1)], ffn=[("w_ffn_in", 1), ("w_ffn_out", 1)])

    def gather_plan(l, call):
        if not mesh:
            return []
        if l == 0:
            plan = first_plan[call]
        else:
            plan = [(dict(proj="w_in", hg="w_out", sb="w_ffn_out", ffn="w_ffn_in")[call], l + 1)]
        return [(k, j) for k, j in plan if j < L]

    def gather_of(plan):
        return (True, [shards[k][j] for k, j in plan]) if plan else None

    def store_gathered(plan, got):
        for (k, j), g in zip(plan, got or []):
            wg[k][j] = g.reshape(-1, D) if k in ("w_out", "w_ffn_out") else g

    def scatter_of(blocks):
        if mesh and blocks is not None:
            return (False, [b.reshape((N_DEV, -1) + b.shape[-1:]) if b.ndim == 2 else b for b in blocks])
        return None

    saved = []
    xcur = x
    for l in range(L):
        mod = mods[l]
        sh1, sc1, g1, sh2, sc2, g2 = [mod[:, i * D:(i + 1) * D] for i in range(N_MOD)]
        h1 = _modnorm_fwd(xcur, row(p["norm1_g"], l), sc1, sh1, tm=tm, name="norm1_fwd")
        plan = gather_plan(l, "proj")
        proj, got = _mm_nn(h1, wg["w_in"][l], tm=tm_big, name="proj_fwd", side=gather_of(plan))
        store_gathered(plan, got)
        lb = lbs[l][None, :]
        plan = gather_plan(l, "hg")
        (o_hg, on_hg, states), got = _hg_fwd(proj, lb, row(p["hg_out_g"], l), n_heads=H, cg=cg, name="hgrn2_fwd",
                                             side=gather_of(plan))
        store_gathered(plan, got)
        qh, kh, vh = _sb_pre(proj, row(p["sb_q_g"], l), row(p["sb_k_g"], l), n_heads=H, col0=4 * H,
                             tm=tm_tn, name="sb_qknorm_fwd")
        plan = gather_plan(l, "sb")
        (o_sb, on_sb), got = _sb_fwd(qh, kh, vh, row(p["sb_out_g"], l), n_heads=H, tq=tq, name="sb_fwd",
                                     side=gather_of(plan))
        store_gathered(plan, got)
        o_cat = jnp.concatenate([on_hg, on_sb], axis=1)
        (x1, mixed, h2), _ = _mm_nn(o_cat, wg["w_out"][l], tm=tm, nb=D, resid=xcur, gate=g1,
                                    norm=(row(p["norm2_g"], l), sc2, sh2), name="out_proj_fwd")
        w_fin = wg["w_ffn_in"][l]
        if w_fin.shape[0] % 2:
            w_fin = jnp.stack([w_fin[0][:, :F], w_fin[0][:, F:]])
        plan = gather_plan(l, "ffn")
        (gate, up, a), got = _ffn_in_fwd(h2, w_fin, tm=tm, nb=F, name="ffn_in_fwd", side=gather_of(plan))
        store_gathered(plan, got)
        (x2, ffn), _ = _mm_nn(a, wg["w_ffn_out"][l], tm=tm, nb=nb_out // 2, resid=x1, gate=g2, name="ffn_out_fwd")
        saved.append(dict(x=xcur, h1=h1, proj=proj, o_hg=o_hg, o_sb=o_sb, states=states, qh=qh, kh=kh, vh=vh,
                          o_cat=o_cat, mixed=mixed, x1=x1, h2=h2, gate=gate, up=up, a=a, ffn=ffn, lb=lb,
                          sc1=sc1, g1=g1, sc2=sc2, g2=g2, w_fin=w_fin))
        xcur = x2

    last = saved[-1]
    dx, dffn, dg2, loss = _loss_bwd(xcur, target, last["ffn"], last["g2"], tm=tm, name="loss_bwd")

    big = {k: [None] * L for k in ("w_in", "w_out", "w_ffn_in", "w_ffn_out")}
    small = {k: [None] * L for k in ("norm1_g", "hg_lb", "hg_out_g", "sb_q_g", "sb_k_g", "sb_out_g", "norm2_g")}
    dmods = [None] * L
    for l in reversed(range(L)):
        sv = saved[l]
        (dgate, dup), _ = _ffn_out_bwd_x(dffn, wg["w_ffn_out"][l], sv["gate"], sv["up"], tm=tm, kb=kb_f,
                                         name="ffn_out_bwd_x")
        g_fout, _ = _mm_tn(sv["a"], dffn, tm=tm_tn, kb=kb_f, nb=nb_out, blocked=False, name="ffn_out_bwd_w")
        dh2, got = _mm_nt(dgate, sv["w_fin"], dy2=dup, tm=tm_big, kb=D, name="ffn_in_bwd_x",
                          side=scatter_of([g_fout]))
        big["w_ffn_out"][l] = g_fout if got is None else got[0]
        g_fin, _ = _mm_tn(sv["h2"], dgate, dy2=dup, tm=tm_tn, kb=_tile(D, 1024), nb=sv["w_fin"].shape[2],
                          blocked=True, name="ffn_in_bwd_w")
        dx1, dmixed, dg1, dsh2, dsc2, dn2 = _modnorm_bwd(
            sv["x1"], dh2, dx, row(p["norm2_g"], l), sv["sc2"], sv["mixed"], sv["g1"], tm=tm_sw * 2,
            name="norm2_bwd")
        small["norm2_g"][l] = dn2
        d_ocat, _ = _mm_nt(dmixed, wg["w_out"][l], tm=tm_big, kb=nb_out, nb=D, name="out_proj_bwd_x")
        g_out, _ = _mm_tn(sv["o_cat"], dmixed, tm=tm_tn, kb=D, nb=nb_out, blocked=False, name="out_proj_bwd_w")
        (dhq, dhf, dhi, dhg, dlb, dhog), got = _hg_bwd(sv["proj"], sv["o_hg"], d_ocat, 0, sv["states"], sv["lb"],
                                                       row(p["hg_out_g"], l), n_heads=H, cg=cg, name="hgrn2_bwd",
                                                       side=scatter_of([g_out]))
        big["w_out"][l] = g_out if got is None else got[0]
        (dqh, dkh, dvh, dsog), got = _sb_bwd(sv["qh"], sv["kh"], sv["vh"], sv["o_sb"], d_ocat, H,
                                             row(p["sb_out_g"], l), n_heads=H, tq=tq, name="sb_bwd",
                                             side=scatter_of([g_fin]))
        big["w_ffn_in"][l] = g_fin if got is None else got[0]
        dsq, dsk, dsv, dqg, dkg = _sb_pre_bwd(sv["proj"], dqh, dkh, dvh, row(p["sb_q_g"], l),
                                              row(p["sb_k_g"], l), n_heads=H, col0=4 * H, tm=tm_tn,
                                              name="sb_qknorm_bwd")
        small["hg_lb"][l] = dlb
        small["hg_out_g"][l] = dhog
        small["sb_out_g"][l] = dsog
        small["sb_q_g"][l] = dqg
        small["sb_k_g"][l] = dkg
        dproj = jnp.concatenate([dhq, dhf, dhi, dhg, dsq, dsk, dsv], axis=1)
        g_in, _ = _mm_tn(sv["h1"], dproj, tm=tm_tn, kb=D, nb=wg["w_in"][l].shape[2], blocked=True,
                         name="proj_bwd_w")
        dh1, got = _mm_nt(dproj, wg["w_in"][l], tm=tm_big, kb=D, name="proj_bwd_x", side=scatter_of([g_in]))
        big["w_in"][l] = g_in if got is None else got[0]
        if l > 0:
            prev = saved[l - 1]
            dx0, dffn_prev, dg2_prev, dsh1, dsc1, dn1 = _modnorm_bwd(
                sv["x"], dh1, dx1, row(p["norm1_g"], l), sv["sc1"], prev["ffn"], prev["g2"], tm=tm_sw * 2,
                name="norm1_bwd")
        else:
            dx0, dsh1, dsc1, dn1 = _modnorm_bwd(sv["x"], dh1, dx1, row(p["norm1_g"], l), sv["sc1"], None, None,
                                                tm=tm_sw * 2, name="norm1_bwd_first")
            dffn_prev, dg2_prev = None, None
        small["norm1_g"][l] = dn1
        dmods[l] = jnp.concatenate([dsh1, dsc1, dg1, dsh2, dsc2, dg2], axis=1)
        dx, dffn, dg2 = dx0, dffn_prev, dg2_prev
    return loss, dx, big, small, dmods


def kernel(x, c, norm1_g, w_in, hg_lb_logits, hg_out_g, sb_q_g, sb_k_g, sb_out_g, w_out, norm2_g, w_ffn_in, w_ffn_out, w_ada, b_ada, loss_target, m_norm1_g, m_w_in, m_hg_lb_logits, m_hg_out_g, m_sb_q_g, m_sb_k_g, m_sb_out_g, m_w_out, m_norm2_g, m_w_ffn_in, m_w_ffn_out, m_w_ada, m_b_ada, v_norm1_g, v_w_in, v_hg_lb_logits, v_hg_out_g, v_sb_q_g, v_sb_k_g, v_sb_out_g, v_w_out, v_norm2_g, v_w_ffn_in, v_w_ffn_out, v_w_ada, v_b_ada):
    L, D = norm1_g.shape
    S = x.shape[1]
    me = 4 * lax.axis_index("x") + 2 * lax.axis_index("y") + lax.axis_index("c")

    c_all = _allgather_small(jnp.broadcast_to(c, (8, D)), name="gather_c").reshape(N_DEV, 8, D)[:, 0, :]
    n_ada = w_ada.shape[2]
    mod_cols, cond = _ada_mod(c_all, w_ada, nb=_tile(n_ada, 512), name="ada_mod")
    mod_all = _allgather_small(mod_cols.reshape(L * N_DEV, n_ada), name="gather_mod")
    mod_all = mod_all.reshape(N_DEV, L, N_DEV, n_ada)
    mod_mine = lax.dynamic_index_in_dim(mod_all, me, axis=2, keepdims=False)
    mods = jnp.transpose(mod_mine, (1, 0, 2)).reshape(L, 1, N_DEV * n_ada) + b_ada[:, None, :]

    lbs = _lb_fwd(hg_lb_logits, name="lower_bounds_fwd")

    shards = dict(w_in=[w_in[l].astype(BF16) for l in range(L)], w_out=[w_out[l].astype(BF16) for l in range(L)],
                  w_ffn_in=[w_ffn_in[l].astype(BF16) for l in range(L)],
                  w_ffn_out=[w_ffn_out[l].astype(BF16) for l in range(L)])
    g_in, = _allgather_hbm([shards["w_in"][0]], name="gather_first_weight")
    wg = dict(w_in=[g_in] + [None] * (L - 1), w_out=[None] * L, w_ffn_in=[None] * L, w_ffn_out=[None] * L)

    p = dict(norm1_g=norm1_g, hg_out_g=hg_out_g, sb_q_g=sb_q_g, sb_k_g=sb_k_g, sb_out_g=sb_out_g,
             norm2_g=norm2_g)
    loss_part, grad_x, recv, small, dmods = _local_step(x.reshape(S, D), loss_target.reshape(S, D), mods, lbs, p,
                                                        wg, shards)

    dmod = jnp.concatenate(dmods, axis=0)
    pieces = [jnp.concatenate(small[k], axis=0) for k in
              ("norm1_g", "hg_lb", "hg_out_g", "sb_q_g", "sb_k_g", "sb_out_g", "norm2_g")] + [dmod]
    flat = jnp.concatenate([a.reshape(-1) for a in pieces] + [loss_part.reshape(-1)])
    n_flat = flat.shape[0]
    rows = -(-n_flat // 1024) * 8
    flat = jnp.pad(flat, (0, rows * 128 - n_flat)).reshape(rows, 128)
    gathered = _allgather_small(flat, name="gather_small_grads").reshape(N_DEV, rows * 128)

    def take(off, shape):
        size = 1
        for s in shape:
            size *= s
        return gathered[:, off:off + size].reshape((N_DEV,) + tuple(shape)), off + size

    off = 0
    parts = {}
    for k, a in zip(("norm1_g", "hg_lb", "hg_out_g", "sb_q_g", "sb_k_g", "sb_out_g", "norm2_g", "dmod"), pieces):
        parts[k], off = take(off, a.shape)
    loss_parts = gathered[:, off:off + 1]
    loss = jnp.sum(loss_parts)

    def pad8(a):
        return jnp.pad(a, ((0, 0), (0, 8 - a.shape[1]), (0, 0)))

    def small_update(w, m, v, gparts):
        Lw = w.shape[0]
        g, d, m2, v2 = _adamw(pad8(w[None])[0], pad8(m[None])[0], pad8(v[None])[0], pad8(gparts),
                              tr=8, name="adamw_small")
        return g[:Lw], d[:Lw], m2[:Lw], v2[:Lw]

    out = {}
    out["norm1_g"] = small_update(norm1_g, m_norm1_g, v_norm1_g, parts["norm1_g"])
    dlogits = _lb_bwd(hg_lb_logits, parts["hg_lb"], name="lower_bounds_bwd")
    out["hg_lb_logits"] = small_update(hg_lb_logits, m_hg_lb_logits, v_hg_lb_logits, dlogits[None])
    out["hg_out_g"] = small_update(hg_out_g, m_hg_out_g, v_hg_out_g, parts["hg_out_g"])
    out["sb_q_g"] = small_update(sb_q_g, m_sb_q_g, v_sb_q_g, parts["sb_q_g"])
    out["sb_k_g"] = small_update(sb_k_g, m_sb_k_g, v_sb_k_g, parts["sb_k_g"])
    out["sb_out_g"] = small_update(sb_out_g, m_sb_out_g, v_sb_out_g, parts["sb_out_g"])
    out["norm2_g"] = small_update(norm2_g, m_norm2_g, v_norm2_g, parts["norm2_g"])
    out["b_ada"] = small_update(b_ada, m_b_ada, v_b_ada, parts["dmod"])

    dmod_all = parts["dmod"].reshape(N_DEV, L, N_DEV, n_ada)
    dmod_mine = lax.dynamic_index_in_dim(dmod_all, me, axis=2, keepdims=False)
    dmod_mine = jnp.pad(jnp.transpose(dmod_mine, (1, 0, 2)), ((0, 0), (0, 128 - N_DEV), (0, 0)))
    cond_t = jnp.pad(jnp.transpose(cond), ((0, 0), (0, 128 - N_DEV)))
    out["w_ada"] = _adamw_ada(w_ada, m_w_ada, v_w_ada, cond_t, dmod_mine, tr=_tile(D, 256), name="adamw_ada")

    def big_update(w, m, v, recv_l, name):
        return _adamw_layers(w, m, v, recv_l, tr=_tile(w.shape[1], 131072 // w.shape[2]), name=name)[0]

    out["w_ffn_in"] = big_update(w_ffn_in, m_w_ffn_in, v_w_ffn_in, recv["w_ffn_in"], "adamw_w_ffn_in")
    out["w_ffn_out"] = big_update(w_ffn_out, m_w_ffn_out, v_w_ffn_out, recv["w_ffn_out"], "adamw_w_ffn_out")
    out["w_out"] = big_update(w_out, m_w_out, v_w_out, recv["w_out"], "adamw_w_out")
    out["w_in"] = big_update(w_in, m_w_in, v_w_in, recv["w_in"], "adamw_w_in")

    order = ("norm1_g", "w_in", "hg_lb_logits", "hg_out_g", "sb_q_g", "sb_k_g", "sb_out_g", "w_out", "norm2_g",
             "w_ffn_in", "w_ffn_out", "w_ada", "b_ada")
    grads = [out[k][0] for k in order]
    deltas = [out[k][1] for k in order]
    new_m = [out[k][2] for k in order]
    new_v = [out[k][3] for k in order]
    return (loss, grad_x.reshape(1, S, D), *grads, *deltas, *new_m, *new_v)
```

```python
import functools

import jax
import jax.numpy as jnp
from jax import lax
from jax.experimental import pallas as pl
from jax.experimental.pallas import tpu as pltpu

F32 = jnp.float32
BF16 = jnp.bfloat16
MESH = pl.DeviceIdType.MESH

N_DEV = 8
HEAD = 128
CHUNK = 64
N_MOD = 6
EPS = 1e-6
TINY = 1e-30
ADAM_LR = 0.001
ADAM_B1 = 0.9
ADAM_B2 = 0.999
ADAM_EPS = 1e-08
ADAM_WD = 0.01
ADAM_STEP = 10
V7X_VMEM_LIMIT = 56 * 1024 * 1024
SKIP_LOG = -104.0


def _params(sem):
    return pltpu.CompilerParams(dimension_semantics=sem, vmem_limit_bytes=V7X_VMEM_LIMIT)


def _bdot(a, b, dims=(((1,), (0,)), ((), ()))):
    return lax.dot_general(a.astype(BF16), b.astype(BF16), dims, preferred_element_type=F32)


_NT = (((1,), (1,)), ((), ()))
_TN = (((0,), (0,)), ((), ()))


def _sigmoid_pair(x):
    e = jnp.exp(-jnp.abs(x))
    r = 1.0 / (1.0 + e)
    er = e * r
    pos = x >= 0
    return jnp.where(pos, r, er), jnp.where(pos, er, r)


def _split_dot(x, u, parts):
    acc = None
    rem = x
    for _ in range(parts):
        p = rem.astype(BF16)
        rem = rem - p.astype(F32)
        t = lax.dot_general(p, u, (((1,), (0,)), ((), ())), preferred_element_type=F32)
        acc = t if acc is None else acc + t
    return acc


def _split_dot_left(u, x, parts):
    acc = None
    rem = x
    for _ in range(parts):
        p = rem.astype(BF16)
        rem = rem - p.astype(F32)
        t = lax.dot_general(u, p, (((1,), (0,)), ((), ())), preferred_element_type=F32)
        acc = t if acc is None else acc + t
    return acc


def _peer(x, y, c, rel):
    return (x ^ ((rel >> 2) & 1), y ^ ((rel >> 1) & 1), c ^ (rel & 1))


def _exchange(gather, srcs, dsts, send_sems, recv_sems, local_sems, phase):
    x, y, c = lax.axis_index("x"), lax.axis_index("y"), lax.axis_index("c")
    me = 4 * x + 2 * y + c
    for t in range(len(srcs)):
        own = srcs[t] if gather else srcs[t].at[me]
        local = pltpu.make_async_copy(own, dsts[t].at[me], local_sems.at[t])
        if phase == "start":
            local.start()
        for rel in range(1, N_DEV):
            px, py, pc = _peer(x, y, c, rel)
            pid = 4 * px + 2 * py + pc
            k = t * (N_DEV - 1) + rel - 1
            if phase == "start":
                pltpu.make_async_remote_copy(
                    src_ref=srcs[t] if gather else srcs[t].at[pid], dst_ref=dsts[t].at[me],
                    send_sem=send_sems.at[k], recv_sem=recv_sems.at[k],
                    device_id=(px, py, pc), device_id_type=MESH).start()
            else:
                cp = pltpu.make_async_remote_copy(
                    src_ref=own, dst_ref=dsts[t].at[pid], send_sem=send_sems.at[k], recv_sem=recv_sems.at[k],
                    device_id=(px, py, pc), device_id_type=MESH)
                cp.wait_recv()
                cp.wait_send()
        if phase == "wait":
            local.wait()


def _gather_two_level(srcs, dsts, send_sems, recv_sems, local_sems, phase):
    x, y, c = lax.axis_index("x"), lax.axis_index("y"), lax.axis_index("c")
    me, sibling = (x, y, c), (x, y, 1 - c)
    chips = [(1 - x, y), (x, 1 - y), (1 - x, 1 - y)]
    per = N_DEV - 1
    for t in range(len(srcs)):
        def slot(px, py, pc, t=t):
            return dsts[t].at[4 * px + 2 * py + pc]

        def copy(k, block, to, src=None, t=t):
            return pltpu.make_async_remote_copy(
                src_ref=slot(*block) if src is None else src, dst_ref=slot(*block),
                send_sem=send_sems.at[t * per + k], recv_sem=recv_sems.at[t * per + k],
                device_id=to, device_id_type=MESH)

        local = pltpu.make_async_copy(srcs[t], slot(*me), local_sems.at[t])
        first = [copy(0, me, sibling, src=srcs[t])]
        first += [copy(1 + j, me, (*chip, c), src=srcs[t]) for j, chip in enumerate(chips)]
        passed = [copy(4 + j, (*chip, c), sibling) for j, chip in enumerate(chips)]
        if phase == "start":
            local.start()
            for cp in first:
                cp.start()
        elif phase == "forward":
            for j, chip in enumerate(chips):
                copy(1 + j, (*chip, c), me).wait_recv()
                passed[j].start()
        else:
            copy(0, sibling, me).wait_recv()
            for j, chip in enumerate(chips):
                copy(4 + j, (*chip, 1 - c), me).wait_recv()
            for cp in first + passed:
                cp.wait_send()
            local.wait()


def _exchange_scratch(n):
    return [pltpu.SemaphoreType.DMA(((N_DEV - 1) * n,)), pltpu.SemaphoreType.DMA(((N_DEV - 1) * n,)),
            pltpu.SemaphoreType.DMA((n,))]


def _pcall(body, *, grid, in_specs, out_specs, out_shape, scratch_shapes=(), semantics, name, args, side=None):
    single = not isinstance(out_shape, (tuple, list))
    if single:
        out_specs, out_shape = [out_specs], [out_shape]
    in_specs, out_specs, out_shape = list(in_specs), list(out_specs), list(out_shape)
    scratch_shapes = list(scratch_shapes)
    n_in, n_out, n_scr = len(in_specs), len(out_specs), len(scratch_shapes)
    if side is None:
        res = pl.pallas_call(body, grid=grid, in_specs=in_specs, out_specs=out_specs, out_shape=out_shape,
                             scratch_shapes=scratch_shapes, compiler_params=_params(semantics), name=name)(*args)
        return (res[0] if single else tuple(res)), None
    gather, srcs = side
    n = len(srcs)

    def full(*refs):
        ins = refs[:n_in]
        s_in = refs[n_in:n_in + n]
        outs = refs[n_in + n:n_in + n + n_out]
        s_out = refs[n_in + n + n_out:n_in + 2 * n + n_out]
        scr = refs[n_in + 2 * n + n_out:n_in + 2 * n + n_out + n_scr]
        send_sems, recv_sems, local_sems = refs[n_in + 2 * n + n_out + n_scr:]
        step = pl.program_id(0)
        steps = grid[0]
        for ax in range(1, len(grid)):
            step = step * grid[ax] + pl.program_id(ax)
            steps *= grid[ax]

        def exchange(phase):
            if gather:
                _gather_two_level(s_in, s_out, send_sems, recv_sems, local_sems, phase)
            elif phase != "forward":
                _exchange(False, s_in, s_out, send_sems, recv_sems, local_sems, phase)

        pl.when(step == 0)(lambda: exchange("start"))
        if gather:
            pl.when(step == (steps * 4) // 5)(lambda: exchange("forward"))
        body(*ins, *outs, *scr)
        pl.when(step == steps - 1)(lambda: exchange("wait"))

    any_spec = pl.BlockSpec(memory_space=pl.ANY)
    s_shapes = [jax.ShapeDtypeStruct(((N_DEV,) + s.shape) if gather else s.shape, s.dtype) for s in srcs]
    res = pl.pallas_call(full, grid=grid, in_specs=in_specs + [any_spec] * n,
                         out_specs=out_specs + [any_spec] * n, out_shape=out_shape + s_shapes,
                         scratch_shapes=scratch_shapes + _exchange_scratch(n),
                         compiler_params=_params(("arbitrary",) * len(grid)), name=name)(*args, *srcs)
    main = res[:n_out]
    return (main[0] if single else tuple(main)), list(res[n_out:])


def _mm_nn(a, b, *, tm, nb=None, out_dtype=F32, resid=None, gate=None, norm=None, name, side=None):
    M, K = a.shape
    if b.ndim == 3:
        NB, _, n = b.shape
        b_spec = pl.BlockSpec((None, K, n), lambda j, i: (j, 0, 0))
    else:
        n = nb
        NB = b.shape[1] // nb
        b_spec = pl.BlockSpec((K, n), lambda j, i: (0, j))
    N = NB * n
    epi = resid is not None
    assert norm is None or (epi and NB == 1)

    def body(*refs):
        if norm is not None:
            a_ref, b_ref, r_ref, g_ref, ng_ref, sc_ref, sh_ref, o_ref, acc_ref, h_ref = refs
        elif epi:
            a_ref, b_ref, r_ref, g_ref, o_ref, acc_ref = refs
        else:
            a_ref, b_ref, o_ref = refs
        acc = jnp.dot(a_ref[...], b_ref[...], preferred_element_type=F32)
        if epi:
            xv = r_ref[...] + g_ref[...] * acc
            o_ref[...] = xv
            acc_ref[...] = acc.astype(BF16)
            if norm is not None:
                rstd = lax.rsqrt(jnp.mean(xv * xv, axis=-1, keepdims=True) + EPS)
                y = (xv * rstd) * ng_ref[...]
                h_ref[...] = (y * (1.0 + sc_ref[...]) + sh_ref[...]).astype(BF16)
        else:
            o_ref[...] = acc.astype(out_dtype)

    in_specs = [pl.BlockSpec((tm, K), lambda j, i: (i, 0)), b_spec]
    args = [a, b]
    o_spec = pl.BlockSpec((tm, n), lambda j, i: (i, j))
    if epi:
        row = pl.BlockSpec((1, n), lambda j, i: (0, j))
        in_specs += [pl.BlockSpec((tm, n), lambda j, i: (i, j)), row]
        args += [resid, gate]
        out_shape = [jax.ShapeDtypeStruct((M, N), F32), jax.ShapeDtypeStruct((M, N), BF16)]
        out_specs = [o_spec, o_spec]
        if norm is not None:
            in_specs += [row, row, row]
            args += list(norm)
            out_shape.append(jax.ShapeDtypeStruct((M, N), BF16))
            out_specs.append(o_spec)
    else:
        out_shape = [jax.ShapeDtypeStruct((M, N), out_dtype)]
        out_specs = [o_spec]
    res, got = _pcall(body, grid=(NB, M // tm), in_specs=in_specs, out_specs=out_specs, out_shape=out_shape,
                      semantics=("parallel", "parallel"), name=name, args=args, side=side)
    return (res[0] if len(res) == 1 else res), got


def _halves(dy, dy2, blk_rows, blk_cols, nblocks, row_of, col_of, last_row=None):
    if dy2 is None:
        return [pl.BlockSpec((blk_rows, blk_cols), lambda *g: (row_of(*g), col_of(*g)))], None
    half = nblocks // 2

    def left(*g):
        r, c = row_of(*g), col_of(*g)
        if last_row is None:
            return (r, jnp.minimum(c, half - 1))
        return (jnp.where(c < half, r, last_row), jnp.minimum(c, half - 1))

    def right(*g):
        r, c = row_of(*g), col_of(*g)
        if last_row is None:
            return (r, jnp.maximum(c - half, 0))
        return (jnp.where(c >= half, r, 0), jnp.maximum(c - half, 0))

    return [pl.BlockSpec((blk_rows, blk_cols), left), pl.BlockSpec((blk_rows, blk_cols), right)], half


def _mm_nt(dy, w, *, tm, kb, nb=None, name, side=None, dy2=None):
    M = dy.shape[0]
    N = dy.shape[1] * (1 if dy2 is None else 2)
    if w.ndim == 3:
        NB, Kt, n = w.shape
        w_spec = pl.BlockSpec((None, kb, n), lambda i, k, j: (j, k, 0))
    else:
        Kt = w.shape[0]
        n = nb
        NB = N // nb
        w_spec = pl.BlockSpec((kb, n), lambda i, k, j: (k, j))
    KB = Kt // kb
    dy_specs, half = _halves(dy, dy2, tm, n, NB, lambda i, k, j: i, lambda i, k, j: j)
    n_op = len(dy_specs)

    def body(*refs):
        dy_refs = refs[:n_op]
        w_ref = refs[n_op]
        o_ref = refs[n_op + 1]
        acc = refs[n_op + 2:]
        j = pl.program_id(2)

        def use(dy_ref):
            part = lax.dot_general(dy_ref[...], w_ref[...], _NT, preferred_element_type=F32)
            if NB == 1:
                o_ref[...] = part
                return
            acc_ref, = acc

            @pl.when(j == 0)
            def _():
                acc_ref[...] = part

            @pl.when(jnp.logical_and(j > 0, j < NB - 1))
            def _():
                acc_ref[...] += part

            @pl.when(j == NB - 1)
            def _():
                o_ref[...] = acc_ref[...] + part

        if half is None:
            use(dy_refs[0])
        else:
            pl.when(j < half)(lambda: use(dy_refs[0]))
            pl.when(j >= half)(lambda: use(dy_refs[1]))

    return _pcall(
        body, grid=(M // tm, KB, NB), in_specs=dy_specs + [w_spec],
        out_specs=pl.BlockSpec((tm, kb), lambda i, k, j: (i, k)),
        out_shape=jax.ShapeDtypeStruct((M, Kt), F32),
        scratch_shapes=[] if NB == 1 else [pltpu.VMEM((tm, kb), F32)],
        semantics=("parallel", "parallel", "arbitrary"), name=name,
        args=(dy, w) if dy2 is None else (dy, dy2, w), side=side)


def _mm_tn(x, dy, *, tm, kb, nb, blocked, name, side=None, dy2=None):
    M, K = x.shape
    N = dy.shape[1] * (1 if dy2 is None else 2)
    KB, NB, MB = K // kb, N // nb, M // tm
    dy_specs, half = _halves(dy, dy2, tm, nb, NB, lambda k, n, m: m, lambda k, n, m: n, last_row=MB - 1)
    n_dy = len(dy_specs)

    def body(*refs):
        x_ref = refs[0]
        dy_refs = refs[1:1 + n_dy]
        o_ref = refs[1 + n_dy]
        acc = refs[2 + n_dy:]
        m = pl.program_id(2)

        def use(dy_ref):
            part = lax.dot_general(x_ref[...], dy_ref[...], _TN, preferred_element_type=F32)
            if MB == 1:
                o_ref[...] = part.astype(BF16)
                return
            acc_ref, = acc

            @pl.when(m == 0)
            def _():
                acc_ref[...] = part

            @pl.when(jnp.logical_and(m > 0, m < MB - 1))
            def _():
                acc_ref[...] += part

            @pl.when(m == MB - 1)
            def _():
                o_ref[...] = (acc_ref[...] + part).astype(BF16)

        if half is None:
            use(dy_refs[0])
        else:
            nblk = pl.program_id(1)
            pl.when(nblk < half)(lambda: use(dy_refs[0]))
            pl.when(nblk >= half)(lambda: use(dy_refs[1]))

    if blocked:
        out_shape = jax.ShapeDtypeStruct((NB, K, nb), BF16)
        o_spec = pl.BlockSpec((None, kb, nb), lambda k, n, m: (n, k, 0))
    else:
        out_shape = jax.ShapeDtypeStruct((K, N), BF16)
        o_spec = pl.BlockSpec((kb, nb), lambda k, n, m: (k, n))
    x_spec = pl.BlockSpec((tm, kb), lambda k, n, m: (m, k))
    return _pcall(
        body, grid=(KB, NB, MB), in_specs=[x_spec] + dy_specs,
        out_specs=o_spec, out_shape=out_shape,
        scratch_shapes=[] if MB == 1 else [pltpu.VMEM((kb, nb), F32)],
        semantics=("parallel", "parallel", "arbitrary"), name=name,
        args=(x, dy) if dy2 is None else (x, dy, dy2), side=side)


def _ffn_in_fwd(h, w, *, tm, nb, name, side=None):
    M, K = h.shape
    if w.ndim == 3:
        J, _, n = w.shape
        half = J // 2
        specs = [pl.BlockSpec((None, K, n), lambda j, i: (j, 0, 0)),
                 pl.BlockSpec((None, K, n), lambda j, i: (j + half, 0, 0))]
    else:
        n = nb
        half = w.shape[1] // (2 * nb)
        specs = [pl.BlockSpec((K, n), lambda j, i: (0, j)), pl.BlockSpec((K, n), lambda j, i: (0, j + half))]
    F = half * n

    def body(h_ref, wg_ref, wu_ref, gate_ref, up_ref, act_ref):
        hv = h_ref[...]
        gate = jnp.dot(hv, wg_ref[...], preferred_element_type=F32)
        up = jnp.dot(hv, wu_ref[...], preferred_element_type=F32)
        s, _ = _sigmoid_pair(gate)
        gate_ref[...] = gate
        up_ref[...] = up
        act_ref[...] = (gate * s * up).astype(BF16)

    o_spec = pl.BlockSpec((tm, n), lambda j, i: (i, j))
    f32 = jax.ShapeDtypeStruct((M, F), F32)
    return _pcall(
        body, grid=(half, M // tm), in_specs=[pl.BlockSpec((tm, K), lambda j, i: (i, 0))] + specs,
        out_specs=(o_spec, o_spec, o_spec), out_shape=(f32, f32, jax.ShapeDtypeStruct((M, F), BF16)),
        semantics=("parallel", "parallel"), name=name, args=(h, w, w), side=side)


def _ffn_out_bwd_x(dy, w, gate, up, *, tm, kb, name, side=None):
    M, D = dy.shape
    F = w.shape[0]

    def body(dy_ref, w_ref, g_ref, u_ref, dg_ref, du_ref):
        da = lax.dot_general(dy_ref[...], w_ref[...], _NT, preferred_element_type=F32)
        gate = g_ref[...]
        s, ns = _sigmoid_pair(gate)
        dg_ref[...] = (da * u_ref[...] * (s * (1.0 + gate * ns))).astype(BF16)
        du_ref[...] = (da * (gate * s)).astype(BF16)

    tile = pl.BlockSpec((tm, kb), lambda i, k: (i, k))
    act = jax.ShapeDtypeStruct((M, F), BF16)
    return _pcall(
        body, grid=(M // tm, F // kb),
        in_specs=[pl.BlockSpec((tm, D), lambda i, k: (i, 0)), pl.BlockSpec((kb, D), lambda i, k: (k, 0)), tile, tile],
        out_specs=(tile, tile), out_shape=(act, act),
        semantics=("parallel", "parallel"), name=name, args=(dy, w, gate, up), side=side)


def _modnorm_fwd(x, gain, sc, sh, *, tm, name):
    S, D = x.shape

    def body(x_ref, g_ref, sc_ref, sh_ref, h_ref):
        xv = x_ref[...]
        rstd = lax.rsqrt(jnp.mean(xv * xv, axis=-1, keepdims=True) + EPS)
        y = (xv * rstd) * g_ref[...]
        h_ref[...] = (y * (1.0 + sc_ref[...]) + sh_ref[...]).astype(BF16)

    row = pl.BlockSpec((1, D), lambda i: (0, 0))
    return pl.pallas_call(
        body, grid=(S // tm,),
        in_specs=[pl.BlockSpec((tm, D), lambda i: (i, 0)), row, row, row],
        out_specs=pl.BlockSpec((tm, D), lambda i: (i, 0)),
        out_shape=jax.ShapeDtypeStruct((S, D), BF16),
        compiler_params=_params(("parallel",)), name=name)(x, gain, sc, sh)


def _modnorm_bwd(x, dh, dres, gain, sc, branch, gate, *, tm, name):
    S, D = x.shape
    has_prev = branch is not None

    def body(*refs):
        if has_prev:
            (x_ref, dh_ref, dr_ref, g_ref, sc_ref, br_ref, gt_ref,
             dx_ref, dbr_ref, dgt_ref, dsh_ref, dsc_ref, dgn_ref) = refs
        else:
            (x_ref, dh_ref, dr_ref, g_ref, sc_ref,
             dx_ref, dsh_ref, dsc_ref, dgn_ref) = refs
        i = pl.program_id(0)
        xv = x_ref[...]
        dh_v = dh_ref[...]
        gv = g_ref[...]
        scale1 = 1.0 + sc_ref[...]
        rstd = lax.rsqrt(jnp.mean(xv * xv, axis=-1, keepdims=True) + EPS)
        n = xv * rstd
        dn = dh_v * (gv * scale1)
        dx = rstd * (dn - n * jnp.mean(dn * n, axis=-1, keepdims=True)) + dr_ref[...]
        dx_ref[...] = dx
        dhn = dh_v * n
        p_sh = jnp.sum(dh_v, axis=0, keepdims=True)
        p_sc = jnp.sum(dhn, axis=0, keepdims=True) * gv
        p_gn = jnp.sum(dhn, axis=0, keepdims=True) * scale1
        if has_prev:
            dbr_ref[...] = (gt_ref[...] * dx).astype(BF16)
            p_gt = jnp.sum(dx * br_ref[...].astype(F32), axis=0, keepdims=True)

        @pl.when(i == 0)
        def _():
            dsh_ref[...] = p_sh
            dsc_ref[...] = p_sc
            dgn_ref[...] = p_gn
            if has_prev:
                dgt_ref[...] = p_gt

        @pl.when(i > 0)
        def _():
            dsh_ref[...] += p_sh
            dsc_ref[...] += p_sc
            dgn_ref[...] += p_gn
            if has_prev:
                dgt_ref[...] += p_gt

    tile = pl.BlockSpec((tm, D), lambda i: (i, 0))
    row = pl.BlockSpec((1, D), lambda i: (0, 0))
    row_shape = jax.ShapeDtypeStruct((1, D), F32)
    if has_prev:
        in_specs = [tile, tile, tile, row, row, tile, row]
        args = (x, dh, dres, gain, sc, branch, gate)
        out_specs = (tile, tile, row, row, row, row)
        out_shape = (jax.ShapeDtypeStruct((S, D), F32), jax.ShapeDtypeStruct((S, D), BF16),
                     row_shape, row_shape, row_shape, row_shape)
    else:
        in_specs = [tile, tile, tile, row, row]
        args = (x, dh, dres, gain, sc)
        out_specs = (tile, row, row, row)
        out_shape = (jax.ShapeDtypeStruct((S, D), F32), row_shape, row_shape, row_shape)
    return pl.pallas_call(body, grid=(S // tm,), in_specs=in_specs, out_specs=out_specs,
                          out_shape=out_shape, compiler_params=_params(("arbitrary",)),
                          name=name)(*args)


def _loss_bwd(y, target, branch, gate, *, tm, name):
    S, D = y.shape
    nsteps = S // tm

    def body(y_ref, t_ref, br_ref, gt_ref, dy_ref, dbr_ref, dgt_ref, loss_ref, col_ref):
        i = pl.program_id(0)
        diff = y_ref[...] - t_ref[...]
        dy = diff * (1.0 / D)
        dy_ref[...] = dy
        dbr_ref[...] = (gt_ref[...] * dy).astype(BF16)
        p_gt = jnp.sum(dy * br_ref[...].astype(F32), axis=0, keepdims=True)
        p_col = jnp.sum(diff * diff, axis=0, keepdims=True)

        @pl.when(i == 0)
        def _():
            dgt_ref[...] = p_gt
            col_ref[...] = p_col

        @pl.when(i > 0)
        def _():
            dgt_ref[...] += p_gt
            col_ref[...] += p_col

        @pl.when(i == nsteps - 1)
        def _():
            tot = jnp.sum(col_ref[...], axis=-1, keepdims=True) * (0.5 / D)
            loss_ref[...] = jnp.broadcast_to(tot, (1, 128))

    tile = pl.BlockSpec((tm, D), lambda i: (i, 0))
    row = pl.BlockSpec((1, D), lambda i: (0, 0))
    return pl.pallas_call(
        body, grid=(nsteps,), in_specs=[tile, tile, tile, row],
        out_specs=(tile, tile, row, pl.BlockSpec((1, 128), lambda i: (0, 0))),
        out_shape=(jax.ShapeDtypeStruct((S, D), F32), jax.ShapeDtypeStruct((S, D), BF16),
                   jax.ShapeDtypeStruct((1, D), F32), jax.ShapeDtypeStruct((1, 128), F32)),
        scratch_shapes=[pltpu.VMEM((1, D), F32)],
        compiler_params=_params(("arbitrary",)), name=name)(y, target, branch, gate)


def _hg_chunk(q, fl, lbv, tri):
    C = q.shape[0]
    sq, nsq = _sigmoid_pair(q)
    qa = q * sq
    sig, nsig = _sigmoid_pair(fl)
    one_lb = 1.0 - lbv
    f = lbv + one_lb * sig
    fc = jnp.maximum(f, TINY)
    lf = jnp.log(fc)
    k = one_lb * nsig
    b = _split_dot_left(tri, lf, 3)
    row = lax.broadcasted_iota(jnp.int32, b.shape, 0)
    bm = jnp.sum(jnp.where(row == C // 2 - 1, b, 0.0), axis=0, keepdims=True)
    bl = jnp.sum(jnp.where(row == C - 1, b, 0.0), axis=0, keepdims=True)
    eb = jnp.exp(b)
    ebm = jnp.exp(b - bm)
    enbm = jnp.exp(bm - b)
    ebl = jnp.exp(bl - b)
    ebL = jnp.exp(bl)

    def operand(t):
        return t.astype(BF16).astype(F32)

    return dict(sq=sq, nsq=nsq, qa=qa, sig=sig, nsig=nsig, one_lb=one_lb, f=f, fc=fc, k=k,
                eb=eb, ebm=ebm, enbm=enbm, ebl=ebl, ebL=ebL,
                Qm=operand(qa * ebm), Km=operand(k * enbm), Qb=operand(qa * eb), Kh=operand(k * ebl), row=row)


def _causal_incl(C):
    r = lax.broadcasted_iota(jnp.int32, (C, C), 0)
    c = lax.broadcasted_iota(jnp.int32, (C, C), 1)
    return r >= c


def _hg_fwd(proj, lb, out_g, *, n_heads, cg, name, side=None):
    S = proj.shape[0]
    H = n_heads
    W = H * HEAD
    T = cg * CHUNK
    NG = S // T
    tri = jnp.tril(jnp.ones((CHUNK, CHUNK), F32)).astype(BF16)

    def body(q_ref, f_ref, v_ref, g_ref, lb_ref, og_ref, tri_ref, o_ref, on_ref, st_ref, s_scr):
        @pl.when(pl.program_id(1) == 0)
        def _():
            s_scr[...] = jnp.zeros_like(s_scr)

        lbv = lb_ref[...]
        ogv = og_ref[...]
        triv = tri_ref[...]
        mask = _causal_incl(CHUNK)
        for c in range(cg):
            rows = pl.ds(c * CHUNK, CHUNK)
            v = v_ref[rows, :]
            gg = g_ref[rows, :]
            cm = _hg_chunk(q_ref[rows, :], f_ref[rows, :], lbv, triv)
            s0 = s_scr[...]
            st_ref[c] = s0
            A = jnp.where(mask, _bdot(cm["Qm"], cm["Km"], _NT), 0.0)
            o = _bdot(A, v) + _bdot(cm["Qb"], s0, _NT)
            s_scr[...] = s0 * cm["ebL"] + _bdot(v, cm["Kh"], _TN)
            o_ref[rows, :] = o
            rstd = lax.rsqrt(jnp.mean(o * o, axis=-1, keepdims=True) + EPS)
            sg, _ = _sigmoid_pair(gg)
            on_ref[rows, :] = (((o * rstd) * ogv) * (gg * sg)).astype(BF16)

    def col(group):
        return pl.BlockSpec((T, HEAD), lambda h, g: (g, group * H + h))

    vec = pl.BlockSpec((1, HEAD), lambda h, g: (0, h))
    return _pcall(
        body, grid=(H, NG),
        in_specs=[col(0), col(1), col(2), col(3), vec, vec,
                  pl.BlockSpec((CHUNK, CHUNK), lambda h, g: (0, 0))],
        out_specs=(pl.BlockSpec((T, HEAD), lambda h, g: (g, h)),
                   pl.BlockSpec((T, HEAD), lambda h, g: (g, h)),
                   pl.BlockSpec((cg, None, HEAD, HEAD), lambda h, g: (g, h, 0, 0))),
        out_shape=(jax.ShapeDtypeStruct((S, W), F32), jax.ShapeDtypeStruct((S, W), BF16),
                   jax.ShapeDtypeStruct((S // CHUNK, H, HEAD, HEAD), F32)),
        scratch_shapes=[pltpu.VMEM((HEAD, HEAD), F32)],
        semantics=("parallel", "arbitrary"), name=name,
        args=(proj, proj, proj, proj, lb, out_g, tri), side=side)


def _hg_bwd(proj, o_pre, d_on, d_on_col0, states, lb, out_g, *, n_heads, cg, name, side=None):
    S = proj.shape[0]
    H = n_heads
    W = H * HEAD
    T = cg * CHUNK
    NG = S // T
    tri = jnp.tril(jnp.ones((CHUNK, CHUNK), F32)).astype(BF16)
    triu = jnp.triu(jnp.ones((CHUNK, CHUNK), F32)).astype(BF16)

    def body(q_ref, f_ref, v_ref, g_ref, o_ref, dy_ref, st_ref, lb_ref, og_ref, tri_ref, triu_ref,
             dq_ref, df_ref, di_ref, dg_ref, dlb_ref, dog_ref, ds_scr):
        gstep = pl.program_id(1)

        @pl.when(gstep == 0)
        def _():
            ds_scr[...] = jnp.zeros_like(ds_scr)
            dlb_ref[...] = jnp.zeros_like(dlb_ref)
            dog_ref[...] = jnp.zeros_like(dog_ref)

        lbv = lb_ref[...]
        ogv = og_ref[...]
        triv = tri_ref[...]
        triuv = triu_ref[...]
        mask = _causal_incl(CHUNK)
        dlb_acc = jnp.zeros((1, HEAD), F32)
        dog_acc = jnp.zeros((1, HEAD), F32)
        for c in reversed(range(cg)):
            rows = pl.ds(c * CHUNK, CHUNK)
            q = q_ref[rows, :]
            v = v_ref[rows, :]
            gg = g_ref[rows, :]
            o = o_ref[rows, :]
            dy = dy_ref[rows, :]
            cm = _hg_chunk(q, f_ref[rows, :], lbv, triv)
            s0 = st_ref[c]
            ds1 = ds_scr[...]
            rstd = lax.rsqrt(jnp.mean(o * o, axis=-1, keepdims=True) + EPS)
            n = o * rstd
            sg, nsg = _sigmoid_pair(gg)
            silu_g = gg * sg
            dyn = dy * n
            dog_acc = dog_acc + jnp.sum(dyn * silu_g, axis=0, keepdims=True)
            dg_ref[rows, :] = (dyn * ogv * (sg * (1.0 + gg * nsg))).astype(BF16)
            dn = dy * (ogv * silu_g)
            d_o = rstd * (dn - n * jnp.mean(dn * n, axis=-1, keepdims=True))
            A = jnp.where(mask, _bdot(cm["Qm"], cm["Km"], _NT), 0.0)
            dA = jnp.where(mask, _bdot(d_o, v, _NT), 0.0)
            dV = _bdot(A, d_o, _TN) + _bdot(cm["Kh"], ds1, _NT)
            dQm = _bdot(dA, cm["Km"])
            dKm = _bdot(dA, cm["Qm"], _TN)
            dQb = _bdot(d_o, s0)
            dKh = _bdot(v, ds1)
            ds_scr[...] = ds1 * cm["ebL"] + _bdot(d_o, cm["Qb"], _TN)
            kh_term = dKh * cm["Kh"]
            db = dQm * cm["Qm"] - dKm * cm["Km"] + dQb * cm["Qb"] - kh_term
            dbl = (jnp.sum(kh_term, axis=0, keepdims=True)
                   + cm["ebL"] * jnp.sum(ds1 * s0, axis=0, keepdims=True))
            db = db + jnp.where(cm["row"] == CHUNK - 1, dbl, 0.0)
            dlf = _split_dot_left(triuv, db, 3)
            dqa = dQm * cm["ebm"] + dQb * cm["eb"]
            dq_ref[rows, :] = (dqa * (cm["sq"] * (1.0 + q * cm["nsq"]))).astype(BF16)
            dk = dKm * cm["enbm"] + dKh * cm["ebl"]
            dfc = jnp.where(cm["f"] > TINY, dlf / cm["fc"], 0.0)
            t = dfc - dk
            df_ref[rows, :] = (t * (cm["one_lb"] * cm["sig"] * cm["nsig"])).astype(BF16)
            dlb_acc = dlb_acc + jnp.sum(t * cm["nsig"], axis=0, keepdims=True)
            di_ref[rows, :] = dV.astype(BF16)
        dlb_ref[...] += dlb_acc
        dog_ref[...] += dog_acc

    def col(group):
        return pl.BlockSpec((T, HEAD), lambda h, g: (NG - 1 - g, group * H + h))

    own = pl.BlockSpec((T, HEAD), lambda h, g: (NG - 1 - g, h))
    vec = pl.BlockSpec((1, HEAD), lambda h, g: (0, h))
    cst = pl.BlockSpec((CHUNK, CHUNK), lambda h, g: (0, 0))
    act = jax.ShapeDtypeStruct((S, W), BF16)
    vec_shape = jax.ShapeDtypeStruct((1, W), F32)
    return _pcall(
        body, grid=(H, NG),
        in_specs=[col(0), col(1), col(2), col(3), own,
                  pl.BlockSpec((T, HEAD), lambda h, g: (NG - 1 - g, d_on_col0 + h)),
                  pl.BlockSpec((cg, None, HEAD, HEAD), lambda h, g: (NG - 1 - g, h, 0, 0)),
                  vec, vec, cst, cst],
        out_specs=(own, own, own, own, vec, vec),
        out_shape=(act, act, act, act, vec_shape, vec_shape),
        scratch_shapes=[pltpu.VMEM((HEAD, HEAD), F32)],
        semantics=("parallel", "arbitrary"), name=name,
        args=(proj, proj, proj, proj, o_pre, d_on, states, lb, out_g, tri, triu), side=side)


def _sb_pre(proj, q_g, k_g, *, n_heads, col0, tm, name):
    S = proj.shape[0]
    H = n_heads
    W = H * HEAD

    def body(q_ref, k_ref, v_ref, qg_ref, kg_ref, qh_ref, kh_ref, vh_ref):
        for src, g_ref, dst in ((q_ref, qg_ref, qh_ref), (k_ref, kg_ref, kh_ref)):
            xv = src[...]
            rstd = lax.rsqrt(jnp.mean(xv * xv, axis=-1, keepdims=True) + EPS)
            dst[...] = ((xv * rstd) * g_ref[...]).astype(BF16)
        vh_ref[...] = v_ref[...].astype(BF16)

    def col(group):
        return pl.BlockSpec((tm, HEAD), lambda i, h: (i, col0 + group * H + h))

    vec = pl.BlockSpec((1, HEAD), lambda i, h: (0, 0))
    own = pl.BlockSpec((tm, HEAD), lambda i, h: (i, h))
    act = jax.ShapeDtypeStruct((S, W), BF16)
    return pl.pallas_call(
        body, grid=(S // tm, H), in_specs=[col(0), col(1), col(2), vec, vec],
        out_specs=(own, own, own), out_shape=(act, act, act),
        compiler_params=_params(("parallel", "parallel")), name=name)(proj, proj, proj, q_g, k_g)


def _sb_scores(q, k_blk, scale):
    z = lax.dot_general(q, k_blk, _NT, preferred_element_type=F32) * scale
    e = jnp.exp(-jnp.abs(z))
    sp = jnp.maximum(z, 0.0) + jnp.log(1.0 + e)
    return z, e, sp


def _heads_per_step(n_heads):
    return 2 if n_heads % 2 == 0 else 1


def _max_all(values):
    m = jnp.max(values[0])
    for v in values[1:]:
        m = jnp.maximum(m, jnp.max(v))
    return m


def _strict_lower_mask(t):
    r = lax.broadcasted_iota(jnp.int32, (t, t), 0)
    c = lax.broadcasted_iota(jnp.int32, (t, t), 1)
    return c < r


def _sb_fwd(qh, kh, vh, out_g, *, n_heads, tq, name, side=None):
    S, W = qh.shape
    HP = _heads_per_step(n_heads)
    WP = HP * HEAD
    NQ = S // tq
    scale = HEAD ** -0.5
    u_strict = jnp.tril(jnp.ones((tq, tq), F32), -1).astype(BF16)

    def body(q_ref, k_ref, v_ref, og_ref, u_ref, o_ref, on_ref):
        qi = pl.program_id(1)
        u = u_ref[...]
        heads = [slice(hh * HEAD, (hh + 1) * HEAD) for hh in range(HP)]
        qs = [q_ref[:, cols] for cols in heads]

        def block(kb, r_carry, diag, valid=None):
            rows = pl.ds(pl.multiple_of(kb * tq, tq), tq)
            pvs, rs = [], []
            for hh, cols in enumerate(heads):
                k_blk = k_ref[rows, cols]
                v_blk = v_ref[rows, cols]
                z, _, sp = _sb_scores(qs[hh], k_blk, scale)
                if diag:
                    m = _strict_lower_mask(tq)
                    L = jnp.where(m, -sp, 0.0)
                else:
                    L = -sp
                C = _split_dot(L, u, 2)
                a = jnp.exp(z - sp + C + r_carry[hh])
                if diag:
                    a = jnp.where(m, a, 0.0)
                if valid is not None:
                    a = jnp.where(valid, a, 0.0)
                pvs.append(lax.dot_general(a.astype(BF16), v_blk, (((1,), (0,)), ((), ())),
                                           preferred_element_type=F32))
                rs.append(r_carry[hh] + (C[:, 0:1] + L[:, 0:1]))
            return tuple(pvs), tuple(rs)

        acc_d, r_d = block(qi, (jnp.zeros((tq, 1), F32),) * HP, True)
        acc_p, r0 = block(jnp.maximum(qi - 1, 0), r_d, False, valid=qi > 0)
        acc0 = tuple(a + b for a, b in zip(acc_d, acc_p))

        def cond(st):
            kb, _, _, rmax = st
            return jnp.logical_and(kb >= 0, rmax > SKIP_LOG)

        def step(st):
            kb, acc, r, _ = st
            pv, r2 = block(kb, r, False)
            return kb - 1, tuple(a + b for a, b in zip(acc, pv)), r2, _max_all(r2)

        _, accs, _, _ = lax.while_loop(cond, step, (qi - 2, acc0, r0, _max_all(r0)))
        for hh, cols in enumerate(heads):
            acc = accs[hh]
            o_ref[:, cols] = acc
            rstd = lax.rsqrt(jnp.mean(acc * acc, axis=-1, keepdims=True) + EPS)
            on_ref[:, cols] = ((acc * rstd) * og_ref[:, cols]).astype(BF16)

    blk = pl.BlockSpec((tq, WP), lambda h, i: (i, h))
    full = pl.BlockSpec((S, WP), lambda h, i: (0, h))
    return _pcall(
        body, grid=(n_heads // HP, NQ),
        in_specs=[blk, full, full, pl.BlockSpec((1, WP), lambda h, i: (0, h)),
                  pl.BlockSpec((tq, tq), lambda h, i: (0, 0))],
        out_specs=(blk, blk),
        out_shape=(jax.ShapeDtypeStruct((S, W), F32), jax.ShapeDtypeStruct((S, W), BF16)),
        semantics=("parallel", "arbitrary"), name=name, args=(qh, kh, vh, out_g, u_strict), side=side)


def _sb_bwd(qh, kh, vh, o_pre, d_on, d_on_col0, out_g, *, n_heads, tq, name, side=None):
    S, W = qh.shape
    HP = _heads_per_step(n_heads)
    WP = HP * HEAD
    assert d_on_col0 % HP == 0
    NQ = S // tq
    scale = HEAD ** -0.5
    u_strict = jnp.tril(jnp.ones((tq, tq), F32), -1).astype(BF16)
    u_incl = jnp.tril(jnp.ones((tq, tq), F32)).astype(BF16)

    def body(q_ref, k_ref, v_ref, o_ref, dy_ref, og_ref, us_ref, ui_ref,
             dq_ref, dk_ref, dv_ref, dog_ref):
        qi = pl.program_id(1)

        @pl.when(qi == 0)
        def _():
            dk_ref[...] = jnp.zeros_like(dk_ref)
            dv_ref[...] = jnp.zeros_like(dv_ref)
            dog_ref[...] = jnp.zeros_like(dog_ref)

        us = us_ref[...]
        ui = ui_ref[...]
        heads = [slice(hh * HEAD, (hh + 1) * HEAD) for hh in range(HP)]
        qs, d_obs, deltas = [], [], []
        for cols in heads:
            qs.append(q_ref[:, cols])
            o = o_ref[:, cols]
            dy = dy_ref[:, cols]
            rstd = lax.rsqrt(jnp.mean(o * o, axis=-1, keepdims=True) + EPS)
            n = o * rstd
            dog_ref[:, cols] += jnp.sum(dy * n, axis=0, keepdims=True)
            dn = dy * og_ref[:, cols]
            d_o = rstd * (dn - n * jnp.mean(dn * n, axis=-1, keepdims=True))
            d_ob = d_o.astype(BF16)
            d_obs.append(d_ob)
            deltas.append(jnp.sum(d_ob.astype(F32) * o, axis=-1, keepdims=True))
        q_ts = [q.T for q in qs]
        d_ob_ts = [d.T for d in d_obs]

        def block(kb, r_carry, g_carry, diag, valid=None):
            rows = pl.ds(pl.multiple_of(kb * tq, tq), tq)
            dqs, rs, gs = [], [], []
            for hh, cols in enumerate(heads):
                k_blk = k_ref[rows, cols]
                v_blk = v_ref[rows, cols]
                z, e, sp = _sb_scores(qs[hh], k_blk, scale)
                if diag:
                    m = _strict_lower_mask(tq)
                    L = jnp.where(m, -sp, 0.0)
                else:
                    L = -sp
                C = _split_dot(L, us, 2)
                a = jnp.exp(z - sp + C + r_carry[hh])
                if diag:
                    a = jnp.where(m, a, 0.0)
                if valid is not None:
                    a = jnp.where(valid, a, 0.0)
                ab = a.astype(BF16)
                dA = lax.dot_general(d_obs[hh], v_blk, _NT, preferred_element_type=F32)
                G = ab.astype(F32) * dA
                SI = _split_dot(G, ui, 2)
                P = deltas[hh] - (g_carry[hh] + SI)
                r = 1.0 / (1.0 + e)
                sig = jnp.where(z >= 0, r, e * r)
                dz = G - (G + P) * sig
                if diag:
                    dz = jnp.where(m, dz, 0.0)
                if valid is not None:
                    dz = jnp.where(valid, dz, 0.0)
                dzb = (dz * scale).astype(BF16)
                dqs.append(lax.dot_general(dzb, k_blk, (((1,), (0,)), ((), ())), preferred_element_type=F32))
                dk_ref[cols, rows] += jnp.dot(q_ts[hh], dzb, preferred_element_type=F32)
                dv_ref[cols, rows] += jnp.dot(d_ob_ts[hh], ab, preferred_element_type=F32)
                rs.append(r_carry[hh] + (C[:, 0:1] + L[:, 0:1]))
                gs.append(g_carry[hh] + SI[:, 0:1])
            return tuple(dqs), tuple(rs), tuple(gs)

        zero = (jnp.zeros((tq, 1), F32),) * HP
        dq_d, r_d, g_d = block(qi, zero, zero, True)
        dq_p, r0, g0 = block(jnp.maximum(qi - 1, 0), r_d, g_d, False, valid=qi > 0)
        dq0 = tuple(a + b for a, b in zip(dq_d, dq_p))

        def cond(st):
            kb, _, _, _, rmax = st
            return jnp.logical_and(kb >= 0, rmax > SKIP_LOG)

        def step(st):
            kb, dq, r, g, _ = st
            dq_part, r2, g2 = block(kb, r, g, False)
            return kb - 1, tuple(a + b for a, b in zip(dq, dq_part)), r2, g2, _max_all(r2)

        _, dqs, _, _, _ = lax.while_loop(cond, step, (qi - 2, dq0, r0, g0, _max_all(r0)))
        for hh, cols in enumerate(heads):
            dq_ref[:, cols] = dqs[hh]

    blk = pl.BlockSpec((tq, WP), lambda h, i: (i, h))
    full = pl.BlockSpec((S, WP), lambda h, i: (0, h))
    vec = pl.BlockSpec((1, WP), lambda h, i: (0, h))
    cst = pl.BlockSpec((tq, tq), lambda h, i: (0, 0))
    act = jax.ShapeDtypeStruct((S, W), F32)
    act_t = jax.ShapeDtypeStruct((W, S), F32)
    full_t = pl.BlockSpec((WP, S), lambda h, i: (h, 0))
    return _pcall(
        body, grid=(n_heads // HP, NQ),
        in_specs=[blk, full, full, blk,
                  pl.BlockSpec((tq, WP), lambda h, i: (i, d_on_col0 // HP + h)), vec, cst, cst],
        out_specs=(blk, full_t, full_t, vec),
        out_shape=(act, act_t, act_t, jax.ShapeDtypeStruct((1, W), F32)),
        semantics=("parallel", "arbitrary"), name=name,
        args=(qh, kh, vh, o_pre, d_on, out_g, u_strict, u_incl), side=side)


def _sb_pre_bwd(proj, dqh, dkh, dvh, q_g, k_g, *, n_heads, col0, tm, name):
    S = proj.shape[0]
    H = n_heads
    W = H * HEAD

    def body(q_ref, k_ref, dqh_ref, dkh_ref, dvh_ref, qg_ref, kg_ref,
             dq_ref, dk_ref, dv_ref, dqg_ref, dkg_ref):
        first = jnp.logical_and(pl.program_id(0) == 0, pl.program_id(1) == 0)

        @pl.when(first)
        def _():
            dqg_ref[...] = jnp.zeros_like(dqg_ref)
            dkg_ref[...] = jnp.zeros_like(dkg_ref)

        for src, dh, g_ref, dst, dg_ref in ((q_ref, dqh_ref[...], qg_ref, dq_ref, dqg_ref),
                                            (k_ref, dkh_ref[...].T, kg_ref, dk_ref, dkg_ref)):
            xv = src[...]
            rstd = lax.rsqrt(jnp.mean(xv * xv, axis=-1, keepdims=True) + EPS)
            n = xv * rstd
            dg_ref[...] += jnp.sum(dh * n, axis=0, keepdims=True)
            dn = dh * g_ref[...]
            dst[...] = (rstd * (dn - n * jnp.mean(dn * n, axis=-1, keepdims=True))).astype(BF16)
        dv_ref[...] = dvh_ref[...].T.astype(BF16)

    def col(group):
        return pl.BlockSpec((tm, HEAD), lambda i, h: (i, col0 + group * H + h))

    vec = pl.BlockSpec((1, HEAD), lambda i, h: (0, 0))
    own = pl.BlockSpec((tm, HEAD), lambda i, h: (i, h))
    own_t = pl.BlockSpec((HEAD, tm), lambda i, h: (h, i))
    act = jax.ShapeDtypeStruct((S, W), BF16)
    vec_shape = jax.ShapeDtypeStruct((1, HEAD), F32)
    return pl.pallas_call(
        body, grid=(S // tm, H), in_specs=[col(0), col(1), own, own_t, own_t, vec, vec],
        out_specs=(own, own, own, vec, vec), out_shape=(act, act, act, vec_shape, vec_shape),
        compiler_params=_params(("arbitrary", "arbitrary")), name=name,
    )(proj, proj, dqh, dkh, dvh, q_g, k_g)


def _softmax_rows(x_ref, L):
    rows = [x_ref[l:l + 1, :] for l in range(L)]
    mx = rows[0]
    for r in rows[1:]:
        mx = jnp.maximum(mx, r)
    ex = [jnp.exp(r - mx) for r in rows]
    tot = ex[0]
    for e in ex[1:]:
        tot = tot + e
    return [e / tot for e in ex]


def _lb_fwd(logits, *, name):
    L, W = logits.shape

    def body(x_ref, o_ref):
        s = _softmax_rows(x_ref, L)
        run = jnp.zeros((1, W), F32)
        for l in range(L):
            run = run + s[l]
            o_ref[l:l + 1, :] = run - s[0]

    return pl.pallas_call(body, out_shape=jax.ShapeDtypeStruct((L, W), F32), name=name)(logits)


def _lb_bwd(logits, dlb_parts, *, name):
    L, W = logits.shape
    P = dlb_parts.shape[0]

    def body(x_ref, d_ref, o_ref):
        s = _softmax_rows(x_ref, L)
        dlb = []
        for l in range(L):
            t = d_ref[0, l:l + 1, :]
            for q in range(1, P):
                t = t + d_ref[q, l:l + 1, :]
            dlb.append(t)
        ds = [None] * L
        run = jnp.zeros((1, W), F32)
        for j in reversed(range(L)):
            run = run + dlb[j]
            ds[j] = run
        ds[0] = jnp.zeros((1, W), F32)
        inner = jnp.zeros((1, W), F32)
        for j in range(L):
            inner = inner + s[j] * ds[j]
        for j in range(L):
            o_ref[j:j + 1, :] = s[j] * (ds[j] - inner)

    return pl.pallas_call(body, out_shape=jax.ShapeDtypeStruct((L, W), F32), name=name)(logits, dlb_parts)


def _ada_mod(c_all, w_ada, *, nb, name):
    L, D, n = w_ada.shape
    B = c_all.shape[0]

    def body(c_ref, w_ref, o_ref, cond_ref):
        cv = c_ref[...]
        s, _ = _sigmoid_pair(cv)
        cond = cv * s
        cond_ref[...] = cond
        o_ref[...] = _bdot(cond, w_ref[...])

    return pl.pallas_call(
        body, grid=(L, n // nb),
        in_specs=[pl.BlockSpec((B, D), lambda l, j: (0, 0)),
                  pl.BlockSpec((None, D, nb), lambda l, j: (l, 0, j))],
        out_specs=(pl.BlockSpec((None, B, nb), lambda l, j: (l, 0, j)),
                   pl.BlockSpec((B, D), lambda l, j: (0, 0))),
        out_shape=(jax.ShapeDtypeStruct((L, B, n), F32), jax.ShapeDtypeStruct((B, D), F32)),
        compiler_params=_params(("arbitrary", "arbitrary")), name=name)(c_all, w_ada)


def _adam_math(w, g, m, v):
    m2 = ADAM_B1 * m + (1.0 - ADAM_B1) * g
    v2 = ADAM_B2 * v + (1.0 - ADAM_B2) * (g * g)
    m_hat = m2 / (1.0 - ADAM_B1 ** ADAM_STEP)
    v_hat = v2 / (1.0 - ADAM_B2 ** ADAM_STEP)
    delta = -ADAM_LR * (m_hat / (jnp.sqrt(v_hat) + ADAM_EPS) + ADAM_WD * w)
    return delta, m2, v2


def _adamw(w, m, v, gparts, *, tr, name):
    R, C = w.shape
    P = gparts.shape[0]

    def body(w_ref, m_ref, v_ref, gp_ref, g_ref, d_ref, m2_ref, v2_ref):
        g = gp_ref[0].astype(F32)
        for p in range(1, P):
            g = g + gp_ref[p].astype(F32)
        delta, m2, v2 = _adam_math(w_ref[...], g, m_ref[...], v_ref[...])
        g_ref[...] = g
        d_ref[...] = delta
        m2_ref[...] = m2
        v2_ref[...] = v2

    tile = pl.BlockSpec((tr, C), lambda i: (i, 0))
    shp = jax.ShapeDtypeStruct((R, C), F32)
    return pl.pallas_call(
        body, grid=(R // tr,),
        in_specs=[tile, tile, tile, pl.BlockSpec((P, tr, C), lambda i: (0, i, 0))],
        out_specs=(tile, tile, tile, tile), out_shape=(shp, shp, shp, shp),
        compiler_params=_params(("parallel",)), name=name)(w, m, v, gparts)


def _adamw_layers(w, m, v, gparts, *, tr, name, side=None):
    L, R, C = w.shape
    P = gparts[0].shape[0]
    nblk = R // tr

    def body(*refs):
        w_ref, m_ref, v_ref = refs[:3]
        gp_refs = refs[3:3 + L]
        g_ref, d_ref, m2_ref, v2_ref = refs[3 + L:]
        layer = pl.program_id(0)
        for t in range(L):
            @pl.when(layer == t)
            def _(t=t):
                g = gp_refs[t][0].astype(F32)
                for q in range(1, P):
                    g = g + gp_refs[t][q].astype(F32)
                delta, m2, v2 = _adam_math(w_ref[...], g, m_ref[...], v_ref[...])
                g_ref[...] = g
                d_ref[...] = delta
                m2_ref[...] = m2
                v2_ref[...] = v2

    def gp_spec(t):
        def index(l, i):
            return (0, jnp.where(l == t, i, jnp.where(l < t, 0, nblk - 1)), 0)
        return pl.BlockSpec((P, tr, C), index)

    tile = pl.BlockSpec((None, tr, C), lambda l, i: (l, i, 0))
    shp = jax.ShapeDtypeStruct((L, R, C), F32)
    return _pcall(
        body, grid=(L, nblk), in_specs=[tile, tile, tile] + [gp_spec(t) for t in range(L)],
        out_specs=(tile, tile, tile, tile), out_shape=(shp, shp, shp, shp),
        semantics=("arbitrary", "arbitrary"), name=name, args=(w, m, v, *gparts), side=side)


def _adamw_ada(w, m, v, cond_t, dmod, *, tr, name):
    L, D, n = w.shape
    Bp = cond_t.shape[1]

    def body(w_ref, m_ref, v_ref, c_ref, dm_ref, g_ref, d_ref, m2_ref, v2_ref):
        g = _bdot(c_ref[...], dm_ref[...])
        delta, m2, v2 = _adam_math(w_ref[...], g, m_ref[...], v_ref[...])
        g_ref[...] = g
        d_ref[...] = delta
        m2_ref[...] = m2
        v2_ref[...] = v2

    tile = pl.BlockSpec((None, tr, n), lambda l, i: (l, i, 0))
    shp = jax.ShapeDtypeStruct((L, D, n), F32)
    return pl.pallas_call(
        body, grid=(L, D // tr),
        in_specs=[tile, tile, tile, pl.BlockSpec((tr, Bp), lambda l, i: (i, 0)),
                  pl.BlockSpec((None, Bp, n), lambda l, i: (l, 0, 0))],
        out_specs=(tile, tile, tile, tile), out_shape=(shp, shp, shp, shp),
        compiler_params=_params(("parallel", "parallel")), name=name)(w, m, v, cond_t, dmod)


def _allgather_small(block, *, name):
    R, C = block.shape

    def body(x_ref, out_ref, send_sems, recv_sems, local_sem):
        x, y, c = lax.axis_index("x"), lax.axis_index("y"), lax.axis_index("c")

        def rows(px, py, pc):
            return out_ref.at[pl.ds((4 * px + 2 * py + pc) * R, R), :]

        mine = pltpu.make_async_copy(x_ref, rows(x, y, c), local_sem)
        mine.start()
        sends = []
        for rel in range(1, N_DEV):
            to = _peer(x, y, c, rel)
            cp = pltpu.make_async_remote_copy(src_ref=x_ref, dst_ref=rows(x, y, c),
                                              send_sem=send_sems.at[rel - 1], recv_sem=recv_sems.at[rel - 1],
                                              device_id=to, device_id_type=MESH)
            cp.start()
            sends.append(cp)
        for rel in range(1, N_DEV):
            frm = _peer(x, y, c, rel)
            pltpu.make_async_remote_copy(src_ref=x_ref, dst_ref=rows(*frm),
                                         send_sem=send_sems.at[rel - 1], recv_sem=recv_sems.at[rel - 1],
                                         device_id=frm, device_id_type=MESH).wait_recv()
        for cp in sends:
            cp.wait_send()
        mine.wait()

    return pl.pallas_call(
        body, out_shape=jax.ShapeDtypeStruct((N_DEV * R, C), block.dtype),
        in_specs=[pl.BlockSpec(memory_space=pltpu.VMEM)],
        out_specs=pl.BlockSpec(memory_space=pltpu.VMEM),
        scratch_shapes=[pltpu.SemaphoreType.DMA((N_DEV - 1,)), pltpu.SemaphoreType.DMA((N_DEV - 1,)),
                        pltpu.SemaphoreType.DMA],
        compiler_params=pltpu.CompilerParams(vmem_limit_bytes=V7X_VMEM_LIMIT), name=name)(block)


def _allgather_hbm(shards, *, name):
    n = len(shards)

    def body(*refs):
        ins = refs[:n]
        outs = refs[n:2 * n]
        send_sems, recv_sems, local_sems = refs[2 * n:]
        x, y, c = lax.axis_index("x"), lax.axis_index("y"), lax.axis_index("c")
        sibling = (x, y, 1 - c)
        chips = [(1 - x, y), (x, 1 - y), (1 - x, 1 - y)]

        def slot(t, px, py, pc):
            return outs[t].at[4 * px + 2 * py + pc]

        def copy(t, k, block, to, src=None):
            return pltpu.make_async_remote_copy(
                src_ref=slot(t, *block) if src is None else src, dst_ref=slot(t, *block),
                send_sem=send_sems.at[t * 7 + k], recv_sem=recv_sems.at[t * 7 + k],
                device_id=to, device_id_type=MESH)

        me = (x, y, c)
        started = []
        mine = []
        for t in range(n):
            cp = pltpu.make_async_copy(ins[t], slot(t, *me), local_sems.at[t])
            cp.start()
            mine.append(cp)
            first = [copy(t, 0, me, sibling, src=ins[t])]
            first += [copy(t, 1 + j, me, (*chip, c), src=ins[t]) for j, chip in enumerate(chips)]
            for cp in first:
                cp.start()
            started += first
        for t in range(n):
            for j, chip in enumerate(chips):
                copy(t, 1 + j, (*chip, c), me).wait_recv()
                fwd = copy(t, 4 + j, (*chip, c), sibling)
                fwd.start()
                started.append(fwd)
        for t in range(n):
            copy(t, 0, sibling, me).wait_recv()
            for j, chip in enumerate(chips):
                copy(t, 4 + j, (*chip, 1 - c), me).wait_recv()
        for cp in started:
            cp.wait_send()
        for cp in mine:
            cp.wait()

    any_spec = pl.BlockSpec(memory_space=pl.ANY)
    return pl.pallas_call(
        body, out_shape=[jax.ShapeDtypeStruct((N_DEV,) + s.shape, s.dtype) for s in shards],
        in_specs=[any_spec] * n, out_specs=[any_spec] * n,
        scratch_shapes=[pltpu.SemaphoreType.DMA((7 * n,)), pltpu.SemaphoreType.DMA((7 * n,)),
                        pltpu.SemaphoreType.DMA((n,))],
        name=name)(*shards)


def _tile(total, want):
    step = 128 if total % 128 == 0 else 8
    best = step
    t = step
    while t <= min(total, want):
        if total % t == 0:
            best = t
        t += step
    return best


def _local_step(x, target, mods, lbs, p, wg, shards=None):
    S, D = x.shape
    L = mods.shape[0]
    W = D // 2
    H = W // HEAD
    mesh = shards is not None
    F = shards["w_ffn_out"][0].shape[0] * N_DEV if mesh else wg["w_ffn_out"][0].shape[0]
    tm = _tile(S, 512)
    tm_big = _tile(S, 1024)
    tm_tn = _tile(S, 2048)
    tm_sw = _tile(S, 128)
    tq = _tile(S, 256)
    cg = max(1, min(32, S // CHUNK))
    nb_out = _tile(D, 1024)
    kb_f = _tile(F, 1408)

    def row(a, l):
        return a[l][None, :]

    first_plan = dict(proj=[("w_ffn_in", 0)], hg=[("w_ffn_out", 0), ("w_out", 0)],
                      sb=[("w_in", 1), ("w_out", 1)], ffn=[("w_ffn_in", 1), ("w_ffn_out", 1)])

    def gather_plan(l, call):
        if not mesh:
            return []
        if l == 0:
            plan = first_plan[call]
        else:
            plan = [(dict(proj="w_in", hg="w_out", sb="w_ffn_out", ffn="w_ffn_in")[call], l + 1)]
        return [(k, j) for k, j in plan if j < L]

    def gather_of(plan):
        return (True, [shards[k][j] for k, j in plan]) if plan else None

    def store_gathered(plan, got):
        for (k, j), g in zip(plan, got or []):
            wg[k][j] = g.reshape(-1, D) if k in ("w_out", "w_ffn_out") else g

    def scatter_of(blocks):
        if mesh and blocks is not None:
            return (False, [b.reshape((N_DEV, -1) + b.shape[-1:]) if b.ndim == 2 else b for b in blocks])
        return None

    saved = []
    xcur = x
    for l in range(L):
        mod = mods[l]
        sh1, sc1, g1, sh2, sc2, g2 = [mod[:, i * D:(i + 1) * D] for i in range(N_MOD)]
        h1 = _modnorm_fwd(xcur, row(p["norm1_g"], l), sc1, sh1, tm=tm, name="norm1_fwd")
        plan = gather_plan(l, "proj")
        proj, got = _mm_nn(h1, wg["w_in"][l], tm=tm_big, name="proj_fwd", side=gather_of(plan))
        store_gathered(plan, got)
        lb = lbs[l][None, :]
        plan = gather_plan(l, "hg")
        (o_hg, on_hg, states), got = _hg_fwd(proj, lb, row(p["hg_out_g"], l), n_heads=H, cg=cg, name="hgrn2_fwd",
                                             side=gather_of(plan))
        store_gathered(plan, got)
        qh, kh, vh = _sb_pre(proj, row(p["sb_q_g"], l), row(p["sb_k_g"], l), n_heads=H, col0=4 * H,
                             tm=tm_tn, name="sb_qknorm_fwd")
        plan = gather_plan(l, "sb")
        (o_sb, on_sb), got = _sb_fwd(qh, kh, vh, row(p["sb_out_g"], l), n_heads=H, tq=tq, name="sb_fwd",
                                     side=gather_of(plan))
        store_gathered(plan, got)
        o_cat = jnp.concatenate([on_hg, on_sb], axis=1)
        (x1, mixed, h2), _ = _mm_nn(o_cat, wg["w_out"][l], tm=tm, nb=D, resid=xcur, gate=g1,
                                    norm=(row(p["norm2_g"], l), sc2, sh2), name="out_proj_fwd")
        w_fin = wg["w_ffn_in"][l]
        if w_fin.shape[0] % 2:
            w_fin = jnp.stack([w_fin[0][:, :F], w_fin[0][:, F:]])
        plan = gather_plan(l, "ffn")
        (gate, up, a), got = _ffn_in_fwd(h2, w_fin, tm=tm, nb=F, name="ffn_in_fwd", side=gather_of(plan))
        store_gathered(plan, got)
        (x2, ffn), _ = _mm_nn(a, wg["w_ffn_out"][l], tm=tm, nb=nb_out // 2, resid=x1, gate=g2, name="ffn_out_fwd")
        saved.append(dict(x=xcur, h1=h1, proj=proj, o_hg=o_hg, o_sb=o_sb, states=states, qh=qh, kh=kh, vh=vh,
                          o_cat=o_cat, mixed=mixed, x1=x1, h2=h2, gate=gate, up=up, a=a, ffn=ffn, lb=lb,
                          sc1=sc1, g1=g1, sc2=sc2, g2=g2, w_fin=w_fin))
        xcur = x2

    last = saved[-1]
    dx, dffn, dg2, loss = _loss_bwd(xcur, target, last["ffn"], last["g2"], tm=tm, name="loss_bwd")

    big = {k: [None] * L for k in ("w_in", "w_out", "w_ffn_in", "w_ffn_out")}
    small = {k: [None] * L for k in ("norm1_g", "hg_lb", "hg_out_g", "sb_q_g", "sb_k_g", "sb_out_g", "norm2_g")}
    dmods = [None] * L
    for l in reversed(range(L)):
        sv = saved[l]
        (dgate, dup), _ = _ffn_out_bwd_x(dffn, wg["w_ffn_out"][l], sv["gate"], sv["up"], tm=tm, kb=kb_f,
                                         name="ffn_out_bwd_x")
        g_fout, _ = _mm_tn(sv["a"], dffn, tm=tm_tn, kb=kb_f, nb=nb_out, blocked=False, name="ffn_out_bwd_w")
        g_fin, _ = _mm_tn(sv["h2"], dgate, dy2=dup, tm=tm_tn, kb=_tile(D, 1024), nb=sv["w_fin"].shape[2],
                          blocked=True, name="ffn_in_bwd_w")
        dh2, got = _mm_nt(dgate, sv["w_fin"], dy2=dup, tm=tm_big, kb=D, name="ffn_in_bwd_x",
                          side=scatter_of([g_fin]))
        big["w_ffn_in"][l] = g_fin if got is None else got[0]
        dx1, dmixed, dg1, dsh2, dsc2, dn2 = _modnorm_bwd(
            sv["x1"], dh2, dx, row(p["norm2_g"], l), sv["sc2"], sv["mixed"], sv["g1"], tm=tm_sw * 2,
            name="norm2_bwd")
        small["norm2_g"][l] = dn2
        d_ocat, _ = _mm_nt(dmixed, wg["w_out"][l], tm=tm_big, kb=nb_out, nb=D, name="out_proj_bwd_x")
        g_out, _ = _mm_tn(sv["o_cat"], dmixed, tm=tm_tn, kb=D, nb=nb_out, blocked=False, name="out_proj_bwd_w")
        (dhq, dhf, dhi, dhg, dlb, dhog), got = _hg_bwd(sv["proj"], sv["o_hg"], d_ocat, 0, sv["states"], sv["lb"],
                                                       row(p["hg_out_g"], l), n_heads=H, cg=cg, name="hgrn2_bwd",
                                                       side=scatter_of([g_out]))
        big["w_out"][l] = g_out if got is None else got[0]
        (dqh, dkh, dvh, dsog), got = _sb_bwd(sv["qh"], sv["kh"], sv["vh"], sv["o_sb"], d_ocat, H,
                                             row(p["sb_out_g"], l), n_heads=H, tq=tq, name="sb_bwd",
                                             side=scatter_of([g_fout]))
        big["w_ffn_out"][l] = g_fout if got is None else got[0]
        dsq, dsk, dsv, dqg, dkg = _sb_pre_bwd(sv["proj"], dqh, dkh, dvh, row(p["sb_q_g"], l),
                                              row(p["sb_k_g"], l), n_heads=H, col0=4 * H, tm=tm_tn,
                                              name="sb_qknorm_bwd")
        small["hg_lb"][l] = dlb
        small["hg_out_g"][l] = dhog
        small["sb_out_g"][l] = dsog
        small["sb_q_g"][l] = dqg
        small["sb_k_g"][l] = dkg
        dproj = jnp.concatenate([dhq, dhf, dhi, dhg, dsq, dsk, dsv], axis=1)
        g_in, _ = _mm_tn(sv["h1"], dproj, tm=tm_tn, kb=D, nb=wg["w_in"][l].shape[2], blocked=True,
                         name="proj_bwd_w")
        dh1, got = _mm_nt(dproj, wg["w_in"][l], tm=tm_big, kb=D, name="proj_bwd_x", side=scatter_of([g_in]))
        big["w_in"][l] = g_in if got is None else got[0]
        if l > 0:
            prev = saved[l - 1]
            dx0, dffn_prev, dg2_prev, dsh1, dsc1, dn1 = _modnorm_bwd(
                sv["x"], dh1, dx1, row(p["norm1_g"], l), sv["sc1"], prev["ffn"], prev["g2"], tm=tm_sw * 2,
                name="norm1_bwd")
        else:
            dx0, dsh1, dsc1, dn1 = _modnorm_bwd(sv["x"], dh1, dx1, row(p["norm1_g"], l), sv["sc1"], None, None,
                                                tm=tm_sw * 2, name="norm1_bwd_first")
            dffn_prev, dg2_prev = None, None
        small["norm1_g"][l] = dn1
        dmods[l] = jnp.concatenate([dsh1, dsc1, dg1, dsh2, dsc2, dg2], axis=1)
        dx, dffn, dg2 = dx0, dffn_prev, dg2_prev
    return loss, dx, big, small, dmods


def kernel(x, c, norm1_g, w_in, hg_lb_logits, hg_out_g, sb_q_g, sb_k_g, sb_out_g, w_out, norm2_g, w_ffn_in, w_ffn_out, w_ada, b_ada, loss_target, m_norm1_g, m_w_in, m_hg_lb_logits, m_hg_out_g, m_sb_q_g, m_sb_k_g, m_sb_out_g, m_w_out, m_norm2_g, m_w_ffn_in, m_w_ffn_out, m_w_ada, m_b_ada, v_norm1_g, v_w_in, v_hg_lb_logits, v_hg_out_g, v_sb_q_g, v_sb_k_g, v_sb_out_g, v_w_out, v_norm2_g, v_w_ffn_in, v_w_ffn_out, v_w_ada, v_b_ada):
    L, D = norm1_g.shape
    S = x.shape[1]
    me = 4 * lax.axis_index("x") + 2 * lax.axis_index("y") + lax.axis_index("c")

    c_all = _allgather_small(jnp.broadcast_to(c, (8, D)), name="gather_c").reshape(N_DEV, 8, D)[:, 0, :]
    n_ada = w_ada.shape[2]
    mod_cols, cond = _ada_mod(c_all, w_ada, nb=_tile(n_ada, 512), name="ada_mod")
    mod_all = _allgather_small(mod_cols.reshape(L * N_DEV, n_ada), name="gather_mod")
    mod_all = mod_all.reshape(N_DEV, L, N_DEV, n_ada)
    mod_mine = lax.dynamic_index_in_dim(mod_all, me, axis=2, keepdims=False)
    mods = jnp.transpose(mod_mine, (1, 0, 2)).reshape(L, 1, N_DEV * n_ada) + b_ada[:, None, :]

    lbs = _lb_fwd(hg_lb_logits, name="lower_bounds_fwd")

    shards = dict(w_in=[w_in[l].astype(BF16) for l in range(L)], w_out=[w_out[l].astype(BF16) for l in range(L)],
                  w_ffn_in=[w_ffn_in[l].astype(BF16) for l in range(L)],
                  w_ffn_out=[w_ffn_out[l].astype(BF16) for l in range(L)])
    g_in, = _allgather_hbm([shards["w_in"][0]], name="gather_first_weight")
    wg = dict(w_in=[g_in] + [None] * (L - 1), w_out=[None] * L, w_ffn_in=[None] * L, w_ffn_out=[None] * L)

    p = dict(norm1_g=norm1_g, hg_out_g=hg_out_g, sb_q_g=sb_q_g, sb_k_g=sb_k_g, sb_out_g=sb_out_g,
             norm2_g=norm2_g)
    loss_part, grad_x, recv, small, dmods = _local_step(x.reshape(S, D), loss_target.reshape(S, D), mods, lbs, p,
                                                        wg, shards)

    dmod = jnp.concatenate(dmods, axis=0)
    pieces = [jnp.concatenate(small[k], axis=0) for k in
              ("norm1_g", "hg_lb", "hg_out_g", "sb_q_g", "sb_k_g", "sb_out_g", "norm2_g")] + [dmod]
    flat = jnp.concatenate([a.reshape(-1) for a in pieces] + [loss_part.reshape(-1)])
    n_flat = flat.shape[0]
    rows = -(-n_flat // 1024) * 8
    flat = jnp.pad(flat, (0, rows * 128 - n_flat)).reshape(rows, 128)
    gathered = _allgather_small(flat, name="gather_small_grads").reshape(N_DEV, rows * 128)

    def take(off, shape):
        size = 1
        for s in shape:
            size *= s
        return gathered[:, off:off + size].reshape((N_DEV,) + tuple(shape)), off + size

    off = 0
    parts = {}
    for k, a in zip(("norm1_g", "hg_lb", "hg_out_g", "sb_q_g", "sb_k_g", "sb_out_g", "norm2_g", "dmod"), pieces):
        parts[k], off = take(off, a.shape)
    loss_parts = gathered[:, off:off + 1]
    loss = jnp.sum(loss_parts)

    def pad8(a):
        return jnp.pad(a, ((0, 0), (0, 8 - a.shape[1]), (0, 0)))

    def small_update(w, m, v, gparts):
        Lw = w.shape[0]
        g, d, m2, v2 = _adamw(pad8(w[None])[0], pad8(m[None])[0], pad8(v[None])[0], pad8(gparts),
                              tr=8, name="adamw_small")
        return g[:Lw], d[:Lw], m2[:Lw], v2[:Lw]

    out = {}
    out["norm1_g"] = small_update(norm1_g, m_norm1_g, v_norm1_g, parts["norm1_g"])
    dlogits = _lb_bwd(hg_lb_logits, parts["hg_lb"], name="lower_bounds_bwd")
    out["hg_lb_logits"] = small_update(hg_lb_logits, m_hg_lb_logits, v_hg_lb_logits, dlogits[None])
    out["hg_out_g"] = small_update(hg_out_g, m_hg_out_g, v_hg_out_g, parts["hg_out_g"])
    out["sb_q_g"] = small_update(sb_q_g, m_sb_q_g, v_sb_q_g, parts["sb_q_g"])
    out["sb_k_g"] = small_update(sb_k_g, m_sb_k_g, v_sb_k_g, parts["sb_k_g"])
    out["sb_out_g"] = small_update(sb_out_g, m_sb_out_g, v_sb_out_g, parts["sb_out_g"])
    out["norm2_g"] = small_update(norm2_g, m_norm2_g, v_norm2_g, parts["norm2_g"])
    out["b_ada"] = small_update(b_ada, m_b_ada, v_b_ada, parts["dmod"])

    dmod_all = parts["dmod"].reshape(N_DEV, L, N_DEV, n_ada)
    dmod_mine = lax.dynamic_index_in_dim(dmod_all, me, axis=2, keepdims=False)
    dmod_mine = jnp.pad(jnp.transpose(dmod_mine, (1, 0, 2)), ((0, 0), (0, 128 - N_DEV), (0, 0)))
    cond_t = jnp.pad(jnp.transpose(cond), ((0, 0), (0, 128 - N_DEV)))
    out["w_ada"] = _adamw_ada(w_ada, m_w_ada, v_w_ada, cond_t, dmod_mine, tr=_tile(D, 256), name="adamw_ada")

    def big_update(w, m, v, recv_l, name):
        return _adamw_layers(w, m, v, recv_l, tr=_tile(w.shape[1], 131072 // w.shape[2]), name=name)[0]

    out["w_ffn_in"] = big_update(w_ffn_in, m_w_ffn_in, v_w_ffn_in, recv["w_ffn_in"], "adamw_w_ffn_in")
    out["w_ffn_out"] = big_update(w_ffn_out, m_w_ffn_out, v_w_ffn_out, recv["w_ffn_out"], "adamw_w_ffn_out")
    out["w_out"] = big_update(w_out, m_w_out, v_w_out, recv["w_out"], "adamw_w_out")
    out["w_in"] = big_update(w_in, m_w_in, v_w_in, recv["w_in"], "adamw_w_in")

    order = ("norm1_g", "w_in", "hg_lb_logits", "hg_out_g", "sb_q_g", "sb_k_g", "sb_out_g", "w_out", "norm2_g",
             "w_ffn_in", "w_ffn_out", "w_ada", "b_ada")
    grads = [out[k][0] for k in order]
    deltas = [out[k][1] for k in order]
    new_m = [out[k][2] for k in order]
    new_v = [out[k][3] for k in order]
    return (loss, grad_x.reshape(1, S, D), *grads, *deltas, *new_m, *new_v)
```

```python
import functools

import jax
import jax.numpy as jnp
from jax import lax
from jax.experimental import pallas as pl
from jax.experimental.pallas import tpu as pltpu

F32 = jnp.float32
BF16 = jnp.bfloat16
MESH = pl.DeviceIdType.MESH

N_DEV = 8
HEAD = 128
CHUNK = 64
N_MOD = 6
EPS = 1e-6
TINY = 1e-30
ADAM_LR = 0.001
ADAM_B1 = 0.9
ADAM_B2 = 0.999
ADAM_EPS = 1e-08
ADAM_WD = 0.01
ADAM_STEP = 10
V7X_VMEM_LIMIT = 56 * 1024 * 1024
SKIP_LOG = -104.0


def _params(sem):
    return pltpu.CompilerParams(dimension_semantics=sem, vmem_limit_bytes=V7X_VMEM_LIMIT)


def _bdot(a, b, dims=(((1,), (0,)), ((), ()))):
    return lax.dot_general(a.astype(BF16), b.astype(BF16), dims, preferred_element_type=F32)


_NT = (((1,), (1,)), ((), ()))
_TN = (((0,), (0,)), ((), ()))


def _sigmoid_pair(x):
    e = jnp.exp(-jnp.abs(x))
    r = 1.0 / (1.0 + e)
    er = e * r
    pos = x >= 0
    return jnp.where(pos, r, er), jnp.where(pos, er, r)


def _split_dot(x, u, parts):
    acc = None
    rem = x
    for _ in range(parts):
        p = rem.astype(BF16)
        rem = rem - p.astype(F32)
        t = lax.dot_general(p, u, (((1,), (0,)), ((), ())), preferred_element_type=F32)
        acc = t if acc is None else acc + t
    return acc


def _split_dot_left(u, x, parts):
    acc = None
    rem = x
    for _ in range(parts):
        p = rem.astype(BF16)
        rem = rem - p.astype(F32)
        t = lax.dot_general(u, p, (((1,), (0,)), ((), ())), preferred_element_type=F32)
        acc = t if acc is None else acc + t
    return acc


def _peer(x, y, c, rel):
    return (x ^ ((rel >> 2) & 1), y ^ ((rel >> 1) & 1), c ^ (rel & 1))


def _exchange(gather, srcs, dsts, send_sems, recv_sems, local_sems, phase):
    x, y, c = lax.axis_index("x"), lax.axis_index("y"), lax.axis_index("c")
    me = 4 * x + 2 * y + c
    for t in range(len(srcs)):
        own = srcs[t] if gather else srcs[t].at[me]
        local = pltpu.make_async_copy(own, dsts[t].at[me], local_sems.at[t])
        if phase == "start":
            local.start()
        for rel in range(1, N_DEV):
            px, py, pc = _peer(x, y, c, rel)
            pid = 4 * px + 2 * py + pc
            k = t * (N_DEV - 1) + rel - 1
            if phase == "start":
                pltpu.make_async_remote_copy(
                    src_ref=srcs[t] if gather else srcs[t].at[pid], dst_ref=dsts[t].at[me],
                    send_sem=send_sems.at[k], recv_sem=recv_sems.at[k],
                    device_id=(px, py, pc), device_id_type=MESH).start()
            else:
                cp = pltpu.make_async_remote_copy(
                    src_ref=own, dst_ref=dsts[t].at[pid], send_sem=send_sems.at[k], recv_sem=recv_sems.at[k],
                    device_id=(px, py, pc), device_id_type=MESH)
                cp.wait_recv()
                cp.wait_send()
        if phase == "wait":
            local.wait()


def _gather_two_level(srcs, dsts, send_sems, recv_sems, local_sems, phase):
    x, y, c = lax.axis_index("x"), lax.axis_index("y"), lax.axis_index("c")
    me, sibling = (x, y, c), (x, y, 1 - c)
    chips = [(1 - x, y), (x, 1 - y), (1 - x, 1 - y)]
    per = N_DEV - 1
    for t in range(len(srcs)):
        def slot(px, py, pc, t=t):
            return dsts[t].at[4 * px + 2 * py + pc]

        def copy(k, block, to, src=None, t=t):
            return pltpu.make_async_remote_copy(
                src_ref=slot(*block) if src is None else src, dst_ref=slot(*block),
                send_sem=send_sems.at[t * per + k], recv_sem=recv_sems.at[t * per + k],
                device_id=to, device_id_type=MESH)

        local = pltpu.make_async_copy(srcs[t], slot(*me), local_sems.at[t])
        first = [copy(0, me, sibling, src=srcs[t])]
        first += [copy(1 + j, me, (*chip, c), src=srcs[t]) for j, chip in enumerate(chips)]
        passed = [copy(4 + j, (*chip, c), sibling) for j, chip in enumerate(chips)]
        if phase == "start":
            local.start()
            for cp in first:
                cp.start()
        elif phase == "forward":
            for j, chip in enumerate(chips):
                copy(1 + j, (*chip, c), me).wait_recv()
                passed[j].start()
        else:
            copy(0, sibling, me).wait_recv()
            for j, chip in enumerate(chips):
                copy(4 + j, (*chip, 1 - c), me).wait_recv()
            for cp in first + passed:
                cp.wait_send()
            local.wait()


def _exchange_scratch(n):
    return [pltpu.SemaphoreType.DMA(((N_DEV - 1) * n,)), pltpu.SemaphoreType.DMA(((N_DEV - 1) * n,)),
            pltpu.SemaphoreType.DMA((n,))]


def _pcall(body, *, grid, in_specs, out_specs, out_shape, scratch_shapes=(), semantics, name, args, side=None):
    single = not isinstance(out_shape, (tuple, list))
    if single:
        out_specs, out_shape = [out_specs], [out_shape]
    in_specs, out_specs, out_shape = list(in_specs), list(out_specs), list(out_shape)
    scratch_shapes = list(scratch_shapes)
    n_in, n_out, n_scr = len(in_specs), len(out_specs), len(scratch_shapes)
    if side is None:
        res = pl.pallas_call(body, grid=grid, in_specs=in_specs, out_specs=out_specs, out_shape=out_shape,
                             scratch_shapes=scratch_shapes, compiler_params=_params(semantics), name=name)(*args)
        return (res[0] if single else tuple(res)), None
    gather, srcs = side
    n = len(srcs)

    def full(*refs):
        ins = refs[:n_in]
        s_in = refs[n_in:n_in + n]
        outs = refs[n_in + n:n_in + n + n_out]
        s_out = refs[n_in + n + n_out:n_in + 2 * n + n_out]
        scr = refs[n_in + 2 * n + n_out:n_in + 2 * n + n_out + n_scr]
        send_sems, recv_sems, local_sems = refs[n_in + 2 * n + n_out + n_scr:]
        step = pl.program_id(0)
        steps = grid[0]
        for ax in range(1, len(grid)):
            step = step * grid[ax] + pl.program_id(ax)
            steps *= grid[ax]

        def exchange(phase):
            if gather:
                _gather_two_level(s_in, s_out, send_sems, recv_sems, local_sems, phase)
            elif phase != "forward":
                _exchange(False, s_in, s_out, send_sems, recv_sems, local_sems, phase)

        pl.when(step == 0)(lambda: exchange("start"))
        if gather:
            pl.when(step == (steps * 4) // 5)(lambda: exchange("forward"))
        body(*ins, *outs, *scr)
        pl.when(step == steps - 1)(lambda: exchange("wait"))

    any_spec = pl.BlockSpec(memory_space=pl.ANY)
    s_shapes = [jax.ShapeDtypeStruct(((N_DEV,) + s.shape) if gather else s.shape, s.dtype) for s in srcs]
    res = pl.pallas_call(full, grid=grid, in_specs=in_specs + [any_spec] * n,
                         out_specs=out_specs + [any_spec] * n, out_shape=out_shape + s_shapes,
                         scratch_shapes=scratch_shapes + _exchange_scratch(n),
                         compiler_params=_params(("arbitrary",) * len(grid)), name=name)(*args, *srcs)
    main = res[:n_out]
    return (main[0] if single else tuple(main)), list(res[n_out:])


def _mm_nn(a, b, *, tm, nb=None, out_dtype=F32, resid=None, gate=None, norm=None, name, side=None):
    M, K = a.shape
    if b.ndim == 3:
        NB, _, n = b.shape
        b_spec = pl.BlockSpec((None, K, n), lambda j, i: (j, 0, 0))
    else:
        n = nb
        NB = b.shape[1] // nb
        b_spec = pl.BlockSpec((K, n), lambda j, i: (0, j))
    N = NB * n
    epi = resid is not None
    assert norm is None or (epi and NB == 1)

    def body(*refs):
        if norm is not None:
            a_ref, b_ref, r_ref, g_ref, ng_ref, sc_ref, sh_ref, o_ref, acc_ref, h_ref = refs
        elif epi:
            a_ref, b_ref, r_ref, g_ref, o_ref, acc_ref = refs
        else:
            a_ref, b_ref, o_ref = refs
        acc = jnp.dot(a_ref[...], b_ref[...], preferred_element_type=F32)
        if epi:
            xv = r_ref[...] + g_ref[...] * acc
            o_ref[...] = xv
            acc_ref[...] = acc.astype(BF16)
            if norm is not None:
                rstd = lax.rsqrt(jnp.mean(xv * xv, axis=-1, keepdims=True) + EPS)
                y = (xv * rstd) * ng_ref[...]
                h_ref[...] = (y * (1.0 + sc_ref[...]) + sh_ref[...]).astype(BF16)
        else:
            o_ref[...] = acc.astype(out_dtype)

    in_specs = [pl.BlockSpec((tm, K), lambda j, i: (i, 0)), b_spec]
    args = [a, b]
    o_spec = pl.BlockSpec((tm, n), lambda j, i: (i, j))
    if epi:
        row = pl.BlockSpec((1, n), lambda j, i: (0, j))
        in_specs += [pl.BlockSpec((tm, n), lambda j, i: (i, j)), row]
        args += [resid, gate]
        out_shape = [jax.ShapeDtypeStruct((M, N), F32), jax.ShapeDtypeStruct((M, N), BF16)]
        out_specs = [o_spec, o_spec]
        if norm is not None:
            in_specs += [row, row, row]
            args += list(norm)
            out_shape.append(jax.ShapeDtypeStruct((M, N), BF16))
            out_specs.append(o_spec)
    else:
        out_shape = [jax.ShapeDtypeStruct((M, N), out_dtype)]
        out_specs = [o_spec]
    res, got = _pcall(body, grid=(NB, M // tm), in_specs=in_specs, out_specs=out_specs, out_shape=out_shape,
                      semantics=("parallel", "parallel"), name=name, args=args, side=side)
    return (res[0] if len(res) == 1 else res), got


def _halves(dy, dy2, blk_rows, blk_cols, nblocks, row_of, col_of, last_row=None):
    if dy2 is None:
        return [pl.BlockSpec((blk_rows, blk_cols), lambda *g: (row_of(*g), col_of(*g)))], None
    half = nblocks // 2

    def left(*g):
        r, c = row_of(*g), col_of(*g)
        if last_row is None:
            return (r, jnp.minimum(c, half - 1))
        return (jnp.where(c < half, r, last_row), jnp.minimum(c, half - 1))

    def right(*g):
        r, c = row_of(*g), col_of(*g)
        if last_row is None:
            return (r, jnp.maximum(c - half, 0))
        return (jnp.where(c >= half, r, 0), jnp.maximum(c - half, 0))

    return [pl.BlockSpec((blk_rows, blk_cols), left), pl.BlockSpec((blk_rows, blk_cols), right)], half


def _mm_nt(dy, w, *, tm, kb, nb=None, name, side=None, dy2=None):
    M = dy.shape[0]
    N = dy.shape[1] * (1 if dy2 is None else 2)
    if w.ndim == 3:
        NB, Kt, n = w.shape
        w_spec = pl.BlockSpec((None, kb, n), lambda i, k, j: (j, k, 0))
    else:
        Kt = w.shape[0]
        n = nb
        NB = N // nb
        w_spec = pl.BlockSpec((kb, n), lambda i, k, j: (k, j))
    KB = Kt // kb
    dy_specs, half = _halves(dy, dy2, tm, n, NB, lambda i, k, j: i, lambda i, k, j: j)
    n_op = len(dy_specs)

    def body(*refs):
        dy_refs = refs[:n_op]
        w_ref = refs[n_op]
        o_ref = refs[n_op + 1]
        acc = refs[n_op + 2:]
        j = pl.program_id(2)

        def use(dy_ref):
            part = lax.dot_general(dy_ref[...], w_ref[...], _NT, preferred_element_type=F32)
            if NB == 1:
                o_ref[...] = part
                return
            acc_ref, = acc

            @pl.when(j == 0)
            def _():
                acc_ref[...] = part

            @pl.when(jnp.logical_and(j > 0, j < NB - 1))
            def _():
                acc_ref[...] += part

            @pl.when(j == NB - 1)
            def _():
                o_ref[...] = acc_ref[...] + part

        if half is None:
            use(dy_refs[0])
        else:
            pl.when(j < half)(lambda: use(dy_refs[0]))
            pl.when(j >= half)(lambda: use(dy_refs[1]))

    return _pcall(
        body, grid=(M // tm, KB, NB), in_specs=dy_specs + [w_spec],
        out_specs=pl.BlockSpec((tm, kb), lambda i, k, j: (i, k)),
        out_shape=jax.ShapeDtypeStruct((M, Kt), F32),
        scratch_shapes=[] if NB == 1 else [pltpu.VMEM((tm, kb), F32)],
        semantics=("parallel", "parallel", "arbitrary"), name=name,
        args=(dy, w) if dy2 is None else (dy, dy2, w), side=side)


def _mm_tn(x, dy, *, tm, kb, nb, blocked, name, side=None, dy2=None):
    M, K = x.shape
    N = dy.shape[1] * (1 if dy2 is None else 2)
    KB, NB, MB = K // kb, N // nb, M // tm
    dy_specs, half = _halves(dy, dy2, tm, nb, NB, lambda k, n, m: m, lambda k, n, m: n, last_row=MB - 1)
    n_dy = len(dy_specs)

    def body(*refs):
        x_ref = refs[0]
        dy_refs = refs[1:1 + n_dy]
        o_ref = refs[1 + n_dy]
        acc = refs[2 + n_dy:]
        m = pl.program_id(2)

        def use(dy_ref):
            part = lax.dot_general(x_ref[...], dy_ref[...], _TN, preferred_element_type=F32)
            if MB == 1:
                o_ref[...] = part.astype(BF16)
                return
            acc_ref, = acc

            @pl.when(m == 0)
            def _():
                acc_ref[...] = part

            @pl.when(jnp.logical_and(m > 0, m < MB - 1))
            def _():
                acc_ref[...] += part

            @pl.when(m == MB - 1)
            def _():
                o_ref[...] = (acc_ref[...] + part).astype(BF16)

        if half is None:
            use(dy_refs[0])
        else:
            nblk = pl.program_id(1)
            pl.when(nblk < half)(lambda: use(dy_refs[0]))
            pl.when(nblk >= half)(lambda: use(dy_refs[1]))

    if blocked:
        out_shape = jax.ShapeDtypeStruct((NB, K, nb), BF16)
        o_spec = pl.BlockSpec((None, kb, nb), lambda k, n, m: (n, k, 0))
    else:
        out_shape = jax.ShapeDtypeStruct((K, N), BF16)
        o_spec = pl.BlockSpec((kb, nb), lambda k, n, m: (k, n))
    x_spec = pl.BlockSpec((tm, kb), lambda k, n, m: (m, k))
    return _pcall(
        body, grid=(KB, NB, MB), in_specs=[x_spec] + dy_specs,
        out_specs=o_spec, out_shape=out_shape,
        scratch_shapes=[] if MB == 1 else [pltpu.VMEM((kb, nb), F32)],
        semantics=("parallel", "parallel", "arbitrary"), name=name,
        args=(x, dy) if dy2 is None else (x, dy, dy2), side=side)


def _ffn_in_fwd(h, w, *, tm, nb, name, side=None):
    M, K = h.shape
    if w.ndim == 3:
        J, _, n = w.shape
        half = J // 2
        specs = [pl.BlockSpec((None, K, n), lambda j, i: (j, 0, 0)),
                 pl.BlockSpec((None, K, n), lambda j, i: (j + half, 0, 0))]
    else:
        n = nb
        half = w.shape[1] // (2 * nb)
        specs = [pl.BlockSpec((K, n), lambda j, i: (0, j)), pl.BlockSpec((K, n), lambda j, i: (0, j + half))]
    F = half * n

    def body(h_ref, wg_ref, wu_ref, gate_ref, up_ref, act_ref):
        hv = h_ref[...]
        gate = jnp.dot(hv, wg_ref[...], preferred_element_type=F32)
        up = jnp.dot(hv, wu_ref[...], preferred_element_type=F32)
        s, _ = _sigmoid_pair(gate)
        gate_ref[...] = gate
        up_ref[...] = up
        act_ref[...] = (gate * s * up).astype(BF16)

    o_spec = pl.BlockSpec((tm, n), lambda j, i: (i, j))
    f32 = jax.ShapeDtypeStruct((M, F), F32)
    return _pcall(
        body, grid=(half, M // tm), in_specs=[pl.BlockSpec((tm, K), lambda j, i: (i, 0))] + specs,
        out_specs=(o_spec, o_spec, o_spec), out_shape=(f32, f32, jax.ShapeDtypeStruct((M, F), BF16)),
        semantics=("parallel", "parallel"), name=name, args=(h, w, w), side=side)


def _ffn_out_bwd_x(dy, w, gate, up, *, tm, kb, name, side=None):
    M, D = dy.shape
    F = w.shape[0]

    def body(dy_ref, w_ref, g_ref, u_ref, dg_ref, du_ref):
        da = lax.dot_general(dy_ref[...], w_ref[...], _NT, preferred_element_type=F32)
        gate = g_ref[...]
        s, ns = _sigmoid_pair(gate)
        dg_ref[...] = (da * u_ref[...] * (s * (1.0 + gate * ns))).astype(BF16)
        du_ref[...] = (da * (gate * s)).astype(BF16)

    tile = pl.BlockSpec((tm, kb), lambda i, k: (i, k))
    act = jax.ShapeDtypeStruct((M, F), BF16)
    return _pcall(
        body, grid=(M // tm, F // kb),
        in_specs=[pl.BlockSpec((tm, D), lambda i, k: (i, 0)), pl.BlockSpec((kb, D), lambda i, k: (k, 0)), tile, tile],
        out_specs=(tile, tile), out_shape=(act, act),
        semantics=("parallel", "parallel"), name=name, args=(dy, w, gate, up), side=side)


def _modnorm_fwd(x, gain, sc, sh, *, tm, name):
    S, D = x.shape

    def body(x_ref, g_ref, sc_ref, sh_ref, h_ref):
        xv = x_ref[...]
        rstd = lax.rsqrt(jnp.mean(xv * xv, axis=-1, keepdims=True) + EPS)
        y = (xv * rstd) * g_ref[...]
        h_ref[...] = (y * (1.0 + sc_ref[...]) + sh_ref[...]).astype(BF16)

    row = pl.BlockSpec((1, D), lambda i: (0, 0))
    return pl.pallas_call(
        body, grid=(S // tm,),
        in_specs=[pl.BlockSpec((tm, D), lambda i: (i, 0)), row, row, row],
        out_specs=pl.BlockSpec((tm, D), lambda i: (i, 0)),
        out_shape=jax.ShapeDtypeStruct((S, D), BF16),
        compiler_params=_params(("parallel",)), name=name)(x, gain, sc, sh)


def _modnorm_bwd(x, dh, dres, gain, sc, branch, gate, *, tm, name):
    S, D = x.shape
    has_prev = branch is not None

    def body(*refs):
        if has_prev:
            (x_ref, dh_ref, dr_ref, g_ref, sc_ref, br_ref, gt_ref,
             dx_ref, dbr_ref, dgt_ref, dsh_ref, dsc_ref, dgn_ref) = refs
        else:
            (x_ref, dh_ref, dr_ref, g_ref, sc_ref,
             dx_ref, dsh_ref, dsc_ref, dgn_ref) = refs
        i = pl.program_id(0)
        xv = x_ref[...]
        dh_v = dh_ref[...]
        gv = g_ref[...]
        scale1 = 1.0 + sc_ref[...]
        rstd = lax.rsqrt(jnp.mean(xv * xv, axis=-1, keepdims=True) + EPS)
        n = xv * rstd
        dn = dh_v * (gv * scale1)
        dx = rstd * (dn - n * jnp.mean(dn * n, axis=-1, keepdims=True)) + dr_ref[...]
        dx_ref[...] = dx
        dhn = dh_v * n
        p_sh = jnp.sum(dh_v, axis=0, keepdims=True)
        p_sc = jnp.sum(dhn, axis=0, keepdims=True) * gv
        p_gn = jnp.sum(dhn, axis=0, keepdims=True) * scale1
        if has_prev:
            dbr_ref[...] = (gt_ref[...] * dx).astype(BF16)
            p_gt = jnp.sum(dx * br_ref[...].astype(F32), axis=0, keepdims=True)

        @pl.when(i == 0)
        def _():
            dsh_ref[...] = p_sh
            dsc_ref[...] = p_sc
            dgn_ref[...] = p_gn
            if has_prev:
                dgt_ref[...] = p_gt

        @pl.when(i > 0)
        def _():
            dsh_ref[...] += p_sh
            dsc_ref[...] += p_sc
            dgn_ref[...] += p_gn
            if has_prev:
                dgt_ref[...] += p_gt

    tile = pl.BlockSpec((tm, D), lambda i: (i, 0))
    row = pl.BlockSpec((1, D), lambda i: (0, 0))
    row_shape = jax.ShapeDtypeStruct((1, D), F32)
    if has_prev:
        in_specs = [tile, tile, tile, row, row, tile, row]
        args = (x, dh, dres, gain, sc, branch, gate)
        out_specs = (tile, tile, row, row, row, row)
        out_shape = (jax.ShapeDtypeStruct((S, D), F32), jax.ShapeDtypeStruct((S, D), BF16),
                     row_shape, row_shape, row_shape, row_shape)
    else:
        in_specs = [tile, tile, tile, row, row]
        args = (x, dh, dres, gain, sc)
        out_specs = (tile, row, row, row)
        out_shape = (jax.ShapeDtypeStruct((S, D), F32), row_shape, row_shape, row_shape)
    return pl.pallas_call(body, grid=(S // tm,), in_specs=in_specs, out_specs=out_specs,
                          out_shape=out_shape, compiler_params=_params(("arbitrary",)),
                          name=name)(*args)


def _loss_bwd(y, target, branch, gate, *, tm, name):
    S, D = y.shape
    nsteps = S // tm

    def body(y_ref, t_ref, br_ref, gt_ref, dy_ref, dbr_ref, dgt_ref, loss_ref, col_ref):
        i = pl.program_id(0)
        diff = y_ref[...] - t_ref[...]
        dy = diff * (1.0 / D)
        dy_ref[...] = dy
        dbr_ref[...] = (gt_ref[...] * dy).astype(BF16)
        p_gt = jnp.sum(dy * br_ref[...].astype(F32), axis=0, keepdims=True)
        p_col = jnp.sum(diff * diff, axis=0, keepdims=True)

        @pl.when(i == 0)
        def _():
            dgt_ref[...] = p_gt
            col_ref[...] = p_col

        @pl.when(i > 0)
        def _():
            dgt_ref[...] += p_gt
            col_ref[...] += p_col

        @pl.when(i == nsteps - 1)
        def _():
            tot = jnp.sum(col_ref[...], axis=-1, keepdims=True) * (0.5 / D)
            loss_ref[...] = jnp.broadcast_to(tot, (1, 128))

    tile = pl.BlockSpec((tm, D), lambda i: (i, 0))
    row = pl.BlockSpec((1, D), lambda i: (0, 0))
    return pl.pallas_call(
        body, grid=(nsteps,), in_specs=[tile, tile, tile, row],
        out_specs=(tile, tile, row, pl.BlockSpec((1, 128), lambda i: (0, 0))),
        out_shape=(jax.ShapeDtypeStruct((S, D), F32), jax.ShapeDtypeStruct((S, D), BF16),
                   jax.ShapeDtypeStruct((1, D), F32), jax.ShapeDtypeStruct((1, 128), F32)),
        scratch_shapes=[pltpu.VMEM((1, D), F32)],
        compiler_params=_params(("arbitrary",)), name=name)(y, target, branch, gate)


def _hg_chunk(q, fl, lbv, tri):
    C = q.shape[0]
    sq, nsq = _sigmoid_pair(q)
    qa = q * sq
    sig, nsig = _sigmoid_pair(fl)
    one_lb = 1.0 - lbv
    f = lbv + one_lb * sig
    fc = jnp.maximum(f, TINY)
    lf = jnp.log(fc)
    k = one_lb * nsig
    b = _split_dot_left(tri, lf, 3)
    row = lax.broadcasted_iota(jnp.int32, b.shape, 0)
    bm = jnp.sum(jnp.where(row == C // 2 - 1, b, 0.0), axis=0, keepdims=True)
    bl = jnp.sum(jnp.where(row == C - 1, b, 0.0), axis=0, keepdims=True)
    eb = jnp.exp(b)
    ebm = jnp.exp(b - bm)
    enbm = jnp.exp(bm - b)
    ebl = jnp.exp(bl - b)
    ebL = jnp.exp(bl)

    def operand(t):
        return t.astype(BF16).astype(F32)

    return dict(sq=sq, nsq=nsq, qa=qa, sig=sig, nsig=nsig, one_lb=one_lb, f=f, fc=fc, k=k,
                eb=eb, ebm=ebm, enbm=enbm, ebl=ebl, ebL=ebL,
                Qm=operand(qa * ebm), Km=operand(k * enbm), Qb=operand(qa * eb), Kh=operand(k * ebl), row=row)


def _causal_incl(C):
    r = lax.broadcasted_iota(jnp.int32, (C, C), 0)
    c = lax.broadcasted_iota(jnp.int32, (C, C), 1)
    return r >= c


def _hg_fwd(proj, lb, out_g, *, n_heads, cg, name, side=None):
    S = proj.shape[0]
    H = n_heads
    W = H * HEAD
    T = cg * CHUNK
    NG = S // T
    tri = jnp.tril(jnp.ones((CHUNK, CHUNK), F32)).astype(BF16)

    def body(q_ref, f_ref, v_ref, g_ref, lb_ref, og_ref, tri_ref, o_ref, on_ref, st_ref, s_scr):
        @pl.when(pl.program_id(1) == 0)
        def _():
            s_scr[...] = jnp.zeros_like(s_scr)

        lbv = lb_ref[...]
        ogv = og_ref[...]
        triv = tri_ref[...]
        mask = _causal_incl(CHUNK)
        for c in range(cg):
            rows = pl.ds(c * CHUNK, CHUNK)
            v = v_ref[rows, :]
            gg = g_ref[rows, :]
            cm = _hg_chunk(q_ref[rows, :], f_ref[rows, :], lbv, triv)
            s0 = s_scr[...]
            st_ref[c] = s0
            A = jnp.where(mask, _bdot(cm["Qm"], cm["Km"], _NT), 0.0)
            o = _bdot(A, v) + _bdot(cm["Qb"], s0, _NT)
            s_scr[...] = s0 * cm["ebL"] + _bdot(v, cm["Kh"], _TN)
            o_ref[rows, :] = o
            rstd = lax.rsqrt(jnp.mean(o * o, axis=-1, keepdims=True) + EPS)
            sg, _ = _sigmoid_pair(gg)
            on_ref[rows, :] = (((o * rstd) * ogv) * (gg * sg)).astype(BF16)

    def col(group):
        return pl.BlockSpec((T, HEAD), lambda h, g: (g, group * H + h))

    vec = pl.BlockSpec((1, HEAD), lambda h, g: (0, h))
    return _pcall(
        body, grid=(H, NG),
        in_specs=[col(0), col(1), col(2), col(3), vec, vec,
                  pl.BlockSpec((CHUNK, CHUNK), lambda h, g: (0, 0))],
        out_specs=(pl.BlockSpec((T, HEAD), lambda h, g: (g, h)),
                   pl.BlockSpec((T, HEAD), lambda h, g: (g, h)),
                   pl.BlockSpec((cg, None, HEAD, HEAD), lambda h, g: (g, h, 0, 0))),
        out_shape=(jax.ShapeDtypeStruct((S, W), F32), jax.ShapeDtypeStruct((S, W), BF16),
                   jax.ShapeDtypeStruct((S // CHUNK, H, HEAD, HEAD), F32)),
        scratch_shapes=[pltpu.VMEM((HEAD, HEAD), F32)],
        semantics=("parallel", "arbitrary"), name=name,
        args=(proj, proj, proj, proj, lb, out_g, tri), side=side)


def _hg_bwd(proj, o_pre, d_on, d_on_col0, states, lb, out_g, *, n_heads, cg, name, side=None):
    S = proj.shape[0]
    H = n_heads
    W = H * HEAD
    T = cg * CHUNK
    NG = S // T
    tri = jnp.tril(jnp.ones((CHUNK, CHUNK), F32)).astype(BF16)
    triu = jnp.triu(jnp.ones((CHUNK, CHUNK), F32)).astype(BF16)

    def body(q_ref, f_ref, v_ref, g_ref, o_ref, dy_ref, st_ref, lb_ref, og_ref, tri_ref, triu_ref,
             dq_ref, df_ref, di_ref, dg_ref, dlb_ref, dog_ref, ds_scr):
        gstep = pl.program_id(1)

        @pl.when(gstep == 0)
        def _():
            ds_scr[...] = jnp.zeros_like(ds_scr)
            dlb_ref[...] = jnp.zeros_like(dlb_ref)
            dog_ref[...] = jnp.zeros_like(dog_ref)

        lbv = lb_ref[...]
        ogv = og_ref[...]
        triv = tri_ref[...]
        triuv = triu_ref[...]
        mask = _causal_incl(CHUNK)
        dlb_acc = jnp.zeros((1, HEAD), F32)
        dog_acc = jnp.zeros((1, HEAD), F32)
        for c in reversed(range(cg)):
            rows = pl.ds(c * CHUNK, CHUNK)
            q = q_ref[rows, :]
            v = v_ref[rows, :]
            gg = g_ref[rows, :]
            o = o_ref[rows, :]
            dy = dy_ref[rows, :]
            cm = _hg_chunk(q, f_ref[rows, :], lbv, triv)
            s0 = st_ref[c]
            ds1 = ds_scr[...]
            rstd = lax.rsqrt(jnp.mean(o * o, axis=-1, keepdims=True) + EPS)
            n = o * rstd
            sg, nsg = _sigmoid_pair(gg)
            silu_g = gg * sg
            dyn = dy * n
            dog_acc = dog_acc + jnp.sum(dyn * silu_g, axis=0, keepdims=True)
            dg_ref[rows, :] = (dyn * ogv * (sg * (1.0 + gg * nsg))).astype(BF16)
            dn = dy * (ogv * silu_g)
            d_o = rstd * (dn - n * jnp.mean(dn * n, axis=-1, keepdims=True))
            A = jnp.where(mask, _bdot(cm["Qm"], cm["Km"], _NT), 0.0)
            dA = jnp.where(mask, _bdot(d_o, v, _NT), 0.0)
            dV = _bdot(A, d_o, _TN) + _bdot(cm["Kh"], ds1, _NT)
            dQm = _bdot(dA, cm["Km"])
            dKm = _bdot(dA, cm["Qm"], _TN)
            dQb = _bdot(d_o, s0)
            dKh = _bdot(v, ds1)
            ds_scr[...] = ds1 * cm["ebL"] + _bdot(d_o, cm["Qb"], _TN)
            kh_term = dKh * cm["Kh"]
            db = dQm * cm["Qm"] - dKm * cm["Km"] + dQb * cm["Qb"] - kh_term
            dbl = (jnp.sum(kh_term, axis=0, keepdims=True)
                   + cm["ebL"] * jnp.sum(ds1 * s0, axis=0, keepdims=True))
            db = db + jnp.where(cm["row"] == CHUNK - 1, dbl, 0.0)
            dlf = _split_dot_left(triuv, db, 3)
            dqa = dQm * cm["ebm"] + dQb * cm["eb"]
            dq_ref[rows, :] = (dqa * (cm["sq"] * (1.0 + q * cm["nsq"]))).astype(BF16)
            dk = dKm * cm["enbm"] + dKh * cm["ebl"]
            dfc = jnp.where(cm["f"] > TINY, dlf / cm["fc"], 0.0)
            t = dfc - dk
            df_ref[rows, :] = (t * (cm["one_lb"] * cm["sig"] * cm["nsig"])).astype(BF16)
            dlb_acc = dlb_acc + jnp.sum(t * cm["nsig"], axis=0, keepdims=True)
            di_ref[rows, :] = dV.astype(BF16)
        dlb_ref[...] += dlb_acc
        dog_ref[...] += dog_acc

    def col(group):
        return pl.BlockSpec((T, HEAD), lambda h, g: (NG - 1 - g, group * H + h))

    own = pl.BlockSpec((T, HEAD), lambda h, g: (NG - 1 - g, h))
    vec = pl.BlockSpec((1, HEAD), lambda h, g: (0, h))
    cst = pl.BlockSpec((CHUNK, CHUNK), lambda h, g: (0, 0))
    act = jax.ShapeDtypeStruct((S, W), BF16)
    vec_shape = jax.ShapeDtypeStruct((1, W), F32)
    return _pcall(
        body, grid=(H, NG),
        in_specs=[col(0), col(1), col(2), col(3), own,
                  pl.BlockSpec((T, HEAD), lambda h, g: (NG - 1 - g, d_on_col0 + h)),
                  pl.BlockSpec((cg, None, HEAD, HEAD), lambda h, g: (NG - 1 - g, h, 0, 0)),
                  vec, vec, cst, cst],
        out_specs=(own, own, own, own, vec, vec),
        out_shape=(act, act, act, act, vec_shape, vec_shape),
        scratch_shapes=[pltpu.VMEM((HEAD, HEAD), F32)],
        semantics=("parallel", "arbitrary"), name=name,
        args=(proj, proj, proj, proj, o_pre, d_on, states, lb, out_g, tri, triu), side=side)


def _sb_pre(proj, q_g, k_g, *, n_heads, col0, tm, name):
    S = proj.shape[0]
    H = n_heads
    W = H * HEAD

    def body(q_ref, k_ref, v_ref, qg_ref, kg_ref, qh_ref, kh_ref, vh_ref):
        for src, g_ref, dst in ((q_ref, qg_ref, qh_ref), (k_ref, kg_ref, kh_ref)):
            xv = src[...]
            rstd = lax.rsqrt(jnp.mean(xv * xv, axis=-1, keepdims=True) + EPS)
            dst[...] = ((xv * rstd) * g_ref[...]).astype(BF16)
        vh_ref[...] = v_ref[...].astype(BF16)

    def col(group):
        return pl.BlockSpec((tm, HEAD), lambda i, h: (i, col0 + group * H + h))

    vec = pl.BlockSpec((1, HEAD), lambda i, h: (0, 0))
    own = pl.BlockSpec((tm, HEAD), lambda i, h: (i, h))
    act = jax.ShapeDtypeStruct((S, W), BF16)
    return pl.pallas_call(
        body, grid=(S // tm, H), in_specs=[col(0), col(1), col(2), vec, vec],
        out_specs=(own, own, own), out_shape=(act, act, act),
        compiler_params=_params(("parallel", "parallel")), name=name)(proj, proj, proj, q_g, k_g)


def _sb_scores(q, k_blk, scale):
    z = lax.dot_general(q, k_blk, _NT, preferred_element_type=F32) * scale
    e = jnp.exp(-jnp.abs(z))
    sp = jnp.maximum(z, 0.0) + jnp.log(1.0 + e)
    return z, e, sp


def _heads_per_step(n_heads):
    return 2 if n_heads % 2 == 0 else 1


def _max_all(values):
    m = jnp.max(values[0])
    for v in values[1:]:
        m = jnp.maximum(m, jnp.max(v))
    return m


def _strict_lower_mask(t):
    r = lax.broadcasted_iota(jnp.int32, (t, t), 0)
    c = lax.broadcasted_iota(jnp.int32, (t, t), 1)
    return c < r


def _sb_fwd(qh, kh, vh, out_g, *, n_heads, tq, name, side=None):
    S, W = qh.shape
    HP = _heads_per_step(n_heads)
    WP = HP * HEAD
    NQ = S // tq
    scale = HEAD ** -0.5
    u_strict = jnp.tril(jnp.ones((tq, tq), F32), -1).astype(BF16)

    def body(q_ref, k_ref, v_ref, og_ref, u_ref, o_ref, on_ref):
        qi = pl.program_id(1)
        u = u_ref[...]
        heads = [slice(hh * HEAD, (hh + 1) * HEAD) for hh in range(HP)]
        qs = [q_ref[:, cols] for cols in heads]

        def block(kb, r_carry, diag, valid=None):
            rows = pl.ds(pl.multiple_of(kb * tq, tq), tq)
            pvs, rs = [], []
            for hh, cols in enumerate(heads):
                k_blk = k_ref[rows, cols]
                v_blk = v_ref[rows, cols]
                z, _, sp = _sb_scores(qs[hh], k_blk, scale)
                if diag:
                    m = _strict_lower_mask(tq)
                    L = jnp.where(m, -sp, 0.0)
                else:
                    L = -sp
                C = _split_dot(L, u, 2)
                a = jnp.exp(z - sp + C + r_carry[hh])
                if diag:
                    a = jnp.where(m, a, 0.0)
                if valid is not None:
                    a = jnp.where(valid, a, 0.0)
                pvs.append(lax.dot_general(a.astype(BF16), v_blk, (((1,), (0,)), ((), ())),
                                           preferred_element_type=F32))
                rs.append(r_carry[hh] + (C[:, 0:1] + L[:, 0:1]))
            return tuple(pvs), tuple(rs)

        acc_d, r_d = block(qi, (jnp.zeros((tq, 1), F32),) * HP, True)
        acc_p, r0 = block(jnp.maximum(qi - 1, 0), r_d, False, valid=qi > 0)
        acc0 = tuple(a + b for a, b in zip(acc_d, acc_p))

        def cond(st):
            kb, _, _, rmax = st
            return jnp.logical_and(kb >= 0, rmax > SKIP_LOG)

        def step(st):
            kb, acc, r, _ = st
            pv, r2 = block(kb, r, False)
            return kb - 1, tuple(a + b for a, b in zip(acc, pv)), r2, _max_all(r2)

        _, accs, _, _ = lax.while_loop(cond, step, (qi - 2, acc0, r0, _max_all(r0)))
        for hh, cols in enumerate(heads):
            acc = accs[hh]
            o_ref[:, cols] = acc
            rstd = lax.rsqrt(jnp.mean(acc * acc, axis=-1, keepdims=True) + EPS)
            on_ref[:, cols] = ((acc * rstd) * og_ref[:, cols]).astype(BF16)

    blk = pl.BlockSpec((tq, WP), lambda h, i: (i, h))
    full = pl.BlockSpec((S, WP), lambda h, i: (0, h))
    return _pcall(
        body, grid=(n_heads // HP, NQ),
        in_specs=[blk, full, full, pl.BlockSpec((1, WP), lambda h, i: (0, h)),
                  pl.BlockSpec((tq, tq), lambda h, i: (0, 0))],
        out_specs=(blk, blk),
        out_shape=(jax.ShapeDtypeStruct((S, W), F32), jax.ShapeDtypeStruct((S, W), BF16)),
        semantics=("parallel", "arbitrary"), name=name, args=(qh, kh, vh, out_g, u_strict), side=side)


def _sb_bwd(qh, kh, vh, o_pre, d_on, d_on_col0, out_g, *, n_heads, tq, name, side=None):
    S, W = qh.shape
    HP = _heads_per_step(n_heads)
    WP = HP * HEAD
    assert d_on_col0 % HP == 0
    NQ = S // tq
    scale = HEAD ** -0.5
    u_strict = jnp.tril(jnp.ones((tq, tq), F32), -1).astype(BF16)
    u_incl = jnp.tril(jnp.ones((tq, tq), F32)).astype(BF16)

    def body(q_ref, k_ref, v_ref, o_ref, dy_ref, og_ref, us_ref, ui_ref,
             dq_ref, dk_ref, dv_ref, dog_ref):
        qi = pl.program_id(1)

        @pl.when(qi == 0)
        def _():
            dk_ref[...] = jnp.zeros_like(dk_ref)
            dv_ref[...] = jnp.zeros_like(dv_ref)
            dog_ref[...] = jnp.zeros_like(dog_ref)

        us = us_ref[...]
        ui = ui_ref[...]
        heads = [slice(hh * HEAD, (hh + 1) * HEAD) for hh in range(HP)]
        qs, d_obs, deltas = [], [], []
        for cols in heads:
            qs.append(q_ref[:, cols])
            o = o_ref[:, cols]
            dy = dy_ref[:, cols]
            rstd = lax.rsqrt(jnp.mean(o * o, axis=-1, keepdims=True) + EPS)
            n = o * rstd
            dog_ref[:, cols] += jnp.sum(dy * n, axis=0, keepdims=True)
            dn = dy * og_ref[:, cols]
            d_o = rstd * (dn - n * jnp.mean(dn * n, axis=-1, keepdims=True))
            d_ob = d_o.astype(BF16)
            d_obs.append(d_ob)
            deltas.append(jnp.sum(d_ob.astype(F32) * o, axis=-1, keepdims=True))
        q_ts = [q.T for q in qs]
        d_ob_ts = [d.T for d in d_obs]

        def block(kb, r_carry, g_carry, diag, valid=None):
            rows = pl.ds(pl.multiple_of(kb * tq, tq), tq)
            dqs, rs, gs = [], [], []
            for hh, cols in enumerate(heads):
                k_blk = k_ref[rows, cols]
                v_blk = v_ref[rows, cols]
                z, e, sp = _sb_scores(qs[hh], k_blk, scale)
                if diag:
                    m = _strict_lower_mask(tq)
                    L = jnp.where(m, -sp, 0.0)
                else:
                    L = -sp
                C = _split_dot(L, us, 2)
                a = jnp.exp(z - sp + C + r_carry[hh])
                if diag:
                    a = jnp.where(m, a, 0.0)
                if valid is not None:
                    a = jnp.where(valid, a, 0.0)
                ab = a.astype(BF16)
                dA = lax.dot_general(d_obs[hh], v_blk, _NT, preferred_element_type=F32)
                G = ab.astype(F32) * dA
                SI = _split_dot(G, ui, 2)
                P = deltas[hh] - (g_carry[hh] + SI)
                r = 1.0 / (1.0 + e)
                sig = jnp.where(z >= 0, r, e * r)
                dz = G - (G + P) * sig
                if diag:
                    dz = jnp.where(m, dz, 0.0)
                if valid is not None:
                    dz = jnp.where(valid, dz, 0.0)
                dzb = (dz * scale).astype(BF16)
                dqs.append(lax.dot_general(dzb, k_blk, (((1,), (0,)), ((), ())), preferred_element_type=F32))
                dk_ref[cols, rows] += jnp.dot(q_ts[hh], dzb, preferred_element_type=F32)
                dv_ref[cols, rows] += jnp.dot(d_ob_ts[hh], ab, preferred_element_type=F32)
                rs.append(r_carry[hh] + (C[:, 0:1] + L[:, 0:1]))
                gs.append(g_carry[hh] + SI[:, 0:1])
            return tuple(dqs), tuple(rs), tuple(gs)

        zero = (jnp.zeros((tq, 1), F32),) * HP
        dq_d, r_d, g_d = block(qi, zero, zero, True)
        dq_p, r0, g0 = block(jnp.maximum(qi - 1, 0), r_d, g_d, False, valid=qi > 0)
        dq0 = tuple(a + b for a, b in zip(dq_d, dq_p))

        def cond(st):
            kb, _, _, _, rmax = st
            return jnp.logical_and(kb >= 0, rmax > SKIP_LOG)

        def step(st):
            kb, dq, r, g, _ = st
            dq_part, r2, g2 = block(kb, r, g, False)
            return kb - 1, tuple(a + b for a, b in zip(dq, dq_part)), r2, g2, _max_all(r2)

        _, dqs, _, _, _ = lax.while_loop(cond, step, (qi - 2, dq0, r0, g0, _max_all(r0)))
        for hh, cols in enumerate(heads):
            dq_ref[:, cols] = dqs[hh]

    blk = pl.BlockSpec((tq, WP), lambda h, i: (i, h))
    full = pl.BlockSpec((S, WP), lambda h, i: (0, h))
    vec = pl.BlockSpec((1, WP), lambda h, i: (0, h))
    cst = pl.BlockSpec((tq, tq), lambda h, i: (0, 0))
    act = jax.ShapeDtypeStruct((S, W), F32)
    act_t = jax.ShapeDtypeStruct((W, S), F32)
    full_t = pl.BlockSpec((WP, S), lambda h, i: (h, 0))
    return _pcall(
        body, grid=(n_heads // HP, NQ),
        in_specs=[blk, full, full, blk,
                  pl.BlockSpec((tq, WP), lambda h, i: (i, d_on_col0 // HP + h)), vec, cst, cst],
        out_specs=(blk, full_t, full_t, vec),
        out_shape=(act, act_t, act_t, jax.ShapeDtypeStruct((1, W), F32)),
        semantics=("parallel", "arbitrary"), name=name,
        args=(qh, kh, vh, o_pre, d_on, out_g, u_strict, u_incl), side=side)


def _sb_pre_bwd(proj, dqh, dkh, dvh, q_g, k_g, *, n_heads, col0, tm, name):
    S = proj.shape[0]
    H = n_heads
    W = H * HEAD

    def body(q_ref, k_ref, dqh_ref, dkh_ref, dvh_ref, qg_ref, kg_ref,
             dq_ref, dk_ref, dv_ref, dqg_ref, dkg_ref):
        first = jnp.logical_and(pl.program_id(0) == 0, pl.program_id(1) == 0)

        @pl.when(first)
        def _():
            dqg_ref[...] = jnp.zeros_like(dqg_ref)
            dkg_ref[...] = jnp.zeros_like(dkg_ref)

        for src, dh, g_ref, dst, dg_ref in ((q_ref, dqh_ref[...], qg_ref, dq_ref, dqg_ref),
                                            (k_ref, dkh_ref[...].T, kg_ref, dk_ref, dkg_ref)):
            xv = src[...]
            rstd = lax.rsqrt(jnp.mean(xv * xv, axis=-1, keepdims=True) + EPS)
            n = xv * rstd
            dg_ref[...] += jnp.sum(dh * n, axis=0, keepdims=True)
            dn = dh * g_ref[...]
            dst[...] = (rstd * (dn - n * jnp.mean(dn * n, axis=-1, keepdims=True))).astype(BF16)
        dv_ref[...] = dvh_ref[...].T.astype(BF16)

    def col(group):
        return pl.BlockSpec((tm, HEAD), lambda i, h: (i, col0 + group * H + h))

    vec = pl.BlockSpec((1, HEAD), lambda i, h: (0, 0))
    own = pl.BlockSpec((tm, HEAD), lambda i, h: (i, h))
    own_t = pl.BlockSpec((HEAD, tm), lambda i, h: (h, i))
    act = jax.ShapeDtypeStruct((S, W), BF16)
    vec_shape = jax.ShapeDtypeStruct((1, HEAD), F32)
    return pl.pallas_call(
        body, grid=(S // tm, H), in_specs=[col(0), col(1), own, own_t, own_t, vec, vec],
        out_specs=(own, own, own, vec, vec), out_shape=(act, act, act, vec_shape, vec_shape),
        compiler_params=_params(("arbitrary", "arbitrary")), name=name,
    )(proj, proj, dqh, dkh, dvh, q_g, k_g)


def _softmax_rows(x_ref, L):
    rows = [x_ref[l:l + 1, :] for l in range(L)]
    mx = rows[0]
    for r in rows[1:]:
        mx = jnp.maximum(mx, r)
    ex = [jnp.exp(r - mx) for r in rows]
    tot = ex[0]
    for e in ex[1:]:
        tot = tot + e
    return [e / tot for e in ex]


def _lb_fwd(logits, *, name):
    L, W = logits.shape

    def body(x_ref, o_ref):
        s = _softmax_rows(x_ref, L)
        run = jnp.zeros((1, W), F32)
        for l in range(L):
            run = run + s[l]
            o_ref[l:l + 1, :] = run - s[0]

    return pl.pallas_call(body, out_shape=jax.ShapeDtypeStruct((L, W), F32), name=name)(logits)


def _lb_bwd(logits, dlb_parts, *, name):
    L, W = logits.shape
    P = dlb_parts.shape[0]

    def body(x_ref, d_ref, o_ref):
        s = _softmax_rows(x_ref, L)
        dlb = []
        for l in range(L):
            t = d_ref[0, l:l + 1, :]
            for q in range(1, P):
                t = t + d_ref[q, l:l + 1, :]
            dlb.append(t)
        ds = [None] * L
        run = jnp.zeros((1, W), F32)
        for j in reversed(range(L)):
            run = run + dlb[j]
            ds[j] = run
        ds[0] = jnp.zeros((1, W), F32)
        inner = jnp.zeros((1, W), F32)
        for j in range(L):
            inner = inner + s[j] * ds[j]
        for j in range(L):
            o_ref[j:j + 1, :] = s[j] * (ds[j] - inner)

    return pl.pallas_call(body, out_shape=jax.ShapeDtypeStruct((L, W), F32), name=name)(logits, dlb_parts)


def _ada_mod(c_all, w_ada, *, nb, name):
    L, D, n = w_ada.shape
    B = c_all.shape[0]

    def body(c_ref, w_ref, o_ref, cond_ref):
        cv = c_ref[...]
        s, _ = _sigmoid_pair(cv)
        cond = cv * s
        cond_ref[...] = cond
        o_ref[...] = _bdot(cond, w_ref[...])

    return pl.pallas_call(
        body, grid=(L, n // nb),
        in_specs=[pl.BlockSpec((B, D), lambda l, j: (0, 0)),
                  pl.BlockSpec((None, D, nb), lambda l, j: (l, 0, j))],
        out_specs=(pl.BlockSpec((None, B, nb), lambda l, j: (l, 0, j)),
                   pl.BlockSpec((B, D), lambda l, j: (0, 0))),
        out_shape=(jax.ShapeDtypeStruct((L, B, n), F32), jax.ShapeDtypeStruct((B, D), F32)),
        compiler_params=_params(("arbitrary", "arbitrary")), name=name)(c_all, w_ada)


def _adam_math(w, g, m, v):
    m2 = ADAM_B1 * m + (1.0 - ADAM_B1) * g
    v2 = ADAM_B2 * v + (1.0 - ADAM_B2) * (g * g)
    m_hat = m2 / (1.0 - ADAM_B1 ** ADAM_STEP)
    v_hat = v2 / (1.0 - ADAM_B2 ** ADAM_STEP)
    delta = -ADAM_LR * (m_hat / (jnp.sqrt(v_hat) + ADAM_EPS) + ADAM_WD * w)
    return delta, m2, v2


def _adamw(w, m, v, gparts, *, tr, name):
    R, C = w.shape
    P = gparts.shape[0]

    def body(w_ref, m_ref, v_ref, gp_ref, g_ref, d_ref, m2_ref, v2_ref):
        g = gp_ref[0].astype(F32)
        for p in range(1, P):
            g = g + gp_ref[p].astype(F32)
        delta, m2, v2 = _adam_math(w_ref[...], g, m_ref[...], v_ref[...])
        g_ref[...] = g
        d_ref[...] = delta
        m2_ref[...] = m2
        v2_ref[...] = v2

    tile = pl.BlockSpec((tr, C), lambda i: (i, 0))
    shp = jax.ShapeDtypeStruct((R, C), F32)
    return pl.pallas_call(
        body, grid=(R // tr,),
        in_specs=[tile, tile, tile, pl.BlockSpec((P, tr, C), lambda i: (0, i, 0))],
        out_specs=(tile, tile, tile, tile), out_shape=(shp, shp, shp, shp),
        compiler_params=_params(("parallel",)), name=name)(w, m, v, gparts)


def _adamw_layers(w, m, v, gparts, *, tr, name, side=None):
    L, R, C = w.shape
    P = gparts[0].shape[0]
    nblk = R // tr

    def body(*refs):
        w_ref, m_ref, v_ref = refs[:3]
        gp_refs = refs[3:3 + L]
        g_ref, d_ref, m2_ref, v2_ref = refs[3 + L:]
        layer = pl.program_id(0)
        for t in range(L):
            @pl.when(layer == t)
            def _(t=t):
                g = gp_refs[t][0].astype(F32)
                for q in range(1, P):
                    g = g + gp_refs[t][q].astype(F32)
                delta, m2, v2 = _adam_math(w_ref[...], g, m_ref[...], v_ref[...])
                g_ref[...] = g
                d_ref[...] = delta
                m2_ref[...] = m2
                v2_ref[...] = v2

    def gp_spec(t):
        def index(l, i):
            return (0, jnp.where(l == t, i, jnp.where(l < t, 0, nblk - 1)), 0)
        return pl.BlockSpec((P, tr, C), index)

    tile = pl.BlockSpec((None, tr, C), lambda l, i: (l, i, 0))
    shp = jax.ShapeDtypeStruct((L, R, C), F32)
    return _pcall(
        body, grid=(L, nblk), in_specs=[tile, tile, tile] + [gp_spec(t) for t in range(L)],
        out_specs=(tile, tile, tile, tile), out_shape=(shp, shp, shp, shp),
        semantics=("arbitrary", "arbitrary"), name=name, args=(w, m, v, *gparts), side=side)


def _adamw_ada(w, m, v, cond_t, dmod, *, tr, name):
    L, D, n = w.shape
    Bp = cond_t.shape[1]

    def body(w_ref, m_ref, v_ref, c_ref, dm_ref, g_ref, d_ref, m2_ref, v2_ref):
        g = _bdot(c_ref[...], dm_ref[...])
        delta, m2, v2 = _adam_math(w_ref[...], g, m_ref[...], v_ref[...])
        g_ref[...] = g
        d_ref[...] = delta
        m2_ref[...] = m2
        v2_ref[...] = v2

    tile = pl.BlockSpec((None, tr, n), lambda l, i: (l, i, 0))
    shp = jax.ShapeDtypeStruct((L, D, n), F32)
    return pl.pallas_call(
        body, grid=(L, D // tr),
        in_specs=[tile, tile, tile, pl.BlockSpec((tr, Bp), lambda l, i: (i, 0)),
                  pl.BlockSpec((None, Bp, n), lambda l, i: (l, 0, 0))],
        out_specs=(tile, tile, tile, tile), out_shape=(shp, shp, shp, shp),
        compiler_params=_params(("parallel", "parallel")), name=name)(w, m, v, cond_t, dmod)


def _allgather_small(block, *, name):
    R, C = block.shape

    def body(x_ref, out_ref, send_sems, recv_sems, local_sem):
        x, y, c = lax.axis_index("x"), lax.axis_index("y"), lax.axis_index("c")

        def rows(px, py, pc):
            return out_ref.at[pl.ds((4 * px + 2 * py + pc) * R, R), :]

        mine = pltpu.make_async_copy(x_ref, rows(x, y, c), local_sem)
        mine.start()
        sends = []
        for rel in range(1, N_DEV):
            to = _peer(x, y, c, rel)
            cp = pltpu.make_async_remote_copy(src_ref=x_ref, dst_ref=rows(x, y, c),
                                              send_sem=send_sems.at[rel - 1], recv_sem=recv_sems.at[rel - 1],
                                              device_id=to, device_id_type=MESH)
            cp.start()
            sends.append(cp)
        for rel in range(1, N_DEV):
            frm = _peer(x, y, c, rel)
            pltpu.make_async_remote_copy(src_ref=x_ref, dst_ref=rows(*frm),
                                         send_sem=send_sems.at[rel - 1], recv_sem=recv_sems.at[rel - 1],
                                         device_id=frm, device_id_type=MESH).wait_recv()
        for cp in sends:
            cp.wait_send()
        mine.wait()

    return pl.pallas_call(
        body, out_shape=jax.ShapeDtypeStruct((N_DEV * R, C), block.dtype),
        in_specs=[pl.BlockSpec(memory_space=pltpu.VMEM)],
        out_specs=pl.BlockSpec(memory_space=pltpu.VMEM),
        scratch_shapes=[pltpu.SemaphoreType.DMA((N_DEV - 1,)), pltpu.SemaphoreType.DMA((N_DEV - 1,)),
                        pltpu.SemaphoreType.DMA],
        compiler_params=pltpu.CompilerParams(vmem_limit_bytes=V7X_VMEM_LIMIT), name=name)(block)


def _allgather_hbm(shards, *, name):
    n = len(shards)

    def body(*refs):
        ins = refs[:n]
        outs = refs[n:2 * n]
        send_sems, recv_sems, local_sems = refs[2 * n:]
        x, y, c = lax.axis_index("x"), lax.axis_index("y"), lax.axis_index("c")
        sibling = (x, y, 1 - c)
        chips = [(1 - x, y), (x, 1 - y), (1 - x, 1 - y)]

        def slot(t, px, py, pc):
            return outs[t].at[4 * px + 2 * py + pc]

        def copy(t, k, block, to, src=None):
            return pltpu.make_async_remote_copy(
                src_ref=slot(t, *block) if src is None else src, dst_ref=slot(t, *block),
                send_sem=send_sems.at[t * 7 + k], recv_sem=recv_sems.at[t * 7 + k],
                device_id=to, device_id_type=MESH)

        me = (x, y, c)
        started = []
        mine = []
        for t in range(n):
            cp = pltpu.make_async_copy(ins[t], slot(t, *me), local_sems.at[t])
            cp.start()
            mine.append(cp)
            first = [copy(t, 0, me, sibling, src=ins[t])]
            first += [copy(t, 1 + j, me, (*chip, c), src=ins[t]) for j, chip in enumerate(chips)]
            for cp in first:
                cp.start()
            started += first
        for t in range(n):
            for j, chip in enumerate(chips):
                copy(t, 1 + j, (*chip, c), me).wait_recv()
                fwd = copy(t, 4 + j, (*chip, c), sibling)
                fwd.start()
                started.append(fwd)
        for t in range(n):
            copy(t, 0, sibling, me).wait_recv()
            for j, chip in enumerate(chips):
                copy(t, 4 + j, (*chip, 1 - c), me).wait_recv()
        for cp in started:
            cp.wait_send()
        for cp in mine:
            cp.wait()

    any_spec = pl.BlockSpec(memory_space=pl.ANY)
    return pl.pallas_call(
        body, out_shape=[jax.ShapeDtypeStruct((N_DEV,) + s.shape, s.dtype) for s in shards],
        in_specs=[any_spec] * n, out_specs=[any_spec] * n,
        scratch_shapes=[pltpu.SemaphoreType.DMA((7 * n,)), pltpu.SemaphoreType.DMA((7 * n,)),
                        pltpu.SemaphoreType.DMA((n,))],
        name=name)(*shards)


def _tile(total, want):
    step = 128 if total % 128 == 0 else 8
    best = step
    t = step
    while t <= min(total, want):
        if total % t == 0:
            best = t
        t += step
    return best


def _local_step(x, target, mods, lbs, p, wg, shards=None):
    S, D = x.shape
    L = mods.shape[0]
    W = D // 2
    H = W // HEAD
    mesh = shards is not None
    F = shards["w_ffn_out"][0].shape[0] * N_DEV if mesh else wg["w_ffn_out"][0].shape[0]
    tm = _tile(S, 512)
    tm_big = _tile(S, 1024)
    tm_tn = _tile(S, 2048)
    tm_rows = _tile(S, 4096)
    tq = _tile(S, 256)
    cg = max(1, min(32, S // CHUNK))
    nb_out = _tile(D, 1024)
    kb_f = _tile(F, 1408)

    def row(a, l):
        return a[l][None, :]

    first_plan = dict(proj=[("w_ffn_in", 0)], hg=[("w_ffn_out", 0), ("w_out", 0)],
                      sb=[("w_in", 1), ("w_out", 1)], ffn=[("w_ffn_in", 1), ("w_ffn_out", 1)])

    def gather_plan(l, call):
        if not mesh:
            return []
        if l == 0:
            plan = first_plan[call]
        else:
            plan = [(dict(proj="w_in", hg="w_out", sb="w_ffn_out", ffn="w_ffn_in")[call], l + 1)]
        return [(k, j) for k, j in plan if j < L]

    def gather_of(plan):
        return (True, [shards[k][j] for k, j in plan]) if plan else None

    def store_gathered(plan, got):
        for (k, j), g in zip(plan, got or []):
            wg[k][j] = g.reshape(-1, D) if k in ("w_out", "w_ffn_out") else g

    def scatter_of(blocks):
        if mesh and blocks is not None:
            return (False, [b.reshape((N_DEV, -1) + b.shape[-1:]) if b.ndim == 2 else b for b in blocks])
        return None

    saved = []
    xcur = x
    for l in range(L):
        mod = mods[l]
        sh1, sc1, g1, sh2, sc2, g2 = [mod[:, i * D:(i + 1) * D] for i in range(N_MOD)]
        h1 = _modnorm_fwd(xcur, row(p["norm1_g"], l), sc1, sh1, tm=tm, name="norm1_fwd")
        plan = gather_plan(l, "proj")
        proj, got = _mm_nn(h1, wg["w_in"][l], tm=tm_big, name="proj_fwd", side=gather_of(plan))
        store_gathered(plan, got)
        lb = lbs[l][None, :]
        plan = gather_plan(l, "hg")
        (o_hg, on_hg, states), got = _hg_fwd(proj, lb, row(p["hg_out_g"], l), n_heads=H, cg=cg, name="hgrn2_fwd",
                                             side=gather_of(plan))
        store_gathered(plan, got)
        qh, kh, vh = _sb_pre(proj, row(p["sb_q_g"], l), row(p["sb_k_g"], l), n_heads=H, col0=4 * H,
                             tm=tm_rows, name="sb_qknorm_fwd")
        plan = gather_plan(l, "sb")
        (o_sb, on_sb), got = _sb_fwd(qh, kh, vh, row(p["sb_out_g"], l), n_heads=H, tq=tq, name="sb_fwd",
                                     side=gather_of(plan))
        store_gathered(plan, got)
        o_cat = jnp.concatenate([on_hg, on_sb], axis=1)
        (x1, mixed, h2), _ = _mm_nn(o_cat, wg["w_out"][l], tm=tm, nb=D, resid=xcur, gate=g1,
                                    norm=(row(p["norm2_g"], l), sc2, sh2), name="out_proj_fwd")
        w_fin = wg["w_ffn_in"][l]
        if w_fin.shape[0] % 2:
            w_fin = jnp.stack([w_fin[0][:, :F], w_fin[0][:, F:]])
        plan = gather_plan(l, "ffn")
        (gate, up, a), got = _ffn_in_fwd(h2, w_fin, tm=tm, nb=F, name="ffn_in_fwd", side=gather_of(plan))
        store_gathered(plan, got)
        (x2, ffn), _ = _mm_nn(a, wg["w_ffn_out"][l], tm=tm, nb=nb_out // 2, resid=x1, gate=g2, name="ffn_out_fwd")
        saved.append(dict(x=xcur, h1=h1, proj=proj, o_hg=o_hg, o_sb=o_sb, states=states, qh=qh, kh=kh, vh=vh,
                          o_cat=o_cat, mixed=mixed, x1=x1, h2=h2, gate=gate, up=up, a=a, ffn=ffn, lb=lb,
                          sc1=sc1, g1=g1, sc2=sc2, g2=g2, w_fin=w_fin))
        xcur = x2

    last = saved[-1]
    dx, dffn, dg2, loss = _loss_bwd(xcur, target, last["ffn"], last["g2"], tm=tm, name="loss_bwd")

    big = {k: [None] * L for k in ("w_in", "w_out", "w_ffn_in", "w_ffn_out")}
    small = {k: [None] * L for k in ("norm1_g", "hg_lb", "hg_out_g", "sb_q_g", "sb_k_g", "sb_out_g", "norm2_g")}
    dmods = [None] * L
    for l in reversed(range(L)):
        sv = saved[l]
        (dgate, dup), _ = _ffn_out_bwd_x(dffn, wg["w_ffn_out"][l], sv["gate"], sv["up"], tm=tm, kb=kb_f,
                                         name="ffn_out_bwd_x")
        g_fout, _ = _mm_tn(sv["a"], dffn, tm=tm_tn, kb=kb_f, nb=nb_out, blocked=False, name="ffn_out_bwd_w")
        dh2, got = _mm_nt(dgate, sv["w_fin"], dy2=dup, tm=tm_big, kb=D, name="ffn_in_bwd_x",
                          side=scatter_of([g_fout]))
        big["w_ffn_out"][l] = g_fout if got is None else got[0]
        g_fin, _ = _mm_tn(sv["h2"], dgate, dy2=dup, tm=tm_tn, kb=_tile(D, 1024), nb=sv["w_fin"].shape[2],
                          blocked=True, name="ffn_in_bwd_w")
        dx1, dmixed, dg1, dsh2, dsc2, dn2 = _modnorm_bwd(
            sv["x1"], dh2, dx, row(p["norm2_g"], l), sv["sc2"], sv["mixed"], sv["g1"], tm=tm,
            name="norm2_bwd")
        small["norm2_g"][l] = dn2
        d_ocat, _ = _mm_nt(dmixed, wg["w_out"][l], tm=tm_big, kb=nb_out, nb=D, name="out_proj_bwd_x")
        g_out, _ = _mm_tn(sv["o_cat"], dmixed, tm=tm_tn, kb=D, nb=nb_out, blocked=False, name="out_proj_bwd_w")
        (dhq, dhf, dhi, dhg, dlb, dhog), got = _hg_bwd(sv["proj"], sv["o_hg"], d_ocat, 0, sv["states"], sv["lb"],
                                                       row(p["hg_out_g"], l), n_heads=H, cg=cg, name="hgrn2_bwd",
                                                       side=scatter_of([g_out]))
        big["w_out"][l] = g_out if got is None else got[0]
        (dqh, dkh, dvh, dsog), got = _sb_bwd(sv["qh"], sv["kh"], sv["vh"], sv["o_sb"], d_ocat, H,
                                             row(p["sb_out_g"], l), n_heads=H, tq=tq, name="sb_bwd",
                                             side=scatter_of([g_fin]))
        big["w_ffn_in"][l] = g_fin if got is None else got[0]
        dsq, dsk, dsv, dqg, dkg = _sb_pre_bwd(sv["proj"], dqh, dkh, dvh, row(p["sb_q_g"], l),
                                              row(p["sb_k_g"], l), n_heads=H, col0=4 * H, tm=tm_rows,
                                              name="sb_qknorm_bwd")
        small["hg_lb"][l] = dlb
        small["hg_out_g"][l] = dhog
        small["sb_out_g"][l] = dsog
        small["sb_q_g"][l] = dqg
        small["sb_k_g"][l] = dkg
        dproj = jnp.concatenate([dhq, dhf, dhi, dhg, dsq, dsk, dsv], axis=1)
        g_in, _ = _mm_tn(sv["h1"], dproj, tm=tm_tn, kb=D, nb=wg["w_in"][l].shape[2], blocked=True,
                         name="proj_bwd_w")
        dh1, got = _mm_nt(dproj, wg["w_in"][l], tm=tm_big, kb=D, name="proj_bwd_x", side=scatter_of([g_in]))
        big["w_in"][l] = g_in if got is None else got[0]
        if l > 0:
            prev = saved[l - 1]
            dx0, dffn_prev, dg2_prev, dsh1, dsc1, dn1 = _modnorm_bwd(
                sv["x"], dh1, dx1, row(p["norm1_g"], l), sv["sc1"], prev["ffn"], prev["g2"], tm=tm,
                name="norm1_bwd")
        else:
            dx0, dsh1, dsc1, dn1 = _modnorm_bwd(sv["x"], dh1, dx1, row(p["norm1_g"], l), sv["sc1"], None, None,
                                                tm=tm, name="norm1_bwd_first")
            dffn_prev, dg2_prev = None, None
        small["norm1_g"][l] = dn1
        dmods[l] = jnp.concatenate([dsh1, dsc1, dg1, dsh2, dsc2, dg2], axis=1)
        dx, dffn, dg2 = dx0, dffn_prev, dg2_prev
    return loss, dx, big, small, dmods


def kernel(x, c, norm1_g, w_in, hg_lb_logits, hg_out_g, sb_q_g, sb_k_g, sb_out_g, w_out, norm2_g, w_ffn_in, w_ffn_out, w_ada, b_ada, loss_target, m_norm1_g, m_w_in, m_hg_lb_logits, m_hg_out_g, m_sb_q_g, m_sb_k_g, m_sb_out_g, m_w_out, m_norm2_g, m_w_ffn_in, m_w_ffn_out, m_w_ada, m_b_ada, v_norm1_g, v_w_in, v_hg_lb_logits, v_hg_out_g, v_sb_q_g, v_sb_k_g, v_sb_out_g, v_w_out, v_norm2_g, v_w_ffn_in, v_w_ffn_out, v_w_ada, v_b_ada):
    L, D = norm1_g.shape
    S = x.shape[1]
    me = 4 * lax.axis_index("x") + 2 * lax.axis_index("y") + lax.axis_index("c")

    c_all = _allgather_small(jnp.broadcast_to(c, (8, D)), name="gather_c").reshape(N_DEV, 8, D)[:, 0, :]
    n_ada = w_ada.shape[2]
    mod_cols, cond = _ada_mod(c_all, w_ada, nb=_tile(n_ada, 512), name="ada_mod")
    mod_all = _allgather_small(mod_cols.reshape(L * N_DEV, n_ada), name="gather_mod")
    mod_all = mod_all.reshape(N_DEV, L, N_DEV, n_ada)
    mod_mine = lax.dynamic_index_in_dim(mod_all, me, axis=2, keepdims=False)
    mods = jnp.transpose(mod_mine, (1, 0, 2)).reshape(L, 1, N_DEV * n_ada) + b_ada[:, None, :]

    lbs = _lb_fwd(hg_lb_logits, name="lower_bounds_fwd")

    shards = dict(w_in=[w_in[l].astype(BF16) for l in range(L)], w_out=[w_out[l].astype(BF16) for l in range(L)],
                  w_ffn_in=[w_ffn_in[l].astype(BF16) for l in range(L)],
                  w_ffn_out=[w_ffn_out[l].astype(BF16) for l in range(L)])
    g_in, = _allgather_hbm([shards["w_in"][0]], name="gather_first_weight")
    wg = dict(w_in=[g_in] + [None] * (L - 1), w_out=[None] * L, w_ffn_in=[None] * L, w_ffn_out=[None] * L)

    p = dict(norm1_g=norm1_g, hg_out_g=hg_out_g, sb_q_g=sb_q_g, sb_k_g=sb_k_g, sb_out_g=sb_out_g,
             norm2_g=norm2_g)
    loss_part, grad_x, recv, small, dmods = _local_step(x.reshape(S, D), loss_target.reshape(S, D), mods, lbs, p,
                                                        wg, shards)

    dmod = jnp.concatenate(dmods, axis=0)
    pieces = [jnp.concatenate(small[k], axis=0) for k in
              ("norm1_g", "hg_lb", "hg_out_g", "sb_q_g", "sb_k_g", "sb_out_g", "norm2_g")] + [dmod]
    flat = jnp.concatenate([a.reshape(-1) for a in pieces] + [loss_part.reshape(-1)])
    n_flat = flat.shape[0]
    rows = -(-n_flat // 1024) * 8
    flat = jnp.pad(flat, (0, rows * 128 - n_flat)).reshape(rows, 128)
    gathered = _allgather_small(flat, name="gather_small_grads").reshape(N_DEV, rows * 128)

    def take(off, shape):
        size = 1
        for s in shape:
            size *= s
        return gathered[:, off:off + size].reshape((N_DEV,) + tuple(shape)), off + size

    off = 0
    parts = {}
    for k, a in zip(("norm1_g", "hg_lb", "hg_out_g", "sb_q_g", "sb_k_g", "sb_out_g", "norm2_g", "dmod"), pieces):
        parts[k], off = take(off, a.shape)
    loss_parts = gathered[:, off:off + 1]
    loss = jnp.sum(loss_parts)

    def pad8(a):
        return jnp.pad(a, ((0, 0), (0, 8 - a.shape[1]), (0, 0)))

    def small_update(w, m, v, gparts):
        Lw = w.shape[0]
        g, d, m2, v2 = _adamw(pad8(w[None])[0], pad8(m[None])[0], pad8(v[None])[0], pad8(gparts),
                              tr=8, name="adamw_small")
        return g[:Lw], d[:Lw], m2[:Lw], v2[:Lw]

    out = {}
    out["norm1_g"] = small_update(norm1_g, m_norm1_g, v_norm1_g, parts["norm1_g"])
    dlogits = _lb_bwd(hg_lb_logits, parts["hg_lb"], name="lower_bounds_bwd")
    out["hg_lb_logits"] = small_update(hg_lb_logits, m_hg_lb_logits, v_hg_lb_logits, dlogits[None])
    out["hg_out_g"] = small_update(hg_out_g, m_hg_out_g, v_hg_out_g, parts["hg_out_g"])
    out["sb_q_g"] = small_update(sb_q_g, m_sb_q_g, v_sb_q_g, parts["sb_q_g"])
    out["sb_k_g"] = small_update(sb_k_g, m_sb_k_g, v_sb_k_g, parts["sb_k_g"])
    out["sb_out_g"] = small_update(sb_out_g, m_sb_out_g, v_sb_out_g, parts["sb_out_g"])
    out["norm2_g"] = small_update(norm2_g, m_norm2_g, v_norm2_g, parts["norm2_g"])
    out["b_ada"] = small_update(b_ada, m_b_ada, v_b_ada, parts["dmod"])

    dmod_all = parts["dmod"].reshape(N_DEV, L, N_DEV, n_ada)
    dmod_mine = lax.dynamic_index_in_dim(dmod_all, me, axis=2, keepdims=False)
    dmod_mine = jnp.pad(jnp.transpose(dmod_mine, (1, 0, 2)), ((0, 0), (0, 128 - N_DEV), (0, 0)))
    cond_t = jnp.pad(jnp.transpose(cond), ((0, 0), (0, 128 - N_DEV)))
    out["w_ada"] = _adamw_ada(w_ada, m_w_ada, v_w_ada, cond_t, dmod_mine, tr=_tile(D, 256), name="adamw_ada")

    def big_update(w, m, v, recv_l, name):
        return _adamw_layers(w, m, v, recv_l, tr=_tile(w.shape[1], 131072 // w.shape[2]), name=name)[0]

    out["w_ffn_in"] = big_update(w_ffn_in, m_w_ffn_in, v_w_ffn_in, recv["w_ffn_in"], "adamw_w_ffn_in")
    out["w_ffn_out"] = big_update(w_ffn_out, m_w_ffn_out, v_w_ffn_out, recv["w_ffn_out"], "adamw_w_ffn_out")
    out["w_out"] = big_update(w_out, m_w_out, v_w_out, recv["w_out"], "adamw_w_out")
    out["w_in"] = big_update(w_in, m_w_in, v_w_in, recv["w_in"], "adamw_w_in")

    order = ("norm1_g", "w_in", "hg_lb_logits", "hg_out_g", "sb_q_g", "sb_k_g", "sb_out_g", "w_out", "norm2_g",
             "w_ffn_in", "w_ffn_out", "w_ada", "b_ada")
    grads = [out[k][0] for k in order]
    deltas = [out[k][1] for k in order]
    new_m = [out[k][2] for k in order]
    new_v = [out[k][3] for k in order]
    return (loss, grad_x.reshape(1, S, D), *grads, *deltas, *new_m, *new_v)
```
